```python
import math
import functools
import jax
import jax.numpy as jnp
from jax import lax
import numpy as np


D_MODEL = 1024
BATCH = 16
SEQ = 2048
DEPTH = 2

CTX_LEN = 256
GRID_W = 64
NORM_EPS = 1e-6

D_HY = D_MODEL // 2
HY_ORDER = 2
HY_SHORT = 3
HY_BANDS = 16
HY_EMB = 1 + 2 * HY_BANDS
HY_FF = 64
HY_DECAY_PCT_LO = 0.3
HY_DECAY_PCT_HI = 1.5
HY_DECAY_TARGET = 1e-2
HY_IN = 3 * D_HY

D_SSM = D_MODEL // 2
SSD_P = 64
SSD_H = D_SSM // SSD_P
SSD_G = 2
SSD_HPG = SSD_H // SSD_G
SSD_N = 128
SSD_CONV = 4
SSD_CHUNK = 128
SSD_XBC = D_SSM + 2 * SSD_G * SSD_N
SSD_IN = D_SSM + SSD_XBC + 2 * SSD_H
EV_IN = HY_IN + SSD_IN
EV_MIX = D_HY + D_SSM

GLA_H = 4
GLA_DV = (D_MODEL // 2) // GLA_H
GLA_DK = GLA_DV // 2
GLA_RANK = 16
GLA_GATE_NORM = 16.0
GLA_CHUNK = 64
GLA_IN = 2 * GLA_H * GLA_DK + 2 * GLA_H * GLA_DV + 2 * GLA_RANK

D_RG = D_MODEL // 2
RG_BLOCKS = 8
RG_BW = D_RG // RG_BLOCKS
RG_CONV = 4
RG_C = 8.0
RG_IN = 2 * D_RG
OD_IN = GLA_IN + RG_IN
OD_MIX = GLA_H * GLA_DV + D_RG

MOE_EXPERTS = 64
MOE_TOPK = 8
MOE_D_EXPERT = 256
MOE_D_SHARED = 256
MOE_SCALE = 2.5
MOE_BLOCK = 256

kernel_name = 'hybrid_hyena_ssd_gla_rglru_moe_dit'

F32 = jnp.float32


def rms_norm(x, w):
    xf = x.astype(F32)
    y = xf * lax.rsqrt(jnp.mean(jnp.square(xf), axis=-1, keepdims=True) + NORM_EPS)
    return y.astype(x.dtype) * w


def adaln(cvec, w, b):
    return jnp.split(jax.nn.silu(cvec) @ w + b, 6, axis=-1)


def modulate(h, shift, scale):
    return h * (1.0 + scale) + shift


def dwconv(x, w, b):
    y = lax.conv_general_dilated(x, w[:, None, :].astype(x.dtype), window_strides=(1,), padding='SAME',
                                 dimension_numbers=('NWC', 'WIO', 'NWC'), feature_group_count=x.shape[-1])
    return y + b.astype(x.dtype)


def maybe_flip(a, rev):
    return jnp.flip(a, axis=1) if rev else a


def to_col_major(x):
    b, n, d = x.shape
    rows = n // GRID_W
    return x.reshape(b, rows, GRID_W, d).transpose(0, 2, 1, 3).reshape(b, n, d)


def from_col_major(x):
    b, n, d = x.shape
    rows = n // GRID_W
    return x.reshape(b, GRID_W, rows, d).transpose(0, 2, 1, 3).reshape(b, n, d)


def hyena_filter_spectrum(n, fw0, fb0, fw1, fb1, fw2, fb2, fw3, freq):
    pos = jnp.arange(n, dtype=F32)
    t = pos / max(n - 1, 1)
    bands = jnp.linspace(1e-4, HY_BANDS - 1, HY_BANDS, dtype=F32)
    ang = (2.0 * math.pi / n) * pos[:, None] * bands[None, :]
    feats = jnp.concatenate([t[:, None], jnp.cos(ang), -jnp.sin(ang)], axis=-1)
    h = jnp.sin(freq * (feats @ fw0 + fb0))
    h = jnp.sin(freq * (h @ fw1 + fb1))
    h = jnp.sin(freq * (h @ fw2 + fb2))
    h = (h @ fw3).astype(F32).reshape(n, 2, HY_ORDER, D_HY)
    deltas = jnp.abs(jnp.linspace(math.log(HY_DECAY_PCT_LO) / HY_DECAY_TARGET,
                                  math.log(HY_DECAY_PCT_HI) / HY_DECAY_TARGET, D_HY, dtype=F32))
    h = h * jnp.exp(-t[:, None] * deltas)[:, None, None, :]
    taps = jnp.concatenate([h[:, 0], jnp.zeros((1, HY_ORDER, D_HY), F32), h[:0:-1, 1]], axis=0)
    taps = taps / jnp.sum(jnp.abs(taps), axis=0, keepdims=True)
    return jnp.fft.rfft(taps, axis=0)


def fft_long_conv(u, spec, bias):
    n = u.shape[1]
    uf = jnp.fft.rfft(u.astype(F32), n=2 * n, axis=1)
    y = jnp.fft.irfft(uf * spec, n=2 * n, axis=1)[:, :n]
    return (y + u.astype(F32) * bias).astype(u.dtype)


def hyena_mixer(p, conv_w, conv_b, filt, bias):
    u = dwconv(p, conv_w, conv_b)
    v, x1, x2 = jnp.split(u, 3, axis=-1)
    spec = hyena_filter_spectrum(p.shape[1], *filt)
    z = x1 * fft_long_conv(v, spec[:, 0], bias[0])
    return x2 * fft_long_conv(z, spec[:, 1], bias[1])


def ssd_inputs(p, conv_w, conv_b, dt_bias):
    b, n, _ = p.shape
    z = p[..., :D_SSM]
    xbc = jax.nn.silu(dwconv(p[..., D_SSM:D_SSM + SSD_XBC], conv_w, conv_b))
    xs = xbc[..., :D_SSM].reshape(b, n, SSD_G, SSD_HPG, SSD_P)
    bm = xbc[..., D_SSM:D_SSM + SSD_G * SSD_N].reshape(b, n, SSD_G, SSD_N)
    cm = xbc[..., D_SSM + SSD_G * SSD_N:].reshape(b, n, SSD_G, SSD_N)
    dt = jax.nn.softplus((p[..., D_SSM + SSD_XBC:] + dt_bias.reshape(-1)).astype(F32))
    return z, xs, bm, cm, dt.reshape(b, n, 2, SSD_G, SSD_HPG)


def ssd_states(xs, dt, a, bm, h0):
    b, n = xs.shape[:2]
    nc = n // SSD_CHUNK
    xc = xs.reshape(b, nc, SSD_CHUNK, SSD_G, SSD_HPG, SSD_P)
    dtc = dt.reshape(b, nc, SSD_CHUNK, SSD_G, SSD_HPG)
    bc = bm.reshape(b, nc, SSD_CHUNK, SSD_G, SSD_N)
    acs = jnp.cumsum(dtc * a, axis=2)
    w_end = jnp.exp(acs[:, :, -1:] - acs) * dtc
    states = jnp.einsum('bcqgn,bcqgk,bcqgkp->bcgkpn', bc, w_end, xc)
    chunk_decay = jnp.exp(acs[:, :, -1])

    def step(h, inp):
        st, dcy = inp
        return dcy[..., None, None] * h + st, h

    h_fin, h_prev = lax.scan(step, h0, (jnp.moveaxis(states, 1, 0), jnp.moveaxis(chunk_decay, 1, 0)))
    return jnp.moveaxis(h_prev, 0, 1), h_fin


def ssd_output(xs, dt, a, bm, cm, h_prev):
    b, n = xs.shape[:2]
    nc = n // SSD_CHUNK
    xc = xs.reshape(b, nc, SSD_CHUNK, SSD_G, SSD_HPG, SSD_P)
    dtc = dt.reshape(b, nc, SSD_CHUNK, SSD_G, SSD_HPG)
    bc = bm.reshape(b, nc, SSD_CHUNK, SSD_G, SSD_N)
    cc = cm.reshape(b, nc, SSD_CHUNK, SSD_G, SSD_N)
    acs = jnp.cumsum(dtc * a, axis=2)
    seg = acs[:, :, :, None] - acs[:, :, None, :]
    causal = jnp.tril(jnp.ones((SSD_CHUNK, SSD_CHUNK), bool))[:, :, None, None]
    decay = jnp.exp(jnp.where(causal, seg, -jnp.inf))
    cb = jnp.einsum('bcign,bcjgn->bcijg', cc, bc)
    lw = cb[..., None] * decay * dtc[:, :, None]
    y_in = jnp.einsum('bcijgk,bcjgkp->bcigkp', lw, xc)
    y_st = jnp.einsum('bcign,bcgkpn,bcigk->bcigkp', cc, h_prev, jnp.exp(acs))
    return (y_in + y_st).reshape(b, n, SSD_G, SSD_HPG, SSD_P)


def ssd_merge(ys, xs, z, d_skip, norm_w):
    b, n = z.shape[:2]
    y = ys[0] + ys[1] + d_skip.reshape(SSD_G, SSD_HPG, 1) * xs
    g = y.reshape(b, n, D_SSM) * jax.nn.silu(z)
    return rms_norm(g.reshape(b, n, SSD_G, D_SSM // SSD_G), norm_w.reshape(SSD_G, -1)).reshape(b, n, D_SSM)


def ssd_mixer(p_c, p_l, conv_w, conv_b, dt_bias, a_log, d_skip, norm_w, with_ctx):
    a = (-jnp.exp(a_log.astype(F32))).reshape(2, SSD_G, SSD_HPG)
    z_c, x_c, b_c, c_c, dt_c = ssd_inputs(p_c, conv_w, conv_b, dt_bias)
    z_l, x_l, b_l, c_l, dt_l = ssd_inputs(p_l, conv_w, conv_b, dt_bias)
    h0 = jnp.zeros((p_c.shape[0], SSD_G, SSD_HPG, SSD_P, SSD_N), F32)
    ys_c, ys_l = [], []
    for d, rev in enumerate((False, True)):
        f = functools.partial(maybe_flip, rev=rev)
        hp_c, hf_c = ssd_states(f(x_c), f(dt_c[:, :, d]), a[d], f(b_c), h0)
        if with_ctx:
            ys_c.append(f(ssd_output(f(x_c), f(dt_c[:, :, d]), a[d], f(b_c), f(c_c), hp_c)))
        hp_l, _ = ssd_states(f(x_l), f(dt_l[:, :, d]), a[d], f(b_l), hf_c)
        ys_l.append(f(ssd_output(f(x_l), f(dt_l[:, :, d]), a[d], f(b_l), f(c_l), hp_l)))
    y_l = ssd_merge(ys_l, x_l, z_l, d_skip, norm_w)
    if not with_ctx:
        return None, y_l
    return ssd_merge(ys_c, x_c, z_c, d_skip, norm_w), y_l


def gla_inputs(p, gate_w, gate_b):
    b, n, _ = p.shape
    nqk, nv = GLA_H * GLA_DK, GLA_H * GLA_DV
    q = p[..., :nqk].reshape(b, n, GLA_H, GLA_DK) * GLA_DK ** -0.5
    k = p[..., nqk:2 * nqk].reshape(b, n, GLA_H, GLA_DK)
    v = p[..., 2 * nqk:2 * nqk + nv].reshape(b, n, GLA_H, GLA_DV)
    lr = p[..., 2 * nqk + nv:2 * nqk + nv + 2 * GLA_RANK].reshape(b, n, 2, GLA_RANK)
    r = p[..., 2 * nqk + nv + 2 * GLA_RANK:]
    logit = jnp.einsum('bler,erk->blek', lr, gate_w) + gate_b
    log_g = (jax.nn.log_sigmoid(logit.astype(F32)) / GLA_GATE_NORM).reshape(b, n, 2, GLA_H, GLA_DK)
    return q, k, v, log_g, r


def gla_states(k, v, log_g, s0):
    b, n = k.shape[:2]
    nc = n // GLA_CHUNK
    kc = k.reshape(b, nc, GLA_CHUNK, GLA_H, GLA_DK)
    vc = v.reshape(b, nc, GLA_CHUNK, GLA_H, GLA_DV)
    gcum = jnp.cumsum(log_g.reshape(b, nc, GLA_CHUNK, GLA_H, GLA_DK), axis=2)
    states = jnp.einsum('bcqhd,bcqhv->bchdv', kc * jnp.exp(gcum[:, :, -1:] - gcum), vc)
    chunk_decay = jnp.exp(gcum[:, :, -1])

    def step(s, inp):
        st, dcy = inp
        return dcy[..., None] * s + st, s

    s_fin, s_prev = lax.scan(step, s0, (jnp.moveaxis(states, 1, 0), jnp.moveaxis(chunk_decay, 1, 0)))
    return jnp.moveaxis(s_prev, 0, 1), s_fin


def gla_output(q, k, v, log_g, s_prev):
    b, n = q.shape[:2]
    nc = n // GLA_CHUNK
    qc = q.reshape(b, nc, GLA_CHUNK, GLA_H, GLA_DK)
    kc = k.reshape(b, nc, GLA_CHUNK, GLA_H, GLA_DK)
    vc = v.reshape(b, nc, GLA_CHUNK, GLA_H, GLA_DV)
    gcum = jnp.cumsum(log_g.reshape(b, nc, GLA_CHUNK, GLA_H, GLA_DK), axis=2)
    qg = qc * jnp.exp(gcum)
    kg = kc * jnp.exp(-gcum)
    mask = jnp.tril(jnp.ones((GLA_CHUNK, GLA_CHUNK), bool))
    att = jnp.where(mask, jnp.einsum('bcihd,bcjhd->bchij', qg, kg), 0.0)
    o = jnp.einsum('bchij,bcjhv->bcihv', att, vc) + jnp.einsum('bcihd,bchdv->bcihv', qg, s_prev)
    return o.reshape(b, n, GLA_H, GLA_DV)


def gla_merge(os_, r, norm_w):
    b, n = r.shape[:2]
    o = rms_norm(os_[0] + os_[1], norm_w.reshape(GLA_H, GLA_DV))
    return o.reshape(b, n, GLA_H * GLA_DV) * jax.nn.silu(r)


def gla_mixer(p_c, p_l, gate_w, gate_b, norm_w, with_ctx):
    q_c, k_c, v_c, g_c, r_c = gla_inputs(p_c, gate_w, gate_b)
    q_l, k_l, v_l, g_l, r_l = gla_inputs(p_l, gate_w, gate_b)
    s0 = jnp.zeros((p_c.shape[0], GLA_H, GLA_DK, GLA_DV), F32)
    os_c, os_l = [], []
    for d, rev in enumerate((False, True)):
        f = functools.partial(maybe_flip, rev=rev)
        sp_c, sf_c = gla_states(f(k_c), f(v_c), f(g_c[:, :, d]), s0)
        if with_ctx:
            os_c.append(f(gla_output(f(q_c), f(k_c), f(v_c), f(g_c[:, :, d]), sp_c)))
        sp_l, _ = gla_states(f(k_l), f(v_l), f(g_l[:, :, d]), sf_c)
        os_l.append(f(gla_output(f(q_l), f(k_l), f(v_l), f(g_l[:, :, d]), sp_l)))
    y_l = gla_merge(os_l, r_l, norm_w)
    if not with_ctx:
        return None, y_l
    return gla_merge(os_c, r_c, norm_w), y_l


def rglru_inputs(p, conv_w, conv_b, w_a, b_a, w_x, b_x, lam):
    b, n, _ = p.shape
    u = dwconv(p[..., :D_RG], conv_w, conv_b)
    ub = u.reshape(b, n, RG_BLOCKS, RG_BW)
    r = jax.nn.sigmoid((jnp.einsum('blni,enio->bleno', ub, w_a).reshape(b, n, 2, D_RG) + b_a).astype(F32))
    i = jax.nn.sigmoid((jnp.einsum('blni,enio->bleno', ub, w_x).reshape(b, n, 2, D_RG) + b_x).astype(F32))
    log_a = -RG_C * jax.nn.softplus(-lam.astype(F32)) * r
    x_in = jnp.sqrt(-jnp.expm1(2.0 * log_a)) * i * u[:, :, None, :].astype(F32)
    return p[..., D_RG:], jnp.exp(log_a), x_in


def lru_scan(a, u, h0):
    u = u.at[:, 0].add(a[:, 0] * h0)

    def combine(lhs, rhs):
        a1, b1 = lhs
        a2, b2 = rhs
        return a1 * a2, a2 * b1 + b2

    return lax.associative_scan(combine, (a, u), axis=1)[1]


def rglru_mixer(p_c, p_l, conv_w, conv_b, w_a, b_a, w_x, b_x, lam, with_ctx):
    gb_c, a_c, u_c = rglru_inputs(p_c, conv_w, conv_b, w_a, b_a, w_x, b_x, lam)
    gb_l, a_l, u_l = rglru_inputs(p_l, conv_w, conv_b, w_a, b_a, w_x, b_x, lam)
    h0 = jnp.zeros((p_c.shape[0], D_RG), F32)
    hs_c, hs_l = [], []
    for d, rev in enumerate((False, True)):
        f = functools.partial(maybe_flip, rev=rev)
        h_c = f(lru_scan(f(a_c[:, :, d]), f(u_c[:, :, d]), h0))
        h_end = h_c[:, 0] if rev else h_c[:, -1]
        hs_c.append(h_c)
        hs_l.append(f(lru_scan(f(a_l[:, :, d]), f(u_l[:, :, d]), h_end)))
    y_l = (hs_l[0] + hs_l[1]) * jax.nn.gelu(gb_l.astype(F32))
    if not with_ctx:
        return None, y_l
    return (hs_c[0] + hs_c[1]) * jax.nn.gelu(gb_c.astype(F32)), y_l


def even_mixer(h_c, h_l, w_in, w_out, hy_conv_w, hy_conv_b, hy_filt, hy_bias, ssd_args, with_ctx):
    p_c = h_c @ w_in
    p_l = h_l @ w_in
    s_c, s_l = ssd_mixer(p_c[..., HY_IN:], p_l[..., HY_IN:], *ssd_args, with_ctx)
    y_l = jnp.concatenate([hyena_mixer(p_l[..., :HY_IN], hy_conv_w, hy_conv_b, hy_filt, hy_bias), s_l], axis=-1) @ w_out
    if not with_ctx:
        return None, y_l
    y_c = jnp.concatenate([hyena_mixer(p_c[..., :HY_IN], hy_conv_w, hy_conv_b, hy_filt, hy_bias), s_c], axis=-1) @ w_out
    return y_c, y_l


def odd_mixer(h_c, h_l, w_in, w_out, gla_args, rg_args, with_ctx):
    p_c = h_c @ w_in
    p_l = to_col_major(h_l) @ w_in
    a_c, a_l = gla_mixer(p_c[..., :GLA_IN], p_l[..., :GLA_IN], *gla_args, with_ctx)
    r_c, r_l = rglru_mixer(p_c[..., GLA_IN:], p_l[..., GLA_IN:], *rg_args, with_ctx)
    y_l = from_col_major(jnp.concatenate([a_l, r_l], axis=-1)) @ w_out
    if not with_ctx:
        return None, y_l
    return jnp.concatenate([a_c, r_c], axis=-1) @ w_out, y_l


def moe_ffn(h, router_w, router_b, w_gate, w_up, w_down, sh_gate, sh_up, sh_down):
    n, d = h.shape
    scores = jax.nn.sigmoid((h @ router_w).astype(F32))
    _, idx = lax.top_k(scores + router_b.astype(F32), MOE_TOPK)
    wsel = jnp.take_along_axis(scores, idx, axis=1)
    wsel = wsel / jnp.sum(wsel, axis=1, keepdims=True) * MOE_SCALE
    flat_e = idx.reshape(-1)
    nk = flat_e.shape[0]
    order = jnp.argsort(flat_e)
    sorted_e = flat_e[order]
    counts = jnp.bincount(flat_e, length=MOE_EXPERTS)
    padded = (counts + MOE_BLOCK - 1) // MOE_BLOCK * MOE_BLOCK
    ends = jnp.cumsum(padded)
    dest = (ends - padded)[sorted_e] + jnp.arange(nk) - (jnp.cumsum(counts) - counts)[sorted_e]
    n_blocks = -(-nk // MOE_BLOCK) + MOE_EXPERTS
    rows = n_blocks * MOE_BLOCK
    row_tok = jnp.full((rows,), n, jnp.int32).at[dest].set((order // MOE_TOPK).astype(jnp.int32))
    row_w = jnp.zeros((rows,), F32).at[dest].set(wsel.reshape(-1)[order])
    block_e = jnp.minimum(jnp.searchsorted(ends, jnp.arange(n_blocks) * MOE_BLOCK, side='right'), MOE_EXPERTS - 1)
    h_pad = jnp.concatenate([h, jnp.zeros((1, d), h.dtype)], axis=0)

    def expert_block(args):
        tok, e = args
        xb = h_pad[tok]
        return (jax.nn.silu(xb @ w_gate[e]) * (xb @ w_up[e])) @ w_down[e]

    y_rows = lax.map(expert_block, (row_tok.reshape(n_blocks, MOE_BLOCK), block_e))
    y_rows = y_rows.reshape(rows, d)
    routed = jax.ops.segment_sum(y_rows * row_w[:, None].astype(y_rows.dtype), row_tok, num_segments=n + 1)[:n]
    shared = (jax.nn.silu(h @ sh_gate) * (h @ sh_up)) @ sh_down
    return routed + shared


def setup_inputs(seed: int = 0) -> dict:
    key = jax.random.key(seed)
    ks = iter(jax.random.split(key, 64))
    n_ev = (DEPTH + 1) // 2
    n_od = DEPTH // 2

    def nrm(shape, scale=1.0):
        return scale * jax.random.normal(next(ks), shape, F32)

    def gain(shape):
        return 1.0 + nrm(shape, 0.05)

    def uni(shape, lo, hi):
        return jax.random.uniform(next(ks), shape, F32, lo, hi)

    dt0 = jnp.exp(uni((n_ev, 2, SSD_H), math.log(1e-3), math.log(1e-1)))
    a_root = uni((n_od, 2, D_RG), 0.9, 0.999) ** (1.0 / RG_C)
    return {
        'x': nrm((BATCH, SEQ, D_MODEL)),
        'c': nrm((BATCH, D_MODEL)),
        'ctx': nrm((BATCH, CTX_LEN, D_MODEL)),
        'c_ctx': nrm((D_MODEL,)),
        'ada_w': nrm((DEPTH, D_MODEL, 6 * D_MODEL), 0.5 * D_MODEL ** -0.5),
        'ada_b': nrm((DEPTH, 6 * D_MODEL), 0.02),
        'norm1_w': gain((DEPTH, D_MODEL)),
        'norm2_w': gain((DEPTH, D_MODEL)),
        'ev_w_in': nrm((n_ev, D_MODEL, EV_IN), D_MODEL ** -0.5),
        'ev_w_out': nrm((n_ev, EV_MIX, D_MODEL), EV_MIX ** -0.5),
        'hy_conv_w': nrm((n_ev, HY_SHORT, HY_IN), HY_SHORT ** -0.5),
        'hy_conv_b': nrm((n_ev, HY_IN), 0.02),
        'hy_fw0': nrm((n_ev, HY_EMB, HY_FF), HY_EMB ** -0.5),
        'hy_fb0': nrm((n_ev, HY_FF), 0.1),
        'hy_fw1': nrm((n_ev, HY_FF, HY_FF), HY_FF ** -0.5),
        'hy_fb1': nrm((n_ev, HY_FF), 0.1),
        'hy_fw2': nrm((n_ev, HY_FF, HY_FF), HY_FF ** -0.5),
        'hy_fb2': nrm((n_ev, HY_FF), 0.1),
        'hy_fw3': nrm((n_ev, HY_FF, 2 * HY_ORDER * D_HY), HY_FF ** -0.5),
        'hy_freq': 1.0 + nrm((n_ev, HY_FF), 0.1),
        'hy_bias': nrm((n_ev, HY_ORDER, D_HY), 0.5),
        'ssd_conv_w': nrm((n_ev, SSD_CONV, SSD_XBC), SSD_CONV ** -0.5),
        'ssd_conv_b': nrm((n_ev, SSD_XBC), 0.02),
        'ssd_dt_bias': dt0 + jnp.log(-jnp.expm1(-dt0)),
        'ssd_a_log': jnp.log(uni((n_ev, 2, SSD_H), 1.0, 16.0)),
        'ssd_d': 1.0 + nrm((n_ev, SSD_H), 0.1),
        'ssd_norm_w': gain((n_ev, D_SSM)),
        'od_w_in': nrm((n_od, D_MODEL, OD_IN), D_MODEL ** -0.5),
        'od_w_out': nrm((n_od, OD_MIX, D_MODEL), OD_MIX ** -0.5),
        'gla_gate_w': nrm((n_od, 2, GLA_RANK, GLA_H * GLA_DK), GLA_RANK ** -0.5),
        'gla_gate_b': nrm((n_od, 2, GLA_H * GLA_DK), 0.1),
        'gla_norm_w': gain((n_od, GLA_H * GLA_DV)),
        'rg_conv_w': nrm((n_od, RG_CONV, D_RG), RG_CONV ** -0.5),
        'rg_conv_b': nrm((n_od, D_RG), 0.02),
        'rg_w_a': nrm((n_od, 2, RG_BLOCKS, RG_BW, RG_BW), RG_BW ** -0.5),
        'rg_b_a': nrm((n_od, 2, D_RG), 0.02),
        'rg_w_x': nrm((n_od, 2, RG_BLOCKS, RG_BW, RG_BW), RG_BW ** -0.5),
        'rg_b_x': nrm((n_od, 2, D_RG), 0.02),
        'rg_lambda': jnp.log(a_root) - jnp.log1p(-a_root),
        'router_w': nrm((DEPTH, D_MODEL, MOE_EXPERTS), D_MODEL ** -0.5),
        'router_b': nrm((DEPTH, MOE_EXPERTS), 0.01),
        'moe_w_gate': nrm((DEPTH, MOE_EXPERTS, D_MODEL, MOE_D_EXPERT), D_MODEL ** -0.5),
        'moe_w_up': nrm((DEPTH, MOE_EXPERTS, D_MODEL, MOE_D_EXPERT), D_MODEL ** -0.5),
        'moe_w_down': nrm((DEPTH, MOE_EXPERTS, MOE_D_EXPERT, D_MODEL), MOE_D_EXPERT ** -0.5),
        'sh_w_gate': nrm((DEPTH, D_MODEL, MOE_D_SHARED), D_MODEL ** -0.5),
        'sh_w_up': nrm((DEPTH, D_MODEL, MOE_D_SHARED), D_MODEL ** -0.5),
        'sh_w_down': nrm((DEPTH, MOE_D_SHARED, D_MODEL), MOE_D_SHARED ** -0.5),
        'final_norm_w': gain((D_MODEL,)),
    }


def reference(x, c, ctx, c_ctx, ada_w, ada_b, norm1_w, norm2_w,
              ev_w_in, ev_w_out, hy_conv_w, hy_conv_b, hy_fw0, hy_fb0, hy_fw1, hy_fb1, hy_fw2, hy_fb2,
              hy_fw3, hy_freq, hy_bias, ssd_conv_w, ssd_conv_b, ssd_dt_bias, ssd_a_log, ssd_d, ssd_norm_w,
              od_w_in, od_w_out, gla_gate_w, gla_gate_b, gla_norm_w,
              rg_conv_w, rg_conv_b, rg_w_a, rg_b_a, rg_w_x, rg_b_x, rg_lambda,
              router_w, router_b, moe_w_gate, moe_w_up, moe_w_down, sh_w_gate, sh_w_up, sh_w_down,
              final_norm_w):
    for i in range(DEPTH):
        last = i == DEPTH - 1
        j = i // 2
        sh1, sc1, g1, sh2, sc2, g2 = [m[:, None, :] for m in adaln(c, ada_w[i], ada_b[i])]
        csh1, csc1, cg1, csh2, csc2, cg2 = adaln(c_ctx, ada_w[i], ada_b[i])
        h_l = modulate(rms_norm(x, norm1_w[i]), sh1, sc1)
        h_c = modulate(rms_norm(ctx, norm1_w[i]), csh1, csc1)
        if i % 2 == 0:
            hy_filt = (hy_fw0[j], hy_fb0[j], hy_fw1[j], hy_fb1[j], hy_fw2[j], hy_fb2[j], hy_fw3[j], hy_freq[j])
            ssd_args = (ssd_conv_w[j], ssd_conv_b[j], ssd_dt_bias[j], ssd_a_log[j], ssd_d[j], ssd_norm_w[j])
            mix_c, mix_l = even_mixer(h_c, h_l, ev_w_in[j], ev_w_out[j], hy_conv_w[j], hy_conv_b[j],
                                      hy_filt, hy_bias[j], ssd_args, not last)
        else:
            gla_args = (gla_gate_w[j], gla_gate_b[j], gla_norm_w[j])
            rg_args = (rg_conv_w[j], rg_conv_b[j], rg_w_a[j], rg_b_a[j], rg_w_x[j], rg_b_x[j], rg_lambda[j])
            mix_c, mix_l = odd_mixer(h_c, h_l, od_w_in[j], od_w_out[j], gla_args, rg_args, not last)
        x = x + g1 * mix_l
        moe_w = (router_w[i], router_b[i], moe_w_gate[i], moe_w_up[i], moe_w_down[i],
                 sh_w_gate[i], sh_w_up[i], sh_w_down[i])
        h_l = modulate(rms_norm(x, norm2_w[i]), sh2, sc2)
        if last:
            x = x + g2 * moe_ffn(h_l.reshape(-1, D_MODEL), *moe_w).reshape(h_l.shape)
        else:
            ctx = ctx + cg1 * mix_c
            h_c = modulate(rms_norm(ctx, norm2_w[i]), csh2, csc2)
            h_all = jnp.concatenate([h_c, h_l], axis=1)
            y = moe_ffn(h_all.reshape(-1, D_MODEL), *moe_w).reshape(h_all.shape)
            n_ctx = ctx.shape[1]
            ctx = ctx + cg2 * y[:, :n_ctx]
            x = x + g2 * y[:, n_ctx:]
    return rms_norm(x, final_norm_w)
```

```python
import functools
import math

import jax
import jax.numpy as jnp
from jax import lax
from jax.experimental import pallas as pl
from jax.experimental.pallas import tpu as pltpu

D_MODEL = 1024
BATCH = 16
SEQ = 2048
DEPTH = 2

CTX_LEN = 256
GRID_W = 64
NORM_EPS = 1e-6

D_HY = D_MODEL // 2
HY_ORDER = 2
HY_SHORT = 3
HY_BANDS = 16
HY_EMB = 1 + 2 * HY_BANDS
HY_FF = 64
HY_DECAY_PCT_LO = 0.3
HY_DECAY_PCT_HI = 1.5
HY_DECAY_TARGET = 1e-2
HY_IN = 3 * D_HY

D_SSM = D_MODEL // 2
SSD_P = 64
SSD_H = D_SSM // SSD_P
SSD_G = 2
SSD_HPG = SSD_H // SSD_G
SSD_N = 128
SSD_CONV = 4
SSD_CHUNK = 128
SSD_XBC = D_SSM + 2 * SSD_G * SSD_N
SSD_IN = D_SSM + SSD_XBC + 2 * SSD_H
EV_IN = HY_IN + SSD_IN
EV_MIX = D_HY + D_SSM

GLA_H = 4
GLA_DV = (D_MODEL // 2) // GLA_H
GLA_DK = GLA_DV // 2
GLA_RANK = 16
GLA_GATE_NORM = 16.0
GLA_CHUNK = 64
GLA_IN = 2 * GLA_H * GLA_DK + 2 * GLA_H * GLA_DV + 2 * GLA_RANK

D_RG = D_MODEL // 2
RG_BLOCKS = 8
RG_BW = D_RG // RG_BLOCKS
RG_CONV = 4
RG_C = 8.0
RG_IN = 2 * D_RG
OD_IN = GLA_IN + RG_IN
OD_MIX = GLA_H * GLA_DV + D_RG

MOE_EXPERTS = 64
MOE_TOPK = 8
MOE_D_EXPERT = 256
MOE_D_SHARED = 256
MOE_SCALE = 2.5
MOE_BLOCK = 256

F32 = jnp.float32
BF16 = jnp.bfloat16

V7X_VMEM_BYTES = 64 * 1024 * 1024
VMEM_LIMIT = 48 * 1024 * 1024


def _cparams(sem):
    return pltpu.CompilerParams(dimension_semantics=sem, vmem_limit_bytes=VMEM_LIMIT)


def _mm_bf16_body(a_ref, w_ref, o_ref):
    o_ref[...] = jnp.dot(a_ref[...].astype(BF16), w_ref[...].astype(BF16),
                         preferred_element_type=F32).astype(o_ref.dtype)


def _mm_f32_body(a_ref, w_ref, o_ref):
    o_ref[...] = jnp.dot(a_ref[...], w_ref[...], preferred_element_type=F32,
                         precision=lax.Precision.HIGHEST).astype(o_ref.dtype)


def _pick_tile(n, pref):
    t = min(n, pref)
    while n % t:
        t //= 2
    return t


def pmatmul(a, w, *, exact=False, out_dtype=F32, tm=512, tn=None):
    m, k = a.shape
    n = w.shape[1]
    tm = _pick_tile(m, tm)
    tn = n if tn is None else _pick_tile(n, tn)
    body = _mm_f32_body if exact else _mm_bf16_body
    return pl.pallas_call(
        body,
        grid=(m // tm, n // tn),
        in_specs=[pl.BlockSpec((tm, k), lambda i, j: (i, 0)),
                  pl.BlockSpec((k, tn), lambda i, j: (0, j))],
        out_specs=pl.BlockSpec((tm, tn), lambda i, j: (i, j)),
        out_shape=jax.ShapeDtypeStruct((m, n), out_dtype),
        compiler_params=_cparams(("parallel", "parallel")),
    )(a, w)


def _mm3(a, w, **kw):
    lead = a.shape[:-1]
    return pmatmul(a.reshape(-1, a.shape[-1]), w, **kw).reshape(*lead, w.shape[1])


def _silu(x):
    return x / (1.0 + jnp.exp(-x))


def _swiglu(x, wg, wu, wd):
    g = jnp.dot(x, wg, preferred_element_type=F32)
    u = jnp.dot(x, wu, preferred_element_type=F32)
    h = (_silu(g) * u).astype(BF16)
    return jnp.dot(h, wd, preferred_element_type=F32)


def _expert_body(be_ref, x_ref, wg_ref, wu_ref, wd_ref, rw_ref, o_ref):
    del be_ref
    y = _swiglu(x_ref[...], wg_ref[0], wu_ref[0], wd_ref[0])
    o_ref[...] = y * rw_ref[...]


def moe_experts(x_rows, block_e, row_w, wg, wu, wd):
    rows, d = x_rows.shape
    n_blocks = rows // MOE_BLOCK
    f = wg.shape[-1]
    grid_spec = pltpu.PrefetchScalarGridSpec(
        num_scalar_prefetch=1,
        grid=(n_blocks,),
        in_specs=[
            pl.BlockSpec((MOE_BLOCK, d), lambda i, be: (i, 0)),
            pl.BlockSpec((1, d, f), lambda i, be: (be[i], 0, 0)),
            pl.BlockSpec((1, d, f), lambda i, be: (be[i], 0, 0)),
            pl.BlockSpec((1, f, d), lambda i, be: (be[i], 0, 0)),
            pl.BlockSpec((MOE_BLOCK, 1), lambda i, be: (i, 0)),
        ],
        out_specs=pl.BlockSpec((MOE_BLOCK, d), lambda i, be: (i, 0)),
    )
    return pl.pallas_call(
        _expert_body,
        grid_spec=grid_spec,
        out_shape=jax.ShapeDtypeStruct((rows, d), F32),
        compiler_params=_cparams(("arbitrary",)),
    )(block_e, x_rows, wg, wu, wd, row_w.reshape(rows, 1))


def _shared_body(x_ref, wg_ref, wu_ref, wd_ref, o_ref):
    o_ref[...] = _swiglu(x_ref[...], wg_ref[...], wu_ref[...], wd_ref[...])


def shared_expert(x, wg, wu, wd, tm=512):
    m, d = x.shape
    f = wg.shape[-1]
    tm = _pick_tile(m, tm)
    return pl.pallas_call(
        _shared_body,
        grid=(m // tm,),
        in_specs=[pl.BlockSpec((tm, d), lambda i: (i, 0)),
                  pl.BlockSpec((d, f), lambda i: (0, 0)),
                  pl.BlockSpec((d, f), lambda i: (0, 0)),
                  pl.BlockSpec((f, d), lambda i: (0, 0))],
        out_specs=pl.BlockSpec((tm, d), lambda i: (i, 0)),
        out_shape=jax.ShapeDtypeStruct((m, d), F32),
        compiler_params=_cparams(("parallel",)),
    )(x, wg, wu, wd)


def rms_norm(x, w):
    xf = x.astype(F32)
    y = xf * lax.rsqrt(jnp.mean(jnp.square(xf), axis=-1, keepdims=True) + NORM_EPS)
    return y.astype(x.dtype) * w


def adaln(cvec, w, b):
    cv = jax.nn.silu(cvec)
    if cv.ndim == 1:
        return jnp.split(pmatmul(cv[None, :], w, exact=True, tn=1536)[0] + b, 6, axis=-1)
    return jnp.split(pmatmul(cv, w, exact=True, tn=1536) + b, 6, axis=-1)


def modulate(h, shift, scale):
    return h * (1.0 + scale) + shift


def dwconv(x, w, b):
    y = lax.conv_general_dilated(x, w[:, None, :].astype(x.dtype), window_strides=(1,), padding='SAME',
                                 dimension_numbers=('NWC', 'WIO', 'NWC'), feature_group_count=x.shape[-1])
    return y + b.astype(x.dtype)


def maybe_flip(a, rev):
    return jnp.flip(a, axis=1) if rev else a


def to_col_major(x):
    b, n, d = x.shape
    rows = n // GRID_W
    return x.reshape(b, rows, GRID_W, d).transpose(0, 2, 1, 3).reshape(b, n, d)


def from_col_major(x):
    b, n, d = x.shape
    rows = n // GRID_W
    return x.reshape(b, GRID_W, rows, d).transpose(0, 2, 1, 3).reshape(b, n, d)


def hyena_filter_spectrum(n, fw0, fb0, fw1, fb1, fw2, fb2, fw3, freq):
    pos = jnp.arange(n, dtype=F32)
    t = pos / max(n - 1, 1)
    bands = jnp.linspace(1e-4, HY_BANDS - 1, HY_BANDS, dtype=F32)
    ang = (2.0 * math.pi / n) * pos[:, None] * bands[None, :]
    feats = jnp.concatenate([t[:, None], jnp.cos(ang), -jnp.sin(ang)], axis=-1)
    hp = lax.Precision.HIGHEST
    h = jnp.sin(freq * (jnp.dot(feats, fw0, precision=hp) + fb0))
    h = jnp.sin(freq * (jnp.dot(h, fw1, precision=hp) + fb1))
    h = jnp.sin(freq * (jnp.dot(h, fw2, precision=hp) + fb2))
    h = jnp.dot(h, fw3, precision=hp).astype(F32).reshape(n, 2, HY_ORDER, D_HY)
    deltas = jnp.abs(jnp.linspace(math.log(HY_DECAY_PCT_LO) / HY_DECAY_TARGET,
                                  math.log(HY_DECAY_PCT_HI) / HY_DECAY_TARGET, D_HY, dtype=F32))
    h = h * jnp.exp(-t[:, None] * deltas)[:, None, None, :]
    taps = jnp.concatenate([h[:, 0], jnp.zeros((1, HY_ORDER, D_HY), F32), h[:0:-1, 1]], axis=0)
    taps = taps / jnp.sum(jnp.abs(taps), axis=0, keepdims=True)
    return jnp.fft.rfft(taps, axis=0)


def fft_long_conv(u, spec, bias):
    n = u.shape[1]
    uf = jnp.fft.rfft(u.astype(F32), n=2 * n, axis=1)
    y = jnp.fft.irfft(uf * spec, n=2 * n, axis=1)[:, :n]
    return (y + u.astype(F32) * bias).astype(u.dtype)


def hyena_mixer(p, conv_w, conv_b, filt, bias):
    u = dwconv(p, conv_w, conv_b)
    v, x1, x2 = jnp.split(u, 3, axis=-1)
    spec = hyena_filter_spectrum(p.shape[1], *filt)
    z = x1 * fft_long_conv(v, spec[:, 0], bias[0])
    return x2 * fft_long_conv(z, spec[:, 1], bias[1])


def ssd_inputs(p, conv_w, conv_b, dt_bias):
    b, n, _ = p.shape
    z = p[..., :D_SSM]
    xbc = jax.nn.silu(dwconv(p[..., D_SSM:D_SSM + SSD_XBC], conv_w, conv_b))
    xs = xbc[..., :D_SSM].reshape(b, n, SSD_G, SSD_HPG, SSD_P)
    bm = xbc[..., D_SSM:D_SSM + SSD_G * SSD_N].reshape(b, n, SSD_G, SSD_N)
    cm = xbc[..., D_SSM + SSD_G * SSD_N:].reshape(b, n, SSD_G, SSD_N)
    dt = jax.nn.softplus((p[..., D_SSM + SSD_XBC:] + dt_bias.reshape(-1)).astype(F32))
    return z, xs, bm, cm, dt.reshape(b, n, 2, SSD_G, SSD_HPG)


def ssd_states(xs, dt, a, bm, h0):
    b, n = xs.shape[:2]
    nc = n // SSD_CHUNK
    xc = xs.reshape(b, nc, SSD_CHUNK, SSD_G, SSD_HPG, SSD_P)
    dtc = dt.reshape(b, nc, SSD_CHUNK, SSD_G, SSD_HPG)
    bc = bm.reshape(b, nc, SSD_CHUNK, SSD_G, SSD_N)
    acs = jnp.cumsum(dtc * a, axis=2)
    w_end = jnp.exp(acs[:, :, -1:] - acs) * dtc
    states = jnp.einsum('bcqgn,bcqgk,bcqgkp->bcgkpn', bc, w_end, xc)
    chunk_decay = jnp.exp(acs[:, :, -1])

    def step(h, inp):
        st, dcy = inp
        return dcy[..., None, None] * h + st, h

    h_fin, h_prev = lax.scan(step, h0, (jnp.moveaxis(states, 1, 0), jnp.moveaxis(chunk_decay, 1, 0)))
    return jnp.moveaxis(h_prev, 0, 1), h_fin


def ssd_output(xs, dt, a, bm, cm, h_prev):
    b, n = xs.shape[:2]
    nc = n // SSD_CHUNK
    xc = xs.reshape(b, nc, SSD_CHUNK, SSD_G, SSD_HPG, SSD_P)
    dtc = dt.reshape(b, nc, SSD_CHUNK, SSD_G, SSD_HPG)
    bc = bm.reshape(b, nc, SSD_CHUNK, SSD_G, SSD_N)
    cc = cm.reshape(b, nc, SSD_CHUNK, SSD_G, SSD_N)
    acs = jnp.cumsum(dtc * a, axis=2)
    seg = acs[:, :, :, None] - acs[:, :, None, :]
    causal = jnp.tril(jnp.ones((SSD_CHUNK, SSD_CHUNK), bool))[:, :, None, None]
    decay = jnp.exp(jnp.where(causal, seg, -jnp.inf))
    cb = jnp.einsum('bcign,bcjgn->bcijg', cc, bc)
    lw = cb[..., None] * decay * dtc[:, :, None]
    y_in = jnp.einsum('bcijgk,bcjgkp->bcigkp', lw, xc)
    y_st = jnp.einsum('bcign,bcgkpn,bcigk->bcigkp', cc, h_prev, jnp.exp(acs))
    return (y_in + y_st).reshape(b, n, SSD_G, SSD_HPG, SSD_P)


def ssd_merge(ys, xs, z, d_skip, norm_w):
    b, n = z.shape[:2]
    y = ys[0] + ys[1] + d_skip.reshape(SSD_G, SSD_HPG, 1) * xs
    g = y.reshape(b, n, D_SSM) * jax.nn.silu(z)
    return rms_norm(g.reshape(b, n, SSD_G, D_SSM // SSD_G), norm_w.reshape(SSD_G, -1)).reshape(b, n, D_SSM)


def ssd_mixer(p_c, p_l, conv_w, conv_b, dt_bias, a_log, d_skip, norm_w, with_ctx):
    a = (-jnp.exp(a_log.astype(F32))).reshape(2, SSD_G, SSD_HPG)
    z_c, x_c, b_c, c_c, dt_c = ssd_inputs(p_c, conv_w, conv_b, dt_bias)
    z_l, x_l, b_l, c_l, dt_l = ssd_inputs(p_l, conv_w, conv_b, dt_bias)
    h0 = jnp.zeros((p_c.shape[0], SSD_G, SSD_HPG, SSD_P, SSD_N), F32)
    ys_c, ys_l = [], []
    for d, rev in enumerate((False, True)):
        f = functools.partial(maybe_flip, rev=rev)
        hp_c, hf_c = ssd_states(f(x_c), f(dt_c[:, :, d]), a[d], f(b_c), h0)
        if with_ctx:
            ys_c.append(f(ssd_output(f(x_c), f(dt_c[:, :, d]), a[d], f(b_c), f(c_c), hp_c)))
        hp_l, _ = ssd_states(f(x_l), f(dt_l[:, :, d]), a[d], f(b_l), hf_c)
        ys_l.append(f(ssd_output(f(x_l), f(dt_l[:, :, d]), a[d], f(b_l), f(c_l), hp_l)))
    y_l = ssd_merge(ys_l, x_l, z_l, d_skip, norm_w)
    if not with_ctx:
        return None, y_l
    return ssd_merge(ys_c, x_c, z_c, d_skip, norm_w), y_l


def gla_inputs(p, gate_w, gate_b):
    b, n, _ = p.shape
    nqk, nv = GLA_H * GLA_DK, GLA_H * GLA_DV
    q = p[..., :nqk].reshape(b, n, GLA_H, GLA_DK) * GLA_DK ** -0.5
    k = p[..., nqk:2 * nqk].reshape(b, n, GLA_H, GLA_DK)
    v = p[..., 2 * nqk:2 * nqk + nv].reshape(b, n, GLA_H, GLA_DV)
    lr = p[..., 2 * nqk + nv:2 * nqk + nv + 2 * GLA_RANK].reshape(b, n, 2, GLA_RANK)
    r = p[..., 2 * nqk + nv + 2 * GLA_RANK:]
    logit = jnp.einsum('bler,erk->blek', lr, gate_w) + gate_b
    log_g = (jax.nn.log_sigmoid(logit.astype(F32)) / GLA_GATE_NORM).reshape(b, n, 2, GLA_H, GLA_DK)
    return q, k, v, log_g, r


def gla_states(k, v, log_g, s0):
    b, n = k.shape[:2]
    nc = n // GLA_CHUNK
    kc = k.reshape(b, nc, GLA_CHUNK, GLA_H, GLA_DK)
    vc = v.reshape(b, nc, GLA_CHUNK, GLA_H, GLA_DV)
    gcum = jnp.cumsum(log_g.reshape(b, nc, GLA_CHUNK, GLA_H, GLA_DK), axis=2)
    states = jnp.einsum('bcqhd,bcqhv->bchdv', kc * jnp.exp(gcum[:, :, -1:] - gcum), vc)
    chunk_decay = jnp.exp(gcum[:, :, -1])

    def step(s, inp):
        st, dcy = inp
        return dcy[..., None] * s + st, s

    s_fin, s_prev = lax.scan(step, s0, (jnp.moveaxis(states, 1, 0), jnp.moveaxis(chunk_decay, 1, 0)))
    return jnp.moveaxis(s_prev, 0, 1), s_fin


def gla_output(q, k, v, log_g, s_prev):
    b, n = q.shape[:2]
    nc = n // GLA_CHUNK
    qc = q.reshape(b, nc, GLA_CHUNK, GLA_H, GLA_DK)
    kc = k.reshape(b, nc, GLA_CHUNK, GLA_H, GLA_DK)
    vc = v.reshape(b, nc, GLA_CHUNK, GLA_H, GLA_DV)
    gcum = jnp.cumsum(log_g.reshape(b, nc, GLA_CHUNK, GLA_H, GLA_DK), axis=2)
    qg = qc * jnp.exp(gcum)
    kg = kc * jnp.exp(-gcum)
    mask = jnp.tril(jnp.ones((GLA_CHUNK, GLA_CHUNK), bool))
    att = jnp.where(mask, jnp.einsum('bcihd,bcjhd->bchij', qg, kg), 0.0)
    o = jnp.einsum('bchij,bcjhv->bcihv', att, vc) + jnp.einsum('bcihd,bchdv->bcihv', qg, s_prev)
    return o.reshape(b, n, GLA_H, GLA_DV)


def gla_merge(os_, r, norm_w):
    b, n = r.shape[:2]
    o = rms_norm(os_[0] + os_[1], norm_w.reshape(GLA_H, GLA_DV))
    return o.reshape(b, n, GLA_H * GLA_DV) * jax.nn.silu(r)


def gla_mixer(p_c, p_l, gate_w, gate_b, norm_w, with_ctx):
    q_c, k_c, v_c, g_c, r_c = gla_inputs(p_c, gate_w, gate_b)
    q_l, k_l, v_l, g_l, r_l = gla_inputs(p_l, gate_w, gate_b)
    s0 = jnp.zeros((p_c.shape[0], GLA_H, GLA_DK, GLA_DV), F32)
    os_c, os_l = [], []
    for d, rev in enumerate((False, True)):
        f = functools.partial(maybe_flip, rev=rev)
        sp_c, sf_c = gla_states(f(k_c), f(v_c), f(g_c[:, :, d]), s0)
        if with_ctx:
            os_c.append(f(gla_output(f(q_c), f(k_c), f(v_c), f(g_c[:, :, d]), sp_c)))
        sp_l, _ = gla_states(f(k_l), f(v_l), f(g_l[:, :, d]), sf_c)
        os_l.append(f(gla_output(f(q_l), f(k_l), f(v_l), f(g_l[:, :, d]), sp_l)))
    y_l = gla_merge(os_l, r_l, norm_w)
    if not with_ctx:
        return None, y_l
    return gla_merge(os_c, r_c, norm_w), y_l


def rglru_inputs(p, conv_w, conv_b, w_a, b_a, w_x, b_x, lam):
    b, n, _ = p.shape
    u = dwconv(p[..., :D_RG], conv_w, conv_b)
    ub = u.reshape(b, n, RG_BLOCKS, RG_BW)
    r = jax.nn.sigmoid((jnp.einsum('blni,enio->bleno', ub, w_a).reshape(b, n, 2, D_RG) + b_a).astype(F32))
    i = jax.nn.sigmoid((jnp.einsum('blni,enio->bleno', ub, w_x).reshape(b, n, 2, D_RG) + b_x).astype(F32))
    log_a = -RG_C * jax.nn.softplus(-lam.astype(F32)) * r
    x_in = jnp.sqrt(-jnp.expm1(2.0 * log_a)) * i * u[:, :, None, :].astype(F32)
    return p[..., D_RG:], jnp.exp(log_a), x_in


def lru_scan(a, u, h0):
    u = u.at[:, 0].add(a[:, 0] * h0)

    def combine(lhs, rhs):
        a1, b1 = lhs
        a2, b2 = rhs
        return a1 * a2, a2 * b1 + b2

    return lax.associative_scan(combine, (a, u), axis=1)[1]


def rglru_mixer(p_c, p_l, conv_w, conv_b, w_a, b_a, w_x, b_x, lam, with_ctx):
    gb_c, a_c, u_c = rglru_inputs(p_c, conv_w, conv_b, w_a, b_a, w_x, b_x, lam)
    gb_l, a_l, u_l = rglru_inputs(p_l, conv_w, conv_b, w_a, b_a, w_x, b_x, lam)
    h0 = jnp.zeros((p_c.shape[0], D_RG), F32)
    hs_c, hs_l = [], []
    for d, rev in enumerate((False, True)):
        f = functools.partial(maybe_flip, rev=rev)
        h_c = f(lru_scan(f(a_c[:, :, d]), f(u_c[:, :, d]), h0))
        h_end = h_c[:, 0] if rev else h_c[:, -1]
        hs_c.append(h_c)
        hs_l.append(f(lru_scan(f(a_l[:, :, d]), f(u_l[:, :, d]), h_end)))
    y_l = (hs_l[0] + hs_l[1]) * jax.nn.gelu(gb_l.astype(F32))
    if not with_ctx:
        return None, y_l
    return (hs_c[0] + hs_c[1]) * jax.nn.gelu(gb_c.astype(F32)), y_l


def even_mixer(h_c, h_l, w_in, w_out, hy_conv_w, hy_conv_b, hy_filt, hy_bias, ssd_args, with_ctx):
    w_in = w_in.astype(BF16)
    w_out = w_out.astype(BF16)
    p_c = _mm3(h_c, w_in)
    p_l = _mm3(h_l, w_in)
    s_c, s_l = ssd_mixer(p_c[..., HY_IN:], p_l[..., HY_IN:], *ssd_args, with_ctx)
    y_l = _mm3(jnp.concatenate([hyena_mixer(p_l[..., :HY_IN], hy_conv_w, hy_conv_b, hy_filt, hy_bias), s_l],
                               axis=-1), w_out)
    if not with_ctx:
        return None, y_l
    y_c = _mm3(jnp.concatenate([hyena_mixer(p_c[..., :HY_IN], hy_conv_w, hy_conv_b, hy_filt, hy_bias), s_c],
                               axis=-1), w_out)
    return y_c, y_l


def odd_mixer(h_c, h_l, w_in, w_out, gla_args, rg_args, with_ctx):
    w_in = w_in.astype(BF16)
    w_out = w_out.astype(BF16)
    p_c = _mm3(h_c, w_in)
    p_l = _mm3(to_col_major(h_l), w_in)
    a_c, a_l = gla_mixer(p_c[..., :GLA_IN], p_l[..., :GLA_IN], *gla_args, with_ctx)
    r_c, r_l = rglru_mixer(p_c[..., GLA_IN:], p_l[..., GLA_IN:], *rg_args, with_ctx)
    y_l = _mm3(from_col_major(jnp.concatenate([a_l, r_l], axis=-1)), w_out)
    if not with_ctx:
        return None, y_l
    return _mm3(jnp.concatenate([a_c, r_c], axis=-1), w_out), y_l


def moe_ffn(h, router_w, router_b, w_gate, w_up, w_down, sh_gate, sh_up, sh_down):
    n, d = h.shape
    scores = jax.nn.sigmoid(pmatmul(h, router_w, exact=True))
    _, idx = lax.top_k(scores + router_b.astype(F32), MOE_TOPK)
    wsel = jnp.take_along_axis(scores, idx, axis=1)
    wsel = wsel / jnp.sum(wsel, axis=1, keepdims=True) * MOE_SCALE
    flat_e = idx.reshape(-1)
    nk = flat_e.shape[0]
    order = jnp.argsort(flat_e)
    sorted_e = flat_e[order]
    counts = jnp.bincount(flat_e, length=MOE_EXPERTS)
    padded = (counts + MOE_BLOCK - 1) // MOE_BLOCK * MOE_BLOCK
    ends = jnp.cumsum(padded)
    dest = (ends - padded)[sorted_e] + jnp.arange(nk) - (jnp.cumsum(counts) - counts)[sorted_e]
    n_blocks = -(-nk // MOE_BLOCK) + MOE_EXPERTS
    rows = n_blocks * MOE_BLOCK
    row_tok = jnp.full((rows,), n, jnp.int32).at[dest].set((order // MOE_TOPK).astype(jnp.int32))
    row_w = jnp.zeros((rows,), F32).at[dest].set(wsel.reshape(-1)[order])
    block_e = jnp.minimum(jnp.searchsorted(ends, jnp.arange(n_blocks) * MOE_BLOCK, side='right'),
                          MOE_EXPERTS - 1).astype(jnp.int32)
    hb = h.astype(BF16)
    h_pad = jnp.concatenate([hb, jnp.zeros((1, d), BF16)], axis=0)
    x_rows = h_pad[row_tok]
    y_rows = moe_experts(x_rows, block_e, row_w, w_gate.astype(BF16), w_up.astype(BF16), w_down.astype(BF16))
    routed = jax.ops.segment_sum(y_rows, row_tok, num_segments=n + 1)[:n]
    shared = shared_expert(hb, sh_gate.astype(BF16), sh_up.astype(BF16), sh_down.astype(BF16))
    return routed + shared


def kernel(x, c, ctx, c_ctx, ada_w, ada_b, norm1_w, norm2_w, ev_w_in, ev_w_out, hy_conv_w, hy_conv_b, hy_fw0, hy_fb0, hy_fw1, hy_fb1, hy_fw2, hy_fb2, hy_fw3, hy_freq, hy_bias, ssd_conv_w, ssd_conv_b, ssd_dt_bias, ssd_a_log, ssd_d, ssd_norm_w, od_w_in, od_w_out, gla_gate_w, gla_gate_b, gla_norm_w, rg_conv_w, rg_conv_b, rg_w_a, rg_b_a, rg_w_x, rg_b_x, rg_lambda, router_w, router_b, moe_w_gate, moe_w_up, moe_w_down, sh_w_gate, sh_w_up, sh_w_down, final_norm_w):
    for i in range(DEPTH):
        last = i == DEPTH - 1
        j = i // 2
        sh1, sc1, g1, sh2, sc2, g2 = [m[:, None, :] for m in adaln(c, ada_w[i], ada_b[i])]
        csh1, csc1, cg1, csh2, csc2, cg2 = adaln(c_ctx, ada_w[i], ada_b[i])
        h_l = modulate(rms_norm(x, norm1_w[i]), sh1, sc1)
        h_c = modulate(rms_norm(ctx, norm1_w[i]), csh1, csc1)
        if i % 2 == 0:
            hy_filt = (hy_fw0[j], hy_fb0[j], hy_fw1[j], hy_fb1[j], hy_fw2[j], hy_fb2[j], hy_fw3[j], hy_freq[j])
            ssd_args = (ssd_conv_w[j], ssd_conv_b[j], ssd_dt_bias[j], ssd_a_log[j], ssd_d[j], ssd_norm_w[j])
            mix_c, mix_l = even_mixer(h_c, h_l, ev_w_in[j], ev_w_out[j], hy_conv_w[j], hy_conv_b[j],
                                      hy_filt, hy_bias[j], ssd_args, not last)
        else:
            gla_args = (gla_gate_w[j], gla_gate_b[j], gla_norm_w[j])
            rg_args = (rg_conv_w[j], rg_conv_b[j], rg_w_a[j], rg_b_a[j], rg_w_x[j], rg_b_x[j], rg_lambda[j])
            mix_c, mix_l = odd_mixer(h_c, h_l, od_w_in[j], od_w_out[j], gla_args, rg_args, not last)
        x = x + g1 * mix_l
        moe_w = (router_w[i], router_b[i], moe_w_gate[i], moe_w_up[i], moe_w_down[i],
                 sh_w_gate[i], sh_w_up[i], sh_w_down[i])
        h_l = modulate(rms_norm(x, norm2_w[i]), sh2, sc2)
        if last:
            x = x + g2 * moe_ffn(h_l.reshape(-1, D_MODEL), *moe_w).reshape(h_l.shape)
        else:
            ctx = ctx + cg1 * mix_c
            h_c = modulate(rms_norm(ctx, norm2_w[i]), csh2, csc2)
            h_all = jnp.concatenate([h_c, h_l], axis=1)
            y = moe_ffn(h_all.reshape(-1, D_MODEL), *moe_w).reshape(h_all.shape)
            n_ctx = ctx.shape[1]
            ctx = ctx + cg2 * y[:, :n_ctx]
            x = x + g2 * y[:, n_ctx:]
    return rms_norm(x, final_norm_w)
```

```python
import functools
import math

import jax
import jax.numpy as jnp
from jax import lax
from jax.experimental import pallas as pl
from jax.experimental.pallas import tpu as pltpu

D_MODEL = 1024
BATCH = 16
SEQ = 2048
DEPTH = 2

CTX_LEN = 256
GRID_W = 64
NORM_EPS = 1e-6

D_HY = D_MODEL // 2
HY_ORDER = 2
HY_SHORT = 3
HY_BANDS = 16
HY_EMB = 1 + 2 * HY_BANDS
HY_FF = 64
HY_DECAY_PCT_LO = 0.3
HY_DECAY_PCT_HI = 1.5
HY_DECAY_TARGET = 1e-2
HY_IN = 3 * D_HY

D_SSM = D_MODEL // 2
SSD_P = 64
SSD_H = D_SSM // SSD_P
SSD_G = 2
SSD_HPG = SSD_H // SSD_G
SSD_N = 128
SSD_CONV = 4
SSD_CHUNK = 128
SSD_XBC = D_SSM + 2 * SSD_G * SSD_N
SSD_IN = D_SSM + SSD_XBC + 2 * SSD_H
EV_IN = HY_IN + SSD_IN
EV_MIX = D_HY + D_SSM

GLA_H = 4
GLA_DV = (D_MODEL // 2) // GLA_H
GLA_DK = GLA_DV // 2
GLA_RANK = 16
GLA_GATE_NORM = 16.0
GLA_CHUNK = 64
GLA_IN = 2 * GLA_H * GLA_DK + 2 * GLA_H * GLA_DV + 2 * GLA_RANK

D_RG = D_MODEL // 2
RG_BLOCKS = 8
RG_BW = D_RG // RG_BLOCKS
RG_CONV = 4
RG_C = 8.0
RG_IN = 2 * D_RG
OD_IN = GLA_IN + RG_IN
OD_MIX = GLA_H * GLA_DV + D_RG

MOE_EXPERTS = 64
MOE_TOPK = 8
MOE_D_EXPERT = 256
MOE_D_SHARED = 256
MOE_SCALE = 2.5
MOE_BLOCK = 256

F32 = jnp.float32
BF16 = jnp.bfloat16
HIGHEST = lax.Precision.HIGHEST

NTOK = SEQ + CTX_LEN
ROW_TILE = 256
N_ROW_TILES = NTOK // ROW_TILE
N_LAT_TILES = SEQ // ROW_TILE

VMEM_LIMIT = 48 * 1024 * 1024
VMEM_LIMIT_BIG = 56 * 1024 * 1024


def _cparams(sem, limit=VMEM_LIMIT):
    return pltpu.CompilerParams(dimension_semantics=sem, vmem_limit_bytes=limit)


def _pick_tile(n, pref):
    t = min(n, pref)
    while n % t:
        t //= 2
    return t


def _silu(x):
    return x / (1.0 + jnp.exp(-x))


def _softplus(x):
    return jnp.maximum(x, 0.0) + jnp.log(1.0 + jnp.exp(-jnp.abs(x)))


def _mm_bf16_body(a_ref, w_ref, o_ref):
    o_ref[...] = jnp.dot(a_ref[...].astype(BF16), w_ref[...].astype(BF16),
                         preferred_element_type=F32).astype(o_ref.dtype)


def _mm_f32_body(a_ref, w_ref, o_ref):
    o_ref[...] = jnp.dot(a_ref[...], w_ref[...], preferred_element_type=F32,
                         precision=HIGHEST).astype(o_ref.dtype)


def pmatmul(a, w, *, exact=False, out_dtype=F32, tm=512, tn=None):
    m, k = a.shape
    n = w.shape[1]
    tm = _pick_tile(m, tm)
    tn = n if tn is None else _pick_tile(n, tn)
    body = _mm_f32_body if exact else _mm_bf16_body
    return pl.pallas_call(
        body,
        grid=(m // tm, n // tn),
        in_specs=[pl.BlockSpec((tm, k), lambda i, j: (i, 0)),
                  pl.BlockSpec((k, tn), lambda i, j: (0, j))],
        out_specs=pl.BlockSpec((tm, tn), lambda i, j: (i, j)),
        out_shape=jax.ShapeDtypeStruct((m, n), out_dtype),
        compiler_params=_cparams(("parallel", "parallel")),
    )(a, w)


def _mm_split_body(a_ref, w_ref, *o_refs, splits):
    a = a_ref[...]
    for o_ref, (start, width) in zip(o_refs, splits):
        o_ref[...] = jnp.dot(a, w_ref[:, start:start + width],
                             preferred_element_type=F32).astype(o_ref.dtype)


def mm_split(a, w, splits, dtypes, tm=512):
    m, k = a.shape
    n = w.shape[1]
    tm = _pick_tile(m, tm)
    return pl.pallas_call(
        functools.partial(_mm_split_body, splits=tuple(splits)),
        grid=(m // tm,),
        in_specs=[pl.BlockSpec((tm, k), lambda i: (i, 0)),
                  pl.BlockSpec((k, n), lambda i: (0, 0))],
        out_specs=[pl.BlockSpec((tm, wd), lambda i: (i, 0)) for _, wd in splits],
        out_shape=[jax.ShapeDtypeStruct((m, wd), dt) for (_, wd), dt in zip(splits, dtypes)],
        compiler_params=_cparams(("parallel",)),
    )(a, w)


def _mm_resid_body(*refs, n_pairs):
    a_refs = refs[:n_pairs]
    w_refs = refs[n_pairs:2 * n_pairs]
    x_ref, g_ref, o_ref = refs[2 * n_pairs:]
    acc = jnp.dot(a_refs[0][0], w_refs[0][...], preferred_element_type=F32)
    for a_ref, w_ref in zip(a_refs[1:], w_refs[1:]):
        acc = acc + jnp.dot(a_ref[0], w_ref[...], preferred_element_type=F32)
    o_ref[0] = x_ref[0] + g_ref[...] * acc


def mm_resid(a_list, w_list, xs, mods, gate_idx, n_tiles):
    b, nt, d = xs.shape
    n_pairs = len(a_list)
    in_specs = [pl.BlockSpec((1, ROW_TILE, a.shape[-1]), lambda i, j: (i, j, 0)) for a in a_list]
    in_specs += [pl.BlockSpec(w.shape, lambda i, j: (0, 0)) for w in w_list]
    in_specs += [pl.BlockSpec((1, ROW_TILE, d), lambda i, j: (i, j, 0)),
                 _mod_spec(gate_idx, d)]
    return pl.pallas_call(
        functools.partial(_mm_resid_body, n_pairs=n_pairs),
        grid=(b, n_tiles),
        in_specs=in_specs,
        out_specs=pl.BlockSpec((1, ROW_TILE, d), lambda i, j: (i, j, 0)),
        out_shape=jax.ShapeDtypeStruct(xs.shape, F32),
        input_output_aliases={2 * n_pairs: 0},
        compiler_params=_cparams(("parallel", "parallel")),
    )(*a_list, *w_list, xs, mods)


def _mod_spec(idx, d):
    return pl.BlockSpec((None, None, None, 1, d), lambda i, j: (i, 1 - j // N_LAT_TILES, idx, 0, 0))


def adaln_table(c, c_ctx, w, b):
    cv = jax.nn.silu(jnp.concatenate([c, c_ctx[None, :]], axis=0))
    cv = jnp.pad(cv, ((0, 24 - cv.shape[0]), (0, 0)))
    m = pmatmul(cv, w, exact=True, tn=1536)[:BATCH + 1] + b
    per_sample = m[:BATCH]
    ctx_row = jnp.broadcast_to(m[BATCH][None, :], per_sample.shape)
    return jnp.stack([ctx_row, per_sample], axis=1).reshape(BATCH, 2, 6, 1, D_MODEL)


def _norm_mod(x, w, shift, scale):
    ms = jnp.mean(x * x, axis=-1, keepdims=True)
    return (x * lax.rsqrt(ms + NORM_EPS) * w) * (1.0 + scale) + shift


def _norm_mod_body(x_ref, w_ref, sh_ref, sc_ref, o_ref):
    o_ref[0] = _norm_mod(x_ref[0], w_ref[...], sh_ref[...], sc_ref[...]).astype(o_ref.dtype)


def norm_mod(xs, w, mods, shift_idx, scale_idx):
    b, nt, d = xs.shape
    return pl.pallas_call(
        _norm_mod_body,
        grid=(b, nt // ROW_TILE),
        in_specs=[pl.BlockSpec((1, ROW_TILE, d), lambda i, j: (i, j, 0)),
                  pl.BlockSpec((1, d), lambda i, j: (0, 0)),
                  _mod_spec(shift_idx, d), _mod_spec(scale_idx, d)],
        out_specs=pl.BlockSpec((1, ROW_TILE, d), lambda i, j: (i, j, 0)),
        out_shape=jax.ShapeDtypeStruct(xs.shape, BF16),
        compiler_params=_cparams(("parallel", "parallel")),
    )(xs, w.reshape(1, d), mods, mods)


def _final_norm_body(x_ref, w_ref, o_ref):
    x = x_ref[0]
    ms = jnp.mean(x * x, axis=-1, keepdims=True)
    o_ref[0] = x * lax.rsqrt(ms + NORM_EPS) * w_ref[...]


def final_norm(xs, w):
    b, _, d = xs.shape
    return pl.pallas_call(
        _final_norm_body,
        grid=(b, N_LAT_TILES),
        in_specs=[pl.BlockSpec((1, ROW_TILE, d), lambda i, j: (i, j, 0)),
                  pl.BlockSpec((1, d), lambda i, j: (0, 0))],
        out_specs=pl.BlockSpec((1, ROW_TILE, d), lambda i, j: (i, j, 0)),
        out_shape=jax.ShapeDtypeStruct((b, SEQ, d), F32),
        compiler_params=_cparams(("parallel", "parallel")),
    )(xs, w.reshape(1, d))


def _route_body(x_ref, w_ref, sh_ref, sc_ref, rw_ref, rb_ref, h_ref, idx_ref, wsel_ref, rank_ref, cnt_ref):
    first = (pl.program_id(0) == 0) & (pl.program_id(1) == 0)

    @pl.when(first)
    def _():
        cnt_ref[...] = jnp.zeros_like(cnt_ref)

    h = _norm_mod(x_ref[0], w_ref[...], sh_ref[...], sc_ref[...])
    h_ref[0] = h.astype(h_ref.dtype)
    logits = jnp.dot(h, rw_ref[...], preferred_element_type=F32, precision=HIGHEST)
    scores = 1.0 / (1.0 + jnp.exp(-logits))
    tm, ne = scores.shape
    lane = lax.broadcasted_iota(jnp.int32, (tm, ne), 1).astype(F32)
    slot = lax.broadcasted_iota(jnp.int32, (tm, MOE_TOPK), 1)
    sel = scores + rb_ref[...]
    picked = jnp.zeros((tm, ne), F32)
    hits = []
    idx_out = jnp.zeros((tm, MOE_TOPK), F32)
    w_out = jnp.zeros((tm, MOE_TOPK), F32)
    for k in range(MOE_TOPK):
        m = jnp.max(sel, axis=-1, keepdims=True)
        ik = jnp.min(jnp.where(sel == m, lane, float(ne)), axis=-1, keepdims=True)
        hit = lane == ik
        wk = jnp.sum(jnp.where(hit, scores, 0.0), axis=-1, keepdims=True)
        sel = jnp.where(hit, -jnp.inf, sel)
        picked = picked + hit.astype(F32)
        hits.append(hit)
        idx_out = jnp.where(slot == k, ik, idx_out)
        w_out = jnp.where(slot == k, wk, w_out)
    wsum = jnp.sum(w_out, axis=-1, keepdims=True)
    wsel_ref[0] = w_out / wsum * MOE_SCALE
    idx_ref[0] = idx_out.astype(jnp.int32)
    r_i = lax.broadcasted_iota(jnp.int32, (tm, tm), 0)
    c_i = lax.broadcasted_iota(jnp.int32, (tm, tm), 1)
    strict_lower = (c_i < r_i).astype(BF16)
    before = jnp.dot(strict_lower, picked.astype(BF16), preferred_element_type=F32) + cnt_ref[...]
    rank_out = jnp.zeros((tm, MOE_TOPK), F32)
    for k in range(MOE_TOPK):
        rk = jnp.sum(jnp.where(hits[k], before, 0.0), axis=-1, keepdims=True)
        rank_out = jnp.where(slot == k, rk, rank_out)
    rank_ref[0] = rank_out.astype(jnp.int32)
    cnt_ref[...] = cnt_ref[...] + jnp.sum(picked, axis=0, keepdims=True)


def route(xs, w, mods, router_w, router_b, n_tiles):
    b, nt, d = xs.shape
    rows = n_tiles * ROW_TILE
    small = lambda dt: jax.ShapeDtypeStruct((b, rows, MOE_TOPK), dt)
    small_spec = pl.BlockSpec((1, ROW_TILE, MOE_TOPK), lambda i, j: (i, j, 0))
    return pl.pallas_call(
        _route_body,
        grid=(b, n_tiles),
        in_specs=[pl.BlockSpec((1, ROW_TILE, d), lambda i, j: (i, j, 0)),
                  pl.BlockSpec((1, d), lambda i, j: (0, 0)),
                  _mod_spec(3, d), _mod_spec(4, d),
                  pl.BlockSpec((d, MOE_EXPERTS), lambda i, j: (0, 0)),
                  pl.BlockSpec((1, MOE_EXPERTS), lambda i, j: (0, 0))],
        out_specs=[pl.BlockSpec((1, ROW_TILE, d), lambda i, j: (i, j, 0)),
                   small_spec, small_spec, small_spec,
                   pl.BlockSpec((1, MOE_EXPERTS), lambda i, j: (0, 0))],
        out_shape=[jax.ShapeDtypeStruct((b, rows, d), BF16), small(jnp.int32), small(F32), small(jnp.int32),
                   jax.ShapeDtypeStruct((1, MOE_EXPERTS), F32)],
        compiler_params=_cparams(("arbitrary", "arbitrary")),
    )(xs, w.reshape(1, d), mods, mods, router_w, router_b.reshape(1, MOE_EXPERTS))


def _swiglu(x, wg, wu, wd):
    g = jnp.dot(x, wg, preferred_element_type=F32)
    u = jnp.dot(x, wu, preferred_element_type=F32)
    h = (_silu(g) * u).astype(BF16)
    return jnp.dot(h, wd, preferred_element_type=F32)


def _expert_body(be_ref, x_ref, wg_ref, wu_ref, wd_ref, o_ref):
    del be_ref
    o_ref[...] = _swiglu(x_ref[...], wg_ref[0], wu_ref[0], wd_ref[0])


def moe_experts(x_rows, block_e, wg, wu, wd):
    rows, d = x_rows.shape
    n_blocks = rows // MOE_BLOCK
    f = wg.shape[-1]
    grid_spec = pltpu.PrefetchScalarGridSpec(
        num_scalar_prefetch=1,
        grid=(n_blocks,),
        in_specs=[
            pl.BlockSpec((MOE_BLOCK, d), lambda i, be: (i, 0)),
            pl.BlockSpec((1, d, f), lambda i, be: (be[i], 0, 0)),
            pl.BlockSpec((1, d, f), lambda i, be: (be[i], 0, 0)),
            pl.BlockSpec((1, f, d), lambda i, be: (be[i], 0, 0)),
        ],
        out_specs=pl.BlockSpec((MOE_BLOCK, d), lambda i, be: (i, 0)),
    )
    return pl.pallas_call(
        _expert_body,
        grid_spec=grid_spec,
        out_shape=jax.ShapeDtypeStruct((rows, d), F32),
        compiler_params=_cparams(("arbitrary",)),
    )(block_e, x_rows, wg, wu, wd)


def _shared_resid_body(h_ref, wg_ref, wu_ref, wd_ref, r_ref, x_ref, g_ref, o_ref):
    y = _swiglu(h_ref[0], wg_ref[...], wu_ref[...], wd_ref[...]) + r_ref[0]
    o_ref[0] = x_ref[0] + g_ref[...] * y


def shared_resid(h, routed, xs, mods, wg, wu, wd, n_tiles):
    b, _, d = xs.shape
    f = wg.shape[-1]
    tile = pl.BlockSpec((1, ROW_TILE, d), lambda i, j: (i, j, 0))
    return pl.pallas_call(
        _shared_resid_body,
        grid=(b, n_tiles),
        in_specs=[tile,
                  pl.BlockSpec((d, f), lambda i, j: (0, 0)),
                  pl.BlockSpec((d, f), lambda i, j: (0, 0)),
                  pl.BlockSpec((f, d), lambda i, j: (0, 0)),
                  tile, tile, _mod_spec(5, d)],
        out_specs=tile,
        out_shape=jax.ShapeDtypeStruct(xs.shape, F32),
        input_output_aliases={5: 0},
        compiler_params=_cparams(("parallel", "parallel")),
    )(h, wg, wu, wd, routed, xs, mods)


def moe_layer(xs, norm_w, mods, router_w, router_b, w_gate, w_up, w_down, sh_gate, sh_up, sh_down, n_tiles):
    b, _, d = xs.shape
    rows_per_sample = n_tiles * ROW_TILE
    n = b * rows_per_sample
    h, idx, wsel, rank, counts = route(xs, norm_w, mods, router_w, router_b, n_tiles)
    counts = counts[0].astype(jnp.int32)
    padded = (counts + MOE_BLOCK - 1) // MOE_BLOCK * MOE_BLOCK
    ends = jnp.cumsum(padded)
    starts = ends - padded
    firsts = jnp.cumsum(counts) - counts
    nk = n * MOE_TOPK
    n_blocks = -(-nk // MOE_BLOCK) + MOE_EXPERTS
    rows = n_blocks * MOE_BLOCK
    idx_f = idx.reshape(-1)
    dest = starts[idx_f] + rank.reshape(-1)
    tok = jnp.arange(nk, dtype=jnp.int32) // MOE_TOPK
    _, tok_sorted = lax.sort((dest, tok), num_keys=1)
    block_e = jnp.minimum(jnp.searchsorted(ends, jnp.arange(n_blocks) * MOE_BLOCK, side='right'),
                          MOE_EXPERTS - 1).astype(jnp.int32)
    r = jnp.arange(rows, dtype=jnp.int32)
    e_r = block_e[r // MOE_BLOCK]
    off = r - starts[e_r]
    valid = off < counts[e_r]
    row_tok = jnp.where(valid, tok_sorted[jnp.clip(firsts[e_r] + off, 0, nk - 1)], 0)
    h_flat = h.reshape(n, d)
    x_rows = h_flat[row_tok]
    y_rows = moe_experts(x_rows, block_e, w_gate.astype(BF16), w_up.astype(BF16), w_down.astype(BF16))
    picked = y_rows[dest].reshape(n, MOE_TOPK, d)
    routed = jnp.einsum('tk,tkd->td', wsel.reshape(n, MOE_TOPK), picked)
    return shared_resid(h, routed.reshape(b, rows_per_sample, d), xs, mods,
                        sh_gate.astype(BF16), sh_up.astype(BF16), sh_down.astype(BF16), n_tiles)


def _dwconv_body(x_ref, w_ref, b_ref, o_ref, *, width, act):
    chunk = ROW_TILE
    n_chunks = NTOK // chunk
    first_of_seq = (0, N_LAT_TILES)
    last_of_seq = (N_LAT_TILES - 1, n_chunks - 1)
    tc = x_ref.shape[-1]
    halo = 16
    row = lax.broadcasted_iota(jnp.int32, (chunk, tc), 0)
    zero_row = jnp.zeros((1, tc), F32)
    for c in range(n_chunks):
        r0 = c * chunk
        cur = x_ref[0, r0:r0 + chunk, :].astype(F32)
        if c in first_of_seq:
            prev_last = zero_row
        else:
            prev_last = x_ref[0, r0 - halo:r0, :].astype(F32)[halo - 1:halo, :]
        if c in last_of_seq:
            next0 = next1 = zero_row
        else:
            nxt = x_ref[0, r0 + chunk:r0 + chunk + halo, :].astype(F32)
            next0, next1 = nxt[0:1, :], nxt[1:2, :]
        xm1 = jnp.where(row == 0, prev_last, pltpu.roll(cur, 1, 0))
        xp1 = jnp.where(row == chunk - 1, next0, pltpu.roll(cur, chunk - 1, 0))
        y = w_ref[0:1, :] * xm1 + w_ref[1:2, :] * cur + w_ref[2:3, :] * xp1 + b_ref[...]
        if width == 4:
            xp2 = jnp.where(row == chunk - 2, next0,
                            jnp.where(row == chunk - 1, next1, pltpu.roll(cur, chunk - 2, 0)))
            y = y + w_ref[3:4, :] * xp2
        if act:
            y = _silu(y)
        o_ref[0, c * chunk:(c + 1) * chunk, :] = y.astype(o_ref.dtype)


def dwconv_stream(x, w, b, act, tc=256):
    bsz, nt, c = x.shape
    width = w.shape[0]
    return pl.pallas_call(
        functools.partial(_dwconv_body, width=width, act=act),
        grid=(bsz, c // tc),
        in_specs=[pl.BlockSpec((1, nt, tc), lambda i, j: (i, 0, j)),
                  pl.BlockSpec((width, tc), lambda i, j: (0, j)),
                  pl.BlockSpec((1, tc), lambda i, j: (0, j))],
        out_specs=pl.BlockSpec((1, nt, tc), lambda i, j: (i, 0, j)),
        out_shape=jax.ShapeDtypeStruct(x.shape, BF16),
        compiler_params=_cparams(("parallel", "parallel")),
    )(x, w, b.reshape(1, c))


HY_FB = 512


def dft_matrices(n):
    k = jnp.arange(n, dtype=jnp.int32)[:, None]
    t = jnp.arange(n, dtype=jnp.int32)[None, :]
    ang = (2.0 * math.pi / (2 * n)) * ((k * t) % (2 * n)).astype(F32)
    fre = jnp.cos(ang)
    fim = -jnp.sin(ang)
    nyq = jnp.where(t % 2 == 0, 1.0, -1.0).astype(F32)
    fim = jnp.where(k == 0, nyq, fim)
    fwd = jnp.concatenate([fre, fim], axis=0)
    colscale = jnp.where(jnp.arange(2 * n) % n == 0, 0.5, 1.0) / n
    inv = fwd.T * colscale[None, :]
    return fwd, inv


def hyena_filter_taps(n, fw0, fb0, fw1, fb1, fw2, fb2, fw3, freq):
    pos = jnp.arange(n, dtype=F32)
    t = pos / max(n - 1, 1)
    bands = jnp.linspace(1e-4, HY_BANDS - 1, HY_BANDS, dtype=F32)
    ang = (2.0 * math.pi / n) * pos[:, None] * bands[None, :]
    feats = jnp.concatenate([t[:, None], jnp.cos(ang), -jnp.sin(ang)], axis=-1)
    h = jnp.sin(freq * (jnp.dot(feats, fw0, precision=HIGHEST) + fb0))
    h = jnp.sin(freq * (jnp.dot(h, fw1, precision=HIGHEST) + fb1))
    h = jnp.sin(freq * (jnp.dot(h, fw2, precision=HIGHEST) + fb2))
    h = pmatmul(h, fw3, exact=True).reshape(n, 2, HY_ORDER, D_HY)
    deltas = jnp.abs(jnp.linspace(math.log(HY_DECAY_PCT_LO) / HY_DECAY_TARGET,
                                  math.log(HY_DECAY_PCT_HI) / HY_DECAY_TARGET, D_HY, dtype=F32))
    h = h * jnp.exp(-t[:, None] * deltas)[:, None, None, :]
    h0 = h[:, 0]
    h1 = h[:, 1].at[0].set(0.0)
    norm = jnp.sum(jnp.abs(h0), axis=0, keepdims=True) + jnp.sum(jnp.abs(h1), axis=0, keepdims=True)
    h0 = (h0 / norm).reshape(n, HY_ORDER * D_HY)
    h1 = (h1 / norm).reshape(n, HY_ORDER * D_HY)
    return h0 + h1, h0 - h1


def _split_bf16(a):
    hi = a.astype(BF16)
    return hi, (a - hi.astype(F32)).astype(BF16)


def hyena_spectrum(fwd, hsum, hdiff, fb):
    n = hsum.shape[0]
    f_hi, f_lo = _split_bf16(fwd)

    def dft(h):
        h_hi, h_lo = _split_bf16(h)
        return pmatmul(f_hi, h_hi) + pmatmul(f_hi, h_lo) + pmatmul(f_lo, h_hi)

    a = dft(hsum)
    bm = dft(hdiff)
    sr = a[:n]
    si = bm[n:]
    nyq = a[n]
    first = (jnp.arange(n) == 0)[:, None]
    p = sr
    q = jnp.where(first, 0.0, si)
    s = jnp.where(first, nyq[None, :], sr)
    spec = jnp.stack([p, q, s], axis=0).reshape(3, n // fb, fb, HY_ORDER, D_HY)
    return spec.transpose(3, 1, 0, 2, 4), f_hi


def _hyena_body(u_ref, fre_ref, fim_ref, gre_ref, gim_ref, sp_ref, bias_ref, prev_ref, o_ref,
                vin, acc, *, nf):
    del prev_ref
    o = pl.program_id(1)
    f = pl.program_id(2)
    c = D_HY

    @pl.when((o == 0) & (f == 0))
    def _():
        vin[...] = u_ref[0, :, 0:c]

    @pl.when(f == 0)
    def _():
        acc[...] = jnp.zeros_like(acc)

    v = vin[...]
    vr = jnp.dot(fre_ref[...], v, preferred_element_type=F32)
    vi = jnp.dot(fim_ref[...], v, preferred_element_type=F32)
    p, q, s = sp_ref[0], sp_ref[1], sp_ref[2]
    zr = (vr * p - vi * q).astype(BF16)
    zi = (vr * q + vi * s).astype(BF16)
    acc[...] += (jnp.dot(gre_ref[...], zr, preferred_element_type=F32)
                 + jnp.dot(gim_ref[...], zi, preferred_element_type=F32))

    @pl.when((o == 0) & (f == nf - 1))
    def _():
        z = u_ref[0, :, c:2 * c].astype(F32) * (acc[...] + bias_ref[0:1, :] * vin[...].astype(F32))
        vin[...] = z.astype(BF16)

    @pl.when((o == 1) & (f == nf - 1))
    def _():
        y = u_ref[0, :, 2 * c:3 * c].astype(F32) * (acc[...] + bias_ref[1:2, :] * vin[...].astype(F32))
        o_ref[0] = y.astype(o_ref.dtype)


def hyena_long_conv(u, fwd_bf16, inv_bf16, spec, bias, n, row_block, prev_out):
    bsz = u.shape[0]
    fb = spec.shape[3]
    nf = n // fb
    out_shape = jax.ShapeDtypeStruct((bsz, NTOK, D_HY), BF16)
    args = [u, fwd_bf16, fwd_bf16, inv_bf16, inv_bf16, spec, bias]
    aliases = {}
    if prev_out is None:
        prev_out = jnp.zeros((8, 128), BF16)
    else:
        aliases = {7: 0}
    args.append(prev_out)
    return pl.pallas_call(
        functools.partial(_hyena_body, nf=nf),
        grid=(bsz, HY_ORDER, nf),
        in_specs=[pl.BlockSpec((1, n, 3 * D_HY), lambda b, o, f: (b, row_block, 0)),
                  pl.BlockSpec((fb, n), lambda b, o, f: (f, 0)),
                  pl.BlockSpec((fb, n), lambda b, o, f: (nf + f, 0)),
                  pl.BlockSpec((n, fb), lambda b, o, f: (0, f)),
                  pl.BlockSpec((n, fb), lambda b, o, f: (0, nf + f)),
                  pl.BlockSpec((None, None, 3, fb, D_HY), lambda b, o, f: (o, f, 0, 0, 0)),
                  pl.BlockSpec((HY_ORDER, D_HY), lambda b, o, f: (0, 0)),
                  pl.BlockSpec(memory_space=pl.ANY)],
        out_specs=pl.BlockSpec((1, n, D_HY), lambda b, o, f: (b, row_block, 0)),
        out_shape=out_shape,
        scratch_shapes=[pltpu.VMEM((n, D_HY), BF16), pltpu.VMEM((n, D_HY), F32)],
        input_output_aliases=aliases,
        compiler_params=_cparams(("parallel", "arbitrary", "arbitrary"), VMEM_LIMIT_BIG),
    )(*args)


def hyena_mixer_stream(p_hy, conv_w, conv_b, filt, bias):
    u = dwconv_stream(p_hy, conv_w, conv_b, act=False)
    out = None
    for n, row_block in ((SEQ, 0), (CTX_LEN, SEQ // CTX_LEN)):
        fb = min(HY_FB, n)
        fwd, inv = dft_matrices(n)
        hsum, hdiff = hyena_filter_taps(n, *filt)
        spec, fwd_bf16 = hyena_spectrum(fwd, hsum, hdiff, fb)
        out = hyena_long_conv(u, fwd_bf16, inv.astype(BF16), spec, bias, n, row_block, out)
    return out


def _tri(n, kind):
    r = lax.broadcasted_iota(jnp.int32, (n, n), 0)
    c = lax.broadcasted_iota(jnp.int32, (n, n), 1)
    return (c <= r) if kind == 'lower' else (c >= r)


def _ssd_dir(xbc_ref, dt_ref, dtt_ref, bias_r, bias_c, a_r, a_c, st_ref, y_ref, *, d, reverse):
    q = SSD_CHUNK
    nh = SSD_H
    gw = SSD_HPG * SSD_P
    lower = _tri(q, 'lower')
    upper = _tri(q, 'upper')
    lower_f = lower.astype(F32)
    upper_f = upper.astype(F32)
    dt_col = _softplus(dt_ref[0] + bias_r)
    dt_row = _softplus(dtt_ref[0] + bias_c)
    da_col = dt_col * a_r
    da_row = dt_row * a_c
    if not reverse:
        acs_col = jnp.dot(lower_f, da_col, preferred_element_type=F32, precision=HIGHEST)
        acs_row = jnp.dot(da_row, upper_f, preferred_element_type=F32, precision=HIGHEST)
        mask = lower
        edge = q - 1
    else:
        acs_col = jnp.dot(upper_f, da_col, preferred_element_type=F32, precision=HIGHEST)
        acs_row = jnp.dot(da_row, lower_f, preferred_element_type=F32, precision=HIGHEST)
        mask = upper
        edge = 0
    h0 = d * nh
    hh = lax.broadcasted_iota(jnp.int32, (2 * nh, nh * SSD_P), 0)
    cc = lax.broadcasted_iota(jnp.int32, (2 * nh, nh * SSD_P), 1) // SSD_P
    expand = (hh == cc + h0).astype(F32)
    acs_c = jnp.dot(acs_col, expand, preferred_element_type=F32, precision=HIGHEST)
    dt_c = jnp.dot(dt_col, expand, preferred_element_type=F32, precision=HIGHEST)
    total_c = acs_c[edge:edge + 1, :]
    e_in_c = jnp.exp(acs_c)
    w_end_c = jnp.exp(total_c - acs_c) * dt_c
    dec_c = jnp.exp(total_c)
    xs = xbc_ref[0, :, 0:D_SSM]
    xs_f = xs.astype(F32)
    for g in range(SSD_G):
        bm = xbc_ref[0, :, D_SSM + g * SSD_N:D_SSM + (g + 1) * SSD_N]
        cm = xbc_ref[0, :, D_SSM + SSD_G * SSD_N + g * SSD_N:D_SSM + SSD_G * SSD_N + (g + 1) * SSD_N]
        cb = lax.dot_general(cm, bm, (((1,), (1,)), ((), ())), preferred_element_type=F32)
        lws = []
        for k in range(SSD_HPG):
            h = h0 + g * SSD_HPG + k
            seg = acs_col[:, h:h + 1] - acs_row[h:h + 1, :]
            decay = jnp.exp(jnp.where(mask, seg, -jnp.inf))
            lws.append((cb * decay * dt_row[h:h + 1, :]).astype(BF16))
        lw = jnp.concatenate(lws, axis=1)
        xg = xs[:, g * gw:(g + 1) * gw]
        rb = lax.broadcasted_iota(jnp.int32, (SSD_HPG * q, gw), 0) // q
        cbk = lax.broadcasted_iota(jnp.int32, (SSD_HPG * q, gw), 1) // SSD_P
        x_bd = jnp.where(rb == cbk, jnp.concatenate([xg] * SSD_HPG, axis=0), jnp.zeros((), BF16))
        y_in = jnp.dot(lw, x_bd, preferred_element_type=F32)
        st = st_ref[g]
        y_st = jnp.dot(cm, st.astype(BF16), preferred_element_type=F32) * e_in_c[:, g * gw:(g + 1) * gw]
        y_ref[0, :, g * gw:(g + 1) * gw] = y_in + y_st
        xw = (xs_f[:, g * gw:(g + 1) * gw] * w_end_c[:, g * gw:(g + 1) * gw]).astype(BF16)
        upd = lax.dot_general(bm, xw, (((0,), (0,)), ((), ())), preferred_element_type=F32)
        st_ref[g] = st * dec_c[:, g * gw:(g + 1) * gw] + upd


def _ssd_body(xf_ref, dtf_ref, dttf_ref, xb_ref, dtb_ref, dttb_ref, bias_r, bias_c, a_r, a_c,
              yf_ref, yb_ref, stf, stb):
    @pl.when(pl.program_id(1) == 0)
    def _():
        stf[...] = jnp.zeros_like(stf)
        stb[...] = jnp.zeros_like(stb)

    _ssd_dir(xf_ref, dtf_ref, dttf_ref, bias_r[...], bias_c[...], a_r[...], a_c[...], stf, yf_ref,
             d=0, reverse=False)
    _ssd_dir(xb_ref, dtb_ref, dttb_ref, bias_r[...], bias_c[...], a_r[...], a_c[...], stb, yb_ref,
             d=1, reverse=True)


def ssd_scan(xbc, dt, dt_bias, a_log):
    bsz = xbc.shape[0]
    nc = NTOK // SSD_CHUNK
    nlat = SEQ // SSD_CHUNK
    dtt = jnp.swapaxes(dt, 1, 2)
    fwd_chunk = lambda s: (s + nlat) % nc
    bwd_chunk = lambda s: nc - 1 - s
    a = -jnp.exp(a_log.astype(F32)).reshape(1, 2 * SSD_H)
    bias = dt_bias.astype(F32).reshape(1, 2 * SSD_H)
    x_spec = lambda cm: pl.BlockSpec((1, SSD_CHUNK, SSD_XBC), lambda b, s: (b, cm(s), 0))
    dt_spec = lambda cm: pl.BlockSpec((1, SSD_CHUNK, 2 * SSD_H), lambda b, s: (b, cm(s), 0))
    dtt_spec = lambda cm: pl.BlockSpec((1, 2 * SSD_H, SSD_CHUNK), lambda b, s: (b, 0, cm(s)))
    y_spec = lambda cm: pl.BlockSpec((1, SSD_CHUNK, D_SSM), lambda b, s: (b, cm(s), 0))
    row = pl.BlockSpec((1, 2 * SSD_H), lambda b, s: (0, 0))
    col = pl.BlockSpec((2 * SSD_H, 1), lambda b, s: (0, 0))
    y_shape = jax.ShapeDtypeStruct((bsz, NTOK, D_SSM), F32)
    gw = SSD_HPG * SSD_P
    return pl.pallas_call(
        _ssd_body,
        grid=(bsz, nc),
        in_specs=[x_spec(fwd_chunk), dt_spec(fwd_chunk), dtt_spec(fwd_chunk),
                  x_spec(bwd_chunk), dt_spec(bwd_chunk), dtt_spec(bwd_chunk),
                  row, col, row, col],
        out_specs=[y_spec(fwd_chunk), y_spec(bwd_chunk)],
        out_shape=[y_shape, y_shape],
        scratch_shapes=[pltpu.VMEM((SSD_G, SSD_N, gw), F32), pltpu.VMEM((SSD_G, SSD_N, gw), F32)],
        compiler_params=_cparams(("parallel", "arbitrary")),
    )(xbc, dt, dtt, xbc, dt, dtt, bias, bias.reshape(-1, 1), a, a.reshape(-1, 1))


def _ssd_merge_body(yf_ref, yb_ref, xbc_ref, z_ref, d_ref, nw_ref, o_ref):
    xs = xbc_ref[0, :, 0:D_SSM].astype(F32)
    z = z_ref[0].astype(F32)
    g = (yf_ref[0] + yb_ref[0] + d_ref[...] * xs) * _silu(z)
    gw = D_SSM // SSD_G
    for k in range(SSD_G):
        gk = g[:, k * gw:(k + 1) * gw]
        ms = jnp.mean(gk * gk, axis=-1, keepdims=True)
        o_ref[0, :, k * gw:(k + 1) * gw] = (gk * lax.rsqrt(ms + NORM_EPS)
                                            * nw_ref[:, k * gw:(k + 1) * gw]).astype(o_ref.dtype)


def ssd_merge(yf, yb, xbc, z, d_skip, norm_w):
    bsz = yf.shape[0]
    tile = lambda w: pl.BlockSpec((1, ROW_TILE, w), lambda i, j: (i, j, 0))
    vec = pl.BlockSpec((1, D_SSM), lambda i, j: (0, 0))
    d_chan = jnp.repeat(d_skip.astype(F32), SSD_P).reshape(1, D_SSM)
    return pl.pallas_call(
        _ssd_merge_body,
        grid=(bsz, N_ROW_TILES),
        in_specs=[tile(D_SSM), tile(D_SSM), tile(SSD_XBC), tile(D_SSM), vec, vec],
        out_specs=tile(D_SSM),
        out_shape=jax.ShapeDtypeStruct((bsz, NTOK, D_SSM), BF16),
        compiler_params=_cparams(("parallel", "parallel")),
    )(yf, yb, xbc, z, d_chan, norm_w.reshape(1, D_SSM))


def even_layer_mixer(xs, mods, norm_w, w_in, w_out, hy_conv_w, hy_conv_b, hy_filt, hy_bias,
                     ssd_conv_w, ssd_conv_b, ssd_dt_bias, ssd_a_log, ssd_d, ssd_norm_w):
    bsz = xs.shape[0]
    h = norm_mod(xs, norm_w, mods, 0, 1)
    splits = ((0, HY_IN), (HY_IN, D_SSM), (HY_IN + D_SSM, SSD_XBC), (HY_IN + D_SSM + SSD_XBC, 2 * SSD_H))
    p_hy, z, xbc_raw, dt = mm_split(h.reshape(bsz * NTOK, D_MODEL), w_in.astype(BF16), splits,
                                    (BF16, BF16, BF16, F32))
    to3 = lambda a: a.reshape(bsz, NTOK, a.shape[-1])
    y_hy = hyena_mixer_stream(to3(p_hy), hy_conv_w, hy_conv_b, hy_filt, hy_bias)
    xbc = dwconv_stream(to3(xbc_raw), ssd_conv_w, ssd_conv_b, act=True)
    yf, yb = ssd_scan(xbc, to3(dt), ssd_dt_bias, ssd_a_log)
    s = ssd_merge(yf, yb, xbc, to3(z), ssd_d, ssd_norm_w)
    wo = w_out.astype(BF16)
    return mm_resid([y_hy, s], [wo[:D_HY], wo[D_HY:]], xs, mods, 2, N_ROW_TILES)


def rms_norm(x, w):
    xf = x.astype(F32)
    y = xf * lax.rsqrt(jnp.mean(jnp.square(xf), axis=-1, keepdims=True) + NORM_EPS)
    return y.astype(x.dtype) * w


def dwconv(x, w, b):
    y = lax.conv_general_dilated(x, w[:, None, :].astype(x.dtype), window_strides=(1,), padding='SAME',
                                 dimension_numbers=('NWC', 'WIO', 'NWC'), feature_group_count=x.shape[-1])
    return y + b.astype(x.dtype)


def maybe_flip(a, rev):
    return jnp.flip(a, axis=1) if rev else a


def to_col_major(x):
    b, n, d = x.shape
    rows = n // GRID_W
    return x.reshape(b, rows, GRID_W, d).transpose(0, 2, 1, 3).reshape(b, n, d)


def from_col_major(x):
    b, n, d = x.shape
    rows = n // GRID_W
    return x.reshape(b, GRID_W, rows, d).transpose(0, 2, 1, 3).reshape(b, n, d)


def gla_inputs(p, gate_w, gate_b):
    b, n, _ = p.shape
    nqk, nv = GLA_H * GLA_DK, GLA_H * GLA_DV
    q = p[..., :nqk].reshape(b, n, GLA_H, GLA_DK) * GLA_DK ** -0.5
    k = p[..., nqk:2 * nqk].reshape(b, n, GLA_H, GLA_DK)
    v = p[..., 2 * nqk:2 * nqk + nv].reshape(b, n, GLA_H, GLA_DV)
    lr = p[..., 2 * nqk + nv:2 * nqk + nv + 2 * GLA_RANK].reshape(b, n, 2, GLA_RANK)
    r = p[..., 2 * nqk + nv + 2 * GLA_RANK:]
    logit = jnp.einsum('bler,erk->blek', lr, gate_w) + gate_b
    log_g = (jax.nn.log_sigmoid(logit.astype(F32)) / GLA_GATE_NORM).reshape(b, n, 2, GLA_H, GLA_DK)
    return q, k, v, log_g, r


def gla_states(k, v, log_g, s0):
    b, n = k.shape[:2]
    nc = n // GLA_CHUNK
    kc = k.reshape(b, nc, GLA_CHUNK, GLA_H, GLA_DK)
    vc = v.reshape(b, nc, GLA_CHUNK, GLA_H, GLA_DV)
    gcum = jnp.cumsum(log_g.reshape(b, nc, GLA_CHUNK, GLA_H, GLA_DK), axis=2)
    states = jnp.einsum('bcqhd,bcqhv->bchdv', kc * jnp.exp(gcum[:, :, -1:] - gcum), vc)
    chunk_decay = jnp.exp(gcum[:, :, -1])

    def step(s, inp):
        st, dcy = inp
        return dcy[..., None] * s + st, s

    s_fin, s_prev = lax.scan(step, s0, (jnp.moveaxis(states, 1, 0), jnp.moveaxis(chunk_decay, 1, 0)))
    return jnp.moveaxis(s_prev, 0, 1), s_fin


def gla_output(q, k, v, log_g, s_prev):
    b, n = q.shape[:2]
    nc = n // GLA_CHUNK
    qc = q.reshape(b, nc, GLA_CHUNK, GLA_H, GLA_DK)
    kc = k.reshape(b, nc, GLA_CHUNK, GLA_H, GLA_DK)
    vc = v.reshape(b, nc, GLA_CHUNK, GLA_H, GLA_DV)
    gcum = jnp.cumsum(log_g.reshape(b, nc, GLA_CHUNK, GLA_H, GLA_DK), axis=2)
    qg = qc * jnp.exp(gcum)
    kg = kc * jnp.exp(-gcum)
    mask = jnp.tril(jnp.ones((GLA_CHUNK, GLA_CHUNK), bool))
    att = jnp.where(mask, jnp.einsum('bcihd,bcjhd->bchij', qg, kg), 0.0)
    o = jnp.einsum('bchij,bcjhv->bcihv', att, vc) + jnp.einsum('bcihd,bchdv->bcihv', qg, s_prev)
    return o.reshape(b, n, GLA_H, GLA_DV)


def gla_merge(os_, r, norm_w):
    b, n = r.shape[:2]
    o = rms_norm(os_[0] + os_[1], norm_w.reshape(GLA_H, GLA_DV))
    return o.reshape(b, n, GLA_H * GLA_DV) * jax.nn.silu(r)


def gla_mixer(p_c, p_l, gate_w, gate_b, norm_w):
    q_c, k_c, v_c, g_c, r_c = gla_inputs(p_c, gate_w, gate_b)
    q_l, k_l, v_l, g_l, r_l = gla_inputs(p_l, gate_w, gate_b)
    s0 = jnp.zeros((p_c.shape[0], GLA_H, GLA_DK, GLA_DV), F32)
    os_l = []
    for d, rev in enumerate((False, True)):
        f = functools.partial(maybe_flip, rev=rev)
        _, sf_c = gla_states(f(k_c), f(v_c), f(g_c[:, :, d]), s0)
        sp_l, _ = gla_states(f(k_l), f(v_l), f(g_l[:, :, d]), sf_c)
        os_l.append(f(gla_output(f(q_l), f(k_l), f(v_l), f(g_l[:, :, d]), sp_l)))
    return gla_merge(os_l, r_l, norm_w)


def rglru_inputs(p, conv_w, conv_b, w_a, b_a, w_x, b_x, lam):
    b, n, _ = p.shape
    u = dwconv(p[..., :D_RG], conv_w, conv_b)
    ub = u.reshape(b, n, RG_BLOCKS, RG_BW)
    r = jax.nn.sigmoid((jnp.einsum('blni,enio->bleno', ub, w_a).reshape(b, n, 2, D_RG) + b_a).astype(F32))
    i = jax.nn.sigmoid((jnp.einsum('blni,enio->bleno', ub, w_x).reshape(b, n, 2, D_RG) + b_x).astype(F32))
    log_a = -RG_C * jax.nn.softplus(-lam.astype(F32)) * r
    x_in = jnp.sqrt(-jnp.expm1(2.0 * log_a)) * i * u[:, :, None, :].astype(F32)
    return p[..., D_RG:], jnp.exp(log_a), x_in


def lru_scan(a, u, h0):
    u = u.at[:, 0].add(a[:, 0] * h0)

    def combine(lhs, rhs):
        a1, b1 = lhs
        a2, b2 = rhs
        return a1 * a2, a2 * b1 + b2

    return lax.associative_scan(combine, (a, u), axis=1)[1]


def rglru_mixer(p_c, p_l, conv_w, conv_b, w_a, b_a, w_x, b_x, lam):
    gb_c, a_c, u_c = rglru_inputs(p_c, conv_w, conv_b, w_a, b_a, w_x, b_x, lam)
    gb_l, a_l, u_l = rglru_inputs(p_l, conv_w, conv_b, w_a, b_a, w_x, b_x, lam)
    h0 = jnp.zeros((p_c.shape[0], D_RG), F32)
    hs_l = []
    for d, rev in enumerate((False, True)):
        f = functools.partial(maybe_flip, rev=rev)
        h_c = f(lru_scan(f(a_c[:, :, d]), f(u_c[:, :, d]), h0))
        h_end = h_c[:, 0] if rev else h_c[:, -1]
        hs_l.append(f(lru_scan(f(a_l[:, :, d]), f(u_l[:, :, d]), h_end)))
    return (hs_l[0] + hs_l[1]) * jax.nn.gelu(gb_l.astype(F32))


def odd_layer_mixer(xs, mods, norm_w, w_in, w_out, gla_args, rg_args):
    bsz = xs.shape[0]
    h = norm_mod(xs, norm_w, mods, 0, 1)
    h = jnp.concatenate([to_col_major(h[:, :SEQ]), h[:, SEQ:]], axis=1)
    p = pmatmul(h.reshape(bsz * NTOK, D_MODEL), w_in.astype(BF16)).reshape(bsz, NTOK, OD_IN)
    p_l, p_c = p[:, :SEQ], p[:, SEQ:]
    a_l = gla_mixer(p_c[..., :GLA_IN], p_l[..., :GLA_IN], *gla_args)
    r_l = rglru_mixer(p_c[..., GLA_IN:], p_l[..., GLA_IN:], *rg_args)
    mix = from_col_major(jnp.concatenate([a_l, r_l], axis=-1)).astype(BF16)
    return mm_resid([mix], [w_out.astype(BF16)], xs, mods, 2, N_LAT_TILES)


def kernel(x, c, ctx, c_ctx, ada_w, ada_b, norm1_w, norm2_w, ev_w_in, ev_w_out, hy_conv_w, hy_conv_b, hy_fw0, hy_fb0, hy_fw1, hy_fb1, hy_fw2, hy_fb2, hy_fw3, hy_freq, hy_bias, ssd_conv_w, ssd_conv_b, ssd_dt_bias, ssd_a_log, ssd_d, ssd_norm_w, od_w_in, od_w_out, gla_gate_w, gla_gate_b, gla_norm_w, rg_conv_w, rg_conv_b, rg_w_a, rg_b_a, rg_w_x, rg_b_x, rg_lambda, router_w, router_b, moe_w_gate, moe_w_up, moe_w_down, sh_w_gate, sh_w_up, sh_w_down, final_norm_w):
    xs = jnp.concatenate([x, ctx], axis=1)
    for i in range(DEPTH):
        last = i == DEPTH - 1
        j = i // 2
        mods = adaln_table(c, c_ctx, ada_w[i], ada_b[i])
        if i % 2 == 0:
            hy_filt = (hy_fw0[j], hy_fb0[j], hy_fw1[j], hy_fb1[j], hy_fw2[j], hy_fb2[j], hy_fw3[j], hy_freq[j])
            xs = even_layer_mixer(xs, mods, norm1_w[i], ev_w_in[j], ev_w_out[j], hy_conv_w[j], hy_conv_b[j],
                                  hy_filt, hy_bias[j], ssd_conv_w[j], ssd_conv_b[j], ssd_dt_bias[j],
                                  ssd_a_log[j], ssd_d[j], ssd_norm_w[j])
        else:
            gla_args = (gla_gate_w[j], gla_gate_b[j], gla_norm_w[j])
            rg_args = (rg_conv_w[j], rg_conv_b[j], rg_w_a[j], rg_b_a[j], rg_w_x[j], rg_b_x[j], rg_lambda[j])
            xs = odd_layer_mixer(xs, mods, norm1_w[i], od_w_in[j], od_w_out[j], gla_args, rg_args)
        n_tiles = N_LAT_TILES if last else N_ROW_TILES
        xs = moe_layer(xs, norm2_w[i], mods, router_w[i], router_b[i], moe_w_gate[i], moe_w_up[i],
                       moe_w_down[i], sh_w_gate[i], sh_w_up[i], sh_w_down[i], n_tiles)
    return final_norm(xs, final_norm_w)
```

```python
import functools
import math

import jax
import jax.numpy as jnp
from jax import lax
from jax.experimental import pallas as pl
from jax.experimental.pallas import tpu as pltpu

D_MODEL = 1024
BATCH = 16
SEQ = 2048
DEPTH = 2

CTX_LEN = 256
GRID_W = 64
NORM_EPS = 1e-6

D_HY = D_MODEL // 2
HY_ORDER = 2
HY_SHORT = 3
HY_BANDS = 16
HY_EMB = 1 + 2 * HY_BANDS
HY_FF = 64
HY_DECAY_PCT_LO = 0.3
HY_DECAY_PCT_HI = 1.5
HY_DECAY_TARGET = 1e-2
HY_IN = 3 * D_HY

D_SSM = D_MODEL // 2
SSD_P = 64
SSD_H = D_SSM // SSD_P
SSD_G = 2
SSD_HPG = SSD_H // SSD_G
SSD_N = 128
SSD_CONV = 4
SSD_CHUNK = 128
SSD_XBC = D_SSM + 2 * SSD_G * SSD_N
SSD_IN = D_SSM + SSD_XBC + 2 * SSD_H
EV_IN = HY_IN + SSD_IN
EV_MIX = D_HY + D_SSM

GLA_H = 4
GLA_DV = (D_MODEL // 2) // GLA_H
GLA_DK = GLA_DV // 2
GLA_RANK = 16
GLA_GATE_NORM = 16.0
GLA_CHUNK = 64
GLA_IN = 2 * GLA_H * GLA_DK + 2 * GLA_H * GLA_DV + 2 * GLA_RANK

D_RG = D_MODEL // 2
RG_BLOCKS = 8
RG_BW = D_RG // RG_BLOCKS
RG_CONV = 4
RG_C = 8.0
RG_IN = 2 * D_RG
OD_IN = GLA_IN + RG_IN
OD_MIX = GLA_H * GLA_DV + D_RG

MOE_EXPERTS = 64
MOE_TOPK = 8
MOE_D_EXPERT = 256
MOE_D_SHARED = 256
MOE_SCALE = 2.5
MOE_BLOCK = 256

F32 = jnp.float32
BF16 = jnp.bfloat16
HIGHEST = lax.Precision.HIGHEST

NTOK = SEQ + CTX_LEN
ROW_TILE = 256
N_ROW_TILES = NTOK // ROW_TILE
N_LAT_TILES = SEQ // ROW_TILE

VMEM_LIMIT = 48 * 1024 * 1024
VMEM_LIMIT_BIG = 56 * 1024 * 1024


def _cparams(sem, limit=VMEM_LIMIT):
    return pltpu.CompilerParams(dimension_semantics=sem, vmem_limit_bytes=limit)


def _pick_tile(n, pref):
    t = min(n, pref)
    while n % t:
        t //= 2
    return t


def _silu(x):
    return x / (1.0 + jnp.exp(-x))


def _softplus(x):
    return jnp.maximum(x, 0.0) + jnp.log(1.0 + jnp.exp(-jnp.abs(x)))


def _mm_bf16_body(a_ref, w_ref, o_ref):
    o_ref[...] = jnp.dot(a_ref[...].astype(BF16), w_ref[...].astype(BF16),
                         preferred_element_type=F32).astype(o_ref.dtype)


def _mm_f32_body(a_ref, w_ref, o_ref):
    o_ref[...] = jnp.dot(a_ref[...], w_ref[...], preferred_element_type=F32,
                         precision=HIGHEST).astype(o_ref.dtype)


def pmatmul(a, w, *, exact=False, out_dtype=F32, tm=512, tn=None):
    m, k = a.shape
    n = w.shape[1]
    tm = _pick_tile(m, tm)
    tn = n if tn is None else _pick_tile(n, tn)
    body = _mm_f32_body if exact else _mm_bf16_body
    return pl.pallas_call(
        body,
        name='mm',
        grid=(m // tm, n // tn),
        in_specs=[pl.BlockSpec((tm, k), lambda i, j: (i, 0)),
                  pl.BlockSpec((k, tn), lambda i, j: (0, j))],
        out_specs=pl.BlockSpec((tm, tn), lambda i, j: (i, j)),
        out_shape=jax.ShapeDtypeStruct((m, n), out_dtype),
        compiler_params=_cparams(("parallel", "parallel")),
    )(a, w)


def _mm_split_body(a_ref, w_ref, *o_refs, splits):
    a = a_ref[...]
    for o_ref, (start, width) in zip(o_refs, splits):
        o_ref[...] = jnp.dot(a, w_ref[:, start:start + width],
                             preferred_element_type=F32).astype(o_ref.dtype)


def mm_split(a, w, splits, dtypes, tm=512):
    m, k = a.shape
    n = w.shape[1]
    tm = _pick_tile(m, tm)
    return pl.pallas_call(
        functools.partial(_mm_split_body, splits=tuple(splits)),
        name='mm_split',
        grid=(m // tm,),
        in_specs=[pl.BlockSpec((tm, k), lambda i: (i, 0)),
                  pl.BlockSpec((k, n), lambda i: (0, 0))],
        out_specs=[pl.BlockSpec((tm, wd), lambda i: (i, 0)) for _, wd in splits],
        out_shape=[jax.ShapeDtypeStruct((m, wd), dt) for (_, wd), dt in zip(splits, dtypes)],
        compiler_params=_cparams(("parallel",)),
    )(a, w)


def _mm_resid_body(*refs, n_pairs):
    a_refs = refs[:n_pairs]
    w_refs = refs[n_pairs:2 * n_pairs]
    x_ref, g_ref, o_ref = refs[2 * n_pairs:]
    acc = jnp.dot(a_refs[0][0], w_refs[0][...], preferred_element_type=F32)
    for a_ref, w_ref in zip(a_refs[1:], w_refs[1:]):
        acc = acc + jnp.dot(a_ref[0], w_ref[...], preferred_element_type=F32)
    o_ref[0] = x_ref[0] + g_ref[...] * acc


def mm_resid(a_list, w_list, xs, mods, gate_idx, n_tiles):
    b, nt, d = xs.shape
    n_pairs = len(a_list)
    in_specs = [pl.BlockSpec((1, ROW_TILE, a.shape[-1]), lambda i, j: (i, j, 0)) for a in a_list]
    in_specs += [pl.BlockSpec(w.shape, lambda i, j: (0, 0)) for w in w_list]
    in_specs += [pl.BlockSpec((1, ROW_TILE, d), lambda i, j: (i, j, 0)),
                 _mod_spec(gate_idx, d)]
    return pl.pallas_call(
        functools.partial(_mm_resid_body, n_pairs=n_pairs),
        name='mm_resid',
        grid=(b, n_tiles),
        in_specs=in_specs,
        out_specs=pl.BlockSpec((1, ROW_TILE, d), lambda i, j: (i, j, 0)),
        out_shape=jax.ShapeDtypeStruct(xs.shape, F32),
        input_output_aliases={2 * n_pairs: 0},
        compiler_params=_cparams(("parallel", "parallel")),
    )(*a_list, *w_list, xs, mods)


def _mod_spec(idx, d):
    return pl.BlockSpec((None, None, None, 1, d), lambda i, j: (i, 1 - j // N_LAT_TILES, idx, 0, 0))


def adaln_table(c, c_ctx, w, b):
    cv = jax.nn.silu(jnp.concatenate([c, c_ctx[None, :]], axis=0))
    cv = jnp.pad(cv, ((0, 24 - cv.shape[0]), (0, 0)))
    m = pmatmul(cv, w, exact=True, tn=1536)[:BATCH + 1] + b
    per_sample = m[:BATCH]
    ctx_row = jnp.broadcast_to(m[BATCH][None, :], per_sample.shape)
    return jnp.stack([ctx_row, per_sample], axis=1).reshape(BATCH, 2, 6, 1, D_MODEL)


def _norm_mod(x, w, shift, scale):
    ms = jnp.mean(x * x, axis=-1, keepdims=True)
    return (x * lax.rsqrt(ms + NORM_EPS) * w) * (1.0 + scale) + shift


def _norm_mod_body(x_ref, w_ref, sh_ref, sc_ref, o_ref):
    o_ref[0] = _norm_mod(x_ref[0], w_ref[...], sh_ref[...], sc_ref[...]).astype(o_ref.dtype)


def norm_mod(xs, w, mods, shift_idx, scale_idx):
    b, nt, d = xs.shape
    return pl.pallas_call(
        _norm_mod_body,
        name='norm_mod',
        grid=(b, nt // ROW_TILE),
        in_specs=[pl.BlockSpec((1, ROW_TILE, d), lambda i, j: (i, j, 0)),
                  pl.BlockSpec((1, d), lambda i, j: (0, 0)),
                  _mod_spec(shift_idx, d), _mod_spec(scale_idx, d)],
        out_specs=pl.BlockSpec((1, ROW_TILE, d), lambda i, j: (i, j, 0)),
        out_shape=jax.ShapeDtypeStruct(xs.shape, BF16),
        compiler_params=_cparams(("parallel", "parallel")),
    )(xs, w.reshape(1, d), mods, mods)


def _final_norm_body(x_ref, w_ref, o_ref):
    x = x_ref[0]
    ms = jnp.mean(x * x, axis=-1, keepdims=True)
    o_ref[0] = x * lax.rsqrt(ms + NORM_EPS) * w_ref[...]


def final_norm(xs, w):
    b, _, d = xs.shape
    return pl.pallas_call(
        _final_norm_body,
        name='final_norm',
        grid=(b, N_LAT_TILES),
        in_specs=[pl.BlockSpec((1, ROW_TILE, d), lambda i, j: (i, j, 0)),
                  pl.BlockSpec((1, d), lambda i, j: (0, 0))],
        out_specs=pl.BlockSpec((1, ROW_TILE, d), lambda i, j: (i, j, 0)),
        out_shape=jax.ShapeDtypeStruct((b, SEQ, d), F32),
        compiler_params=_cparams(("parallel", "parallel")),
    )(xs, w.reshape(1, d))


def _route_body(x_ref, w_ref, sh_ref, sc_ref, rw_ref, rb_ref, h_ref, idx_ref, wsel_ref, rank_ref, cnt_ref):
    first = (pl.program_id(0) == 0) & (pl.program_id(1) == 0)

    @pl.when(first)
    def _():
        cnt_ref[...] = jnp.zeros_like(cnt_ref)

    h = _norm_mod(x_ref[0], w_ref[...], sh_ref[...], sc_ref[...])
    h_ref[0] = h.astype(h_ref.dtype)
    logits = jnp.dot(h, rw_ref[...], preferred_element_type=F32, precision=HIGHEST)
    scores = 1.0 / (1.0 + jnp.exp(-logits))
    tm, ne = scores.shape
    lane = lax.broadcasted_iota(jnp.int32, (tm, ne), 1).astype(F32)
    slot = lax.broadcasted_iota(jnp.int32, (tm, MOE_TOPK), 1)
    sel = scores + rb_ref[...]
    picked = jnp.zeros((tm, ne), F32)
    hits = []
    idx_out = jnp.zeros((tm, MOE_TOPK), F32)
    w_out = jnp.zeros((tm, MOE_TOPK), F32)
    for k in range(MOE_TOPK):
        m = jnp.max(sel, axis=-1, keepdims=True)
        ik = jnp.min(jnp.where(sel == m, lane, float(ne)), axis=-1, keepdims=True)
        hit = lane == ik
        wk = jnp.sum(jnp.where(hit, scores, 0.0), axis=-1, keepdims=True)
        sel = jnp.where(hit, -jnp.inf, sel)
        picked = picked + hit.astype(F32)
        hits.append(hit)
        idx_out = jnp.where(slot == k, ik, idx_out)
        w_out = jnp.where(slot == k, wk, w_out)
    wsum = jnp.sum(w_out, axis=-1, keepdims=True)
    wsel_ref[0] = w_out / wsum * MOE_SCALE
    idx_ref[0] = idx_out.astype(jnp.int32)
    r_i = lax.broadcasted_iota(jnp.int32, (tm, tm), 0)
    c_i = lax.broadcasted_iota(jnp.int32, (tm, tm), 1)
    strict_lower = (c_i < r_i).astype(BF16)
    before = jnp.dot(strict_lower, picked.astype(BF16), preferred_element_type=F32) + cnt_ref[...]
    rank_out = jnp.zeros((tm, MOE_TOPK), F32)
    for k in range(MOE_TOPK):
        rk = jnp.sum(jnp.where(hits[k], before, 0.0), axis=-1, keepdims=True)
        rank_out = jnp.where(slot == k, rk, rank_out)
    rank_ref[0] = rank_out.astype(jnp.int32)
    cnt_ref[...] = cnt_ref[...] + jnp.sum(picked, axis=0, keepdims=True)


def route(xs, w, mods, router_w, router_b, n_tiles):
    b, nt, d = xs.shape
    rows = n_tiles * ROW_TILE
    small = lambda dt: jax.ShapeDtypeStruct((b, rows, MOE_TOPK), dt)
    small_spec = pl.BlockSpec((1, ROW_TILE, MOE_TOPK), lambda i, j: (i, j, 0))
    return pl.pallas_call(
        _route_body,
        name='route',
        grid=(b, n_tiles),
        in_specs=[pl.BlockSpec((1, ROW_TILE, d), lambda i, j: (i, j, 0)),
                  pl.BlockSpec((1, d), lambda i, j: (0, 0)),
                  _mod_spec(3, d), _mod_spec(4, d),
                  pl.BlockSpec((d, MOE_EXPERTS), lambda i, j: (0, 0)),
                  pl.BlockSpec((1, MOE_EXPERTS), lambda i, j: (0, 0))],
        out_specs=[pl.BlockSpec((1, ROW_TILE, d), lambda i, j: (i, j, 0)),
                   small_spec, small_spec, small_spec,
                   pl.BlockSpec((1, MOE_EXPERTS), lambda i, j: (0, 0))],
        out_shape=[jax.ShapeDtypeStruct((b, rows, d), BF16), small(jnp.int32), small(F32), small(jnp.int32),
                   jax.ShapeDtypeStruct((1, MOE_EXPERTS), F32)],
        compiler_params=_cparams(("arbitrary", "arbitrary")),
    )(xs, w.reshape(1, d), mods, mods, router_w, router_b.reshape(1, MOE_EXPERTS))


def _swiglu(x, wg, wu, wd):
    g = jnp.dot(x, wg, preferred_element_type=F32)
    u = jnp.dot(x, wu, preferred_element_type=F32)
    h = (_silu(g) * u).astype(BF16)
    return jnp.dot(h, wd, preferred_element_type=F32)


def _expert_body(be_ref, x_ref, wg_ref, wu_ref, wd_ref, o_ref):
    del be_ref
    o_ref[...] = _swiglu(x_ref[...], wg_ref[0], wu_ref[0], wd_ref[0]).astype(o_ref.dtype)


def moe_experts(x_rows, block_e, wg, wu, wd):
    rows, d = x_rows.shape
    n_blocks = rows // MOE_BLOCK
    f = wg.shape[-1]
    grid_spec = pltpu.PrefetchScalarGridSpec(
        num_scalar_prefetch=1,
        grid=(n_blocks,),
        in_specs=[
            pl.BlockSpec((MOE_BLOCK, d), lambda i, be: (i, 0)),
            pl.BlockSpec((1, d, f), lambda i, be: (be[i], 0, 0)),
            pl.BlockSpec((1, d, f), lambda i, be: (be[i], 0, 0)),
            pl.BlockSpec((1, f, d), lambda i, be: (be[i], 0, 0)),
        ],
        out_specs=pl.BlockSpec((MOE_BLOCK, d), lambda i, be: (i, 0)),
    )
    return pl.pallas_call(
        _expert_body,
        name='experts',
        grid_spec=grid_spec,
        out_shape=jax.ShapeDtypeStruct((rows, d), BF16),
        compiler_params=_cparams(("arbitrary",)),
    )(block_e, x_rows, wg, wu, wd)


def _shared_resid_body(h_ref, wg_ref, wu_ref, wd_ref, pk_ref, ws_ref, x_ref, g_ref, o_ref):
    y = _swiglu(h_ref[0], wg_ref[...], wu_ref[...], wd_ref[...])
    ws = ws_ref[0]
    for k in range(MOE_TOPK):
        y = y + ws[:, k:k + 1] * pk_ref[k, 0].astype(F32)
    o_ref[0] = x_ref[0] + g_ref[...] * y


def shared_resid(h, picked, wsel, xs, mods, wg, wu, wd, n_tiles):
    b, _, d = xs.shape
    f = wg.shape[-1]
    tile = pl.BlockSpec((1, ROW_TILE, d), lambda i, j: (i, j, 0))
    return pl.pallas_call(
        _shared_resid_body,
        name='shared_resid',
        grid=(b, n_tiles),
        in_specs=[tile,
                  pl.BlockSpec((d, f), lambda i, j: (0, 0)),
                  pl.BlockSpec((d, f), lambda i, j: (0, 0)),
                  pl.BlockSpec((f, d), lambda i, j: (0, 0)),
                  pl.BlockSpec((MOE_TOPK, 1, ROW_TILE, d), lambda i, j: (0, i, j, 0)),
                  pl.BlockSpec((1, ROW_TILE, MOE_TOPK), lambda i, j: (i, j, 0)),
                  tile, _mod_spec(5, d)],
        out_specs=tile,
        out_shape=jax.ShapeDtypeStruct(xs.shape, F32),
        input_output_aliases={6: 0},
        compiler_params=_cparams(("parallel", "parallel")),
    )(h, wg, wu, wd, picked, wsel, xs, mods)


def moe_layer(xs, norm_w, mods, router_w, router_b, w_gate, w_up, w_down, sh_gate, sh_up, sh_down, n_tiles):
    b, _, d = xs.shape
    rows_per_sample = n_tiles * ROW_TILE
    n = b * rows_per_sample
    h, idx, wsel, rank, counts = route(xs, norm_w, mods, router_w, router_b, n_tiles)
    counts = counts[0].astype(jnp.int32)
    padded = (counts + MOE_BLOCK - 1) // MOE_BLOCK * MOE_BLOCK
    ends = jnp.cumsum(padded)
    starts = ends - padded
    firsts = jnp.cumsum(counts) - counts
    nk = n * MOE_TOPK
    n_blocks = -(-nk // MOE_BLOCK) + MOE_EXPERTS
    rows = n_blocks * MOE_BLOCK
    idx_f = idx.reshape(-1)
    dest = starts[idx_f] + rank.reshape(-1)
    tok = jnp.arange(nk, dtype=jnp.int32) // MOE_TOPK
    _, tok_sorted = lax.sort((dest, tok), num_keys=1)
    block_e = jnp.minimum(jnp.searchsorted(ends, jnp.arange(n_blocks) * MOE_BLOCK, side='right'),
                          MOE_EXPERTS - 1).astype(jnp.int32)
    r = jnp.arange(rows, dtype=jnp.int32)
    e_r = block_e[r // MOE_BLOCK]
    off = r - starts[e_r]
    valid = off < counts[e_r]
    row_tok = jnp.where(valid, tok_sorted[jnp.clip(firsts[e_r] + off, 0, nk - 1)], 0)
    h_flat = h.reshape(n, d)
    x_rows = h_flat[row_tok]
    y_rows = moe_experts(x_rows, block_e, w_gate.astype(BF16), w_up.astype(BF16), w_down.astype(BF16))
    picked = y_rows[dest.reshape(n, MOE_TOPK).T].reshape(MOE_TOPK, b, rows_per_sample, d)
    return shared_resid(h, picked, wsel, xs, mods,
                        sh_gate.astype(BF16), sh_up.astype(BF16), sh_down.astype(BF16), n_tiles)


def _dwconv_body(x_ref, w_ref, b_ref, o_ref, *, width, act):
    chunk = ROW_TILE
    n_chunks = NTOK // chunk
    first_of_seq = (0, N_LAT_TILES)
    last_of_seq = (N_LAT_TILES - 1, n_chunks - 1)
    tc = x_ref.shape[-1]
    halo = 16
    row = lax.broadcasted_iota(jnp.int32, (chunk, tc), 0)
    zero_row = jnp.zeros((1, tc), F32)
    for c in range(n_chunks):
        r0 = c * chunk
        cur = x_ref[0, r0:r0 + chunk, :].astype(F32)
        if c in first_of_seq:
            prev_last = zero_row
        else:
            prev_last = x_ref[0, r0 - halo:r0, :].astype(F32)[halo - 1:halo, :]
        if c in last_of_seq:
            next0 = next1 = zero_row
        else:
            nxt = x_ref[0, r0 + chunk:r0 + chunk + halo, :].astype(F32)
            next0, next1 = nxt[0:1, :], nxt[1:2, :]
        xm1 = jnp.where(row == 0, prev_last, pltpu.roll(cur, 1, 0))
        xp1 = jnp.where(row == chunk - 1, next0, pltpu.roll(cur, chunk - 1, 0))
        y = w_ref[0:1, :] * xm1 + w_ref[1:2, :] * cur + w_ref[2:3, :] * xp1 + b_ref[...]
        if width == 4:
            xp2 = jnp.where(row == chunk - 2, next0,
                            jnp.where(row == chunk - 1, next1, pltpu.roll(cur, chunk - 2, 0)))
            y = y + w_ref[3:4, :] * xp2
        if act:
            y = _silu(y)
        o_ref[0, c * chunk:(c + 1) * chunk, :] = y.astype(o_ref.dtype)


def dwconv_stream(x, w, b, act, tc=256):
    bsz, nt, c = x.shape
    width = w.shape[0]
    return pl.pallas_call(
        functools.partial(_dwconv_body, width=width, act=act),
        name='dwconv',
        grid=(bsz, c // tc),
        in_specs=[pl.BlockSpec((1, nt, tc), lambda i, j: (i, 0, j)),
                  pl.BlockSpec((width, tc), lambda i, j: (0, j)),
                  pl.BlockSpec((1, tc), lambda i, j: (0, j))],
        out_specs=pl.BlockSpec((1, nt, tc), lambda i, j: (i, 0, j)),
        out_shape=jax.ShapeDtypeStruct(x.shape, BF16),
        compiler_params=_cparams(("parallel", "parallel")),
    )(x, w, b.reshape(1, c))


HY_FB = 512


def dft_matrices(n):
    k = jnp.arange(n, dtype=jnp.int32)[:, None]
    t = jnp.arange(n, dtype=jnp.int32)[None, :]
    ang = (2.0 * math.pi / (2 * n)) * ((k * t) % (2 * n)).astype(F32)
    fre = jnp.cos(ang)
    fim = -jnp.sin(ang)
    nyq = jnp.where(t % 2 == 0, 1.0, -1.0).astype(F32)
    fim = jnp.where(k == 0, nyq, fim)
    fwd = jnp.concatenate([fre, fim], axis=0)
    colscale = jnp.where(jnp.arange(2 * n) % n == 0, 0.5, 1.0) / n
    inv = fwd.T * colscale[None, :]
    return fwd, inv


def hyena_filter_taps(n, fw0, fb0, fw1, fb1, fw2, fb2, fw3, freq):
    pos = jnp.arange(n, dtype=F32)
    t = pos / max(n - 1, 1)
    bands = jnp.linspace(1e-4, HY_BANDS - 1, HY_BANDS, dtype=F32)
    ang = (2.0 * math.pi / n) * pos[:, None] * bands[None, :]
    feats = jnp.concatenate([t[:, None], jnp.cos(ang), -jnp.sin(ang)], axis=-1)
    h = jnp.sin(freq * (jnp.dot(feats, fw0, precision=HIGHEST) + fb0))
    h = jnp.sin(freq * (jnp.dot(h, fw1, precision=HIGHEST) + fb1))
    h = jnp.sin(freq * (jnp.dot(h, fw2, precision=HIGHEST) + fb2))
    h = pmatmul(h, fw3, exact=True).reshape(n, 2, HY_ORDER, D_HY)
    deltas = jnp.abs(jnp.linspace(math.log(HY_DECAY_PCT_LO) / HY_DECAY_TARGET,
                                  math.log(HY_DECAY_PCT_HI) / HY_DECAY_TARGET, D_HY, dtype=F32))
    h = h * jnp.exp(-t[:, None] * deltas)[:, None, None, :]
    h0 = h[:, 0]
    h1 = h[:, 1].at[0].set(0.0)
    norm = jnp.sum(jnp.abs(h0), axis=0, keepdims=True) + jnp.sum(jnp.abs(h1), axis=0, keepdims=True)
    h0 = (h0 / norm).reshape(n, HY_ORDER * D_HY)
    h1 = (h1 / norm).reshape(n, HY_ORDER * D_HY)
    return h0 + h1, h0 - h1


def _split_bf16(a):
    hi = a.astype(BF16)
    return hi, (a - hi.astype(F32)).astype(BF16)


def hyena_spectrum(fwd, hsum, hdiff, fb):
    n = hsum.shape[0]
    f_hi, f_lo = _split_bf16(fwd)

    def dft(h):
        h_hi, h_lo = _split_bf16(h)
        return pmatmul(f_hi, h_hi) + pmatmul(f_hi, h_lo) + pmatmul(f_lo, h_hi)

    a = dft(hsum)
    bm = dft(hdiff)
    sr = a[:n]
    si = bm[n:]
    nyq = a[n]
    first = (jnp.arange(n) == 0)[:, None]
    p = sr
    q = jnp.where(first, 0.0, si)
    s = jnp.where(first, nyq[None, :], sr)
    spec = jnp.stack([p, q, s], axis=0).reshape(3, n // fb, fb, HY_ORDER, D_HY)
    return spec.transpose(3, 1, 0, 2, 4), f_hi


def _hyena_body(u_ref, fre_ref, fim_ref, gre_ref, gim_ref, sp_ref, bias_ref, prev_ref, o_ref,
                vin, acc, *, nf):
    del prev_ref
    o = pl.program_id(1)
    f = pl.program_id(2)
    c = D_HY

    @pl.when((o == 0) & (f == 0))
    def _():
        vin[...] = u_ref[0, :, 0:c]

    @pl.when(f == 0)
    def _():
        acc[...] = jnp.zeros_like(acc)

    v = vin[...]
    vr = jnp.dot(fre_ref[...], v, preferred_element_type=F32)
    vi = jnp.dot(fim_ref[...], v, preferred_element_type=F32)
    p, q, s = sp_ref[0], sp_ref[1], sp_ref[2]
    zr = (vr * p - vi * q).astype(BF16)
    zi = (vr * q + vi * s).astype(BF16)
    acc[...] += (jnp.dot(gre_ref[...], zr, preferred_element_type=F32)
                 + jnp.dot(gim_ref[...], zi, preferred_element_type=F32))

    @pl.when((o == 0) & (f == nf - 1))
    def _():
        z = u_ref[0, :, c:2 * c].astype(F32) * (acc[...] + bias_ref[0:1, :] * vin[...].astype(F32))
        vin[...] = z.astype(BF16)

    @pl.when((o == 1) & (f == nf - 1))
    def _():
        y = u_ref[0, :, 2 * c:3 * c].astype(F32) * (acc[...] + bias_ref[1:2, :] * vin[...].astype(F32))
        o_ref[0] = y.astype(o_ref.dtype)


def hyena_long_conv(u, fwd_bf16, inv_bf16, spec, bias, n, row_block, prev_out):
    bsz = u.shape[0]
    fb = spec.shape[3]
    nf = n // fb
    out_shape = jax.ShapeDtypeStruct((bsz, NTOK, D_HY), BF16)
    args = [u, fwd_bf16, fwd_bf16, inv_bf16, inv_bf16, spec, bias]
    aliases = {}
    if prev_out is None:
        prev_out = jnp.zeros((8, 128), BF16)
    else:
        aliases = {7: 0}
    args.append(prev_out)
    return pl.pallas_call(
        functools.partial(_hyena_body, nf=nf),
        name='hyena',
        grid=(bsz, HY_ORDER, nf),
        in_specs=[pl.BlockSpec((1, n, 3 * D_HY), lambda b, o, f: (b, row_block, 0)),
                  pl.BlockSpec((fb, n), lambda b, o, f: (f, 0)),
                  pl.BlockSpec((fb, n), lambda b, o, f: (nf + f, 0)),
                  pl.BlockSpec((n, fb), lambda b, o, f: (0, f)),
                  pl.BlockSpec((n, fb), lambda b, o, f: (0, nf + f)),
                  pl.BlockSpec((None, None, 3, fb, D_HY), lambda b, o, f: (o, f, 0, 0, 0)),
                  pl.BlockSpec((HY_ORDER, D_HY), lambda b, o, f: (0, 0)),
                  pl.BlockSpec(memory_space=pl.ANY)],
        out_specs=pl.BlockSpec((1, n, D_HY), lambda b, o, f: (b, row_block, 0)),
        out_shape=out_shape,
        scratch_shapes=[pltpu.VMEM((n, D_HY), BF16), pltpu.VMEM((n, D_HY), F32)],
        input_output_aliases=aliases,
        compiler_params=_cparams(("parallel", "arbitrary", "arbitrary"), VMEM_LIMIT_BIG),
    )(*args)


def hyena_mixer_stream(p_hy, conv_w, conv_b, filt, bias):
    u = dwconv_stream(p_hy, conv_w, conv_b, act=False)
    out = None
    for n, row_block in ((SEQ, 0), (CTX_LEN, SEQ // CTX_LEN)):
        fb = min(HY_FB, n)
        fwd, inv = dft_matrices(n)
        hsum, hdiff = hyena_filter_taps(n, *filt)
        spec, fwd_bf16 = hyena_spectrum(fwd, hsum, hdiff, fb)
        out = hyena_long_conv(u, fwd_bf16, inv.astype(BF16), spec, bias, n, row_block, out)
    return out


def _tri(n, kind):
    r = lax.broadcasted_iota(jnp.int32, (n, n), 0)
    c = lax.broadcasted_iota(jnp.int32, (n, n), 1)
    return (c <= r) if kind == 'lower' else (c >= r)


def _ssd_dir(xbc_ref, dt_ref, dtt_ref, bias_r, bias_c, a_r, a_c, st_ref, y_ref, *, d, reverse):
    q = SSD_CHUNK
    nh = SSD_H
    gw = SSD_HPG * SSD_P
    lower = _tri(q, 'lower')
    upper = _tri(q, 'upper')
    lower_f = lower.astype(F32)
    upper_f = upper.astype(F32)
    dt_col = _softplus(dt_ref[0] + bias_r)
    dt_row = _softplus(dtt_ref[0] + bias_c)
    da_col = dt_col * a_r
    da_row = dt_row * a_c
    if not reverse:
        acs_col = jnp.dot(lower_f, da_col, preferred_element_type=F32, precision=HIGHEST)
        acs_row = jnp.dot(da_row, upper_f, preferred_element_type=F32, precision=HIGHEST)
        mask = lower
        edge = q - 1
    else:
        acs_col = jnp.dot(upper_f, da_col, preferred_element_type=F32, precision=HIGHEST)
        acs_row = jnp.dot(da_row, lower_f, preferred_element_type=F32, precision=HIGHEST)
        mask = upper
        edge = 0
    h0 = d * nh
    hh = lax.broadcasted_iota(jnp.int32, (2 * nh, nh * SSD_P), 0)
    cc = lax.broadcasted_iota(jnp.int32, (2 * nh, nh * SSD_P), 1) // SSD_P
    expand = (hh == cc + h0).astype(F32)
    acs_c = jnp.dot(acs_col, expand, preferred_element_type=F32, precision=HIGHEST)
    dt_c = jnp.dot(dt_col, expand, preferred_element_type=F32, precision=HIGHEST)
    total_c = acs_c[edge:edge + 1, :]
    e_in_c = jnp.exp(acs_c)
    w_end_c = jnp.exp(total_c - acs_c) * dt_c
    dec_c = jnp.exp(total_c)
    xs = xbc_ref[0, :, 0:D_SSM]
    xs_f = xs.astype(F32)
    for g in range(SSD_G):
        bm = xbc_ref[0, :, D_SSM + g * SSD_N:D_SSM + (g + 1) * SSD_N]
        cm = xbc_ref[0, :, D_SSM + SSD_G * SSD_N + g * SSD_N:D_SSM + SSD_G * SSD_N + (g + 1) * SSD_N]
        cb = lax.dot_general(cm, bm, (((1,), (1,)), ((), ())), preferred_element_type=F32)
        lws = []
        for k in range(SSD_HPG):
            h = h0 + g * SSD_HPG + k
            seg = acs_col[:, h:h + 1] - acs_row[h:h + 1, :]
            decay = jnp.exp(jnp.where(mask, seg, -jnp.inf))
            lws.append((cb * decay * dt_row[h:h + 1, :]).astype(BF16))
        lw = jnp.concatenate(lws, axis=1)
        xg = xs[:, g * gw:(g + 1) * gw]
        rb = lax.broadcasted_iota(jnp.int32, (SSD_HPG * q, gw), 0) // q
        cbk = lax.broadcasted_iota(jnp.int32, (SSD_HPG * q, gw), 1) // SSD_P
        x_bd = jnp.where(rb == cbk, jnp.concatenate([xg] * SSD_HPG, axis=0), jnp.zeros((), BF16))
        y_in = jnp.dot(lw, x_bd, preferred_element_type=F32)
        st = st_ref[g]
        y_st = jnp.dot(cm, st.astype(BF16), preferred_element_type=F32) * e_in_c[:, g * gw:(g + 1) * gw]
        y_ref[0, :, g * gw:(g + 1) * gw] = y_in + y_st
        xw = (xs_f[:, g * gw:(g + 1) * gw] * w_end_c[:, g * gw:(g + 1) * gw]).astype(BF16)
        upd = lax.dot_general(bm, xw, (((0,), (0,)), ((), ())), preferred_element_type=F32)
        st_ref[g] = st * dec_c[:, g * gw:(g + 1) * gw] + upd


def _ssd_body(xf_ref, dtf_ref, dttf_ref, xb_ref, dtb_ref, dttb_ref, bias_r, bias_c, a_r, a_c,
              yf_ref, yb_ref, stf, stb):
    @pl.when(pl.program_id(1) == 0)
    def _():
        stf[...] = jnp.zeros_like(stf)
        stb[...] = jnp.zeros_like(stb)

    _ssd_dir(xf_ref, dtf_ref, dttf_ref, bias_r[...], bias_c[...], a_r[...], a_c[...], stf, yf_ref,
             d=0, reverse=False)
    _ssd_dir(xb_ref, dtb_ref, dttb_ref, bias_r[...], bias_c[...], a_r[...], a_c[...], stb, yb_ref,
             d=1, reverse=True)


def ssd_scan(xbc, dt, dt_bias, a_log):
    bsz = xbc.shape[0]
    nc = NTOK // SSD_CHUNK
    nlat = SEQ // SSD_CHUNK
    dtt = jnp.swapaxes(dt, 1, 2)
    fwd_chunk = lambda s: (s + nlat) % nc
    bwd_chunk = lambda s: nc - 1 - s
    a = -jnp.exp(a_log.astype(F32)).reshape(1, 2 * SSD_H)
    bias = dt_bias.astype(F32).reshape(1, 2 * SSD_H)
    x_spec = lambda cm: pl.BlockSpec((1, SSD_CHUNK, SSD_XBC), lambda b, s: (b, cm(s), 0))
    dt_spec = lambda cm: pl.BlockSpec((1, SSD_CHUNK, 2 * SSD_H), lambda b, s: (b, cm(s), 0))
    dtt_spec = lambda cm: pl.BlockSpec((1, 2 * SSD_H, SSD_CHUNK), lambda b, s: (b, 0, cm(s)))
    y_spec = lambda cm: pl.BlockSpec((1, SSD_CHUNK, D_SSM), lambda b, s: (b, cm(s), 0))
    row = pl.BlockSpec((1, 2 * SSD_H), lambda b, s: (0, 0))
    col = pl.BlockSpec((2 * SSD_H, 1), lambda b, s: (0, 0))
    y_shape = jax.ShapeDtypeStruct((bsz, NTOK, D_SSM), F32)
    gw = SSD_HPG * SSD_P
    return pl.pallas_call(
        _ssd_body,
        name='ssd_scan',
        grid=(bsz, nc),
        in_specs=[x_spec(fwd_chunk), dt_spec(fwd_chunk), dtt_spec(fwd_chunk),
                  x_spec(bwd_chunk), dt_spec(bwd_chunk), dtt_spec(bwd_chunk),
                  row, col, row, col],
        out_specs=[y_spec(fwd_chunk), y_spec(bwd_chunk)],
        out_shape=[y_shape, y_shape],
        scratch_shapes=[pltpu.VMEM((SSD_G, SSD_N, gw), F32), pltpu.VMEM((SSD_G, SSD_N, gw), F32)],
        compiler_params=_cparams(("parallel", "arbitrary")),
    )(xbc, dt, dtt, xbc, dt, dtt, bias, bias.reshape(-1, 1), a, a.reshape(-1, 1))


def _ssd_merge_body(yf_ref, yb_ref, xbc_ref, z_ref, d_ref, nw_ref, o_ref):
    xs = xbc_ref[0, :, 0:D_SSM].astype(F32)
    z = z_ref[0].astype(F32)
    g = (yf_ref[0] + yb_ref[0] + d_ref[...] * xs) * _silu(z)
    gw = D_SSM // SSD_G
    for k in range(SSD_G):
        gk = g[:, k * gw:(k + 1) * gw]
        ms = jnp.mean(gk * gk, axis=-1, keepdims=True)
        o_ref[0, :, k * gw:(k + 1) * gw] = (gk * lax.rsqrt(ms + NORM_EPS)
                                            * nw_ref[:, k * gw:(k + 1) * gw]).astype(o_ref.dtype)


def ssd_merge(yf, yb, xbc, z, d_skip, norm_w):
    bsz = yf.shape[0]
    tile = lambda w: pl.BlockSpec((1, ROW_TILE, w), lambda i, j: (i, j, 0))
    vec = pl.BlockSpec((1, D_SSM), lambda i, j: (0, 0))
    d_chan = jnp.repeat(d_skip.astype(F32), SSD_P).reshape(1, D_SSM)
    return pl.pallas_call(
        _ssd_merge_body,
        name='ssd_merge',
        grid=(bsz, N_ROW_TILES),
        in_specs=[tile(D_SSM), tile(D_SSM), tile(SSD_XBC), tile(D_SSM), vec, vec],
        out_specs=tile(D_SSM),
        out_shape=jax.ShapeDtypeStruct((bsz, NTOK, D_SSM), BF16),
        compiler_params=_cparams(("parallel", "parallel")),
    )(yf, yb, xbc, z, d_chan, norm_w.reshape(1, D_SSM))


def even_layer_mixer(xs, mods, norm_w, w_in, w_out, hy_conv_w, hy_conv_b, hy_filt, hy_bias,
                     ssd_conv_w, ssd_conv_b, ssd_dt_bias, ssd_a_log, ssd_d, ssd_norm_w):
    bsz = xs.shape[0]
    h = norm_mod(xs, norm_w, mods, 0, 1)
    splits = ((0, HY_IN), (HY_IN, D_SSM), (HY_IN + D_SSM, SSD_XBC), (HY_IN + D_SSM + SSD_XBC, 2 * SSD_H))
    p_hy, z, xbc_raw, dt = mm_split(h.reshape(bsz * NTOK, D_MODEL), w_in.astype(BF16), splits,
                                    (BF16, BF16, BF16, F32))
    to3 = lambda a: a.reshape(bsz, NTOK, a.shape[-1])
    y_hy = hyena_mixer_stream(to3(p_hy), hy_conv_w, hy_conv_b, hy_filt, hy_bias)
    xbc = dwconv_stream(to3(xbc_raw), ssd_conv_w, ssd_conv_b, act=True)
    yf, yb = ssd_scan(xbc, to3(dt), ssd_dt_bias, ssd_a_log)
    s = ssd_merge(yf, yb, xbc, to3(z), ssd_d, ssd_norm_w)
    wo = w_out.astype(BF16)
    return mm_resid([y_hy, s], [wo[:D_HY], wo[D_HY:]], xs, mods, 2, N_ROW_TILES)


GLA_QK = GLA_H * GLA_DK
GLA_V = GLA_H * GLA_DV


def _gla_dir(qkv_ref, lr_ref, gw_ref, gb_ref, st_ref, o_ref, *, d, reverse):
    q = GLA_CHUNK
    tri = _tri(q, 'upper' if reverse else 'lower')
    edge = 0 if reverse else q - 1
    lr = lr_ref[0, :, d * GLA_RANK:(d + 1) * GLA_RANK]
    logit = jnp.dot(lr, gw_ref[d], preferred_element_type=F32, precision=HIGHEST) + gb_ref[d:d + 1, :]
    log_g = -_softplus(-logit) * (1.0 / GLA_GATE_NORM)
    gcum = jnp.dot(tri.astype(F32), log_g, preferred_element_type=F32, precision=HIGHEST)
    total = gcum[edge:edge + 1, :]
    qf = qkv_ref[0, :, 0:GLA_QK].astype(F32)
    kf = qkv_ref[0, :, GLA_QK:2 * GLA_QK].astype(F32)
    v = qkv_ref[0, :, 2 * GLA_QK:2 * GLA_QK + GLA_V]
    qg = (qf * (GLA_DK ** -0.5) * jnp.exp(gcum)).astype(BF16)
    kg = (kf * jnp.exp(-gcum)).astype(BF16)
    kw = (kf * jnp.exp(total - gcum)).astype(BF16)
    rb = lax.broadcasted_iota(jnp.int32, (GLA_H * q, GLA_QK), 0) // q
    cb = lax.broadcasted_iota(jnp.int32, (GLA_H * q, GLA_QK), 1) // GLA_DK
    k_bd = jnp.where(rb == cb, jnp.concatenate([kg] * GLA_H, axis=0), jnp.zeros((), BF16))
    att = lax.dot_general(qg, k_bd, (((1,), (1,)), ((), ())), preferred_element_type=F32)
    i_i = lax.broadcasted_iota(jnp.int32, (q, GLA_H * q), 0)
    j_i = lax.broadcasted_iota(jnp.int32, (q, GLA_H * q), 1) % q
    keep = (j_i >= i_i) if reverse else (j_i <= i_i)
    att = jnp.where(keep, att, 0.0).astype(BF16)
    rv = lax.broadcasted_iota(jnp.int32, (GLA_H * q, GLA_V), 0) // q
    cv = lax.broadcasted_iota(jnp.int32, (GLA_H * q, GLA_V), 1) // GLA_DV
    v_bd = jnp.where(rv == cv, jnp.concatenate([v] * GLA_H, axis=0), jnp.zeros((), BF16))
    st = st_ref[...]
    o_in = jnp.dot(att, v_bd, preferred_element_type=F32)
    o_st = lax.dot_general(qg, st.astype(BF16), (((1,), (1,)), ((), ())), preferred_element_type=F32)
    o_ref[0] = o_in + o_st
    upd = lax.dot_general(v, kw, (((0,), (0,)), ((), ())), preferred_element_type=F32)
    rs = lax.broadcasted_iota(jnp.int32, (GLA_V, GLA_QK), 0) // GLA_DV
    cs = lax.broadcasted_iota(jnp.int32, (GLA_V, GLA_QK), 1) // GLA_DK
    st_ref[...] = st * jnp.exp(total) + jnp.where(rs == cs, upd, 0.0)


def _gla_body(qf_ref, lf_ref, qb_ref, lb_ref, gw_ref, gb_ref, of_ref, ob_ref, stf, stb):
    @pl.when(pl.program_id(1) == 0)
    def _():
        stf[...] = jnp.zeros_like(stf)
        stb[...] = jnp.zeros_like(stb)

    _gla_dir(qf_ref, lf_ref, gw_ref, gb_ref, stf, of_ref, d=0, reverse=False)
    _gla_dir(qb_ref, lb_ref, gw_ref, gb_ref, stb, ob_ref, d=1, reverse=True)


def gla_scan(qkv, lr, gate_w, gate_b):
    bsz = qkv.shape[0]
    nc = NTOK // GLA_CHUNK
    nlat = SEQ // GLA_CHUNK
    fwd_chunk = lambda s: (s + nlat) % nc
    bwd_chunk = lambda s: nc - 1 - s
    q_spec = lambda cm: pl.BlockSpec((1, GLA_CHUNK, qkv.shape[-1]), lambda b, s: (b, cm(s), 0))
    l_spec = lambda cm: pl.BlockSpec((1, GLA_CHUNK, 2 * GLA_RANK), lambda b, s: (b, cm(s), 0))
    o_spec = lambda cm: pl.BlockSpec((1, GLA_CHUNK, GLA_V), lambda b, s: (b, cm(s), 0))
    o_shape = jax.ShapeDtypeStruct((bsz, NTOK, GLA_V), F32)
    return pl.pallas_call(
        _gla_body,
        name='gla_scan',
        grid=(bsz, nc),
        in_specs=[q_spec(fwd_chunk), l_spec(fwd_chunk), q_spec(bwd_chunk), l_spec(bwd_chunk),
                  pl.BlockSpec((2, GLA_RANK, GLA_QK), lambda b, s: (0, 0, 0)),
                  pl.BlockSpec((2, GLA_QK), lambda b, s: (0, 0))],
        out_specs=[o_spec(fwd_chunk), o_spec(bwd_chunk)],
        out_shape=[o_shape, o_shape],
        scratch_shapes=[pltpu.VMEM((GLA_V, GLA_QK), F32), pltpu.VMEM((GLA_V, GLA_QK), F32)],
        compiler_params=_cparams(("parallel", "arbitrary")),
    )(qkv, lr, qkv, lr, gate_w.astype(F32), gate_b.astype(F32))


def _gla_merge_body(of_ref, ob_ref, r_ref, nw_ref, o_ref):
    o = of_ref[0] + ob_ref[0]
    r = r_ref[0].astype(F32)
    for h in range(GLA_H):
        sl = slice(h * GLA_DV, (h + 1) * GLA_DV)
        oh = o[:, sl]
        ms = jnp.mean(oh * oh, axis=-1, keepdims=True)
        o_ref[0, :, sl] = (oh * lax.rsqrt(ms + NORM_EPS) * nw_ref[:, sl] * _silu(r[:, sl])).astype(o_ref.dtype)


def gla_merge_stream(of, ob, r, norm_w):
    bsz = of.shape[0]
    tile = pl.BlockSpec((1, ROW_TILE, GLA_V), lambda i, j: (i, j, 0))
    return pl.pallas_call(
        _gla_merge_body,
        name='gla_merge',
        grid=(bsz, N_LAT_TILES),
        in_specs=[tile, tile, tile, pl.BlockSpec((1, GLA_V), lambda i, j: (0, 0))],
        out_specs=tile,
        out_shape=jax.ShapeDtypeStruct((bsz, SEQ, GLA_V), BF16),
        compiler_params=_cparams(("parallel", "parallel")),
    )(of, ob, r, norm_w.reshape(1, GLA_V))


RG_TILE = 8


def _gelu_tanh(x):
    return 0.5 * x * (1.0 + jnp.tanh(math.sqrt(2.0 / math.pi) * (x + 0.044715 * x * x * x)))


def _rg_scan_block(a_s, x_s, h_s, base, carry, reverse):
    n_tiles = ROW_TILE // RG_TILE
    row = lax.broadcasted_iota(jnp.int32, (RG_TILE, D_RG), 0)

    def tile_step(i, h_prev):
        t = (n_tiles - 1 - i) if reverse else i
        r0 = pl.multiple_of(t * RG_TILE, RG_TILE)
        a = a_s[pl.ds(r0, RG_TILE), :]
        x = x_s[pl.ds(r0, RG_TILE), :]
        for s in (1, 2, 4):
            if reverse:
                ok = row < RG_TILE - s
                shift = RG_TILE - s
            else:
                ok = row >= s
                shift = s
            a_sh = jnp.where(ok, pltpu.roll(a, shift, 0), 1.0)
            x_sh = jnp.where(ok, pltpu.roll(x, shift, 0), 0.0)
            x = a * x_sh + x
            a = a * a_sh
        h = x + a * h_prev
        h_s[pl.ds(base + r0, RG_TILE), :] = h
        edge = 0 if reverse else RG_TILE - 1
        return jnp.broadcast_to(h[edge:edge + 1, :], (RG_TILE, D_RG))

    return lax.fori_loop(0, n_tiles, tile_step, carry)


def _rglru_body(u_ref, g_ref, w_ref, b_ref, c_ref, o_ref, hf_s, a_s, x_s, hb_s):
    n_blocks = NTOK // ROW_TILE
    fwd_order = list(range(N_LAT_TILES, n_blocks)) + list(range(N_LAT_TILES))
    bwd_order = list(range(n_blocks - 1, N_LAT_TILES - 1, -1)) + list(range(N_LAT_TILES - 1, -1, -1))

    def gates(blk, d):
        ub = u_ref[0, blk * ROW_TILE:(blk + 1) * ROW_TILE, :]
        z = jnp.dot(ub, w_ref[:, 2 * d * D_RG:2 * (d + 1) * D_RG], preferred_element_type=F32)
        z = z + b_ref[:, 2 * d * D_RG:2 * (d + 1) * D_RG]
        r = 1.0 / (1.0 + jnp.exp(-z[:, :D_RG]))
        i = 1.0 / (1.0 + jnp.exp(-z[:, D_RG:]))
        a = jnp.exp(c_ref[d:d + 1, :] * r)
        a_s[...] = a
        x_s[...] = jnp.sqrt(1.0 - a * a) * i * ub.astype(F32)

    carry = jnp.zeros((RG_TILE, D_RG), F32)
    for blk in fwd_order:
        gates(blk, 0)
        carry = _rg_scan_block(a_s, x_s, hf_s, blk * ROW_TILE, carry, reverse=False)
    carry = jnp.zeros((RG_TILE, D_RG), F32)
    for blk in bwd_order:
        gates(blk, 1)
        carry = _rg_scan_block(a_s, x_s, hb_s, 0, carry, reverse=True)
        rows = slice(blk * ROW_TILE, (blk + 1) * ROW_TILE)
        gate = g_ref[0, rows, :].astype(F32)
        o_ref[0, rows, :] = ((hf_s[rows, :] + hb_s[...]) * _gelu_tanh(gate)).astype(o_ref.dtype)


def rglru_stream(u, gate, w_a, b_a, w_x, b_x, lam):
    bsz = u.shape[0]
    eye = jnp.eye(RG_BLOCKS, dtype=F32)
    dense = lambda w: jnp.einsum('nio,nm->nimo', w, eye).reshape(D_RG, D_RG)
    w_cat = jnp.concatenate([dense(w_a[0]), dense(w_x[0]), dense(w_a[1]), dense(w_x[1])], axis=1).astype(BF16)
    b_cat = jnp.concatenate([b_a[0], b_x[0], b_a[1], b_x[1]]).astype(F32).reshape(1, 4 * D_RG)
    c = -RG_C * jax.nn.softplus(-lam.astype(F32))
    seq = pl.BlockSpec((1, NTOK, D_RG), lambda i: (i, 0, 0))
    return pl.pallas_call(
        _rglru_body,
        name='rglru',
        grid=(bsz,),
        in_specs=[seq, seq,
                  pl.BlockSpec((D_RG, 4 * D_RG), lambda i: (0, 0)),
                  pl.BlockSpec((1, 4 * D_RG), lambda i: (0, 0)),
                  pl.BlockSpec((2, D_RG), lambda i: (0, 0))],
        out_specs=seq,
        out_shape=jax.ShapeDtypeStruct((bsz, NTOK, D_RG), BF16),
        scratch_shapes=[pltpu.VMEM((NTOK, D_RG), F32), pltpu.VMEM((ROW_TILE, D_RG), F32),
                        pltpu.VMEM((ROW_TILE, D_RG), F32), pltpu.VMEM((ROW_TILE, D_RG), F32)],
        compiler_params=_cparams(("parallel",)),
    )(u, gate, w_cat, b_cat, c)


def odd_layer_mixer_pallas(xs, mods, norm_w, w_in, w_out, gla_args, rg_args):
    bsz = xs.shape[0]
    gate_w, gate_b, gla_norm_w = gla_args
    rg_conv_w, rg_conv_b, w_a, b_a, w_x, b_x, lam = rg_args
    h = norm_mod(xs, norm_w, mods, 0, 1)
    h = jnp.concatenate([to_col_major(h[:, :SEQ]), h[:, SEQ:]], axis=1)
    nqk, nv = GLA_QK, GLA_V
    r0 = 2 * nqk + nv + 2 * GLA_RANK
    w = jnp.concatenate([w_in[:, :2 * nqk + nv], w_in[:, r0:r0 + nv], w_in[:, GLA_IN:],
                         w_in[:, 2 * nqk + nv:r0]], axis=1).astype(BF16)
    qkv_w = 2 * nqk + nv
    splits = ((0, qkv_w), (qkv_w, nv), (qkv_w + nv, D_RG), (qkv_w + nv + D_RG, D_RG),
              (qkv_w + nv + 2 * D_RG, 2 * GLA_RANK))
    qkv, r, u_raw, gate, lr = mm_split(h.reshape(bsz * NTOK, D_MODEL), w, splits, (BF16, BF16, BF16, BF16, F32))
    to3 = lambda a: a.reshape(bsz, NTOK, a.shape[-1])
    of, ob = gla_scan(to3(qkv), to3(lr), gate_w, gate_b.reshape(2, GLA_QK))
    a_l = gla_merge_stream(of, ob, to3(r), gla_norm_w)
    u = dwconv_stream(to3(u_raw), rg_conv_w, rg_conv_b, act=False)
    r_l = rglru_stream(u, to3(gate), w_a, b_a, w_x, b_x, lam)[:, :SEQ]
    wo = w_out.astype(BF16)
    return mm_resid([from_col_major(a_l), from_col_major(r_l)], [wo[:GLA_V], wo[GLA_V:]], xs, mods, 2, N_LAT_TILES)


def rms_norm(x, w):
    xf = x.astype(F32)
    y = xf * lax.rsqrt(jnp.mean(jnp.square(xf), axis=-1, keepdims=True) + NORM_EPS)
    return y.astype(x.dtype) * w


def dwconv(x, w, b):
    y = lax.conv_general_dilated(x, w[:, None, :].astype(x.dtype), window_strides=(1,), padding='SAME',
                                 dimension_numbers=('NWC', 'WIO', 'NWC'), feature_group_count=x.shape[-1])
    return y + b.astype(x.dtype)


def maybe_flip(a, rev):
    return jnp.flip(a, axis=1) if rev else a


def to_col_major(x):
    b, n, d = x.shape
    rows = n // GRID_W
    return x.reshape(b, rows, GRID_W, d).transpose(0, 2, 1, 3).reshape(b, n, d)


def from_col_major(x):
    b, n, d = x.shape
    rows = n // GRID_W
    return x.reshape(b, GRID_W, rows, d).transpose(0, 2, 1, 3).reshape(b, n, d)


def gla_inputs(p, gate_w, gate_b):
    b, n, _ = p.shape
    nqk, nv = GLA_H * GLA_DK, GLA_H * GLA_DV
    q = p[..., :nqk].reshape(b, n, GLA_H, GLA_DK) * GLA_DK ** -0.5
    k = p[..., nqk:2 * nqk].reshape(b, n, GLA_H, GLA_DK)
    v = p[..., 2 * nqk:2 * nqk + nv].reshape(b, n, GLA_H, GLA_DV)
    lr = p[..., 2 * nqk + nv:2 * nqk + nv + 2 * GLA_RANK].reshape(b, n, 2, GLA_RANK)
    r = p[..., 2 * nqk + nv + 2 * GLA_RANK:]
    logit = jnp.einsum('bler,erk->blek', lr, gate_w) + gate_b
    log_g = (jax.nn.log_sigmoid(logit.astype(F32)) / GLA_GATE_NORM).reshape(b, n, 2, GLA_H, GLA_DK)
    return q, k, v, log_g, r


def gla_states(k, v, log_g, s0):
    b, n = k.shape[:2]
    nc = n // GLA_CHUNK
    kc = k.reshape(b, nc, GLA_CHUNK, GLA_H, GLA_DK)
    vc = v.reshape(b, nc, GLA_CHUNK, GLA_H, GLA_DV)
    gcum = jnp.cumsum(log_g.reshape(b, nc, GLA_CHUNK, GLA_H, GLA_DK), axis=2)
    states = jnp.einsum('bcqhd,bcqhv->bchdv', kc * jnp.exp(gcum[:, :, -1:] - gcum), vc)
    chunk_decay = jnp.exp(gcum[:, :, -1])

    def step(s, inp):
        st, dcy = inp
        return dcy[..., None] * s + st, s

    s_fin, s_prev = lax.scan(step, s0, (jnp.moveaxis(states, 1, 0), jnp.moveaxis(chunk_decay, 1, 0)))
    return jnp.moveaxis(s_prev, 0, 1), s_fin


def gla_output(q, k, v, log_g, s_prev):
    b, n = q.shape[:2]
    nc = n // GLA_CHUNK
    qc = q.reshape(b, nc, GLA_CHUNK, GLA_H, GLA_DK)
    kc = k.reshape(b, nc, GLA_CHUNK, GLA_H, GLA_DK)
    vc = v.reshape(b, nc, GLA_CHUNK, GLA_H, GLA_DV)
    gcum = jnp.cumsum(log_g.reshape(b, nc, GLA_CHUNK, GLA_H, GLA_DK), axis=2)
    qg = qc * jnp.exp(gcum)
    kg = kc * jnp.exp(-gcum)
    mask = jnp.tril(jnp.ones((GLA_CHUNK, GLA_CHUNK), bool))
    att = jnp.where(mask, jnp.einsum('bcihd,bcjhd->bchij', qg, kg), 0.0)
    o = jnp.einsum('bchij,bcjhv->bcihv', att, vc) + jnp.einsum('bcihd,bchdv->bcihv', qg, s_prev)
    return o.reshape(b, n, GLA_H, GLA_DV)


def gla_merge(os_, r, norm_w):
    b, n = r.shape[:2]
    o = rms_norm(os_[0] + os_[1], norm_w.reshape(GLA_H, GLA_DV))
    return o.reshape(b, n, GLA_H * GLA_DV) * jax.nn.silu(r)


def gla_mixer(p_c, p_l, gate_w, gate_b, norm_w):
    q_c, k_c, v_c, g_c, r_c = gla_inputs(p_c, gate_w, gate_b)
    q_l, k_l, v_l, g_l, r_l = gla_inputs(p_l, gate_w, gate_b)
    s0 = jnp.zeros((p_c.shape[0], GLA_H, GLA_DK, GLA_DV), F32)
    os_l = []
    for d, rev in enumerate((False, True)):
        f = functools.partial(maybe_flip, rev=rev)
        _, sf_c = gla_states(f(k_c), f(v_c), f(g_c[:, :, d]), s0)
        sp_l, _ = gla_states(f(k_l), f(v_l), f(g_l[:, :, d]), sf_c)
        os_l.append(f(gla_output(f(q_l), f(k_l), f(v_l), f(g_l[:, :, d]), sp_l)))
    return gla_merge(os_l, r_l, norm_w)


def rglru_inputs(p, conv_w, conv_b, w_a, b_a, w_x, b_x, lam):
    b, n, _ = p.shape
    u = dwconv(p[..., :D_RG], conv_w, conv_b)
    ub = u.reshape(b, n, RG_BLOCKS, RG_BW)
    r = jax.nn.sigmoid((jnp.einsum('blni,enio->bleno', ub, w_a).reshape(b, n, 2, D_RG) + b_a).astype(F32))
    i = jax.nn.sigmoid((jnp.einsum('blni,enio->bleno', ub, w_x).reshape(b, n, 2, D_RG) + b_x).astype(F32))
    log_a = -RG_C * jax.nn.softplus(-lam.astype(F32)) * r
    x_in = jnp.sqrt(-jnp.expm1(2.0 * log_a)) * i * u[:, :, None, :].astype(F32)
    return p[..., D_RG:], jnp.exp(log_a), x_in


def lru_scan(a, u, h0):
    u = u.at[:, 0].add(a[:, 0] * h0)

    def combine(lhs, rhs):
        a1, b1 = lhs
        a2, b2 = rhs
        return a1 * a2, a2 * b1 + b2

    return lax.associative_scan(combine, (a, u), axis=1)[1]


def rglru_mixer(p_c, p_l, conv_w, conv_b, w_a, b_a, w_x, b_x, lam):
    gb_c, a_c, u_c = rglru_inputs(p_c, conv_w, conv_b, w_a, b_a, w_x, b_x, lam)
    gb_l, a_l, u_l = rglru_inputs(p_l, conv_w, conv_b, w_a, b_a, w_x, b_x, lam)
    h0 = jnp.zeros((p_c.shape[0], D_RG), F32)
    hs_l = []
    for d, rev in enumerate((False, True)):
        f = functools.partial(maybe_flip, rev=rev)
        h_c = f(lru_scan(f(a_c[:, :, d]), f(u_c[:, :, d]), h0))
        h_end = h_c[:, 0] if rev else h_c[:, -1]
        hs_l.append(f(lru_scan(f(a_l[:, :, d]), f(u_l[:, :, d]), h_end)))
    return (hs_l[0] + hs_l[1]) * jax.nn.gelu(gb_l.astype(F32))


def odd_layer_mixer(xs, mods, norm_w, w_in, w_out, gla_args, rg_args):
    bsz = xs.shape[0]
    h = norm_mod(xs, norm_w, mods, 0, 1)
    h = jnp.concatenate([to_col_major(h[:, :SEQ]), h[:, SEQ:]], axis=1)
    p = pmatmul(h.reshape(bsz * NTOK, D_MODEL), w_in.astype(BF16)).reshape(bsz, NTOK, OD_IN)
    p_l, p_c = p[:, :SEQ], p[:, SEQ:]
    a_l = gla_mixer(p_c[..., :GLA_IN], p_l[..., :GLA_IN], *gla_args)
    r_l = rglru_mixer(p_c[..., GLA_IN:], p_l[..., GLA_IN:], *rg_args)
    mix = from_col_major(jnp.concatenate([a_l, r_l], axis=-1)).astype(BF16)
    return mm_resid([mix], [w_out.astype(BF16)], xs, mods, 2, N_LAT_TILES)


def kernel(x, c, ctx, c_ctx, ada_w, ada_b, norm1_w, norm2_w, ev_w_in, ev_w_out, hy_conv_w, hy_conv_b, hy_fw0, hy_fb0, hy_fw1, hy_fb1, hy_fw2, hy_fb2, hy_fw3, hy_freq, hy_bias, ssd_conv_w, ssd_conv_b, ssd_dt_bias, ssd_a_log, ssd_d, ssd_norm_w, od_w_in, od_w_out, gla_gate_w, gla_gate_b, gla_norm_w, rg_conv_w, rg_conv_b, rg_w_a, rg_b_a, rg_w_x, rg_b_x, rg_lambda, router_w, router_b, moe_w_gate, moe_w_up, moe_w_down, sh_w_gate, sh_w_up, sh_w_down, final_norm_w):
    xs = jnp.concatenate([x, ctx], axis=1)
    for i in range(DEPTH):
        last = i == DEPTH - 1
        j = i // 2
        mods = adaln_table(c, c_ctx, ada_w[i], ada_b[i])
        if i % 2 == 0:
            hy_filt = (hy_fw0[j], hy_fb0[j], hy_fw1[j], hy_fb1[j], hy_fw2[j], hy_fb2[j], hy_fw3[j], hy_freq[j])
            xs = even_layer_mixer(xs, mods, norm1_w[i], ev_w_in[j], ev_w_out[j], hy_conv_w[j], hy_conv_b[j],
                                  hy_filt, hy_bias[j], ssd_conv_w[j], ssd_conv_b[j], ssd_dt_bias[j],
                                  ssd_a_log[j], ssd_d[j], ssd_norm_w[j])
        else:
            gla_args = (gla_gate_w[j], gla_gate_b[j], gla_norm_w[j])
            rg_args = (rg_conv_w[j], rg_conv_b[j], rg_w_a[j], rg_b_a[j], rg_w_x[j], rg_b_x[j], rg_lambda[j])
            xs = odd_layer_mixer_pallas(xs, mods, norm1_w[i], od_w_in[j], od_w_out[j], gla_args, rg_args)
        n_tiles = N_LAT_TILES if last else N_ROW_TILES
        xs = moe_layer(xs, norm2_w[i], mods, router_w[i], router_b[i], moe_w_gate[i], moe_w_up[i],
                       moe_w_down[i], sh_w_gate[i], sh_w_up[i], sh_w_down[i], n_tiles)
    return final_norm(xs, final_norm_w)
```

```python
import functools
import math

import jax
import jax.numpy as jnp
from jax import lax
from jax.experimental import pallas as pl
from jax.experimental.pallas import tpu as pltpu

D_MODEL = 1024
BATCH = 16
SEQ = 2048
DEPTH = 2

CTX_LEN = 256
GRID_W = 64
NORM_EPS = 1e-6

D_HY = D_MODEL // 2
HY_ORDER = 2
HY_SHORT = 3
HY_BANDS = 16
HY_EMB = 1 + 2 * HY_BANDS
HY_FF = 64
HY_DECAY_PCT_LO = 0.3
HY_DECAY_PCT_HI = 1.5
HY_DECAY_TARGET = 1e-2
HY_IN = 3 * D_HY

D_SSM = D_MODEL // 2
SSD_P = 64
SSD_H = D_SSM // SSD_P
SSD_G = 2
SSD_HPG = SSD_H // SSD_G
SSD_N = 128
SSD_CONV = 4
SSD_CHUNK = 128
SSD_XBC = D_SSM + 2 * SSD_G * SSD_N
SSD_IN = D_SSM + SSD_XBC + 2 * SSD_H
EV_IN = HY_IN + SSD_IN
EV_MIX = D_HY + D_SSM

GLA_H = 4
GLA_DV = (D_MODEL // 2) // GLA_H
GLA_DK = GLA_DV // 2
GLA_RANK = 16
GLA_GATE_NORM = 16.0
GLA_CHUNK = 64
GLA_IN = 2 * GLA_H * GLA_DK + 2 * GLA_H * GLA_DV + 2 * GLA_RANK

D_RG = D_MODEL // 2
RG_BLOCKS = 8
RG_BW = D_RG // RG_BLOCKS
RG_CONV = 4
RG_C = 8.0
RG_IN = 2 * D_RG
OD_IN = GLA_IN + RG_IN
OD_MIX = GLA_H * GLA_DV + D_RG

MOE_EXPERTS = 64
MOE_TOPK = 8
MOE_D_EXPERT = 256
MOE_D_SHARED = 256
MOE_SCALE = 2.5
MOE_BLOCK = 256

F32 = jnp.float32
BF16 = jnp.bfloat16
HIGHEST = lax.Precision.HIGHEST

NTOK = SEQ + CTX_LEN
ROW_TILE = 256
N_ROW_TILES = NTOK // ROW_TILE
N_LAT_TILES = SEQ // ROW_TILE

VMEM_LIMIT = 48 * 1024 * 1024
VMEM_LIMIT_BIG = 56 * 1024 * 1024


def _cparams(sem, limit=VMEM_LIMIT):
    return pltpu.CompilerParams(dimension_semantics=sem, vmem_limit_bytes=limit)


def _pick_tile(n, pref):
    t = min(n, pref)
    while n % t:
        t //= 2
    return t


def _silu(x):
    return x / (1.0 + jnp.exp(-x))


def _softplus(x):
    return jnp.maximum(x, 0.0) + jnp.log(1.0 + jnp.exp(-jnp.abs(x)))


def _mm_bf16_body(a_ref, w_ref, o_ref):
    o_ref[...] = jnp.dot(a_ref[...].astype(BF16), w_ref[...].astype(BF16),
                         preferred_element_type=F32).astype(o_ref.dtype)


def _mm_f32_body(a_ref, w_ref, o_ref):
    o_ref[...] = jnp.dot(a_ref[...], w_ref[...], preferred_element_type=F32,
                         precision=HIGHEST).astype(o_ref.dtype)


def pmatmul(a, w, *, exact=False, out_dtype=F32, tm=512, tn=None):
    m, k = a.shape
    n = w.shape[1]
    tm = _pick_tile(m, tm)
    tn = n if tn is None else _pick_tile(n, tn)
    body = _mm_f32_body if exact else _mm_bf16_body
    return pl.pallas_call(
        body,
        name='mm',
        grid=(m // tm, n // tn),
        in_specs=[pl.BlockSpec((tm, k), lambda i, j: (i, 0)),
                  pl.BlockSpec((k, tn), lambda i, j: (0, j))],
        out_specs=pl.BlockSpec((tm, tn), lambda i, j: (i, j)),
        out_shape=jax.ShapeDtypeStruct((m, n), out_dtype),
        compiler_params=_cparams(("parallel", "parallel")),
    )(a, w)


def _mm_split_body(a_ref, w_ref, *o_refs, splits):
    a = a_ref[...]
    for o_ref, (start, width) in zip(o_refs, splits):
        o_ref[...] = jnp.dot(a, w_ref[:, start:start + width],
                             preferred_element_type=F32).astype(o_ref.dtype)


def mm_split(a, w, splits, dtypes, tm=512):
    m, k = a.shape
    n = w.shape[1]
    tm = _pick_tile(m, tm)
    return pl.pallas_call(
        functools.partial(_mm_split_body, splits=tuple(splits)),
        name='mm_split',
        grid=(m // tm,),
        in_specs=[pl.BlockSpec((tm, k), lambda i: (i, 0)),
                  pl.BlockSpec((k, n), lambda i: (0, 0))],
        out_specs=[pl.BlockSpec((tm, wd), lambda i: (i, 0)) for _, wd in splits],
        out_shape=[jax.ShapeDtypeStruct((m, wd), dt) for (_, wd), dt in zip(splits, dtypes)],
        compiler_params=_cparams(("parallel",)),
    )(a, w)


def _mm_resid_body(*refs, n_pairs):
    a_refs = refs[:n_pairs]
    w_refs = refs[n_pairs:2 * n_pairs]
    x_ref, g_ref, o_ref = refs[2 * n_pairs:]
    acc = jnp.dot(a_refs[0][0], w_refs[0][...], preferred_element_type=F32)
    for a_ref, w_ref in zip(a_refs[1:], w_refs[1:]):
        acc = acc + jnp.dot(a_ref[0], w_ref[...], preferred_element_type=F32)
    o_ref[0] = x_ref[0] + g_ref[...] * acc


def mm_resid(a_list, w_list, xs, mods, gate_idx, n_tiles):
    b, nt, d = xs.shape
    n_pairs = len(a_list)
    in_specs = [pl.BlockSpec((1, ROW_TILE, a.shape[-1]), lambda i, j: (i, j, 0)) for a in a_list]
    in_specs += [pl.BlockSpec(w.shape, lambda i, j: (0, 0)) for w in w_list]
    in_specs += [pl.BlockSpec((1, ROW_TILE, d), lambda i, j: (i, j, 0)),
                 _mod_spec(gate_idx, d)]
    return pl.pallas_call(
        functools.partial(_mm_resid_body, n_pairs=n_pairs),
        name='mm_resid',
        grid=(b, n_tiles),
        in_specs=in_specs,
        out_specs=pl.BlockSpec((1, ROW_TILE, d), lambda i, j: (i, j, 0)),
        out_shape=jax.ShapeDtypeStruct(xs.shape, F32),
        input_output_aliases={2 * n_pairs: 0},
        compiler_params=_cparams(("parallel", "parallel")),
    )(*a_list, *w_list, xs, mods)


def _mod_spec(idx, d):
    return pl.BlockSpec((None, None, None, 1, d), lambda i, j: (i, 1 - j // N_LAT_TILES, idx, 0, 0))


def adaln_table(c, c_ctx, w, b):
    cv = jax.nn.silu(jnp.concatenate([c, c_ctx[None, :]], axis=0))
    cv = jnp.pad(cv, ((0, 24 - cv.shape[0]), (0, 0)))
    m = pmatmul(cv, w, exact=True, tn=1536)[:BATCH + 1] + b
    per_sample = m[:BATCH]
    ctx_row = jnp.broadcast_to(m[BATCH][None, :], per_sample.shape)
    return jnp.stack([ctx_row, per_sample], axis=1).reshape(BATCH, 2, 6, 1, D_MODEL)


def _norm_mod(x, w, shift, scale):
    ms = jnp.mean(x * x, axis=-1, keepdims=True)
    return (x * lax.rsqrt(ms + NORM_EPS) * w) * (1.0 + scale) + shift


def _norm_mod_body(x_ref, w_ref, sh_ref, sc_ref, o_ref):
    o_ref[0] = _norm_mod(x_ref[0], w_ref[...], sh_ref[...], sc_ref[...]).astype(o_ref.dtype)


def norm_mod(xs, w, mods, shift_idx, scale_idx):
    b, nt, d = xs.shape
    return pl.pallas_call(
        _norm_mod_body,
        name='norm_mod',
        grid=(b, nt // ROW_TILE),
        in_specs=[pl.BlockSpec((1, ROW_TILE, d), lambda i, j: (i, j, 0)),
                  pl.BlockSpec((1, d), lambda i, j: (0, 0)),
                  _mod_spec(shift_idx, d), _mod_spec(scale_idx, d)],
        out_specs=pl.BlockSpec((1, ROW_TILE, d), lambda i, j: (i, j, 0)),
        out_shape=jax.ShapeDtypeStruct(xs.shape, BF16),
        compiler_params=_cparams(("parallel", "parallel")),
    )(xs, w.reshape(1, d), mods, mods)


def _final_norm_body(x_ref, w_ref, o_ref):
    x = x_ref[0]
    ms = jnp.mean(x * x, axis=-1, keepdims=True)
    o_ref[0] = x * lax.rsqrt(ms + NORM_EPS) * w_ref[...]


def final_norm(xs, w):
    b, _, d = xs.shape
    return pl.pallas_call(
        _final_norm_body,
        name='final_norm',
        grid=(b, N_LAT_TILES),
        in_specs=[pl.BlockSpec((1, ROW_TILE, d), lambda i, j: (i, j, 0)),
                  pl.BlockSpec((1, d), lambda i, j: (0, 0))],
        out_specs=pl.BlockSpec((1, ROW_TILE, d), lambda i, j: (i, j, 0)),
        out_shape=jax.ShapeDtypeStruct((b, SEQ, d), F32),
        compiler_params=_cparams(("parallel", "parallel")),
    )(xs, w.reshape(1, d))


def _route_body(x_ref, w_ref, sh_ref, sc_ref, rw_ref, rb_ref, h_ref, idx_ref, wsel_ref, rank_ref, cnt_ref):
    first = (pl.program_id(0) == 0) & (pl.program_id(1) == 0)

    @pl.when(first)
    def _():
        cnt_ref[...] = jnp.zeros_like(cnt_ref)

    h = _norm_mod(x_ref[0], w_ref[...], sh_ref[...], sc_ref[...])
    h_ref[0] = h.astype(h_ref.dtype)
    logits = jnp.dot(h, rw_ref[...], preferred_element_type=F32, precision=HIGHEST)
    scores = 1.0 / (1.0 + jnp.exp(-logits))
    tm, ne = scores.shape
    lane = lax.broadcasted_iota(jnp.int32, (tm, ne), 1).astype(F32)
    slot = lax.broadcasted_iota(jnp.int32, (tm, MOE_TOPK), 1)
    sel = scores + rb_ref[...]
    picked = jnp.zeros((tm, ne), F32)
    hits = []
    idx_out = jnp.zeros((tm, MOE_TOPK), F32)
    w_out = jnp.zeros((tm, MOE_TOPK), F32)
    for k in range(MOE_TOPK):
        m = jnp.max(sel, axis=-1, keepdims=True)
        ik = jnp.min(jnp.where(sel == m, lane, float(ne)), axis=-1, keepdims=True)
        hit = lane == ik
        wk = jnp.sum(jnp.where(hit, scores, 0.0), axis=-1, keepdims=True)
        sel = jnp.where(hit, -jnp.inf, sel)
        picked = picked + hit.astype(F32)
        hits.append(hit)
        idx_out = jnp.where(slot == k, ik, idx_out)
        w_out = jnp.where(slot == k, wk, w_out)
    wsum = jnp.sum(w_out, axis=-1, keepdims=True)
    wsel_ref[0] = w_out / wsum * MOE_SCALE
    idx_ref[0] = idx_out.astype(jnp.int32)
    r_i = lax.broadcasted_iota(jnp.int32, (tm, tm), 0)
    c_i = lax.broadcasted_iota(jnp.int32, (tm, tm), 1)
    strict_lower = (c_i < r_i).astype(BF16)
    before = jnp.dot(strict_lower, picked.astype(BF16), preferred_element_type=F32) + cnt_ref[...]
    rank_out = jnp.zeros((tm, MOE_TOPK), F32)
    for k in range(MOE_TOPK):
        rk = jnp.sum(jnp.where(hits[k], before, 0.0), axis=-1, keepdims=True)
        rank_out = jnp.where(slot == k, rk, rank_out)
    rank_ref[0] = rank_out.astype(jnp.int32)
    cnt_ref[...] = cnt_ref[...] + jnp.sum(picked, axis=0, keepdims=True)


def route(xs, w, mods, router_w, router_b, n_tiles):
    b, nt, d = xs.shape
    rows = n_tiles * ROW_TILE
    small = lambda dt: jax.ShapeDtypeStruct((b, rows, MOE_TOPK), dt)
    small_spec = pl.BlockSpec((1, ROW_TILE, MOE_TOPK), lambda i, j: (i, j, 0))
    return pl.pallas_call(
        _route_body,
        name='route',
        grid=(b, n_tiles),
        in_specs=[pl.BlockSpec((1, ROW_TILE, d), lambda i, j: (i, j, 0)),
                  pl.BlockSpec((1, d), lambda i, j: (0, 0)),
                  _mod_spec(3, d), _mod_spec(4, d),
                  pl.BlockSpec((d, MOE_EXPERTS), lambda i, j: (0, 0)),
                  pl.BlockSpec((1, MOE_EXPERTS), lambda i, j: (0, 0))],
        out_specs=[pl.BlockSpec((1, ROW_TILE, d), lambda i, j: (i, j, 0)),
                   small_spec, small_spec, small_spec,
                   pl.BlockSpec((1, MOE_EXPERTS), lambda i, j: (0, 0))],
        out_shape=[jax.ShapeDtypeStruct((b, rows, d), BF16), small(jnp.int32), small(F32), small(jnp.int32),
                   jax.ShapeDtypeStruct((1, MOE_EXPERTS), F32)],
        compiler_params=_cparams(("arbitrary", "arbitrary")),
    )(xs, w.reshape(1, d), mods, mods, router_w, router_b.reshape(1, MOE_EXPERTS))


def _swiglu(x, wg, wu, wd):
    g = jnp.dot(x, wg, preferred_element_type=F32)
    u = jnp.dot(x, wu, preferred_element_type=F32)
    h = (_silu(g) * u).astype(BF16)
    return jnp.dot(h, wd, preferred_element_type=F32)


def _expert_body(be_ref, nu_ref, x_ref, wg_ref, wu_ref, wd_ref, o_ref, wg_s, wu_s, wd_s):
    i = pl.program_id(0)
    used = i < nu_ref[0]

    @pl.when(used & ((i == 0) | (be_ref[i] != be_ref[jnp.maximum(i - 1, 0)])))
    def _():
        wg_s[...] = wg_ref[0].astype(BF16)
        wu_s[...] = wu_ref[0].astype(BF16)
        wd_s[...] = wd_ref[0].astype(BF16)

    @pl.when(used)
    def _():
        o_ref[...] = _swiglu(x_ref[...], wg_s[...], wu_s[...], wd_s[...]).astype(o_ref.dtype)

    @pl.when(jnp.logical_not(used))
    def _():
        o_ref[...] = jnp.zeros_like(o_ref)


def moe_experts(x_rows, block_e, n_used, wg, wu, wd):
    rows, d = x_rows.shape
    n_blocks = rows // MOE_BLOCK
    f = wg.shape[-1]
    grid_spec = pltpu.PrefetchScalarGridSpec(
        num_scalar_prefetch=2,
        grid=(n_blocks,),
        in_specs=[
            pl.BlockSpec((MOE_BLOCK, d), lambda i, be, nu: (i, 0)),
            pl.BlockSpec((1, d, f), lambda i, be, nu: (be[i], 0, 0)),
            pl.BlockSpec((1, d, f), lambda i, be, nu: (be[i], 0, 0)),
            pl.BlockSpec((1, f, d), lambda i, be, nu: (be[i], 0, 0)),
        ],
        out_specs=pl.BlockSpec((MOE_BLOCK, d), lambda i, be, nu: (i, 0)),
        scratch_shapes=[pltpu.VMEM((d, f), BF16), pltpu.VMEM((d, f), BF16), pltpu.VMEM((f, d), BF16)],
    )
    return pl.pallas_call(
        _expert_body,
        name='experts',
        grid_spec=grid_spec,
        out_shape=jax.ShapeDtypeStruct((rows, d), BF16),
        compiler_params=_cparams(("arbitrary",)),
    )(block_e, n_used, x_rows, wg, wu, wd)


def _shared_resid_body(h_ref, wg_ref, wu_ref, wd_ref, pk_ref, ws_ref, x_ref, g_ref, o_ref):
    y = _swiglu(h_ref[0], wg_ref[...], wu_ref[...], wd_ref[...])
    ws = ws_ref[0]
    for k in range(MOE_TOPK):
        y = y + ws[:, k:k + 1] * pk_ref[k, 0].astype(F32)
    o_ref[0] = x_ref[0] + g_ref[...] * y


def shared_resid(h, picked, wsel, xs, mods, wg, wu, wd, n_tiles):
    b, _, d = xs.shape
    f = wg.shape[-1]
    tile = pl.BlockSpec((1, ROW_TILE, d), lambda i, j: (i, j, 0))
    return pl.pallas_call(
        _shared_resid_body,
        name='shared_resid',
        grid=(b, n_tiles),
        in_specs=[tile,
                  pl.BlockSpec((d, f), lambda i, j: (0, 0)),
                  pl.BlockSpec((d, f), lambda i, j: (0, 0)),
                  pl.BlockSpec((f, d), lambda i, j: (0, 0)),
                  pl.BlockSpec((MOE_TOPK, 1, ROW_TILE, d), lambda i, j: (0, i, j, 0)),
                  pl.BlockSpec((1, ROW_TILE, MOE_TOPK), lambda i, j: (i, j, 0)),
                  tile, _mod_spec(5, d)],
        out_specs=tile,
        out_shape=jax.ShapeDtypeStruct(xs.shape, F32),
        input_output_aliases={6: 0},
        compiler_params=_cparams(("parallel", "parallel")),
    )(h, wg, wu, wd, picked, wsel, xs, mods)


def moe_layer(xs, norm_w, mods, router_w, router_b, w_gate, w_up, w_down, sh_gate, sh_up, sh_down, n_tiles):
    b, _, d = xs.shape
    rows_per_sample = n_tiles * ROW_TILE
    n = b * rows_per_sample
    h, idx, wsel, rank, counts = route(xs, norm_w, mods, router_w, router_b, n_tiles)
    counts = counts[0].astype(jnp.int32)
    padded = (counts + MOE_BLOCK - 1) // MOE_BLOCK * MOE_BLOCK
    ends = jnp.cumsum(padded)
    starts = ends - padded
    nk = n * MOE_TOPK
    n_blocks = -(-nk // MOE_BLOCK) + MOE_EXPERTS
    rows = n_blocks * MOE_BLOCK
    n_pad = rows - nk
    e_iota = jnp.arange(MOE_EXPERTS, dtype=jnp.int32)
    dest = jnp.sum(jnp.where(idx[..., None] == e_iota, starts, 0), axis=-1) + rank
    blk_start = jnp.arange(n_blocks, dtype=jnp.int32) * MOE_BLOCK
    block_e = jnp.minimum(jnp.sum(ends[None, :] <= blk_start[:, None], axis=1), MOE_EXPERTS - 1).astype(jnp.int32)
    n_used = (ends[-1:] // MOE_BLOCK).astype(jnp.int32)
    pad = padded - counts
    cum_pad = jnp.cumsum(pad)
    m = jnp.arange(n_pad, dtype=jnp.int32)
    e_m = jnp.sum(cum_pad[None, :] <= m[:, None], axis=1)
    base = jnp.sum(jnp.where(jnp.minimum(e_m, MOE_EXPERTS - 1)[:, None] == e_iota,
                             starts + counts - (cum_pad - pad), 0), axis=1)
    pad_row = jnp.where(e_m < MOE_EXPERTS, base + m, ends[-1] + m - cum_pad[-1])
    tok = jnp.arange(nk, dtype=jnp.int32) // MOE_TOPK
    _, row_tok = lax.sort((jnp.concatenate([dest.reshape(-1), pad_row]).astype(jnp.int32),
                           jnp.concatenate([tok, jnp.zeros((n_pad,), jnp.int32)])), num_keys=1)
    h_flat = h.reshape(n, d)
    x_rows = h_flat[row_tok]
    y_rows = moe_experts(x_rows, block_e, n_used, w_gate, w_up, w_down)
    picked = y_rows[dest.reshape(n, MOE_TOPK).T].reshape(MOE_TOPK, b, rows_per_sample, d)
    return shared_resid(h, picked, wsel, xs, mods,
                        sh_gate.astype(BF16), sh_up.astype(BF16), sh_down.astype(BF16), n_tiles)


def _dwconv_body(x_ref, w_ref, b_ref, o_ref, *, width, act):
    chunk = ROW_TILE
    n_chunks = NTOK // chunk
    first_of_seq = (0, N_LAT_TILES)
    last_of_seq = (N_LAT_TILES - 1, n_chunks - 1)
    tc = x_ref.shape[-1]
    halo = 16
    row = lax.broadcasted_iota(jnp.int32, (chunk, tc), 0)
    zero_row = jnp.zeros((1, tc), F32)
    for c in range(n_chunks):
        r0 = c * chunk
        cur = x_ref[0, r0:r0 + chunk, :].astype(F32)
        if c in first_of_seq:
            prev_last = zero_row
        else:
            prev_last = x_ref[0, r0 - halo:r0, :].astype(F32)[halo - 1:halo, :]
        if c in last_of_seq:
            next0 = next1 = zero_row
        else:
            nxt = x_ref[0, r0 + chunk:r0 + chunk + halo, :].astype(F32)
            next0, next1 = nxt[0:1, :], nxt[1:2, :]
        xm1 = jnp.where(row == 0, prev_last, pltpu.roll(cur, 1, 0))
        xp1 = jnp.where(row == chunk - 1, next0, pltpu.roll(cur, chunk - 1, 0))
        y = w_ref[0:1, :] * xm1 + w_ref[1:2, :] * cur + w_ref[2:3, :] * xp1 + b_ref[...]
        if width == 4:
            xp2 = jnp.where(row == chunk - 2, next0,
                            jnp.where(row == chunk - 1, next1, pltpu.roll(cur, chunk - 2, 0)))
            y = y + w_ref[3:4, :] * xp2
        if act:
            y = _silu(y)
        o_ref[0, c * chunk:(c + 1) * chunk, :] = y.astype(o_ref.dtype)


def dwconv_stream(x, w, b, act, tc=256):
    bsz, nt, c = x.shape
    width = w.shape[0]
    return pl.pallas_call(
        functools.partial(_dwconv_body, width=width, act=act),
        name='dwconv',
        grid=(bsz, c // tc),
        in_specs=[pl.BlockSpec((1, nt, tc), lambda i, j: (i, 0, j)),
                  pl.BlockSpec((width, tc), lambda i, j: (0, j)),
                  pl.BlockSpec((1, tc), lambda i, j: (0, j))],
        out_specs=pl.BlockSpec((1, nt, tc), lambda i, j: (i, 0, j)),
        out_shape=jax.ShapeDtypeStruct(x.shape, BF16),
        compiler_params=_cparams(("parallel", "parallel")),
    )(x, w, b.reshape(1, c))


HY_FB = 512


def dft_matrices(n):
    k = jnp.arange(n, dtype=jnp.int32)[:, None]
    t = jnp.arange(n, dtype=jnp.int32)[None, :]
    ang = (2.0 * math.pi / (2 * n)) * ((k * t) % (2 * n)).astype(F32)
    fre = jnp.cos(ang)
    fim = -jnp.sin(ang)
    nyq = jnp.where(t % 2 == 0, 1.0, -1.0).astype(F32)
    fim = jnp.where(k == 0, nyq, fim)
    fwd = jnp.concatenate([fre, fim], axis=0)
    colscale = jnp.where(jnp.arange(2 * n) % n == 0, 0.5, 1.0) / n
    inv = fwd.T * colscale[None, :]
    return fwd, inv


def hyena_filter_taps(n, fw0, fb0, fw1, fb1, fw2, fb2, fw3, freq):
    pos = jnp.arange(n, dtype=F32)
    t = pos / max(n - 1, 1)
    bands = jnp.linspace(1e-4, HY_BANDS - 1, HY_BANDS, dtype=F32)
    ang = (2.0 * math.pi / n) * pos[:, None] * bands[None, :]
    feats = jnp.concatenate([t[:, None], jnp.cos(ang), -jnp.sin(ang)], axis=-1)
    h = jnp.sin(freq * (jnp.dot(feats, fw0, precision=HIGHEST) + fb0))
    h = jnp.sin(freq * (jnp.dot(h, fw1, precision=HIGHEST) + fb1))
    h = jnp.sin(freq * (jnp.dot(h, fw2, precision=HIGHEST) + fb2))
    h = pmatmul(h, fw3, exact=True).reshape(n, 2, HY_ORDER, D_HY)
    deltas = jnp.abs(jnp.linspace(math.log(HY_DECAY_PCT_LO) / HY_DECAY_TARGET,
                                  math.log(HY_DECAY_PCT_HI) / HY_DECAY_TARGET, D_HY, dtype=F32))
    h = h * jnp.exp(-t[:, None] * deltas)[:, None, None, :]
    h0 = h[:, 0]
    h1 = h[:, 1].at[0].set(0.0)
    norm = jnp.sum(jnp.abs(h0), axis=0, keepdims=True) + jnp.sum(jnp.abs(h1), axis=0, keepdims=True)
    h0 = (h0 / norm).reshape(n, HY_ORDER * D_HY)
    h1 = (h1 / norm).reshape(n, HY_ORDER * D_HY)
    return h0 + h1, h0 - h1


def _split_bf16(a):
    hi = a.astype(BF16)
    return hi, (a - hi.astype(F32)).astype(BF16)


def hyena_spectrum(fwd, hsum, hdiff, fb):
    n = hsum.shape[0]
    f_hi, f_lo = _split_bf16(fwd)

    def dft(h):
        h_hi, h_lo = _split_bf16(h)
        return pmatmul(f_hi, h_hi) + pmatmul(f_hi, h_lo) + pmatmul(f_lo, h_hi)

    a = dft(hsum)
    bm = dft(hdiff)
    sr = a[:n]
    si = bm[n:]
    nyq = a[n]
    first = (jnp.arange(n) == 0)[:, None]
    p = sr
    q = jnp.where(first, 0.0, si)
    s = jnp.where(first, nyq[None, :], sr)
    spec = jnp.stack([p, q, s], axis=0).reshape(3, n // fb, fb, HY_ORDER, D_HY)
    return spec.transpose(3, 1, 0, 2, 4), f_hi


def _hyena_body(u_ref, fre_ref, fim_ref, gre_ref, gim_ref, sp_ref, bias_ref, prev_ref, o_ref,
                vin, acc, *, nf):
    del prev_ref
    o = pl.program_id(1)
    f = pl.program_id(2)
    c = D_HY

    @pl.when((o == 0) & (f == 0))
    def _():
        vin[...] = u_ref[0, :, 0:c]

    @pl.when(f == 0)
    def _():
        acc[...] = jnp.zeros_like(acc)

    v = vin[...]
    vr = jnp.dot(fre_ref[...], v, preferred_element_type=F32)
    vi = jnp.dot(fim_ref[...], v, preferred_element_type=F32)
    p, q, s = sp_ref[0], sp_ref[1], sp_ref[2]
    zr = (vr * p - vi * q).astype(BF16)
    zi = (vr * q + vi * s).astype(BF16)
    acc[...] += (jnp.dot(gre_ref[...], zr, preferred_element_type=F32)
                 + jnp.dot(gim_ref[...], zi, preferred_element_type=F32))

    @pl.when((o == 0) & (f == nf - 1))
    def _():
        z = u_ref[0, :, c:2 * c].astype(F32) * (acc[...] + bias_ref[0:1, :] * vin[...].astype(F32))
        vin[...] = z.astype(BF16)

    @pl.when((o == 1) & (f == nf - 1))
    def _():
        y = u_ref[0, :, 2 * c:3 * c].astype(F32) * (acc[...] + bias_ref[1:2, :] * vin[...].astype(F32))
        o_ref[0] = y.astype(o_ref.dtype)


def hyena_long_conv(u, fwd_bf16, inv_bf16, spec, bias, n, row_block, prev_out):
    bsz = u.shape[0]
    fb = spec.shape[3]
    nf = n // fb
    out_shape = jax.ShapeDtypeStruct((bsz, NTOK, D_HY), BF16)
    if prev_out is None:
        prev_out = jnp.zeros(out_shape.shape, BF16)
    args = [u, fwd_bf16, fwd_bf16, inv_bf16, inv_bf16, spec, bias, prev_out]
    aliases = {7: 0}
    return pl.pallas_call(
        functools.partial(_hyena_body, nf=nf),
        name='hyena',
        grid=(bsz, HY_ORDER, nf),
        in_specs=[pl.BlockSpec((1, n, 3 * D_HY), lambda b, o, f: (b, row_block, 0)),
                  pl.BlockSpec((fb, n), lambda b, o, f: (f, 0)),
                  pl.BlockSpec((fb, n), lambda b, o, f: (nf + f, 0)),
                  pl.BlockSpec((n, fb), lambda b, o, f: (0, f)),
                  pl.BlockSpec((n, fb), lambda b, o, f: (0, nf + f)),
                  pl.BlockSpec((None, None, 3, fb, D_HY), lambda b, o, f: (o, f, 0, 0, 0)),
                  pl.BlockSpec((HY_ORDER, D_HY), lambda b, o, f: (0, 0)),
                  pl.BlockSpec(memory_space=pl.ANY)],
        out_specs=pl.BlockSpec((1, n, D_HY), lambda b, o, f: (b, row_block, 0)),
        out_shape=out_shape,
        scratch_shapes=[pltpu.VMEM((n, D_HY), BF16), pltpu.VMEM((n, D_HY), F32)],
        input_output_aliases=aliases,
        compiler_params=_cparams(("parallel", "arbitrary", "arbitrary"), VMEM_LIMIT_BIG),
    )(*args)


def hyena_mixer_stream(p_hy, conv_w, conv_b, filt, bias):
    u = dwconv_stream(p_hy, conv_w, conv_b, act=False)
    out = None
    for n, row_block in ((SEQ, 0), (CTX_LEN, SEQ // CTX_LEN)):
        fb = min(HY_FB, n)
        fwd, inv = dft_matrices(n)
        hsum, hdiff = hyena_filter_taps(n, *filt)
        spec, fwd_bf16 = hyena_spectrum(fwd, hsum, hdiff, fb)
        out = hyena_long_conv(u, fwd_bf16, inv.astype(BF16), spec, bias, n, row_block, out)
    return out


def _tri(n, kind):
    r = lax.broadcasted_iota(jnp.int32, (n, n), 0)
    c = lax.broadcasted_iota(jnp.int32, (n, n), 1)
    return (c <= r) if kind == 'lower' else (c >= r)


def _ssd_dir(xbc_ref, dt_ref, dtt_ref, bias_r, bias_c, a_r, a_c, st_ref, y_ref, *, d, reverse):
    q = SSD_CHUNK
    nh = SSD_H
    gw = SSD_HPG * SSD_P
    lower = _tri(q, 'lower')
    upper = _tri(q, 'upper')
    lower_f = lower.astype(F32)
    upper_f = upper.astype(F32)
    dt_col = _softplus(dt_ref[0] + bias_r)
    dt_row = _softplus(dtt_ref[0] + bias_c)
    da_col = dt_col * a_r
    da_row = dt_row * a_c
    if not reverse:
        acs_col = jnp.dot(lower_f, da_col, preferred_element_type=F32, precision=HIGHEST)
        acs_row = jnp.dot(da_row, upper_f, preferred_element_type=F32, precision=HIGHEST)
        mask = lower
        edge = q - 1
    else:
        acs_col = jnp.dot(upper_f, da_col, preferred_element_type=F32, precision=HIGHEST)
        acs_row = jnp.dot(da_row, lower_f, preferred_element_type=F32, precision=HIGHEST)
        mask = upper
        edge = 0
    h0 = d * nh
    hh = lax.broadcasted_iota(jnp.int32, (2 * nh, nh * SSD_P), 0)
    cc = lax.broadcasted_iota(jnp.int32, (2 * nh, nh * SSD_P), 1) // SSD_P
    expand = (hh == cc + h0).astype(F32)
    acs_c = jnp.dot(acs_col, expand, preferred_element_type=F32, precision=HIGHEST)
    dt_c = jnp.dot(dt_col, expand, preferred_element_type=F32, precision=HIGHEST)
    total_c = acs_c[edge:edge + 1, :]
    e_in_c = jnp.exp(acs_c)
    w_end_c = jnp.exp(total_c - acs_c) * dt_c
    dec_c = jnp.exp(total_c)
    xs = xbc_ref[0, :, 0:D_SSM]
    xs_f = xs.astype(F32)
    for g in range(SSD_G):
        bm = xbc_ref[0, :, D_SSM + g * SSD_N:D_SSM + (g + 1) * SSD_N]
        cm = xbc_ref[0, :, D_SSM + SSD_G * SSD_N + g * SSD_N:D_SSM + SSD_G * SSD_N + (g + 1) * SSD_N]
        cb = lax.dot_general(cm, bm, (((1,), (1,)), ((), ())), preferred_element_type=F32)
        lws = []
        for k in range(SSD_HPG):
            h = h0 + g * SSD_HPG + k
            seg = acs_col[:, h:h + 1] - acs_row[h:h + 1, :]
            decay = jnp.exp(jnp.where(mask, seg, -jnp.inf))
            lws.append((cb * decay * dt_row[h:h + 1, :]).astype(BF16))
        lw = jnp.concatenate(lws, axis=1)
        xg = xs[:, g * gw:(g + 1) * gw]
        rb = lax.broadcasted_iota(jnp.int32, (SSD_HPG * q, gw), 0) // q
        cbk = lax.broadcasted_iota(jnp.int32, (SSD_HPG * q, gw), 1) // SSD_P
        x_bd = jnp.where(rb == cbk, jnp.concatenate([xg] * SSD_HPG, axis=0), jnp.zeros((), BF16))
        y_in = jnp.dot(lw, x_bd, preferred_element_type=F32)
        st = st_ref[g]
        y_st = jnp.dot(cm, st.astype(BF16), preferred_element_type=F32) * e_in_c[:, g * gw:(g + 1) * gw]
        y_ref[0, :, g * gw:(g + 1) * gw] = y_in + y_st
        xw = (xs_f[:, g * gw:(g + 1) * gw] * w_end_c[:, g * gw:(g + 1) * gw]).astype(BF16)
        upd = lax.dot_general(bm, xw, (((0,), (0,)), ((), ())), preferred_element_type=F32)
        st_ref[g] = st * dec_c[:, g * gw:(g + 1) * gw] + upd


def _ssd_body(xf_ref, dtf_ref, dttf_ref, xb_ref, dtb_ref, dttb_ref, bias_r, bias_c, a_r, a_c,
              yf_ref, yb_ref, stf, stb):
    @pl.when(pl.program_id(1) == 0)
    def _():
        stf[...] = jnp.zeros_like(stf)
        stb[...] = jnp.zeros_like(stb)

    _ssd_dir(xf_ref, dtf_ref, dttf_ref, bias_r[...], bias_c[...], a_r[...], a_c[...], stf, yf_ref,
             d=0, reverse=False)
    _ssd_dir(xb_ref, dtb_ref, dttb_ref, bias_r[...], bias_c[...], a_r[...], a_c[...], stb, yb_ref,
             d=1, reverse=True)


def ssd_scan(xbc, dt, dt_bias, a_log):
    bsz = xbc.shape[0]
    nc = NTOK // SSD_CHUNK
    nlat = SEQ // SSD_CHUNK
    dtt = jnp.swapaxes(dt, 1, 2)
    fwd_chunk = lambda s: (s + nlat) % nc
    bwd_chunk = lambda s: nc - 1 - s
    a = -jnp.exp(a_log.astype(F32)).reshape(1, 2 * SSD_H)
    bias = dt_bias.astype(F32).reshape(1, 2 * SSD_H)
    x_spec = lambda cm: pl.BlockSpec((1, SSD_CHUNK, SSD_XBC), lambda b, s: (b, cm(s), 0))
    dt_spec = lambda cm: pl.BlockSpec((1, SSD_CHUNK, 2 * SSD_H), lambda b, s: (b, cm(s), 0))
    dtt_spec = lambda cm: pl.BlockSpec((1, 2 * SSD_H, SSD_CHUNK), lambda b, s: (b, 0, cm(s)))
    y_spec = lambda cm: pl.BlockSpec((1, SSD_CHUNK, D_SSM), lambda b, s: (b, cm(s), 0))
    row = pl.BlockSpec((1, 2 * SSD_H), lambda b, s: (0, 0))
    col = pl.BlockSpec((2 * SSD_H, 1), lambda b, s: (0, 0))
    y_shape = jax.ShapeDtypeStruct((bsz, NTOK, D_SSM), F32)
    gw = SSD_HPG * SSD_P
    return pl.pallas_call(
        _ssd_body,
        name='ssd_scan',
        grid=(bsz, nc),
        in_specs=[x_spec(fwd_chunk), dt_spec(fwd_chunk), dtt_spec(fwd_chunk),
                  x_spec(bwd_chunk), dt_spec(bwd_chunk), dtt_spec(bwd_chunk),
                  row, col, row, col],
        out_specs=[y_spec(fwd_chunk), y_spec(bwd_chunk)],
        out_shape=[y_shape, y_shape],
        scratch_shapes=[pltpu.VMEM((SSD_G, SSD_N, gw), F32), pltpu.VMEM((SSD_G, SSD_N, gw), F32)],
        compiler_params=_cparams(("parallel", "arbitrary")),
    )(xbc, dt, dtt, xbc, dt, dtt, bias, bias.reshape(-1, 1), a, a.reshape(-1, 1))


def _ssd_merge_body(yf_ref, yb_ref, xbc_ref, z_ref, d_ref, nw_ref, o_ref):
    xs = xbc_ref[0, :, 0:D_SSM].astype(F32)
    z = z_ref[0].astype(F32)
    g = (yf_ref[0] + yb_ref[0] + d_ref[...] * xs) * _silu(z)
    gw = D_SSM // SSD_G
    for k in range(SSD_G):
        gk = g[:, k * gw:(k + 1) * gw]
        ms = jnp.mean(gk * gk, axis=-1, keepdims=True)
        o_ref[0, :, k * gw:(k + 1) * gw] = (gk * lax.rsqrt(ms + NORM_EPS)
                                            * nw_ref[:, k * gw:(k + 1) * gw]).astype(o_ref.dtype)


def ssd_merge(yf, yb, xbc, z, d_skip, norm_w):
    bsz = yf.shape[0]
    tile = lambda w: pl.BlockSpec((1, ROW_TILE, w), lambda i, j: (i, j, 0))
    vec = pl.BlockSpec((1, D_SSM), lambda i, j: (0, 0))
    d_chan = jnp.repeat(d_skip.astype(F32), SSD_P).reshape(1, D_SSM)
    return pl.pallas_call(
        _ssd_merge_body,
        name='ssd_merge',
        grid=(bsz, N_ROW_TILES),
        in_specs=[tile(D_SSM), tile(D_SSM), tile(SSD_XBC), tile(D_SSM), vec, vec],
        out_specs=tile(D_SSM),
        out_shape=jax.ShapeDtypeStruct((bsz, NTOK, D_SSM), BF16),
        compiler_params=_cparams(("parallel", "parallel")),
    )(yf, yb, xbc, z, d_chan, norm_w.reshape(1, D_SSM))


def even_layer_mixer(xs, mods, norm_w, w_in, w_out, hy_conv_w, hy_conv_b, hy_filt, hy_bias,
                     ssd_conv_w, ssd_conv_b, ssd_dt_bias, ssd_a_log, ssd_d, ssd_norm_w):
    bsz = xs.shape[0]
    h = norm_mod(xs, norm_w, mods, 0, 1)
    splits = ((0, HY_IN), (HY_IN, D_SSM), (HY_IN + D_SSM, SSD_XBC), (HY_IN + D_SSM + SSD_XBC, 2 * SSD_H))
    p_hy, z, xbc_raw, dt = mm_split(h.reshape(bsz * NTOK, D_MODEL), w_in.astype(BF16), splits,
                                    (BF16, BF16, BF16, F32))
    to3 = lambda a: a.reshape(bsz, NTOK, a.shape[-1])
    y_hy = hyena_mixer_stream(to3(p_hy), hy_conv_w, hy_conv_b, hy_filt, hy_bias)
    xbc = dwconv_stream(to3(xbc_raw), ssd_conv_w, ssd_conv_b, act=True)
    yf, yb = ssd_scan(xbc, to3(dt), ssd_dt_bias, ssd_a_log)
    s = ssd_merge(yf, yb, xbc, to3(z), ssd_d, ssd_norm_w)
    wo = w_out.astype(BF16)
    return mm_resid([y_hy, s], [wo[:D_HY], wo[D_HY:]], xs, mods, 2, N_ROW_TILES)


GLA_QK = GLA_H * GLA_DK
GLA_V = GLA_H * GLA_DV


def _gla_dir(qkv_ref, lr_ref, gw_ref, gb_ref, st_ref, o_ref, *, d, reverse):
    q = GLA_CHUNK
    tri = _tri(q, 'upper' if reverse else 'lower')
    edge = 0 if reverse else q - 1
    lr = lr_ref[0, :, d * GLA_RANK:(d + 1) * GLA_RANK]
    logit = jnp.dot(lr, gw_ref[d], preferred_element_type=F32, precision=HIGHEST) + gb_ref[d:d + 1, :]
    log_g = -_softplus(-logit) * (1.0 / GLA_GATE_NORM)
    gcum = jnp.dot(tri.astype(F32), log_g, preferred_element_type=F32, precision=HIGHEST)
    total = gcum[edge:edge + 1, :]
    qf = qkv_ref[0, :, 0:GLA_QK].astype(F32)
    kf = qkv_ref[0, :, GLA_QK:2 * GLA_QK].astype(F32)
    v = qkv_ref[0, :, 2 * GLA_QK:2 * GLA_QK + GLA_V]
    qg = (qf * (GLA_DK ** -0.5) * jnp.exp(gcum)).astype(BF16)
    kg = (kf * jnp.exp(-gcum)).astype(BF16)
    kw = (kf * jnp.exp(total - gcum)).astype(BF16)
    rb = lax.broadcasted_iota(jnp.int32, (GLA_H * q, GLA_QK), 0) // q
    cb = lax.broadcasted_iota(jnp.int32, (GLA_H * q, GLA_QK), 1) // GLA_DK
    k_bd = jnp.where(rb == cb, jnp.concatenate([kg] * GLA_H, axis=0), jnp.zeros((), BF16))
    att = lax.dot_general(qg, k_bd, (((1,), (1,)), ((), ())), preferred_element_type=F32)
    i_i = lax.broadcasted_iota(jnp.int32, (q, GLA_H * q), 0)
    j_i = lax.broadcasted_iota(jnp.int32, (q, GLA_H * q), 1) % q
    keep = (j_i >= i_i) if reverse else (j_i <= i_i)
    att = jnp.where(keep, att, 0.0).astype(BF16)
    rv = lax.broadcasted_iota(jnp.int32, (GLA_H * q, GLA_V), 0) // q
    cv = lax.broadcasted_iota(jnp.int32, (GLA_H * q, GLA_V), 1) // GLA_DV
    v_bd = jnp.where(rv == cv, jnp.concatenate([v] * GLA_H, axis=0), jnp.zeros((), BF16))
    st = st_ref[...]
    o_in = jnp.dot(att, v_bd, preferred_element_type=F32)
    o_st = lax.dot_general(qg, st.astype(BF16), (((1,), (1,)), ((), ())), preferred_element_type=F32)
    o_ref[0] = o_in + o_st
    upd = lax.dot_general(v, kw, (((0,), (0,)), ((), ())), preferred_element_type=F32)
    rs = lax.broadcasted_iota(jnp.int32, (GLA_V, GLA_QK), 0) // GLA_DV
    cs = lax.broadcasted_iota(jnp.int32, (GLA_V, GLA_QK), 1) // GLA_DK
    st_ref[...] = st * jnp.exp(total) + jnp.where(rs == cs, upd, 0.0)


def _gla_body(qf_ref, lf_ref, qb_ref, lb_ref, gw_ref, gb_ref, of_ref, ob_ref, stf, stb):
    @pl.when(pl.program_id(1) == 0)
    def _():
        stf[...] = jnp.zeros_like(stf)
        stb[...] = jnp.zeros_like(stb)

    _gla_dir(qf_ref, lf_ref, gw_ref, gb_ref, stf, of_ref, d=0, reverse=False)
    _gla_dir(qb_ref, lb_ref, gw_ref, gb_ref, stb, ob_ref, d=1, reverse=True)


def gla_scan(qkv, lr, gate_w, gate_b):
    bsz = qkv.shape[0]
    nc = NTOK // GLA_CHUNK
    nlat = SEQ // GLA_CHUNK
    fwd_chunk = lambda s: (s + nlat) % nc
    bwd_chunk = lambda s: nc - 1 - s
    q_spec = lambda cm: pl.BlockSpec((1, GLA_CHUNK, qkv.shape[-1]), lambda b, s: (b, cm(s), 0))
    l_spec = lambda cm: pl.BlockSpec((1, GLA_CHUNK, 2 * GLA_RANK), lambda b, s: (b, cm(s), 0))
    o_spec = lambda cm: pl.BlockSpec((1, GLA_CHUNK, GLA_V), lambda b, s: (b, cm(s), 0))
    o_shape = jax.ShapeDtypeStruct((bsz, NTOK, GLA_V), F32)
    return pl.pallas_call(
        _gla_body,
        name='gla_scan',
        grid=(bsz, nc),
        in_specs=[q_spec(fwd_chunk), l_spec(fwd_chunk), q_spec(bwd_chunk), l_spec(bwd_chunk),
                  pl.BlockSpec((2, GLA_RANK, GLA_QK), lambda b, s: (0, 0, 0)),
                  pl.BlockSpec((2, GLA_QK), lambda b, s: (0, 0))],
        out_specs=[o_spec(fwd_chunk), o_spec(bwd_chunk)],
        out_shape=[o_shape, o_shape],
        scratch_shapes=[pltpu.VMEM((GLA_V, GLA_QK), F32), pltpu.VMEM((GLA_V, GLA_QK), F32)],
        compiler_params=_cparams(("parallel", "arbitrary")),
    )(qkv, lr, qkv, lr, gate_w.astype(F32), gate_b.astype(F32))


def _gla_merge_body(of_ref, ob_ref, r_ref, nw_ref, o_ref):
    o = of_ref[0] + ob_ref[0]
    r = r_ref[0].astype(F32)
    for h in range(GLA_H):
        sl = slice(h * GLA_DV, (h + 1) * GLA_DV)
        oh = o[:, sl]
        ms = jnp.mean(oh * oh, axis=-1, keepdims=True)
        o_ref[0, :, sl] = (oh * lax.rsqrt(ms + NORM_EPS) * nw_ref[:, sl] * _silu(r[:, sl])).astype(o_ref.dtype)


def gla_merge_stream(of, ob, r, norm_w):
    bsz = of.shape[0]
    tile = pl.BlockSpec((1, ROW_TILE, GLA_V), lambda i, j: (i, j, 0))
    return pl.pallas_call(
        _gla_merge_body,
        name='gla_merge',
        grid=(bsz, N_LAT_TILES),
        in_specs=[tile, tile, tile, pl.BlockSpec((1, GLA_V), lambda i, j: (0, 0))],
        out_specs=tile,
        out_shape=jax.ShapeDtypeStruct((bsz, SEQ, GLA_V), BF16),
        compiler_params=_cparams(("parallel", "parallel")),
    )(of, ob, r, norm_w.reshape(1, GLA_V))


RG_TILE = 8


def _gelu_tanh(x):
    return 0.5 * x * (1.0 + jnp.tanh(math.sqrt(2.0 / math.pi) * (x + 0.044715 * x * x * x)))


def _rg_scan_block(a_s, x_s, h_s, base, carry, reverse):
    n_tiles = ROW_TILE // RG_TILE
    row = lax.broadcasted_iota(jnp.int32, (RG_TILE, D_RG), 0)

    def tile_step(i, h_prev):
        t = (n_tiles - 1 - i) if reverse else i
        r0 = pl.multiple_of(t * RG_TILE, RG_TILE)
        a = a_s[pl.ds(r0, RG_TILE), :]
        x = x_s[pl.ds(r0, RG_TILE), :]
        for s in (1, 2, 4):
            if reverse:
                ok = row < RG_TILE - s
                shift = RG_TILE - s
            else:
                ok = row >= s
                shift = s
            a_sh = jnp.where(ok, pltpu.roll(a, shift, 0), 1.0)
            x_sh = jnp.where(ok, pltpu.roll(x, shift, 0), 0.0)
            x = a * x_sh + x
            a = a * a_sh
        h = x + a * h_prev
        h_s[pl.ds(base + r0, RG_TILE), :] = h
        edge = 0 if reverse else RG_TILE - 1
        return jnp.broadcast_to(h[edge:edge + 1, :], (RG_TILE, D_RG))

    return lax.fori_loop(0, n_tiles, tile_step, carry)


def _rglru_body(u_ref, g_ref, w_ref, b_ref, c_ref, o_ref, hf_s, a_s, x_s, hb_s):
    n_blocks = NTOK // ROW_TILE
    fwd_order = list(range(N_LAT_TILES, n_blocks)) + list(range(N_LAT_TILES))
    bwd_order = list(range(n_blocks - 1, N_LAT_TILES - 1, -1)) + list(range(N_LAT_TILES - 1, -1, -1))

    def gates(blk, d):
        ub = u_ref[0, blk * ROW_TILE:(blk + 1) * ROW_TILE, :]
        z = jnp.dot(ub, w_ref[:, 2 * d * D_RG:2 * (d + 1) * D_RG], preferred_element_type=F32)
        z = z + b_ref[:, 2 * d * D_RG:2 * (d + 1) * D_RG]
        r = 1.0 / (1.0 + jnp.exp(-z[:, :D_RG]))
        i = 1.0 / (1.0 + jnp.exp(-z[:, D_RG:]))
        a = jnp.exp(c_ref[d:d + 1, :] * r)
        a_s[...] = a
        x_s[...] = jnp.sqrt(1.0 - a * a) * i * ub.astype(F32)

    carry = jnp.zeros((RG_TILE, D_RG), F32)
    for blk in fwd_order:
        gates(blk, 0)
        carry = _rg_scan_block(a_s, x_s, hf_s, blk * ROW_TILE, carry, reverse=False)
    carry = jnp.zeros((RG_TILE, D_RG), F32)
    for blk in bwd_order:
        gates(blk, 1)
        carry = _rg_scan_block(a_s, x_s, hb_s, 0, carry, reverse=True)
        rows = slice(blk * ROW_TILE, (blk + 1) * ROW_TILE)
        gate = g_ref[0, rows, :].astype(F32)
        o_ref[0, rows, :] = ((hf_s[rows, :] + hb_s[...]) * _gelu_tanh(gate)).astype(o_ref.dtype)


def rglru_stream(u, gate, w_a, b_a, w_x, b_x, lam):
    bsz = u.shape[0]
    eye = jnp.eye(RG_BLOCKS, dtype=F32)
    dense = lambda w: jnp.einsum('nio,nm->nimo', w, eye).reshape(D_RG, D_RG)
    w_cat = jnp.concatenate([dense(w_a[0]), dense(w_x[0]), dense(w_a[1]), dense(w_x[1])], axis=1).astype(BF16)
    b_cat = jnp.concatenate([b_a[0], b_x[0], b_a[1], b_x[1]]).astype(F32).reshape(1, 4 * D_RG)
    c = -RG_C * jax.nn.softplus(-lam.astype(F32))
    seq = pl.BlockSpec((1, NTOK, D_RG), lambda i: (i, 0, 0))
    return pl.pallas_call(
        _rglru_body,
        name='rglru',
        grid=(bsz,),
        in_specs=[seq, seq,
                  pl.BlockSpec((D_RG, 4 * D_RG), lambda i: (0, 0)),
                  pl.BlockSpec((1, 4 * D_RG), lambda i: (0, 0)),
                  pl.BlockSpec((2, D_RG), lambda i: (0, 0))],
        out_specs=seq,
        out_shape=jax.ShapeDtypeStruct((bsz, NTOK, D_RG), BF16),
        scratch_shapes=[pltpu.VMEM((NTOK, D_RG), F32), pltpu.VMEM((ROW_TILE, D_RG), F32),
                        pltpu.VMEM((ROW_TILE, D_RG), F32), pltpu.VMEM((ROW_TILE, D_RG), F32)],
        compiler_params=_cparams(("parallel",)),
    )(u, gate, w_cat, b_cat, c)


def odd_layer_mixer_pallas(xs, mods, norm_w, w_in, w_out, gla_args, rg_args):
    bsz = xs.shape[0]
    gate_w, gate_b, gla_norm_w = gla_args
    rg_conv_w, rg_conv_b, w_a, b_a, w_x, b_x, lam = rg_args
    h = norm_mod(xs, norm_w, mods, 0, 1)
    h = jnp.concatenate([to_col_major(h[:, :SEQ]), h[:, SEQ:]], axis=1)
    nqk, nv = GLA_QK, GLA_V
    r0 = 2 * nqk + nv + 2 * GLA_RANK
    w = jnp.concatenate([w_in[:, :2 * nqk + nv], w_in[:, r0:r0 + nv], w_in[:, GLA_IN:],
                         w_in[:, 2 * nqk + nv:r0]], axis=1).astype(BF16)
    qkv_w = 2 * nqk + nv
    splits = ((0, qkv_w), (qkv_w, nv), (qkv_w + nv, D_RG), (qkv_w + nv + D_RG, D_RG),
              (qkv_w + nv + 2 * D_RG, 2 * GLA_RANK))
    qkv, r, u_raw, gate, lr = mm_split(h.reshape(bsz * NTOK, D_MODEL), w, splits, (BF16, BF16, BF16, BF16, F32))
    to3 = lambda a: a.reshape(bsz, NTOK, a.shape[-1])
    of, ob = gla_scan(to3(qkv), to3(lr), gate_w, gate_b.reshape(2, GLA_QK))
    a_l = gla_merge_stream(of, ob, to3(r), gla_norm_w)
    u = dwconv_stream(to3(u_raw), rg_conv_w, rg_conv_b, act=False)
    r_l = rglru_stream(u, to3(gate), w_a, b_a, w_x, b_x, lam)[:, :SEQ]
    wo = w_out.astype(BF16)
    return mm_resid([from_col_major(a_l), from_col_major(r_l)], [wo[:GLA_V], wo[GLA_V:]], xs, mods, 2, N_LAT_TILES)


def rms_norm(x, w):
    xf = x.astype(F32)
    y = xf * lax.rsqrt(jnp.mean(jnp.square(xf), axis=-1, keepdims=True) + NORM_EPS)
    return y.astype(x.dtype) * w


def dwconv(x, w, b):
    y = lax.conv_general_dilated(x, w[:, None, :].astype(x.dtype), window_strides=(1,), padding='SAME',
                                 dimension_numbers=('NWC', 'WIO', 'NWC'), feature_group_count=x.shape[-1])
    return y + b.astype(x.dtype)


def maybe_flip(a, rev):
    return jnp.flip(a, axis=1) if rev else a


def to_col_major(x):
    b, n, d = x.shape
    rows = n // GRID_W
    return x.reshape(b, rows, GRID_W, d).transpose(0, 2, 1, 3).reshape(b, n, d)


def from_col_major(x):
    b, n, d = x.shape
    rows = n // GRID_W
    return x.reshape(b, GRID_W, rows, d).transpose(0, 2, 1, 3).reshape(b, n, d)


def gla_inputs(p, gate_w, gate_b):
    b, n, _ = p.shape
    nqk, nv = GLA_H * GLA_DK, GLA_H * GLA_DV
    q = p[..., :nqk].reshape(b, n, GLA_H, GLA_DK) * GLA_DK ** -0.5
    k = p[..., nqk:2 * nqk].reshape(b, n, GLA_H, GLA_DK)
    v = p[..., 2 * nqk:2 * nqk + nv].reshape(b, n, GLA_H, GLA_DV)
    lr = p[..., 2 * nqk + nv:2 * nqk + nv + 2 * GLA_RANK].reshape(b, n, 2, GLA_RANK)
    r = p[..., 2 * nqk + nv + 2 * GLA_RANK:]
    logit = jnp.einsum('bler,erk->blek', lr, gate_w) + gate_b
    log_g = (jax.nn.log_sigmoid(logit.astype(F32)) / GLA_GATE_NORM).reshape(b, n, 2, GLA_H, GLA_DK)
    return q, k, v, log_g, r


def gla_states(k, v, log_g, s0):
    b, n = k.shape[:2]
    nc = n // GLA_CHUNK
    kc = k.reshape(b, nc, GLA_CHUNK, GLA_H, GLA_DK)
    vc = v.reshape(b, nc, GLA_CHUNK, GLA_H, GLA_DV)
    gcum = jnp.cumsum(log_g.reshape(b, nc, GLA_CHUNK, GLA_H, GLA_DK), axis=2)
    states = jnp.einsum('bcqhd,bcqhv->bchdv', kc * jnp.exp(gcum[:, :, -1:] - gcum), vc)
    chunk_decay = jnp.exp(gcum[:, :, -1])

    def step(s, inp):
        st, dcy = inp
        return dcy[..., None] * s + st, s

    s_fin, s_prev = lax.scan(step, s0, (jnp.moveaxis(states, 1, 0), jnp.moveaxis(chunk_decay, 1, 0)))
    return jnp.moveaxis(s_prev, 0, 1), s_fin


def gla_output(q, k, v, log_g, s_prev):
    b, n = q.shape[:2]
    nc = n // GLA_CHUNK
    qc = q.reshape(b, nc, GLA_CHUNK, GLA_H, GLA_DK)
    kc = k.reshape(b, nc, GLA_CHUNK, GLA_H, GLA_DK)
    vc = v.reshape(b, nc, GLA_CHUNK, GLA_H, GLA_DV)
    gcum = jnp.cumsum(log_g.reshape(b, nc, GLA_CHUNK, GLA_H, GLA_DK), axis=2)
    qg = qc * jnp.exp(gcum)
    kg = kc * jnp.exp(-gcum)
    mask = jnp.tril(jnp.ones((GLA_CHUNK, GLA_CHUNK), bool))
    att = jnp.where(mask, jnp.einsum('bcihd,bcjhd->bchij', qg, kg), 0.0)
    o = jnp.einsum('bchij,bcjhv->bcihv', att, vc) + jnp.einsum('bcihd,bchdv->bcihv', qg, s_prev)
    return o.reshape(b, n, GLA_H, GLA_DV)


def gla_merge(os_, r, norm_w):
    b, n = r.shape[:2]
    o = rms_norm(os_[0] + os_[1], norm_w.reshape(GLA_H, GLA_DV))
    return o.reshape(b, n, GLA_H * GLA_DV) * jax.nn.silu(r)


def gla_mixer(p_c, p_l, gate_w, gate_b, norm_w):
    q_c, k_c, v_c, g_c, r_c = gla_inputs(p_c, gate_w, gate_b)
    q_l, k_l, v_l, g_l, r_l = gla_inputs(p_l, gate_w, gate_b)
    s0 = jnp.zeros((p_c.shape[0], GLA_H, GLA_DK, GLA_DV), F32)
    os_l = []
    for d, rev in enumerate((False, True)):
        f = functools.partial(maybe_flip, rev=rev)
        _, sf_c = gla_states(f(k_c), f(v_c), f(g_c[:, :, d]), s0)
        sp_l, _ = gla_states(f(k_l), f(v_l), f(g_l[:, :, d]), sf_c)
        os_l.append(f(gla_output(f(q_l), f(k_l), f(v_l), f(g_l[:, :, d]), sp_l)))
    return gla_merge(os_l, r_l, norm_w)


def rglru_inputs(p, conv_w, conv_b, w_a, b_a, w_x, b_x, lam):
    b, n, _ = p.shape
    u = dwconv(p[..., :D_RG], conv_w, conv_b)
    ub = u.reshape(b, n, RG_BLOCKS, RG_BW)
    r = jax.nn.sigmoid((jnp.einsum('blni,enio->bleno', ub, w_a).reshape(b, n, 2, D_RG) + b_a).astype(F32))
    i = jax.nn.sigmoid((jnp.einsum('blni,enio->bleno', ub, w_x).reshape(b, n, 2, D_RG) + b_x).astype(F32))
    log_a = -RG_C * jax.nn.softplus(-lam.astype(F32)) * r
    x_in = jnp.sqrt(-jnp.expm1(2.0 * log_a)) * i * u[:, :, None, :].astype(F32)
    return p[..., D_RG:], jnp.exp(log_a), x_in


def lru_scan(a, u, h0):
    u = u.at[:, 0].add(a[:, 0] * h0)

    def combine(lhs, rhs):
        a1, b1 = lhs
        a2, b2 = rhs
        return a1 * a2, a2 * b1 + b2

    return lax.associative_scan(combine, (a, u), axis=1)[1]


def rglru_mixer(p_c, p_l, conv_w, conv_b, w_a, b_a, w_x, b_x, lam):
    gb_c, a_c, u_c = rglru_inputs(p_c, conv_w, conv_b, w_a, b_a, w_x, b_x, lam)
    gb_l, a_l, u_l = rglru_inputs(p_l, conv_w, conv_b, w_a, b_a, w_x, b_x, lam)
    h0 = jnp.zeros((p_c.shape[0], D_RG), F32)
    hs_l = []
    for d, rev in enumerate((False, True)):
        f = functools.partial(maybe_flip, rev=rev)
        h_c = f(lru_scan(f(a_c[:, :, d]), f(u_c[:, :, d]), h0))
        h_end = h_c[:, 0] if rev else h_c[:, -1]
        hs_l.append(f(lru_scan(f(a_l[:, :, d]), f(u_l[:, :, d]), h_end)))
    return (hs_l[0] + hs_l[1]) * jax.nn.gelu(gb_l.astype(F32))


def odd_layer_mixer(xs, mods, norm_w, w_in, w_out, gla_args, rg_args):
    bsz = xs.shape[0]
    h = norm_mod(xs, norm_w, mods, 0, 1)
    h = jnp.concatenate([to_col_major(h[:, :SEQ]), h[:, SEQ:]], axis=1)
    p = pmatmul(h.reshape(bsz * NTOK, D_MODEL), w_in.astype(BF16)).reshape(bsz, NTOK, OD_IN)
    p_l, p_c = p[:, :SEQ], p[:, SEQ:]
    a_l = gla_mixer(p_c[..., :GLA_IN], p_l[..., :GLA_IN], *gla_args)
    r_l = rglru_mixer(p_c[..., GLA_IN:], p_l[..., GLA_IN:], *rg_args)
    mix = from_col_major(jnp.concatenate([a_l, r_l], axis=-1)).astype(BF16)
    return mm_resid([mix], [w_out.astype(BF16)], xs, mods, 2, N_LAT_TILES)


def kernel(x, c, ctx, c_ctx, ada_w, ada_b, norm1_w, norm2_w, ev_w_in, ev_w_out, hy_conv_w, hy_conv_b, hy_fw0, hy_fb0, hy_fw1, hy_fb1, hy_fw2, hy_fb2, hy_fw3, hy_freq, hy_bias, ssd_conv_w, ssd_conv_b, ssd_dt_bias, ssd_a_log, ssd_d, ssd_norm_w, od_w_in, od_w_out, gla_gate_w, gla_gate_b, gla_norm_w, rg_conv_w, rg_conv_b, rg_w_a, rg_b_a, rg_w_x, rg_b_x, rg_lambda, router_w, router_b, moe_w_gate, moe_w_up, moe_w_down, sh_w_gate, sh_w_up, sh_w_down, final_norm_w):
    xs = jnp.concatenate([x, ctx], axis=1)
    for i in range(DEPTH):
        last = i == DEPTH - 1
        j = i // 2
        mods = adaln_table(c, c_ctx, ada_w[i], ada_b[i])
        if i % 2 == 0:
            hy_filt = (hy_fw0[j], hy_fb0[j], hy_fw1[j], hy_fb1[j], hy_fw2[j], hy_fb2[j], hy_fw3[j], hy_freq[j])
            xs = even_layer_mixer(xs, mods, norm1_w[i], ev_w_in[j], ev_w_out[j], hy_conv_w[j], hy_conv_b[j],
                                  hy_filt, hy_bias[j], ssd_conv_w[j], ssd_conv_b[j], ssd_dt_bias[j],
                                  ssd_a_log[j], ssd_d[j], ssd_norm_w[j])
        else:
            gla_args = (gla_gate_w[j], gla_gate_b[j], gla_norm_w[j])
            rg_args = (rg_conv_w[j], rg_conv_b[j], rg_w_a[j], rg_b_a[j], rg_w_x[j], rg_b_x[j], rg_lambda[j])
            xs = odd_layer_mixer_pallas(xs, mods, norm1_w[i], od_w_in[j], od_w_out[j], gla_args, rg_args)
        n_tiles = N_LAT_TILES if last else N_ROW_TILES
        xs = moe_layer(xs, norm2_w[i], mods, router_w[i], router_b[i], moe_w_gate[i], moe_w_up[i],
                       moe_w_down[i], sh_w_gate[i], sh_w_up[i], sh_w_down[i], n_tiles)
    return final_norm(xs, final_norm_w)
```

```python
import functools
import math

import jax
import jax.numpy as jnp
from jax import lax
from jax.experimental import pallas as pl
from jax.experimental.pallas import tpu as pltpu

D_MODEL = 1024
BATCH = 16
SEQ = 2048
DEPTH = 2

CTX_LEN = 256
GRID_W = 64
NORM_EPS = 1e-6

D_HY = D_MODEL // 2
HY_ORDER = 2
HY_SHORT = 3
HY_BANDS = 16
HY_EMB = 1 + 2 * HY_BANDS
HY_FF = 64
HY_DECAY_PCT_LO = 0.3
HY_DECAY_PCT_HI = 1.5
HY_DECAY_TARGET = 1e-2
HY_IN = 3 * D_HY

D_SSM = D_MODEL // 2
SSD_P = 64
SSD_H = D_SSM // SSD_P
SSD_G = 2
SSD_HPG = SSD_H // SSD_G
SSD_N = 128
SSD_CONV = 4
SSD_CHUNK = 128
SSD_XBC = D_SSM + 2 * SSD_G * SSD_N
SSD_IN = D_SSM + SSD_XBC + 2 * SSD_H
EV_IN = HY_IN + SSD_IN
EV_MIX = D_HY + D_SSM

GLA_H = 4
GLA_DV = (D_MODEL // 2) // GLA_H
GLA_DK = GLA_DV // 2
GLA_RANK = 16
GLA_GATE_NORM = 16.0
GLA_CHUNK = 64
GLA_IN = 2 * GLA_H * GLA_DK + 2 * GLA_H * GLA_DV + 2 * GLA_RANK

D_RG = D_MODEL // 2
RG_BLOCKS = 8
RG_BW = D_RG // RG_BLOCKS
RG_CONV = 4
RG_C = 8.0
RG_IN = 2 * D_RG
OD_IN = GLA_IN + RG_IN
OD_MIX = GLA_H * GLA_DV + D_RG

MOE_EXPERTS = 64
MOE_TOPK = 8
MOE_D_EXPERT = 256
MOE_D_SHARED = 256
MOE_SCALE = 2.5
MOE_BLOCK = 256

F32 = jnp.float32
BF16 = jnp.bfloat16
HIGHEST = lax.Precision.HIGHEST

NTOK = SEQ + CTX_LEN
ROW_TILE = 256
N_ROW_TILES = NTOK // ROW_TILE
N_LAT_TILES = SEQ // ROW_TILE

VMEM_LIMIT = 48 * 1024 * 1024
VMEM_LIMIT_BIG = 56 * 1024 * 1024


def _cparams(sem, limit=VMEM_LIMIT):
    return pltpu.CompilerParams(dimension_semantics=sem, vmem_limit_bytes=limit)


def _pick_tile(n, pref):
    t = min(n, pref)
    while n % t:
        t //= 2
    return t


def _silu(x):
    return x / (1.0 + jnp.exp(-x))


def _softplus(x):
    return jnp.maximum(x, 0.0) + jnp.log(1.0 + jnp.exp(-jnp.abs(x)))


def _mm_bf16_body(a_ref, w_ref, o_ref):
    o_ref[...] = jnp.dot(a_ref[...].astype(BF16), w_ref[...].astype(BF16),
                         preferred_element_type=F32).astype(o_ref.dtype)


def _mm_f32_body(a_ref, w_ref, o_ref):
    o_ref[...] = jnp.dot(a_ref[...], w_ref[...], preferred_element_type=F32,
                         precision=HIGHEST).astype(o_ref.dtype)


def pmatmul(a, w, *, exact=False, out_dtype=F32, tm=512, tn=None):
    m, k = a.shape
    n = w.shape[1]
    tm = _pick_tile(m, tm)
    tn = n if tn is None else _pick_tile(n, tn)
    body = _mm_f32_body if exact else _mm_bf16_body
    return pl.pallas_call(
        body,
        name='mm',
        grid=(m // tm, n // tn),
        in_specs=[pl.BlockSpec((tm, k), lambda i, j: (i, 0)),
                  pl.BlockSpec((k, tn), lambda i, j: (0, j))],
        out_specs=pl.BlockSpec((tm, tn), lambda i, j: (i, j)),
        out_shape=jax.ShapeDtypeStruct((m, n), out_dtype),
        compiler_params=_cparams(("parallel", "parallel")),
    )(a, w)


def _mm_split_body(a_ref, w_ref, *o_refs, splits):
    a = a_ref[...]
    for o_ref, (start, width) in zip(o_refs, splits):
        o_ref[...] = jnp.dot(a, w_ref[:, start:start + width],
                             preferred_element_type=F32).astype(o_ref.dtype)


def mm_split(a, w, splits, dtypes, tm=512):
    m, k = a.shape
    n = w.shape[1]
    tm = _pick_tile(m, tm)
    return pl.pallas_call(
        functools.partial(_mm_split_body, splits=tuple(splits)),
        name='mm_split',
        grid=(m // tm,),
        in_specs=[pl.BlockSpec((tm, k), lambda i: (i, 0)),
                  pl.BlockSpec((k, n), lambda i: (0, 0))],
        out_specs=[pl.BlockSpec((tm, wd), lambda i: (i, 0)) for _, wd in splits],
        out_shape=[jax.ShapeDtypeStruct((m, wd), dt) for (_, wd), dt in zip(splits, dtypes)],
        compiler_params=_cparams(("parallel",)),
    )(a, w)


def _mm_resid_body(*refs, n_pairs):
    a_refs = refs[:n_pairs]
    w_refs = refs[n_pairs:2 * n_pairs]
    x_ref, g_ref, o_ref = refs[2 * n_pairs:]
    acc = jnp.dot(a_refs[0][0], w_refs[0][...], preferred_element_type=F32)
    for a_ref, w_ref in zip(a_refs[1:], w_refs[1:]):
        acc = acc + jnp.dot(a_ref[0], w_ref[...], preferred_element_type=F32)
    o_ref[0] = x_ref[0] + g_ref[...] * acc


def mm_resid(a_list, w_list, xs, mods, gate_idx, n_tiles):
    b, nt, d = xs.shape
    n_pairs = len(a_list)
    in_specs = [pl.BlockSpec((1, ROW_TILE, a.shape[-1]), lambda i, j: (i, j, 0)) for a in a_list]
    in_specs += [pl.BlockSpec(w.shape, lambda i, j: (0, 0)) for w in w_list]
    in_specs += [pl.BlockSpec((1, ROW_TILE, d), lambda i, j: (i, j, 0)),
                 _mod_spec(gate_idx, d)]
    return pl.pallas_call(
        functools.partial(_mm_resid_body, n_pairs=n_pairs),
        name='mm_resid',
        grid=(b, n_tiles),
        in_specs=in_specs,
        out_specs=pl.BlockSpec((1, ROW_TILE, d), lambda i, j: (i, j, 0)),
        out_shape=jax.ShapeDtypeStruct(xs.shape, F32),
        input_output_aliases={2 * n_pairs: 0},
        compiler_params=_cparams(("parallel", "parallel")),
    )(*a_list, *w_list, xs, mods)


def _mod_spec(idx, d, b0=0):
    return pl.BlockSpec((None, None, None, 1, d), lambda i, j: (i + b0, 1 - j // N_LAT_TILES, idx, 0, 0))


def adaln_table(c, c_ctx, w, b):
    cv = jax.nn.silu(jnp.concatenate([c, c_ctx[None, :]], axis=0))
    cv = jnp.pad(cv, ((0, 24 - cv.shape[0]), (0, 0)))
    m = pmatmul(cv, w, exact=True, tn=1536)[:BATCH + 1] + b
    per_sample = m[:BATCH]
    ctx_row = jnp.broadcast_to(m[BATCH][None, :], per_sample.shape)
    return jnp.stack([ctx_row, per_sample], axis=1).reshape(BATCH, 2, 6, 1, D_MODEL)


def _norm_mod(x, w, shift, scale):
    ms = jnp.mean(x * x, axis=-1, keepdims=True)
    return (x * lax.rsqrt(ms + NORM_EPS) * w) * (1.0 + scale) + shift


def _norm_mod_body(x_ref, w_ref, sh_ref, sc_ref, o_ref):
    o_ref[0] = _norm_mod(x_ref[0], w_ref[...], sh_ref[...], sc_ref[...]).astype(o_ref.dtype)


def norm_mod(xs, w, mods, shift_idx, scale_idx):
    b, nt, d = xs.shape
    return pl.pallas_call(
        _norm_mod_body,
        name='norm_mod',
        grid=(b, nt // ROW_TILE),
        in_specs=[pl.BlockSpec((1, ROW_TILE, d), lambda i, j: (i, j, 0)),
                  pl.BlockSpec((1, d), lambda i, j: (0, 0)),
                  _mod_spec(shift_idx, d), _mod_spec(scale_idx, d)],
        out_specs=pl.BlockSpec((1, ROW_TILE, d), lambda i, j: (i, j, 0)),
        out_shape=jax.ShapeDtypeStruct(xs.shape, BF16),
        compiler_params=_cparams(("parallel", "parallel")),
    )(xs, w.reshape(1, d), mods, mods)


def _final_norm_body(x_ref, w_ref, o_ref):
    x = x_ref[0]
    ms = jnp.mean(x * x, axis=-1, keepdims=True)
    o_ref[0] = x * lax.rsqrt(ms + NORM_EPS) * w_ref[...]


def final_norm(xs, w):
    b, _, d = xs.shape
    return pl.pallas_call(
        _final_norm_body,
        name='final_norm',
        grid=(b, N_LAT_TILES),
        in_specs=[pl.BlockSpec((1, ROW_TILE, d), lambda i, j: (i, j, 0)),
                  pl.BlockSpec((1, d), lambda i, j: (0, 0))],
        out_specs=pl.BlockSpec((1, ROW_TILE, d), lambda i, j: (i, j, 0)),
        out_shape=jax.ShapeDtypeStruct((b, SEQ, d), F32),
        compiler_params=_cparams(("parallel", "parallel")),
    )(xs, w.reshape(1, d))


def _route_body(x_ref, w_ref, sh_ref, sc_ref, rw_ref, rb_ref, h_ref, idx_ref, wsel_ref, rank_ref, cnt_ref):
    first = (pl.program_id(0) == 0) & (pl.program_id(1) == 0)

    @pl.when(first)
    def _():
        cnt_ref[...] = jnp.zeros_like(cnt_ref)

    h = _norm_mod(x_ref[0], w_ref[...], sh_ref[...], sc_ref[...])
    h_ref[0] = h.astype(h_ref.dtype)
    logits = jnp.dot(h, rw_ref[...], preferred_element_type=F32, precision=HIGHEST)
    scores = 1.0 / (1.0 + jnp.exp(-logits))
    tm, ne = scores.shape
    lane = lax.broadcasted_iota(jnp.int32, (tm, ne), 1).astype(F32)
    slot = lax.broadcasted_iota(jnp.int32, (tm, MOE_TOPK), 1)
    sel = scores + rb_ref[...]
    picked = jnp.zeros((tm, ne), F32)
    hits = []
    idx_out = jnp.zeros((tm, MOE_TOPK), F32)
    w_out = jnp.zeros((tm, MOE_TOPK), F32)
    for k in range(MOE_TOPK):
        m = jnp.max(sel, axis=-1, keepdims=True)
        ik = jnp.min(jnp.where(sel == m, lane, float(ne)), axis=-1, keepdims=True)
        hit = lane == ik
        wk = jnp.sum(jnp.where(hit, scores, 0.0), axis=-1, keepdims=True)
        sel = jnp.where(hit, -jnp.inf, sel)
        picked = picked + hit.astype(F32)
        hits.append(hit)
        idx_out = jnp.where(slot == k, ik, idx_out)
        w_out = jnp.where(slot == k, wk, w_out)
    wsum = jnp.sum(w_out, axis=-1, keepdims=True)
    wsel_ref[0] = w_out / wsum * MOE_SCALE
    idx_ref[0] = idx_out.astype(jnp.int32)
    r_i = lax.broadcasted_iota(jnp.int32, (tm, tm), 0)
    c_i = lax.broadcasted_iota(jnp.int32, (tm, tm), 1)
    strict_lower = (c_i < r_i).astype(BF16)
    before = jnp.dot(strict_lower, picked.astype(BF16), preferred_element_type=F32) + cnt_ref[...]
    rank_out = jnp.zeros((tm, MOE_TOPK), F32)
    for k in range(MOE_TOPK):
        rk = jnp.sum(jnp.where(hits[k], before, 0.0), axis=-1, keepdims=True)
        rank_out = jnp.where(slot == k, rk, rank_out)
    rank_ref[0] = rank_out.astype(jnp.int32)
    cnt_ref[...] = cnt_ref[...] + jnp.sum(picked, axis=0, keepdims=True)


def route(xs, w, mods, router_w, router_b, n_tiles, b0, b):
    d = xs.shape[-1]
    rows = n_tiles * ROW_TILE
    small = lambda dt: jax.ShapeDtypeStruct((b, rows, MOE_TOPK), dt)
    small_spec = pl.BlockSpec((1, ROW_TILE, MOE_TOPK), lambda i, j: (i, j, 0))
    return pl.pallas_call(
        _route_body,
        name='route',
        grid=(b, n_tiles),
        in_specs=[pl.BlockSpec((1, ROW_TILE, d), lambda i, j: (i + b0, j, 0)),
                  pl.BlockSpec((1, d), lambda i, j: (0, 0)),
                  _mod_spec(3, d, b0), _mod_spec(4, d, b0),
                  pl.BlockSpec((d, MOE_EXPERTS), lambda i, j: (0, 0)),
                  pl.BlockSpec((1, MOE_EXPERTS), lambda i, j: (0, 0))],
        out_specs=[pl.BlockSpec((1, ROW_TILE, d), lambda i, j: (i, j, 0)),
                   small_spec, small_spec, small_spec,
                   pl.BlockSpec((1, MOE_EXPERTS), lambda i, j: (0, 0))],
        out_shape=[jax.ShapeDtypeStruct((b, rows, d), BF16), small(jnp.int32), small(F32), small(jnp.int32),
                   jax.ShapeDtypeStruct((1, MOE_EXPERTS), F32)],
        compiler_params=_cparams(("arbitrary", "arbitrary")),
    )(xs, w.reshape(1, d), mods, mods, router_w, router_b.reshape(1, MOE_EXPERTS))


def _swiglu(x, wg, wu, wd):
    g = jnp.dot(x, wg, preferred_element_type=F32)
    u = jnp.dot(x, wu, preferred_element_type=F32)
    h = (_silu(g) * u).astype(BF16)
    return jnp.dot(h, wd, preferred_element_type=F32)


def _expert_body(be_ref, nu_ref, x_ref, wg_ref, wu_ref, wd_ref, o_ref, wg_s, wu_s, wd_s):
    i = pl.program_id(0)
    used = i < nu_ref[0]

    @pl.when(used & ((i == 0) | (be_ref[i] != be_ref[jnp.maximum(i - 1, 0)])))
    def _():
        wg_s[...] = wg_ref[0].astype(BF16)
        wu_s[...] = wu_ref[0].astype(BF16)
        wd_s[...] = wd_ref[0].astype(BF16)

    @pl.when(used)
    def _():
        o_ref[...] = _swiglu(x_ref[...], wg_s[...], wu_s[...], wd_s[...]).astype(o_ref.dtype)

    @pl.when(jnp.logical_not(used))
    def _():
        o_ref[...] = jnp.zeros_like(o_ref)


def moe_experts(x_rows, block_e, n_used, wg, wu, wd, layer):
    rows, d = x_rows.shape
    n_blocks = rows // MOE_BLOCK
    f = wg.shape[-1]
    grid_spec = pltpu.PrefetchScalarGridSpec(
        num_scalar_prefetch=2,
        grid=(n_blocks,),
        in_specs=[
            pl.BlockSpec((MOE_BLOCK, d), lambda i, be, nu: (i, 0)),
            pl.BlockSpec((None, 1, d, f), lambda i, be, nu: (layer, be[i], 0, 0)),
            pl.BlockSpec((None, 1, d, f), lambda i, be, nu: (layer, be[i], 0, 0)),
            pl.BlockSpec((None, 1, f, d), lambda i, be, nu: (layer, be[i], 0, 0)),
        ],
        out_specs=pl.BlockSpec((MOE_BLOCK, d), lambda i, be, nu: (i, 0)),
        scratch_shapes=[pltpu.VMEM((d, f), BF16), pltpu.VMEM((d, f), BF16), pltpu.VMEM((f, d), BF16)],
    )
    return pl.pallas_call(
        _expert_body,
        name='experts',
        grid_spec=grid_spec,
        out_shape=jax.ShapeDtypeStruct((rows, d), BF16),
        compiler_params=_cparams(("arbitrary",)),
    )(block_e, n_used, x_rows, wg, wu, wd)


def _shared_resid_body(h_ref, wg_ref, wu_ref, wd_ref, pk_ref, ws_ref, x_ref, g_ref, o_ref):
    y = _swiglu(h_ref[0], wg_ref[...], wu_ref[...], wd_ref[...])
    ws = ws_ref[0]
    for k in range(MOE_TOPK):
        y = y + ws[:, k:k + 1] * pk_ref[k, 0].astype(F32)
    o_ref[0] = x_ref[0] + g_ref[...] * y


def shared_resid(h, picked, wsel, xs, mods, wg, wu, wd, n_tiles, b0):
    b, _, d = h.shape
    f = wg.shape[-1]
    tile = pl.BlockSpec((1, ROW_TILE, d), lambda i, j: (i, j, 0))
    xs_tile = pl.BlockSpec((1, ROW_TILE, d), lambda i, j: (i + b0, j, 0))
    return pl.pallas_call(
        _shared_resid_body,
        name='shared_resid',
        grid=(b, n_tiles),
        in_specs=[tile,
                  pl.BlockSpec((d, f), lambda i, j: (0, 0)),
                  pl.BlockSpec((d, f), lambda i, j: (0, 0)),
                  pl.BlockSpec((f, d), lambda i, j: (0, 0)),
                  pl.BlockSpec((MOE_TOPK, 1, ROW_TILE, d), lambda i, j: (0, i, j, 0)),
                  pl.BlockSpec((1, ROW_TILE, MOE_TOPK), lambda i, j: (i, j, 0)),
                  xs_tile, _mod_spec(5, d, b0)],
        out_specs=xs_tile,
        out_shape=jax.ShapeDtypeStruct(xs.shape, F32),
        input_output_aliases={6: 0},
        compiler_params=_cparams(("parallel", "parallel")),
    )(h, wg, wu, wd, picked, wsel, xs, mods)


MOE_SPLITS = 2


def moe_layer(xs, norm_w, mods, router_w, router_b, w_gate, w_up, w_down, sh_gate, sh_up, sh_down, n_tiles, layer):
    bsz = xs.shape[0]
    shared_w = (sh_gate.astype(BF16), sh_up.astype(BF16), sh_down.astype(BF16))
    b = bsz // MOE_SPLITS
    for g in range(MOE_SPLITS):
        xs = _moe_group(xs, norm_w, mods, router_w, router_b, w_gate, w_up, w_down, shared_w, n_tiles, layer, g * b, b)
    return xs


def _moe_group(xs, norm_w, mods, router_w, router_b, w_gate, w_up, w_down, shared_w, n_tiles, layer, b0, b):
    d = xs.shape[-1]
    rows_per_sample = n_tiles * ROW_TILE
    n = b * rows_per_sample
    h, idx, wsel, rank, counts = route(xs, norm_w, mods, router_w, router_b, n_tiles, b0, b)
    counts = counts[0].astype(jnp.int32)
    padded = (counts + MOE_BLOCK - 1) // MOE_BLOCK * MOE_BLOCK
    ends = jnp.cumsum(padded)
    starts = ends - padded
    nk = n * MOE_TOPK
    n_blocks = -(-nk // MOE_BLOCK) + MOE_EXPERTS
    rows = n_blocks * MOE_BLOCK
    n_pad = rows - nk
    e_iota = jnp.arange(MOE_EXPERTS, dtype=jnp.int32)
    dest = jnp.sum(jnp.where(idx[..., None] == e_iota, starts, 0), axis=-1) + rank
    blk_start = jnp.arange(n_blocks, dtype=jnp.int32) * MOE_BLOCK
    block_e = jnp.minimum(jnp.sum(ends[None, :] <= blk_start[:, None], axis=1), MOE_EXPERTS - 1).astype(jnp.int32)
    n_used = (ends[-1:] // MOE_BLOCK).astype(jnp.int32)
    pad = padded - counts
    cum_pad = jnp.cumsum(pad)
    m = jnp.arange(n_pad, dtype=jnp.int32)
    e_m = jnp.sum(cum_pad[None, :] <= m[:, None], axis=1)
    base = jnp.sum(jnp.where(jnp.minimum(e_m, MOE_EXPERTS - 1)[:, None] == e_iota,
                             starts + counts - (cum_pad - pad), 0), axis=1)
    pad_row = jnp.where(e_m < MOE_EXPERTS, base + m, ends[-1] + m - cum_pad[-1])
    tok = jnp.arange(nk, dtype=jnp.int32) // MOE_TOPK
    _, row_tok = lax.sort((jnp.concatenate([dest.reshape(-1), pad_row]).astype(jnp.int32),
                           jnp.concatenate([tok, jnp.zeros((n_pad,), jnp.int32)])), num_keys=1)
    h_flat = h.reshape(n, d)
    x_rows = h_flat[row_tok]
    y_rows = moe_experts(x_rows, block_e, n_used, w_gate, w_up, w_down, layer)
    picked = y_rows[dest.reshape(n, MOE_TOPK).T].reshape(MOE_TOPK, b, rows_per_sample, d)
    return shared_resid(h, picked, wsel, xs, mods, *shared_w, n_tiles, b0)


def _dwconv_body(x_ref, w_ref, b_ref, o_ref, *, width, act):
    chunk = ROW_TILE
    n_chunks = NTOK // chunk
    first_of_seq = (0, N_LAT_TILES)
    last_of_seq = (N_LAT_TILES - 1, n_chunks - 1)
    tc = x_ref.shape[-1]
    halo = 16
    row = lax.broadcasted_iota(jnp.int32, (chunk, tc), 0)
    zero_row = jnp.zeros((1, tc), F32)
    for c in range(n_chunks):
        r0 = c * chunk
        cur = x_ref[0, r0:r0 + chunk, :].astype(F32)
        if c in first_of_seq:
            prev_last = zero_row
        else:
            prev_last = x_ref[0, r0 - halo:r0, :].astype(F32)[halo - 1:halo, :]
        if c in last_of_seq:
            next0 = next1 = zero_row
        else:
            nxt = x_ref[0, r0 + chunk:r0 + chunk + halo, :].astype(F32)
            next0, next1 = nxt[0:1, :], nxt[1:2, :]
        xm1 = jnp.where(row == 0, prev_last, pltpu.roll(cur, 1, 0))
        xp1 = jnp.where(row == chunk - 1, next0, pltpu.roll(cur, chunk - 1, 0))
        y = w_ref[0:1, :] * xm1 + w_ref[1:2, :] * cur + w_ref[2:3, :] * xp1 + b_ref[...]
        if width == 4:
            xp2 = jnp.where(row == chunk - 2, next0,
                            jnp.where(row == chunk - 1, next1, pltpu.roll(cur, chunk - 2, 0)))
            y = y + w_ref[3:4, :] * xp2
        if act:
            y = _silu(y)
        o_ref[0, c * chunk:(c + 1) * chunk, :] = y.astype(o_ref.dtype)


def dwconv_stream(x, w, b, act, tc=256):
    bsz, nt, c = x.shape
    width = w.shape[0]
    return pl.pallas_call(
        functools.partial(_dwconv_body, width=width, act=act),
        name='dwconv',
        grid=(bsz, c // tc),
        in_specs=[pl.BlockSpec((1, nt, tc), lambda i, j: (i, 0, j)),
                  pl.BlockSpec((width, tc), lambda i, j: (0, j)),
                  pl.BlockSpec((1, tc), lambda i, j: (0, j))],
        out_specs=pl.BlockSpec((1, nt, tc), lambda i, j: (i, 0, j)),
        out_shape=jax.ShapeDtypeStruct(x.shape, BF16),
        compiler_params=_cparams(("parallel", "parallel")),
    )(x, w, b.reshape(1, c))


HY_FB = 512


def dft_matrices(n):
    k = jnp.arange(n, dtype=jnp.int32)[:, None]
    t = jnp.arange(n, dtype=jnp.int32)[None, :]
    ang = (2.0 * math.pi / (2 * n)) * ((k * t) % (2 * n)).astype(F32)
    fre = jnp.cos(ang)
    fim = -jnp.sin(ang)
    nyq = jnp.where(t % 2 == 0, 1.0, -1.0).astype(F32)
    fim = jnp.where(k == 0, nyq, fim)
    fwd = jnp.concatenate([fre, fim], axis=0)
    colscale = jnp.where(jnp.arange(2 * n) % n == 0, 0.5, 1.0) / n
    inv = fwd.T * colscale[None, :]
    return fwd, inv


def hyena_filter_taps(n, fw0, fb0, fw1, fb1, fw2, fb2, fw3, freq):
    pos = jnp.arange(n, dtype=F32)
    t = pos / max(n - 1, 1)
    bands = jnp.linspace(1e-4, HY_BANDS - 1, HY_BANDS, dtype=F32)
    ang = (2.0 * math.pi / n) * pos[:, None] * bands[None, :]
    feats = jnp.concatenate([t[:, None], jnp.cos(ang), -jnp.sin(ang)], axis=-1)
    h = jnp.sin(freq * (jnp.dot(feats, fw0, precision=HIGHEST) + fb0))
    h = jnp.sin(freq * (jnp.dot(h, fw1, precision=HIGHEST) + fb1))
    h = jnp.sin(freq * (jnp.dot(h, fw2, precision=HIGHEST) + fb2))
    h = pmatmul(h, fw3, exact=True).reshape(n, 2, HY_ORDER, D_HY)
    deltas = jnp.abs(jnp.linspace(math.log(HY_DECAY_PCT_LO) / HY_DECAY_TARGET,
                                  math.log(HY_DECAY_PCT_HI) / HY_DECAY_TARGET, D_HY, dtype=F32))
    h = h * jnp.exp(-t[:, None] * deltas)[:, None, None, :]
    h0 = h[:, 0]
    h1 = h[:, 1].at[0].set(0.0)
    norm = jnp.sum(jnp.abs(h0), axis=0, keepdims=True) + jnp.sum(jnp.abs(h1), axis=0, keepdims=True)
    h0 = (h0 / norm).reshape(n, HY_ORDER * D_HY)
    h1 = (h1 / norm).reshape(n, HY_ORDER * D_HY)
    return h0 + h1, h0 - h1


def _split_bf16(a):
    hi = a.astype(BF16)
    return hi, (a - hi.astype(F32)).astype(BF16)


def hyena_spectrum(fwd, hsum, hdiff, fb):
    n = hsum.shape[0]
    f_hi, f_lo = _split_bf16(fwd)

    def dft(h):
        h_hi, h_lo = _split_bf16(h)
        return pmatmul(f_hi, h_hi) + pmatmul(f_hi, h_lo) + pmatmul(f_lo, h_hi)

    a = dft(hsum)
    bm = dft(hdiff)
    sr = a[:n]
    si = bm[n:]
    nyq = a[n]
    first = (jnp.arange(n) == 0)[:, None]
    p = sr
    q = jnp.where(first, 0.0, si)
    s = jnp.where(first, nyq[None, :], sr)
    spec = jnp.stack([p, q, s], axis=0).reshape(3, n // fb, fb, HY_ORDER, D_HY)
    return spec.transpose(3, 1, 0, 2, 4), f_hi


def _hyena_body(u_ref, fre_ref, fim_ref, gre_ref, gim_ref, sp_ref, bias_ref, prev_ref, o_ref,
                vin, acc, *, nf):
    del prev_ref
    o = pl.program_id(1)
    f = pl.program_id(2)
    c = D_HY

    @pl.when((o == 0) & (f == 0))
    def _():
        vin[...] = u_ref[0, :, 0:c]

    @pl.when(f == 0)
    def _():
        acc[...] = jnp.zeros_like(acc)

    v = vin[...]
    vr = jnp.dot(fre_ref[...], v, preferred_element_type=F32)
    vi = jnp.dot(fim_ref[...], v, preferred_element_type=F32)
    p, q, s = sp_ref[0], sp_ref[1], sp_ref[2]
    zr = (vr * p - vi * q).astype(BF16)
    zi = (vr * q + vi * s).astype(BF16)
    acc[...] += (jnp.dot(gre_ref[...], zr, preferred_element_type=F32)
                 + jnp.dot(gim_ref[...], zi, preferred_element_type=F32))

    @pl.when((o == 0) & (f == nf - 1))
    def _():
        z = u_ref[0, :, c:2 * c].astype(F32) * (acc[...] + bias_ref[0:1, :] * vin[...].astype(F32))
        vin[...] = z.astype(BF16)

    @pl.when((o == 1) & (f == nf - 1))
    def _():
        y = u_ref[0, :, 2 * c:3 * c].astype(F32) * (acc[...] + bias_ref[1:2, :] * vin[...].astype(F32))
        o_ref[0] = y.astype(o_ref.dtype)


def hyena_long_conv(u, fwd_bf16, inv_bf16, spec, bias, n, row_block, prev_out):
    bsz = u.shape[0]
    fb = spec.shape[3]
    nf = n // fb
    out_shape = jax.ShapeDtypeStruct((bsz, NTOK, D_HY), BF16)
    if prev_out is None:
        prev_out = jnp.zeros(out_shape.shape, BF16)
    args = [u, fwd_bf16, fwd_bf16, inv_bf16, inv_bf16, spec, bias, prev_out]
    aliases = {7: 0}
    return pl.pallas_call(
        functools.partial(_hyena_body, nf=nf),
        name='hyena',
        grid=(bsz, HY_ORDER, nf),
        in_specs=[pl.BlockSpec((1, n, 3 * D_HY), lambda b, o, f: (b, row_block, 0)),
                  pl.BlockSpec((fb, n), lambda b, o, f: (f, 0)),
                  pl.BlockSpec((fb, n), lambda b, o, f: (nf + f, 0)),
                  pl.BlockSpec((n, fb), lambda b, o, f: (0, f)),
                  pl.BlockSpec((n, fb), lambda b, o, f: (0, nf + f)),
                  pl.BlockSpec((None, None, 3, fb, D_HY), lambda b, o, f: (o, f, 0, 0, 0)),
                  pl.BlockSpec((HY_ORDER, D_HY), lambda b, o, f: (0, 0)),
                  pl.BlockSpec(memory_space=pl.ANY)],
        out_specs=pl.BlockSpec((1, n, D_HY), lambda b, o, f: (b, row_block, 0)),
        out_shape=out_shape,
        scratch_shapes=[pltpu.VMEM((n, D_HY), BF16), pltpu.VMEM((n, D_HY), F32)],
        input_output_aliases=aliases,
        compiler_params=_cparams(("parallel", "arbitrary", "arbitrary"), VMEM_LIMIT_BIG),
    )(*args)


def hyena_mixer_stream(p_hy, conv_w, conv_b, filt, bias):
    u = dwconv_stream(p_hy, conv_w, conv_b, act=False)
    out = None
    for n, row_block in ((SEQ, 0), (CTX_LEN, SEQ // CTX_LEN)):
        fb = min(HY_FB, n)
        fwd, inv = dft_matrices(n)
        hsum, hdiff = hyena_filter_taps(n, *filt)
        spec, fwd_bf16 = hyena_spectrum(fwd, hsum, hdiff, fb)
        out = hyena_long_conv(u, fwd_bf16, inv.astype(BF16), spec, bias, n, row_block, out)
    return out


def _tri(n, kind):
    r = lax.broadcasted_iota(jnp.int32, (n, n), 0)
    c = lax.broadcasted_iota(jnp.int32, (n, n), 1)
    return (c <= r) if kind == 'lower' else (c >= r)


def _ssd_dir(xbc_ref, dt_ref, dtt_ref, bias_r, bias_c, a_r, a_c, st_ref, y_ref, *, d, reverse):
    q = SSD_CHUNK
    nh = SSD_H
    gw = SSD_HPG * SSD_P
    lower = _tri(q, 'lower')
    upper = _tri(q, 'upper')
    lower_f = lower.astype(F32)
    upper_f = upper.astype(F32)
    dt_col = _softplus(dt_ref[0] + bias_r)
    dt_row = _softplus(dtt_ref[0] + bias_c)
    da_col = dt_col * a_r
    da_row = dt_row * a_c
    if not reverse:
        acs_col = jnp.dot(lower_f, da_col, preferred_element_type=F32, precision=HIGHEST)
        acs_row = jnp.dot(da_row, upper_f, preferred_element_type=F32, precision=HIGHEST)
        mask = lower
        edge = q - 1
    else:
        acs_col = jnp.dot(upper_f, da_col, preferred_element_type=F32, precision=HIGHEST)
        acs_row = jnp.dot(da_row, lower_f, preferred_element_type=F32, precision=HIGHEST)
        mask = upper
        edge = 0
    h0 = d * nh
    hh = lax.broadcasted_iota(jnp.int32, (2 * nh, nh * SSD_P), 0)
    cc = lax.broadcasted_iota(jnp.int32, (2 * nh, nh * SSD_P), 1) // SSD_P
    expand = (hh == cc + h0).astype(F32)
    acs_c = jnp.dot(acs_col, expand, preferred_element_type=F32, precision=HIGHEST)
    dt_c = jnp.dot(dt_col, expand, preferred_element_type=F32, precision=HIGHEST)
    total_c = acs_c[edge:edge + 1, :]
    e_in_c = jnp.exp(acs_c)
    w_end_c = jnp.exp(total_c - acs_c) * dt_c
    dec_c = jnp.exp(total_c)
    xs = xbc_ref[0, :, 0:D_SSM]
    xs_f = xs.astype(F32)
    for g in range(SSD_G):
        bm = xbc_ref[0, :, D_SSM + g * SSD_N:D_SSM + (g + 1) * SSD_N]
        cm = xbc_ref[0, :, D_SSM + SSD_G * SSD_N + g * SSD_N:D_SSM + SSD_G * SSD_N + (g + 1) * SSD_N]
        cb = lax.dot_general(cm, bm, (((1,), (1,)), ((), ())), preferred_element_type=F32)
        lws = []
        for k in range(SSD_HPG):
            h = h0 + g * SSD_HPG + k
            seg = acs_col[:, h:h + 1] - acs_row[h:h + 1, :]
            decay = jnp.exp(jnp.where(mask, seg, -jnp.inf))
            lws.append((cb * decay * dt_row[h:h + 1, :]).astype(BF16))
        lw = jnp.concatenate(lws, axis=1)
        xg = xs[:, g * gw:(g + 1) * gw]
        rb = lax.broadcasted_iota(jnp.int32, (SSD_HPG * q, gw), 0) // q
        cbk = lax.broadcasted_iota(jnp.int32, (SSD_HPG * q, gw), 1) // SSD_P
        x_bd = jnp.where(rb == cbk, jnp.concatenate([xg] * SSD_HPG, axis=0), jnp.zeros((), BF16))
        y_in = jnp.dot(lw, x_bd, preferred_element_type=F32)
        st = st_ref[g]
        y_st = jnp.dot(cm, st.astype(BF16), preferred_element_type=F32) * e_in_c[:, g * gw:(g + 1) * gw]
        y_ref[0, :, g * gw:(g + 1) * gw] = y_in + y_st
        xw = (xs_f[:, g * gw:(g + 1) * gw] * w_end_c[:, g * gw:(g + 1) * gw]).astype(BF16)
        upd = lax.dot_general(bm, xw, (((0,), (0,)), ((), ())), preferred_element_type=F32)
        st_ref[g] = st * dec_c[:, g * gw:(g + 1) * gw] + upd


def _ssd_body(xf_ref, dtf_ref, dttf_ref, xb_ref, dtb_ref, dttb_ref, bias_r, bias_c, a_r, a_c,
              yf_ref, yb_ref, stf, stb):
    @pl.when(pl.program_id(1) == 0)
    def _():
        stf[...] = jnp.zeros_like(stf)
        stb[...] = jnp.zeros_like(stb)

    _ssd_dir(xf_ref, dtf_ref, dttf_ref, bias_r[...], bias_c[...], a_r[...], a_c[...], stf, yf_ref,
             d=0, reverse=False)
    _ssd_dir(xb_ref, dtb_ref, dttb_ref, bias_r[...], bias_c[...], a_r[...], a_c[...], stb, yb_ref,
             d=1, reverse=True)


def ssd_scan(xbc, dt, dt_bias, a_log):
    bsz = xbc.shape[0]
    nc = NTOK // SSD_CHUNK
    nlat = SEQ // SSD_CHUNK
    dtt = jnp.swapaxes(dt, 1, 2)
    fwd_chunk = lambda s: (s + nlat) % nc
    bwd_chunk = lambda s: nc - 1 - s
    a = -jnp.exp(a_log.astype(F32)).reshape(1, 2 * SSD_H)
    bias = dt_bias.astype(F32).reshape(1, 2 * SSD_H)
    x_spec = lambda cm: pl.BlockSpec((1, SSD_CHUNK, SSD_XBC), lambda b, s: (b, cm(s), 0))
    dt_spec = lambda cm: pl.BlockSpec((1, SSD_CHUNK, 2 * SSD_H), lambda b, s: (b, cm(s), 0))
    dtt_spec = lambda cm: pl.BlockSpec((1, 2 * SSD_H, SSD_CHUNK), lambda b, s: (b, 0, cm(s)))
    y_spec = lambda cm: pl.BlockSpec((1, SSD_CHUNK, D_SSM), lambda b, s: (b, cm(s), 0))
    row = pl.BlockSpec((1, 2 * SSD_H), lambda b, s: (0, 0))
    col = pl.BlockSpec((2 * SSD_H, 1), lambda b, s: (0, 0))
    y_shape = jax.ShapeDtypeStruct((bsz, NTOK, D_SSM), F32)
    gw = SSD_HPG * SSD_P
    return pl.pallas_call(
        _ssd_body,
        name='ssd_scan',
        grid=(bsz, nc),
        in_specs=[x_spec(fwd_chunk), dt_spec(fwd_chunk), dtt_spec(fwd_chunk),
                  x_spec(bwd_chunk), dt_spec(bwd_chunk), dtt_spec(bwd_chunk),
                  row, col, row, col],
        out_specs=[y_spec(fwd_chunk), y_spec(bwd_chunk)],
        out_shape=[y_shape, y_shape],
        scratch_shapes=[pltpu.VMEM((SSD_G, SSD_N, gw), F32), pltpu.VMEM((SSD_G, SSD_N, gw), F32)],
        compiler_params=_cparams(("parallel", "arbitrary")),
    )(xbc, dt, dtt, xbc, dt, dtt, bias, bias.reshape(-1, 1), a, a.reshape(-1, 1))


def _ssd_merge_body(yf_ref, yb_ref, xbc_ref, z_ref, d_ref, nw_ref, o_ref):
    xs = xbc_ref[0, :, 0:D_SSM].astype(F32)
    z = z_ref[0].astype(F32)
    g = (yf_ref[0] + yb_ref[0] + d_ref[...] * xs) * _silu(z)
    gw = D_SSM // SSD_G
    for k in range(SSD_G):
        gk = g[:, k * gw:(k + 1) * gw]
        ms = jnp.mean(gk * gk, axis=-1, keepdims=True)
        o_ref[0, :, k * gw:(k + 1) * gw] = (gk * lax.rsqrt(ms + NORM_EPS)
                                            * nw_ref[:, k * gw:(k + 1) * gw]).astype(o_ref.dtype)


def ssd_merge(yf, yb, xbc, z, d_skip, norm_w):
    bsz = yf.shape[0]
    tile = lambda w: pl.BlockSpec((1, ROW_TILE, w), lambda i, j: (i, j, 0))
    vec = pl.BlockSpec((1, D_SSM), lambda i, j: (0, 0))
    d_chan = jnp.repeat(d_skip.astype(F32), SSD_P).reshape(1, D_SSM)
    return pl.pallas_call(
        _ssd_merge_body,
        name='ssd_merge',
        grid=(bsz, N_ROW_TILES),
        in_specs=[tile(D_SSM), tile(D_SSM), tile(SSD_XBC), tile(D_SSM), vec, vec],
        out_specs=tile(D_SSM),
        out_shape=jax.ShapeDtypeStruct((bsz, NTOK, D_SSM), BF16),
        compiler_params=_cparams(("parallel", "parallel")),
    )(yf, yb, xbc, z, d_chan, norm_w.reshape(1, D_SSM))


def even_layer_mixer(xs, mods, norm_w, w_in, w_out, hy_conv_w, hy_conv_b, hy_filt, hy_bias,
                     ssd_conv_w, ssd_conv_b, ssd_dt_bias, ssd_a_log, ssd_d, ssd_norm_w):
    bsz = xs.shape[0]
    h = norm_mod(xs, norm_w, mods, 0, 1)
    splits = ((0, HY_IN), (HY_IN, D_SSM), (HY_IN + D_SSM, SSD_XBC), (HY_IN + D_SSM + SSD_XBC, 2 * SSD_H))
    p_hy, z, xbc_raw, dt = mm_split(h.reshape(bsz * NTOK, D_MODEL), w_in.astype(BF16), splits,
                                    (BF16, BF16, BF16, F32))
    to3 = lambda a: a.reshape(bsz, NTOK, a.shape[-1])
    y_hy = hyena_mixer_stream(to3(p_hy), hy_conv_w, hy_conv_b, hy_filt, hy_bias)
    xbc = dwconv_stream(to3(xbc_raw), ssd_conv_w, ssd_conv_b, act=True)
    yf, yb = ssd_scan(xbc, to3(dt), ssd_dt_bias, ssd_a_log)
    s = ssd_merge(yf, yb, xbc, to3(z), ssd_d, ssd_norm_w)
    wo = w_out.astype(BF16)
    return mm_resid([y_hy, s], [wo[:D_HY], wo[D_HY:]], xs, mods, 2, N_ROW_TILES)


GLA_QK = GLA_H * GLA_DK
GLA_V = GLA_H * GLA_DV


def _gla_dir(qkv_ref, lr_ref, gw_ref, gb_ref, st_ref, o_ref, *, d, reverse):
    q = GLA_CHUNK
    tri = _tri(q, 'upper' if reverse else 'lower')
    edge = 0 if reverse else q - 1
    lr = lr_ref[0, :, d * GLA_RANK:(d + 1) * GLA_RANK]
    logit = jnp.dot(lr, gw_ref[d], preferred_element_type=F32, precision=HIGHEST) + gb_ref[d:d + 1, :]
    log_g = -_softplus(-logit) * (1.0 / GLA_GATE_NORM)
    gcum = jnp.dot(tri.astype(F32), log_g, preferred_element_type=F32, precision=HIGHEST)
    total = gcum[edge:edge + 1, :]
    qf = qkv_ref[0, :, 0:GLA_QK].astype(F32)
    kf = qkv_ref[0, :, GLA_QK:2 * GLA_QK].astype(F32)
    v = qkv_ref[0, :, 2 * GLA_QK:2 * GLA_QK + GLA_V]
    qg = (qf * (GLA_DK ** -0.5) * jnp.exp(gcum)).astype(BF16)
    kg = (kf * jnp.exp(-gcum)).astype(BF16)
    kw = (kf * jnp.exp(total - gcum)).astype(BF16)
    rb = lax.broadcasted_iota(jnp.int32, (GLA_H * q, GLA_QK), 0) // q
    cb = lax.broadcasted_iota(jnp.int32, (GLA_H * q, GLA_QK), 1) // GLA_DK
    k_bd = jnp.where(rb == cb, jnp.concatenate([kg] * GLA_H, axis=0), jnp.zeros((), BF16))
    att = lax.dot_general(qg, k_bd, (((1,), (1,)), ((), ())), preferred_element_type=F32)
    i_i = lax.broadcasted_iota(jnp.int32, (q, GLA_H * q), 0)
    j_i = lax.broadcasted_iota(jnp.int32, (q, GLA_H * q), 1) % q
    keep = (j_i >= i_i) if reverse else (j_i <= i_i)
    att = jnp.where(keep, att, 0.0).astype(BF16)
    rv = lax.broadcasted_iota(jnp.int32, (GLA_H * q, GLA_V), 0) // q
    cv = lax.broadcasted_iota(jnp.int32, (GLA_H * q, GLA_V), 1) // GLA_DV
    v_bd = jnp.where(rv == cv, jnp.concatenate([v] * GLA_H, axis=0), jnp.zeros((), BF16))
    st = st_ref[...]
    o_in = jnp.dot(att, v_bd, preferred_element_type=F32)
    o_st = lax.dot_general(qg, st.astype(BF16), (((1,), (1,)), ((), ())), preferred_element_type=F32)
    o_ref[0] = o_in + o_st
    upd = lax.dot_general(v, kw, (((0,), (0,)), ((), ())), preferred_element_type=F32)
    rs = lax.broadcasted_iota(jnp.int32, (GLA_V, GLA_QK), 0) // GLA_DV
    cs = lax.broadcasted_iota(jnp.int32, (GLA_V, GLA_QK), 1) // GLA_DK
    st_ref[...] = st * jnp.exp(total) + jnp.where(rs == cs, upd, 0.0)


def _gla_body(qf_ref, lf_ref, qb_ref, lb_ref, gw_ref, gb_ref, of_ref, ob_ref, stf, stb):
    @pl.when(pl.program_id(1) == 0)
    def _():
        stf[...] = jnp.zeros_like(stf)
        stb[...] = jnp.zeros_like(stb)

    _gla_dir(qf_ref, lf_ref, gw_ref, gb_ref, stf, of_ref, d=0, reverse=False)
    _gla_dir(qb_ref, lb_ref, gw_ref, gb_ref, stb, ob_ref, d=1, reverse=True)


def gla_scan(qkv, lr, gate_w, gate_b):
    bsz = qkv.shape[0]
    nc = NTOK // GLA_CHUNK
    nlat = SEQ // GLA_CHUNK
    fwd_chunk = lambda s: (s + nlat) % nc
    bwd_chunk = lambda s: nc - 1 - s
    q_spec = lambda cm: pl.BlockSpec((1, GLA_CHUNK, qkv.shape[-1]), lambda b, s: (b, cm(s), 0))
    l_spec = lambda cm: pl.BlockSpec((1, GLA_CHUNK, 2 * GLA_RANK), lambda b, s: (b, cm(s), 0))
    o_spec = lambda cm: pl.BlockSpec((1, GLA_CHUNK, GLA_V), lambda b, s: (b, cm(s), 0))
    o_shape = jax.ShapeDtypeStruct((bsz, NTOK, GLA_V), F32)
    return pl.pallas_call(
        _gla_body,
        name='gla_scan',
        grid=(bsz, nc),
        in_specs=[q_spec(fwd_chunk), l_spec(fwd_chunk), q_spec(bwd_chunk), l_spec(bwd_chunk),
                  pl.BlockSpec((2, GLA_RANK, GLA_QK), lambda b, s: (0, 0, 0)),
                  pl.BlockSpec((2, GLA_QK), lambda b, s: (0, 0))],
        out_specs=[o_spec(fwd_chunk), o_spec(bwd_chunk)],
        out_shape=[o_shape, o_shape],
        scratch_shapes=[pltpu.VMEM((GLA_V, GLA_QK), F32), pltpu.VMEM((GLA_V, GLA_QK), F32)],
        compiler_params=_cparams(("parallel", "arbitrary")),
    )(qkv, lr, qkv, lr, gate_w.astype(F32), gate_b.astype(F32))


def _gla_merge_body(of_ref, ob_ref, r_ref, nw_ref, o_ref):
    o = of_ref[0] + ob_ref[0]
    r = r_ref[0].astype(F32)
    for h in range(GLA_H):
        sl = slice(h * GLA_DV, (h + 1) * GLA_DV)
        oh = o[:, sl]
        ms = jnp.mean(oh * oh, axis=-1, keepdims=True)
        o_ref[0, :, sl] = (oh * lax.rsqrt(ms + NORM_EPS) * nw_ref[:, sl] * _silu(r[:, sl])).astype(o_ref.dtype)


def gla_merge_stream(of, ob, r, norm_w):
    bsz = of.shape[0]
    tile = pl.BlockSpec((1, ROW_TILE, GLA_V), lambda i, j: (i, j, 0))
    return pl.pallas_call(
        _gla_merge_body,
        name='gla_merge',
        grid=(bsz, N_LAT_TILES),
        in_specs=[tile, tile, tile, pl.BlockSpec((1, GLA_V), lambda i, j: (0, 0))],
        out_specs=tile,
        out_shape=jax.ShapeDtypeStruct((bsz, SEQ, GLA_V), BF16),
        compiler_params=_cparams(("parallel", "parallel")),
    )(of, ob, r, norm_w.reshape(1, GLA_V))


RG_TILE = 8


def _gelu_tanh(x):
    return 0.5 * x * (1.0 + jnp.tanh(math.sqrt(2.0 / math.pi) * (x + 0.044715 * x * x * x)))


def _rg_scan_block(a_s, x_s, h_s, base, carry, reverse):
    n_tiles = ROW_TILE // RG_TILE
    row = lax.broadcasted_iota(jnp.int32, (RG_TILE, D_RG), 0)

    def tile_step(i, h_prev):
        t = (n_tiles - 1 - i) if reverse else i
        r0 = pl.multiple_of(t * RG_TILE, RG_TILE)
        a = a_s[pl.ds(r0, RG_TILE), :]
        x = x_s[pl.ds(r0, RG_TILE), :]
        for s in (1, 2, 4):
            if reverse:
                ok = row < RG_TILE - s
                shift = RG_TILE - s
            else:
                ok = row >= s
                shift = s
            a_sh = jnp.where(ok, pltpu.roll(a, shift, 0), 1.0)
            x_sh = jnp.where(ok, pltpu.roll(x, shift, 0), 0.0)
            x = a * x_sh + x
            a = a * a_sh
        h = x + a * h_prev
        h_s[pl.ds(base + r0, RG_TILE), :] = h
        edge = 0 if reverse else RG_TILE - 1
        return jnp.broadcast_to(h[edge:edge + 1, :], (RG_TILE, D_RG))

    return lax.fori_loop(0, n_tiles, tile_step, carry)


def _rglru_body(u_ref, g_ref, w_ref, b_ref, c_ref, o_ref, hf_s, a_s, x_s, hb_s):
    n_blocks = NTOK // ROW_TILE
    fwd_order = list(range(N_LAT_TILES, n_blocks)) + list(range(N_LAT_TILES))
    bwd_order = list(range(n_blocks - 1, N_LAT_TILES - 1, -1)) + list(range(N_LAT_TILES - 1, -1, -1))

    def gates(blk, d):
        ub = u_ref[0, blk * ROW_TILE:(blk + 1) * ROW_TILE, :]
        z = jnp.dot(ub, w_ref[:, 2 * d * D_RG:2 * (d + 1) * D_RG], preferred_element_type=F32)
        z = z + b_ref[:, 2 * d * D_RG:2 * (d + 1) * D_RG]
        r = 1.0 / (1.0 + jnp.exp(-z[:, :D_RG]))
        i = 1.0 / (1.0 + jnp.exp(-z[:, D_RG:]))
        a = jnp.exp(c_ref[d:d + 1, :] * r)
        a_s[...] = a
        x_s[...] = jnp.sqrt(1.0 - a * a) * i * ub.astype(F32)

    carry = jnp.zeros((RG_TILE, D_RG), F32)
    for blk in fwd_order:
        gates(blk, 0)
        carry = _rg_scan_block(a_s, x_s, hf_s, blk * ROW_TILE, carry, reverse=False)
    carry = jnp.zeros((RG_TILE, D_RG), F32)
    for blk in bwd_order:
        gates(blk, 1)
        carry = _rg_scan_block(a_s, x_s, hb_s, 0, carry, reverse=True)
        rows = slice(blk * ROW_TILE, (blk + 1) * ROW_TILE)
        gate = g_ref[0, rows, :].astype(F32)
        o_ref[0, rows, :] = ((hf_s[rows, :] + hb_s[...]) * _gelu_tanh(gate)).astype(o_ref.dtype)


def rglru_stream(u, gate, w_a, b_a, w_x, b_x, lam):
    bsz = u.shape[0]
    eye = jnp.eye(RG_BLOCKS, dtype=F32)
    dense = lambda w: jnp.einsum('nio,nm->nimo', w, eye).reshape(D_RG, D_RG)
    w_cat = jnp.concatenate([dense(w_a[0]), dense(w_x[0]), dense(w_a[1]), dense(w_x[1])], axis=1).astype(BF16)
    b_cat = jnp.concatenate([b_a[0], b_x[0], b_a[1], b_x[1]]).astype(F32).reshape(1, 4 * D_RG)
    c = -RG_C * jax.nn.softplus(-lam.astype(F32))
    seq = pl.BlockSpec((1, NTOK, D_RG), lambda i: (i, 0, 0))
    return pl.pallas_call(
        _rglru_body,
        name='rglru',
        grid=(bsz,),
        in_specs=[seq, seq,
                  pl.BlockSpec((D_RG, 4 * D_RG), lambda i: (0, 0)),
                  pl.BlockSpec((1, 4 * D_RG), lambda i: (0, 0)),
                  pl.BlockSpec((2, D_RG), lambda i: (0, 0))],
        out_specs=seq,
        out_shape=jax.ShapeDtypeStruct((bsz, NTOK, D_RG), BF16),
        scratch_shapes=[pltpu.VMEM((NTOK, D_RG), F32), pltpu.VMEM((ROW_TILE, D_RG), F32),
                        pltpu.VMEM((ROW_TILE, D_RG), F32), pltpu.VMEM((ROW_TILE, D_RG), F32)],
        compiler_params=_cparams(("parallel",)),
    )(u, gate, w_cat, b_cat, c)


def odd_layer_mixer_pallas(xs, mods, norm_w, w_in, w_out, gla_args, rg_args):
    bsz = xs.shape[0]
    gate_w, gate_b, gla_norm_w = gla_args
    rg_conv_w, rg_conv_b, w_a, b_a, w_x, b_x, lam = rg_args
    h = norm_mod(xs, norm_w, mods, 0, 1)
    h = jnp.concatenate([to_col_major(h[:, :SEQ]), h[:, SEQ:]], axis=1)
    nqk, nv = GLA_QK, GLA_V
    r0 = 2 * nqk + nv + 2 * GLA_RANK
    w = jnp.concatenate([w_in[:, :2 * nqk + nv], w_in[:, r0:r0 + nv], w_in[:, GLA_IN:],
                         w_in[:, 2 * nqk + nv:r0]], axis=1).astype(BF16)
    qkv_w = 2 * nqk + nv
    splits = ((0, qkv_w), (qkv_w, nv), (qkv_w + nv, D_RG), (qkv_w + nv + D_RG, D_RG),
              (qkv_w + nv + 2 * D_RG, 2 * GLA_RANK))
    qkv, r, u_raw, gate, lr = mm_split(h.reshape(bsz * NTOK, D_MODEL), w, splits, (BF16, BF16, BF16, BF16, F32))
    to3 = lambda a: a.reshape(bsz, NTOK, a.shape[-1])
    of, ob = gla_scan(to3(qkv), to3(lr), gate_w, gate_b.reshape(2, GLA_QK))
    a_l = gla_merge_stream(of, ob, to3(r), gla_norm_w)
    u = dwconv_stream(to3(u_raw), rg_conv_w, rg_conv_b, act=False)
    r_l = rglru_stream(u, to3(gate), w_a, b_a, w_x, b_x, lam)[:, :SEQ]
    wo = w_out.astype(BF16)
    return mm_resid([from_col_major(a_l), from_col_major(r_l)], [wo[:GLA_V], wo[GLA_V:]], xs, mods, 2, N_LAT_TILES)


def rms_norm(x, w):
    xf = x.astype(F32)
    y = xf * lax.rsqrt(jnp.mean(jnp.square(xf), axis=-1, keepdims=True) + NORM_EPS)
    return y.astype(x.dtype) * w


def dwconv(x, w, b):
    y = lax.conv_general_dilated(x, w[:, None, :].astype(x.dtype), window_strides=(1,), padding='SAME',
                                 dimension_numbers=('NWC', 'WIO', 'NWC'), feature_group_count=x.shape[-1])
    return y + b.astype(x.dtype)


def maybe_flip(a, rev):
    return jnp.flip(a, axis=1) if rev else a


def to_col_major(x):
    b, n, d = x.shape
    rows = n // GRID_W
    return x.reshape(b, rows, GRID_W, d).transpose(0, 2, 1, 3).reshape(b, n, d)


def from_col_major(x):
    b, n, d = x.shape
    rows = n // GRID_W
    return x.reshape(b, GRID_W, rows, d).transpose(0, 2, 1, 3).reshape(b, n, d)


def gla_inputs(p, gate_w, gate_b):
    b, n, _ = p.shape
    nqk, nv = GLA_H * GLA_DK, GLA_H * GLA_DV
    q = p[..., :nqk].reshape(b, n, GLA_H, GLA_DK) * GLA_DK ** -0.5
    k = p[..., nqk:2 * nqk].reshape(b, n, GLA_H, GLA_DK)
    v = p[..., 2 * nqk:2 * nqk + nv].reshape(b, n, GLA_H, GLA_DV)
    lr = p[..., 2 * nqk + nv:2 * nqk + nv + 2 * GLA_RANK].reshape(b, n, 2, GLA_RANK)
    r = p[..., 2 * nqk + nv + 2 * GLA_RANK:]
    logit = jnp.einsum('bler,erk->blek', lr, gate_w) + gate_b
    log_g = (jax.nn.log_sigmoid(logit.astype(F32)) / GLA_GATE_NORM).reshape(b, n, 2, GLA_H, GLA_DK)
    return q, k, v, log_g, r


def gla_states(k, v, log_g, s0):
    b, n = k.shape[:2]
    nc = n // GLA_CHUNK
    kc = k.reshape(b, nc, GLA_CHUNK, GLA_H, GLA_DK)
    vc = v.reshape(b, nc, GLA_CHUNK, GLA_H, GLA_DV)
    gcum = jnp.cumsum(log_g.reshape(b, nc, GLA_CHUNK, GLA_H, GLA_DK), axis=2)
    states = jnp.einsum('bcqhd,bcqhv->bchdv', kc * jnp.exp(gcum[:, :, -1:] - gcum), vc)
    chunk_decay = jnp.exp(gcum[:, :, -1])

    def step(s, inp):
        st, dcy = inp
        return dcy[..., None] * s + st, s

    s_fin, s_prev = lax.scan(step, s0, (jnp.moveaxis(states, 1, 0), jnp.moveaxis(chunk_decay, 1, 0)))
    return jnp.moveaxis(s_prev, 0, 1), s_fin


def gla_output(q, k, v, log_g, s_prev):
    b, n = q.shape[:2]
    nc = n // GLA_CHUNK
    qc = q.reshape(b, nc, GLA_CHUNK, GLA_H, GLA_DK)
    kc = k.reshape(b, nc, GLA_CHUNK, GLA_H, GLA_DK)
    vc = v.reshape(b, nc, GLA_CHUNK, GLA_H, GLA_DV)
    gcum = jnp.cumsum(log_g.reshape(b, nc, GLA_CHUNK, GLA_H, GLA_DK), axis=2)
    qg = qc * jnp.exp(gcum)
    kg = kc * jnp.exp(-gcum)
    mask = jnp.tril(jnp.ones((GLA_CHUNK, GLA_CHUNK), bool))
    att = jnp.where(mask, jnp.einsum('bcihd,bcjhd->bchij', qg, kg), 0.0)
    o = jnp.einsum('bchij,bcjhv->bcihv', att, vc) + jnp.einsum('bcihd,bchdv->bcihv', qg, s_prev)
    return o.reshape(b, n, GLA_H, GLA_DV)


def gla_merge(os_, r, norm_w):
    b, n = r.shape[:2]
    o = rms_norm(os_[0] + os_[1], norm_w.reshape(GLA_H, GLA_DV))
    return o.reshape(b, n, GLA_H * GLA_DV) * jax.nn.silu(r)


def gla_mixer(p_c, p_l, gate_w, gate_b, norm_w):
    q_c, k_c, v_c, g_c, r_c = gla_inputs(p_c, gate_w, gate_b)
    q_l, k_l, v_l, g_l, r_l = gla_inputs(p_l, gate_w, gate_b)
    s0 = jnp.zeros((p_c.shape[0], GLA_H, GLA_DK, GLA_DV), F32)
    os_l = []
    for d, rev in enumerate((False, True)):
        f = functools.partial(maybe_flip, rev=rev)
        _, sf_c = gla_states(f(k_c), f(v_c), f(g_c[:, :, d]), s0)
        sp_l, _ = gla_states(f(k_l), f(v_l), f(g_l[:, :, d]), sf_c)
        os_l.append(f(gla_output(f(q_l), f(k_l), f(v_l), f(g_l[:, :, d]), sp_l)))
    return gla_merge(os_l, r_l, norm_w)


def rglru_inputs(p, conv_w, conv_b, w_a, b_a, w_x, b_x, lam):
    b, n, _ = p.shape
    u = dwconv(p[..., :D_RG], conv_w, conv_b)
    ub = u.reshape(b, n, RG_BLOCKS, RG_BW)
    r = jax.nn.sigmoid((jnp.einsum('blni,enio->bleno', ub, w_a).reshape(b, n, 2, D_RG) + b_a).astype(F32))
    i = jax.nn.sigmoid((jnp.einsum('blni,enio->bleno', ub, w_x).reshape(b, n, 2, D_RG) + b_x).astype(F32))
    log_a = -RG_C * jax.nn.softplus(-lam.astype(F32)) * r
    x_in = jnp.sqrt(-jnp.expm1(2.0 * log_a)) * i * u[:, :, None, :].astype(F32)
    return p[..., D_RG:], jnp.exp(log_a), x_in


def lru_scan(a, u, h0):
    u = u.at[:, 0].add(a[:, 0] * h0)

    def combine(lhs, rhs):
        a1, b1 = lhs
        a2, b2 = rhs
        return a1 * a2, a2 * b1 + b2

    return lax.associative_scan(combine, (a, u), axis=1)[1]


def rglru_mixer(p_c, p_l, conv_w, conv_b, w_a, b_a, w_x, b_x, lam):
    gb_c, a_c, u_c = rglru_inputs(p_c, conv_w, conv_b, w_a, b_a, w_x, b_x, lam)
    gb_l, a_l, u_l = rglru_inputs(p_l, conv_w, conv_b, w_a, b_a, w_x, b_x, lam)
    h0 = jnp.zeros((p_c.shape[0], D_RG), F32)
    hs_l = []
    for d, rev in enumerate((False, True)):
        f = functools.partial(maybe_flip, rev=rev)
        h_c = f(lru_scan(f(a_c[:, :, d]), f(u_c[:, :, d]), h0))
        h_end = h_c[:, 0] if rev else h_c[:, -1]
        hs_l.append(f(lru_scan(f(a_l[:, :, d]), f(u_l[:, :, d]), h_end)))
    return (hs_l[0] + hs_l[1]) * jax.nn.gelu(gb_l.astype(F32))


def odd_layer_mixer(xs, mods, norm_w, w_in, w_out, gla_args, rg_args):
    bsz = xs.shape[0]
    h = norm_mod(xs, norm_w, mods, 0, 1)
    h = jnp.concatenate([to_col_major(h[:, :SEQ]), h[:, SEQ:]], axis=1)
    p = pmatmul(h.reshape(bsz * NTOK, D_MODEL), w_in.astype(BF16)).reshape(bsz, NTOK, OD_IN)
    p_l, p_c = p[:, :SEQ], p[:, SEQ:]
    a_l = gla_mixer(p_c[..., :GLA_IN], p_l[..., :GLA_IN], *gla_args)
    r_l = rglru_mixer(p_c[..., GLA_IN:], p_l[..., GLA_IN:], *rg_args)
    mix = from_col_major(jnp.concatenate([a_l, r_l], axis=-1)).astype(BF16)
    return mm_resid([mix], [w_out.astype(BF16)], xs, mods, 2, N_LAT_TILES)


def kernel(x, c, ctx, c_ctx, ada_w, ada_b, norm1_w, norm2_w, ev_w_in, ev_w_out, hy_conv_w, hy_conv_b, hy_fw0, hy_fb0, hy_fw1, hy_fb1, hy_fw2, hy_fb2, hy_fw3, hy_freq, hy_bias, ssd_conv_w, ssd_conv_b, ssd_dt_bias, ssd_a_log, ssd_d, ssd_norm_w, od_w_in, od_w_out, gla_gate_w, gla_gate_b, gla_norm_w, rg_conv_w, rg_conv_b, rg_w_a, rg_b_a, rg_w_x, rg_b_x, rg_lambda, router_w, router_b, moe_w_gate, moe_w_up, moe_w_down, sh_w_gate, sh_w_up, sh_w_down, final_norm_w):
    xs = jnp.concatenate([x, ctx], axis=1)
    for i in range(DEPTH):
        last = i == DEPTH - 1
        j = i // 2
        mods = adaln_table(c, c_ctx, ada_w[i], ada_b[i])
        if i % 2 == 0:
            hy_filt = (hy_fw0[j], hy_fb0[j], hy_fw1[j], hy_fb1[j], hy_fw2[j], hy_fb2[j], hy_fw3[j], hy_freq[j])
            xs = even_layer_mixer(xs, mods, norm1_w[i], ev_w_in[j], ev_w_out[j], hy_conv_w[j], hy_conv_b[j],
                                  hy_filt, hy_bias[j], ssd_conv_w[j], ssd_conv_b[j], ssd_dt_bias[j],
                                  ssd_a_log[j], ssd_d[j], ssd_norm_w[j])
        else:
            gla_args = (gla_gate_w[j], gla_gate_b[j], gla_norm_w[j])
            rg_args = (rg_conv_w[j], rg_conv_b[j], rg_w_a[j], rg_b_a[j], rg_w_x[j], rg_b_x[j], rg_lambda[j])
            xs = odd_layer_mixer_pallas(xs, mods, norm1_w[i], od_w_in[j], od_w_out[j], gla_args, rg_args)
        n_tiles = N_LAT_TILES if last else N_ROW_TILES
        xs = moe_layer(xs, norm2_w[i], mods, router_w[i], router_b[i], moe_w_gate, moe_w_up, moe_w_down,
                       sh_w_gate[i], sh_w_up[i], sh_w_down[i], n_tiles, i)
    return final_norm(xs, final_norm_w)
```

```python
import functools
import math

import jax
import jax.numpy as jnp
from jax import lax
from jax.experimental import pallas as pl
from jax.experimental.pallas import tpu as pltpu

D_MODEL = 1024
BATCH = 16
SEQ = 2048
DEPTH = 2

CTX_LEN = 256
GRID_W = 64
NORM_EPS = 1e-6

D_HY = D_MODEL // 2
HY_ORDER = 2
HY_SHORT = 3
HY_BANDS = 16
HY_EMB = 1 + 2 * HY_BANDS
HY_FF = 64
HY_DECAY_PCT_LO = 0.3
HY_DECAY_PCT_HI = 1.5
HY_DECAY_TARGET = 1e-2
HY_IN = 3 * D_HY

D_SSM = D_MODEL // 2
SSD_P = 64
SSD_H = D_SSM // SSD_P
SSD_G = 2
SSD_HPG = SSD_H // SSD_G
SSD_N = 128
SSD_CONV = 4
SSD_CHUNK = 128
SSD_XBC = D_SSM + 2 * SSD_G * SSD_N
SSD_IN = D_SSM + SSD_XBC + 2 * SSD_H
EV_IN = HY_IN + SSD_IN
EV_MIX = D_HY + D_SSM

GLA_H = 4
GLA_DV = (D_MODEL // 2) // GLA_H
GLA_DK = GLA_DV // 2
GLA_RANK = 16
GLA_GATE_NORM = 16.0
GLA_CHUNK = 64
GLA_IN = 2 * GLA_H * GLA_DK + 2 * GLA_H * GLA_DV + 2 * GLA_RANK

D_RG = D_MODEL // 2
RG_BLOCKS = 8
RG_BW = D_RG // RG_BLOCKS
RG_CONV = 4
RG_C = 8.0
RG_IN = 2 * D_RG
OD_IN = GLA_IN + RG_IN
OD_MIX = GLA_H * GLA_DV + D_RG

MOE_EXPERTS = 64
MOE_TOPK = 8
MOE_D_EXPERT = 256
MOE_D_SHARED = 256
MOE_SCALE = 2.5
MOE_BLOCK = 256

F32 = jnp.float32
BF16 = jnp.bfloat16
HIGHEST = lax.Precision.HIGHEST

NTOK = SEQ + CTX_LEN
ROW_TILE = 256
N_ROW_TILES = NTOK // ROW_TILE
N_LAT_TILES = SEQ // ROW_TILE

VMEM_LIMIT = 48 * 1024 * 1024
VMEM_LIMIT_BIG = 56 * 1024 * 1024


def _cparams(sem, limit=VMEM_LIMIT):
    return pltpu.CompilerParams(dimension_semantics=sem, vmem_limit_bytes=limit)


def _pick_tile(n, pref):
    t = min(n, pref)
    while n % t:
        t //= 2
    return t


def _silu(x):
    return x / (1.0 + jnp.exp(-x))


def _softplus(x):
    return jnp.maximum(x, 0.0) + jnp.log(1.0 + jnp.exp(-jnp.abs(x)))


def _mm_bf16_body(a_ref, w_ref, o_ref):
    o_ref[...] = jnp.dot(a_ref[...].astype(BF16), w_ref[...].astype(BF16),
                         preferred_element_type=F32).astype(o_ref.dtype)


def _mm_f32_body(a_ref, w_ref, o_ref):
    o_ref[...] = jnp.dot(a_ref[...], w_ref[...], preferred_element_type=F32,
                         precision=HIGHEST).astype(o_ref.dtype)


def pmatmul(a, w, *, exact=False, out_dtype=F32, tm=512, tn=None):
    m, k = a.shape
    n = w.shape[1]
    tm = _pick_tile(m, tm)
    tn = n if tn is None else _pick_tile(n, tn)
    body = _mm_f32_body if exact else _mm_bf16_body
    return pl.pallas_call(
        body,
        name='mm',
        grid=(m // tm, n // tn),
        in_specs=[pl.BlockSpec((tm, k), lambda i, j: (i, 0)),
                  pl.BlockSpec((k, tn), lambda i, j: (0, j))],
        out_specs=pl.BlockSpec((tm, tn), lambda i, j: (i, j)),
        out_shape=jax.ShapeDtypeStruct((m, n), out_dtype),
        compiler_params=_cparams(("parallel", "parallel")),
    )(a, w)


def _mm_split_body(a_ref, w_ref, *o_refs, splits):
    a = a_ref[...]
    for o_ref, (start, width) in zip(o_refs, splits):
        o_ref[...] = jnp.dot(a, w_ref[:, start:start + width],
                             preferred_element_type=F32).astype(o_ref.dtype)


def mm_split(a, w, splits, dtypes, tm=512):
    m, k = a.shape
    n = w.shape[1]
    tm = _pick_tile(m, tm)
    return pl.pallas_call(
        functools.partial(_mm_split_body, splits=tuple(splits)),
        name='mm_split',
        grid=(m // tm,),
        in_specs=[pl.BlockSpec((tm, k), lambda i: (i, 0)),
                  pl.BlockSpec((k, n), lambda i: (0, 0))],
        out_specs=[pl.BlockSpec((tm, wd), lambda i: (i, 0)) for _, wd in splits],
        out_shape=[jax.ShapeDtypeStruct((m, wd), dt) for (_, wd), dt in zip(splits, dtypes)],
        compiler_params=_cparams(("parallel",)),
    )(a, w)


def _mm_resid_body(*refs, n_pairs):
    a_refs = refs[:n_pairs]
    w_refs = refs[n_pairs:2 * n_pairs]
    x_ref, g_ref, o_ref = refs[2 * n_pairs:]
    acc = jnp.dot(a_refs[0][0], w_refs[0][...], preferred_element_type=F32)
    for a_ref, w_ref in zip(a_refs[1:], w_refs[1:]):
        acc = acc + jnp.dot(a_ref[0], w_ref[...], preferred_element_type=F32)
    o_ref[0] = x_ref[0] + g_ref[...] * acc


def mm_resid(a_list, w_list, xs, mods, gate_idx, n_tiles):
    b, nt, d = xs.shape
    n_pairs = len(a_list)
    in_specs = [pl.BlockSpec((1, ROW_TILE, a.shape[-1]), lambda i, j: (i, j, 0)) for a in a_list]
    in_specs += [pl.BlockSpec(w.shape, lambda i, j: (0, 0)) for w in w_list]
    in_specs += [pl.BlockSpec((1, ROW_TILE, d), lambda i, j: (i, j, 0)),
                 _mod_spec(gate_idx, d)]
    return pl.pallas_call(
        functools.partial(_mm_resid_body, n_pairs=n_pairs),
        name='mm_resid',
        grid=(b, n_tiles),
        in_specs=in_specs,
        out_specs=pl.BlockSpec((1, ROW_TILE, d), lambda i, j: (i, j, 0)),
        out_shape=jax.ShapeDtypeStruct(xs.shape, F32),
        input_output_aliases={2 * n_pairs: 0},
        compiler_params=_cparams(("parallel", "parallel")),
    )(*a_list, *w_list, xs, mods)


def _mod_spec(idx, d, b0=0):
    return pl.BlockSpec((None, None, None, 1, d), lambda i, j: (i + b0, 1 - j // N_LAT_TILES, idx, 0, 0))


def adaln_table(c, c_ctx, w, b):
    cv = jax.nn.silu(jnp.concatenate([c, c_ctx[None, :]], axis=0))
    cv = jnp.pad(cv, ((0, 24 - cv.shape[0]), (0, 0)))
    m = pmatmul(cv, w, exact=True, tn=1536)[:BATCH + 1] + b
    per_sample = m[:BATCH]
    ctx_row = jnp.broadcast_to(m[BATCH][None, :], per_sample.shape)
    return jnp.stack([ctx_row, per_sample], axis=1).reshape(BATCH, 2, 6, 1, D_MODEL)


def _norm_mod(x, w, shift, scale):
    ms = jnp.mean(x * x, axis=-1, keepdims=True)
    return (x * lax.rsqrt(ms + NORM_EPS) * w) * (1.0 + scale) + shift


def _norm_mod_body(x_ref, w_ref, sh_ref, sc_ref, o_ref):
    o_ref[0] = _norm_mod(x_ref[0], w_ref[...], sh_ref[...], sc_ref[...]).astype(o_ref.dtype)


def norm_mod(xs, w, mods, shift_idx, scale_idx):
    b, nt, d = xs.shape
    return pl.pallas_call(
        _norm_mod_body,
        name='norm_mod',
        grid=(b, nt // ROW_TILE),
        in_specs=[pl.BlockSpec((1, ROW_TILE, d), lambda i, j: (i, j, 0)),
                  pl.BlockSpec((1, d), lambda i, j: (0, 0)),
                  _mod_spec(shift_idx, d), _mod_spec(scale_idx, d)],
        out_specs=pl.BlockSpec((1, ROW_TILE, d), lambda i, j: (i, j, 0)),
        out_shape=jax.ShapeDtypeStruct(xs.shape, BF16),
        compiler_params=_cparams(("parallel", "parallel")),
    )(xs, w.reshape(1, d), mods, mods)


def _final_norm_body(x_ref, w_ref, o_ref):
    x = x_ref[0]
    ms = jnp.mean(x * x, axis=-1, keepdims=True)
    o_ref[0] = x * lax.rsqrt(ms + NORM_EPS) * w_ref[...]


def final_norm(xs, w):
    b, _, d = xs.shape
    return pl.pallas_call(
        _final_norm_body,
        name='final_norm',
        grid=(b, N_LAT_TILES),
        in_specs=[pl.BlockSpec((1, ROW_TILE, d), lambda i, j: (i, j, 0)),
                  pl.BlockSpec((1, d), lambda i, j: (0, 0))],
        out_specs=pl.BlockSpec((1, ROW_TILE, d), lambda i, j: (i, j, 0)),
        out_shape=jax.ShapeDtypeStruct((b, SEQ, d), F32),
        compiler_params=_cparams(("parallel", "parallel")),
    )(xs, w.reshape(1, d))


def _route_body(x_ref, w_ref, sh_ref, sc_ref, rw_ref, rb_ref, h_ref, idx_ref, wsel_ref, rank_ref, cnt_ref):
    first = (pl.program_id(0) == 0) & (pl.program_id(1) == 0)

    @pl.when(first)
    def _():
        cnt_ref[...] = jnp.zeros_like(cnt_ref)

    h = _norm_mod(x_ref[0], w_ref[...], sh_ref[...], sc_ref[...])
    h_ref[0] = h.astype(h_ref.dtype)
    logits = jnp.dot(h, rw_ref[...], preferred_element_type=F32, precision=HIGHEST)
    scores = 1.0 / (1.0 + jnp.exp(-logits))
    tm, ne = scores.shape
    lane = lax.broadcasted_iota(jnp.int32, (tm, ne), 1).astype(F32)
    slot = lax.broadcasted_iota(jnp.int32, (tm, MOE_TOPK), 1)
    sel = scores + rb_ref[...]
    picked = jnp.zeros((tm, ne), F32)
    hits = []
    idx_out = jnp.zeros((tm, MOE_TOPK), F32)
    w_out = jnp.zeros((tm, MOE_TOPK), F32)
    for k in range(MOE_TOPK):
        m = jnp.max(sel, axis=-1, keepdims=True)
        ik = jnp.min(jnp.where(sel == m, lane, float(ne)), axis=-1, keepdims=True)
        hit = lane == ik
        wk = jnp.sum(jnp.where(hit, scores, 0.0), axis=-1, keepdims=True)
        sel = jnp.where(hit, -jnp.inf, sel)
        picked = picked + hit.astype(F32)
        hits.append(hit)
        idx_out = jnp.where(slot == k, ik, idx_out)
        w_out = jnp.where(slot == k, wk, w_out)
    wsum = jnp.sum(w_out, axis=-1, keepdims=True)
    wsel_ref[0] = w_out / wsum * MOE_SCALE
    idx_ref[0] = idx_out.astype(jnp.int32)
    r_i = lax.broadcasted_iota(jnp.int32, (tm, tm), 0)
    c_i = lax.broadcasted_iota(jnp.int32, (tm, tm), 1)
    strict_lower = (c_i < r_i).astype(BF16)
    before = jnp.dot(strict_lower, picked.astype(BF16), preferred_element_type=F32) + cnt_ref[...]
    rank_out = jnp.zeros((tm, MOE_TOPK), F32)
    for k in range(MOE_TOPK):
        rk = jnp.sum(jnp.where(hits[k], before, 0.0), axis=-1, keepdims=True)
        rank_out = jnp.where(slot == k, rk, rank_out)
    rank_ref[0] = rank_out.astype(jnp.int32)
    cnt_ref[...] = cnt_ref[...] + jnp.sum(picked, axis=0, keepdims=True)


def route(xs, w, mods, router_w, router_b, n_tiles, b0, b):
    d = xs.shape[-1]
    rows = n_tiles * ROW_TILE
    small = lambda dt: jax.ShapeDtypeStruct((b, rows, MOE_TOPK), dt)
    small_spec = pl.BlockSpec((1, ROW_TILE, MOE_TOPK), lambda i, j: (i, j, 0))
    return pl.pallas_call(
        _route_body,
        name='route',
        grid=(b, n_tiles),
        in_specs=[pl.BlockSpec((1, ROW_TILE, d), lambda i, j: (i + b0, j, 0)),
                  pl.BlockSpec((1, d), lambda i, j: (0, 0)),
                  _mod_spec(3, d, b0), _mod_spec(4, d, b0),
                  pl.BlockSpec((d, MOE_EXPERTS), lambda i, j: (0, 0)),
                  pl.BlockSpec((1, MOE_EXPERTS), lambda i, j: (0, 0))],
        out_specs=[pl.BlockSpec((1, ROW_TILE, d), lambda i, j: (i, j, 0)),
                   small_spec, small_spec, small_spec,
                   pl.BlockSpec((1, MOE_EXPERTS), lambda i, j: (0, 0))],
        out_shape=[jax.ShapeDtypeStruct((b, rows, d), BF16), small(jnp.int32), small(F32), small(jnp.int32),
                   jax.ShapeDtypeStruct((1, MOE_EXPERTS), F32)],
        compiler_params=_cparams(("arbitrary", "arbitrary")),
    )(xs, w.reshape(1, d), mods, mods, router_w, router_b.reshape(1, MOE_EXPERTS))


def _swiglu(x, wg, wu, wd):
    g = jnp.dot(x, wg, preferred_element_type=F32)
    u = jnp.dot(x, wu, preferred_element_type=F32)
    h = (_silu(g) * u).astype(BF16)
    return jnp.dot(h, wd, preferred_element_type=F32)


def _expert_body(be_ref, nu_ref, x_ref, wg_ref, wu_ref, wd_ref, o_ref, wg_s, wu_s, wd_s):
    i = pl.program_id(0)
    used = i < nu_ref[0]

    @pl.when(used & ((i == 0) | (be_ref[i] != be_ref[jnp.maximum(i - 1, 0)])))
    def _():
        wg_s[...] = wg_ref[0].astype(BF16)
        wu_s[...] = wu_ref[0].astype(BF16)
        wd_s[...] = wd_ref[0].astype(BF16)

    @pl.when(used)
    def _():
        o_ref[...] = _swiglu(x_ref[...], wg_s[...], wu_s[...], wd_s[...]).astype(o_ref.dtype)

    @pl.when(jnp.logical_not(used))
    def _():
        o_ref[...] = jnp.zeros_like(o_ref)


def moe_experts(x_rows, block_e, n_used, wg, wu, wd, layer):
    rows, d = x_rows.shape
    n_blocks = rows // MOE_BLOCK
    f = wg.shape[-1]
    grid_spec = pltpu.PrefetchScalarGridSpec(
        num_scalar_prefetch=2,
        grid=(n_blocks,),
        in_specs=[
            pl.BlockSpec((MOE_BLOCK, d), lambda i, be, nu: (i, 0)),
            pl.BlockSpec((None, 1, d, f), lambda i, be, nu: (layer, be[i], 0, 0)),
            pl.BlockSpec((None, 1, d, f), lambda i, be, nu: (layer, be[i], 0, 0)),
            pl.BlockSpec((None, 1, f, d), lambda i, be, nu: (layer, be[i], 0, 0)),
        ],
        out_specs=pl.BlockSpec((MOE_BLOCK, d), lambda i, be, nu: (i, 0)),
        scratch_shapes=[pltpu.VMEM((d, f), BF16), pltpu.VMEM((d, f), BF16), pltpu.VMEM((f, d), BF16)],
    )
    return pl.pallas_call(
        _expert_body,
        name='experts',
        grid_spec=grid_spec,
        out_shape=jax.ShapeDtypeStruct((rows, d), BF16),
        compiler_params=_cparams(("arbitrary",)),
    )(block_e, n_used, x_rows, wg, wu, wd)


def _shared_resid_body(h_ref, wg_ref, wu_ref, wd_ref, pk_ref, ws_ref, x_ref, g_ref, o_ref):
    y = _swiglu(h_ref[0], wg_ref[...], wu_ref[...], wd_ref[...])
    ws = ws_ref[0]
    for k in range(MOE_TOPK):
        y = y + ws[:, k:k + 1] * pk_ref[k, 0].astype(F32)
    o_ref[0] = x_ref[0] + g_ref[...] * y


def shared_resid(h, picked, wsel, xs, mods, wg, wu, wd, n_tiles, b0):
    b, _, d = h.shape
    f = wg.shape[-1]
    tile = pl.BlockSpec((1, ROW_TILE, d), lambda i, j: (i, j, 0))
    xs_tile = pl.BlockSpec((1, ROW_TILE, d), lambda i, j: (i + b0, j, 0))
    return pl.pallas_call(
        _shared_resid_body,
        name='shared_resid',
        grid=(b, n_tiles),
        in_specs=[tile,
                  pl.BlockSpec((d, f), lambda i, j: (0, 0)),
                  pl.BlockSpec((d, f), lambda i, j: (0, 0)),
                  pl.BlockSpec((f, d), lambda i, j: (0, 0)),
                  pl.BlockSpec((MOE_TOPK, 1, ROW_TILE, d), lambda i, j: (0, i, j, 0)),
                  pl.BlockSpec((1, ROW_TILE, MOE_TOPK), lambda i, j: (i, j, 0)),
                  xs_tile, _mod_spec(5, d, b0)],
        out_specs=xs_tile,
        out_shape=jax.ShapeDtypeStruct(xs.shape, F32),
        input_output_aliases={6: 0},
        compiler_params=_cparams(("parallel", "parallel")),
    )(h, wg, wu, wd, picked, wsel, xs, mods)


MOE_SPLITS = 1


def moe_layer(xs, norm_w, mods, router_w, router_b, w_gate, w_up, w_down, sh_gate, sh_up, sh_down, n_tiles, layer):
    bsz = xs.shape[0]
    shared_w = (sh_gate.astype(BF16), sh_up.astype(BF16), sh_down.astype(BF16))
    b = bsz // MOE_SPLITS
    for g in range(MOE_SPLITS):
        xs = _moe_group(xs, norm_w, mods, router_w, router_b, w_gate, w_up, w_down, shared_w, n_tiles, layer, g * b, b)
    return xs


def _moe_group(xs, norm_w, mods, router_w, router_b, w_gate, w_up, w_down, shared_w, n_tiles, layer, b0, b):
    d = xs.shape[-1]
    rows_per_sample = n_tiles * ROW_TILE
    n = b * rows_per_sample
    h, idx, wsel, rank, counts = route(xs, norm_w, mods, router_w, router_b, n_tiles, b0, b)
    counts = counts[0].astype(jnp.int32)
    padded = (counts + MOE_BLOCK - 1) // MOE_BLOCK * MOE_BLOCK
    ends = jnp.cumsum(padded)
    starts = ends - padded
    nk = n * MOE_TOPK
    n_blocks = -(-nk // MOE_BLOCK) + MOE_EXPERTS
    rows = n_blocks * MOE_BLOCK
    n_pad = rows - nk
    e_iota = jnp.arange(MOE_EXPERTS, dtype=jnp.int32)
    dest = jnp.sum(jnp.where(idx[..., None] == e_iota, starts, 0), axis=-1) + rank
    blk_start = jnp.arange(n_blocks, dtype=jnp.int32) * MOE_BLOCK
    block_e = jnp.minimum(jnp.sum(ends[None, :] <= blk_start[:, None], axis=1), MOE_EXPERTS - 1).astype(jnp.int32)
    n_used = (ends[-1:] // MOE_BLOCK).astype(jnp.int32)
    pad = padded - counts
    cum_pad = jnp.cumsum(pad)
    m = jnp.arange(n_pad, dtype=jnp.int32)
    e_m = jnp.sum(cum_pad[None, :] <= m[:, None], axis=1)
    base = jnp.sum(jnp.where(jnp.minimum(e_m, MOE_EXPERTS - 1)[:, None] == e_iota,
                             starts + counts - (cum_pad - pad), 0), axis=1)
    pad_row = jnp.where(e_m < MOE_EXPERTS, base + m, ends[-1] + m - cum_pad[-1])
    tok = jnp.arange(nk, dtype=jnp.int32) // MOE_TOPK
    _, row_tok = lax.sort((jnp.concatenate([dest.reshape(-1), pad_row]).astype(jnp.int32),
                           jnp.concatenate([tok, m % n])), num_keys=1)
    h_flat = h.reshape(n, d)
    x_rows = h_flat[row_tok]
    y_rows = moe_experts(x_rows, block_e, n_used, w_gate, w_up, w_down, layer)
    picked = y_rows[dest.reshape(n, MOE_TOPK).T].reshape(MOE_TOPK, b, rows_per_sample, d)
    return shared_resid(h, picked, wsel, xs, mods, *shared_w, n_tiles, b0)


def _dwconv_body(x_ref, w_ref, b_ref, o_ref, *, width, act):
    chunk = ROW_TILE
    n_chunks = NTOK // chunk
    first_of_seq = (0, N_LAT_TILES)
    last_of_seq = (N_LAT_TILES - 1, n_chunks - 1)
    tc = x_ref.shape[-1]
    halo = 16
    row = lax.broadcasted_iota(jnp.int32, (chunk, tc), 0)
    zero_row = jnp.zeros((1, tc), F32)
    for c in range(n_chunks):
        r0 = c * chunk
        cur = x_ref[0, r0:r0 + chunk, :].astype(F32)
        if c in first_of_seq:
            prev_last = zero_row
        else:
            prev_last = x_ref[0, r0 - halo:r0, :].astype(F32)[halo - 1:halo, :]
        if c in last_of_seq:
            next0 = next1 = zero_row
        else:
            nxt = x_ref[0, r0 + chunk:r0 + chunk + halo, :].astype(F32)
            next0, next1 = nxt[0:1, :], nxt[1:2, :]
        xm1 = jnp.where(row == 0, prev_last, pltpu.roll(cur, 1, 0))
        xp1 = jnp.where(row == chunk - 1, next0, pltpu.roll(cur, chunk - 1, 0))
        y = w_ref[0:1, :] * xm1 + w_ref[1:2, :] * cur + w_ref[2:3, :] * xp1 + b_ref[...]
        if width == 4:
            xp2 = jnp.where(row == chunk - 2, next0,
                            jnp.where(row == chunk - 1, next1, pltpu.roll(cur, chunk - 2, 0)))
            y = y + w_ref[3:4, :] * xp2
        if act:
            y = _silu(y)
        o_ref[0, c * chunk:(c + 1) * chunk, :] = y.astype(o_ref.dtype)


def dwconv_stream(x, w, b, act, tc=256):
    bsz, nt, c = x.shape
    width = w.shape[0]
    return pl.pallas_call(
        functools.partial(_dwconv_body, width=width, act=act),
        name='dwconv',
        grid=(bsz, c // tc),
        in_specs=[pl.BlockSpec((1, nt, tc), lambda i, j: (i, 0, j)),
                  pl.BlockSpec((width, tc), lambda i, j: (0, j)),
                  pl.BlockSpec((1, tc), lambda i, j: (0, j))],
        out_specs=pl.BlockSpec((1, nt, tc), lambda i, j: (i, 0, j)),
        out_shape=jax.ShapeDtypeStruct(x.shape, BF16),
        compiler_params=_cparams(("parallel", "parallel")),
    )(x, w, b.reshape(1, c))


HY_FB = 512


def dft_matrices(n):
    k = jnp.arange(n, dtype=jnp.int32)[:, None]
    t = jnp.arange(n, dtype=jnp.int32)[None, :]
    ang = (2.0 * math.pi / (2 * n)) * ((k * t) % (2 * n)).astype(F32)
    fre = jnp.cos(ang)
    fim = -jnp.sin(ang)
    nyq = jnp.where(t % 2 == 0, 1.0, -1.0).astype(F32)
    fim = jnp.where(k == 0, nyq, fim)
    fwd = jnp.concatenate([fre, fim], axis=0)
    colscale = jnp.where(jnp.arange(2 * n) % n == 0, 0.5, 1.0) / n
    inv = fwd.T * colscale[None, :]
    return fwd, inv


def hyena_filter_taps(n, fw0, fb0, fw1, fb1, fw2, fb2, fw3, freq):
    pos = jnp.arange(n, dtype=F32)
    t = pos / max(n - 1, 1)
    bands = jnp.linspace(1e-4, HY_BANDS - 1, HY_BANDS, dtype=F32)
    ang = (2.0 * math.pi / n) * pos[:, None] * bands[None, :]
    feats = jnp.concatenate([t[:, None], jnp.cos(ang), -jnp.sin(ang)], axis=-1)
    h = jnp.sin(freq * (jnp.dot(feats, fw0, precision=HIGHEST) + fb0))
    h = jnp.sin(freq * (jnp.dot(h, fw1, precision=HIGHEST) + fb1))
    h = jnp.sin(freq * (jnp.dot(h, fw2, precision=HIGHEST) + fb2))
    h = pmatmul(h, fw3, exact=True).reshape(n, 2, HY_ORDER, D_HY)
    deltas = jnp.abs(jnp.linspace(math.log(HY_DECAY_PCT_LO) / HY_DECAY_TARGET,
                                  math.log(HY_DECAY_PCT_HI) / HY_DECAY_TARGET, D_HY, dtype=F32))
    h = h * jnp.exp(-t[:, None] * deltas)[:, None, None, :]
    h0 = h[:, 0]
    h1 = h[:, 1].at[0].set(0.0)
    norm = jnp.sum(jnp.abs(h0), axis=0, keepdims=True) + jnp.sum(jnp.abs(h1), axis=0, keepdims=True)
    h0 = (h0 / norm).reshape(n, HY_ORDER * D_HY)
    h1 = (h1 / norm).reshape(n, HY_ORDER * D_HY)
    return h0 + h1, h0 - h1


def _split_bf16(a):
    hi = a.astype(BF16)
    return hi, (a - hi.astype(F32)).astype(BF16)


def hyena_spectrum(fwd, hsum, hdiff, fb):
    n = hsum.shape[0]
    f_hi, f_lo = _split_bf16(fwd)

    def dft(h):
        h_hi, h_lo = _split_bf16(h)
        return pmatmul(f_hi, h_hi) + pmatmul(f_hi, h_lo) + pmatmul(f_lo, h_hi)

    a = dft(hsum)
    bm = dft(hdiff)
    sr = a[:n]
    si = bm[n:]
    nyq = a[n]
    first = (jnp.arange(n) == 0)[:, None]
    p = sr
    q = jnp.where(first, 0.0, si)
    s = jnp.where(first, nyq[None, :], sr)
    spec = jnp.stack([p, q, s], axis=0).reshape(3, n // fb, fb, HY_ORDER, D_HY)
    return spec.transpose(3, 1, 0, 2, 4), f_hi


def _hyena_body(u_ref, fre_ref, fim_ref, gre_ref, gim_ref, sp_ref, bias_ref, prev_ref, o_ref,
                vin, acc, *, nf):
    del prev_ref
    o = pl.program_id(1)
    f = pl.program_id(2)
    c = D_HY

    @pl.when((o == 0) & (f == 0))
    def _():
        vin[...] = u_ref[0, :, 0:c]

    @pl.when(f == 0)
    def _():
        acc[...] = jnp.zeros_like(acc)

    v = vin[...]
    vr = jnp.dot(fre_ref[...], v, preferred_element_type=F32)
    vi = jnp.dot(fim_ref[...], v, preferred_element_type=F32)
    p, q, s = sp_ref[0], sp_ref[1], sp_ref[2]
    zr = (vr * p - vi * q).astype(BF16)
    zi = (vr * q + vi * s).astype(BF16)
    acc[...] += (jnp.dot(gre_ref[...], zr, preferred_element_type=F32)
                 + jnp.dot(gim_ref[...], zi, preferred_element_type=F32))

    @pl.when((o == 0) & (f == nf - 1))
    def _():
        z = u_ref[0, :, c:2 * c].astype(F32) * (acc[...] + bias_ref[0:1, :] * vin[...].astype(F32))
        vin[...] = z.astype(BF16)

    @pl.when((o == 1) & (f == nf - 1))
    def _():
        y = u_ref[0, :, 2 * c:3 * c].astype(F32) * (acc[...] + bias_ref[1:2, :] * vin[...].astype(F32))
        o_ref[0] = y.astype(o_ref.dtype)


def hyena_long_conv(u, fwd_bf16, inv_bf16, spec, bias, n, row_block, prev_out):
    bsz = u.shape[0]
    fb = spec.shape[3]
    nf = n // fb
    out_shape = jax.ShapeDtypeStruct((bsz, NTOK, D_HY), BF16)
    if prev_out is None:
        prev_out = jnp.zeros(out_shape.shape, BF16)
    args = [u, fwd_bf16, fwd_bf16, inv_bf16, inv_bf16, spec, bias, prev_out]
    aliases = {7: 0}
    return pl.pallas_call(
        functools.partial(_hyena_body, nf=nf),
        name='hyena',
        grid=(bsz, HY_ORDER, nf),
        in_specs=[pl.BlockSpec((1, n, 3 * D_HY), lambda b, o, f: (b, row_block, 0)),
                  pl.BlockSpec((fb, n), lambda b, o, f: (f, 0)),
                  pl.BlockSpec((fb, n), lambda b, o, f: (nf + f, 0)),
                  pl.BlockSpec((n, fb), lambda b, o, f: (0, f)),
                  pl.BlockSpec((n, fb), lambda b, o, f: (0, nf + f)),
                  pl.BlockSpec((None, None, 3, fb, D_HY), lambda b, o, f: (o, f, 0, 0, 0)),
                  pl.BlockSpec((HY_ORDER, D_HY), lambda b, o, f: (0, 0)),
                  pl.BlockSpec(memory_space=pl.ANY)],
        out_specs=pl.BlockSpec((1, n, D_HY), lambda b, o, f: (b, row_block, 0)),
        out_shape=out_shape,
        scratch_shapes=[pltpu.VMEM((n, D_HY), BF16), pltpu.VMEM((n, D_HY), F32)],
        input_output_aliases=aliases,
        compiler_params=_cparams(("parallel", "arbitrary", "arbitrary"), VMEM_LIMIT_BIG),
    )(*args)


def hyena_mixer_stream(p_hy, conv_w, conv_b, filt, bias):
    u = dwconv_stream(p_hy, conv_w, conv_b, act=False)
    out = None
    for n, row_block in ((SEQ, 0), (CTX_LEN, SEQ // CTX_LEN)):
        fb = min(HY_FB, n)
        fwd, inv = dft_matrices(n)
        hsum, hdiff = hyena_filter_taps(n, *filt)
        spec, fwd_bf16 = hyena_spectrum(fwd, hsum, hdiff, fb)
        out = hyena_long_conv(u, fwd_bf16, inv.astype(BF16), spec, bias, n, row_block, out)
    return out


def _tri(n, kind):
    r = lax.broadcasted_iota(jnp.int32, (n, n), 0)
    c = lax.broadcasted_iota(jnp.int32, (n, n), 1)
    return (c <= r) if kind == 'lower' else (c >= r)


def _ssd_dir(xbc_ref, dt_ref, dtt_ref, bias_r, bias_c, a_r, a_c, st_ref, y_ref, *, d, reverse):
    q = SSD_CHUNK
    nh = SSD_H
    gw = SSD_HPG * SSD_P
    lower = _tri(q, 'lower')
    upper = _tri(q, 'upper')
    lower_f = lower.astype(F32)
    upper_f = upper.astype(F32)
    dt_col = _softplus(dt_ref[0] + bias_r)
    dt_row = _softplus(dtt_ref[0] + bias_c)
    da_col = dt_col * a_r
    da_row = dt_row * a_c
    if not reverse:
        acs_col = jnp.dot(lower_f, da_col, preferred_element_type=F32, precision=HIGHEST)
        acs_row = jnp.dot(da_row, upper_f, preferred_element_type=F32, precision=HIGHEST)
        mask = lower
        edge = q - 1
    else:
        acs_col = jnp.dot(upper_f, da_col, preferred_element_type=F32, precision=HIGHEST)
        acs_row = jnp.dot(da_row, lower_f, preferred_element_type=F32, precision=HIGHEST)
        mask = upper
        edge = 0
    h0 = d * nh
    hh = lax.broadcasted_iota(jnp.int32, (2 * nh, nh * SSD_P), 0)
    cc = lax.broadcasted_iota(jnp.int32, (2 * nh, nh * SSD_P), 1) // SSD_P
    expand = (hh == cc + h0).astype(F32)
    acs_c = jnp.dot(acs_col, expand, preferred_element_type=F32, precision=HIGHEST)
    dt_c = jnp.dot(dt_col, expand, preferred_element_type=F32, precision=HIGHEST)
    total_c = acs_c[edge:edge + 1, :]
    e_in_c = jnp.exp(acs_c)
    w_end_c = jnp.exp(total_c - acs_c) * dt_c
    dec_c = jnp.exp(total_c)
    xs = xbc_ref[0, :, 0:D_SSM]
    xs_f = xs.astype(F32)
    for g in range(SSD_G):
        bm = xbc_ref[0, :, D_SSM + g * SSD_N:D_SSM + (g + 1) * SSD_N]
        cm = xbc_ref[0, :, D_SSM + SSD_G * SSD_N + g * SSD_N:D_SSM + SSD_G * SSD_N + (g + 1) * SSD_N]
        cb = lax.dot_general(cm, bm, (((1,), (1,)), ((), ())), preferred_element_type=F32)
        lws = []
        for k in range(SSD_HPG):
            h = h0 + g * SSD_HPG + k
            seg = acs_col[:, h:h + 1] - acs_row[h:h + 1, :]
            decay = jnp.exp(jnp.where(mask, seg, -jnp.inf))
            lws.append((cb * decay * dt_row[h:h + 1, :]).astype(BF16))
        lw = jnp.concatenate(lws, axis=1)
        xg = xs[:, g * gw:(g + 1) * gw]
        rb = lax.broadcasted_iota(jnp.int32, (SSD_HPG * q, gw), 0) // q
        cbk = lax.broadcasted_iota(jnp.int32, (SSD_HPG * q, gw), 1) // SSD_P
        x_bd = jnp.where(rb == cbk, jnp.concatenate([xg] * SSD_HPG, axis=0), jnp.zeros((), BF16))
        y_in = jnp.dot(lw, x_bd, preferred_element_type=F32)
        st = st_ref[g]
        y_st = jnp.dot(cm, st.astype(BF16), preferred_element_type=F32) * e_in_c[:, g * gw:(g + 1) * gw]
        y_ref[0, :, g * gw:(g + 1) * gw] = y_in + y_st
        xw = (xs_f[:, g * gw:(g + 1) * gw] * w_end_c[:, g * gw:(g + 1) * gw]).astype(BF16)
        upd = lax.dot_general(bm, xw, (((0,), (0,)), ((), ())), preferred_element_type=F32)
        st_ref[g] = st * dec_c[:, g * gw:(g + 1) * gw] + upd


def _ssd_body(xf_ref, dtf_ref, dttf_ref, xb_ref, dtb_ref, dttb_ref, bias_r, bias_c, a_r, a_c,
              yf_ref, yb_ref, stf, stb):
    @pl.when(pl.program_id(1) == 0)
    def _():
        stf[...] = jnp.zeros_like(stf)
        stb[...] = jnp.zeros_like(stb)

    _ssd_dir(xf_ref, dtf_ref, dttf_ref, bias_r[...], bias_c[...], a_r[...], a_c[...], stf, yf_ref,
             d=0, reverse=False)
    _ssd_dir(xb_ref, dtb_ref, dttb_ref, bias_r[...], bias_c[...], a_r[...], a_c[...], stb, yb_ref,
             d=1, reverse=True)


def ssd_scan(xbc, dt, dt_bias, a_log):
    bsz = xbc.shape[0]
    nc = NTOK // SSD_CHUNK
    nlat = SEQ // SSD_CHUNK
    dtt = jnp.swapaxes(dt, 1, 2)
    fwd_chunk = lambda s: (s + nlat) % nc
    bwd_chunk = lambda s: nc - 1 - s
    a = -jnp.exp(a_log.astype(F32)).reshape(1, 2 * SSD_H)
    bias = dt_bias.astype(F32).reshape(1, 2 * SSD_H)
    x_spec = lambda cm: pl.BlockSpec((1, SSD_CHUNK, SSD_XBC), lambda b, s: (b, cm(s), 0))
    dt_spec = lambda cm: pl.BlockSpec((1, SSD_CHUNK, 2 * SSD_H), lambda b, s: (b, cm(s), 0))
    dtt_spec = lambda cm: pl.BlockSpec((1, 2 * SSD_H, SSD_CHUNK), lambda b, s: (b, 0, cm(s)))
    y_spec = lambda cm: pl.BlockSpec((1, SSD_CHUNK, D_SSM), lambda b, s: (b, cm(s), 0))
    row = pl.BlockSpec((1, 2 * SSD_H), lambda b, s: (0, 0))
    col = pl.BlockSpec((2 * SSD_H, 1), lambda b, s: (0, 0))
    y_shape = jax.ShapeDtypeStruct((bsz, NTOK, D_SSM), F32)
    gw = SSD_HPG * SSD_P
    return pl.pallas_call(
        _ssd_body,
        name='ssd_scan',
        grid=(bsz, nc),
        in_specs=[x_spec(fwd_chunk), dt_spec(fwd_chunk), dtt_spec(fwd_chunk),
                  x_spec(bwd_chunk), dt_spec(bwd_chunk), dtt_spec(bwd_chunk),
                  row, col, row, col],
        out_specs=[y_spec(fwd_chunk), y_spec(bwd_chunk)],
        out_shape=[y_shape, y_shape],
        scratch_shapes=[pltpu.VMEM((SSD_G, SSD_N, gw), F32), pltpu.VMEM((SSD_G, SSD_N, gw), F32)],
        compiler_params=_cparams(("parallel", "arbitrary")),
    )(xbc, dt, dtt, xbc, dt, dtt, bias, bias.reshape(-1, 1), a, a.reshape(-1, 1))


def _ssd_merge_body(yf_ref, yb_ref, xbc_ref, z_ref, d_ref, nw_ref, o_ref):
    xs = xbc_ref[0, :, 0:D_SSM].astype(F32)
    z = z_ref[0].astype(F32)
    g = (yf_ref[0] + yb_ref[0] + d_ref[...] * xs) * _silu(z)
    gw = D_SSM // SSD_G
    for k in range(SSD_G):
        gk = g[:, k * gw:(k + 1) * gw]
        ms = jnp.mean(gk * gk, axis=-1, keepdims=True)
        o_ref[0, :, k * gw:(k + 1) * gw] = (gk * lax.rsqrt(ms + NORM_EPS)
                                            * nw_ref[:, k * gw:(k + 1) * gw]).astype(o_ref.dtype)


def ssd_merge(yf, yb, xbc, z, d_skip, norm_w):
    bsz = yf.shape[0]
    tile = lambda w: pl.BlockSpec((1, ROW_TILE, w), lambda i, j: (i, j, 0))
    vec = pl.BlockSpec((1, D_SSM), lambda i, j: (0, 0))
    d_chan = jnp.repeat(d_skip.astype(F32), SSD_P).reshape(1, D_SSM)
    return pl.pallas_call(
        _ssd_merge_body,
        name='ssd_merge',
        grid=(bsz, N_ROW_TILES),
        in_specs=[tile(D_SSM), tile(D_SSM), tile(SSD_XBC), tile(D_SSM), vec, vec],
        out_specs=tile(D_SSM),
        out_shape=jax.ShapeDtypeStruct((bsz, NTOK, D_SSM), BF16),
        compiler_params=_cparams(("parallel", "parallel")),
    )(yf, yb, xbc, z, d_chan, norm_w.reshape(1, D_SSM))


def even_layer_mixer(xs, mods, norm_w, w_in, w_out, hy_conv_w, hy_conv_b, hy_filt, hy_bias,
                     ssd_conv_w, ssd_conv_b, ssd_dt_bias, ssd_a_log, ssd_d, ssd_norm_w):
    bsz = xs.shape[0]
    h = norm_mod(xs, norm_w, mods, 0, 1)
    splits = ((0, HY_IN), (HY_IN, D_SSM), (HY_IN + D_SSM, SSD_XBC), (HY_IN + D_SSM + SSD_XBC, 2 * SSD_H))
    p_hy, z, xbc_raw, dt = mm_split(h.reshape(bsz * NTOK, D_MODEL), w_in.astype(BF16), splits,
                                    (BF16, BF16, BF16, F32))
    to3 = lambda a: a.reshape(bsz, NTOK, a.shape[-1])
    y_hy = hyena_mixer_stream(to3(p_hy), hy_conv_w, hy_conv_b, hy_filt, hy_bias)
    xbc = dwconv_stream(to3(xbc_raw), ssd_conv_w, ssd_conv_b, act=True)
    yf, yb = ssd_scan(xbc, to3(dt), ssd_dt_bias, ssd_a_log)
    s = ssd_merge(yf, yb, xbc, to3(z), ssd_d, ssd_norm_w)
    wo = w_out.astype(BF16)
    return mm_resid([y_hy, s], [wo[:D_HY], wo[D_HY:]], xs, mods, 2, N_ROW_TILES)


GLA_QK = GLA_H * GLA_DK
GLA_V = GLA_H * GLA_DV


def _gla_dir(qkv_ref, lr_ref, gw_ref, gb_ref, st_ref, o_ref, *, d, reverse):
    q = GLA_CHUNK
    tri = _tri(q, 'upper' if reverse else 'lower')
    edge = 0 if reverse else q - 1
    lr = lr_ref[0, :, d * GLA_RANK:(d + 1) * GLA_RANK]
    logit = jnp.dot(lr, gw_ref[d], preferred_element_type=F32, precision=HIGHEST) + gb_ref[d:d + 1, :]
    log_g = -_softplus(-logit) * (1.0 / GLA_GATE_NORM)
    gcum = jnp.dot(tri.astype(F32), log_g, preferred_element_type=F32, precision=HIGHEST)
    total = gcum[edge:edge + 1, :]
    qf = qkv_ref[0, :, 0:GLA_QK].astype(F32)
    kf = qkv_ref[0, :, GLA_QK:2 * GLA_QK].astype(F32)
    v = qkv_ref[0, :, 2 * GLA_QK:2 * GLA_QK + GLA_V]
    qg = (qf * (GLA_DK ** -0.5) * jnp.exp(gcum)).astype(BF16)
    kg = (kf * jnp.exp(-gcum)).astype(BF16)
    kw = (kf * jnp.exp(total - gcum)).astype(BF16)
    rb = lax.broadcasted_iota(jnp.int32, (GLA_H * q, GLA_QK), 0) // q
    cb = lax.broadcasted_iota(jnp.int32, (GLA_H * q, GLA_QK), 1) // GLA_DK
    k_bd = jnp.where(rb == cb, jnp.concatenate([kg] * GLA_H, axis=0), jnp.zeros((), BF16))
    att = lax.dot_general(qg, k_bd, (((1,), (1,)), ((), ())), preferred_element_type=F32)
    i_i = lax.broadcasted_iota(jnp.int32, (q, GLA_H * q), 0)
    j_i = lax.broadcasted_iota(jnp.int32, (q, GLA_H * q), 1) % q
    keep = (j_i >= i_i) if reverse else (j_i <= i_i)
    att = jnp.where(keep, att, 0.0).astype(BF16)
    rv = lax.broadcasted_iota(jnp.int32, (GLA_H * q, GLA_V), 0) // q
    cv = lax.broadcasted_iota(jnp.int32, (GLA_H * q, GLA_V), 1) // GLA_DV
    v_bd = jnp.where(rv == cv, jnp.concatenate([v] * GLA_H, axis=0), jnp.zeros((), BF16))
    st = st_ref[...]
    o_in = jnp.dot(att, v_bd, preferred_element_type=F32)
    o_st = lax.dot_general(qg, st.astype(BF16), (((1,), (1,)), ((), ())), preferred_element_type=F32)
    o_ref[0] = o_in + o_st
    upd = lax.dot_general(v, kw, (((0,), (0,)), ((), ())), preferred_element_type=F32)
    rs = lax.broadcasted_iota(jnp.int32, (GLA_V, GLA_QK), 0) // GLA_DV
    cs = lax.broadcasted_iota(jnp.int32, (GLA_V, GLA_QK), 1) // GLA_DK
    st_ref[...] = st * jnp.exp(total) + jnp.where(rs == cs, upd, 0.0)


def _gla_body(qf_ref, lf_ref, qb_ref, lb_ref, gw_ref, gb_ref, of_ref, ob_ref, stf, stb):
    @pl.when(pl.program_id(1) == 0)
    def _():
        stf[...] = jnp.zeros_like(stf)
        stb[...] = jnp.zeros_like(stb)

    _gla_dir(qf_ref, lf_ref, gw_ref, gb_ref, stf, of_ref, d=0, reverse=False)
    _gla_dir(qb_ref, lb_ref, gw_ref, gb_ref, stb, ob_ref, d=1, reverse=True)


def gla_scan(qkv, lr, gate_w, gate_b):
    bsz = qkv.shape[0]
    nc = NTOK // GLA_CHUNK
    nlat = SEQ // GLA_CHUNK
    fwd_chunk = lambda s: (s + nlat) % nc
    bwd_chunk = lambda s: nc - 1 - s
    q_spec = lambda cm: pl.BlockSpec((1, GLA_CHUNK, qkv.shape[-1]), lambda b, s: (b, cm(s), 0))
    l_spec = lambda cm: pl.BlockSpec((1, GLA_CHUNK, 2 * GLA_RANK), lambda b, s: (b, cm(s), 0))
    o_spec = lambda cm: pl.BlockSpec((1, GLA_CHUNK, GLA_V), lambda b, s: (b, cm(s), 0))
    o_shape = jax.ShapeDtypeStruct((bsz, NTOK, GLA_V), F32)
    return pl.pallas_call(
        _gla_body,
        name='gla_scan',
        grid=(bsz, nc),
        in_specs=[q_spec(fwd_chunk), l_spec(fwd_chunk), q_spec(bwd_chunk), l_spec(bwd_chunk),
                  pl.BlockSpec((2, GLA_RANK, GLA_QK), lambda b, s: (0, 0, 0)),
                  pl.BlockSpec((2, GLA_QK), lambda b, s: (0, 0))],
        out_specs=[o_spec(fwd_chunk), o_spec(bwd_chunk)],
        out_shape=[o_shape, o_shape],
        scratch_shapes=[pltpu.VMEM((GLA_V, GLA_QK), F32), pltpu.VMEM((GLA_V, GLA_QK), F32)],
        compiler_params=_cparams(("parallel", "arbitrary")),
    )(qkv, lr, qkv, lr, gate_w.astype(F32), gate_b.astype(F32))


def _gla_merge_body(of_ref, ob_ref, r_ref, nw_ref, o_ref):
    o = of_ref[0] + ob_ref[0]
    r = r_ref[0].astype(F32)
    for h in range(GLA_H):
        sl = slice(h * GLA_DV, (h + 1) * GLA_DV)
        oh = o[:, sl]
        ms = jnp.mean(oh * oh, axis=-1, keepdims=True)
        o_ref[0, :, sl] = (oh * lax.rsqrt(ms + NORM_EPS) * nw_ref[:, sl] * _silu(r[:, sl])).astype(o_ref.dtype)


def gla_merge_stream(of, ob, r, norm_w):
    bsz = of.shape[0]
    tile = pl.BlockSpec((1, ROW_TILE, GLA_V), lambda i, j: (i, j, 0))
    return pl.pallas_call(
        _gla_merge_body,
        name='gla_merge',
        grid=(bsz, N_LAT_TILES),
        in_specs=[tile, tile, tile, pl.BlockSpec((1, GLA_V), lambda i, j: (0, 0))],
        out_specs=tile,
        out_shape=jax.ShapeDtypeStruct((bsz, SEQ, GLA_V), BF16),
        compiler_params=_cparams(("parallel", "parallel")),
    )(of, ob, r, norm_w.reshape(1, GLA_V))


RG_TILE = 8


def _gelu_tanh(x):
    return 0.5 * x * (1.0 + jnp.tanh(math.sqrt(2.0 / math.pi) * (x + 0.044715 * x * x * x)))


def _rg_scan_block(a_s, x_s, h_s, base, carry, reverse):
    n_tiles = ROW_TILE // RG_TILE
    row = lax.broadcasted_iota(jnp.int32, (RG_TILE, D_RG), 0)

    def tile_step(i, h_prev):
        t = (n_tiles - 1 - i) if reverse else i
        r0 = pl.multiple_of(t * RG_TILE, RG_TILE)
        a = a_s[pl.ds(r0, RG_TILE), :]
        x = x_s[pl.ds(r0, RG_TILE), :]
        for s in (1, 2, 4):
            if reverse:
                ok = row < RG_TILE - s
                shift = RG_TILE - s
            else:
                ok = row >= s
                shift = s
            a_sh = jnp.where(ok, pltpu.roll(a, shift, 0), 1.0)
            x_sh = jnp.where(ok, pltpu.roll(x, shift, 0), 0.0)
            x = a * x_sh + x
            a = a * a_sh
        h = x + a * h_prev
        h_s[pl.ds(base + r0, RG_TILE), :] = h
        edge = 0 if reverse else RG_TILE - 1
        return jnp.broadcast_to(h[edge:edge + 1, :], (RG_TILE, D_RG))

    return lax.fori_loop(0, n_tiles, tile_step, carry)


def _rglru_body(u_ref, g_ref, w_ref, b_ref, c_ref, o_ref, hf_s, a_s, x_s, hb_s):
    n_blocks = NTOK // ROW_TILE
    fwd_order = list(range(N_LAT_TILES, n_blocks)) + list(range(N_LAT_TILES))
    bwd_order = list(range(n_blocks - 1, N_LAT_TILES - 1, -1)) + list(range(N_LAT_TILES - 1, -1, -1))

    def gates(blk, d):
        ub = u_ref[0, blk * ROW_TILE:(blk + 1) * ROW_TILE, :]
        z = jnp.dot(ub, w_ref[:, 2 * d * D_RG:2 * (d + 1) * D_RG], preferred_element_type=F32)
        z = z + b_ref[:, 2 * d * D_RG:2 * (d + 1) * D_RG]
        r = 1.0 / (1.0 + jnp.exp(-z[:, :D_RG]))
        i = 1.0 / (1.0 + jnp.exp(-z[:, D_RG:]))
        a = jnp.exp(c_ref[d:d + 1, :] * r)
        a_s[...] = a
        x_s[...] = jnp.sqrt(1.0 - a * a) * i * ub.astype(F32)

    carry = jnp.zeros((RG_TILE, D_RG), F32)
    for blk in fwd_order:
        gates(blk, 0)
        carry = _rg_scan_block(a_s, x_s, hf_s, blk * ROW_TILE, carry, reverse=False)
    carry = jnp.zeros((RG_TILE, D_RG), F32)
    for blk in bwd_order:
        gates(blk, 1)
        carry = _rg_scan_block(a_s, x_s, hb_s, 0, carry, reverse=True)
        rows = slice(blk * ROW_TILE, (blk + 1) * ROW_TILE)
        gate = g_ref[0, rows, :].astype(F32)
        o_ref[0, rows, :] = ((hf_s[rows, :] + hb_s[...]) * _gelu_tanh(gate)).astype(o_ref.dtype)


def rglru_stream(u, gate, w_a, b_a, w_x, b_x, lam):
    bsz = u.shape[0]
    eye = jnp.eye(RG_BLOCKS, dtype=F32)
    dense = lambda w: jnp.einsum('nio,nm->nimo', w, eye).reshape(D_RG, D_RG)
    w_cat = jnp.concatenate([dense(w_a[0]), dense(w_x[0]), dense(w_a[1]), dense(w_x[1])], axis=1).astype(BF16)
    b_cat = jnp.concatenate([b_a[0], b_x[0], b_a[1], b_x[1]]).astype(F32).reshape(1, 4 * D_RG)
    c = -RG_C * jax.nn.softplus(-lam.astype(F32))
    seq = pl.BlockSpec((1, NTOK, D_RG), lambda i: (i, 0, 0))
    return pl.pallas_call(
        _rglru_body,
        name='rglru',
        grid=(bsz,),
        in_specs=[seq, seq,
                  pl.BlockSpec((D_RG, 4 * D_RG), lambda i: (0, 0)),
                  pl.BlockSpec((1, 4 * D_RG), lambda i: (0, 0)),
                  pl.BlockSpec((2, D_RG), lambda i: (0, 0))],
        out_specs=seq,
        out_shape=jax.ShapeDtypeStruct((bsz, NTOK, D_RG), BF16),
        scratch_shapes=[pltpu.VMEM((NTOK, D_RG), F32), pltpu.VMEM((ROW_TILE, D_RG), F32),
                        pltpu.VMEM((ROW_TILE, D_RG), F32), pltpu.VMEM((ROW_TILE, D_RG), F32)],
        compiler_params=_cparams(("parallel",)),
    )(u, gate, w_cat, b_cat, c)


def odd_layer_mixer_pallas(xs, mods, norm_w, w_in, w_out, gla_args, rg_args):
    bsz = xs.shape[0]
    gate_w, gate_b, gla_norm_w = gla_args
    rg_conv_w, rg_conv_b, w_a, b_a, w_x, b_x, lam = rg_args
    h = norm_mod(xs, norm_w, mods, 0, 1)
    h = jnp.concatenate([to_col_major(h[:, :SEQ]), h[:, SEQ:]], axis=1)
    nqk, nv = GLA_QK, GLA_V
    r0 = 2 * nqk + nv + 2 * GLA_RANK
    w = jnp.concatenate([w_in[:, :2 * nqk + nv], w_in[:, r0:r0 + nv], w_in[:, GLA_IN:],
                         w_in[:, 2 * nqk + nv:r0]], axis=1).astype(BF16)
    qkv_w = 2 * nqk + nv
    splits = ((0, qkv_w), (qkv_w, nv), (qkv_w + nv, D_RG), (qkv_w + nv + D_RG, D_RG),
              (qkv_w + nv + 2 * D_RG, 2 * GLA_RANK))
    qkv, r, u_raw, gate, lr = mm_split(h.reshape(bsz * NTOK, D_MODEL), w, splits, (BF16, BF16, BF16, BF16, F32))
    to3 = lambda a: a.reshape(bsz, NTOK, a.shape[-1])
    of, ob = gla_scan(to3(qkv), to3(lr), gate_w, gate_b.reshape(2, GLA_QK))
    a_l = gla_merge_stream(of, ob, to3(r), gla_norm_w)
    u = dwconv_stream(to3(u_raw), rg_conv_w, rg_conv_b, act=False)
    r_l = rglru_stream(u, to3(gate), w_a, b_a, w_x, b_x, lam)[:, :SEQ]
    wo = w_out.astype(BF16)
    return mm_resid([from_col_major(a_l), from_col_major(r_l)], [wo[:GLA_V], wo[GLA_V:]], xs, mods, 2, N_LAT_TILES)


def rms_norm(x, w):
    xf = x.astype(F32)
    y = xf * lax.rsqrt(jnp.mean(jnp.square(xf), axis=-1, keepdims=True) + NORM_EPS)
    return y.astype(x.dtype) * w


def dwconv(x, w, b):
    y = lax.conv_general_dilated(x, w[:, None, :].astype(x.dtype), window_strides=(1,), padding='SAME',
                                 dimension_numbers=('NWC', 'WIO', 'NWC'), feature_group_count=x.shape[-1])
    return y + b.astype(x.dtype)


def maybe_flip(a, rev):
    return jnp.flip(a, axis=1) if rev else a


def to_col_major(x):
    b, n, d = x.shape
    rows = n // GRID_W
    return x.reshape(b, rows, GRID_W, d).transpose(0, 2, 1, 3).reshape(b, n, d)


def from_col_major(x):
    b, n, d = x.shape
    rows = n // GRID_W
    return x.reshape(b, GRID_W, rows, d).transpose(0, 2, 1, 3).reshape(b, n, d)


def gla_inputs(p, gate_w, gate_b):
    b, n, _ = p.shape
    nqk, nv = GLA_H * GLA_DK, GLA_H * GLA_DV
    q = p[..., :nqk].reshape(b, n, GLA_H, GLA_DK) * GLA_DK ** -0.5
    k = p[..., nqk:2 * nqk].reshape(b, n, GLA_H, GLA_DK)
    v = p[..., 2 * nqk:2 * nqk + nv].reshape(b, n, GLA_H, GLA_DV)
    lr = p[..., 2 * nqk + nv:2 * nqk + nv + 2 * GLA_RANK].reshape(b, n, 2, GLA_RANK)
    r = p[..., 2 * nqk + nv + 2 * GLA_RANK:]
    logit = jnp.einsum('bler,erk->blek', lr, gate_w) + gate_b
    log_g = (jax.nn.log_sigmoid(logit.astype(F32)) / GLA_GATE_NORM).reshape(b, n, 2, GLA_H, GLA_DK)
    return q, k, v, log_g, r


def gla_states(k, v, log_g, s0):
    b, n = k.shape[:2]
    nc = n // GLA_CHUNK
    kc = k.reshape(b, nc, GLA_CHUNK, GLA_H, GLA_DK)
    vc = v.reshape(b, nc, GLA_CHUNK, GLA_H, GLA_DV)
    gcum = jnp.cumsum(log_g.reshape(b, nc, GLA_CHUNK, GLA_H, GLA_DK), axis=2)
    states = jnp.einsum('bcqhd,bcqhv->bchdv', kc * jnp.exp(gcum[:, :, -1:] - gcum), vc)
    chunk_decay = jnp.exp(gcum[:, :, -1])

    def step(s, inp):
        st, dcy = inp
        return dcy[..., None] * s + st, s

    s_fin, s_prev = lax.scan(step, s0, (jnp.moveaxis(states, 1, 0), jnp.moveaxis(chunk_decay, 1, 0)))
    return jnp.moveaxis(s_prev, 0, 1), s_fin


def gla_output(q, k, v, log_g, s_prev):
    b, n = q.shape[:2]
    nc = n // GLA_CHUNK
    qc = q.reshape(b, nc, GLA_CHUNK, GLA_H, GLA_DK)
    kc = k.reshape(b, nc, GLA_CHUNK, GLA_H, GLA_DK)
    vc = v.reshape(b, nc, GLA_CHUNK, GLA_H, GLA_DV)
    gcum = jnp.cumsum(log_g.reshape(b, nc, GLA_CHUNK, GLA_H, GLA_DK), axis=2)
    qg = qc * jnp.exp(gcum)
    kg = kc * jnp.exp(-gcum)
    mask = jnp.tril(jnp.ones((GLA_CHUNK, GLA_CHUNK), bool))
    att = jnp.where(mask, jnp.einsum('bcihd,bcjhd->bchij', qg, kg), 0.0)
    o = jnp.einsum('bchij,bcjhv->bcihv', att, vc) + jnp.einsum('bcihd,bchdv->bcihv', qg, s_prev)
    return o.reshape(b, n, GLA_H, GLA_DV)


def gla_merge(os_, r, norm_w):
    b, n = r.shape[:2]
    o = rms_norm(os_[0] + os_[1], norm_w.reshape(GLA_H, GLA_DV))
    return o.reshape(b, n, GLA_H * GLA_DV) * jax.nn.silu(r)


def gla_mixer(p_c, p_l, gate_w, gate_b, norm_w):
    q_c, k_c, v_c, g_c, r_c = gla_inputs(p_c, gate_w, gate_b)
    q_l, k_l, v_l, g_l, r_l = gla_inputs(p_l, gate_w, gate_b)
    s0 = jnp.zeros((p_c.shape[0], GLA_H, GLA_DK, GLA_DV), F32)
    os_l = []
    for d, rev in enumerate((False, True)):
        f = functools.partial(maybe_flip, rev=rev)
        _, sf_c = gla_states(f(k_c), f(v_c), f(g_c[:, :, d]), s0)
        sp_l, _ = gla_states(f(k_l), f(v_l), f(g_l[:, :, d]), sf_c)
        os_l.append(f(gla_output(f(q_l), f(k_l), f(v_l), f(g_l[:, :, d]), sp_l)))
    return gla_merge(os_l, r_l, norm_w)


def rglru_inputs(p, conv_w, conv_b, w_a, b_a, w_x, b_x, lam):
    b, n, _ = p.shape
    u = dwconv(p[..., :D_RG], conv_w, conv_b)
    ub = u.reshape(b, n, RG_BLOCKS, RG_BW)
    r = jax.nn.sigmoid((jnp.einsum('blni,enio->bleno', ub, w_a).reshape(b, n, 2, D_RG) + b_a).astype(F32))
    i = jax.nn.sigmoid((jnp.einsum('blni,enio->bleno', ub, w_x).reshape(b, n, 2, D_RG) + b_x).astype(F32))
    log_a = -RG_C * jax.nn.softplus(-lam.astype(F32)) * r
    x_in = jnp.sqrt(-jnp.expm1(2.0 * log_a)) * i * u[:, :, None, :].astype(F32)
    return p[..., D_RG:], jnp.exp(log_a), x_in


def lru_scan(a, u, h0):
    u = u.at[:, 0].add(a[:, 0] * h0)

    def combine(lhs, rhs):
        a1, b1 = lhs
        a2, b2 = rhs
        return a1 * a2, a2 * b1 + b2

    return lax.associative_scan(combine, (a, u), axis=1)[1]


def rglru_mixer(p_c, p_l, conv_w, conv_b, w_a, b_a, w_x, b_x, lam):
    gb_c, a_c, u_c = rglru_inputs(p_c, conv_w, conv_b, w_a, b_a, w_x, b_x, lam)
    gb_l, a_l, u_l = rglru_inputs(p_l, conv_w, conv_b, w_a, b_a, w_x, b_x, lam)
    h0 = jnp.zeros((p_c.shape[0], D_RG), F32)
    hs_l = []
    for d, rev in enumerate((False, True)):
        f = functools.partial(maybe_flip, rev=rev)
        h_c = f(lru_scan(f(a_c[:, :, d]), f(u_c[:, :, d]), h0))
        h_end = h_c[:, 0] if rev else h_c[:, -1]
        hs_l.append(f(lru_scan(f(a_l[:, :, d]), f(u_l[:, :, d]), h_end)))
    return (hs_l[0] + hs_l[1]) * jax.nn.gelu(gb_l.astype(F32))


def odd_layer_mixer(xs, mods, norm_w, w_in, w_out, gla_args, rg_args):
    bsz = xs.shape[0]
    h = norm_mod(xs, norm_w, mods, 0, 1)
    h = jnp.concatenate([to_col_major(h[:, :SEQ]), h[:, SEQ:]], axis=1)
    p = pmatmul(h.reshape(bsz * NTOK, D_MODEL), w_in.astype(BF16)).reshape(bsz, NTOK, OD_IN)
    p_l, p_c = p[:, :SEQ], p[:, SEQ:]
    a_l = gla_mixer(p_c[..., :GLA_IN], p_l[..., :GLA_IN], *gla_args)
    r_l = rglru_mixer(p_c[..., GLA_IN:], p_l[..., GLA_IN:], *rg_args)
    mix = from_col_major(jnp.concatenate([a_l, r_l], axis=-1)).astype(BF16)
    return mm_resid([mix], [w_out.astype(BF16)], xs, mods, 2, N_LAT_TILES)


def kernel(x, c, ctx, c_ctx, ada_w, ada_b, norm1_w, norm2_w, ev_w_in, ev_w_out, hy_conv_w, hy_conv_b, hy_fw0, hy_fb0, hy_fw1, hy_fb1, hy_fw2, hy_fb2, hy_fw3, hy_freq, hy_bias, ssd_conv_w, ssd_conv_b, ssd_dt_bias, ssd_a_log, ssd_d, ssd_norm_w, od_w_in, od_w_out, gla_gate_w, gla_gate_b, gla_norm_w, rg_conv_w, rg_conv_b, rg_w_a, rg_b_a, rg_w_x, rg_b_x, rg_lambda, router_w, router_b, moe_w_gate, moe_w_up, moe_w_down, sh_w_gate, sh_w_up, sh_w_down, final_norm_w):
    xs = jnp.concatenate([x, ctx], axis=1)
    for i in range(DEPTH):
        last = i == DEPTH - 1
        j = i // 2
        mods = adaln_table(c, c_ctx, ada_w[i], ada_b[i])
        if i % 2 == 0:
            hy_filt = (hy_fw0[j], hy_fb0[j], hy_fw1[j], hy_fb1[j], hy_fw2[j], hy_fb2[j], hy_fw3[j], hy_freq[j])
            xs = even_layer_mixer(xs, mods, norm1_w[i], ev_w_in[j], ev_w_out[j], hy_conv_w[j], hy_conv_b[j],
                                  hy_filt, hy_bias[j], ssd_conv_w[j], ssd_conv_b[j], ssd_dt_bias[j],
                                  ssd_a_log[j], ssd_d[j], ssd_norm_w[j])
        else:
            gla_args = (gla_gate_w[j], gla_gate_b[j], gla_norm_w[j])
            rg_args = (rg_conv_w[j], rg_conv_b[j], rg_w_a[j], rg_b_a[j], rg_w_x[j], rg_b_x[j], rg_lambda[j])
            xs = odd_layer_mixer_pallas(xs, mods, norm1_w[i], od_w_in[j], od_w_out[j], gla_args, rg_args)
        n_tiles = N_LAT_TILES if last else N_ROW_TILES
        xs = moe_layer(xs, norm2_w[i], mods, router_w[i], router_b[i], moe_w_gate, moe_w_up, moe_w_down,
                       sh_w_gate[i], sh_w_up[i], sh_w_down[i], n_tiles, i)
    return final_norm(xs, final_norm_w)
```

```python
import functools
import math

import jax
import jax.numpy as jnp
from jax import lax
from jax.experimental import pallas as pl
from jax.experimental.pallas import tpu as pltpu

D_MODEL = 1024
BATCH = 16
SEQ = 2048
DEPTH = 2

CTX_LEN = 256
GRID_W = 64
NORM_EPS = 1e-6

D_HY = D_MODEL // 2
HY_ORDER = 2
HY_SHORT = 3
HY_BANDS = 16
HY_EMB = 1 + 2 * HY_BANDS
HY_FF = 64
HY_DECAY_PCT_LO = 0.3
HY_DECAY_PCT_HI = 1.5
HY_DECAY_TARGET = 1e-2
HY_IN = 3 * D_HY

D_SSM = D_MODEL // 2
SSD_P = 64
SSD_H = D_SSM // SSD_P
SSD_G = 2
SSD_HPG = SSD_H // SSD_G
SSD_N = 128
SSD_CONV = 4
SSD_CHUNK = 128
SSD_XBC = D_SSM + 2 * SSD_G * SSD_N
SSD_IN = D_SSM + SSD_XBC + 2 * SSD_H
EV_IN = HY_IN + SSD_IN
EV_MIX = D_HY + D_SSM

GLA_H = 4
GLA_DV = (D_MODEL // 2) // GLA_H
GLA_DK = GLA_DV // 2
GLA_RANK = 16
GLA_GATE_NORM = 16.0
GLA_CHUNK = 64
GLA_IN = 2 * GLA_H * GLA_DK + 2 * GLA_H * GLA_DV + 2 * GLA_RANK

D_RG = D_MODEL // 2
RG_BLOCKS = 8
RG_BW = D_RG // RG_BLOCKS
RG_CONV = 4
RG_C = 8.0
RG_IN = 2 * D_RG
OD_IN = GLA_IN + RG_IN
OD_MIX = GLA_H * GLA_DV + D_RG

MOE_EXPERTS = 64
MOE_TOPK = 8
MOE_D_EXPERT = 256
MOE_D_SHARED = 256
MOE_SCALE = 2.5
MOE_BLOCK = 256

F32 = jnp.float32
BF16 = jnp.bfloat16
HIGHEST = lax.Precision.HIGHEST

NTOK = SEQ + CTX_LEN
ROW_TILE = 256
N_ROW_TILES = NTOK // ROW_TILE
N_LAT_TILES = SEQ // ROW_TILE

VMEM_LIMIT = 48 * 1024 * 1024
VMEM_LIMIT_BIG = 56 * 1024 * 1024


def _cparams(sem, limit=VMEM_LIMIT):
    return pltpu.CompilerParams(dimension_semantics=sem, vmem_limit_bytes=limit)


def _pick_tile(n, pref):
    t = min(n, pref)
    while n % t:
        t //= 2
    return t


def _silu(x):
    return x / (1.0 + jnp.exp(-x))


def _softplus(x):
    return jnp.maximum(x, 0.0) + jnp.log(1.0 + jnp.exp(-jnp.abs(x)))


def _mm_bf16_body(a_ref, w_ref, o_ref):
    o_ref[...] = jnp.dot(a_ref[...].astype(BF16), w_ref[...].astype(BF16),
                         preferred_element_type=F32).astype(o_ref.dtype)


def _mm_f32_body(a_ref, w_ref, o_ref):
    o_ref[...] = jnp.dot(a_ref[...], w_ref[...], preferred_element_type=F32,
                         precision=HIGHEST).astype(o_ref.dtype)


def pmatmul(a, w, *, exact=False, out_dtype=F32, tm=512, tn=None):
    m, k = a.shape
    n = w.shape[1]
    tm = _pick_tile(m, tm)
    tn = n if tn is None else _pick_tile(n, tn)
    body = _mm_f32_body if exact else _mm_bf16_body
    return pl.pallas_call(
        body,
        name='mm',
        grid=(m // tm, n // tn),
        in_specs=[pl.BlockSpec((tm, k), lambda i, j: (i, 0)),
                  pl.BlockSpec((k, tn), lambda i, j: (0, j))],
        out_specs=pl.BlockSpec((tm, tn), lambda i, j: (i, j)),
        out_shape=jax.ShapeDtypeStruct((m, n), out_dtype),
        compiler_params=_cparams(("parallel", "parallel")),
    )(a, w)


def _mm_split_body(a_ref, w_ref, *o_refs, splits):
    a = a_ref[...]
    for o_ref, (start, width) in zip(o_refs, splits):
        o_ref[...] = jnp.dot(a, w_ref[:, start:start + width],
                             preferred_element_type=F32).astype(o_ref.dtype)


def mm_split(a, w, splits, dtypes, tm=512):
    m, k = a.shape
    n = w.shape[1]
    tm = _pick_tile(m, tm)
    return pl.pallas_call(
        functools.partial(_mm_split_body, splits=tuple(splits)),
        name='mm_split',
        grid=(m // tm,),
        in_specs=[pl.BlockSpec((tm, k), lambda i: (i, 0)),
                  pl.BlockSpec((k, n), lambda i: (0, 0))],
        out_specs=[pl.BlockSpec((tm, wd), lambda i: (i, 0)) for _, wd in splits],
        out_shape=[jax.ShapeDtypeStruct((m, wd), dt) for (_, wd), dt in zip(splits, dtypes)],
        compiler_params=_cparams(("parallel",)),
    )(a, w)


def _mm_resid_body(*refs, n_pairs):
    a_refs = refs[:n_pairs]
    w_refs = refs[n_pairs:2 * n_pairs]
    x_ref, g_ref, o_ref = refs[2 * n_pairs:]
    acc = jnp.dot(a_refs[0][0], w_refs[0][...], preferred_element_type=F32)
    for a_ref, w_ref in zip(a_refs[1:], w_refs[1:]):
        acc = acc + jnp.dot(a_ref[0], w_ref[...], preferred_element_type=F32)
    o_ref[0] = x_ref[0] + g_ref[...] * acc


def mm_resid(a_list, w_list, xs, mods, gate_idx, n_tiles):
    b, nt, d = xs.shape
    n_pairs = len(a_list)
    in_specs = [pl.BlockSpec((1, ROW_TILE, a.shape[-1]), lambda i, j: (i, j, 0)) for a in a_list]
    in_specs += [pl.BlockSpec(w.shape, lambda i, j: (0, 0)) for w in w_list]
    in_specs += [pl.BlockSpec((1, ROW_TILE, d), lambda i, j: (i, j, 0)),
                 _mod_spec(gate_idx, d)]
    return pl.pallas_call(
        functools.partial(_mm_resid_body, n_pairs=n_pairs),
        name='mm_resid',
        grid=(b, n_tiles),
        in_specs=in_specs,
        out_specs=pl.BlockSpec((1, ROW_TILE, d), lambda i, j: (i, j, 0)),
        out_shape=jax.ShapeDtypeStruct(xs.shape, F32),
        input_output_aliases={2 * n_pairs: 0},
        compiler_params=_cparams(("parallel", "parallel")),
    )(*a_list, *w_list, xs, mods)


def _mod_spec(idx, d, b0=0):
    return pl.BlockSpec((None, None, None, 1, d), lambda i, j: (i + b0, 1 - j // N_LAT_TILES, idx, 0, 0))


def adaln_table(c, c_ctx, w, b):
    cv = jax.nn.silu(jnp.concatenate([c, c_ctx[None, :]], axis=0))
    cv = jnp.pad(cv, ((0, 24 - cv.shape[0]), (0, 0)))
    m = pmatmul(cv, w, exact=True, tn=1536)[:BATCH + 1] + b
    per_sample = m[:BATCH]
    ctx_row = jnp.broadcast_to(m[BATCH][None, :], per_sample.shape)
    return jnp.stack([ctx_row, per_sample], axis=1).reshape(BATCH, 2, 6, 1, D_MODEL)


def _norm_mod(x, w, shift, scale):
    ms = jnp.mean(x * x, axis=-1, keepdims=True)
    return (x * lax.rsqrt(ms + NORM_EPS) * w) * (1.0 + scale) + shift


def _norm_mod_body(x_ref, w_ref, sh_ref, sc_ref, o_ref):
    o_ref[0] = _norm_mod(x_ref[0], w_ref[...], sh_ref[...], sc_ref[...]).astype(o_ref.dtype)


def norm_mod(xs, w, mods, shift_idx, scale_idx):
    b, nt, d = xs.shape
    return pl.pallas_call(
        _norm_mod_body,
        name='norm_mod',
        grid=(b, nt // ROW_TILE),
        in_specs=[pl.BlockSpec((1, ROW_TILE, d), lambda i, j: (i, j, 0)),
                  pl.BlockSpec((1, d), lambda i, j: (0, 0)),
                  _mod_spec(shift_idx, d), _mod_spec(scale_idx, d)],
        out_specs=pl.BlockSpec((1, ROW_TILE, d), lambda i, j: (i, j, 0)),
        out_shape=jax.ShapeDtypeStruct(xs.shape, BF16),
        compiler_params=_cparams(("parallel", "parallel")),
    )(xs, w.reshape(1, d), mods, mods)


def _final_norm_body(x_ref, w_ref, o_ref):
    x = x_ref[0]
    ms = jnp.mean(x * x, axis=-1, keepdims=True)
    o_ref[0] = x * lax.rsqrt(ms + NORM_EPS) * w_ref[...]


def final_norm(xs, w):
    b, _, d = xs.shape
    return pl.pallas_call(
        _final_norm_body,
        name='final_norm',
        grid=(b, N_LAT_TILES),
        in_specs=[pl.BlockSpec((1, ROW_TILE, d), lambda i, j: (i, j, 0)),
                  pl.BlockSpec((1, d), lambda i, j: (0, 0))],
        out_specs=pl.BlockSpec((1, ROW_TILE, d), lambda i, j: (i, j, 0)),
        out_shape=jax.ShapeDtypeStruct((b, SEQ, d), F32),
        compiler_params=_cparams(("parallel", "parallel")),
    )(xs, w.reshape(1, d))


def _route_body(x_ref, w_ref, sh_ref, sc_ref, rw_ref, rb_ref, h_ref, idx_ref, wsel_ref, rank_ref, cnt_ref,
                *, group_size):
    first = (pl.program_id(0) % group_size == 0) & (pl.program_id(1) == 0)

    @pl.when(first)
    def _():
        cnt_ref[...] = jnp.zeros_like(cnt_ref)

    h = _norm_mod(x_ref[0], w_ref[...], sh_ref[...], sc_ref[...])
    h_ref[0] = h.astype(h_ref.dtype)
    logits = jnp.dot(h, rw_ref[...], preferred_element_type=F32, precision=HIGHEST)
    scores = 1.0 / (1.0 + jnp.exp(-logits))
    tm, ne = scores.shape
    lane = lax.broadcasted_iota(jnp.int32, (tm, ne), 1).astype(F32)
    slot = lax.broadcasted_iota(jnp.int32, (tm, MOE_TOPK), 1)
    sel = scores + rb_ref[...]
    picked = jnp.zeros((tm, ne), F32)
    hits = []
    idx_out = jnp.zeros((tm, MOE_TOPK), F32)
    w_out = jnp.zeros((tm, MOE_TOPK), F32)
    for k in range(MOE_TOPK):
        m = jnp.max(sel, axis=-1, keepdims=True)
        ik = jnp.min(jnp.where(sel == m, lane, float(ne)), axis=-1, keepdims=True)
        hit = lane == ik
        wk = jnp.sum(jnp.where(hit, scores, 0.0), axis=-1, keepdims=True)
        sel = jnp.where(hit, -jnp.inf, sel)
        picked = picked + hit.astype(F32)
        hits.append(hit)
        idx_out = jnp.where(slot == k, ik, idx_out)
        w_out = jnp.where(slot == k, wk, w_out)
    wsum = jnp.sum(w_out, axis=-1, keepdims=True)
    wsel_ref[0] = w_out / wsum * MOE_SCALE
    idx_ref[0] = idx_out.astype(jnp.int32)
    r_i = lax.broadcasted_iota(jnp.int32, (tm, tm), 0)
    c_i = lax.broadcasted_iota(jnp.int32, (tm, tm), 1)
    strict_lower = (c_i < r_i).astype(BF16)
    before = jnp.dot(strict_lower, picked.astype(BF16), preferred_element_type=F32) + cnt_ref[...]
    rank_out = jnp.zeros((tm, MOE_TOPK), F32)
    for k in range(MOE_TOPK):
        rk = jnp.sum(jnp.where(hits[k], before, 0.0), axis=-1, keepdims=True)
        rank_out = jnp.where(slot == k, rk, rank_out)
    rank_ref[0] = rank_out.astype(jnp.int32)
    cnt_ref[...] = cnt_ref[...] + jnp.sum(picked, axis=0, keepdims=True)


def route(xs, w, mods, router_w, router_b, n_tiles, group_size):
    b, _, d = xs.shape
    rows = n_tiles * ROW_TILE
    small = lambda dt: jax.ShapeDtypeStruct((b, rows, MOE_TOPK), dt)
    small_spec = pl.BlockSpec((1, ROW_TILE, MOE_TOPK), lambda i, j: (i, j, 0))
    return pl.pallas_call(
        functools.partial(_route_body, group_size=group_size),
        name='route',
        grid=(b, n_tiles),
        in_specs=[pl.BlockSpec((1, ROW_TILE, d), lambda i, j: (i, j, 0)),
                  pl.BlockSpec((1, d), lambda i, j: (0, 0)),
                  _mod_spec(3, d), _mod_spec(4, d),
                  pl.BlockSpec((d, MOE_EXPERTS), lambda i, j: (0, 0)),
                  pl.BlockSpec((1, MOE_EXPERTS), lambda i, j: (0, 0))],
        out_specs=[pl.BlockSpec((1, ROW_TILE, d), lambda i, j: (i, j, 0)),
                   small_spec, small_spec, small_spec,
                   pl.BlockSpec((None, 1, MOE_EXPERTS), lambda i, j: (i // group_size, 0, 0))],
        out_shape=[jax.ShapeDtypeStruct((b, rows, d), BF16), small(jnp.int32), small(F32), small(jnp.int32),
                   jax.ShapeDtypeStruct((b // group_size, 1, MOE_EXPERTS), F32)],
        compiler_params=_cparams(("arbitrary", "arbitrary")),
    )(xs, w.reshape(1, d), mods, mods, router_w, router_b.reshape(1, MOE_EXPERTS))


def _swiglu(x, wg, wu, wd):
    g = jnp.dot(x, wg, preferred_element_type=F32)
    u = jnp.dot(x, wu, preferred_element_type=F32)
    h = (_silu(g) * u).astype(BF16)
    return jnp.dot(h, wd, preferred_element_type=F32)


def _expert_body(be_ref, nu_ref, x_ref, wg_ref, wu_ref, wd_ref, o_ref, wg_s, wu_s, wd_s):
    i = pl.program_id(0)
    used = i < nu_ref[0]

    @pl.when(used & ((i == 0) | (be_ref[i] != be_ref[jnp.maximum(i - 1, 0)])))
    def _():
        wg_s[...] = wg_ref[0].astype(BF16)
        wu_s[...] = wu_ref[0].astype(BF16)
        wd_s[...] = wd_ref[0].astype(BF16)

    @pl.when(used)
    def _():
        o_ref[...] = _swiglu(x_ref[...], wg_s[...], wu_s[...], wd_s[...]).astype(o_ref.dtype)

    @pl.when(jnp.logical_not(used))
    def _():
        o_ref[...] = jnp.zeros_like(o_ref)


def moe_experts(x_rows, block_e, n_used, wg, wu, wd, layer):
    rows, d = x_rows.shape
    n_blocks = rows // MOE_BLOCK
    f = wg.shape[-1]
    grid_spec = pltpu.PrefetchScalarGridSpec(
        num_scalar_prefetch=2,
        grid=(n_blocks,),
        in_specs=[
            pl.BlockSpec((MOE_BLOCK, d), lambda i, be, nu: (i, 0)),
            pl.BlockSpec((None, 1, d, f), lambda i, be, nu: (layer, be[i], 0, 0)),
            pl.BlockSpec((None, 1, d, f), lambda i, be, nu: (layer, be[i], 0, 0)),
            pl.BlockSpec((None, 1, f, d), lambda i, be, nu: (layer, be[i], 0, 0)),
        ],
        out_specs=pl.BlockSpec((MOE_BLOCK, d), lambda i, be, nu: (i, 0)),
        scratch_shapes=[pltpu.VMEM((d, f), BF16), pltpu.VMEM((d, f), BF16), pltpu.VMEM((f, d), BF16)],
    )
    return pl.pallas_call(
        _expert_body,
        name='experts',
        grid_spec=grid_spec,
        out_shape=jax.ShapeDtypeStruct((rows, d), BF16),
        compiler_params=_cparams(("arbitrary",)),
    )(block_e, n_used, x_rows, wg, wu, wd)


def _shared_resid_body(h_ref, wg_ref, wu_ref, wd_ref, pk_ref, ws_ref, x_ref, g_ref, o_ref):
    y = _swiglu(h_ref[0], wg_ref[...], wu_ref[...], wd_ref[...])
    ws = ws_ref[0]
    for k in range(MOE_TOPK):
        y = y + ws[:, k:k + 1] * pk_ref[k, 0].astype(F32)
    o_ref[0] = x_ref[0] + g_ref[...] * y


def shared_resid(h, picked, wsel, xs, mods, wg, wu, wd, n_tiles, b0):
    b, _, d = h.shape
    f = wg.shape[-1]
    tile = pl.BlockSpec((1, ROW_TILE, d), lambda i, j: (i, j, 0))
    xs_tile = pl.BlockSpec((1, ROW_TILE, d), lambda i, j: (i + b0, j, 0))
    return pl.pallas_call(
        _shared_resid_body,
        name='shared_resid',
        grid=(b, n_tiles),
        in_specs=[tile,
                  pl.BlockSpec((d, f), lambda i, j: (0, 0)),
                  pl.BlockSpec((d, f), lambda i, j: (0, 0)),
                  pl.BlockSpec((f, d), lambda i, j: (0, 0)),
                  pl.BlockSpec((MOE_TOPK, 1, ROW_TILE, d), lambda i, j: (0, i, j, 0)),
                  pl.BlockSpec((1, ROW_TILE, MOE_TOPK), lambda i, j: (i, j, 0)),
                  xs_tile, _mod_spec(5, d, b0)],
        out_specs=xs_tile,
        out_shape=jax.ShapeDtypeStruct(xs.shape, F32),
        input_output_aliases={6: 0},
        compiler_params=_cparams(("parallel", "parallel")),
    )(h, wg, wu, wd, picked, wsel, xs, mods)


MOE_GROUPS = 2


def moe_layer(xs, norm_w, mods, router_w, router_b, w_gate, w_up, w_down, sh_gate, sh_up, sh_down, n_tiles, layer):
    bsz, _, d = xs.shape
    shared_w = (sh_gate.astype(BF16), sh_up.astype(BF16), sh_down.astype(BF16))
    b = bsz // MOE_GROUPS
    h, idx, wsel, rank, counts = route(xs, norm_w, mods, router_w, router_b, n_tiles, b)
    h_flat = h.reshape(-1, d)
    for g in range(MOE_GROUPS):
        sl = slice(g * b, (g + 1) * b)
        xs = _moe_group(xs, mods, h_flat, h[sl], idx[sl], wsel[sl], rank[sl], counts[g, 0], w_gate, w_up, w_down,
                        shared_w, n_tiles, layer, g * b)
    return xs


def _moe_group(xs, mods, h_flat, h, idx, wsel, rank, counts, w_gate, w_up, w_down, shared_w, n_tiles, layer, b0):
    b, rows_per_sample, d = h.shape
    n = b * rows_per_sample
    counts = counts.astype(jnp.int32)
    padded = (counts + MOE_BLOCK - 1) // MOE_BLOCK * MOE_BLOCK
    ends = jnp.cumsum(padded)
    starts = ends - padded
    nk = n * MOE_TOPK
    n_blocks = -(-nk // MOE_BLOCK) + MOE_EXPERTS
    rows = n_blocks * MOE_BLOCK
    n_pad = rows - nk
    e_iota = jnp.arange(MOE_EXPERTS, dtype=jnp.int32)
    dest = jnp.sum(jnp.where(idx[..., None] == e_iota, starts, 0), axis=-1) + rank
    blk_start = jnp.arange(n_blocks, dtype=jnp.int32) * MOE_BLOCK
    block_e = jnp.minimum(jnp.sum(ends[None, :] <= blk_start[:, None], axis=1), MOE_EXPERTS - 1).astype(jnp.int32)
    n_used = (ends[-1:] // MOE_BLOCK).astype(jnp.int32)
    pad = padded - counts
    cum_pad = jnp.cumsum(pad)
    m = jnp.arange(n_pad, dtype=jnp.int32)
    e_m = jnp.sum(cum_pad[None, :] <= m[:, None], axis=1)
    base = jnp.sum(jnp.where(jnp.minimum(e_m, MOE_EXPERTS - 1)[:, None] == e_iota,
                             starts + counts - (cum_pad - pad), 0), axis=1)
    pad_row = jnp.where(e_m < MOE_EXPERTS, base + m, ends[-1] + m - cum_pad[-1])
    tok0 = b0 * rows_per_sample
    tok = tok0 + jnp.arange(nk, dtype=jnp.int32) // MOE_TOPK
    _, row_tok = lax.sort((jnp.concatenate([dest.reshape(-1), pad_row]).astype(jnp.int32),
                           jnp.concatenate([tok, tok0 + m % n])), num_keys=1)
    x_rows = h_flat[row_tok]
    y_rows = moe_experts(x_rows, block_e, n_used, w_gate, w_up, w_down, layer)
    picked = y_rows[dest.reshape(n, MOE_TOPK).T].reshape(MOE_TOPK, b, rows_per_sample, d)
    return shared_resid(h, picked, wsel, xs, mods, *shared_w, n_tiles, b0)


def _dwconv_body(x_ref, w_ref, b_ref, o_ref, *, width, act):
    chunk = ROW_TILE
    n_chunks = NTOK // chunk
    first_of_seq = (0, N_LAT_TILES)
    last_of_seq = (N_LAT_TILES - 1, n_chunks - 1)
    tc = x_ref.shape[-1]
    halo = 16
    row = lax.broadcasted_iota(jnp.int32, (chunk, tc), 0)
    zero_row = jnp.zeros((1, tc), F32)
    for c in range(n_chunks):
        r0 = c * chunk
        cur = x_ref[0, r0:r0 + chunk, :].astype(F32)
        if c in first_of_seq:
            prev_last = zero_row
        else:
            prev_last = x_ref[0, r0 - halo:r0, :].astype(F32)[halo - 1:halo, :]
        if c in last_of_seq:
            next0 = next1 = zero_row
        else:
            nxt = x_ref[0, r0 + chunk:r0 + chunk + halo, :].astype(F32)
            next0, next1 = nxt[0:1, :], nxt[1:2, :]
        xm1 = jnp.where(row == 0, prev_last, pltpu.roll(cur, 1, 0))
        xp1 = jnp.where(row == chunk - 1, next0, pltpu.roll(cur, chunk - 1, 0))
        y = w_ref[0:1, :] * xm1 + w_ref[1:2, :] * cur + w_ref[2:3, :] * xp1 + b_ref[...]
        if width == 4:
            xp2 = jnp.where(row == chunk - 2, next0,
                            jnp.where(row == chunk - 1, next1, pltpu.roll(cur, chunk - 2, 0)))
            y = y + w_ref[3:4, :] * xp2
        if act:
            y = _silu(y)
        o_ref[0, c * chunk:(c + 1) * chunk, :] = y.astype(o_ref.dtype)


def dwconv_stream(x, w, b, act, tc=256):
    bsz, nt, c = x.shape
    width = w.shape[0]
    return pl.pallas_call(
        functools.partial(_dwconv_body, width=width, act=act),
        name='dwconv',
        grid=(bsz, c // tc),
        in_specs=[pl.BlockSpec((1, nt, tc), lambda i, j: (i, 0, j)),
                  pl.BlockSpec((width, tc), lambda i, j: (0, j)),
                  pl.BlockSpec((1, tc), lambda i, j: (0, j))],
        out_specs=pl.BlockSpec((1, nt, tc), lambda i, j: (i, 0, j)),
        out_shape=jax.ShapeDtypeStruct(x.shape, BF16),
        compiler_params=_cparams(("parallel", "parallel")),
    )(x, w, b.reshape(1, c))


HY_FB = 512


def dft_matrices(n):
    k = jnp.arange(n, dtype=jnp.int32)[:, None]
    t = jnp.arange(n, dtype=jnp.int32)[None, :]
    ang = (2.0 * math.pi / (2 * n)) * ((k * t) % (2 * n)).astype(F32)
    fre = jnp.cos(ang)
    fim = -jnp.sin(ang)
    nyq = jnp.where(t % 2 == 0, 1.0, -1.0).astype(F32)
    fim = jnp.where(k == 0, nyq, fim)
    fwd = jnp.concatenate([fre, fim], axis=0)
    colscale = jnp.where(jnp.arange(2 * n) % n == 0, 0.5, 1.0) / n
    inv = fwd.T * colscale[None, :]
    return fwd, inv


def hyena_filter_taps(n, fw0, fb0, fw1, fb1, fw2, fb2, fw3, freq):
    pos = jnp.arange(n, dtype=F32)
    t = pos / max(n - 1, 1)
    bands = jnp.linspace(1e-4, HY_BANDS - 1, HY_BANDS, dtype=F32)
    ang = (2.0 * math.pi / n) * pos[:, None] * bands[None, :]
    feats = jnp.concatenate([t[:, None], jnp.cos(ang), -jnp.sin(ang)], axis=-1)
    h = jnp.sin(freq * (jnp.dot(feats, fw0, precision=HIGHEST) + fb0))
    h = jnp.sin(freq * (jnp.dot(h, fw1, precision=HIGHEST) + fb1))
    h = jnp.sin(freq * (jnp.dot(h, fw2, precision=HIGHEST) + fb2))
    h = pmatmul(h, fw3, exact=True).reshape(n, 2, HY_ORDER, D_HY)
    deltas = jnp.abs(jnp.linspace(math.log(HY_DECAY_PCT_LO) / HY_DECAY_TARGET,
                                  math.log(HY_DECAY_PCT_HI) / HY_DECAY_TARGET, D_HY, dtype=F32))
    h = h * jnp.exp(-t[:, None] * deltas)[:, None, None, :]
    h0 = h[:, 0]
    h1 = h[:, 1].at[0].set(0.0)
    norm = jnp.sum(jnp.abs(h0), axis=0, keepdims=True) + jnp.sum(jnp.abs(h1), axis=0, keepdims=True)
    h0 = (h0 / norm).reshape(n, HY_ORDER * D_HY)
    h1 = (h1 / norm).reshape(n, HY_ORDER * D_HY)
    return h0 + h1, h0 - h1


def _split_bf16(a):
    hi = a.astype(BF16)
    return hi, (a - hi.astype(F32)).astype(BF16)


def hyena_spectrum(fwd, hsum, hdiff, fb):
    n = hsum.shape[0]
    f_hi, f_lo = _split_bf16(fwd)

    def dft(h):
        h_hi, h_lo = _split_bf16(h)
        return pmatmul(f_hi, h_hi) + pmatmul(f_hi, h_lo) + pmatmul(f_lo, h_hi)

    a = dft(hsum)
    bm = dft(hdiff)
    sr = a[:n]
    si = bm[n:]
    nyq = a[n]
    first = (jnp.arange(n) == 0)[:, None]
    p = sr
    q = jnp.where(first, 0.0, si)
    s = jnp.where(first, nyq[None, :], sr)
    spec = jnp.stack([p, q, s], axis=0).reshape(3, n // fb, fb, HY_ORDER, D_HY)
    return spec.transpose(3, 1, 0, 2, 4), f_hi


def _hyena_body(u_ref, fre_ref, fim_ref, gre_ref, gim_ref, sp_ref, bias_ref, prev_ref, o_ref,
                vin, acc, *, nf):
    del prev_ref
    o = pl.program_id(1)
    f = pl.program_id(2)
    c = D_HY

    @pl.when((o == 0) & (f == 0))
    def _():
        vin[...] = u_ref[0, :, 0:c]

    @pl.when(f == 0)
    def _():
        acc[...] = jnp.zeros_like(acc)

    v = vin[...]
    vr = jnp.dot(fre_ref[...], v, preferred_element_type=F32)
    vi = jnp.dot(fim_ref[...], v, preferred_element_type=F32)
    p, q, s = sp_ref[0], sp_ref[1], sp_ref[2]
    zr = (vr * p - vi * q).astype(BF16)
    zi = (vr * q + vi * s).astype(BF16)
    acc[...] += (jnp.dot(gre_ref[...], zr, preferred_element_type=F32)
                 + jnp.dot(gim_ref[...], zi, preferred_element_type=F32))

    @pl.when((o == 0) & (f == nf - 1))
    def _():
        z = u_ref[0, :, c:2 * c].astype(F32) * (acc[...] + bias_ref[0:1, :] * vin[...].astype(F32))
        vin[...] = z.astype(BF16)

    @pl.when((o == 1) & (f == nf - 1))
    def _():
        y = u_ref[0, :, 2 * c:3 * c].astype(F32) * (acc[...] + bias_ref[1:2, :] * vin[...].astype(F32))
        o_ref[0] = y.astype(o_ref.dtype)


def hyena_long_conv(u, fwd_bf16, inv_bf16, spec, bias, n, row_block, prev_out):
    bsz = u.shape[0]
    fb = spec.shape[3]
    nf = n // fb
    out_shape = jax.ShapeDtypeStruct((bsz, NTOK, D_HY), BF16)
    if prev_out is None:
        prev_out = jnp.zeros(out_shape.shape, BF16)
    args = [u, fwd_bf16, fwd_bf16, inv_bf16, inv_bf16, spec, bias, prev_out]
    aliases = {7: 0}
    return pl.pallas_call(
        functools.partial(_hyena_body, nf=nf),
        name='hyena',
        grid=(bsz, HY_ORDER, nf),
        in_specs=[pl.BlockSpec((1, n, 3 * D_HY), lambda b, o, f: (b, row_block, 0)),
                  pl.BlockSpec((fb, n), lambda b, o, f: (f, 0)),
                  pl.BlockSpec((fb, n), lambda b, o, f: (nf + f, 0)),
                  pl.BlockSpec((n, fb), lambda b, o, f: (0, f)),
                  pl.BlockSpec((n, fb), lambda b, o, f: (0, nf + f)),
                  pl.BlockSpec((None, None, 3, fb, D_HY), lambda b, o, f: (o, f, 0, 0, 0)),
                  pl.BlockSpec((HY_ORDER, D_HY), lambda b, o, f: (0, 0)),
                  pl.BlockSpec(memory_space=pl.ANY)],
        out_specs=pl.BlockSpec((1, n, D_HY), lambda b, o, f: (b, row_block, 0)),
        out_shape=out_shape,
        scratch_shapes=[pltpu.VMEM((n, D_HY), BF16), pltpu.VMEM((n, D_HY), F32)],
        input_output_aliases=aliases,
        compiler_params=_cparams(("parallel", "arbitrary", "arbitrary"), VMEM_LIMIT_BIG),
    )(*args)


def hyena_mixer_stream(p_hy, conv_w, conv_b, filt, bias):
    u = dwconv_stream(p_hy, conv_w, conv_b, act=False)
    out = None
    for n, row_block in ((SEQ, 0), (CTX_LEN, SEQ // CTX_LEN)):
        fb = min(HY_FB, n)
        fwd, inv = dft_matrices(n)
        hsum, hdiff = hyena_filter_taps(n, *filt)
        spec, fwd_bf16 = hyena_spectrum(fwd, hsum, hdiff, fb)
        out = hyena_long_conv(u, fwd_bf16, inv.astype(BF16), spec, bias, n, row_block, out)
    return out


def _tri(n, kind):
    r = lax.broadcasted_iota(jnp.int32, (n, n), 0)
    c = lax.broadcasted_iota(jnp.int32, (n, n), 1)
    return (c <= r) if kind == 'lower' else (c >= r)


def _ssd_dir(xbc_ref, dt_ref, dtt_ref, bias_r, bias_c, a_r, a_c, st_ref, y_ref, *, d, reverse):
    q = SSD_CHUNK
    nh = SSD_H
    gw = SSD_HPG * SSD_P
    lower = _tri(q, 'lower')
    upper = _tri(q, 'upper')
    lower_f = lower.astype(F32)
    upper_f = upper.astype(F32)
    dt_col = _softplus(dt_ref[0] + bias_r)
    dt_row = _softplus(dtt_ref[0] + bias_c)
    da_col = dt_col * a_r
    da_row = dt_row * a_c
    if not reverse:
        acs_col = jnp.dot(lower_f, da_col, preferred_element_type=F32, precision=HIGHEST)
        acs_row = jnp.dot(da_row, upper_f, preferred_element_type=F32, precision=HIGHEST)
        mask = lower
        edge = q - 1
    else:
        acs_col = jnp.dot(upper_f, da_col, preferred_element_type=F32, precision=HIGHEST)
        acs_row = jnp.dot(da_row, lower_f, preferred_element_type=F32, precision=HIGHEST)
        mask = upper
        edge = 0
    h0 = d * nh
    hh = lax.broadcasted_iota(jnp.int32, (2 * nh, nh * SSD_P), 0)
    cc = lax.broadcasted_iota(jnp.int32, (2 * nh, nh * SSD_P), 1) // SSD_P
    expand = (hh == cc + h0).astype(F32)
    acs_c = jnp.dot(acs_col, expand, preferred_element_type=F32, precision=HIGHEST)
    dt_c = jnp.dot(dt_col, expand, preferred_element_type=F32, precision=HIGHEST)
    total_c = acs_c[edge:edge + 1, :]
    e_in_c = jnp.exp(acs_c)
    w_end_c = jnp.exp(total_c - acs_c) * dt_c
    dec_c = jnp.exp(total_c)
    xs = xbc_ref[0, :, 0:D_SSM]
    xs_f = xs.astype(F32)
    for g in range(SSD_G):
        bm = xbc_ref[0, :, D_SSM + g * SSD_N:D_SSM + (g + 1) * SSD_N]
        cm = xbc_ref[0, :, D_SSM + SSD_G * SSD_N + g * SSD_N:D_SSM + SSD_G * SSD_N + (g + 1) * SSD_N]
        cb = lax.dot_general(cm, bm, (((1,), (1,)), ((), ())), preferred_element_type=F32)
        lws = []
        for k in range(SSD_HPG):
            h = h0 + g * SSD_HPG + k
            seg = acs_col[:, h:h + 1] - acs_row[h:h + 1, :]
            decay = jnp.exp(jnp.where(mask, seg, -jnp.inf))
            lws.append((cb * decay * dt_row[h:h + 1, :]).astype(BF16))
        lw = jnp.concatenate(lws, axis=1)
        xg = xs[:, g * gw:(g + 1) * gw]
        rb = lax.broadcasted_iota(jnp.int32, (SSD_HPG * q, gw), 0) // q
        cbk = lax.broadcasted_iota(jnp.int32, (SSD_HPG * q, gw), 1) // SSD_P
        x_bd = jnp.where(rb == cbk, jnp.concatenate([xg] * SSD_HPG, axis=0), jnp.zeros((), BF16))
        y_in = jnp.dot(lw, x_bd, preferred_element_type=F32)
        st = st_ref[g]
        y_st = jnp.dot(cm, st.astype(BF16), preferred_element_type=F32) * e_in_c[:, g * gw:(g + 1) * gw]
        y_ref[0, :, g * gw:(g + 1) * gw] = y_in + y_st
        xw = (xs_f[:, g * gw:(g + 1) * gw] * w_end_c[:, g * gw:(g + 1) * gw]).astype(BF16)
        upd = lax.dot_general(bm, xw, (((0,), (0,)), ((), ())), preferred_element_type=F32)
        st_ref[g] = st * dec_c[:, g * gw:(g + 1) * gw] + upd


def _ssd_body(xf_ref, dtf_ref, dttf_ref, xb_ref, dtb_ref, dttb_ref, bias_r, bias_c, a_r, a_c,
              yf_ref, yb_ref, stf, stb):
    @pl.when(pl.program_id(1) == 0)
    def _():
        stf[...] = jnp.zeros_like(stf)
        stb[...] = jnp.zeros_like(stb)

    _ssd_dir(xf_ref, dtf_ref, dttf_ref, bias_r[...], bias_c[...], a_r[...], a_c[...], stf, yf_ref,
             d=0, reverse=False)
    _ssd_dir(xb_ref, dtb_ref, dttb_ref, bias_r[...], bias_c[...], a_r[...], a_c[...], stb, yb_ref,
             d=1, reverse=True)


def ssd_scan(xbc, dt, dt_bias, a_log):
    bsz = xbc.shape[0]
    nc = NTOK // SSD_CHUNK
    nlat = SEQ // SSD_CHUNK
    dtt = jnp.swapaxes(dt, 1, 2)
    fwd_chunk = lambda s: (s + nlat) % nc
    bwd_chunk = lambda s: nc - 1 - s
    a = -jnp.exp(a_log.astype(F32)).reshape(1, 2 * SSD_H)
    bias = dt_bias.astype(F32).reshape(1, 2 * SSD_H)
    x_spec = lambda cm: pl.BlockSpec((1, SSD_CHUNK, SSD_XBC), lambda b, s: (b, cm(s), 0))
    dt_spec = lambda cm: pl.BlockSpec((1, SSD_CHUNK, 2 * SSD_H), lambda b, s: (b, cm(s), 0))
    dtt_spec = lambda cm: pl.BlockSpec((1, 2 * SSD_H, SSD_CHUNK), lambda b, s: (b, 0, cm(s)))
    y_spec = lambda cm: pl.BlockSpec((1, SSD_CHUNK, D_SSM), lambda b, s: (b, cm(s), 0))
    row = pl.BlockSpec((1, 2 * SSD_H), lambda b, s: (0, 0))
    col = pl.BlockSpec((2 * SSD_H, 1), lambda b, s: (0, 0))
    y_shape = jax.ShapeDtypeStruct((bsz, NTOK, D_SSM), F32)
    gw = SSD_HPG * SSD_P
    return pl.pallas_call(
        _ssd_body,
        name='ssd_scan',
        grid=(bsz, nc),
        in_specs=[x_spec(fwd_chunk), dt_spec(fwd_chunk), dtt_spec(fwd_chunk),
                  x_spec(bwd_chunk), dt_spec(bwd_chunk), dtt_spec(bwd_chunk),
                  row, col, row, col],
        out_specs=[y_spec(fwd_chunk), y_spec(bwd_chunk)],
        out_shape=[y_shape, y_shape],
        scratch_shapes=[pltpu.VMEM((SSD_G, SSD_N, gw), F32), pltpu.VMEM((SSD_G, SSD_N, gw), F32)],
        compiler_params=_cparams(("parallel", "arbitrary")),
    )(xbc, dt, dtt, xbc, dt, dtt, bias, bias.reshape(-1, 1), a, a.reshape(-1, 1))


def _ssd_merge_body(yf_ref, yb_ref, xbc_ref, z_ref, d_ref, nw_ref, o_ref):
    xs = xbc_ref[0, :, 0:D_SSM].astype(F32)
    z = z_ref[0].astype(F32)
    g = (yf_ref[0] + yb_ref[0] + d_ref[...] * xs) * _silu(z)
    gw = D_SSM // SSD_G
    for k in range(SSD_G):
        gk = g[:, k * gw:(k + 1) * gw]
        ms = jnp.mean(gk * gk, axis=-1, keepdims=True)
        o_ref[0, :, k * gw:(k + 1) * gw] = (gk * lax.rsqrt(ms + NORM_EPS)
                                            * nw_ref[:, k * gw:(k + 1) * gw]).astype(o_ref.dtype)


def ssd_merge(yf, yb, xbc, z, d_skip, norm_w):
    bsz = yf.shape[0]
    tile = lambda w: pl.BlockSpec((1, ROW_TILE, w), lambda i, j: (i, j, 0))
    vec = pl.BlockSpec((1, D_SSM), lambda i, j: (0, 0))
    d_chan = jnp.repeat(d_skip.astype(F32), SSD_P).reshape(1, D_SSM)
    return pl.pallas_call(
        _ssd_merge_body,
        name='ssd_merge',
        grid=(bsz, N_ROW_TILES),
        in_specs=[tile(D_SSM), tile(D_SSM), tile(SSD_XBC), tile(D_SSM), vec, vec],
        out_specs=tile(D_SSM),
        out_shape=jax.ShapeDtypeStruct((bsz, NTOK, D_SSM), BF16),
        compiler_params=_cparams(("parallel", "parallel")),
    )(yf, yb, xbc, z, d_chan, norm_w.reshape(1, D_SSM))


def even_layer_mixer(xs, mods, norm_w, w_in, w_out, hy_conv_w, hy_conv_b, hy_filt, hy_bias,
                     ssd_conv_w, ssd_conv_b, ssd_dt_bias, ssd_a_log, ssd_d, ssd_norm_w):
    bsz = xs.shape[0]
    h = norm_mod(xs, norm_w, mods, 0, 1)
    splits = ((0, HY_IN), (HY_IN, D_SSM), (HY_IN + D_SSM, SSD_XBC), (HY_IN + D_SSM + SSD_XBC, 2 * SSD_H))
    p_hy, z, xbc_raw, dt = mm_split(h.reshape(bsz * NTOK, D_MODEL), w_in.astype(BF16), splits,
                                    (BF16, BF16, BF16, F32))
    to3 = lambda a: a.reshape(bsz, NTOK, a.shape[-1])
    y_hy = hyena_mixer_stream(to3(p_hy), hy_conv_w, hy_conv_b, hy_filt, hy_bias)
    xbc = dwconv_stream(to3(xbc_raw), ssd_conv_w, ssd_conv_b, act=True)
    yf, yb = ssd_scan(xbc, to3(dt), ssd_dt_bias, ssd_a_log)
    s = ssd_merge(yf, yb, xbc, to3(z), ssd_d, ssd_norm_w)
    wo = w_out.astype(BF16)
    return mm_resid([y_hy, s], [wo[:D_HY], wo[D_HY:]], xs, mods, 2, N_ROW_TILES)


GLA_QK = GLA_H * GLA_DK
GLA_V = GLA_H * GLA_DV


def _gla_dir(qkv_ref, lr_ref, gw_ref, gb_ref, st_ref, o_ref, *, d, reverse):
    q = GLA_CHUNK
    tri = _tri(q, 'upper' if reverse else 'lower')
    edge = 0 if reverse else q - 1
    lr = lr_ref[0, :, d * GLA_RANK:(d + 1) * GLA_RANK]
    logit = jnp.dot(lr, gw_ref[d], preferred_element_type=F32, precision=HIGHEST) + gb_ref[d:d + 1, :]
    log_g = -_softplus(-logit) * (1.0 / GLA_GATE_NORM)
    gcum = jnp.dot(tri.astype(F32), log_g, preferred_element_type=F32, precision=HIGHEST)
    total = gcum[edge:edge + 1, :]
    qf = qkv_ref[0, :, 0:GLA_QK].astype(F32)
    kf = qkv_ref[0, :, GLA_QK:2 * GLA_QK].astype(F32)
    v = qkv_ref[0, :, 2 * GLA_QK:2 * GLA_QK + GLA_V]
    qg = (qf * (GLA_DK ** -0.5) * jnp.exp(gcum)).astype(BF16)
    kg = (kf * jnp.exp(-gcum)).astype(BF16)
    kw = (kf * jnp.exp(total - gcum)).astype(BF16)
    rb = lax.broadcasted_iota(jnp.int32, (GLA_H * q, GLA_QK), 0) // q
    cb = lax.broadcasted_iota(jnp.int32, (GLA_H * q, GLA_QK), 1) // GLA_DK
    k_bd = jnp.where(rb == cb, jnp.concatenate([kg] * GLA_H, axis=0), jnp.zeros((), BF16))
    att = lax.dot_general(qg, k_bd, (((1,), (1,)), ((), ())), preferred_element_type=F32)
    i_i = lax.broadcasted_iota(jnp.int32, (q, GLA_H * q), 0)
    j_i = lax.broadcasted_iota(jnp.int32, (q, GLA_H * q), 1) % q
    keep = (j_i >= i_i) if reverse else (j_i <= i_i)
    att = jnp.where(keep, att, 0.0).astype(BF16)
    rv = lax.broadcasted_iota(jnp.int32, (GLA_H * q, GLA_V), 0) // q
    cv = lax.broadcasted_iota(jnp.int32, (GLA_H * q, GLA_V), 1) // GLA_DV
    v_bd = jnp.where(rv == cv, jnp.concatenate([v] * GLA_H, axis=0), jnp.zeros((), BF16))
    st = st_ref[...]
    o_in = jnp.dot(att, v_bd, preferred_element_type=F32)
    o_st = lax.dot_general(qg, st.astype(BF16), (((1,), (1,)), ((), ())), preferred_element_type=F32)
    o_ref[0] = o_in + o_st
    upd = lax.dot_general(v, kw, (((0,), (0,)), ((), ())), preferred_element_type=F32)
    rs = lax.broadcasted_iota(jnp.int32, (GLA_V, GLA_QK), 0) // GLA_DV
    cs = lax.broadcasted_iota(jnp.int32, (GLA_V, GLA_QK), 1) // GLA_DK
    st_ref[...] = st * jnp.exp(total) + jnp.where(rs == cs, upd, 0.0)


def _gla_body(qf_ref, lf_ref, qb_ref, lb_ref, gw_ref, gb_ref, of_ref, ob_ref, stf, stb):
    @pl.when(pl.program_id(1) == 0)
    def _():
        stf[...] = jnp.zeros_like(stf)
        stb[...] = jnp.zeros_like(stb)

    _gla_dir(qf_ref, lf_ref, gw_ref, gb_ref, stf, of_ref, d=0, reverse=False)
    _gla_dir(qb_ref, lb_ref, gw_ref, gb_ref, stb, ob_ref, d=1, reverse=True)


def gla_scan(qkv, lr, gate_w, gate_b):
    bsz = qkv.shape[0]
    nc = NTOK // GLA_CHUNK
    nlat = SEQ // GLA_CHUNK
    fwd_chunk = lambda s: (s + nlat) % nc
    bwd_chunk = lambda s: nc - 1 - s
    q_spec = lambda cm: pl.BlockSpec((1, GLA_CHUNK, qkv.shape[-1]), lambda b, s: (b, cm(s), 0))
    l_spec = lambda cm: pl.BlockSpec((1, GLA_CHUNK, 2 * GLA_RANK), lambda b, s: (b, cm(s), 0))
    o_spec = lambda cm: pl.BlockSpec((1, GLA_CHUNK, GLA_V), lambda b, s: (b, cm(s), 0))
    o_shape = jax.ShapeDtypeStruct((bsz, NTOK, GLA_V), F32)
    return pl.pallas_call(
        _gla_body,
        name='gla_scan',
        grid=(bsz, nc),
        in_specs=[q_spec(fwd_chunk), l_spec(fwd_chunk), q_spec(bwd_chunk), l_spec(bwd_chunk),
                  pl.BlockSpec((2, GLA_RANK, GLA_QK), lambda b, s: (0, 0, 0)),
                  pl.BlockSpec((2, GLA_QK), lambda b, s: (0, 0))],
        out_specs=[o_spec(fwd_chunk), o_spec(bwd_chunk)],
        out_shape=[o_shape, o_shape],
        scratch_shapes=[pltpu.VMEM((GLA_V, GLA_QK), F32), pltpu.VMEM((GLA_V, GLA_QK), F32)],
        compiler_params=_cparams(("parallel", "arbitrary")),
    )(qkv, lr, qkv, lr, gate_w.astype(F32), gate_b.astype(F32))


def _gla_merge_body(of_ref, ob_ref, r_ref, nw_ref, o_ref):
    o = of_ref[0] + ob_ref[0]
    r = r_ref[0].astype(F32)
    for h in range(GLA_H):
        sl = slice(h * GLA_DV, (h + 1) * GLA_DV)
        oh = o[:, sl]
        ms = jnp.mean(oh * oh, axis=-1, keepdims=True)
        o_ref[0, :, sl] = (oh * lax.rsqrt(ms + NORM_EPS) * nw_ref[:, sl] * _silu(r[:, sl])).astype(o_ref.dtype)


def gla_merge_stream(of, ob, r, norm_w):
    bsz = of.shape[0]
    tile = pl.BlockSpec((1, ROW_TILE, GLA_V), lambda i, j: (i, j, 0))
    return pl.pallas_call(
        _gla_merge_body,
        name='gla_merge',
        grid=(bsz, N_LAT_TILES),
        in_specs=[tile, tile, tile, pl.BlockSpec((1, GLA_V), lambda i, j: (0, 0))],
        out_specs=tile,
        out_shape=jax.ShapeDtypeStruct((bsz, SEQ, GLA_V), BF16),
        compiler_params=_cparams(("parallel", "parallel")),
    )(of, ob, r, norm_w.reshape(1, GLA_V))


RG_TILE = 8


def _gelu_tanh(x):
    return 0.5 * x * (1.0 + jnp.tanh(math.sqrt(2.0 / math.pi) * (x + 0.044715 * x * x * x)))


def _rg_scan_block(a_s, x_s, h_s, base, carry, reverse):
    n_tiles = ROW_TILE // RG_TILE
    row = lax.broadcasted_iota(jnp.int32, (RG_TILE, D_RG), 0)

    def tile_step(i, h_prev):
        t = (n_tiles - 1 - i) if reverse else i
        r0 = pl.multiple_of(t * RG_TILE, RG_TILE)
        a = a_s[pl.ds(r0, RG_TILE), :]
        x = x_s[pl.ds(r0, RG_TILE), :]
        for s in (1, 2, 4):
            if reverse:
                ok = row < RG_TILE - s
                shift = RG_TILE - s
            else:
                ok = row >= s
                shift = s
            a_sh = jnp.where(ok, pltpu.roll(a, shift, 0), 1.0)
            x_sh = jnp.where(ok, pltpu.roll(x, shift, 0), 0.0)
            x = a * x_sh + x
            a = a * a_sh
        h = x + a * h_prev
        h_s[pl.ds(base + r0, RG_TILE), :] = h
        edge = 0 if reverse else RG_TILE - 1
        return jnp.broadcast_to(h[edge:edge + 1, :], (RG_TILE, D_RG))

    return lax.fori_loop(0, n_tiles, tile_step, carry)


def _rglru_body(u_ref, g_ref, w_ref, b_ref, c_ref, o_ref, hf_s, a_s, x_s, hb_s):
    n_blocks = NTOK // ROW_TILE
    fwd_order = list(range(N_LAT_TILES, n_blocks)) + list(range(N_LAT_TILES))
    bwd_order = list(range(n_blocks - 1, N_LAT_TILES - 1, -1)) + list(range(N_LAT_TILES - 1, -1, -1))

    def gates(blk, d):
        ub = u_ref[0, blk * ROW_TILE:(blk + 1) * ROW_TILE, :]
        z = jnp.dot(ub, w_ref[:, 2 * d * D_RG:2 * (d + 1) * D_RG], preferred_element_type=F32)
        z = z + b_ref[:, 2 * d * D_RG:2 * (d + 1) * D_RG]
        r = 1.0 / (1.0 + jnp.exp(-z[:, :D_RG]))
        i = 1.0 / (1.0 + jnp.exp(-z[:, D_RG:]))
        a = jnp.exp(c_ref[d:d + 1, :] * r)
        a_s[...] = a
        x_s[...] = jnp.sqrt(1.0 - a * a) * i * ub.astype(F32)

    carry = jnp.zeros((RG_TILE, D_RG), F32)
    for blk in fwd_order:
        gates(blk, 0)
        carry = _rg_scan_block(a_s, x_s, hf_s, blk * ROW_TILE, carry, reverse=False)
    carry = jnp.zeros((RG_TILE, D_RG), F32)
    for blk in bwd_order:
        gates(blk, 1)
        carry = _rg_scan_block(a_s, x_s, hb_s, 0, carry, reverse=True)
        rows = slice(blk * ROW_TILE, (blk + 1) * ROW_TILE)
        gate = g_ref[0, rows, :].astype(F32)
        o_ref[0, rows, :] = ((hf_s[rows, :] + hb_s[...]) * _gelu_tanh(gate)).astype(o_ref.dtype)


def rglru_stream(u, gate, w_a, b_a, w_x, b_x, lam):
    bsz = u.shape[0]
    eye = jnp.eye(RG_BLOCKS, dtype=F32)
    dense = lambda w: jnp.einsum('nio,nm->nimo', w, eye).reshape(D_RG, D_RG)
    w_cat = jnp.concatenate([dense(w_a[0]), dense(w_x[0]), dense(w_a[1]), dense(w_x[1])], axis=1).astype(BF16)
    b_cat = jnp.concatenate([b_a[0], b_x[0], b_a[1], b_x[1]]).astype(F32).reshape(1, 4 * D_RG)
    c = -RG_C * jax.nn.softplus(-lam.astype(F32))
    seq = pl.BlockSpec((1, NTOK, D_RG), lambda i: (i, 0, 0))
    return pl.pallas_call(
        _rglru_body,
        name='rglru',
        grid=(bsz,),
        in_specs=[seq, seq,
                  pl.BlockSpec((D_RG, 4 * D_RG), lambda i: (0, 0)),
                  pl.BlockSpec((1, 4 * D_RG), lambda i: (0, 0)),
                  pl.BlockSpec((2, D_RG), lambda i: (0, 0))],
        out_specs=seq,
        out_shape=jax.ShapeDtypeStruct((bsz, NTOK, D_RG), BF16),
        scratch_shapes=[pltpu.VMEM((NTOK, D_RG), F32), pltpu.VMEM((ROW_TILE, D_RG), F32),
                        pltpu.VMEM((ROW_TILE, D_RG), F32), pltpu.VMEM((ROW_TILE, D_RG), F32)],
        compiler_params=_cparams(("parallel",)),
    )(u, gate, w_cat, b_cat, c)


def odd_layer_mixer_pallas(xs, mods, norm_w, w_in, w_out, gla_args, rg_args):
    bsz = xs.shape[0]
    gate_w, gate_b, gla_norm_w = gla_args
    rg_conv_w, rg_conv_b, w_a, b_a, w_x, b_x, lam = rg_args
    h = norm_mod(xs, norm_w, mods, 0, 1)
    h = jnp.concatenate([to_col_major(h[:, :SEQ]), h[:, SEQ:]], axis=1)
    nqk, nv = GLA_QK, GLA_V
    r0 = 2 * nqk + nv + 2 * GLA_RANK
    w = jnp.concatenate([w_in[:, :2 * nqk + nv], w_in[:, r0:r0 + nv], w_in[:, GLA_IN:],
                         w_in[:, 2 * nqk + nv:r0]], axis=1).astype(BF16)
    qkv_w = 2 * nqk + nv
    splits = ((0, qkv_w), (qkv_w, nv), (qkv_w + nv, D_RG), (qkv_w + nv + D_RG, D_RG),
              (qkv_w + nv + 2 * D_RG, 2 * GLA_RANK))
    qkv, r, u_raw, gate, lr = mm_split(h.reshape(bsz * NTOK, D_MODEL), w, splits, (BF16, BF16, BF16, BF16, F32))
    to3 = lambda a: a.reshape(bsz, NTOK, a.shape[-1])
    of, ob = gla_scan(to3(qkv), to3(lr), gate_w, gate_b.reshape(2, GLA_QK))
    a_l = gla_merge_stream(of, ob, to3(r), gla_norm_w)
    u = dwconv_stream(to3(u_raw), rg_conv_w, rg_conv_b, act=False)
    r_l = rglru_stream(u, to3(gate), w_a, b_a, w_x, b_x, lam)[:, :SEQ]
    wo = w_out.astype(BF16)
    return mm_resid([from_col_major(a_l), from_col_major(r_l)], [wo[:GLA_V], wo[GLA_V:]], xs, mods, 2, N_LAT_TILES)


def rms_norm(x, w):
    xf = x.astype(F32)
    y = xf * lax.rsqrt(jnp.mean(jnp.square(xf), axis=-1, keepdims=True) + NORM_EPS)
    return y.astype(x.dtype) * w


def dwconv(x, w, b):
    y = lax.conv_general_dilated(x, w[:, None, :].astype(x.dtype), window_strides=(1,), padding='SAME',
                                 dimension_numbers=('NWC', 'WIO', 'NWC'), feature_group_count=x.shape[-1])
    return y + b.astype(x.dtype)


def maybe_flip(a, rev):
    return jnp.flip(a, axis=1) if rev else a


def to_col_major(x):
    b, n, d = x.shape
    rows = n // GRID_W
    return x.reshape(b, rows, GRID_W, d).transpose(0, 2, 1, 3).reshape(b, n, d)


def from_col_major(x):
    b, n, d = x.shape
    rows = n // GRID_W
    return x.reshape(b, GRID_W, rows, d).transpose(0, 2, 1, 3).reshape(b, n, d)


def gla_inputs(p, gate_w, gate_b):
    b, n, _ = p.shape
    nqk, nv = GLA_H * GLA_DK, GLA_H * GLA_DV
    q = p[..., :nqk].reshape(b, n, GLA_H, GLA_DK) * GLA_DK ** -0.5
    k = p[..., nqk:2 * nqk].reshape(b, n, GLA_H, GLA_DK)
    v = p[..., 2 * nqk:2 * nqk + nv].reshape(b, n, GLA_H, GLA_DV)
    lr = p[..., 2 * nqk + nv:2 * nqk + nv + 2 * GLA_RANK].reshape(b, n, 2, GLA_RANK)
    r = p[..., 2 * nqk + nv + 2 * GLA_RANK:]
    logit = jnp.einsum('bler,erk->blek', lr, gate_w) + gate_b
    log_g = (jax.nn.log_sigmoid(logit.astype(F32)) / GLA_GATE_NORM).reshape(b, n, 2, GLA_H, GLA_DK)
    return q, k, v, log_g, r


def gla_states(k, v, log_g, s0):
    b, n = k.shape[:2]
    nc = n // GLA_CHUNK
    kc = k.reshape(b, nc, GLA_CHUNK, GLA_H, GLA_DK)
    vc = v.reshape(b, nc, GLA_CHUNK, GLA_H, GLA_DV)
    gcum = jnp.cumsum(log_g.reshape(b, nc, GLA_CHUNK, GLA_H, GLA_DK), axis=2)
    states = jnp.einsum('bcqhd,bcqhv->bchdv', kc * jnp.exp(gcum[:, :, -1:] - gcum), vc)
    chunk_decay = jnp.exp(gcum[:, :, -1])

    def step(s, inp):
        st, dcy = inp
        return dcy[..., None] * s + st, s

    s_fin, s_prev = lax.scan(step, s0, (jnp.moveaxis(states, 1, 0), jnp.moveaxis(chunk_decay, 1, 0)))
    return jnp.moveaxis(s_prev, 0, 1), s_fin


def gla_output(q, k, v, log_g, s_prev):
    b, n = q.shape[:2]
    nc = n // GLA_CHUNK
    qc = q.reshape(b, nc, GLA_CHUNK, GLA_H, GLA_DK)
    kc = k.reshape(b, nc, GLA_CHUNK, GLA_H, GLA_DK)
    vc = v.reshape(b, nc, GLA_CHUNK, GLA_H, GLA_DV)
    gcum = jnp.cumsum(log_g.reshape(b, nc, GLA_CHUNK, GLA_H, GLA_DK), axis=2)
    qg = qc * jnp.exp(gcum)
    kg = kc * jnp.exp(-gcum)
    mask = jnp.tril(jnp.ones((GLA_CHUNK, GLA_CHUNK), bool))
    att = jnp.where(mask, jnp.einsum('bcihd,bcjhd->bchij', qg, kg), 0.0)
    o = jnp.einsum('bchij,bcjhv->bcihv', att, vc) + jnp.einsum('bcihd,bchdv->bcihv', qg, s_prev)
    return o.reshape(b, n, GLA_H, GLA_DV)


def gla_merge(os_, r, norm_w):
    b, n = r.shape[:2]
    o = rms_norm(os_[0] + os_[1], norm_w.reshape(GLA_H, GLA_DV))
    return o.reshape(b, n, GLA_H * GLA_DV) * jax.nn.silu(r)


def gla_mixer(p_c, p_l, gate_w, gate_b, norm_w):
    q_c, k_c, v_c, g_c, r_c = gla_inputs(p_c, gate_w, gate_b)
    q_l, k_l, v_l, g_l, r_l = gla_inputs(p_l, gate_w, gate_b)
    s0 = jnp.zeros((p_c.shape[0], GLA_H, GLA_DK, GLA_DV), F32)
    os_l = []
    for d, rev in enumerate((False, True)):
        f = functools.partial(maybe_flip, rev=rev)
        _, sf_c = gla_states(f(k_c), f(v_c), f(g_c[:, :, d]), s0)
        sp_l, _ = gla_states(f(k_l), f(v_l), f(g_l[:, :, d]), sf_c)
        os_l.append(f(gla_output(f(q_l), f(k_l), f(v_l), f(g_l[:, :, d]), sp_l)))
    return gla_merge(os_l, r_l, norm_w)


def rglru_inputs(p, conv_w, conv_b, w_a, b_a, w_x, b_x, lam):
    b, n, _ = p.shape
    u = dwconv(p[..., :D_RG], conv_w, conv_b)
    ub = u.reshape(b, n, RG_BLOCKS, RG_BW)
    r = jax.nn.sigmoid((jnp.einsum('blni,enio->bleno', ub, w_a).reshape(b, n, 2, D_RG) + b_a).astype(F32))
    i = jax.nn.sigmoid((jnp.einsum('blni,enio->bleno', ub, w_x).reshape(b, n, 2, D_RG) + b_x).astype(F32))
    log_a = -RG_C * jax.nn.softplus(-lam.astype(F32)) * r
    x_in = jnp.sqrt(-jnp.expm1(2.0 * log_a)) * i * u[:, :, None, :].astype(F32)
    return p[..., D_RG:], jnp.exp(log_a), x_in


def lru_scan(a, u, h0):
    u = u.at[:, 0].add(a[:, 0] * h0)

    def combine(lhs, rhs):
        a1, b1 = lhs
        a2, b2 = rhs
        return a1 * a2, a2 * b1 + b2

    return lax.associative_scan(combine, (a, u), axis=1)[1]


def rglru_mixer(p_c, p_l, conv_w, conv_b, w_a, b_a, w_x, b_x, lam):
    gb_c, a_c, u_c = rglru_inputs(p_c, conv_w, conv_b, w_a, b_a, w_x, b_x, lam)
    gb_l, a_l, u_l = rglru_inputs(p_l, conv_w, conv_b, w_a, b_a, w_x, b_x, lam)
    h0 = jnp.zeros((p_c.shape[0], D_RG), F32)
    hs_l = []
    for d, rev in enumerate((False, True)):
        f = functools.partial(maybe_flip, rev=rev)
        h_c = f(lru_scan(f(a_c[:, :, d]), f(u_c[:, :, d]), h0))
        h_end = h_c[:, 0] if rev else h_c[:, -1]
        hs_l.append(f(lru_scan(f(a_l[:, :, d]), f(u_l[:, :, d]), h_end)))
    return (hs_l[0] + hs_l[1]) * jax.nn.gelu(gb_l.astype(F32))


def odd_layer_mixer(xs, mods, norm_w, w_in, w_out, gla_args, rg_args):
    bsz = xs.shape[0]
    h = norm_mod(xs, norm_w, mods, 0, 1)
    h = jnp.concatenate([to_col_major(h[:, :SEQ]), h[:, SEQ:]], axis=1)
    p = pmatmul(h.reshape(bsz * NTOK, D_MODEL), w_in.astype(BF16)).reshape(bsz, NTOK, OD_IN)
    p_l, p_c = p[:, :SEQ], p[:, SEQ:]
    a_l = gla_mixer(p_c[..., :GLA_IN], p_l[..., :GLA_IN], *gla_args)
    r_l = rglru_mixer(p_c[..., GLA_IN:], p_l[..., GLA_IN:], *rg_args)
    mix = from_col_major(jnp.concatenate([a_l, r_l], axis=-1)).astype(BF16)
    return mm_resid([mix], [w_out.astype(BF16)], xs, mods, 2, N_LAT_TILES)


def kernel(x, c, ctx, c_ctx, ada_w, ada_b, norm1_w, norm2_w, ev_w_in, ev_w_out, hy_conv_w, hy_conv_b, hy_fw0, hy_fb0, hy_fw1, hy_fb1, hy_fw2, hy_fb2, hy_fw3, hy_freq, hy_bias, ssd_conv_w, ssd_conv_b, ssd_dt_bias, ssd_a_log, ssd_d, ssd_norm_w, od_w_in, od_w_out, gla_gate_w, gla_gate_b, gla_norm_w, rg_conv_w, rg_conv_b, rg_w_a, rg_b_a, rg_w_x, rg_b_x, rg_lambda, router_w, router_b, moe_w_gate, moe_w_up, moe_w_down, sh_w_gate, sh_w_up, sh_w_down, final_norm_w):
    xs = jnp.concatenate([x, ctx], axis=1)
    for i in range(DEPTH):
        last = i == DEPTH - 1
        j = i // 2
        mods = adaln_table(c, c_ctx, ada_w[i], ada_b[i])
        if i % 2 == 0:
            hy_filt = (hy_fw0[j], hy_fb0[j], hy_fw1[j], hy_fb1[j], hy_fw2[j], hy_fb2[j], hy_fw3[j], hy_freq[j])
            xs = even_layer_mixer(xs, mods, norm1_w[i], ev_w_in[j], ev_w_out[j], hy_conv_w[j], hy_conv_b[j],
                                  hy_filt, hy_bias[j], ssd_conv_w[j], ssd_conv_b[j], ssd_dt_bias[j],
                                  ssd_a_log[j], ssd_d[j], ssd_norm_w[j])
        else:
            gla_args = (gla_gate_w[j], gla_gate_b[j], gla_norm_w[j])
            rg_args = (rg_conv_w[j], rg_conv_b[j], rg_w_a[j], rg_b_a[j], rg_w_x[j], rg_b_x[j], rg_lambda[j])
            xs = odd_layer_mixer_pallas(xs, mods, norm1_w[i], od_w_in[j], od_w_out[j], gla_args, rg_args)
        n_tiles = N_LAT_TILES if last else N_ROW_TILES
        xs = moe_layer(xs, norm2_w[i], mods, router_w[i], router_b[i], moe_w_gate, moe_w_up, moe_w_down,
                       sh_w_gate[i], sh_w_up[i], sh_w_down[i], n_tiles, i)
    return final_norm(xs, final_norm_w)
```

```python
import functools
import math

import jax
import jax.numpy as jnp
from jax import lax
from jax.experimental import pallas as pl
from jax.experimental.pallas import tpu as pltpu

D_MODEL = 1024
BATCH = 16
SEQ = 2048
DEPTH = 2

CTX_LEN = 256
GRID_W = 64
NORM_EPS = 1e-6

D_HY = D_MODEL // 2
HY_ORDER = 2
HY_SHORT = 3
HY_BANDS = 16
HY_EMB = 1 + 2 * HY_BANDS
HY_FF = 64
HY_DECAY_PCT_LO = 0.3
HY_DECAY_PCT_HI = 1.5
HY_DECAY_TARGET = 1e-2
HY_IN = 3 * D_HY

D_SSM = D_MODEL // 2
SSD_P = 64
SSD_H = D_SSM // SSD_P
SSD_G = 2
SSD_HPG = SSD_H // SSD_G
SSD_N = 128
SSD_CONV = 4
SSD_CHUNK = 128
SSD_XBC = D_SSM + 2 * SSD_G * SSD_N
SSD_IN = D_SSM + SSD_XBC + 2 * SSD_H
EV_IN = HY_IN + SSD_IN
EV_MIX = D_HY + D_SSM

GLA_H = 4
GLA_DV = (D_MODEL // 2) // GLA_H
GLA_DK = GLA_DV // 2
GLA_RANK = 16
GLA_GATE_NORM = 16.0
GLA_CHUNK = 64
GLA_IN = 2 * GLA_H * GLA_DK + 2 * GLA_H * GLA_DV + 2 * GLA_RANK

D_RG = D_MODEL // 2
RG_BLOCKS = 8
RG_BW = D_RG // RG_BLOCKS
RG_CONV = 4
RG_C = 8.0
RG_IN = 2 * D_RG
OD_IN = GLA_IN + RG_IN
OD_MIX = GLA_H * GLA_DV + D_RG

MOE_EXPERTS = 64
MOE_TOPK = 8
MOE_D_EXPERT = 256
MOE_D_SHARED = 256
MOE_SCALE = 2.5
MOE_BLOCK = 512

F32 = jnp.float32
BF16 = jnp.bfloat16
HIGHEST = lax.Precision.HIGHEST

NTOK = SEQ + CTX_LEN
ROW_TILE = 256
N_ROW_TILES = NTOK // ROW_TILE
N_LAT_TILES = SEQ // ROW_TILE

VMEM_LIMIT = 48 * 1024 * 1024
VMEM_LIMIT_BIG = 56 * 1024 * 1024


def _cparams(sem, limit=VMEM_LIMIT):
    return pltpu.CompilerParams(dimension_semantics=sem, vmem_limit_bytes=limit)


def _pick_tile(n, pref):
    t = min(n, pref)
    while n % t:
        t //= 2
    return t


def _silu(x):
    return x / (1.0 + jnp.exp(-x))


def _softplus(x):
    return jnp.maximum(x, 0.0) + jnp.log(1.0 + jnp.exp(-jnp.abs(x)))


def _mm_bf16_body(a_ref, w_ref, o_ref):
    o_ref[...] = jnp.dot(a_ref[...].astype(BF16), w_ref[...].astype(BF16),
                         preferred_element_type=F32).astype(o_ref.dtype)


def _mm_f32_body(a_ref, w_ref, o_ref):
    o_ref[...] = jnp.dot(a_ref[...], w_ref[...], preferred_element_type=F32,
                         precision=HIGHEST).astype(o_ref.dtype)


def pmatmul(a, w, *, exact=False, out_dtype=F32, tm=512, tn=None):
    m, k = a.shape
    n = w.shape[1]
    tm = _pick_tile(m, tm)
    tn = n if tn is None else _pick_tile(n, tn)
    body = _mm_f32_body if exact else _mm_bf16_body
    return pl.pallas_call(
        body,
        name='mm',
        grid=(m // tm, n // tn),
        in_specs=[pl.BlockSpec((tm, k), lambda i, j: (i, 0)),
                  pl.BlockSpec((k, tn), lambda i, j: (0, j))],
        out_specs=pl.BlockSpec((tm, tn), lambda i, j: (i, j)),
        out_shape=jax.ShapeDtypeStruct((m, n), out_dtype),
        compiler_params=_cparams(("parallel", "parallel")),
    )(a, w)


def _mm_split_body(a_ref, w_ref, *o_refs, splits):
    a = a_ref[...]
    for o_ref, (start, width) in zip(o_refs, splits):
        o_ref[...] = jnp.dot(a, w_ref[:, start:start + width],
                             preferred_element_type=F32).astype(o_ref.dtype)


def mm_split(a, w, splits, dtypes, tm=512):
    m, k = a.shape
    n = w.shape[1]
    tm = _pick_tile(m, tm)
    return pl.pallas_call(
        functools.partial(_mm_split_body, splits=tuple(splits)),
        name='mm_split',
        grid=(m // tm,),
        in_specs=[pl.BlockSpec((tm, k), lambda i: (i, 0)),
                  pl.BlockSpec((k, n), lambda i: (0, 0))],
        out_specs=[pl.BlockSpec((tm, wd), lambda i: (i, 0)) for _, wd in splits],
        out_shape=[jax.ShapeDtypeStruct((m, wd), dt) for (_, wd), dt in zip(splits, dtypes)],
        compiler_params=_cparams(("parallel",)),
    )(a, w)


def _mm_resid_body(*refs, n_pairs):
    a_refs = refs[:n_pairs]
    w_refs = refs[n_pairs:2 * n_pairs]
    x_ref, g_ref, o_ref = refs[2 * n_pairs:]
    acc = jnp.dot(a_refs[0][0], w_refs[0][...], preferred_element_type=F32)
    for a_ref, w_ref in zip(a_refs[1:], w_refs[1:]):
        acc = acc + jnp.dot(a_ref[0], w_ref[...], preferred_element_type=F32)
    o_ref[0] = x_ref[0] + g_ref[...] * acc


def mm_resid(a_list, w_list, xs, mods, gate_idx, n_tiles):
    b, nt, d = xs.shape
    n_pairs = len(a_list)
    in_specs = [pl.BlockSpec((1, ROW_TILE, a.shape[-1]), lambda i, j: (i, j, 0)) for a in a_list]
    in_specs += [pl.BlockSpec(w.shape, lambda i, j: (0, 0)) for w in w_list]
    in_specs += [pl.BlockSpec((1, ROW_TILE, d), lambda i, j: (i, j, 0)),
                 _mod_spec(gate_idx, d)]
    return pl.pallas_call(
        functools.partial(_mm_resid_body, n_pairs=n_pairs),
        name='mm_resid',
        grid=(b, n_tiles),
        in_specs=in_specs,
        out_specs=pl.BlockSpec((1, ROW_TILE, d), lambda i, j: (i, j, 0)),
        out_shape=jax.ShapeDtypeStruct(xs.shape, F32),
        input_output_aliases={2 * n_pairs: 0},
        compiler_params=_cparams(("parallel", "parallel")),
    )(*a_list, *w_list, xs, mods)


def _mod_spec(idx, d, b0=0):
    return pl.BlockSpec((None, None, None, 1, d), lambda i, j: (i + b0, 1 - j // N_LAT_TILES, idx, 0, 0))


def adaln_table(c, c_ctx, w, b):
    cv = jax.nn.silu(jnp.concatenate([c, c_ctx[None, :]], axis=0))
    cv = jnp.pad(cv, ((0, 24 - cv.shape[0]), (0, 0)))
    m = pmatmul(cv, w, exact=True, tn=1536)[:BATCH + 1] + b
    per_sample = m[:BATCH]
    ctx_row = jnp.broadcast_to(m[BATCH][None, :], per_sample.shape)
    return jnp.stack([ctx_row, per_sample], axis=1).reshape(BATCH, 2, 6, 1, D_MODEL)


def _norm_mod(x, w, shift, scale):
    ms = jnp.mean(x * x, axis=-1, keepdims=True)
    return (x * lax.rsqrt(ms + NORM_EPS) * w) * (1.0 + scale) + shift


def _norm_mod_body(x_ref, w_ref, sh_ref, sc_ref, o_ref):
    o_ref[0] = _norm_mod(x_ref[0], w_ref[...], sh_ref[...], sc_ref[...]).astype(o_ref.dtype)


def norm_mod(xs, w, mods, shift_idx, scale_idx):
    b, nt, d = xs.shape
    return pl.pallas_call(
        _norm_mod_body,
        name='norm_mod',
        grid=(b, nt // ROW_TILE),
        in_specs=[pl.BlockSpec((1, ROW_TILE, d), lambda i, j: (i, j, 0)),
                  pl.BlockSpec((1, d), lambda i, j: (0, 0)),
                  _mod_spec(shift_idx, d), _mod_spec(scale_idx, d)],
        out_specs=pl.BlockSpec((1, ROW_TILE, d), lambda i, j: (i, j, 0)),
        out_shape=jax.ShapeDtypeStruct(xs.shape, BF16),
        compiler_params=_cparams(("parallel", "parallel")),
    )(xs, w.reshape(1, d), mods, mods)


def _final_norm_body(x_ref, w_ref, o_ref):
    x = x_ref[0]
    ms = jnp.mean(x * x, axis=-1, keepdims=True)
    o_ref[0] = x * lax.rsqrt(ms + NORM_EPS) * w_ref[...]


def final_norm(xs, w):
    b, _, d = xs.shape
    return pl.pallas_call(
        _final_norm_body,
        name='final_norm',
        grid=(b, N_LAT_TILES),
        in_specs=[pl.BlockSpec((1, ROW_TILE, d), lambda i, j: (i, j, 0)),
                  pl.BlockSpec((1, d), lambda i, j: (0, 0))],
        out_specs=pl.BlockSpec((1, ROW_TILE, d), lambda i, j: (i, j, 0)),
        out_shape=jax.ShapeDtypeStruct((b, SEQ, d), F32),
        compiler_params=_cparams(("parallel", "parallel")),
    )(xs, w.reshape(1, d))


def _route_body(x_ref, w_ref, sh_ref, sc_ref, rw_ref, rb_ref, h_ref, idx_ref, wsel_ref, rank_ref, cnt_ref,
                *, group_size):
    first = (pl.program_id(0) % group_size == 0) & (pl.program_id(1) == 0)

    @pl.when(first)
    def _():
        cnt_ref[...] = jnp.zeros_like(cnt_ref)

    h = _norm_mod(x_ref[0], w_ref[...], sh_ref[...], sc_ref[...])
    h_ref[0] = h.astype(h_ref.dtype)
    logits = jnp.dot(h, rw_ref[...], preferred_element_type=F32, precision=HIGHEST)
    scores = 1.0 / (1.0 + jnp.exp(-logits))
    tm, ne = scores.shape
    lane = lax.broadcasted_iota(jnp.int32, (tm, ne), 1).astype(F32)
    slot = lax.broadcasted_iota(jnp.int32, (tm, MOE_TOPK), 1)
    sel = scores + rb_ref[...]
    picked = jnp.zeros((tm, ne), F32)
    hits = []
    idx_out = jnp.zeros((tm, MOE_TOPK), F32)
    w_out = jnp.zeros((tm, MOE_TOPK), F32)
    for k in range(MOE_TOPK):
        m = jnp.max(sel, axis=-1, keepdims=True)
        ik = jnp.min(jnp.where(sel == m, lane, float(ne)), axis=-1, keepdims=True)
        hit = lane == ik
        wk = jnp.sum(jnp.where(hit, scores, 0.0), axis=-1, keepdims=True)
        sel = jnp.where(hit, -jnp.inf, sel)
        picked = picked + hit.astype(F32)
        hits.append(hit)
        idx_out = jnp.where(slot == k, ik, idx_out)
        w_out = jnp.where(slot == k, wk, w_out)
    wsum = jnp.sum(w_out, axis=-1, keepdims=True)
    wsel_ref[0] = w_out / wsum * MOE_SCALE
    idx_ref[0] = idx_out.astype(jnp.int32)
    r_i = lax.broadcasted_iota(jnp.int32, (tm, tm), 0)
    c_i = lax.broadcasted_iota(jnp.int32, (tm, tm), 1)
    strict_lower = (c_i < r_i).astype(BF16)
    before = jnp.dot(strict_lower, picked.astype(BF16), preferred_element_type=F32) + cnt_ref[...]
    rank_out = jnp.zeros((tm, MOE_TOPK), F32)
    for k in range(MOE_TOPK):
        rk = jnp.sum(jnp.where(hits[k], before, 0.0), axis=-1, keepdims=True)
        rank_out = jnp.where(slot == k, rk, rank_out)
    rank_ref[0] = rank_out.astype(jnp.int32)
    cnt_ref[...] = cnt_ref[...] + jnp.sum(picked, axis=0, keepdims=True)


def route(xs, w, mods, router_w, router_b, n_tiles, group_size):
    b, _, d = xs.shape
    rows = n_tiles * ROW_TILE
    small = lambda dt: jax.ShapeDtypeStruct((b, rows, MOE_TOPK), dt)
    small_spec = pl.BlockSpec((1, ROW_TILE, MOE_TOPK), lambda i, j: (i, j, 0))
    return pl.pallas_call(
        functools.partial(_route_body, group_size=group_size),
        name='route',
        grid=(b, n_tiles),
        in_specs=[pl.BlockSpec((1, ROW_TILE, d), lambda i, j: (i, j, 0)),
                  pl.BlockSpec((1, d), lambda i, j: (0, 0)),
                  _mod_spec(3, d), _mod_spec(4, d),
                  pl.BlockSpec((d, MOE_EXPERTS), lambda i, j: (0, 0)),
                  pl.BlockSpec((1, MOE_EXPERTS), lambda i, j: (0, 0))],
        out_specs=[pl.BlockSpec((1, ROW_TILE, d), lambda i, j: (i, j, 0)),
                   small_spec, small_spec, small_spec,
                   pl.BlockSpec((None, 1, MOE_EXPERTS), lambda i, j: (i // group_size, 0, 0))],
        out_shape=[jax.ShapeDtypeStruct((b, rows, d), BF16), small(jnp.int32), small(F32), small(jnp.int32),
                   jax.ShapeDtypeStruct((b // group_size, 1, MOE_EXPERTS), F32)],
        compiler_params=_cparams(("arbitrary", "arbitrary")),
    )(xs, w.reshape(1, d), mods, mods, router_w, router_b.reshape(1, MOE_EXPERTS))


def _swiglu(x, wg, wu, wd):
    g = jnp.dot(x, wg, preferred_element_type=F32)
    u = jnp.dot(x, wu, preferred_element_type=F32)
    h = (_silu(g) * u).astype(BF16)
    return jnp.dot(h, wd, preferred_element_type=F32)


def _expert_body(be_ref, nu_ref, x_ref, wg_ref, wu_ref, wd_ref, o_ref, wg_s, wu_s, wd_s):
    i = pl.program_id(0)
    used = i < nu_ref[0]

    @pl.when(used & ((i == 0) | (be_ref[i] != be_ref[jnp.maximum(i - 1, 0)])))
    def _():
        wg_s[...] = wg_ref[0].astype(BF16)
        wu_s[...] = wu_ref[0].astype(BF16)
        wd_s[...] = wd_ref[0].astype(BF16)

    @pl.when(used)
    def _():
        o_ref[...] = _swiglu(x_ref[...], wg_s[...], wu_s[...], wd_s[...]).astype(o_ref.dtype)

    @pl.when(jnp.logical_not(used))
    def _():
        o_ref[...] = jnp.zeros_like(o_ref)


def moe_experts(x_rows, block_e, n_used, wg, wu, wd, layer):
    rows, d = x_rows.shape
    n_blocks = rows // MOE_BLOCK
    f = wg.shape[-1]
    grid_spec = pltpu.PrefetchScalarGridSpec(
        num_scalar_prefetch=2,
        grid=(n_blocks,),
        in_specs=[
            pl.BlockSpec((MOE_BLOCK, d), lambda i, be, nu: (i, 0)),
            pl.BlockSpec((None, 1, d, f), lambda i, be, nu: (layer, be[i], 0, 0)),
            pl.BlockSpec((None, 1, d, f), lambda i, be, nu: (layer, be[i], 0, 0)),
            pl.BlockSpec((None, 1, f, d), lambda i, be, nu: (layer, be[i], 0, 0)),
        ],
        out_specs=pl.BlockSpec((MOE_BLOCK, d), lambda i, be, nu: (i, 0)),
        scratch_shapes=[pltpu.VMEM((d, f), BF16), pltpu.VMEM((d, f), BF16), pltpu.VMEM((f, d), BF16)],
    )
    return pl.pallas_call(
        _expert_body,
        name='experts',
        grid_spec=grid_spec,
        out_shape=jax.ShapeDtypeStruct((rows, d), BF16),
        compiler_params=_cparams(("arbitrary",)),
    )(block_e, n_used, x_rows, wg, wu, wd)


def _shared_resid_body(h_ref, wg_ref, wu_ref, wd_ref, pk_ref, ws_ref, x_ref, g_ref, o_ref):
    y = _swiglu(h_ref[0], wg_ref[...], wu_ref[...], wd_ref[...])
    ws = ws_ref[0]
    for k in range(MOE_TOPK):
        y = y + ws[:, k:k + 1] * pk_ref[k, 0].astype(F32)
    o_ref[0] = x_ref[0] + g_ref[...] * y


def shared_resid(h, picked, wsel, xs, mods, wg, wu, wd, n_tiles, b0):
    b, _, d = h.shape
    f = wg.shape[-1]
    tile = pl.BlockSpec((1, ROW_TILE, d), lambda i, j: (i, j, 0))
    xs_tile = pl.BlockSpec((1, ROW_TILE, d), lambda i, j: (i + b0, j, 0))
    return pl.pallas_call(
        _shared_resid_body,
        name='shared_resid',
        grid=(b, n_tiles),
        in_specs=[tile,
                  pl.BlockSpec((d, f), lambda i, j: (0, 0)),
                  pl.BlockSpec((d, f), lambda i, j: (0, 0)),
                  pl.BlockSpec((f, d), lambda i, j: (0, 0)),
                  pl.BlockSpec((MOE_TOPK, 1, ROW_TILE, d), lambda i, j: (0, i, j, 0)),
                  pl.BlockSpec((1, ROW_TILE, MOE_TOPK), lambda i, j: (i, j, 0)),
                  xs_tile, _mod_spec(5, d, b0)],
        out_specs=xs_tile,
        out_shape=jax.ShapeDtypeStruct(xs.shape, F32),
        input_output_aliases={6: 0},
        compiler_params=_cparams(("parallel", "parallel")),
    )(h, wg, wu, wd, picked, wsel, xs, mods)


MOE_GROUPS = 2


def moe_layer(xs, norm_w, mods, router_w, router_b, w_gate, w_up, w_down, sh_gate, sh_up, sh_down, n_tiles, layer):
    bsz, _, d = xs.shape
    shared_w = (sh_gate.astype(BF16), sh_up.astype(BF16), sh_down.astype(BF16))
    b = bsz // MOE_GROUPS
    h, idx, wsel, rank, counts = route(xs, norm_w, mods, router_w, router_b, n_tiles, b)
    h_flat = h.reshape(-1, d)
    for g in range(MOE_GROUPS):
        sl = slice(g * b, (g + 1) * b)
        xs = _moe_group(xs, mods, h_flat, h[sl], idx[sl], wsel[sl], rank[sl], counts[g, 0], w_gate, w_up, w_down,
                        shared_w, n_tiles, layer, g * b)
    return xs


def _moe_group(xs, mods, h_flat, h, idx, wsel, rank, counts, w_gate, w_up, w_down, shared_w, n_tiles, layer, b0):
    b, rows_per_sample, d = h.shape
    n = b * rows_per_sample
    counts = counts.astype(jnp.int32)
    padded = (counts + MOE_BLOCK - 1) // MOE_BLOCK * MOE_BLOCK
    ends = jnp.cumsum(padded)
    starts = ends - padded
    nk = n * MOE_TOPK
    n_blocks = -(-nk // MOE_BLOCK) + MOE_EXPERTS
    rows = n_blocks * MOE_BLOCK
    n_pad = rows - nk
    e_iota = jnp.arange(MOE_EXPERTS, dtype=jnp.int32)
    dest = jnp.sum(jnp.where(idx[..., None] == e_iota, starts, 0), axis=-1) + rank
    blk_start = jnp.arange(n_blocks, dtype=jnp.int32) * MOE_BLOCK
    block_e = jnp.minimum(jnp.sum(ends[None, :] <= blk_start[:, None], axis=1), MOE_EXPERTS - 1).astype(jnp.int32)
    n_used = (ends[-1:] // MOE_BLOCK).astype(jnp.int32)
    pad = padded - counts
    cum_pad = jnp.cumsum(pad)
    m = jnp.arange(n_pad, dtype=jnp.int32)
    e_m = jnp.sum(cum_pad[None, :] <= m[:, None], axis=1)
    base = jnp.sum(jnp.where(jnp.minimum(e_m, MOE_EXPERTS - 1)[:, None] == e_iota,
                             starts + counts - (cum_pad - pad), 0), axis=1)
    pad_row = jnp.where(e_m < MOE_EXPERTS, base + m, ends[-1] + m - cum_pad[-1])
    tok0 = b0 * rows_per_sample
    tok = tok0 + jnp.arange(nk, dtype=jnp.int32) // MOE_TOPK
    _, row_tok = lax.sort((jnp.concatenate([dest.reshape(-1), pad_row]).astype(jnp.int32),
                           jnp.concatenate([tok, tok0 + m % n])), num_keys=1)
    x_rows = h_flat[row_tok]
    y_rows = moe_experts(x_rows, block_e, n_used, w_gate, w_up, w_down, layer)
    picked = y_rows[dest.reshape(n, MOE_TOPK).T].reshape(MOE_TOPK, b, rows_per_sample, d)
    return shared_resid(h, picked, wsel, xs, mods, *shared_w, n_tiles, b0)


def _dwconv_body(x_ref, w_ref, b_ref, o_ref, *, width, act):
    chunk = ROW_TILE
    n_chunks = NTOK // chunk
    first_of_seq = (0, N_LAT_TILES)
    last_of_seq = (N_LAT_TILES - 1, n_chunks - 1)
    tc = x_ref.shape[-1]
    halo = 16
    row = lax.broadcasted_iota(jnp.int32, (chunk, tc), 0)
    zero_row = jnp.zeros((1, tc), F32)
    for c in range(n_chunks):
        r0 = c * chunk
        cur = x_ref[0, r0:r0 + chunk, :].astype(F32)
        if c in first_of_seq:
            prev_last = zero_row
        else:
            prev_last = x_ref[0, r0 - halo:r0, :].astype(F32)[halo - 1:halo, :]
        if c in last_of_seq:
            next0 = next1 = zero_row
        else:
            nxt = x_ref[0, r0 + chunk:r0 + chunk + halo, :].astype(F32)
            next0, next1 = nxt[0:1, :], nxt[1:2, :]
        xm1 = jnp.where(row == 0, prev_last, pltpu.roll(cur, 1, 0))
        xp1 = jnp.where(row == chunk - 1, next0, pltpu.roll(cur, chunk - 1, 0))
        y = w_ref[0:1, :] * xm1 + w_ref[1:2, :] * cur + w_ref[2:3, :] * xp1 + b_ref[...]
        if width == 4:
            xp2 = jnp.where(row == chunk - 2, next0,
                            jnp.where(row == chunk - 1, next1, pltpu.roll(cur, chunk - 2, 0)))
            y = y + w_ref[3:4, :] * xp2
        if act:
            y = _silu(y)
        o_ref[0, c * chunk:(c + 1) * chunk, :] = y.astype(o_ref.dtype)


def dwconv_stream(x, w, b, act, tc=256):
    bsz, nt, c = x.shape
    width = w.shape[0]
    return pl.pallas_call(
        functools.partial(_dwconv_body, width=width, act=act),
        name='dwconv',
        grid=(bsz, c // tc),
        in_specs=[pl.BlockSpec((1, nt, tc), lambda i, j: (i, 0, j)),
                  pl.BlockSpec((width, tc), lambda i, j: (0, j)),
                  pl.BlockSpec((1, tc), lambda i, j: (0, j))],
        out_specs=pl.BlockSpec((1, nt, tc), lambda i, j: (i, 0, j)),
        out_shape=jax.ShapeDtypeStruct(x.shape, BF16),
        compiler_params=_cparams(("parallel", "parallel")),
    )(x, w, b.reshape(1, c))


HY_FB = 512


def dft_matrices(n):
    k = jnp.arange(n, dtype=jnp.int32)[:, None]
    t = jnp.arange(n, dtype=jnp.int32)[None, :]
    ang = (2.0 * math.pi / (2 * n)) * ((k * t) % (2 * n)).astype(F32)
    fre = jnp.cos(ang)
    fim = -jnp.sin(ang)
    nyq = jnp.where(t % 2 == 0, 1.0, -1.0).astype(F32)
    fim = jnp.where(k == 0, nyq, fim)
    fwd = jnp.concatenate([fre, fim], axis=0)
    colscale = jnp.where(jnp.arange(2 * n) % n == 0, 0.5, 1.0) / n
    inv = fwd.T * colscale[None, :]
    return fwd, inv


def hyena_filter_taps(n, fw0, fb0, fw1, fb1, fw2, fb2, fw3, freq):
    pos = jnp.arange(n, dtype=F32)
    t = pos / max(n - 1, 1)
    bands = jnp.linspace(1e-4, HY_BANDS - 1, HY_BANDS, dtype=F32)
    ang = (2.0 * math.pi / n) * pos[:, None] * bands[None, :]
    feats = jnp.concatenate([t[:, None], jnp.cos(ang), -jnp.sin(ang)], axis=-1)
    h = jnp.sin(freq * (jnp.dot(feats, fw0, precision=HIGHEST) + fb0))
    h = jnp.sin(freq * (jnp.dot(h, fw1, precision=HIGHEST) + fb1))
    h = jnp.sin(freq * (jnp.dot(h, fw2, precision=HIGHEST) + fb2))
    h = pmatmul(h, fw3, exact=True).reshape(n, 2, HY_ORDER, D_HY)
    deltas = jnp.abs(jnp.linspace(math.log(HY_DECAY_PCT_LO) / HY_DECAY_TARGET,
                                  math.log(HY_DECAY_PCT_HI) / HY_DECAY_TARGET, D_HY, dtype=F32))
    h = h * jnp.exp(-t[:, None] * deltas)[:, None, None, :]
    h0 = h[:, 0]
    h1 = h[:, 1].at[0].set(0.0)
    norm = jnp.sum(jnp.abs(h0), axis=0, keepdims=True) + jnp.sum(jnp.abs(h1), axis=0, keepdims=True)
    h0 = (h0 / norm).reshape(n, HY_ORDER * D_HY)
    h1 = (h1 / norm).reshape(n, HY_ORDER * D_HY)
    return h0 + h1, h0 - h1


def _split_bf16(a):
    hi = a.astype(BF16)
    return hi, (a - hi.astype(F32)).astype(BF16)


def hyena_spectrum(fwd, hsum, hdiff, fb):
    n = hsum.shape[0]
    f_hi, f_lo = _split_bf16(fwd)

    def dft(h):
        h_hi, h_lo = _split_bf16(h)
        return pmatmul(f_hi, h_hi) + pmatmul(f_hi, h_lo) + pmatmul(f_lo, h_hi)

    a = dft(hsum)
    bm = dft(hdiff)
    sr = a[:n]
    si = bm[n:]
    nyq = a[n]
    first = (jnp.arange(n) == 0)[:, None]
    p = sr
    q = jnp.where(first, 0.0, si)
    s = jnp.where(first, nyq[None, :], sr)
    spec = jnp.stack([p, q, s], axis=0).reshape(3, n // fb, fb, HY_ORDER, D_HY)
    return spec.transpose(3, 1, 0, 2, 4), f_hi


def _hyena_body(u_ref, fre_ref, fim_ref, gre_ref, gim_ref, sp_ref, bias_ref, prev_ref, o_ref,
                vin, acc, *, nf):
    del prev_ref
    o = pl.program_id(1)
    f = pl.program_id(2)
    c = D_HY

    @pl.when((o == 0) & (f == 0))
    def _():
        vin[...] = u_ref[0, :, 0:c]

    @pl.when(f == 0)
    def _():
        acc[...] = jnp.zeros_like(acc)

    v = vin[...]
    vr = jnp.dot(fre_ref[...], v, preferred_element_type=F32)
    vi = jnp.dot(fim_ref[...], v, preferred_element_type=F32)
    p, q, s = sp_ref[0], sp_ref[1], sp_ref[2]
    zr = (vr * p - vi * q).astype(BF16)
    zi = (vr * q + vi * s).astype(BF16)
    acc[...] += (jnp.dot(gre_ref[...], zr, preferred_element_type=F32)
                 + jnp.dot(gim_ref[...], zi, preferred_element_type=F32))

    @pl.when((o == 0) & (f == nf - 1))
    def _():
        z = u_ref[0, :, c:2 * c].astype(F32) * (acc[...] + bias_ref[0:1, :] * vin[...].astype(F32))
        vin[...] = z.astype(BF16)

    @pl.when((o == 1) & (f == nf - 1))
    def _():
        y = u_ref[0, :, 2 * c:3 * c].astype(F32) * (acc[...] + bias_ref[1:2, :] * vin[...].astype(F32))
        o_ref[0] = y.astype(o_ref.dtype)


def hyena_long_conv(u, fwd_bf16, inv_bf16, spec, bias, n, row_block, prev_out):
    bsz = u.shape[0]
    fb = spec.shape[3]
    nf = n // fb
    out_shape = jax.ShapeDtypeStruct((bsz, NTOK, D_HY), BF16)
    if prev_out is None:
        prev_out = jnp.zeros(out_shape.shape, BF16)
    args = [u, fwd_bf16, fwd_bf16, inv_bf16, inv_bf16, spec, bias, prev_out]
    aliases = {7: 0}
    return pl.pallas_call(
        functools.partial(_hyena_body, nf=nf),
        name='hyena',
        grid=(bsz, HY_ORDER, nf),
        in_specs=[pl.BlockSpec((1, n, 3 * D_HY), lambda b, o, f: (b, row_block, 0)),
                  pl.BlockSpec((fb, n), lambda b, o, f: (f, 0)),
                  pl.BlockSpec((fb, n), lambda b, o, f: (nf + f, 0)),
                  pl.BlockSpec((n, fb), lambda b, o, f: (0, f)),
                  pl.BlockSpec((n, fb), lambda b, o, f: (0, nf + f)),
                  pl.BlockSpec((None, None, 3, fb, D_HY), lambda b, o, f: (o, f, 0, 0, 0)),
                  pl.BlockSpec((HY_ORDER, D_HY), lambda b, o, f: (0, 0)),
                  pl.BlockSpec(memory_space=pl.ANY)],
        out_specs=pl.BlockSpec((1, n, D_HY), lambda b, o, f: (b, row_block, 0)),
        out_shape=out_shape,
        scratch_shapes=[pltpu.VMEM((n, D_HY), BF16), pltpu.VMEM((n, D_HY), F32)],
        input_output_aliases=aliases,
        compiler_params=_cparams(("parallel", "arbitrary", "arbitrary"), VMEM_LIMIT_BIG),
    )(*args)


def hyena_mixer_stream(p_hy, conv_w, conv_b, filt, bias):
    u = dwconv_stream(p_hy, conv_w, conv_b, act=False)
    out = None
    for n, row_block in ((SEQ, 0), (CTX_LEN, SEQ // CTX_LEN)):
        fb = min(HY_FB, n)
        fwd, inv = dft_matrices(n)
        hsum, hdiff = hyena_filter_taps(n, *filt)
        spec, fwd_bf16 = hyena_spectrum(fwd, hsum, hdiff, fb)
        out = hyena_long_conv(u, fwd_bf16, inv.astype(BF16), spec, bias, n, row_block, out)
    return out


def _tri(n, kind):
    r = lax.broadcasted_iota(jnp.int32, (n, n), 0)
    c = lax.broadcasted_iota(jnp.int32, (n, n), 1)
    return (c <= r) if kind == 'lower' else (c >= r)


def _ssd_dir(xbc_ref, dt_ref, dtt_ref, bias_r, bias_c, a_r, a_c, st_ref, y_ref, *, d, reverse):
    q = SSD_CHUNK
    nh = SSD_H
    gw = SSD_HPG * SSD_P
    lower = _tri(q, 'lower')
    upper = _tri(q, 'upper')
    lower_f = lower.astype(F32)
    upper_f = upper.astype(F32)
    dt_col = _softplus(dt_ref[0] + bias_r)
    dt_row = _softplus(dtt_ref[0] + bias_c)
    da_col = dt_col * a_r
    da_row = dt_row * a_c
    if not reverse:
        acs_col = jnp.dot(lower_f, da_col, preferred_element_type=F32, precision=HIGHEST)
        acs_row = jnp.dot(da_row, upper_f, preferred_element_type=F32, precision=HIGHEST)
        mask = lower
        edge = q - 1
    else:
        acs_col = jnp.dot(upper_f, da_col, preferred_element_type=F32, precision=HIGHEST)
        acs_row = jnp.dot(da_row, lower_f, preferred_element_type=F32, precision=HIGHEST)
        mask = upper
        edge = 0
    h0 = d * nh
    hh = lax.broadcasted_iota(jnp.int32, (2 * nh, nh * SSD_P), 0)
    cc = lax.broadcasted_iota(jnp.int32, (2 * nh, nh * SSD_P), 1) // SSD_P
    expand = (hh == cc + h0).astype(F32)
    acs_c = jnp.dot(acs_col, expand, preferred_element_type=F32, precision=HIGHEST)
    dt_c = jnp.dot(dt_col, expand, preferred_element_type=F32, precision=HIGHEST)
    total_c = acs_c[edge:edge + 1, :]
    e_in_c = jnp.exp(acs_c)
    w_end_c = jnp.exp(total_c - acs_c) * dt_c
    dec_c = jnp.exp(total_c)
    xs = xbc_ref[0, :, 0:D_SSM]
    xs_f = xs.astype(F32)
    for g in range(SSD_G):
        bm = xbc_ref[0, :, D_SSM + g * SSD_N:D_SSM + (g + 1) * SSD_N]
        cm = xbc_ref[0, :, D_SSM + SSD_G * SSD_N + g * SSD_N:D_SSM + SSD_G * SSD_N + (g + 1) * SSD_N]
        cb = lax.dot_general(cm, bm, (((1,), (1,)), ((), ())), preferred_element_type=F32)
        lws = []
        for k in range(SSD_HPG):
            h = h0 + g * SSD_HPG + k
            seg = acs_col[:, h:h + 1] - acs_row[h:h + 1, :]
            decay = jnp.exp(jnp.where(mask, seg, -jnp.inf))
            lws.append((cb * decay * dt_row[h:h + 1, :]).astype(BF16))
        lw = jnp.concatenate(lws, axis=1)
        xg = xs[:, g * gw:(g + 1) * gw]
        rb = lax.broadcasted_iota(jnp.int32, (SSD_HPG * q, gw), 0) // q
        cbk = lax.broadcasted_iota(jnp.int32, (SSD_HPG * q, gw), 1) // SSD_P
        x_bd = jnp.where(rb == cbk, jnp.concatenate([xg] * SSD_HPG, axis=0), jnp.zeros((), BF16))
        y_in = jnp.dot(lw, x_bd, preferred_element_type=F32)
        st = st_ref[g]
        y_st = jnp.dot(cm, st.astype(BF16), preferred_element_type=F32) * e_in_c[:, g * gw:(g + 1) * gw]
        y_ref[0, :, g * gw:(g + 1) * gw] = y_in + y_st
        xw = (xs_f[:, g * gw:(g + 1) * gw] * w_end_c[:, g * gw:(g + 1) * gw]).astype(BF16)
        upd = lax.dot_general(bm, xw, (((0,), (0,)), ((), ())), preferred_element_type=F32)
        st_ref[g] = st * dec_c[:, g * gw:(g + 1) * gw] + upd


def _ssd_body(xf_ref, dtf_ref, dttf_ref, xb_ref, dtb_ref, dttb_ref, bias_r, bias_c, a_r, a_c,
              yf_ref, yb_ref, stf, stb):
    @pl.when(pl.program_id(1) == 0)
    def _():
        stf[...] = jnp.zeros_like(stf)
        stb[...] = jnp.zeros_like(stb)

    _ssd_dir(xf_ref, dtf_ref, dttf_ref, bias_r[...], bias_c[...], a_r[...], a_c[...], stf, yf_ref,
             d=0, reverse=False)
    _ssd_dir(xb_ref, dtb_ref, dttb_ref, bias_r[...], bias_c[...], a_r[...], a_c[...], stb, yb_ref,
             d=1, reverse=True)


def ssd_scan(xbc, dt, dt_bias, a_log):
    bsz = xbc.shape[0]
    nc = NTOK // SSD_CHUNK
    nlat = SEQ // SSD_CHUNK
    dtt = jnp.swapaxes(dt, 1, 2)
    fwd_chunk = lambda s: (s + nlat) % nc
    bwd_chunk = lambda s: nc - 1 - s
    a = -jnp.exp(a_log.astype(F32)).reshape(1, 2 * SSD_H)
    bias = dt_bias.astype(F32).reshape(1, 2 * SSD_H)
    x_spec = lambda cm: pl.BlockSpec((1, SSD_CHUNK, SSD_XBC), lambda b, s: (b, cm(s), 0))
    dt_spec = lambda cm: pl.BlockSpec((1, SSD_CHUNK, 2 * SSD_H), lambda b, s: (b, cm(s), 0))
    dtt_spec = lambda cm: pl.BlockSpec((1, 2 * SSD_H, SSD_CHUNK), lambda b, s: (b, 0, cm(s)))
    y_spec = lambda cm: pl.BlockSpec((1, SSD_CHUNK, D_SSM), lambda b, s: (b, cm(s), 0))
    row = pl.BlockSpec((1, 2 * SSD_H), lambda b, s: (0, 0))
    col = pl.BlockSpec((2 * SSD_H, 1), lambda b, s: (0, 0))
    y_shape = jax.ShapeDtypeStruct((bsz, NTOK, D_SSM), F32)
    gw = SSD_HPG * SSD_P
    return pl.pallas_call(
        _ssd_body,
        name='ssd_scan',
        grid=(bsz, nc),
        in_specs=[x_spec(fwd_chunk), dt_spec(fwd_chunk), dtt_spec(fwd_chunk),
                  x_spec(bwd_chunk), dt_spec(bwd_chunk), dtt_spec(bwd_chunk),
                  row, col, row, col],
        out_specs=[y_spec(fwd_chunk), y_spec(bwd_chunk)],
        out_shape=[y_shape, y_shape],
        scratch_shapes=[pltpu.VMEM((SSD_G, SSD_N, gw), F32), pltpu.VMEM((SSD_G, SSD_N, gw), F32)],
        compiler_params=_cparams(("parallel", "arbitrary")),
    )(xbc, dt, dtt, xbc, dt, dtt, bias, bias.reshape(-1, 1), a, a.reshape(-1, 1))


def _ssd_merge_body(yf_ref, yb_ref, xbc_ref, z_ref, d_ref, nw_ref, o_ref):
    xs = xbc_ref[0, :, 0:D_SSM].astype(F32)
    z = z_ref[0].astype(F32)
    g = (yf_ref[0] + yb_ref[0] + d_ref[...] * xs) * _silu(z)
    gw = D_SSM // SSD_G
    for k in range(SSD_G):
        gk = g[:, k * gw:(k + 1) * gw]
        ms = jnp.mean(gk * gk, axis=-1, keepdims=True)
        o_ref[0, :, k * gw:(k + 1) * gw] = (gk * lax.rsqrt(ms + NORM_EPS)
                                            * nw_ref[:, k * gw:(k + 1) * gw]).astype(o_ref.dtype)


def ssd_merge(yf, yb, xbc, z, d_skip, norm_w):
    bsz = yf.shape[0]
    tile = lambda w: pl.BlockSpec((1, ROW_TILE, w), lambda i, j: (i, j, 0))
    vec = pl.BlockSpec((1, D_SSM), lambda i, j: (0, 0))
    d_chan = jnp.repeat(d_skip.astype(F32), SSD_P).reshape(1, D_SSM)
    return pl.pallas_call(
        _ssd_merge_body,
        name='ssd_merge',
        grid=(bsz, N_ROW_TILES),
        in_specs=[tile(D_SSM), tile(D_SSM), tile(SSD_XBC), tile(D_SSM), vec, vec],
        out_specs=tile(D_SSM),
        out_shape=jax.ShapeDtypeStruct((bsz, NTOK, D_SSM), BF16),
        compiler_params=_cparams(("parallel", "parallel")),
    )(yf, yb, xbc, z, d_chan, norm_w.reshape(1, D_SSM))


def even_layer_mixer(xs, mods, norm_w, w_in, w_out, hy_conv_w, hy_conv_b, hy_filt, hy_bias,
                     ssd_conv_w, ssd_conv_b, ssd_dt_bias, ssd_a_log, ssd_d, ssd_norm_w):
    bsz = xs.shape[0]
    h = norm_mod(xs, norm_w, mods, 0, 1)
    splits = ((0, HY_IN), (HY_IN, D_SSM), (HY_IN + D_SSM, SSD_XBC), (HY_IN + D_SSM + SSD_XBC, 2 * SSD_H))
    p_hy, z, xbc_raw, dt = mm_split(h.reshape(bsz * NTOK, D_MODEL), w_in.astype(BF16), splits,
                                    (BF16, BF16, BF16, F32))
    to3 = lambda a: a.reshape(bsz, NTOK, a.shape[-1])
    y_hy = hyena_mixer_stream(to3(p_hy), hy_conv_w, hy_conv_b, hy_filt, hy_bias)
    xbc = dwconv_stream(to3(xbc_raw), ssd_conv_w, ssd_conv_b, act=True)
    yf, yb = ssd_scan(xbc, to3(dt), ssd_dt_bias, ssd_a_log)
    s = ssd_merge(yf, yb, xbc, to3(z), ssd_d, ssd_norm_w)
    wo = w_out.astype(BF16)
    return mm_resid([y_hy, s], [wo[:D_HY], wo[D_HY:]], xs, mods, 2, N_ROW_TILES)


GLA_QK = GLA_H * GLA_DK
GLA_V = GLA_H * GLA_DV


def _gla_dir(qkv_ref, lr_ref, gw_ref, gb_ref, st_ref, o_ref, *, d, reverse):
    q = GLA_CHUNK
    tri = _tri(q, 'upper' if reverse else 'lower')
    edge = 0 if reverse else q - 1
    lr = lr_ref[0, :, d * GLA_RANK:(d + 1) * GLA_RANK]
    logit = jnp.dot(lr, gw_ref[d], preferred_element_type=F32, precision=HIGHEST) + gb_ref[d:d + 1, :]
    log_g = -_softplus(-logit) * (1.0 / GLA_GATE_NORM)
    gcum = jnp.dot(tri.astype(F32), log_g, preferred_element_type=F32, precision=HIGHEST)
    total = gcum[edge:edge + 1, :]
    qf = qkv_ref[0, :, 0:GLA_QK].astype(F32)
    kf = qkv_ref[0, :, GLA_QK:2 * GLA_QK].astype(F32)
    v = qkv_ref[0, :, 2 * GLA_QK:2 * GLA_QK + GLA_V]
    qg = (qf * (GLA_DK ** -0.5) * jnp.exp(gcum)).astype(BF16)
    kg = (kf * jnp.exp(-gcum)).astype(BF16)
    kw = (kf * jnp.exp(total - gcum)).astype(BF16)
    rb = lax.broadcasted_iota(jnp.int32, (GLA_H * q, GLA_QK), 0) // q
    cb = lax.broadcasted_iota(jnp.int32, (GLA_H * q, GLA_QK), 1) // GLA_DK
    k_bd = jnp.where(rb == cb, jnp.concatenate([kg] * GLA_H, axis=0), jnp.zeros((), BF16))
    att = lax.dot_general(qg, k_bd, (((1,), (1,)), ((), ())), preferred_element_type=F32)
    i_i = lax.broadcasted_iota(jnp.int32, (q, GLA_H * q), 0)
    j_i = lax.broadcasted_iota(jnp.int32, (q, GLA_H * q), 1) % q
    keep = (j_i >= i_i) if reverse else (j_i <= i_i)
    att = jnp.where(keep, att, 0.0).astype(BF16)
    rv = lax.broadcasted_iota(jnp.int32, (GLA_H * q, GLA_V), 0) // q
    cv = lax.broadcasted_iota(jnp.int32, (GLA_H * q, GLA_V), 1) // GLA_DV
    v_bd = jnp.where(rv == cv, jnp.concatenate([v] * GLA_H, axis=0), jnp.zeros((), BF16))
    st = st_ref[...]
    o_in = jnp.dot(att, v_bd, preferred_element_type=F32)
    o_st = lax.dot_general(qg, st.astype(BF16), (((1,), (1,)), ((), ())), preferred_element_type=F32)
    o_ref[0] = o_in + o_st
    upd = lax.dot_general(v, kw, (((0,), (0,)), ((), ())), preferred_element_type=F32)
    rs = lax.broadcasted_iota(jnp.int32, (GLA_V, GLA_QK), 0) // GLA_DV
    cs = lax.broadcasted_iota(jnp.int32, (GLA_V, GLA_QK), 1) // GLA_DK
    st_ref[...] = st * jnp.exp(total) + jnp.where(rs == cs, upd, 0.0)


def _gla_body(qf_ref, lf_ref, qb_ref, lb_ref, gw_ref, gb_ref, of_ref, ob_ref, stf, stb):
    @pl.when(pl.program_id(1) == 0)
    def _():
        stf[...] = jnp.zeros_like(stf)
        stb[...] = jnp.zeros_like(stb)

    _gla_dir(qf_ref, lf_ref, gw_ref, gb_ref, stf, of_ref, d=0, reverse=False)
    _gla_dir(qb_ref, lb_ref, gw_ref, gb_ref, stb, ob_ref, d=1, reverse=True)


def gla_scan(qkv, lr, gate_w, gate_b):
    bsz = qkv.shape[0]
    nc = NTOK // GLA_CHUNK
    nlat = SEQ // GLA_CHUNK
    fwd_chunk = lambda s: (s + nlat) % nc
    bwd_chunk = lambda s: nc - 1 - s
    q_spec = lambda cm: pl.BlockSpec((1, GLA_CHUNK, qkv.shape[-1]), lambda b, s: (b, cm(s), 0))
    l_spec = lambda cm: pl.BlockSpec((1, GLA_CHUNK, 2 * GLA_RANK), lambda b, s: (b, cm(s), 0))
    o_spec = lambda cm: pl.BlockSpec((1, GLA_CHUNK, GLA_V), lambda b, s: (b, cm(s), 0))
    o_shape = jax.ShapeDtypeStruct((bsz, NTOK, GLA_V), F32)
    return pl.pallas_call(
        _gla_body,
        name='gla_scan',
        grid=(bsz, nc),
        in_specs=[q_spec(fwd_chunk), l_spec(fwd_chunk), q_spec(bwd_chunk), l_spec(bwd_chunk),
                  pl.BlockSpec((2, GLA_RANK, GLA_QK), lambda b, s: (0, 0, 0)),
                  pl.BlockSpec((2, GLA_QK), lambda b, s: (0, 0))],
        out_specs=[o_spec(fwd_chunk), o_spec(bwd_chunk)],
        out_shape=[o_shape, o_shape],
        scratch_shapes=[pltpu.VMEM((GLA_V, GLA_QK), F32), pltpu.VMEM((GLA_V, GLA_QK), F32)],
        compiler_params=_cparams(("parallel", "arbitrary")),
    )(qkv, lr, qkv, lr, gate_w.astype(F32), gate_b.astype(F32))


def _gla_merge_body(of_ref, ob_ref, r_ref, nw_ref, o_ref):
    o = of_ref[0] + ob_ref[0]
    r = r_ref[0].astype(F32)
    for h in range(GLA_H):
        sl = slice(h * GLA_DV, (h + 1) * GLA_DV)
        oh = o[:, sl]
        ms = jnp.mean(oh * oh, axis=-1, keepdims=True)
        o_ref[0, :, sl] = (oh * lax.rsqrt(ms + NORM_EPS) * nw_ref[:, sl] * _silu(r[:, sl])).astype(o_ref.dtype)


def gla_merge_stream(of, ob, r, norm_w):
    bsz = of.shape[0]
    tile = pl.BlockSpec((1, ROW_TILE, GLA_V), lambda i, j: (i, j, 0))
    return pl.pallas_call(
        _gla_merge_body,
        name='gla_merge',
        grid=(bsz, N_LAT_TILES),
        in_specs=[tile, tile, tile, pl.BlockSpec((1, GLA_V), lambda i, j: (0, 0))],
        out_specs=tile,
        out_shape=jax.ShapeDtypeStruct((bsz, SEQ, GLA_V), BF16),
        compiler_params=_cparams(("parallel", "parallel")),
    )(of, ob, r, norm_w.reshape(1, GLA_V))


RG_TILE = 8


def _gelu_tanh(x):
    return 0.5 * x * (1.0 + jnp.tanh(math.sqrt(2.0 / math.pi) * (x + 0.044715 * x * x * x)))


def _rg_scan_block(a_s, x_s, h_s, base, carry, reverse):
    n_tiles = ROW_TILE // RG_TILE
    row = lax.broadcasted_iota(jnp.int32, (RG_TILE, D_RG), 0)

    def tile_step(i, h_prev):
        t = (n_tiles - 1 - i) if reverse else i
        r0 = pl.multiple_of(t * RG_TILE, RG_TILE)
        a = a_s[pl.ds(r0, RG_TILE), :]
        x = x_s[pl.ds(r0, RG_TILE), :]
        for s in (1, 2, 4):
            if reverse:
                ok = row < RG_TILE - s
                shift = RG_TILE - s
            else:
                ok = row >= s
                shift = s
            a_sh = jnp.where(ok, pltpu.roll(a, shift, 0), 1.0)
            x_sh = jnp.where(ok, pltpu.roll(x, shift, 0), 0.0)
            x = a * x_sh + x
            a = a * a_sh
        h = x + a * h_prev
        h_s[pl.ds(base + r0, RG_TILE), :] = h
        edge = 0 if reverse else RG_TILE - 1
        return jnp.broadcast_to(h[edge:edge + 1, :], (RG_TILE, D_RG))

    return lax.fori_loop(0, n_tiles, tile_step, carry)


def _rglru_body(u_ref, g_ref, w_ref, b_ref, c_ref, o_ref, hf_s, a_s, x_s, hb_s):
    n_blocks = NTOK // ROW_TILE
    fwd_order = list(range(N_LAT_TILES, n_blocks)) + list(range(N_LAT_TILES))
    bwd_order = list(range(n_blocks - 1, N_LAT_TILES - 1, -1)) + list(range(N_LAT_TILES - 1, -1, -1))

    def gates(blk, d):
        ub = u_ref[0, blk * ROW_TILE:(blk + 1) * ROW_TILE, :]
        z = jnp.dot(ub, w_ref[:, 2 * d * D_RG:2 * (d + 1) * D_RG], preferred_element_type=F32)
        z = z + b_ref[:, 2 * d * D_RG:2 * (d + 1) * D_RG]
        r = 1.0 / (1.0 + jnp.exp(-z[:, :D_RG]))
        i = 1.0 / (1.0 + jnp.exp(-z[:, D_RG:]))
        a = jnp.exp(c_ref[d:d + 1, :] * r)
        a_s[...] = a
        x_s[...] = jnp.sqrt(1.0 - a * a) * i * ub.astype(F32)

    carry = jnp.zeros((RG_TILE, D_RG), F32)
    for blk in fwd_order:
        gates(blk, 0)
        carry = _rg_scan_block(a_s, x_s, hf_s, blk * ROW_TILE, carry, reverse=False)
    carry = jnp.zeros((RG_TILE, D_RG), F32)
    for blk in bwd_order:
        gates(blk, 1)
        carry = _rg_scan_block(a_s, x_s, hb_s, 0, carry, reverse=True)
        rows = slice(blk * ROW_TILE, (blk + 1) * ROW_TILE)
        gate = g_ref[0, rows, :].astype(F32)
        o_ref[0, rows, :] = ((hf_s[rows, :] + hb_s[...]) * _gelu_tanh(gate)).astype(o_ref.dtype)


def rglru_stream(u, gate, w_a, b_a, w_x, b_x, lam):
    bsz = u.shape[0]
    eye = jnp.eye(RG_BLOCKS, dtype=F32)
    dense = lambda w: jnp.einsum('nio,nm->nimo', w, eye).reshape(D_RG, D_RG)
    w_cat = jnp.concatenate([dense(w_a[0]), dense(w_x[0]), dense(w_a[1]), dense(w_x[1])], axis=1).astype(BF16)
    b_cat = jnp.concatenate([b_a[0], b_x[0], b_a[1], b_x[1]]).astype(F32).reshape(1, 4 * D_RG)
    c = -RG_C * jax.nn.softplus(-lam.astype(F32))
    seq = pl.BlockSpec((1, NTOK, D_RG), lambda i: (i, 0, 0))
    return pl.pallas_call(
        _rglru_body,
        name='rglru',
        grid=(bsz,),
        in_specs=[seq, seq,
                  pl.BlockSpec((D_RG, 4 * D_RG), lambda i: (0, 0)),
                  pl.BlockSpec((1, 4 * D_RG), lambda i: (0, 0)),
                  pl.BlockSpec((2, D_RG), lambda i: (0, 0))],
        out_specs=seq,
        out_shape=jax.ShapeDtypeStruct((bsz, NTOK, D_RG), BF16),
        scratch_shapes=[pltpu.VMEM((NTOK, D_RG), F32), pltpu.VMEM((ROW_TILE, D_RG), F32),
                        pltpu.VMEM((ROW_TILE, D_RG), F32), pltpu.VMEM((ROW_TILE, D_RG), F32)],
        compiler_params=_cparams(("parallel",)),
    )(u, gate, w_cat, b_cat, c)


def odd_layer_mixer_pallas(xs, mods, norm_w, w_in, w_out, gla_args, rg_args):
    bsz = xs.shape[0]
    gate_w, gate_b, gla_norm_w = gla_args
    rg_conv_w, rg_conv_b, w_a, b_a, w_x, b_x, lam = rg_args
    h = norm_mod(xs, norm_w, mods, 0, 1)
    h = jnp.concatenate([to_col_major(h[:, :SEQ]), h[:, SEQ:]], axis=1)
    nqk, nv = GLA_QK, GLA_V
    r0 = 2 * nqk + nv + 2 * GLA_RANK
    w = jnp.concatenate([w_in[:, :2 * nqk + nv], w_in[:, r0:r0 + nv], w_in[:, GLA_IN:],
                         w_in[:, 2 * nqk + nv:r0]], axis=1).astype(BF16)
    qkv_w = 2 * nqk + nv
    splits = ((0, qkv_w), (qkv_w, nv), (qkv_w + nv, D_RG), (qkv_w + nv + D_RG, D_RG),
              (qkv_w + nv + 2 * D_RG, 2 * GLA_RANK))
    qkv, r, u_raw, gate, lr = mm_split(h.reshape(bsz * NTOK, D_MODEL), w, splits, (BF16, BF16, BF16, BF16, F32))
    to3 = lambda a: a.reshape(bsz, NTOK, a.shape[-1])
    of, ob = gla_scan(to3(qkv), to3(lr), gate_w, gate_b.reshape(2, GLA_QK))
    a_l = gla_merge_stream(of, ob, to3(r), gla_norm_w)
    u = dwconv_stream(to3(u_raw), rg_conv_w, rg_conv_b, act=False)
    r_l = rglru_stream(u, to3(gate), w_a, b_a, w_x, b_x, lam)[:, :SEQ]
    wo = w_out.astype(BF16)
    return mm_resid([from_col_major(a_l), from_col_major(r_l)], [wo[:GLA_V], wo[GLA_V:]], xs, mods, 2, N_LAT_TILES)


def rms_norm(x, w):
    xf = x.astype(F32)
    y = xf * lax.rsqrt(jnp.mean(jnp.square(xf), axis=-1, keepdims=True) + NORM_EPS)
    return y.astype(x.dtype) * w


def dwconv(x, w, b):
    y = lax.conv_general_dilated(x, w[:, None, :].astype(x.dtype), window_strides=(1,), padding='SAME',
                                 dimension_numbers=('NWC', 'WIO', 'NWC'), feature_group_count=x.shape[-1])
    return y + b.astype(x.dtype)


def maybe_flip(a, rev):
    return jnp.flip(a, axis=1) if rev else a


def to_col_major(x):
    b, n, d = x.shape
    rows = n // GRID_W
    return x.reshape(b, rows, GRID_W, d).transpose(0, 2, 1, 3).reshape(b, n, d)


def from_col_major(x):
    b, n, d = x.shape
    rows = n // GRID_W
    return x.reshape(b, GRID_W, rows, d).transpose(0, 2, 1, 3).reshape(b, n, d)


def gla_inputs(p, gate_w, gate_b):
    b, n, _ = p.shape
    nqk, nv = GLA_H * GLA_DK, GLA_H * GLA_DV
    q = p[..., :nqk].reshape(b, n, GLA_H, GLA_DK) * GLA_DK ** -0.5
    k = p[..., nqk:2 * nqk].reshape(b, n, GLA_H, GLA_DK)
    v = p[..., 2 * nqk:2 * nqk + nv].reshape(b, n, GLA_H, GLA_DV)
    lr = p[..., 2 * nqk + nv:2 * nqk + nv + 2 * GLA_RANK].reshape(b, n, 2, GLA_RANK)
    r = p[..., 2 * nqk + nv + 2 * GLA_RANK:]
    logit = jnp.einsum('bler,erk->blek', lr, gate_w) + gate_b
    log_g = (jax.nn.log_sigmoid(logit.astype(F32)) / GLA_GATE_NORM).reshape(b, n, 2, GLA_H, GLA_DK)
    return q, k, v, log_g, r


def gla_states(k, v, log_g, s0):
    b, n = k.shape[:2]
    nc = n // GLA_CHUNK
    kc = k.reshape(b, nc, GLA_CHUNK, GLA_H, GLA_DK)
    vc = v.reshape(b, nc, GLA_CHUNK, GLA_H, GLA_DV)
    gcum = jnp.cumsum(log_g.reshape(b, nc, GLA_CHUNK, GLA_H, GLA_DK), axis=2)
    states = jnp.einsum('bcqhd,bcqhv->bchdv', kc * jnp.exp(gcum[:, :, -1:] - gcum), vc)
    chunk_decay = jnp.exp(gcum[:, :, -1])

    def step(s, inp):
        st, dcy = inp
        return dcy[..., None] * s + st, s

    s_fin, s_prev = lax.scan(step, s0, (jnp.moveaxis(states, 1, 0), jnp.moveaxis(chunk_decay, 1, 0)))
    return jnp.moveaxis(s_prev, 0, 1), s_fin


def gla_output(q, k, v, log_g, s_prev):
    b, n = q.shape[:2]
    nc = n // GLA_CHUNK
    qc = q.reshape(b, nc, GLA_CHUNK, GLA_H, GLA_DK)
    kc = k.reshape(b, nc, GLA_CHUNK, GLA_H, GLA_DK)
    vc = v.reshape(b, nc, GLA_CHUNK, GLA_H, GLA_DV)
    gcum = jnp.cumsum(log_g.reshape(b, nc, GLA_CHUNK, GLA_H, GLA_DK), axis=2)
    qg = qc * jnp.exp(gcum)
    kg = kc * jnp.exp(-gcum)
    mask = jnp.tril(jnp.ones((GLA_CHUNK, GLA_CHUNK), bool))
    att = jnp.where(mask, jnp.einsum('bcihd,bcjhd->bchij', qg, kg), 0.0)
    o = jnp.einsum('bchij,bcjhv->bcihv', att, vc) + jnp.einsum('bcihd,bchdv->bcihv', qg, s_prev)
    return o.reshape(b, n, GLA_H, GLA_DV)


def gla_merge(os_, r, norm_w):
    b, n = r.shape[:2]
    o = rms_norm(os_[0] + os_[1], norm_w.reshape(GLA_H, GLA_DV))
    return o.reshape(b, n, GLA_H * GLA_DV) * jax.nn.silu(r)


def gla_mixer(p_c, p_l, gate_w, gate_b, norm_w):
    q_c, k_c, v_c, g_c, r_c = gla_inputs(p_c, gate_w, gate_b)
    q_l, k_l, v_l, g_l, r_l = gla_inputs(p_l, gate_w, gate_b)
    s0 = jnp.zeros((p_c.shape[0], GLA_H, GLA_DK, GLA_DV), F32)
    os_l = []
    for d, rev in enumerate((False, True)):
        f = functools.partial(maybe_flip, rev=rev)
        _, sf_c = gla_states(f(k_c), f(v_c), f(g_c[:, :, d]), s0)
        sp_l, _ = gla_states(f(k_l), f(v_l), f(g_l[:, :, d]), sf_c)
        os_l.append(f(gla_output(f(q_l), f(k_l), f(v_l), f(g_l[:, :, d]), sp_l)))
    return gla_merge(os_l, r_l, norm_w)


def rglru_inputs(p, conv_w, conv_b, w_a, b_a, w_x, b_x, lam):
    b, n, _ = p.shape
    u = dwconv(p[..., :D_RG], conv_w, conv_b)
    ub = u.reshape(b, n, RG_BLOCKS, RG_BW)
    r = jax.nn.sigmoid((jnp.einsum('blni,enio->bleno', ub, w_a).reshape(b, n, 2, D_RG) + b_a).astype(F32))
    i = jax.nn.sigmoid((jnp.einsum('blni,enio->bleno', ub, w_x).reshape(b, n, 2, D_RG) + b_x).astype(F32))
    log_a = -RG_C * jax.nn.softplus(-lam.astype(F32)) * r
    x_in = jnp.sqrt(-jnp.expm1(2.0 * log_a)) * i * u[:, :, None, :].astype(F32)
    return p[..., D_RG:], jnp.exp(log_a), x_in


def lru_scan(a, u, h0):
    u = u.at[:, 0].add(a[:, 0] * h0)

    def combine(lhs, rhs):
        a1, b1 = lhs
        a2, b2 = rhs
        return a1 * a2, a2 * b1 + b2

    return lax.associative_scan(combine, (a, u), axis=1)[1]


def rglru_mixer(p_c, p_l, conv_w, conv_b, w_a, b_a, w_x, b_x, lam):
    gb_c, a_c, u_c = rglru_inputs(p_c, conv_w, conv_b, w_a, b_a, w_x, b_x, lam)
    gb_l, a_l, u_l = rglru_inputs(p_l, conv_w, conv_b, w_a, b_a, w_x, b_x, lam)
    h0 = jnp.zeros((p_c.shape[0], D_RG), F32)
    hs_l = []
    for d, rev in enumerate((False, True)):
        f = functools.partial(maybe_flip, rev=rev)
        h_c = f(lru_scan(f(a_c[:, :, d]), f(u_c[:, :, d]), h0))
        h_end = h_c[:, 0] if rev else h_c[:, -1]
        hs_l.append(f(lru_scan(f(a_l[:, :, d]), f(u_l[:, :, d]), h_end)))
    return (hs_l[0] + hs_l[1]) * jax.nn.gelu(gb_l.astype(F32))


def odd_layer_mixer(xs, mods, norm_w, w_in, w_out, gla_args, rg_args):
    bsz = xs.shape[0]
    h = norm_mod(xs, norm_w, mods, 0, 1)
    h = jnp.concatenate([to_col_major(h[:, :SEQ]), h[:, SEQ:]], axis=1)
    p = pmatmul(h.reshape(bsz * NTOK, D_MODEL), w_in.astype(BF16)).reshape(bsz, NTOK, OD_IN)
    p_l, p_c = p[:, :SEQ], p[:, SEQ:]
    a_l = gla_mixer(p_c[..., :GLA_IN], p_l[..., :GLA_IN], *gla_args)
    r_l = rglru_mixer(p_c[..., GLA_IN:], p_l[..., GLA_IN:], *rg_args)
    mix = from_col_major(jnp.concatenate([a_l, r_l], axis=-1)).astype(BF16)
    return mm_resid([mix], [w_out.astype(BF16)], xs, mods, 2, N_LAT_TILES)


def kernel(x, c, ctx, c_ctx, ada_w, ada_b, norm1_w, norm2_w, ev_w_in, ev_w_out, hy_conv_w, hy_conv_b, hy_fw0, hy_fb0, hy_fw1, hy_fb1, hy_fw2, hy_fb2, hy_fw3, hy_freq, hy_bias, ssd_conv_w, ssd_conv_b, ssd_dt_bias, ssd_a_log, ssd_d, ssd_norm_w, od_w_in, od_w_out, gla_gate_w, gla_gate_b, gla_norm_w, rg_conv_w, rg_conv_b, rg_w_a, rg_b_a, rg_w_x, rg_b_x, rg_lambda, router_w, router_b, moe_w_gate, moe_w_up, moe_w_down, sh_w_gate, sh_w_up, sh_w_down, final_norm_w):
    xs = jnp.concatenate([x, ctx], axis=1)
    for i in range(DEPTH):
        last = i == DEPTH - 1
        j = i // 2
        mods = adaln_table(c, c_ctx, ada_w[i], ada_b[i])
        if i % 2 == 0:
            hy_filt = (hy_fw0[j], hy_fb0[j], hy_fw1[j], hy_fb1[j], hy_fw2[j], hy_fb2[j], hy_fw3[j], hy_freq[j])
            xs = even_layer_mixer(xs, mods, norm1_w[i], ev_w_in[j], ev_w_out[j], hy_conv_w[j], hy_conv_b[j],
                                  hy_filt, hy_bias[j], ssd_conv_w[j], ssd_conv_b[j], ssd_dt_bias[j],
                                  ssd_a_log[j], ssd_d[j], ssd_norm_w[j])
        else:
            gla_args = (gla_gate_w[j], gla_gate_b[j], gla_norm_w[j])
            rg_args = (rg_conv_w[j], rg_conv_b[j], rg_w_a[j], rg_b_a[j], rg_w_x[j], rg_b_x[j], rg_lambda[j])
            xs = odd_layer_mixer_pallas(xs, mods, norm1_w[i], od_w_in[j], od_w_out[j], gla_args, rg_args)
        n_tiles = N_LAT_TILES if last else N_ROW_TILES
        xs = moe_layer(xs, norm2_w[i], mods, router_w[i], router_b[i], moe_w_gate, moe_w_up, moe_w_down,
                       sh_w_gate[i], sh_w_up[i], sh_w_down[i], n_tiles, i)
    return final_norm(xs, final_norm_w)
```

```python
import functools
import math

import jax
import jax.numpy as jnp
from jax import lax
from jax.experimental import pallas as pl
from jax.experimental.pallas import tpu as pltpu

D_MODEL = 1024
BATCH = 16
SEQ = 2048
DEPTH = 2

CTX_LEN = 256
GRID_W = 64
NORM_EPS = 1e-6

D_HY = D_MODEL // 2
HY_ORDER = 2
HY_SHORT = 3
HY_BANDS = 16
HY_EMB = 1 + 2 * HY_BANDS
HY_FF = 64
HY_DECAY_PCT_LO = 0.3
HY_DECAY_PCT_HI = 1.5
HY_DECAY_TARGET = 1e-2
HY_IN = 3 * D_HY

D_SSM = D_MODEL // 2
SSD_P = 64
SSD_H = D_SSM // SSD_P
SSD_G = 2
SSD_HPG = SSD_H // SSD_G
SSD_N = 128
SSD_CONV = 4
SSD_CHUNK = 128
SSD_XBC = D_SSM + 2 * SSD_G * SSD_N
SSD_IN = D_SSM + SSD_XBC + 2 * SSD_H
EV_IN = HY_IN + SSD_IN
EV_MIX = D_HY + D_SSM

GLA_H = 4
GLA_DV = (D_MODEL // 2) // GLA_H
GLA_DK = GLA_DV // 2
GLA_RANK = 16
GLA_GATE_NORM = 16.0
GLA_CHUNK = 64
GLA_IN = 2 * GLA_H * GLA_DK + 2 * GLA_H * GLA_DV + 2 * GLA_RANK

D_RG = D_MODEL // 2
RG_BLOCKS = 8
RG_BW = D_RG // RG_BLOCKS
RG_CONV = 4
RG_C = 8.0
RG_IN = 2 * D_RG
OD_IN = GLA_IN + RG_IN
OD_MIX = GLA_H * GLA_DV + D_RG

MOE_EXPERTS = 64
MOE_TOPK = 8
MOE_D_EXPERT = 256
MOE_D_SHARED = 256
MOE_SCALE = 2.5
MOE_BLOCK = 512

F32 = jnp.float32
BF16 = jnp.bfloat16
HIGHEST = lax.Precision.HIGHEST

NTOK = SEQ + CTX_LEN
ROW_TILE = 256
N_ROW_TILES = NTOK // ROW_TILE
N_LAT_TILES = SEQ // ROW_TILE

VMEM_LIMIT = 48 * 1024 * 1024
VMEM_LIMIT_BIG = 56 * 1024 * 1024


def _cparams(sem, limit=VMEM_LIMIT):
    return pltpu.CompilerParams(dimension_semantics=sem, vmem_limit_bytes=limit)


def _pick_tile(n, pref):
    t = min(n, pref)
    while n % t:
        t //= 2
    return t


def _silu(x):
    return x / (1.0 + jnp.exp(-x))


def _softplus(x):
    return jnp.maximum(x, 0.0) + jnp.log(1.0 + jnp.exp(-jnp.abs(x)))


def _mm_bf16_body(a_ref, w_ref, o_ref):
    o_ref[...] = jnp.dot(a_ref[...].astype(BF16), w_ref[...].astype(BF16),
                         preferred_element_type=F32).astype(o_ref.dtype)


def _mm_f32_body(a_ref, w_ref, o_ref):
    o_ref[...] = jnp.dot(a_ref[...], w_ref[...], preferred_element_type=F32,
                         precision=HIGHEST).astype(o_ref.dtype)


def pmatmul(a, w, *, exact=False, out_dtype=F32, tm=512, tn=None):
    m, k = a.shape
    n = w.shape[1]
    tm = _pick_tile(m, tm)
    tn = n if tn is None else _pick_tile(n, tn)
    body = _mm_f32_body if exact else _mm_bf16_body
    return pl.pallas_call(
        body,
        name='mm',
        grid=(m // tm, n // tn),
        in_specs=[pl.BlockSpec((tm, k), lambda i, j: (i, 0)),
                  pl.BlockSpec((k, tn), lambda i, j: (0, j))],
        out_specs=pl.BlockSpec((tm, tn), lambda i, j: (i, j)),
        out_shape=jax.ShapeDtypeStruct((m, n), out_dtype),
        compiler_params=_cparams(("parallel", "parallel")),
    )(a, w)


def _mm_split_body(a_ref, w_ref, *o_refs, splits):
    a = a_ref[...]
    for o_ref, (start, width) in zip(o_refs, splits):
        o_ref[...] = jnp.dot(a, w_ref[:, start:start + width],
                             preferred_element_type=F32).astype(o_ref.dtype)


def mm_split(a, w, splits, dtypes, tm=512):
    m, k = a.shape
    n = w.shape[1]
    tm = _pick_tile(m, tm)
    return pl.pallas_call(
        functools.partial(_mm_split_body, splits=tuple(splits)),
        name='mm_split',
        grid=(m // tm,),
        in_specs=[pl.BlockSpec((tm, k), lambda i: (i, 0)),
                  pl.BlockSpec((k, n), lambda i: (0, 0))],
        out_specs=[pl.BlockSpec((tm, wd), lambda i: (i, 0)) for _, wd in splits],
        out_shape=[jax.ShapeDtypeStruct((m, wd), dt) for (_, wd), dt in zip(splits, dtypes)],
        compiler_params=_cparams(("parallel",)),
    )(a, w)


def _mm_resid_body(*refs, n_pairs):
    a_refs = refs[:n_pairs]
    w_refs = refs[n_pairs:2 * n_pairs]
    x_ref, g_ref, o_ref = refs[2 * n_pairs:]
    acc = jnp.dot(a_refs[0][0], w_refs[0][...], preferred_element_type=F32)
    for a_ref, w_ref in zip(a_refs[1:], w_refs[1:]):
        acc = acc + jnp.dot(a_ref[0], w_ref[...], preferred_element_type=F32)
    o_ref[0] = x_ref[0] + g_ref[...] * acc


def mm_resid(a_list, w_list, xs, mods, gate_idx, n_tiles):
    b, nt, d = xs.shape
    n_pairs = len(a_list)
    in_specs = [pl.BlockSpec((1, ROW_TILE, a.shape[-1]), lambda i, j: (i, j, 0)) for a in a_list]
    in_specs += [pl.BlockSpec(w.shape, lambda i, j: (0, 0)) for w in w_list]
    in_specs += [pl.BlockSpec((1, ROW_TILE, d), lambda i, j: (i, j, 0)),
                 _mod_spec(gate_idx, d)]
    return pl.pallas_call(
        functools.partial(_mm_resid_body, n_pairs=n_pairs),
        name='mm_resid',
        grid=(b, n_tiles),
        in_specs=in_specs,
        out_specs=pl.BlockSpec((1, ROW_TILE, d), lambda i, j: (i, j, 0)),
        out_shape=jax.ShapeDtypeStruct(xs.shape, F32),
        input_output_aliases={2 * n_pairs: 0},
        compiler_params=_cparams(("parallel", "parallel")),
    )(*a_list, *w_list, xs, mods)


def _mod_spec(idx, d, b0=0):
    return pl.BlockSpec((None, None, None, 1, d), lambda i, j: (i + b0, 1 - j // N_LAT_TILES, idx, 0, 0))


def adaln_table(c, c_ctx, w, b):
    cv = jax.nn.silu(jnp.concatenate([c, c_ctx[None, :]], axis=0))
    cv = jnp.pad(cv, ((0, 24 - cv.shape[0]), (0, 0)))
    m = pmatmul(cv, w, exact=True, tn=1536)[:BATCH + 1] + b
    per_sample = m[:BATCH]
    ctx_row = jnp.broadcast_to(m[BATCH][None, :], per_sample.shape)
    return jnp.stack([ctx_row, per_sample], axis=1).reshape(BATCH, 2, 6, 1, D_MODEL)


def _norm_mod(x, w, shift, scale):
    ms = jnp.mean(x * x, axis=-1, keepdims=True)
    return (x * lax.rsqrt(ms + NORM_EPS) * w) * (1.0 + scale) + shift


def _norm_mod_body(x_ref, w_ref, sh_ref, sc_ref, o_ref):
    o_ref[0] = _norm_mod(x_ref[0], w_ref[...], sh_ref[...], sc_ref[...]).astype(o_ref.dtype)


def norm_mod(xs, w, mods, shift_idx, scale_idx):
    b, nt, d = xs.shape
    return pl.pallas_call(
        _norm_mod_body,
        name='norm_mod',
        grid=(b, nt // ROW_TILE),
        in_specs=[pl.BlockSpec((1, ROW_TILE, d), lambda i, j: (i, j, 0)),
                  pl.BlockSpec((1, d), lambda i, j: (0, 0)),
                  _mod_spec(shift_idx, d), _mod_spec(scale_idx, d)],
        out_specs=pl.BlockSpec((1, ROW_TILE, d), lambda i, j: (i, j, 0)),
        out_shape=jax.ShapeDtypeStruct(xs.shape, BF16),
        compiler_params=_cparams(("parallel", "parallel")),
    )(xs, w.reshape(1, d), mods, mods)


def _final_norm_body(x_ref, w_ref, o_ref):
    x = x_ref[0]
    ms = jnp.mean(x * x, axis=-1, keepdims=True)
    o_ref[0] = x * lax.rsqrt(ms + NORM_EPS) * w_ref[...]


def final_norm(xs, w):
    b, _, d = xs.shape
    return pl.pallas_call(
        _final_norm_body,
        name='final_norm',
        grid=(b, N_LAT_TILES),
        in_specs=[pl.BlockSpec((1, ROW_TILE, d), lambda i, j: (i, j, 0)),
                  pl.BlockSpec((1, d), lambda i, j: (0, 0))],
        out_specs=pl.BlockSpec((1, ROW_TILE, d), lambda i, j: (i, j, 0)),
        out_shape=jax.ShapeDtypeStruct((b, SEQ, d), F32),
        compiler_params=_cparams(("parallel", "parallel")),
    )(xs, w.reshape(1, d))


def _route_body(x_ref, w_ref, sh_ref, sc_ref, rw_ref, rb_ref, h_ref, idx_ref, wsel_ref, rank_ref, cnt_ref,
                *, group_size):
    first = (pl.program_id(0) % group_size == 0) & (pl.program_id(1) == 0)

    @pl.when(first)
    def _():
        cnt_ref[...] = jnp.zeros_like(cnt_ref)

    h = _norm_mod(x_ref[0], w_ref[...], sh_ref[...], sc_ref[...])
    h_ref[0] = h.astype(h_ref.dtype)
    logits = jnp.dot(h, rw_ref[...], preferred_element_type=F32, precision=HIGHEST)
    scores = 1.0 / (1.0 + jnp.exp(-logits))
    tm, ne = scores.shape
    lane = lax.broadcasted_iota(jnp.int32, (tm, ne), 1).astype(F32)
    slot = lax.broadcasted_iota(jnp.int32, (tm, MOE_TOPK), 1)
    sel = scores + rb_ref[...]
    picked = jnp.zeros((tm, ne), F32)
    hits = []
    idx_out = jnp.zeros((tm, MOE_TOPK), F32)
    w_out = jnp.zeros((tm, MOE_TOPK), F32)
    for k in range(MOE_TOPK):
        m = jnp.max(sel, axis=-1, keepdims=True)
        ik = jnp.min(jnp.where(sel == m, lane, float(ne)), axis=-1, keepdims=True)
        hit = lane == ik
        wk = jnp.sum(jnp.where(hit, scores, 0.0), axis=-1, keepdims=True)
        sel = jnp.where(hit, -jnp.inf, sel)
        picked = picked + hit.astype(F32)
        hits.append(hit)
        idx_out = jnp.where(slot == k, ik, idx_out)
        w_out = jnp.where(slot == k, wk, w_out)
    wsum = jnp.sum(w_out, axis=-1, keepdims=True)
    wsel_ref[0] = w_out / wsum * MOE_SCALE
    idx_ref[0] = idx_out.astype(jnp.int32)
    r_i = lax.broadcasted_iota(jnp.int32, (tm, tm), 0)
    c_i = lax.broadcasted_iota(jnp.int32, (tm, tm), 1)
    strict_lower = (c_i < r_i).astype(BF16)
    before = jnp.dot(strict_lower, picked.astype(BF16), preferred_element_type=F32) + cnt_ref[...]
    rank_out = jnp.zeros((tm, MOE_TOPK), F32)
    for k in range(MOE_TOPK):
        rk = jnp.sum(jnp.where(hits[k], before, 0.0), axis=-1, keepdims=True)
        rank_out = jnp.where(slot == k, rk, rank_out)
    rank_ref[0] = rank_out.astype(jnp.int32)
    cnt_ref[...] = cnt_ref[...] + jnp.sum(picked, axis=0, keepdims=True)


def route(xs, w, mods, router_w, router_b, n_tiles, group_size):
    b, _, d = xs.shape
    rows = n_tiles * ROW_TILE
    small = lambda dt: jax.ShapeDtypeStruct((b, rows, MOE_TOPK), dt)
    small_spec = pl.BlockSpec((1, ROW_TILE, MOE_TOPK), lambda i, j: (i, j, 0))
    return pl.pallas_call(
        functools.partial(_route_body, group_size=group_size),
        name='route',
        grid=(b, n_tiles),
        in_specs=[pl.BlockSpec((1, ROW_TILE, d), lambda i, j: (i, j, 0)),
                  pl.BlockSpec((1, d), lambda i, j: (0, 0)),
                  _mod_spec(3, d), _mod_spec(4, d),
                  pl.BlockSpec((d, MOE_EXPERTS), lambda i, j: (0, 0)),
                  pl.BlockSpec((1, MOE_EXPERTS), lambda i, j: (0, 0))],
        out_specs=[pl.BlockSpec((1, ROW_TILE, d), lambda i, j: (i, j, 0)),
                   small_spec, small_spec, small_spec,
                   pl.BlockSpec((None, 1, MOE_EXPERTS), lambda i, j: (i // group_size, 0, 0))],
        out_shape=[jax.ShapeDtypeStruct((b, rows, d), BF16), small(jnp.int32), small(F32), small(jnp.int32),
                   jax.ShapeDtypeStruct((b // group_size, 1, MOE_EXPERTS), F32)],
        compiler_params=_cparams(("arbitrary", "arbitrary")),
    )(xs, w.reshape(1, d), mods, mods, router_w, router_b.reshape(1, MOE_EXPERTS))


def _swiglu(x, wg, wu, wd):
    g = jnp.dot(x, wg, preferred_element_type=F32)
    u = jnp.dot(x, wu, preferred_element_type=F32)
    h = (_silu(g) * u).astype(BF16)
    return jnp.dot(h, wd, preferred_element_type=F32)


def _expert_body(be_ref, nu_ref, x_ref, wg_ref, wu_ref, wd_ref, o_ref, wg_s, wu_s, wd_s):
    i = pl.program_id(0)
    used = i < nu_ref[0]

    @pl.when(used & ((i == 0) | (be_ref[i] != be_ref[jnp.maximum(i - 1, 0)])))
    def _():
        wg_s[...] = wg_ref[0].astype(BF16)
        wu_s[...] = wu_ref[0].astype(BF16)
        wd_s[...] = wd_ref[0].astype(BF16)

    @pl.when(used)
    def _():
        o_ref[...] = _swiglu(x_ref[...], wg_s[...], wu_s[...], wd_s[...]).astype(o_ref.dtype)

    @pl.when(jnp.logical_not(used))
    def _():
        o_ref[...] = jnp.zeros_like(o_ref)


def moe_experts(x_rows, block_e, n_used, wg, wu, wd, layer):
    rows, d = x_rows.shape
    n_blocks = rows // MOE_BLOCK
    f = wg.shape[-1]
    grid_spec = pltpu.PrefetchScalarGridSpec(
        num_scalar_prefetch=2,
        grid=(n_blocks,),
        in_specs=[
            pl.BlockSpec((MOE_BLOCK, d), lambda i, be, nu: (i, 0)),
            pl.BlockSpec((None, 1, d, f), lambda i, be, nu: (layer, be[i], 0, 0)),
            pl.BlockSpec((None, 1, d, f), lambda i, be, nu: (layer, be[i], 0, 0)),
            pl.BlockSpec((None, 1, f, d), lambda i, be, nu: (layer, be[i], 0, 0)),
        ],
        out_specs=pl.BlockSpec((MOE_BLOCK, d), lambda i, be, nu: (i, 0)),
        scratch_shapes=[pltpu.VMEM((d, f), BF16), pltpu.VMEM((d, f), BF16), pltpu.VMEM((f, d), BF16)],
    )
    return pl.pallas_call(
        _expert_body,
        name='experts',
        grid_spec=grid_spec,
        out_shape=jax.ShapeDtypeStruct((rows, d), BF16),
        compiler_params=_cparams(("arbitrary",)),
    )(block_e, n_used, x_rows, wg, wu, wd)


def _shared_resid_body(h_ref, wg_ref, wu_ref, wd_ref, pk_ref, ws_ref, x_ref, g_ref, o_ref):
    y = _swiglu(h_ref[0], wg_ref[...], wu_ref[...], wd_ref[...])
    ws = ws_ref[0]
    for k in range(MOE_TOPK):
        y = y + ws[:, k:k + 1] * pk_ref[k, 0].astype(F32)
    o_ref[0] = x_ref[0] + g_ref[...] * y


def shared_resid(h, picked, wsel, xs, mods, wg, wu, wd, n_tiles, b0):
    b, _, d = h.shape
    f = wg.shape[-1]
    tile = pl.BlockSpec((1, ROW_TILE, d), lambda i, j: (i, j, 0))
    xs_tile = pl.BlockSpec((1, ROW_TILE, d), lambda i, j: (i + b0, j, 0))
    return pl.pallas_call(
        _shared_resid_body,
        name='shared_resid',
        grid=(b, n_tiles),
        in_specs=[tile,
                  pl.BlockSpec((d, f), lambda i, j: (0, 0)),
                  pl.BlockSpec((d, f), lambda i, j: (0, 0)),
                  pl.BlockSpec((f, d), lambda i, j: (0, 0)),
                  pl.BlockSpec((MOE_TOPK, 1, ROW_TILE, d), lambda i, j: (0, i, j, 0)),
                  pl.BlockSpec((1, ROW_TILE, MOE_TOPK), lambda i, j: (i, j, 0)),
                  xs_tile, _mod_spec(5, d, b0)],
        out_specs=xs_tile,
        out_shape=jax.ShapeDtypeStruct(xs.shape, F32),
        input_output_aliases={6: 0},
        compiler_params=_cparams(("parallel", "parallel")),
    )(h, wg, wu, wd, picked, wsel, xs, mods)


MOE_GROUPS = 2


def moe_layer(xs, norm_w, mods, router_w, router_b, w_gate, w_up, w_down, sh_gate, sh_up, sh_down, n_tiles, layer):
    bsz, _, d = xs.shape
    shared_w = (sh_gate.astype(BF16), sh_up.astype(BF16), sh_down.astype(BF16))
    b = bsz // MOE_GROUPS
    h, idx, wsel, rank, counts = route(xs, norm_w, mods, router_w, router_b, n_tiles, b)
    h_flat = h.reshape(-1, d)
    for g in range(MOE_GROUPS):
        sl = slice(g * b, (g + 1) * b)
        xs = _moe_group(xs, mods, h_flat, h[sl], idx[sl], wsel[sl], rank[sl], counts[g, 0], w_gate, w_up, w_down,
                        shared_w, n_tiles, layer, g * b)
    return xs


def _moe_group(xs, mods, h_flat, h, idx, wsel, rank, counts, w_gate, w_up, w_down, shared_w, n_tiles, layer, b0):
    b, rows_per_sample, d = h.shape
    n = b * rows_per_sample
    counts = counts.astype(jnp.int32)
    padded = (counts + MOE_BLOCK - 1) // MOE_BLOCK * MOE_BLOCK
    ends = jnp.cumsum(padded)
    starts = ends - padded
    nk = n * MOE_TOPK
    n_blocks = -(-nk // MOE_BLOCK) + MOE_EXPERTS
    rows = n_blocks * MOE_BLOCK
    n_pad = rows - nk
    e_iota = jnp.arange(MOE_EXPERTS, dtype=jnp.int32)
    dest = jnp.sum(jnp.where(idx[..., None] == e_iota, starts, 0), axis=-1) + rank
    blk_start = jnp.arange(n_blocks, dtype=jnp.int32) * MOE_BLOCK
    block_e = jnp.minimum(jnp.sum(ends[None, :] <= blk_start[:, None], axis=1), MOE_EXPERTS - 1).astype(jnp.int32)
    n_used = (ends[-1:] // MOE_BLOCK).astype(jnp.int32)
    pad = padded - counts
    cum_pad = jnp.cumsum(pad)
    m = jnp.arange(n_pad, dtype=jnp.int32)
    e_m = jnp.sum(cum_pad[None, :] <= m[:, None], axis=1)
    base = jnp.sum(jnp.where(jnp.minimum(e_m, MOE_EXPERTS - 1)[:, None] == e_iota,
                             starts + counts - (cum_pad - pad), 0), axis=1)
    pad_row = jnp.where(e_m < MOE_EXPERTS, base + m, ends[-1] + m - cum_pad[-1])
    tok0 = b0 * rows_per_sample
    tok = tok0 + jnp.arange(nk, dtype=jnp.int32) // MOE_TOPK
    _, row_tok = lax.sort((jnp.concatenate([dest.reshape(-1), pad_row]).astype(jnp.int32),
                           jnp.concatenate([tok, tok0 + m % n])), num_keys=1)
    x_rows = h_flat[row_tok]
    y_rows = moe_experts(x_rows, block_e, n_used, w_gate, w_up, w_down, layer)
    picked = y_rows[dest.reshape(n, MOE_TOPK).T].reshape(MOE_TOPK, b, rows_per_sample, d)
    return shared_resid(h, picked, wsel, xs, mods, *shared_w, n_tiles, b0)


def _dwconv_body(x_ref, w_ref, b_ref, o_ref, *, width, act):
    chunk = ROW_TILE
    n_chunks = NTOK // chunk
    first_of_seq = (0, N_LAT_TILES)
    last_of_seq = (N_LAT_TILES - 1, n_chunks - 1)
    tc = x_ref.shape[-1]
    halo = 16
    row = lax.broadcasted_iota(jnp.int32, (chunk, tc), 0)
    zero_row = jnp.zeros((1, tc), F32)
    for c in range(n_chunks):
        r0 = c * chunk
        cur = x_ref[0, r0:r0 + chunk, :].astype(F32)
        if c in first_of_seq:
            prev_last = zero_row
        else:
            prev_last = x_ref[0, r0 - halo:r0, :].astype(F32)[halo - 1:halo, :]
        if c in last_of_seq:
            next0 = next1 = zero_row
        else:
            nxt = x_ref[0, r0 + chunk:r0 + chunk + halo, :].astype(F32)
            next0, next1 = nxt[0:1, :], nxt[1:2, :]
        xm1 = jnp.where(row == 0, prev_last, pltpu.roll(cur, 1, 0))
        xp1 = jnp.where(row == chunk - 1, next0, pltpu.roll(cur, chunk - 1, 0))
        y = w_ref[0:1, :] * xm1 + w_ref[1:2, :] * cur + w_ref[2:3, :] * xp1 + b_ref[...]
        if width == 4:
            xp2 = jnp.where(row == chunk - 2, next0,
                            jnp.where(row == chunk - 1, next1, pltpu.roll(cur, chunk - 2, 0)))
            y = y + w_ref[3:4, :] * xp2
        if act:
            y = _silu(y)
        o_ref[0, c * chunk:(c + 1) * chunk, :] = y.astype(o_ref.dtype)


def dwconv_stream(x, w, b, act, tc=256):
    bsz, nt, c = x.shape
    width = w.shape[0]
    return pl.pallas_call(
        functools.partial(_dwconv_body, width=width, act=act),
        name='dwconv',
        grid=(bsz, c // tc),
        in_specs=[pl.BlockSpec((1, nt, tc), lambda i, j: (i, 0, j)),
                  pl.BlockSpec((width, tc), lambda i, j: (0, j)),
                  pl.BlockSpec((1, tc), lambda i, j: (0, j))],
        out_specs=pl.BlockSpec((1, nt, tc), lambda i, j: (i, 0, j)),
        out_shape=jax.ShapeDtypeStruct(x.shape, BF16),
        compiler_params=_cparams(("parallel", "parallel")),
    )(x, w, b.reshape(1, c))


HY_FB = 512


def dft_matrices(n):
    k = jnp.arange(n, dtype=jnp.int32)[:, None]
    t = jnp.arange(n, dtype=jnp.int32)[None, :]
    ang = (2.0 * math.pi / (2 * n)) * ((k * t) % (2 * n)).astype(F32)
    fre = jnp.cos(ang)
    fim = -jnp.sin(ang)
    nyq = jnp.where(t % 2 == 0, 1.0, -1.0).astype(F32)
    fim = jnp.where(k == 0, nyq, fim)
    fwd = jnp.concatenate([fre, fim], axis=0)
    colscale = jnp.where(jnp.arange(2 * n) % n == 0, 0.5, 1.0) / n
    inv = fwd.T * colscale[None, :]
    return fwd, inv


def hyena_filter_taps(n, fw0, fb0, fw1, fb1, fw2, fb2, fw3, freq):
    pos = jnp.arange(n, dtype=F32)
    t = pos / max(n - 1, 1)
    bands = jnp.linspace(1e-4, HY_BANDS - 1, HY_BANDS, dtype=F32)
    ang = (2.0 * math.pi / n) * pos[:, None] * bands[None, :]
    feats = jnp.concatenate([t[:, None], jnp.cos(ang), -jnp.sin(ang)], axis=-1)
    h = jnp.sin(freq * (jnp.dot(feats, fw0, precision=HIGHEST) + fb0))
    h = jnp.sin(freq * (jnp.dot(h, fw1, precision=HIGHEST) + fb1))
    h = jnp.sin(freq * (jnp.dot(h, fw2, precision=HIGHEST) + fb2))
    h = pmatmul(h, fw3, exact=True).reshape(n, 2, HY_ORDER, D_HY)
    deltas = jnp.abs(jnp.linspace(math.log(HY_DECAY_PCT_LO) / HY_DECAY_TARGET,
                                  math.log(HY_DECAY_PCT_HI) / HY_DECAY_TARGET, D_HY, dtype=F32))
    h = h * jnp.exp(-t[:, None] * deltas)[:, None, None, :]
    h0 = h[:, 0]
    h1 = h[:, 1].at[0].set(0.0)
    norm = jnp.sum(jnp.abs(h0), axis=0, keepdims=True) + jnp.sum(jnp.abs(h1), axis=0, keepdims=True)
    h0 = (h0 / norm).reshape(n, HY_ORDER * D_HY)
    h1 = (h1 / norm).reshape(n, HY_ORDER * D_HY)
    return h0 + h1, h0 - h1


def _split_bf16(a):
    hi = a.astype(BF16)
    return hi, (a - hi.astype(F32)).astype(BF16)


def hyena_spectrum(fwd, hsum, hdiff, fb):
    n = hsum.shape[0]
    f_hi, f_lo = _split_bf16(fwd)

    def dft(h):
        h_hi, h_lo = _split_bf16(h)
        return pmatmul(f_hi, h_hi) + pmatmul(f_hi, h_lo) + pmatmul(f_lo, h_hi)

    a = dft(hsum)
    bm = dft(hdiff)
    sr = a[:n]
    si = bm[n:]
    nyq = a[n]
    first = (jnp.arange(n) == 0)[:, None]
    p = sr
    q = jnp.where(first, 0.0, si)
    s = jnp.where(first, nyq[None, :], sr)
    spec = jnp.stack([p, q, s], axis=0).reshape(3, n // fb, fb, HY_ORDER, D_HY)
    return spec.transpose(3, 1, 0, 2, 4), f_hi


def _hyena_body(u_ref, fre_ref, fim_ref, gre_ref, gim_ref, sp_ref, bias_ref, prev_ref, o_ref,
                vin, acc, *, nf):
    del prev_ref
    o = pl.program_id(1)
    f = pl.program_id(2)
    c = D_HY

    @pl.when((o == 0) & (f == 0))
    def _():
        vin[...] = u_ref[0, :, 0:c]

    @pl.when(f == 0)
    def _():
        acc[...] = jnp.zeros_like(acc)

    v = vin[...]
    vr = jnp.dot(fre_ref[...], v, preferred_element_type=F32)
    vi = jnp.dot(fim_ref[...], v, preferred_element_type=F32)
    p, q, s = sp_ref[0], sp_ref[1], sp_ref[2]
    zr = (vr * p - vi * q).astype(BF16)
    zi = (vr * q + vi * s).astype(BF16)
    acc[...] += (jnp.dot(gre_ref[...], zr, preferred_element_type=F32)
                 + jnp.dot(gim_ref[...], zi, preferred_element_type=F32))

    @pl.when((o == 0) & (f == nf - 1))
    def _():
        z = u_ref[0, :, c:2 * c].astype(F32) * (acc[...] + bias_ref[0:1, :] * vin[...].astype(F32))
        vin[...] = z.astype(BF16)

    @pl.when((o == 1) & (f == nf - 1))
    def _():
        y = u_ref[0, :, 2 * c:3 * c].astype(F32) * (acc[...] + bias_ref[1:2, :] * vin[...].astype(F32))
        o_ref[0] = y.astype(o_ref.dtype)


def hyena_long_conv(u, fwd_bf16, inv_bf16, spec, bias, n, row_block, prev_out):
    bsz = u.shape[0]
    fb = spec.shape[3]
    nf = n // fb
    out_shape = jax.ShapeDtypeStruct((bsz, NTOK, D_HY), BF16)
    if prev_out is None:
        prev_out = jnp.zeros(out_shape.shape, BF16)
    args = [u, fwd_bf16, fwd_bf16, inv_bf16, inv_bf16, spec, bias, prev_out]
    aliases = {7: 0}
    return pl.pallas_call(
        functools.partial(_hyena_body, nf=nf),
        name='hyena',
        grid=(bsz, HY_ORDER, nf),
        in_specs=[pl.BlockSpec((1, n, 3 * D_HY), lambda b, o, f: (b, row_block, 0)),
                  pl.BlockSpec((fb, n), lambda b, o, f: (f, 0)),
                  pl.BlockSpec((fb, n), lambda b, o, f: (nf + f, 0)),
                  pl.BlockSpec((n, fb), lambda b, o, f: (0, f)),
                  pl.BlockSpec((n, fb), lambda b, o, f: (0, nf + f)),
                  pl.BlockSpec((None, None, 3, fb, D_HY), lambda b, o, f: (o, f, 0, 0, 0)),
                  pl.BlockSpec((HY_ORDER, D_HY), lambda b, o, f: (0, 0)),
                  pl.BlockSpec(memory_space=pl.ANY)],
        out_specs=pl.BlockSpec((1, n, D_HY), lambda b, o, f: (b, row_block, 0)),
        out_shape=out_shape,
        scratch_shapes=[pltpu.VMEM((n, D_HY), BF16), pltpu.VMEM((n, D_HY), F32)],
        input_output_aliases=aliases,
        compiler_params=_cparams(("parallel", "arbitrary", "arbitrary"), VMEM_LIMIT_BIG),
    )(*args)


def hyena_mixer_stream(p_hy, conv_w, conv_b, filt, bias):
    u = dwconv_stream(p_hy, conv_w, conv_b, act=False)
    out = None
    for n, row_block in ((SEQ, 0), (CTX_LEN, SEQ // CTX_LEN)):
        fb = min(HY_FB, n)
        fwd, inv = dft_matrices(n)
        hsum, hdiff = hyena_filter_taps(n, *filt)
        spec, fwd_bf16 = hyena_spectrum(fwd, hsum, hdiff, fb)
        out = hyena_long_conv(u, fwd_bf16, inv.astype(BF16), spec, bias, n, row_block, out)
    return out


SCAN_SAMPLES = 2


def _tri(n, kind):
    r = lax.broadcasted_iota(jnp.int32, (n, n), 0)
    c = lax.broadcasted_iota(jnp.int32, (n, n), 1)
    return (c <= r) if kind == 'lower' else (c >= r)


def _ssd_dir(xbc_ref, dt_ref, dtt_ref, bias_r, bias_c, a_r, a_c, st_ref, y_ref, *, s, d, reverse):
    q = SSD_CHUNK
    nh = SSD_H
    gw = SSD_HPG * SSD_P
    lower = _tri(q, 'lower')
    upper = _tri(q, 'upper')
    lower_f = lower.astype(F32)
    upper_f = upper.astype(F32)
    dt_col = _softplus(dt_ref[s] + bias_r)
    dt_row = _softplus(dtt_ref[s] + bias_c)
    da_col = dt_col * a_r
    da_row = dt_row * a_c
    if not reverse:
        acs_col = jnp.dot(lower_f, da_col, preferred_element_type=F32, precision=HIGHEST)
        acs_row = jnp.dot(da_row, upper_f, preferred_element_type=F32, precision=HIGHEST)
        mask = lower
        edge = q - 1
    else:
        acs_col = jnp.dot(upper_f, da_col, preferred_element_type=F32, precision=HIGHEST)
        acs_row = jnp.dot(da_row, lower_f, preferred_element_type=F32, precision=HIGHEST)
        mask = upper
        edge = 0
    h0 = d * nh
    hh = lax.broadcasted_iota(jnp.int32, (2 * nh, nh * SSD_P), 0)
    cc = lax.broadcasted_iota(jnp.int32, (2 * nh, nh * SSD_P), 1) // SSD_P
    expand = (hh == cc + h0).astype(F32)
    acs_c = jnp.dot(acs_col, expand, preferred_element_type=F32, precision=HIGHEST)
    dt_c = jnp.dot(dt_col, expand, preferred_element_type=F32, precision=HIGHEST)
    total_c = acs_c[edge:edge + 1, :]
    e_in_c = jnp.exp(acs_c)
    w_end_c = jnp.exp(total_c - acs_c) * dt_c
    dec_c = jnp.exp(total_c)
    xs = xbc_ref[s, :, 0:D_SSM]
    xs_f = xs.astype(F32)
    for g in range(SSD_G):
        bm = xbc_ref[s, :, D_SSM + g * SSD_N:D_SSM + (g + 1) * SSD_N]
        cm = xbc_ref[s, :, D_SSM + SSD_G * SSD_N + g * SSD_N:D_SSM + SSD_G * SSD_N + (g + 1) * SSD_N]
        cb = lax.dot_general(cm, bm, (((1,), (1,)), ((), ())), preferred_element_type=F32)
        lws = []
        for k in range(SSD_HPG):
            h = h0 + g * SSD_HPG + k
            seg = acs_col[:, h:h + 1] - acs_row[h:h + 1, :]
            decay = jnp.exp(jnp.where(mask, seg, -jnp.inf))
            lws.append((cb * decay * dt_row[h:h + 1, :]).astype(BF16))
        lw = jnp.concatenate(lws, axis=1)
        xg = xs[:, g * gw:(g + 1) * gw]
        rb = lax.broadcasted_iota(jnp.int32, (SSD_HPG * q, gw), 0) // q
        cbk = lax.broadcasted_iota(jnp.int32, (SSD_HPG * q, gw), 1) // SSD_P
        x_bd = jnp.where(rb == cbk, jnp.concatenate([xg] * SSD_HPG, axis=0), jnp.zeros((), BF16))
        y_in = jnp.dot(lw, x_bd, preferred_element_type=F32)
        st = st_ref[s, g]
        y_st = jnp.dot(cm, st.astype(BF16), preferred_element_type=F32) * e_in_c[:, g * gw:(g + 1) * gw]
        y_ref[s, :, g * gw:(g + 1) * gw] = y_in + y_st
        xw = (xs_f[:, g * gw:(g + 1) * gw] * w_end_c[:, g * gw:(g + 1) * gw]).astype(BF16)
        upd = lax.dot_general(bm, xw, (((0,), (0,)), ((), ())), preferred_element_type=F32)
        st_ref[s, g] = st * dec_c[:, g * gw:(g + 1) * gw] + upd


def _ssd_body(xf_ref, dtf_ref, dttf_ref, xb_ref, dtb_ref, dttb_ref, bias_r, bias_c, a_r, a_c,
              yf_ref, yb_ref, stf, stb):
    @pl.when(pl.program_id(1) == 0)
    def _():
        stf[...] = jnp.zeros_like(stf)
        stb[...] = jnp.zeros_like(stb)

    for s in range(SCAN_SAMPLES):
        _ssd_dir(xf_ref, dtf_ref, dttf_ref, bias_r[...], bias_c[...], a_r[...], a_c[...], stf, yf_ref,
                 s=s, d=0, reverse=False)
        _ssd_dir(xb_ref, dtb_ref, dttb_ref, bias_r[...], bias_c[...], a_r[...], a_c[...], stb, yb_ref,
                 s=s, d=1, reverse=True)


def ssd_scan(xbc, dt, dt_bias, a_log):
    bsz = xbc.shape[0]
    nc = NTOK // SSD_CHUNK
    nlat = SEQ // SSD_CHUNK
    dtt = jnp.swapaxes(dt, 1, 2)
    fwd_chunk = lambda s: (s + nlat) % nc
    bwd_chunk = lambda s: nc - 1 - s
    a = -jnp.exp(a_log.astype(F32)).reshape(1, 2 * SSD_H)
    bias = dt_bias.astype(F32).reshape(1, 2 * SSD_H)
    ns = SCAN_SAMPLES
    x_spec = lambda cm: pl.BlockSpec((ns, SSD_CHUNK, SSD_XBC), lambda b, s: (b, cm(s), 0))
    dt_spec = lambda cm: pl.BlockSpec((ns, SSD_CHUNK, 2 * SSD_H), lambda b, s: (b, cm(s), 0))
    dtt_spec = lambda cm: pl.BlockSpec((ns, 2 * SSD_H, SSD_CHUNK), lambda b, s: (b, 0, cm(s)))
    y_spec = lambda cm: pl.BlockSpec((ns, SSD_CHUNK, D_SSM), lambda b, s: (b, cm(s), 0))
    row = pl.BlockSpec((1, 2 * SSD_H), lambda b, s: (0, 0))
    col = pl.BlockSpec((2 * SSD_H, 1), lambda b, s: (0, 0))
    y_shape = jax.ShapeDtypeStruct((bsz, NTOK, D_SSM), F32)
    gw = SSD_HPG * SSD_P
    return pl.pallas_call(
        _ssd_body,
        name='ssd_scan',
        grid=(bsz // ns, nc),
        in_specs=[x_spec(fwd_chunk), dt_spec(fwd_chunk), dtt_spec(fwd_chunk),
                  x_spec(bwd_chunk), dt_spec(bwd_chunk), dtt_spec(bwd_chunk),
                  row, col, row, col],
        out_specs=[y_spec(fwd_chunk), y_spec(bwd_chunk)],
        out_shape=[y_shape, y_shape],
        scratch_shapes=[pltpu.VMEM((ns, SSD_G, SSD_N, gw), F32), pltpu.VMEM((ns, SSD_G, SSD_N, gw), F32)],
        compiler_params=_cparams(("parallel", "arbitrary")),
    )(xbc, dt, dtt, xbc, dt, dtt, bias, bias.reshape(-1, 1), a, a.reshape(-1, 1))


def _ssd_merge_body(yf_ref, yb_ref, xbc_ref, z_ref, d_ref, nw_ref, o_ref):
    xs = xbc_ref[0, :, 0:D_SSM].astype(F32)
    z = z_ref[0].astype(F32)
    g = (yf_ref[0] + yb_ref[0] + d_ref[...] * xs) * _silu(z)
    gw = D_SSM // SSD_G
    for k in range(SSD_G):
        gk = g[:, k * gw:(k + 1) * gw]
        ms = jnp.mean(gk * gk, axis=-1, keepdims=True)
        o_ref[0, :, k * gw:(k + 1) * gw] = (gk * lax.rsqrt(ms + NORM_EPS)
                                            * nw_ref[:, k * gw:(k + 1) * gw]).astype(o_ref.dtype)


def ssd_merge(yf, yb, xbc, z, d_skip, norm_w):
    bsz = yf.shape[0]
    tile = lambda w: pl.BlockSpec((1, ROW_TILE, w), lambda i, j: (i, j, 0))
    vec = pl.BlockSpec((1, D_SSM), lambda i, j: (0, 0))
    d_chan = jnp.repeat(d_skip.astype(F32), SSD_P).reshape(1, D_SSM)
    return pl.pallas_call(
        _ssd_merge_body,
        name='ssd_merge',
        grid=(bsz, N_ROW_TILES),
        in_specs=[tile(D_SSM), tile(D_SSM), tile(SSD_XBC), tile(D_SSM), vec, vec],
        out_specs=tile(D_SSM),
        out_shape=jax.ShapeDtypeStruct((bsz, NTOK, D_SSM), BF16),
        compiler_params=_cparams(("parallel", "parallel")),
    )(yf, yb, xbc, z, d_chan, norm_w.reshape(1, D_SSM))


def even_layer_mixer(xs, mods, norm_w, w_in, w_out, hy_conv_w, hy_conv_b, hy_filt, hy_bias,
                     ssd_conv_w, ssd_conv_b, ssd_dt_bias, ssd_a_log, ssd_d, ssd_norm_w):
    bsz = xs.shape[0]
    h = norm_mod(xs, norm_w, mods, 0, 1)
    splits = ((0, HY_IN), (HY_IN, D_SSM), (HY_IN + D_SSM, SSD_XBC), (HY_IN + D_SSM + SSD_XBC, 2 * SSD_H))
    p_hy, z, xbc_raw, dt = mm_split(h.reshape(bsz * NTOK, D_MODEL), w_in.astype(BF16), splits,
                                    (BF16, BF16, BF16, F32))
    to3 = lambda a: a.reshape(bsz, NTOK, a.shape[-1])
    y_hy = hyena_mixer_stream(to3(p_hy), hy_conv_w, hy_conv_b, hy_filt, hy_bias)
    xbc = dwconv_stream(to3(xbc_raw), ssd_conv_w, ssd_conv_b, act=True)
    yf, yb = ssd_scan(xbc, to3(dt), ssd_dt_bias, ssd_a_log)
    s = ssd_merge(yf, yb, xbc, to3(z), ssd_d, ssd_norm_w)
    wo = w_out.astype(BF16)
    return mm_resid([y_hy, s], [wo[:D_HY], wo[D_HY:]], xs, mods, 2, N_ROW_TILES)


GLA_QK = GLA_H * GLA_DK
GLA_V = GLA_H * GLA_DV


def _gla_dir(qkv_ref, lr_ref, gw_ref, gb_ref, st_ref, o_ref, *, s, d, reverse):
    q = GLA_CHUNK
    tri = _tri(q, 'upper' if reverse else 'lower')
    edge = 0 if reverse else q - 1
    lr = lr_ref[s, :, d * GLA_RANK:(d + 1) * GLA_RANK]
    logit = jnp.dot(lr, gw_ref[d], preferred_element_type=F32, precision=HIGHEST) + gb_ref[d:d + 1, :]
    log_g = -_softplus(-logit) * (1.0 / GLA_GATE_NORM)
    gcum = jnp.dot(tri.astype(F32), log_g, preferred_element_type=F32, precision=HIGHEST)
    total = gcum[edge:edge + 1, :]
    qf = qkv_ref[s, :, 0:GLA_QK].astype(F32)
    kf = qkv_ref[s, :, GLA_QK:2 * GLA_QK].astype(F32)
    v = qkv_ref[s, :, 2 * GLA_QK:2 * GLA_QK + GLA_V]
    qg = (qf * (GLA_DK ** -0.5) * jnp.exp(gcum)).astype(BF16)
    kg = (kf * jnp.exp(-gcum)).astype(BF16)
    kw = (kf * jnp.exp(total - gcum)).astype(BF16)
    rb = lax.broadcasted_iota(jnp.int32, (GLA_H * q, GLA_QK), 0) // q
    cb = lax.broadcasted_iota(jnp.int32, (GLA_H * q, GLA_QK), 1) // GLA_DK
    k_bd = jnp.where(rb == cb, jnp.concatenate([kg] * GLA_H, axis=0), jnp.zeros((), BF16))
    att = lax.dot_general(qg, k_bd, (((1,), (1,)), ((), ())), preferred_element_type=F32)
    i_i = lax.broadcasted_iota(jnp.int32, (q, GLA_H * q), 0)
    j_i = lax.broadcasted_iota(jnp.int32, (q, GLA_H * q), 1) % q
    keep = (j_i >= i_i) if reverse else (j_i <= i_i)
    att = jnp.where(keep, att, 0.0).astype(BF16)
    rv = lax.broadcasted_iota(jnp.int32, (GLA_H * q, GLA_V), 0) // q
    cv = lax.broadcasted_iota(jnp.int32, (GLA_H * q, GLA_V), 1) // GLA_DV
    v_bd = jnp.where(rv == cv, jnp.concatenate([v] * GLA_H, axis=0), jnp.zeros((), BF16))
    st = st_ref[s]
    o_in = jnp.dot(att, v_bd, preferred_element_type=F32)
    o_st = lax.dot_general(qg, st.astype(BF16), (((1,), (1,)), ((), ())), preferred_element_type=F32)
    o_ref[s] = o_in + o_st
    upd = lax.dot_general(v, kw, (((0,), (0,)), ((), ())), preferred_element_type=F32)
    rs = lax.broadcasted_iota(jnp.int32, (GLA_V, GLA_QK), 0) // GLA_DV
    cs = lax.broadcasted_iota(jnp.int32, (GLA_V, GLA_QK), 1) // GLA_DK
    st_ref[s] = st * jnp.exp(total) + jnp.where(rs == cs, upd, 0.0)


def _gla_body(qf_ref, lf_ref, qb_ref, lb_ref, gw_ref, gb_ref, of_ref, ob_ref, stf, stb):
    @pl.when(pl.program_id(1) == 0)
    def _():
        stf[...] = jnp.zeros_like(stf)
        stb[...] = jnp.zeros_like(stb)

    for s in range(SCAN_SAMPLES):
        _gla_dir(qf_ref, lf_ref, gw_ref, gb_ref, stf, of_ref, s=s, d=0, reverse=False)
        _gla_dir(qb_ref, lb_ref, gw_ref, gb_ref, stb, ob_ref, s=s, d=1, reverse=True)


def gla_scan(qkv, lr, gate_w, gate_b):
    bsz = qkv.shape[0]
    nc = NTOK // GLA_CHUNK
    nlat = SEQ // GLA_CHUNK
    fwd_chunk = lambda s: (s + nlat) % nc
    bwd_chunk = lambda s: nc - 1 - s
    ns = SCAN_SAMPLES
    q_spec = lambda cm: pl.BlockSpec((ns, GLA_CHUNK, qkv.shape[-1]), lambda b, s: (b, cm(s), 0))
    l_spec = lambda cm: pl.BlockSpec((ns, GLA_CHUNK, 2 * GLA_RANK), lambda b, s: (b, cm(s), 0))
    o_spec = lambda cm: pl.BlockSpec((ns, GLA_CHUNK, GLA_V), lambda b, s: (b, cm(s), 0))
    o_shape = jax.ShapeDtypeStruct((bsz, NTOK, GLA_V), F32)
    return pl.pallas_call(
        _gla_body,
        name='gla_scan',
        grid=(bsz // ns, nc),
        in_specs=[q_spec(fwd_chunk), l_spec(fwd_chunk), q_spec(bwd_chunk), l_spec(bwd_chunk),
                  pl.BlockSpec((2, GLA_RANK, GLA_QK), lambda b, s: (0, 0, 0)),
                  pl.BlockSpec((2, GLA_QK), lambda b, s: (0, 0))],
        out_specs=[o_spec(fwd_chunk), o_spec(bwd_chunk)],
        out_shape=[o_shape, o_shape],
        scratch_shapes=[pltpu.VMEM((ns, GLA_V, GLA_QK), F32), pltpu.VMEM((ns, GLA_V, GLA_QK), F32)],
        compiler_params=_cparams(("parallel", "arbitrary")),
    )(qkv, lr, qkv, lr, gate_w.astype(F32), gate_b.astype(F32))


def _gla_merge_body(of_ref, ob_ref, r_ref, nw_ref, o_ref):
    o = of_ref[0] + ob_ref[0]
    r = r_ref[0].astype(F32)
    for h in range(GLA_H):
        sl = slice(h * GLA_DV, (h + 1) * GLA_DV)
        oh = o[:, sl]
        ms = jnp.mean(oh * oh, axis=-1, keepdims=True)
        o_ref[0, :, sl] = (oh * lax.rsqrt(ms + NORM_EPS) * nw_ref[:, sl] * _silu(r[:, sl])).astype(o_ref.dtype)


def gla_merge_stream(of, ob, r, norm_w):
    bsz = of.shape[0]
    tile = pl.BlockSpec((1, ROW_TILE, GLA_V), lambda i, j: (i, j, 0))
    return pl.pallas_call(
        _gla_merge_body,
        name='gla_merge',
        grid=(bsz, N_LAT_TILES),
        in_specs=[tile, tile, tile, pl.BlockSpec((1, GLA_V), lambda i, j: (0, 0))],
        out_specs=tile,
        out_shape=jax.ShapeDtypeStruct((bsz, SEQ, GLA_V), BF16),
        compiler_params=_cparams(("parallel", "parallel")),
    )(of, ob, r, norm_w.reshape(1, GLA_V))


RG_TILE = 8


def _gelu_tanh(x):
    return 0.5 * x * (1.0 + jnp.tanh(math.sqrt(2.0 / math.pi) * (x + 0.044715 * x * x * x)))


def _rg_scan_block(a_s, x_s, h_s, base, carry, reverse):
    n_tiles = ROW_TILE // RG_TILE
    row = lax.broadcasted_iota(jnp.int32, (RG_TILE, D_RG), 0)

    def tile_step(i, h_prev):
        t = (n_tiles - 1 - i) if reverse else i
        r0 = pl.multiple_of(t * RG_TILE, RG_TILE)
        a = a_s[pl.ds(r0, RG_TILE), :]
        x = x_s[pl.ds(r0, RG_TILE), :]
        for s in (1, 2, 4):
            if reverse:
                ok = row < RG_TILE - s
                shift = RG_TILE - s
            else:
                ok = row >= s
                shift = s
            a_sh = jnp.where(ok, pltpu.roll(a, shift, 0), 1.0)
            x_sh = jnp.where(ok, pltpu.roll(x, shift, 0), 0.0)
            x = a * x_sh + x
            a = a * a_sh
        h = x + a * h_prev
        h_s[pl.ds(base + r0, RG_TILE), :] = h
        edge = 0 if reverse else RG_TILE - 1
        return jnp.broadcast_to(h[edge:edge + 1, :], (RG_TILE, D_RG))

    return lax.fori_loop(0, n_tiles, tile_step, carry)


def _rglru_body(u_ref, g_ref, w_ref, b_ref, c_ref, o_ref, hf_s, a_s, x_s, hb_s):
    n_blocks = NTOK // ROW_TILE
    fwd_order = list(range(N_LAT_TILES, n_blocks)) + list(range(N_LAT_TILES))
    bwd_order = list(range(n_blocks - 1, N_LAT_TILES - 1, -1)) + list(range(N_LAT_TILES - 1, -1, -1))

    def gates(blk, d):
        ub = u_ref[0, blk * ROW_TILE:(blk + 1) * ROW_TILE, :]
        z = jnp.dot(ub, w_ref[:, 2 * d * D_RG:2 * (d + 1) * D_RG], preferred_element_type=F32)
        z = z + b_ref[:, 2 * d * D_RG:2 * (d + 1) * D_RG]
        r = 1.0 / (1.0 + jnp.exp(-z[:, :D_RG]))
        i = 1.0 / (1.0 + jnp.exp(-z[:, D_RG:]))
        a = jnp.exp(c_ref[d:d + 1, :] * r)
        a_s[...] = a
        x_s[...] = jnp.sqrt(1.0 - a * a) * i * ub.astype(F32)

    carry = jnp.zeros((RG_TILE, D_RG), F32)
    for blk in fwd_order:
        gates(blk, 0)
        carry = _rg_scan_block(a_s, x_s, hf_s, blk * ROW_TILE, carry, reverse=False)
    carry = jnp.zeros((RG_TILE, D_RG), F32)
    for blk in bwd_order:
        gates(blk, 1)
        carry = _rg_scan_block(a_s, x_s, hb_s, 0, carry, reverse=True)
        rows = slice(blk * ROW_TILE, (blk + 1) * ROW_TILE)
        gate = g_ref[0, rows, :].astype(F32)
        o_ref[0, rows, :] = ((hf_s[rows, :] + hb_s[...]) * _gelu_tanh(gate)).astype(o_ref.dtype)


def rglru_stream(u, gate, w_a, b_a, w_x, b_x, lam):
    bsz = u.shape[0]
    eye = jnp.eye(RG_BLOCKS, dtype=F32)
    dense = lambda w: jnp.einsum('nio,nm->nimo', w, eye).reshape(D_RG, D_RG)
    w_cat = jnp.concatenate([dense(w_a[0]), dense(w_x[0]), dense(w_a[1]), dense(w_x[1])], axis=1).astype(BF16)
    b_cat = jnp.concatenate([b_a[0], b_x[0], b_a[1], b_x[1]]).astype(F32).reshape(1, 4 * D_RG)
    c = -RG_C * jax.nn.softplus(-lam.astype(F32))
    seq = pl.BlockSpec((1, NTOK, D_RG), lambda i: (i, 0, 0))
    return pl.pallas_call(
        _rglru_body,
        name='rglru',
        grid=(bsz,),
        in_specs=[seq, seq,
                  pl.BlockSpec((D_RG, 4 * D_RG), lambda i: (0, 0)),
                  pl.BlockSpec((1, 4 * D_RG), lambda i: (0, 0)),
                  pl.BlockSpec((2, D_RG), lambda i: (0, 0))],
        out_specs=seq,
        out_shape=jax.ShapeDtypeStruct((bsz, NTOK, D_RG), BF16),
        scratch_shapes=[pltpu.VMEM((NTOK, D_RG), F32), pltpu.VMEM((ROW_TILE, D_RG), F32),
                        pltpu.VMEM((ROW_TILE, D_RG), F32), pltpu.VMEM((ROW_TILE, D_RG), F32)],
        compiler_params=_cparams(("parallel",)),
    )(u, gate, w_cat, b_cat, c)


def odd_layer_mixer_pallas(xs, mods, norm_w, w_in, w_out, gla_args, rg_args):
    bsz = xs.shape[0]
    gate_w, gate_b, gla_norm_w = gla_args
    rg_conv_w, rg_conv_b, w_a, b_a, w_x, b_x, lam = rg_args
    h = norm_mod(xs, norm_w, mods, 0, 1)
    h = jnp.concatenate([to_col_major(h[:, :SEQ]), h[:, SEQ:]], axis=1)
    nqk, nv = GLA_QK, GLA_V
    r0 = 2 * nqk + nv + 2 * GLA_RANK
    w = jnp.concatenate([w_in[:, :2 * nqk + nv], w_in[:, r0:r0 + nv], w_in[:, GLA_IN:],
                         w_in[:, 2 * nqk + nv:r0]], axis=1).astype(BF16)
    qkv_w = 2 * nqk + nv
    splits = ((0, qkv_w), (qkv_w, nv), (qkv_w + nv, D_RG), (qkv_w + nv + D_RG, D_RG),
              (qkv_w + nv + 2 * D_RG, 2 * GLA_RANK))
    qkv, r, u_raw, gate, lr = mm_split(h.reshape(bsz * NTOK, D_MODEL), w, splits, (BF16, BF16, BF16, BF16, F32))
    to3 = lambda a: a.reshape(bsz, NTOK, a.shape[-1])
    of, ob = gla_scan(to3(qkv), to3(lr), gate_w, gate_b.reshape(2, GLA_QK))
    a_l = gla_merge_stream(of, ob, to3(r), gla_norm_w)
    u = dwconv_stream(to3(u_raw), rg_conv_w, rg_conv_b, act=False)
    r_l = rglru_stream(u, to3(gate), w_a, b_a, w_x, b_x, lam)[:, :SEQ]
    wo = w_out.astype(BF16)
    return mm_resid([from_col_major(a_l), from_col_major(r_l)], [wo[:GLA_V], wo[GLA_V:]], xs, mods, 2, N_LAT_TILES)


def rms_norm(x, w):
    xf = x.astype(F32)
    y = xf * lax.rsqrt(jnp.mean(jnp.square(xf), axis=-1, keepdims=True) + NORM_EPS)
    return y.astype(x.dtype) * w


def dwconv(x, w, b):
    y = lax.conv_general_dilated(x, w[:, None, :].astype(x.dtype), window_strides=(1,), padding='SAME',
                                 dimension_numbers=('NWC', 'WIO', 'NWC'), feature_group_count=x.shape[-1])
    return y + b.astype(x.dtype)


def maybe_flip(a, rev):
    return jnp.flip(a, axis=1) if rev else a


def to_col_major(x):
    b, n, d = x.shape
    rows = n // GRID_W
    return x.reshape(b, rows, GRID_W, d).transpose(0, 2, 1, 3).reshape(b, n, d)


def from_col_major(x):
    b, n, d = x.shape
    rows = n // GRID_W
    return x.reshape(b, GRID_W, rows, d).transpose(0, 2, 1, 3).reshape(b, n, d)


def gla_inputs(p, gate_w, gate_b):
    b, n, _ = p.shape
    nqk, nv = GLA_H * GLA_DK, GLA_H * GLA_DV
    q = p[..., :nqk].reshape(b, n, GLA_H, GLA_DK) * GLA_DK ** -0.5
    k = p[..., nqk:2 * nqk].reshape(b, n, GLA_H, GLA_DK)
    v = p[..., 2 * nqk:2 * nqk + nv].reshape(b, n, GLA_H, GLA_DV)
    lr = p[..., 2 * nqk + nv:2 * nqk + nv + 2 * GLA_RANK].reshape(b, n, 2, GLA_RANK)
    r = p[..., 2 * nqk + nv + 2 * GLA_RANK:]
    logit = jnp.einsum('bler,erk->blek', lr, gate_w) + gate_b
    log_g = (jax.nn.log_sigmoid(logit.astype(F32)) / GLA_GATE_NORM).reshape(b, n, 2, GLA_H, GLA_DK)
    return q, k, v, log_g, r


def gla_states(k, v, log_g, s0):
    b, n = k.shape[:2]
    nc = n // GLA_CHUNK
    kc = k.reshape(b, nc, GLA_CHUNK, GLA_H, GLA_DK)
    vc = v.reshape(b, nc, GLA_CHUNK, GLA_H, GLA_DV)
    gcum = jnp.cumsum(log_g.reshape(b, nc, GLA_CHUNK, GLA_H, GLA_DK), axis=2)
    states = jnp.einsum('bcqhd,bcqhv->bchdv', kc * jnp.exp(gcum[:, :, -1:] - gcum), vc)
    chunk_decay = jnp.exp(gcum[:, :, -1])

    def step(s, inp):
        st, dcy = inp
        return dcy[..., None] * s + st, s

    s_fin, s_prev = lax.scan(step, s0, (jnp.moveaxis(states, 1, 0), jnp.moveaxis(chunk_decay, 1, 0)))
    return jnp.moveaxis(s_prev, 0, 1), s_fin


def gla_output(q, k, v, log_g, s_prev):
    b, n = q.shape[:2]
    nc = n // GLA_CHUNK
    qc = q.reshape(b, nc, GLA_CHUNK, GLA_H, GLA_DK)
    kc = k.reshape(b, nc, GLA_CHUNK, GLA_H, GLA_DK)
    vc = v.reshape(b, nc, GLA_CHUNK, GLA_H, GLA_DV)
    gcum = jnp.cumsum(log_g.reshape(b, nc, GLA_CHUNK, GLA_H, GLA_DK), axis=2)
    qg = qc * jnp.exp(gcum)
    kg = kc * jnp.exp(-gcum)
    mask = jnp.tril(jnp.ones((GLA_CHUNK, GLA_CHUNK), bool))
    att = jnp.where(mask, jnp.einsum('bcihd,bcjhd->bchij', qg, kg), 0.0)
    o = jnp.einsum('bchij,bcjhv->bcihv', att, vc) + jnp.einsum('bcihd,bchdv->bcihv', qg, s_prev)
    return o.reshape(b, n, GLA_H, GLA_DV)


def gla_merge(os_, r, norm_w):
    b, n = r.shape[:2]
    o = rms_norm(os_[0] + os_[1], norm_w.reshape(GLA_H, GLA_DV))
    return o.reshape(b, n, GLA_H * GLA_DV) * jax.nn.silu(r)


def gla_mixer(p_c, p_l, gate_w, gate_b, norm_w):
    q_c, k_c, v_c, g_c, r_c = gla_inputs(p_c, gate_w, gate_b)
    q_l, k_l, v_l, g_l, r_l = gla_inputs(p_l, gate_w, gate_b)
    s0 = jnp.zeros((p_c.shape[0], GLA_H, GLA_DK, GLA_DV), F32)
    os_l = []
    for d, rev in enumerate((False, True)):
        f = functools.partial(maybe_flip, rev=rev)
        _, sf_c = gla_states(f(k_c), f(v_c), f(g_c[:, :, d]), s0)
        sp_l, _ = gla_states(f(k_l), f(v_l), f(g_l[:, :, d]), sf_c)
        os_l.append(f(gla_output(f(q_l), f(k_l), f(v_l), f(g_l[:, :, d]), sp_l)))
    return gla_merge(os_l, r_l, norm_w)


def rglru_inputs(p, conv_w, conv_b, w_a, b_a, w_x, b_x, lam):
    b, n, _ = p.shape
    u = dwconv(p[..., :D_RG], conv_w, conv_b)
    ub = u.reshape(b, n, RG_BLOCKS, RG_BW)
    r = jax.nn.sigmoid((jnp.einsum('blni,enio->bleno', ub, w_a).reshape(b, n, 2, D_RG) + b_a).astype(F32))
    i = jax.nn.sigmoid((jnp.einsum('blni,enio->bleno', ub, w_x).reshape(b, n, 2, D_RG) + b_x).astype(F32))
    log_a = -RG_C * jax.nn.softplus(-lam.astype(F32)) * r
    x_in = jnp.sqrt(-jnp.expm1(2.0 * log_a)) * i * u[:, :, None, :].astype(F32)
    return p[..., D_RG:], jnp.exp(log_a), x_in


def lru_scan(a, u, h0):
    u = u.at[:, 0].add(a[:, 0] * h0)

    def combine(lhs, rhs):
        a1, b1 = lhs
        a2, b2 = rhs
        return a1 * a2, a2 * b1 + b2

    return lax.associative_scan(combine, (a, u), axis=1)[1]


def rglru_mixer(p_c, p_l, conv_w, conv_b, w_a, b_a, w_x, b_x, lam):
    gb_c, a_c, u_c = rglru_inputs(p_c, conv_w, conv_b, w_a, b_a, w_x, b_x, lam)
    gb_l, a_l, u_l = rglru_inputs(p_l, conv_w, conv_b, w_a, b_a, w_x, b_x, lam)
    h0 = jnp.zeros((p_c.shape[0], D_RG), F32)
    hs_l = []
    for d, rev in enumerate((False, True)):
        f = functools.partial(maybe_flip, rev=rev)
        h_c = f(lru_scan(f(a_c[:, :, d]), f(u_c[:, :, d]), h0))
        h_end = h_c[:, 0] if rev else h_c[:, -1]
        hs_l.append(f(lru_scan(f(a_l[:, :, d]), f(u_l[:, :, d]), h_end)))
    return (hs_l[0] + hs_l[1]) * jax.nn.gelu(gb_l.astype(F32))


def odd_layer_mixer(xs, mods, norm_w, w_in, w_out, gla_args, rg_args):
    bsz = xs.shape[0]
    h = norm_mod(xs, norm_w, mods, 0, 1)
    h = jnp.concatenate([to_col_major(h[:, :SEQ]), h[:, SEQ:]], axis=1)
    p = pmatmul(h.reshape(bsz * NTOK, D_MODEL), w_in.astype(BF16)).reshape(bsz, NTOK, OD_IN)
    p_l, p_c = p[:, :SEQ], p[:, SEQ:]
    a_l = gla_mixer(p_c[..., :GLA_IN], p_l[..., :GLA_IN], *gla_args)
    r_l = rglru_mixer(p_c[..., GLA_IN:], p_l[..., GLA_IN:], *rg_args)
    mix = from_col_major(jnp.concatenate([a_l, r_l], axis=-1)).astype(BF16)
    return mm_resid([mix], [w_out.astype(BF16)], xs, mods, 2, N_LAT_TILES)


def kernel(x, c, ctx, c_ctx, ada_w, ada_b, norm1_w, norm2_w, ev_w_in, ev_w_out, hy_conv_w, hy_conv_b, hy_fw0, hy_fb0, hy_fw1, hy_fb1, hy_fw2, hy_fb2, hy_fw3, hy_freq, hy_bias, ssd_conv_w, ssd_conv_b, ssd_dt_bias, ssd_a_log, ssd_d, ssd_norm_w, od_w_in, od_w_out, gla_gate_w, gla_gate_b, gla_norm_w, rg_conv_w, rg_conv_b, rg_w_a, rg_b_a, rg_w_x, rg_b_x, rg_lambda, router_w, router_b, moe_w_gate, moe_w_up, moe_w_down, sh_w_gate, sh_w_up, sh_w_down, final_norm_w):
    xs = jnp.concatenate([x, ctx], axis=1)
    for i in range(DEPTH):
        last = i == DEPTH - 1
        j = i // 2
        mods = adaln_table(c, c_ctx, ada_w[i], ada_b[i])
        if i % 2 == 0:
            hy_filt = (hy_fw0[j], hy_fb0[j], hy_fw1[j], hy_fb1[j], hy_fw2[j], hy_fb2[j], hy_fw3[j], hy_freq[j])
            xs = even_layer_mixer(xs, mods, norm1_w[i], ev_w_in[j], ev_w_out[j], hy_conv_w[j], hy_conv_b[j],
                                  hy_filt, hy_bias[j], ssd_conv_w[j], ssd_conv_b[j], ssd_dt_bias[j],
                                  ssd_a_log[j], ssd_d[j], ssd_norm_w[j])
        else:
            gla_args = (gla_gate_w[j], gla_gate_b[j], gla_norm_w[j])
            rg_args = (rg_conv_w[j], rg_conv_b[j], rg_w_a[j], rg_b_a[j], rg_w_x[j], rg_b_x[j], rg_lambda[j])
            xs = odd_layer_mixer_pallas(xs, mods, norm1_w[i], od_w_in[j], od_w_out[j], gla_args, rg_args)
        n_tiles = N_LAT_TILES if last else N_ROW_TILES
        xs = moe_layer(xs, norm2_w[i], mods, router_w[i], router_b[i], moe_w_gate, moe_w_up, moe_w_down,
                       sh_w_gate[i], sh_w_up[i], sh_w_down[i], n_tiles, i)
    return final_norm(xs, final_norm_w)
```

```python
import functools
import math

import jax
import jax.numpy as jnp
from jax import lax
from jax.experimental import pallas as pl
from jax.experimental.pallas import tpu as pltpu

D_MODEL = 1024
BATCH = 16
SEQ = 2048
DEPTH = 2

CTX_LEN = 256
GRID_W = 64
NORM_EPS = 1e-6

D_HY = D_MODEL // 2
HY_ORDER = 2
HY_SHORT = 3
HY_BANDS = 16
HY_EMB = 1 + 2 * HY_BANDS
HY_FF = 64
HY_DECAY_PCT_LO = 0.3
HY_DECAY_PCT_HI = 1.5
HY_DECAY_TARGET = 1e-2
HY_IN = 3 * D_HY

D_SSM = D_MODEL // 2
SSD_P = 64
SSD_H = D_SSM // SSD_P
SSD_G = 2
SSD_HPG = SSD_H // SSD_G
SSD_N = 128
SSD_CONV = 4
SSD_CHUNK = 128
SSD_XBC = D_SSM + 2 * SSD_G * SSD_N
SSD_IN = D_SSM + SSD_XBC + 2 * SSD_H
EV_IN = HY_IN + SSD_IN
EV_MIX = D_HY + D_SSM

GLA_H = 4
GLA_DV = (D_MODEL // 2) // GLA_H
GLA_DK = GLA_DV // 2
GLA_RANK = 16
GLA_GATE_NORM = 16.0
GLA_CHUNK = 64
GLA_IN = 2 * GLA_H * GLA_DK + 2 * GLA_H * GLA_DV + 2 * GLA_RANK

D_RG = D_MODEL // 2
RG_BLOCKS = 8
RG_BW = D_RG // RG_BLOCKS
RG_CONV = 4
RG_C = 8.0
RG_IN = 2 * D_RG
OD_IN = GLA_IN + RG_IN
OD_MIX = GLA_H * GLA_DV + D_RG

MOE_EXPERTS = 64
MOE_TOPK = 8
MOE_D_EXPERT = 256
MOE_D_SHARED = 256
MOE_SCALE = 2.5
MOE_BLOCK = 512

F32 = jnp.float32
BF16 = jnp.bfloat16
HIGHEST = lax.Precision.HIGHEST

NTOK = SEQ + CTX_LEN
ROW_TILE = 256
N_ROW_TILES = NTOK // ROW_TILE
N_LAT_TILES = SEQ // ROW_TILE

VMEM_LIMIT = 48 * 1024 * 1024
VMEM_LIMIT_BIG = 56 * 1024 * 1024


def _cparams(sem, limit=VMEM_LIMIT):
    return pltpu.CompilerParams(dimension_semantics=sem, vmem_limit_bytes=limit)


def _pick_tile(n, pref):
    t = min(n, pref)
    while n % t:
        t //= 2
    return t


def _silu(x):
    return x / (1.0 + jnp.exp(-x))


def _softplus(x):
    return jnp.maximum(x, 0.0) + jnp.log(1.0 + jnp.exp(-jnp.abs(x)))


def _mm_bf16_body(a_ref, w_ref, o_ref):
    o_ref[...] = jnp.dot(a_ref[...].astype(BF16), w_ref[...].astype(BF16),
                         preferred_element_type=F32).astype(o_ref.dtype)


def _mm_f32_body(a_ref, w_ref, o_ref):
    o_ref[...] = jnp.dot(a_ref[...], w_ref[...], preferred_element_type=F32,
                         precision=HIGHEST).astype(o_ref.dtype)


def pmatmul(a, w, *, exact=False, out_dtype=F32, tm=512, tn=None):
    m, k = a.shape
    n = w.shape[1]
    tm = _pick_tile(m, tm)
    tn = n if tn is None else _pick_tile(n, tn)
    body = _mm_f32_body if exact else _mm_bf16_body
    return pl.pallas_call(
        body,
        name='mm',
        grid=(m // tm, n // tn),
        in_specs=[pl.BlockSpec((tm, k), lambda i, j: (i, 0)),
                  pl.BlockSpec((k, tn), lambda i, j: (0, j))],
        out_specs=pl.BlockSpec((tm, tn), lambda i, j: (i, j)),
        out_shape=jax.ShapeDtypeStruct((m, n), out_dtype),
        compiler_params=_cparams(("parallel", "parallel")),
    )(a, w)


def _mm_split_body(a_ref, w_ref, *o_refs, splits):
    a = a_ref[...]
    for o_ref, (start, width) in zip(o_refs, splits):
        o_ref[...] = jnp.dot(a, w_ref[:, start:start + width],
                             preferred_element_type=F32).astype(o_ref.dtype)


def mm_split(a, w, splits, dtypes, tm=512):
    m, k = a.shape
    n = w.shape[1]
    tm = _pick_tile(m, tm)
    return pl.pallas_call(
        functools.partial(_mm_split_body, splits=tuple(splits)),
        name='mm_split',
        grid=(m // tm,),
        in_specs=[pl.BlockSpec((tm, k), lambda i: (i, 0)),
                  pl.BlockSpec((k, n), lambda i: (0, 0))],
        out_specs=[pl.BlockSpec((tm, wd), lambda i: (i, 0)) for _, wd in splits],
        out_shape=[jax.ShapeDtypeStruct((m, wd), dt) for (_, wd), dt in zip(splits, dtypes)],
        compiler_params=_cparams(("parallel",)),
    )(a, w)


def _mm_resid_body(*refs, n_pairs):
    a_refs = refs[:n_pairs]
    w_refs = refs[n_pairs:2 * n_pairs]
    x_ref, g_ref, o_ref = refs[2 * n_pairs:]
    acc = jnp.dot(a_refs[0][0], w_refs[0][...], preferred_element_type=F32)
    for a_ref, w_ref in zip(a_refs[1:], w_refs[1:]):
        acc = acc + jnp.dot(a_ref[0], w_ref[...], preferred_element_type=F32)
    o_ref[0] = x_ref[0] + g_ref[...] * acc


def mm_resid(a_list, w_list, xs, mods, gate_idx, n_tiles):
    b, nt, d = xs.shape
    n_pairs = len(a_list)
    in_specs = [pl.BlockSpec((1, ROW_TILE, a.shape[-1]), lambda i, j: (i, j, 0)) for a in a_list]
    in_specs += [pl.BlockSpec(w.shape, lambda i, j: (0, 0)) for w in w_list]
    in_specs += [pl.BlockSpec((1, ROW_TILE, d), lambda i, j: (i, j, 0)),
                 _mod_spec(gate_idx, d)]
    return pl.pallas_call(
        functools.partial(_mm_resid_body, n_pairs=n_pairs),
        name='mm_resid',
        grid=(b, n_tiles),
        in_specs=in_specs,
        out_specs=pl.BlockSpec((1, ROW_TILE, d), lambda i, j: (i, j, 0)),
        out_shape=jax.ShapeDtypeStruct(xs.shape, F32),
        input_output_aliases={2 * n_pairs: 0},
        compiler_params=_cparams(("parallel", "parallel")),
    )(*a_list, *w_list, xs, mods)


def _mod_spec(idx, d, b0=0):
    return pl.BlockSpec((None, None, None, 1, d), lambda i, j: (i + b0, 1 - j // N_LAT_TILES, idx, 0, 0))


def adaln_table(c, c_ctx, w, b):
    cv = jax.nn.silu(jnp.concatenate([c, c_ctx[None, :]], axis=0))
    cv = jnp.pad(cv, ((0, 24 - cv.shape[0]), (0, 0)))
    m = pmatmul(cv, w, exact=True, tn=1536)[:BATCH + 1] + b
    per_sample = m[:BATCH]
    ctx_row = jnp.broadcast_to(m[BATCH][None, :], per_sample.shape)
    return jnp.stack([ctx_row, per_sample], axis=1).reshape(BATCH, 2, 6, 1, D_MODEL)


def _norm_mod(x, w, shift, scale):
    ms = jnp.mean(x * x, axis=-1, keepdims=True)
    return (x * lax.rsqrt(ms + NORM_EPS) * w) * (1.0 + scale) + shift


def _norm_mod_body(x_ref, w_ref, sh_ref, sc_ref, o_ref):
    o_ref[0] = _norm_mod(x_ref[0], w_ref[...], sh_ref[...], sc_ref[...]).astype(o_ref.dtype)


def norm_mod(xs, w, mods, shift_idx, scale_idx):
    b, nt, d = xs.shape
    return pl.pallas_call(
        _norm_mod_body,
        name='norm_mod',
        grid=(b, nt // ROW_TILE),
        in_specs=[pl.BlockSpec((1, ROW_TILE, d), lambda i, j: (i, j, 0)),
                  pl.BlockSpec((1, d), lambda i, j: (0, 0)),
                  _mod_spec(shift_idx, d), _mod_spec(scale_idx, d)],
        out_specs=pl.BlockSpec((1, ROW_TILE, d), lambda i, j: (i, j, 0)),
        out_shape=jax.ShapeDtypeStruct(xs.shape, BF16),
        compiler_params=_cparams(("parallel", "parallel")),
    )(xs, w.reshape(1, d), mods, mods)


def _final_norm_body(x_ref, w_ref, o_ref):
    x = x_ref[0]
    ms = jnp.mean(x * x, axis=-1, keepdims=True)
    o_ref[0] = x * lax.rsqrt(ms + NORM_EPS) * w_ref[...]


def final_norm(xs, w):
    b, _, d = xs.shape
    return pl.pallas_call(
        _final_norm_body,
        name='final_norm',
        grid=(b, N_LAT_TILES),
        in_specs=[pl.BlockSpec((1, ROW_TILE, d), lambda i, j: (i, j, 0)),
                  pl.BlockSpec((1, d), lambda i, j: (0, 0))],
        out_specs=pl.BlockSpec((1, ROW_TILE, d), lambda i, j: (i, j, 0)),
        out_shape=jax.ShapeDtypeStruct((b, SEQ, d), F32),
        compiler_params=_cparams(("parallel", "parallel")),
    )(xs, w.reshape(1, d))


def _route_body(x_ref, w_ref, sh_ref, sc_ref, rw_ref, rb_ref, h_ref, idx_ref, wsel_ref, rank_ref, cnt_ref,
                *, group_size):
    first = (pl.program_id(0) % group_size == 0) & (pl.program_id(1) == 0)

    @pl.when(first)
    def _():
        cnt_ref[...] = jnp.zeros_like(cnt_ref)

    h = _norm_mod(x_ref[0], w_ref[...], sh_ref[...], sc_ref[...])
    h_ref[0] = h.astype(h_ref.dtype)
    logits = jnp.dot(h, rw_ref[...], preferred_element_type=F32, precision=HIGHEST)
    scores = 1.0 / (1.0 + jnp.exp(-logits))
    tm, ne = scores.shape
    lane = lax.broadcasted_iota(jnp.int32, (tm, ne), 1).astype(F32)
    slot = lax.broadcasted_iota(jnp.int32, (tm, MOE_TOPK), 1)
    sel = scores + rb_ref[...]
    picked = jnp.zeros((tm, ne), F32)
    hits = []
    idx_out = jnp.zeros((tm, MOE_TOPK), F32)
    w_out = jnp.zeros((tm, MOE_TOPK), F32)
    for k in range(MOE_TOPK):
        m = jnp.max(sel, axis=-1, keepdims=True)
        ik = jnp.min(jnp.where(sel == m, lane, float(ne)), axis=-1, keepdims=True)
        hit = lane == ik
        wk = jnp.sum(jnp.where(hit, scores, 0.0), axis=-1, keepdims=True)
        sel = jnp.where(hit, -jnp.inf, sel)
        picked = picked + hit.astype(F32)
        hits.append(hit)
        idx_out = jnp.where(slot == k, ik, idx_out)
        w_out = jnp.where(slot == k, wk, w_out)
    wsum = jnp.sum(w_out, axis=-1, keepdims=True)
    wsel_ref[0] = w_out / wsum * MOE_SCALE
    idx_ref[0] = idx_out.astype(jnp.int32)
    r_i = lax.broadcasted_iota(jnp.int32, (tm, tm), 0)
    c_i = lax.broadcasted_iota(jnp.int32, (tm, tm), 1)
    strict_lower = (c_i < r_i).astype(BF16)
    before = jnp.dot(strict_lower, picked.astype(BF16), preferred_element_type=F32) + cnt_ref[...]
    rank_out = jnp.zeros((tm, MOE_TOPK), F32)
    for k in range(MOE_TOPK):
        rk = jnp.sum(jnp.where(hits[k], before, 0.0), axis=-1, keepdims=True)
        rank_out = jnp.where(slot == k, rk, rank_out)
    rank_ref[0] = rank_out.astype(jnp.int32)
    cnt_ref[...] = cnt_ref[...] + jnp.sum(picked, axis=0, keepdims=True)


def route(xs, w, mods, router_w, router_b, n_tiles, group_size):
    b, _, d = xs.shape
    rows = n_tiles * ROW_TILE
    small = lambda dt: jax.ShapeDtypeStruct((b, rows, MOE_TOPK), dt)
    small_spec = pl.BlockSpec((1, ROW_TILE, MOE_TOPK), lambda i, j: (i, j, 0))
    return pl.pallas_call(
        functools.partial(_route_body, group_size=group_size),
        name='route',
        grid=(b, n_tiles),
        in_specs=[pl.BlockSpec((1, ROW_TILE, d), lambda i, j: (i, j, 0)),
                  pl.BlockSpec((1, d), lambda i, j: (0, 0)),
                  _mod_spec(3, d), _mod_spec(4, d),
                  pl.BlockSpec((d, MOE_EXPERTS), lambda i, j: (0, 0)),
                  pl.BlockSpec((1, MOE_EXPERTS), lambda i, j: (0, 0))],
        out_specs=[pl.BlockSpec((1, ROW_TILE, d), lambda i, j: (i, j, 0)),
                   small_spec, small_spec, small_spec,
                   pl.BlockSpec((None, 1, MOE_EXPERTS), lambda i, j: (i // group_size, 0, 0))],
        out_shape=[jax.ShapeDtypeStruct((b, rows, d), BF16), small(jnp.int32), small(F32), small(jnp.int32),
                   jax.ShapeDtypeStruct((b // group_size, 1, MOE_EXPERTS), F32)],
        compiler_params=_cparams(("arbitrary", "arbitrary")),
    )(xs, w.reshape(1, d), mods, mods, router_w, router_b.reshape(1, MOE_EXPERTS))


def _route_t_body(x_ref, w_ref, sh_ref, sc_ref, rwh_ref, rwl_ref, rb_ref, h_ref, idx_ref, wsel_ref, rank_ref, cnt_ref,
                  *, group_size):
    first = (pl.program_id(0) % group_size == 0) & (pl.program_id(1) == 0)

    @pl.when(first)
    def _():
        cnt_ref[...] = jnp.zeros_like(cnt_ref)

    h = _norm_mod(x_ref[0], w_ref[...], sh_ref[...], sc_ref[...])
    h_hi = h.astype(BF16)
    h_ref[0] = h_hi
    h_lo = (h - h_hi.astype(F32)).astype(BF16)
    nt = (((1,), (1,)), ((), ()))
    logits = (lax.dot_general(rwh_ref[...], h_hi, nt, preferred_element_type=F32)
              + lax.dot_general(rwh_ref[...], h_lo, nt, preferred_element_type=F32)
              + lax.dot_general(rwl_ref[...], h_hi, nt, preferred_element_type=F32))
    scores = 1.0 / (1.0 + jnp.exp(-logits))
    ne, tm = scores.shape
    expert = lax.broadcasted_iota(jnp.int32, (ne, tm), 0).astype(F32)
    slot = lax.broadcasted_iota(jnp.int32, (MOE_TOPK, tm), 0)
    sel = scores + rb_ref[...]
    picked = jnp.zeros((ne, tm), F32)
    hits = []
    idx_out = jnp.zeros((MOE_TOPK, tm), F32)
    w_out = jnp.zeros((MOE_TOPK, tm), F32)
    for k in range(MOE_TOPK):
        m = jnp.max(sel, axis=0, keepdims=True)
        ik = jnp.min(jnp.where(sel == m, expert, float(ne)), axis=0, keepdims=True)
        hit = expert == ik
        wk = jnp.sum(jnp.where(hit, scores, 0.0), axis=0, keepdims=True)
        sel = jnp.where(hit, -jnp.inf, sel)
        picked = picked + hit.astype(F32)
        hits.append(hit)
        idx_out = jnp.where(slot == k, ik, idx_out)
        w_out = jnp.where(slot == k, wk, w_out)
    wsum = jnp.sum(w_out, axis=0, keepdims=True)
    wsel_ref[0] = w_out / wsum * MOE_SCALE
    idx_ref[0] = idx_out.astype(jnp.int32)
    r_i = lax.broadcasted_iota(jnp.int32, (tm, tm), 0)
    c_i = lax.broadcasted_iota(jnp.int32, (tm, tm), 1)
    earlier = (r_i < c_i).astype(BF16)
    before = jnp.dot(picked.astype(BF16), earlier, preferred_element_type=F32) + cnt_ref[...]
    rank_out = jnp.zeros((MOE_TOPK, tm), F32)
    for k in range(MOE_TOPK):
        rk = jnp.sum(jnp.where(hits[k], before, 0.0), axis=0, keepdims=True)
        rank_out = jnp.where(slot == k, rk, rank_out)
    rank_ref[0] = rank_out.astype(jnp.int32)
    cnt_ref[...] = cnt_ref[...] + jnp.sum(picked, axis=1, keepdims=True)


def route_t(xs, w, mods, router_w, router_b, n_tiles, group_size):
    b, _, d = xs.shape
    rows = n_tiles * ROW_TILE
    rwt = router_w.T.astype(F32)
    rwt_hi, rwt_lo = _split_bf16(rwt)
    small = lambda dt: jax.ShapeDtypeStruct((b, MOE_TOPK, rows), dt)
    small_spec = pl.BlockSpec((1, MOE_TOPK, ROW_TILE), lambda i, j: (i, 0, j))
    return pl.pallas_call(
        functools.partial(_route_t_body, group_size=group_size),
        name='route',
        grid=(b, n_tiles),
        in_specs=[pl.BlockSpec((1, ROW_TILE, d), lambda i, j: (i, j, 0)),
                  pl.BlockSpec((1, d), lambda i, j: (0, 0)),
                  _mod_spec(3, d), _mod_spec(4, d),
                  pl.BlockSpec((MOE_EXPERTS, d), lambda i, j: (0, 0)),
                  pl.BlockSpec((MOE_EXPERTS, d), lambda i, j: (0, 0)),
                  pl.BlockSpec((MOE_EXPERTS, 1), lambda i, j: (0, 0))],
        out_specs=[pl.BlockSpec((1, ROW_TILE, d), lambda i, j: (i, j, 0)),
                   small_spec, small_spec, small_spec,
                   pl.BlockSpec((None, MOE_EXPERTS, 1), lambda i, j: (i // group_size, 0, 0))],
        out_shape=[jax.ShapeDtypeStruct((b, rows, d), BF16), small(jnp.int32), small(F32), small(jnp.int32),
                   jax.ShapeDtypeStruct((b // group_size, MOE_EXPERTS, 1), F32)],
        compiler_params=_cparams(("arbitrary", "arbitrary")),
    )(xs, w.reshape(1, d), mods, mods, rwt_hi, rwt_lo, router_b.astype(F32).reshape(MOE_EXPERTS, 1))


def _swiglu(x, wg, wu, wd):
    g = jnp.dot(x, wg, preferred_element_type=F32)
    u = jnp.dot(x, wu, preferred_element_type=F32)
    h = (_silu(g) * u).astype(BF16)
    return jnp.dot(h, wd, preferred_element_type=F32)


def _expert_body(be_ref, nu_ref, x_ref, wg_ref, wu_ref, wd_ref, o_ref, wg_s, wu_s, wd_s):
    i = pl.program_id(0)
    used = i < nu_ref[0]

    @pl.when(used & ((i == 0) | (be_ref[i] != be_ref[jnp.maximum(i - 1, 0)])))
    def _():
        wg_s[...] = wg_ref[0].astype(BF16)
        wu_s[...] = wu_ref[0].astype(BF16)
        wd_s[...] = wd_ref[0].astype(BF16)

    @pl.when(used)
    def _():
        o_ref[...] = _swiglu(x_ref[...], wg_s[...], wu_s[...], wd_s[...]).astype(o_ref.dtype)

    @pl.when(jnp.logical_not(used))
    def _():
        o_ref[...] = jnp.zeros_like(o_ref)


def moe_experts(x_rows, block_e, n_used, wg, wu, wd, layer):
    rows, d = x_rows.shape
    n_blocks = rows // MOE_BLOCK
    f = wg.shape[-1]
    grid_spec = pltpu.PrefetchScalarGridSpec(
        num_scalar_prefetch=2,
        grid=(n_blocks,),
        in_specs=[
            pl.BlockSpec((MOE_BLOCK, d), lambda i, be, nu: (i, 0)),
            pl.BlockSpec((None, 1, d, f), lambda i, be, nu: (layer, be[i], 0, 0)),
            pl.BlockSpec((None, 1, d, f), lambda i, be, nu: (layer, be[i], 0, 0)),
            pl.BlockSpec((None, 1, f, d), lambda i, be, nu: (layer, be[i], 0, 0)),
        ],
        out_specs=pl.BlockSpec((MOE_BLOCK, d), lambda i, be, nu: (i, 0)),
        scratch_shapes=[pltpu.VMEM((d, f), BF16), pltpu.VMEM((d, f), BF16), pltpu.VMEM((f, d), BF16)],
    )
    return pl.pallas_call(
        _expert_body,
        name='experts',
        grid_spec=grid_spec,
        out_shape=jax.ShapeDtypeStruct((rows, d), BF16),
        compiler_params=_cparams(("arbitrary",)),
    )(block_e, n_used, x_rows, wg, wu, wd)


COMBINE_SAMPLES = 2


def _shared_resid_body(h_ref, wg_ref, wu_ref, wd_ref, pk_ref, ws_ref, x_ref, g_ref, o_ref):
    for s in range(COMBINE_SAMPLES):
        y = _swiglu(h_ref[s], wg_ref[...], wu_ref[...], wd_ref[...])
        ws = ws_ref[s]
        for k in range(MOE_TOPK):
            y = y + ws[:, k:k + 1] * pk_ref[k, s].astype(F32)
        o_ref[s] = x_ref[s] + g_ref[s] * y


def shared_resid(h, picked, wsel, xs, mods, wg, wu, wd, n_tiles, b0):
    b, _, d = h.shape
    f = wg.shape[-1]
    ns = COMBINE_SAMPLES
    s0 = b0 // ns
    tile = pl.BlockSpec((ns, ROW_TILE, d), lambda i, j: (i, j, 0))
    xs_tile = pl.BlockSpec((ns, ROW_TILE, d), lambda i, j: (i + s0, j, 0))
    gate = pl.BlockSpec((ns, None, None, 1, d), lambda i, j: (i + s0, 1 - j // N_LAT_TILES, 5, 0, 0))
    return pl.pallas_call(
        _shared_resid_body,
        name='shared_resid',
        grid=(b // ns, n_tiles),
        in_specs=[tile,
                  pl.BlockSpec((d, f), lambda i, j: (0, 0)),
                  pl.BlockSpec((d, f), lambda i, j: (0, 0)),
                  pl.BlockSpec((f, d), lambda i, j: (0, 0)),
                  pl.BlockSpec((MOE_TOPK, ns, ROW_TILE, d), lambda i, j: (0, i, j, 0)),
                  pl.BlockSpec((ns, ROW_TILE, MOE_TOPK), lambda i, j: (i, j, 0)),
                  xs_tile, gate],
        out_specs=xs_tile,
        out_shape=jax.ShapeDtypeStruct(xs.shape, F32),
        input_output_aliases={6: 0},
        compiler_params=_cparams(("parallel", "parallel")),
    )(h, wg, wu, wd, picked, wsel, xs, mods)


MOE_GROUPS = 2


def moe_layer(xs, norm_w, mods, router_w, router_b, w_gate, w_up, w_down, sh_gate, sh_up, sh_down, n_tiles, layer):
    bsz, _, d = xs.shape
    shared_w = (sh_gate.astype(BF16), sh_up.astype(BF16), sh_down.astype(BF16))
    b = bsz // MOE_GROUPS
    h, idx, wsel, rank, counts = route_t(xs, norm_w, mods, router_w, router_b, n_tiles, b)
    h_flat = h.reshape(-1, d)
    wsel = jnp.swapaxes(wsel, 1, 2)
    for g in range(MOE_GROUPS):
        sl = slice(g * b, (g + 1) * b)
        xs = _moe_group(xs, mods, h_flat, h[sl], idx[sl], wsel[sl], rank[sl], counts[g, :, 0], w_gate, w_up, w_down,
                        shared_w, n_tiles, layer, g * b)
    return xs


def _moe_group(xs, mods, h_flat, h, idx, wsel, rank, counts, w_gate, w_up, w_down, shared_w, n_tiles, layer, b0):
    b, rows_per_sample, d = h.shape
    n = b * rows_per_sample
    counts = counts.astype(jnp.int32)
    padded = (counts + MOE_BLOCK - 1) // MOE_BLOCK * MOE_BLOCK
    ends = jnp.cumsum(padded)
    starts = ends - padded
    nk = n * MOE_TOPK
    n_blocks = -(-nk // MOE_BLOCK) + MOE_EXPERTS
    rows = n_blocks * MOE_BLOCK
    n_pad = rows - nk
    e_iota = jnp.arange(MOE_EXPERTS, dtype=jnp.int32)
    dest = jnp.sum(jnp.where(idx[..., None] == e_iota, starts, 0), axis=-1) + rank
    blk_start = jnp.arange(n_blocks, dtype=jnp.int32) * MOE_BLOCK
    block_e = jnp.minimum(jnp.sum(ends[None, :] <= blk_start[:, None], axis=1), MOE_EXPERTS - 1).astype(jnp.int32)
    n_used = (ends[-1:] // MOE_BLOCK).astype(jnp.int32)
    pad = padded - counts
    cum_pad = jnp.cumsum(pad)
    m = jnp.arange(n_pad, dtype=jnp.int32)
    e_m = jnp.sum(cum_pad[None, :] <= m[:, None], axis=1)
    base = jnp.sum(jnp.where(jnp.minimum(e_m, MOE_EXPERTS - 1)[:, None] == e_iota,
                             starts + counts - (cum_pad - pad), 0), axis=1)
    pad_row = jnp.where(e_m < MOE_EXPERTS, base + m, ends[-1] + m - cum_pad[-1])
    tok0 = b0 * rows_per_sample
    tok = (tok0 + jnp.arange(b, dtype=jnp.int32)[:, None, None] * rows_per_sample
           + jnp.arange(rows_per_sample, dtype=jnp.int32)[None, None, :])
    tok = jnp.broadcast_to(tok, dest.shape).reshape(-1)
    _, row_tok = lax.sort((jnp.concatenate([dest.reshape(-1), pad_row]).astype(jnp.int32),
                           jnp.concatenate([tok, tok0 + m % n])), num_keys=1)
    x_rows = h_flat[row_tok]
    y_rows = moe_experts(x_rows, block_e, n_used, w_gate, w_up, w_down, layer)
    picked = y_rows[jnp.swapaxes(dest, 0, 1)]
    return shared_resid(h, picked, wsel, xs, mods, *shared_w, n_tiles, b0)


def _dwconv_body(x_ref, w_ref, b_ref, o_ref, *, width, act):
    chunk = ROW_TILE
    n_chunks = NTOK // chunk
    first_of_seq = (0, N_LAT_TILES)
    last_of_seq = (N_LAT_TILES - 1, n_chunks - 1)
    tc = x_ref.shape[-1]
    halo = 16
    row = lax.broadcasted_iota(jnp.int32, (chunk, tc), 0)
    zero_row = jnp.zeros((1, tc), F32)
    for c in range(n_chunks):
        r0 = c * chunk
        cur = x_ref[0, r0:r0 + chunk, :].astype(F32)
        if c in first_of_seq:
            prev_last = zero_row
        else:
            prev_last = x_ref[0, r0 - halo:r0, :].astype(F32)[halo - 1:halo, :]
        if c in last_of_seq:
            next0 = next1 = zero_row
        else:
            nxt = x_ref[0, r0 + chunk:r0 + chunk + halo, :].astype(F32)
            next0, next1 = nxt[0:1, :], nxt[1:2, :]
        xm1 = jnp.where(row == 0, prev_last, pltpu.roll(cur, 1, 0))
        xp1 = jnp.where(row == chunk - 1, next0, pltpu.roll(cur, chunk - 1, 0))
        y = w_ref[0:1, :] * xm1 + w_ref[1:2, :] * cur + w_ref[2:3, :] * xp1 + b_ref[...]
        if width == 4:
            xp2 = jnp.where(row == chunk - 2, next0,
                            jnp.where(row == chunk - 1, next1, pltpu.roll(cur, chunk - 2, 0)))
            y = y + w_ref[3:4, :] * xp2
        if act:
            y = _silu(y)
        o_ref[0, c * chunk:(c + 1) * chunk, :] = y.astype(o_ref.dtype)


def dwconv_stream(x, w, b, act, tc=256):
    bsz, nt, c = x.shape
    width = w.shape[0]
    return pl.pallas_call(
        functools.partial(_dwconv_body, width=width, act=act),
        name='dwconv',
        grid=(bsz, c // tc),
        in_specs=[pl.BlockSpec((1, nt, tc), lambda i, j: (i, 0, j)),
                  pl.BlockSpec((width, tc), lambda i, j: (0, j)),
                  pl.BlockSpec((1, tc), lambda i, j: (0, j))],
        out_specs=pl.BlockSpec((1, nt, tc), lambda i, j: (i, 0, j)),
        out_shape=jax.ShapeDtypeStruct(x.shape, BF16),
        compiler_params=_cparams(("parallel", "parallel")),
    )(x, w, b.reshape(1, c))


HY_FB = 512


def dft_matrices(n):
    k = jnp.arange(n, dtype=jnp.int32)[:, None]
    t = jnp.arange(n, dtype=jnp.int32)[None, :]
    ang = (2.0 * math.pi / (2 * n)) * ((k * t) % (2 * n)).astype(F32)
    fre = jnp.cos(ang)
    fim = -jnp.sin(ang)
    nyq = jnp.where(t % 2 == 0, 1.0, -1.0).astype(F32)
    fim = jnp.where(k == 0, nyq, fim)
    fwd = jnp.concatenate([fre, fim], axis=0)
    colscale = jnp.where(jnp.arange(2 * n) % n == 0, 0.5, 1.0) / n
    inv = fwd.T * colscale[None, :]
    return fwd, inv


def hyena_filter_taps(n, fw0, fb0, fw1, fb1, fw2, fb2, fw3, freq):
    pos = jnp.arange(n, dtype=F32)
    t = pos / max(n - 1, 1)
    bands = jnp.linspace(1e-4, HY_BANDS - 1, HY_BANDS, dtype=F32)
    ang = (2.0 * math.pi / n) * pos[:, None] * bands[None, :]
    feats = jnp.concatenate([t[:, None], jnp.cos(ang), -jnp.sin(ang)], axis=-1)
    h = jnp.sin(freq * (jnp.dot(feats, fw0, precision=HIGHEST) + fb0))
    h = jnp.sin(freq * (jnp.dot(h, fw1, precision=HIGHEST) + fb1))
    h = jnp.sin(freq * (jnp.dot(h, fw2, precision=HIGHEST) + fb2))
    h = pmatmul(h, fw3, exact=True).reshape(n, 2, HY_ORDER, D_HY)
    deltas = jnp.abs(jnp.linspace(math.log(HY_DECAY_PCT_LO) / HY_DECAY_TARGET,
                                  math.log(HY_DECAY_PCT_HI) / HY_DECAY_TARGET, D_HY, dtype=F32))
    h = h * jnp.exp(-t[:, None] * deltas)[:, None, None, :]
    h0 = h[:, 0]
    h1 = h[:, 1].at[0].set(0.0)
    norm = jnp.sum(jnp.abs(h0), axis=0, keepdims=True) + jnp.sum(jnp.abs(h1), axis=0, keepdims=True)
    h0 = (h0 / norm).reshape(n, HY_ORDER * D_HY)
    h1 = (h1 / norm).reshape(n, HY_ORDER * D_HY)
    return h0 + h1, h0 - h1


def _split_bf16(a):
    hi = a.astype(BF16)
    return hi, (a - hi.astype(F32)).astype(BF16)


def hyena_spectrum(fwd, hsum, hdiff, fb):
    n = hsum.shape[0]
    f_hi, f_lo = _split_bf16(fwd)

    def dft(h):
        h_hi, h_lo = _split_bf16(h)
        return pmatmul(f_hi, h_hi) + pmatmul(f_hi, h_lo) + pmatmul(f_lo, h_hi)

    a = dft(hsum)
    bm = dft(hdiff)
    sr = a[:n]
    si = bm[n:]
    nyq = a[n]
    first = (jnp.arange(n) == 0)[:, None]
    p = sr
    q = jnp.where(first, 0.0, si)
    s = jnp.where(first, nyq[None, :], sr)
    spec = jnp.stack([p, q, s], axis=0).reshape(3, n // fb, fb, HY_ORDER, D_HY)
    return spec.transpose(3, 1, 0, 2, 4), f_hi


def _hyena_body(u_ref, fre_ref, fim_ref, gre_ref, gim_ref, sp_ref, bias_ref, prev_ref, o_ref,
                vin, acc, *, nf):
    del prev_ref
    o = pl.program_id(1)
    f = pl.program_id(2)
    c = D_HY

    @pl.when((o == 0) & (f == 0))
    def _():
        vin[...] = u_ref[0, :, 0:c]

    @pl.when(f == 0)
    def _():
        acc[...] = jnp.zeros_like(acc)

    v = vin[...]
    vr = jnp.dot(fre_ref[...], v, preferred_element_type=F32)
    vi = jnp.dot(fim_ref[...], v, preferred_element_type=F32)
    p, q, s = sp_ref[0], sp_ref[1], sp_ref[2]
    zr = (vr * p - vi * q).astype(BF16)
    zi = (vr * q + vi * s).astype(BF16)
    acc[...] += (jnp.dot(gre_ref[...], zr, preferred_element_type=F32)
                 + jnp.dot(gim_ref[...], zi, preferred_element_type=F32))

    @pl.when((o == 0) & (f == nf - 1))
    def _():
        z = u_ref[0, :, c:2 * c].astype(F32) * (acc[...] + bias_ref[0:1, :] * vin[...].astype(F32))
        vin[...] = z.astype(BF16)

    @pl.when((o == 1) & (f == nf - 1))
    def _():
        y = u_ref[0, :, 2 * c:3 * c].astype(F32) * (acc[...] + bias_ref[1:2, :] * vin[...].astype(F32))
        o_ref[0] = y.astype(o_ref.dtype)


def hyena_long_conv(u, fwd_bf16, inv_bf16, spec, bias, n, row_block, prev_out):
    bsz = u.shape[0]
    fb = spec.shape[3]
    nf = n // fb
    out_shape = jax.ShapeDtypeStruct((bsz, NTOK, D_HY), BF16)
    if prev_out is None:
        prev_out = jnp.zeros(out_shape.shape, BF16)
    args = [u, fwd_bf16, fwd_bf16, inv_bf16, inv_bf16, spec, bias, prev_out]
    aliases = {7: 0}
    return pl.pallas_call(
        functools.partial(_hyena_body, nf=nf),
        name='hyena',
        grid=(bsz, HY_ORDER, nf),
        in_specs=[pl.BlockSpec((1, n, 3 * D_HY), lambda b, o, f: (b, row_block, 0)),
                  pl.BlockSpec((fb, n), lambda b, o, f: (f, 0)),
                  pl.BlockSpec((fb, n), lambda b, o, f: (nf + f, 0)),
                  pl.BlockSpec((n, fb), lambda b, o, f: (0, f)),
                  pl.BlockSpec((n, fb), lambda b, o, f: (0, nf + f)),
                  pl.BlockSpec((None, None, 3, fb, D_HY), lambda b, o, f: (o, f, 0, 0, 0)),
                  pl.BlockSpec((HY_ORDER, D_HY), lambda b, o, f: (0, 0)),
                  pl.BlockSpec(memory_space=pl.ANY)],
        out_specs=pl.BlockSpec((1, n, D_HY), lambda b, o, f: (b, row_block, 0)),
        out_shape=out_shape,
        scratch_shapes=[pltpu.VMEM((n, D_HY), BF16), pltpu.VMEM((n, D_HY), F32)],
        input_output_aliases=aliases,
        compiler_params=_cparams(("parallel", "arbitrary", "arbitrary"), VMEM_LIMIT_BIG),
    )(*args)


def hyena_mixer_stream(p_hy, conv_w, conv_b, filt, bias):
    u = dwconv_stream(p_hy, conv_w, conv_b, act=False)
    out = None
    for n, row_block in ((SEQ, 0), (CTX_LEN, SEQ // CTX_LEN)):
        fb = min(HY_FB, n)
        fwd, inv = dft_matrices(n)
        hsum, hdiff = hyena_filter_taps(n, *filt)
        spec, fwd_bf16 = hyena_spectrum(fwd, hsum, hdiff, fb)
        out = hyena_long_conv(u, fwd_bf16, inv.astype(BF16), spec, bias, n, row_block, out)
    return out


SCAN_SAMPLES = 2


def _tri(n, kind):
    r = lax.broadcasted_iota(jnp.int32, (n, n), 0)
    c = lax.broadcasted_iota(jnp.int32, (n, n), 1)
    return (c <= r) if kind == 'lower' else (c >= r)


def _ssd_dir(xbc_ref, dt_ref, dtt_ref, bias_r, bias_c, a_r, a_c, st_ref, y_ref, *, s, d, reverse):
    q = SSD_CHUNK
    nh = SSD_H
    gw = SSD_HPG * SSD_P
    lower = _tri(q, 'lower')
    upper = _tri(q, 'upper')
    lower_f = lower.astype(F32)
    upper_f = upper.astype(F32)
    dt_col = _softplus(dt_ref[s] + bias_r)
    dt_row = _softplus(dtt_ref[s] + bias_c)
    da_col = dt_col * a_r
    da_row = dt_row * a_c
    if not reverse:
        acs_col = jnp.dot(lower_f, da_col, preferred_element_type=F32, precision=HIGHEST)
        acs_row = jnp.dot(da_row, upper_f, preferred_element_type=F32, precision=HIGHEST)
        mask = lower
        edge = q - 1
    else:
        acs_col = jnp.dot(upper_f, da_col, preferred_element_type=F32, precision=HIGHEST)
        acs_row = jnp.dot(da_row, lower_f, preferred_element_type=F32, precision=HIGHEST)
        mask = upper
        edge = 0
    h0 = d * nh
    hh = lax.broadcasted_iota(jnp.int32, (2 * nh, nh * SSD_P), 0)
    cc = lax.broadcasted_iota(jnp.int32, (2 * nh, nh * SSD_P), 1) // SSD_P
    expand = (hh == cc + h0).astype(F32)
    acs_c = jnp.dot(acs_col, expand, preferred_element_type=F32, precision=HIGHEST)
    dt_c = jnp.dot(dt_col, expand, preferred_element_type=F32, precision=HIGHEST)
    total_c = acs_c[edge:edge + 1, :]
    e_in_c = jnp.exp(acs_c)
    w_end_c = jnp.exp(total_c - acs_c) * dt_c
    dec_c = jnp.exp(total_c)
    xs = xbc_ref[s, :, 0:D_SSM]
    xs_f = xs.astype(F32)
    for g in range(SSD_G):
        bm = xbc_ref[s, :, D_SSM + g * SSD_N:D_SSM + (g + 1) * SSD_N]
        cm = xbc_ref[s, :, D_SSM + SSD_G * SSD_N + g * SSD_N:D_SSM + SSD_G * SSD_N + (g + 1) * SSD_N]
        cb = lax.dot_general(cm, bm, (((1,), (1,)), ((), ())), preferred_element_type=F32)
        lws = []
        for k in range(SSD_HPG):
            h = h0 + g * SSD_HPG + k
            seg = acs_col[:, h:h + 1] - acs_row[h:h + 1, :]
            decay = jnp.exp(jnp.where(mask, seg, -jnp.inf))
            lws.append((cb * decay * dt_row[h:h + 1, :]).astype(BF16))
        lw = jnp.concatenate(lws, axis=1)
        xg = xs[:, g * gw:(g + 1) * gw]
        rb = lax.broadcasted_iota(jnp.int32, (SSD_HPG * q, gw), 0) // q
        cbk = lax.broadcasted_iota(jnp.int32, (SSD_HPG * q, gw), 1) // SSD_P
        x_bd = jnp.where(rb == cbk, jnp.concatenate([xg] * SSD_HPG, axis=0), jnp.zeros((), BF16))
        y_in = jnp.dot(lw, x_bd, preferred_element_type=F32)
        st = st_ref[s, g]
        y_st = jnp.dot(cm, st.astype(BF16), preferred_element_type=F32) * e_in_c[:, g * gw:(g + 1) * gw]
        y_ref[s, :, g * gw:(g + 1) * gw] = y_in + y_st
        xw = (xs_f[:, g * gw:(g + 1) * gw] * w_end_c[:, g * gw:(g + 1) * gw]).astype(BF16)
        upd = lax.dot_general(bm, xw, (((0,), (0,)), ((), ())), preferred_element_type=F32)
        st_ref[s, g] = st * dec_c[:, g * gw:(g + 1) * gw] + upd


def _ssd_body(xf_ref, dtf_ref, dttf_ref, xb_ref, dtb_ref, dttb_ref, bias_r, bias_c, a_r, a_c,
              yf_ref, yb_ref, stf, stb):
    @pl.when(pl.program_id(1) == 0)
    def _():
        stf[...] = jnp.zeros_like(stf)
        stb[...] = jnp.zeros_like(stb)

    for s in range(SCAN_SAMPLES):
        _ssd_dir(xf_ref, dtf_ref, dttf_ref, bias_r[...], bias_c[...], a_r[...], a_c[...], stf, yf_ref,
                 s=s, d=0, reverse=False)
        _ssd_dir(xb_ref, dtb_ref, dttb_ref, bias_r[...], bias_c[...], a_r[...], a_c[...], stb, yb_ref,
                 s=s, d=1, reverse=True)


def ssd_scan(xbc, dt, dt_bias, a_log):
    bsz = xbc.shape[0]
    nc = NTOK // SSD_CHUNK
    nlat = SEQ // SSD_CHUNK
    dtt = jnp.swapaxes(dt, 1, 2)
    fwd_chunk = lambda s: (s + nlat) % nc
    bwd_chunk = lambda s: nc - 1 - s
    a = -jnp.exp(a_log.astype(F32)).reshape(1, 2 * SSD_H)
    bias = dt_bias.astype(F32).reshape(1, 2 * SSD_H)
    ns = SCAN_SAMPLES
    x_spec = lambda cm: pl.BlockSpec((ns, SSD_CHUNK, SSD_XBC), lambda b, s: (b, cm(s), 0))
    dt_spec = lambda cm: pl.BlockSpec((ns, SSD_CHUNK, 2 * SSD_H), lambda b, s: (b, cm(s), 0))
    dtt_spec = lambda cm: pl.BlockSpec((ns, 2 * SSD_H, SSD_CHUNK), lambda b, s: (b, 0, cm(s)))
    y_spec = lambda cm: pl.BlockSpec((ns, SSD_CHUNK, D_SSM), lambda b, s: (b, cm(s), 0))
    row = pl.BlockSpec((1, 2 * SSD_H), lambda b, s: (0, 0))
    col = pl.BlockSpec((2 * SSD_H, 1), lambda b, s: (0, 0))
    y_shape = jax.ShapeDtypeStruct((bsz, NTOK, D_SSM), F32)
    gw = SSD_HPG * SSD_P
    return pl.pallas_call(
        _ssd_body,
        name='ssd_scan',
        grid=(bsz // ns, nc),
        in_specs=[x_spec(fwd_chunk), dt_spec(fwd_chunk), dtt_spec(fwd_chunk),
                  x_spec(bwd_chunk), dt_spec(bwd_chunk), dtt_spec(bwd_chunk),
                  row, col, row, col],
        out_specs=[y_spec(fwd_chunk), y_spec(bwd_chunk)],
        out_shape=[y_shape, y_shape],
        scratch_shapes=[pltpu.VMEM((ns, SSD_G, SSD_N, gw), F32), pltpu.VMEM((ns, SSD_G, SSD_N, gw), F32)],
        compiler_params=_cparams(("parallel", "arbitrary")),
    )(xbc, dt, dtt, xbc, dt, dtt, bias, bias.reshape(-1, 1), a, a.reshape(-1, 1))


def _ssd_merge_body(yf_ref, yb_ref, xbc_ref, z_ref, d_ref, nw_ref, o_ref):
    xs = xbc_ref[0, :, 0:D_SSM].astype(F32)
    z = z_ref[0].astype(F32)
    g = (yf_ref[0] + yb_ref[0] + d_ref[...] * xs) * _silu(z)
    gw = D_SSM // SSD_G
    for k in range(SSD_G):
        gk = g[:, k * gw:(k + 1) * gw]
        ms = jnp.mean(gk * gk, axis=-1, keepdims=True)
        o_ref[0, :, k * gw:(k + 1) * gw] = (gk * lax.rsqrt(ms + NORM_EPS)
                                            * nw_ref[:, k * gw:(k + 1) * gw]).astype(o_ref.dtype)


def ssd_merge(yf, yb, xbc, z, d_skip, norm_w):
    bsz = yf.shape[0]
    tile = lambda w: pl.BlockSpec((1, ROW_TILE, w), lambda i, j: (i, j, 0))
    vec = pl.BlockSpec((1, D_SSM), lambda i, j: (0, 0))
    d_chan = jnp.repeat(d_skip.astype(F32), SSD_P).reshape(1, D_SSM)
    return pl.pallas_call(
        _ssd_merge_body,
        name='ssd_merge',
        grid=(bsz, N_ROW_TILES),
        in_specs=[tile(D_SSM), tile(D_SSM), tile(SSD_XBC), tile(D_SSM), vec, vec],
        out_specs=tile(D_SSM),
        out_shape=jax.ShapeDtypeStruct((bsz, NTOK, D_SSM), BF16),
        compiler_params=_cparams(("parallel", "parallel")),
    )(yf, yb, xbc, z, d_chan, norm_w.reshape(1, D_SSM))


def even_layer_mixer(xs, mods, norm_w, w_in, w_out, hy_conv_w, hy_conv_b, hy_filt, hy_bias,
                     ssd_conv_w, ssd_conv_b, ssd_dt_bias, ssd_a_log, ssd_d, ssd_norm_w):
    bsz = xs.shape[0]
    h = norm_mod(xs, norm_w, mods, 0, 1)
    splits = ((0, HY_IN), (HY_IN, D_SSM), (HY_IN + D_SSM, SSD_XBC), (HY_IN + D_SSM + SSD_XBC, 2 * SSD_H))
    p_hy, z, xbc_raw, dt = mm_split(h.reshape(bsz * NTOK, D_MODEL), w_in.astype(BF16), splits,
                                    (BF16, BF16, BF16, F32))
    to3 = lambda a: a.reshape(bsz, NTOK, a.shape[-1])
    y_hy = hyena_mixer_stream(to3(p_hy), hy_conv_w, hy_conv_b, hy_filt, hy_bias)
    xbc = dwconv_stream(to3(xbc_raw), ssd_conv_w, ssd_conv_b, act=True)
    yf, yb = ssd_scan(xbc, to3(dt), ssd_dt_bias, ssd_a_log)
    s = ssd_merge(yf, yb, xbc, to3(z), ssd_d, ssd_norm_w)
    wo = w_out.astype(BF16)
    return mm_resid([y_hy, s], [wo[:D_HY], wo[D_HY:]], xs, mods, 2, N_ROW_TILES)


GLA_QK = GLA_H * GLA_DK
GLA_V = GLA_H * GLA_DV


def _gla_dir(qkv_ref, lr_ref, gw_ref, gb_ref, st_ref, o_ref, *, s, d, reverse):
    q = GLA_CHUNK
    tri = _tri(q, 'upper' if reverse else 'lower')
    edge = 0 if reverse else q - 1
    lr = lr_ref[s, :, d * GLA_RANK:(d + 1) * GLA_RANK]
    logit = jnp.dot(lr, gw_ref[d], preferred_element_type=F32, precision=HIGHEST) + gb_ref[d:d + 1, :]
    log_g = -_softplus(-logit) * (1.0 / GLA_GATE_NORM)
    gcum = jnp.dot(tri.astype(F32), log_g, preferred_element_type=F32, precision=HIGHEST)
    total = gcum[edge:edge + 1, :]
    qf = qkv_ref[s, :, 0:GLA_QK].astype(F32)
    kf = qkv_ref[s, :, GLA_QK:2 * GLA_QK].astype(F32)
    v = qkv_ref[s, :, 2 * GLA_QK:2 * GLA_QK + GLA_V]
    qg = (qf * (GLA_DK ** -0.5) * jnp.exp(gcum)).astype(BF16)
    kg = (kf * jnp.exp(-gcum)).astype(BF16)
    kw = (kf * jnp.exp(total - gcum)).astype(BF16)
    rb = lax.broadcasted_iota(jnp.int32, (GLA_H * q, GLA_QK), 0) // q
    cb = lax.broadcasted_iota(jnp.int32, (GLA_H * q, GLA_QK), 1) // GLA_DK
    k_bd = jnp.where(rb == cb, jnp.concatenate([kg] * GLA_H, axis=0), jnp.zeros((), BF16))
    att = lax.dot_general(qg, k_bd, (((1,), (1,)), ((), ())), preferred_element_type=F32)
    i_i = lax.broadcasted_iota(jnp.int32, (q, GLA_H * q), 0)
    j_i = lax.broadcasted_iota(jnp.int32, (q, GLA_H * q), 1) % q
    keep = (j_i >= i_i) if reverse else (j_i <= i_i)
    att = jnp.where(keep, att, 0.0).astype(BF16)
    rv = lax.broadcasted_iota(jnp.int32, (GLA_H * q, GLA_V), 0) // q
    cv = lax.broadcasted_iota(jnp.int32, (GLA_H * q, GLA_V), 1) // GLA_DV
    v_bd = jnp.where(rv == cv, jnp.concatenate([v] * GLA_H, axis=0), jnp.zeros((), BF16))
    st = st_ref[s]
    o_in = jnp.dot(att, v_bd, preferred_element_type=F32)
    o_st = lax.dot_general(qg, st.astype(BF16), (((1,), (1,)), ((), ())), preferred_element_type=F32)
    o_ref[s] = o_in + o_st
    upd = lax.dot_general(v, kw, (((0,), (0,)), ((), ())), preferred_element_type=F32)
    rs = lax.broadcasted_iota(jnp.int32, (GLA_V, GLA_QK), 0) // GLA_DV
    cs = lax.broadcasted_iota(jnp.int32, (GLA_V, GLA_QK), 1) // GLA_DK
    st_ref[s] = st * jnp.exp(total) + jnp.where(rs == cs, upd, 0.0)


def _gla_body(qf_ref, lf_ref, qb_ref, lb_ref, gw_ref, gb_ref, of_ref, ob_ref, stf, stb):
    @pl.when(pl.program_id(1) == 0)
    def _():
        stf[...] = jnp.zeros_like(stf)
        stb[...] = jnp.zeros_like(stb)

    for s in range(SCAN_SAMPLES):
        _gla_dir(qf_ref, lf_ref, gw_ref, gb_ref, stf, of_ref, s=s, d=0, reverse=False)
        _gla_dir(qb_ref, lb_ref, gw_ref, gb_ref, stb, ob_ref, s=s, d=1, reverse=True)


def gla_scan(qkv, lr, gate_w, gate_b):
    bsz = qkv.shape[0]
    nc = NTOK // GLA_CHUNK
    nlat = SEQ // GLA_CHUNK
    fwd_chunk = lambda s: (s + nlat) % nc
    bwd_chunk = lambda s: nc - 1 - s
    ns = SCAN_SAMPLES
    q_spec = lambda cm: pl.BlockSpec((ns, GLA_CHUNK, qkv.shape[-1]), lambda b, s: (b, cm(s), 0))
    l_spec = lambda cm: pl.BlockSpec((ns, GLA_CHUNK, 2 * GLA_RANK), lambda b, s: (b, cm(s), 0))
    o_spec = lambda cm: pl.BlockSpec((ns, GLA_CHUNK, GLA_V), lambda b, s: (b, cm(s), 0))
    o_shape = jax.ShapeDtypeStruct((bsz, NTOK, GLA_V), F32)
    return pl.pallas_call(
        _gla_body,
        name='gla_scan',
        grid=(bsz // ns, nc),
        in_specs=[q_spec(fwd_chunk), l_spec(fwd_chunk), q_spec(bwd_chunk), l_spec(bwd_chunk),
                  pl.BlockSpec((2, GLA_RANK, GLA_QK), lambda b, s: (0, 0, 0)),
                  pl.BlockSpec((2, GLA_QK), lambda b, s: (0, 0))],
        out_specs=[o_spec(fwd_chunk), o_spec(bwd_chunk)],
        out_shape=[o_shape, o_shape],
        scratch_shapes=[pltpu.VMEM((ns, GLA_V, GLA_QK), F32), pltpu.VMEM((ns, GLA_V, GLA_QK), F32)],
        compiler_params=_cparams(("parallel", "arbitrary")),
    )(qkv, lr, qkv, lr, gate_w.astype(F32), gate_b.astype(F32))


def _gla_merge_body(of_ref, ob_ref, r_ref, nw_ref, o_ref):
    o = of_ref[0] + ob_ref[0]
    r = r_ref[0].astype(F32)
    for h in range(GLA_H):
        sl = slice(h * GLA_DV, (h + 1) * GLA_DV)
        oh = o[:, sl]
        ms = jnp.mean(oh * oh, axis=-1, keepdims=True)
        o_ref[0, :, sl] = (oh * lax.rsqrt(ms + NORM_EPS) * nw_ref[:, sl] * _silu(r[:, sl])).astype(o_ref.dtype)


def gla_merge_stream(of, ob, r, norm_w):
    bsz = of.shape[0]
    tile = pl.BlockSpec((1, ROW_TILE, GLA_V), lambda i, j: (i, j, 0))
    return pl.pallas_call(
        _gla_merge_body,
        name='gla_merge',
        grid=(bsz, N_LAT_TILES),
        in_specs=[tile, tile, tile, pl.BlockSpec((1, GLA_V), lambda i, j: (0, 0))],
        out_specs=tile,
        out_shape=jax.ShapeDtypeStruct((bsz, SEQ, GLA_V), BF16),
        compiler_params=_cparams(("parallel", "parallel")),
    )(of, ob, r, norm_w.reshape(1, GLA_V))


RG_TILE = 8


def _gelu_tanh(x):
    return 0.5 * x * (1.0 + jnp.tanh(math.sqrt(2.0 / math.pi) * (x + 0.044715 * x * x * x)))


def _rg_scan_block(a_s, x_s, h_s, base, carry, reverse):
    n_tiles = ROW_TILE // RG_TILE
    row = lax.broadcasted_iota(jnp.int32, (RG_TILE, D_RG), 0)

    def tile_step(i, h_prev):
        t = (n_tiles - 1 - i) if reverse else i
        r0 = pl.multiple_of(t * RG_TILE, RG_TILE)
        a = a_s[pl.ds(r0, RG_TILE), :]
        x = x_s[pl.ds(r0, RG_TILE), :]
        for s in (1, 2, 4):
            if reverse:
                ok = row < RG_TILE - s
                shift = RG_TILE - s
            else:
                ok = row >= s
                shift = s
            a_sh = jnp.where(ok, pltpu.roll(a, shift, 0), 1.0)
            x_sh = jnp.where(ok, pltpu.roll(x, shift, 0), 0.0)
            x = a * x_sh + x
            a = a * a_sh
        h = x + a * h_prev
        h_s[pl.ds(base + r0, RG_TILE), :] = h
        edge = 0 if reverse else RG_TILE - 1
        return jnp.broadcast_to(h[edge:edge + 1, :], (RG_TILE, D_RG))

    return lax.fori_loop(0, n_tiles, tile_step, carry)


def _rglru_body(u_ref, g_ref, w_ref, b_ref, c_ref, o_ref, hf_s, a_s, x_s, hb_s):
    n_blocks = NTOK // ROW_TILE
    fwd_order = list(range(N_LAT_TILES, n_blocks)) + list(range(N_LAT_TILES))
    bwd_order = list(range(n_blocks - 1, N_LAT_TILES - 1, -1)) + list(range(N_LAT_TILES - 1, -1, -1))

    def gates(blk, d):
        ub = u_ref[0, blk * ROW_TILE:(blk + 1) * ROW_TILE, :]
        z = jnp.dot(ub, w_ref[:, 2 * d * D_RG:2 * (d + 1) * D_RG], preferred_element_type=F32)
        z = z + b_ref[:, 2 * d * D_RG:2 * (d + 1) * D_RG]
        r = 1.0 / (1.0 + jnp.exp(-z[:, :D_RG]))
        i = 1.0 / (1.0 + jnp.exp(-z[:, D_RG:]))
        a = jnp.exp(c_ref[d:d + 1, :] * r)
        a_s[...] = a
        x_s[...] = jnp.sqrt(1.0 - a * a) * i * ub.astype(F32)

    carry = jnp.zeros((RG_TILE, D_RG), F32)
    for blk in fwd_order:
        gates(blk, 0)
        carry = _rg_scan_block(a_s, x_s, hf_s, blk * ROW_TILE, carry, reverse=False)
    carry = jnp.zeros((RG_TILE, D_RG), F32)
    for blk in bwd_order:
        gates(blk, 1)
        carry = _rg_scan_block(a_s, x_s, hb_s, 0, carry, reverse=True)
        rows = slice(blk * ROW_TILE, (blk + 1) * ROW_TILE)
        gate = g_ref[0, rows, :].astype(F32)
        o_ref[0, rows, :] = ((hf_s[rows, :] + hb_s[...]) * _gelu_tanh(gate)).astype(o_ref.dtype)


def rglru_stream(u, gate, w_a, b_a, w_x, b_x, lam):
    bsz = u.shape[0]
    eye = jnp.eye(RG_BLOCKS, dtype=F32)
    dense = lambda w: jnp.einsum('nio,nm->nimo', w, eye).reshape(D_RG, D_RG)
    w_cat = jnp.concatenate([dense(w_a[0]), dense(w_x[0]), dense(w_a[1]), dense(w_x[1])], axis=1).astype(BF16)
    b_cat = jnp.concatenate([b_a[0], b_x[0], b_a[1], b_x[1]]).astype(F32).reshape(1, 4 * D_RG)
    c = -RG_C * jax.nn.softplus(-lam.astype(F32))
    seq = pl.BlockSpec((1, NTOK, D_RG), lambda i: (i, 0, 0))
    return pl.pallas_call(
        _rglru_body,
        name='rglru',
        grid=(bsz,),
        in_specs=[seq, seq,
                  pl.BlockSpec((D_RG, 4 * D_RG), lambda i: (0, 0)),
                  pl.BlockSpec((1, 4 * D_RG), lambda i: (0, 0)),
                  pl.BlockSpec((2, D_RG), lambda i: (0, 0))],
        out_specs=seq,
        out_shape=jax.ShapeDtypeStruct((bsz, NTOK, D_RG), BF16),
        scratch_shapes=[pltpu.VMEM((NTOK, D_RG), F32), pltpu.VMEM((ROW_TILE, D_RG), F32),
                        pltpu.VMEM((ROW_TILE, D_RG), F32), pltpu.VMEM((ROW_TILE, D_RG), F32)],
        compiler_params=_cparams(("parallel",)),
    )(u, gate, w_cat, b_cat, c)


def odd_layer_mixer_pallas(xs, mods, norm_w, w_in, w_out, gla_args, rg_args):
    bsz = xs.shape[0]
    gate_w, gate_b, gla_norm_w = gla_args
    rg_conv_w, rg_conv_b, w_a, b_a, w_x, b_x, lam = rg_args
    h = norm_mod(xs, norm_w, mods, 0, 1)
    h = jnp.concatenate([to_col_major(h[:, :SEQ]), h[:, SEQ:]], axis=1)
    nqk, nv = GLA_QK, GLA_V
    r0 = 2 * nqk + nv + 2 * GLA_RANK
    w = jnp.concatenate([w_in[:, :2 * nqk + nv], w_in[:, r0:r0 + nv], w_in[:, GLA_IN:],
                         w_in[:, 2 * nqk + nv:r0]], axis=1).astype(BF16)
    qkv_w = 2 * nqk + nv
    splits = ((0, qkv_w), (qkv_w, nv), (qkv_w + nv, D_RG), (qkv_w + nv + D_RG, D_RG),
              (qkv_w + nv + 2 * D_RG, 2 * GLA_RANK))
    qkv, r, u_raw, gate, lr = mm_split(h.reshape(bsz * NTOK, D_MODEL), w, splits, (BF16, BF16, BF16, BF16, F32))
    to3 = lambda a: a.reshape(bsz, NTOK, a.shape[-1])
    of, ob = gla_scan(to3(qkv), to3(lr), gate_w, gate_b.reshape(2, GLA_QK))
    a_l = gla_merge_stream(of, ob, to3(r), gla_norm_w)
    u = dwconv_stream(to3(u_raw), rg_conv_w, rg_conv_b, act=False)
    r_l = rglru_stream(u, to3(gate), w_a, b_a, w_x, b_x, lam)[:, :SEQ]
    wo = w_out.astype(BF16)
    return mm_resid([from_col_major(a_l), from_col_major(r_l)], [wo[:GLA_V], wo[GLA_V:]], xs, mods, 2, N_LAT_TILES)


def rms_norm(x, w):
    xf = x.astype(F32)
    y = xf * lax.rsqrt(jnp.mean(jnp.square(xf), axis=-1, keepdims=True) + NORM_EPS)
    return y.astype(x.dtype) * w


def dwconv(x, w, b):
    y = lax.conv_general_dilated(x, w[:, None, :].astype(x.dtype), window_strides=(1,), padding='SAME',
                                 dimension_numbers=('NWC', 'WIO', 'NWC'), feature_group_count=x.shape[-1])
    return y + b.astype(x.dtype)


def maybe_flip(a, rev):
    return jnp.flip(a, axis=1) if rev else a


def to_col_major(x):
    b, n, d = x.shape
    rows = n // GRID_W
    return x.reshape(b, rows, GRID_W, d).transpose(0, 2, 1, 3).reshape(b, n, d)


def from_col_major(x):
    b, n, d = x.shape
    rows = n // GRID_W
    return x.reshape(b, GRID_W, rows, d).transpose(0, 2, 1, 3).reshape(b, n, d)


def gla_inputs(p, gate_w, gate_b):
    b, n, _ = p.shape
    nqk, nv = GLA_H * GLA_DK, GLA_H * GLA_DV
    q = p[..., :nqk].reshape(b, n, GLA_H, GLA_DK) * GLA_DK ** -0.5
    k = p[..., nqk:2 * nqk].reshape(b, n, GLA_H, GLA_DK)
    v = p[..., 2 * nqk:2 * nqk + nv].reshape(b, n, GLA_H, GLA_DV)
    lr = p[..., 2 * nqk + nv:2 * nqk + nv + 2 * GLA_RANK].reshape(b, n, 2, GLA_RANK)
    r = p[..., 2 * nqk + nv + 2 * GLA_RANK:]
    logit = jnp.einsum('bler,erk->blek', lr, gate_w) + gate_b
    log_g = (jax.nn.log_sigmoid(logit.astype(F32)) / GLA_GATE_NORM).reshape(b, n, 2, GLA_H, GLA_DK)
    return q, k, v, log_g, r


def gla_states(k, v, log_g, s0):
    b, n = k.shape[:2]
    nc = n // GLA_CHUNK
    kc = k.reshape(b, nc, GLA_CHUNK, GLA_H, GLA_DK)
    vc = v.reshape(b, nc, GLA_CHUNK, GLA_H, GLA_DV)
    gcum = jnp.cumsum(log_g.reshape(b, nc, GLA_CHUNK, GLA_H, GLA_DK), axis=2)
    states = jnp.einsum('bcqhd,bcqhv->bchdv', kc * jnp.exp(gcum[:, :, -1:] - gcum), vc)
    chunk_decay = jnp.exp(gcum[:, :, -1])

    def step(s, inp):
        st, dcy = inp
        return dcy[..., None] * s + st, s

    s_fin, s_prev = lax.scan(step, s0, (jnp.moveaxis(states, 1, 0), jnp.moveaxis(chunk_decay, 1, 0)))
    return jnp.moveaxis(s_prev, 0, 1), s_fin


def gla_output(q, k, v, log_g, s_prev):
    b, n = q.shape[:2]
    nc = n // GLA_CHUNK
    qc = q.reshape(b, nc, GLA_CHUNK, GLA_H, GLA_DK)
    kc = k.reshape(b, nc, GLA_CHUNK, GLA_H, GLA_DK)
    vc = v.reshape(b, nc, GLA_CHUNK, GLA_H, GLA_DV)
    gcum = jnp.cumsum(log_g.reshape(b, nc, GLA_CHUNK, GLA_H, GLA_DK), axis=2)
    qg = qc * jnp.exp(gcum)
    kg = kc * jnp.exp(-gcum)
    mask = jnp.tril(jnp.ones((GLA_CHUNK, GLA_CHUNK), bool))
    att = jnp.where(mask, jnp.einsum('bcihd,bcjhd->bchij', qg, kg), 0.0)
    o = jnp.einsum('bchij,bcjhv->bcihv', att, vc) + jnp.einsum('bcihd,bchdv->bcihv', qg, s_prev)
    return o.reshape(b, n, GLA_H, GLA_DV)


def gla_merge(os_, r, norm_w):
    b, n = r.shape[:2]
    o = rms_norm(os_[0] + os_[1], norm_w.reshape(GLA_H, GLA_DV))
    return o.reshape(b, n, GLA_H * GLA_DV) * jax.nn.silu(r)


def gla_mixer(p_c, p_l, gate_w, gate_b, norm_w):
    q_c, k_c, v_c, g_c, r_c = gla_inputs(p_c, gate_w, gate_b)
    q_l, k_l, v_l, g_l, r_l = gla_inputs(p_l, gate_w, gate_b)
    s0 = jnp.zeros((p_c.shape[0], GLA_H, GLA_DK, GLA_DV), F32)
    os_l = []
    for d, rev in enumerate((False, True)):
        f = functools.partial(maybe_flip, rev=rev)
        _, sf_c = gla_states(f(k_c), f(v_c), f(g_c[:, :, d]), s0)
        sp_l, _ = gla_states(f(k_l), f(v_l), f(g_l[:, :, d]), sf_c)
        os_l.append(f(gla_output(f(q_l), f(k_l), f(v_l), f(g_l[:, :, d]), sp_l)))
    return gla_merge(os_l, r_l, norm_w)


def rglru_inputs(p, conv_w, conv_b, w_a, b_a, w_x, b_x, lam):
    b, n, _ = p.shape
    u = dwconv(p[..., :D_RG], conv_w, conv_b)
    ub = u.reshape(b, n, RG_BLOCKS, RG_BW)
    r = jax.nn.sigmoid((jnp.einsum('blni,enio->bleno', ub, w_a).reshape(b, n, 2, D_RG) + b_a).astype(F32))
    i = jax.nn.sigmoid((jnp.einsum('blni,enio->bleno', ub, w_x).reshape(b, n, 2, D_RG) + b_x).astype(F32))
    log_a = -RG_C * jax.nn.softplus(-lam.astype(F32)) * r
    x_in = jnp.sqrt(-jnp.expm1(2.0 * log_a)) * i * u[:, :, None, :].astype(F32)
    return p[..., D_RG:], jnp.exp(log_a), x_in


def lru_scan(a, u, h0):
    u = u.at[:, 0].add(a[:, 0] * h0)

    def combine(lhs, rhs):
        a1, b1 = lhs
        a2, b2 = rhs
        return a1 * a2, a2 * b1 + b2

    return lax.associative_scan(combine, (a, u), axis=1)[1]


def rglru_mixer(p_c, p_l, conv_w, conv_b, w_a, b_a, w_x, b_x, lam):
    gb_c, a_c, u_c = rglru_inputs(p_c, conv_w, conv_b, w_a, b_a, w_x, b_x, lam)
    gb_l, a_l, u_l = rglru_inputs(p_l, conv_w, conv_b, w_a, b_a, w_x, b_x, lam)
    h0 = jnp.zeros((p_c.shape[0], D_RG), F32)
    hs_l = []
    for d, rev in enumerate((False, True)):
        f = functools.partial(maybe_flip, rev=rev)
        h_c = f(lru_scan(f(a_c[:, :, d]), f(u_c[:, :, d]), h0))
        h_end = h_c[:, 0] if rev else h_c[:, -1]
        hs_l.append(f(lru_scan(f(a_l[:, :, d]), f(u_l[:, :, d]), h_end)))
    return (hs_l[0] + hs_l[1]) * jax.nn.gelu(gb_l.astype(F32))


def odd_layer_mixer(xs, mods, norm_w, w_in, w_out, gla_args, rg_args):
    bsz = xs.shape[0]
    h = norm_mod(xs, norm_w, mods, 0, 1)
    h = jnp.concatenate([to_col_major(h[:, :SEQ]), h[:, SEQ:]], axis=1)
    p = pmatmul(h.reshape(bsz * NTOK, D_MODEL), w_in.astype(BF16)).reshape(bsz, NTOK, OD_IN)
    p_l, p_c = p[:, :SEQ], p[:, SEQ:]
    a_l = gla_mixer(p_c[..., :GLA_IN], p_l[..., :GLA_IN], *gla_args)
    r_l = rglru_mixer(p_c[..., GLA_IN:], p_l[..., GLA_IN:], *rg_args)
    mix = from_col_major(jnp.concatenate([a_l, r_l], axis=-1)).astype(BF16)
    return mm_resid([mix], [w_out.astype(BF16)], xs, mods, 2, N_LAT_TILES)


def kernel(x, c, ctx, c_ctx, ada_w, ada_b, norm1_w, norm2_w, ev_w_in, ev_w_out, hy_conv_w, hy_conv_b, hy_fw0, hy_fb0, hy_fw1, hy_fb1, hy_fw2, hy_fb2, hy_fw3, hy_freq, hy_bias, ssd_conv_w, ssd_conv_b, ssd_dt_bias, ssd_a_log, ssd_d, ssd_norm_w, od_w_in, od_w_out, gla_gate_w, gla_gate_b, gla_norm_w, rg_conv_w, rg_conv_b, rg_w_a, rg_b_a, rg_w_x, rg_b_x, rg_lambda, router_w, router_b, moe_w_gate, moe_w_up, moe_w_down, sh_w_gate, sh_w_up, sh_w_down, final_norm_w):
    xs = jnp.concatenate([x, ctx], axis=1)
    for i in range(DEPTH):
        last = i == DEPTH - 1
        j = i // 2
        mods = adaln_table(c, c_ctx, ada_w[i], ada_b[i])
        if i % 2 == 0:
            hy_filt = (hy_fw0[j], hy_fb0[j], hy_fw1[j], hy_fb1[j], hy_fw2[j], hy_fb2[j], hy_fw3[j], hy_freq[j])
            xs = even_layer_mixer(xs, mods, norm1_w[i], ev_w_in[j], ev_w_out[j], hy_conv_w[j], hy_conv_b[j],
                                  hy_filt, hy_bias[j], ssd_conv_w[j], ssd_conv_b[j], ssd_dt_bias[j],
                                  ssd_a_log[j], ssd_d[j], ssd_norm_w[j])
        else:
            gla_args = (gla_gate_w[j], gla_gate_b[j], gla_norm_w[j])
            rg_args = (rg_conv_w[j], rg_conv_b[j], rg_w_a[j], rg_b_a[j], rg_w_x[j], rg_b_x[j], rg_lambda[j])
            xs = odd_layer_mixer_pallas(xs, mods, norm1_w[i], od_w_in[j], od_w_out[j], gla_args, rg_args)
        n_tiles = N_LAT_TILES if last else N_ROW_TILES
        xs = moe_layer(xs, norm2_w[i], mods, router_w[i], router_b[i], moe_w_gate, moe_w_up, moe_w_down,
                       sh_w_gate[i], sh_w_up[i], sh_w_down[i], n_tiles, i)
    return final_norm(xs, final_norm_w)
```

```python
import functools
import math

import jax
import jax.numpy as jnp
from jax import lax
from jax.experimental import pallas as pl
from jax.experimental.pallas import tpu as pltpu

D_MODEL = 1024
BATCH = 16
SEQ = 2048
DEPTH = 2

CTX_LEN = 256
GRID_W = 64
NORM_EPS = 1e-6

D_HY = D_MODEL // 2
HY_ORDER = 2
HY_SHORT = 3
HY_BANDS = 16
HY_EMB = 1 + 2 * HY_BANDS
HY_FF = 64
HY_DECAY_PCT_LO = 0.3
HY_DECAY_PCT_HI = 1.5
HY_DECAY_TARGET = 1e-2
HY_IN = 3 * D_HY

D_SSM = D_MODEL // 2
SSD_P = 64
SSD_H = D_SSM // SSD_P
SSD_G = 2
SSD_HPG = SSD_H // SSD_G
SSD_N = 128
SSD_CONV = 4
SSD_CHUNK = 128
SSD_XBC = D_SSM + 2 * SSD_G * SSD_N
SSD_IN = D_SSM + SSD_XBC + 2 * SSD_H
EV_IN = HY_IN + SSD_IN
EV_MIX = D_HY + D_SSM

GLA_H = 4
GLA_DV = (D_MODEL // 2) // GLA_H
GLA_DK = GLA_DV // 2
GLA_RANK = 16
GLA_GATE_NORM = 16.0
GLA_CHUNK = 64
GLA_IN = 2 * GLA_H * GLA_DK + 2 * GLA_H * GLA_DV + 2 * GLA_RANK

D_RG = D_MODEL // 2
RG_BLOCKS = 8
RG_BW = D_RG // RG_BLOCKS
RG_CONV = 4
RG_C = 8.0
RG_IN = 2 * D_RG
OD_IN = GLA_IN + RG_IN
OD_MIX = GLA_H * GLA_DV + D_RG

MOE_EXPERTS = 64
MOE_TOPK = 8
MOE_D_EXPERT = 256
MOE_D_SHARED = 256
MOE_SCALE = 2.5
MOE_BLOCK = 512

F32 = jnp.float32
BF16 = jnp.bfloat16
HIGHEST = lax.Precision.HIGHEST

NTOK = SEQ + CTX_LEN
ROW_TILE = 256
N_ROW_TILES = NTOK // ROW_TILE
N_LAT_TILES = SEQ // ROW_TILE

VMEM_LIMIT = 48 * 1024 * 1024
VMEM_LIMIT_BIG = 56 * 1024 * 1024


def _cparams(sem, limit=VMEM_LIMIT):
    return pltpu.CompilerParams(dimension_semantics=sem, vmem_limit_bytes=limit)


def _pick_tile(n, pref):
    t = min(n, pref)
    while n % t:
        t //= 2
    return t


def _silu(x):
    return x / (1.0 + jnp.exp(-x))


def _softplus(x):
    return jnp.maximum(x, 0.0) + jnp.log(1.0 + jnp.exp(-jnp.abs(x)))


def _mm_bf16_body(a_ref, w_ref, o_ref):
    o_ref[...] = jnp.dot(a_ref[...].astype(BF16), w_ref[...].astype(BF16),
                         preferred_element_type=F32).astype(o_ref.dtype)


def _mm_f32_body(a_ref, w_ref, o_ref):
    o_ref[...] = jnp.dot(a_ref[...], w_ref[...], preferred_element_type=F32,
                         precision=HIGHEST).astype(o_ref.dtype)


def pmatmul(a, w, *, exact=False, out_dtype=F32, tm=512, tn=None):
    m, k = a.shape
    n = w.shape[1]
    tm = _pick_tile(m, tm)
    tn = n if tn is None else _pick_tile(n, tn)
    body = _mm_f32_body if exact else _mm_bf16_body
    return pl.pallas_call(
        body,
        name='mm',
        grid=(m // tm, n // tn),
        in_specs=[pl.BlockSpec((tm, k), lambda i, j: (i, 0)),
                  pl.BlockSpec((k, tn), lambda i, j: (0, j))],
        out_specs=pl.BlockSpec((tm, tn), lambda i, j: (i, j)),
        out_shape=jax.ShapeDtypeStruct((m, n), out_dtype),
        compiler_params=_cparams(("parallel", "parallel")),
    )(a, w)


def _mm_split_body(a_ref, w_ref, *o_refs, splits):
    a = a_ref[...]
    for o_ref, (start, width) in zip(o_refs, splits):
        o_ref[...] = jnp.dot(a, w_ref[:, start:start + width],
                             preferred_element_type=F32).astype(o_ref.dtype)


def mm_split(a, w, splits, dtypes, tm=512):
    m, k = a.shape
    n = w.shape[1]
    tm = _pick_tile(m, tm)
    return pl.pallas_call(
        functools.partial(_mm_split_body, splits=tuple(splits)),
        name='mm_split',
        grid=(m // tm,),
        in_specs=[pl.BlockSpec((tm, k), lambda i: (i, 0)),
                  pl.BlockSpec((k, n), lambda i: (0, 0))],
        out_specs=[pl.BlockSpec((tm, wd), lambda i: (i, 0)) for _, wd in splits],
        out_shape=[jax.ShapeDtypeStruct((m, wd), dt) for (_, wd), dt in zip(splits, dtypes)],
        compiler_params=_cparams(("parallel",)),
    )(a, w)


MIX_SAMPLES = 2


def _stack_samples(ref):
    return jnp.concatenate([ref[s] for s in range(MIX_SAMPLES)], axis=0)


def _resid_store(o_ref, x_ref, g_ref, acc):
    for s in range(MIX_SAMPLES):
        o_ref[s] = x_ref[s] + g_ref[s] * acc[s * ROW_TILE:(s + 1) * ROW_TILE]


def _mm_resid_body(*refs, n_pairs):
    a_refs = refs[:n_pairs]
    w_refs = refs[n_pairs:2 * n_pairs]
    x_ref, g_ref, o_ref = refs[2 * n_pairs:]
    acc = jnp.dot(_stack_samples(a_refs[0]), w_refs[0][...], preferred_element_type=F32)
    for a_ref, w_ref in zip(a_refs[1:], w_refs[1:]):
        acc = acc + jnp.dot(_stack_samples(a_ref), w_ref[...], preferred_element_type=F32)
    _resid_store(o_ref, x_ref, g_ref, acc)


def _gate_spec(gate_idx, d):
    return pl.BlockSpec((MIX_SAMPLES, None, None, 1, d), lambda i, j: (i, 1 - j // N_LAT_TILES, gate_idx, 0, 0))


def _mix_tile(width):
    return pl.BlockSpec((MIX_SAMPLES, ROW_TILE, width), lambda i, j: (i, j, 0))


def mm_resid(a_list, w_list, xs, mods, gate_idx, n_tiles):
    b, nt, d = xs.shape
    n_pairs = len(a_list)
    in_specs = [_mix_tile(a.shape[-1]) for a in a_list]
    in_specs += [pl.BlockSpec(w.shape, lambda i, j: (0, 0)) for w in w_list]
    in_specs += [_mix_tile(d), _gate_spec(gate_idx, d)]
    return pl.pallas_call(
        functools.partial(_mm_resid_body, n_pairs=n_pairs),
        name='mm_resid',
        grid=(b // MIX_SAMPLES, n_tiles),
        in_specs=in_specs,
        out_specs=_mix_tile(d),
        out_shape=jax.ShapeDtypeStruct(xs.shape, F32),
        input_output_aliases={2 * n_pairs: 0},
        compiler_params=_cparams(("parallel", "parallel")),
    )(*a_list, *w_list, xs, mods)


def _even_out_body(hy_ref, yf_ref, yb_ref, xs_ref, z_ref, d_ref, nw_ref, w1_ref, w2_ref, x_ref, g_ref, o_ref):
    gw = D_SSM // SSD_G
    merged = []
    for s in range(MIX_SAMPLES):
        g = (yf_ref[s] + yb_ref[s] + d_ref[...] * xs_ref[s].astype(F32)) * _silu(z_ref[s].astype(F32))
        groups = []
        for k in range(SSD_G):
            gk = g[:, k * gw:(k + 1) * gw]
            ms = jnp.mean(gk * gk, axis=-1, keepdims=True)
            groups.append(gk * lax.rsqrt(ms + NORM_EPS) * nw_ref[:, k * gw:(k + 1) * gw])
        merged.append(jnp.concatenate(groups, axis=1).astype(BF16))
    acc = (jnp.dot(_stack_samples(hy_ref), w1_ref[...], preferred_element_type=F32)
           + jnp.dot(jnp.concatenate(merged, axis=0), w2_ref[...], preferred_element_type=F32))
    _resid_store(o_ref, x_ref, g_ref, acc)


def even_out_resid(y_hy, yf, yb, xbc, z, d_skip, norm_w, w_hy, w_ssd, xs, mods):
    b, nt, d = xs.shape
    d_chan = jnp.repeat(d_skip.astype(F32), SSD_P).reshape(1, D_SSM)
    vec = pl.BlockSpec((1, D_SSM), lambda i, j: (0, 0))
    wspec = pl.BlockSpec((D_SSM, d), lambda i, j: (0, 0))
    half = _mix_tile(D_SSM)
    return pl.pallas_call(
        _even_out_body,
        name='even_out',
        grid=(b // MIX_SAMPLES, N_ROW_TILES),
        in_specs=[half, half, half, half, half, vec, vec, wspec, wspec, _mix_tile(d), _gate_spec(2, d)],
        out_specs=_mix_tile(d),
        out_shape=jax.ShapeDtypeStruct(xs.shape, F32),
        input_output_aliases={9: 0},
        compiler_params=_cparams(("parallel", "parallel")),
    )(y_hy, yf, yb, xbc, z, d_chan, norm_w.reshape(1, D_SSM), w_hy, w_ssd, xs, mods)


def _mod_spec(idx, d, b0=0):
    return pl.BlockSpec((None, None, None, 1, d), lambda i, j: (i + b0, 1 - j // N_LAT_TILES, idx, 0, 0))


def adaln_table(c, c_ctx, w, b):
    cv = jax.nn.silu(jnp.concatenate([c, c_ctx[None, :]], axis=0))
    cv = jnp.pad(cv, ((0, 24 - cv.shape[0]), (0, 0)))
    m = pmatmul(cv, w, exact=True, tn=1536)[:BATCH + 1] + b
    per_sample = m[:BATCH]
    ctx_row = jnp.broadcast_to(m[BATCH][None, :], per_sample.shape)
    return jnp.stack([ctx_row, per_sample], axis=1).reshape(BATCH, 2, 6, 1, D_MODEL)


def _norm_mod(x, w, shift, scale):
    ms = jnp.mean(x * x, axis=-1, keepdims=True)
    return (x * lax.rsqrt(ms + NORM_EPS) * w) * (1.0 + scale) + shift


def _norm_mod_body(x_ref, w_ref, sh_ref, sc_ref, o_ref):
    o_ref[0] = _norm_mod(x_ref[0], w_ref[...], sh_ref[...], sc_ref[...]).astype(o_ref.dtype)


def norm_mod(xs, w, mods, shift_idx, scale_idx):
    b, nt, d = xs.shape
    return pl.pallas_call(
        _norm_mod_body,
        name='norm_mod',
        grid=(b, nt // ROW_TILE),
        in_specs=[pl.BlockSpec((1, ROW_TILE, d), lambda i, j: (i, j, 0)),
                  pl.BlockSpec((1, d), lambda i, j: (0, 0)),
                  _mod_spec(shift_idx, d), _mod_spec(scale_idx, d)],
        out_specs=pl.BlockSpec((1, ROW_TILE, d), lambda i, j: (i, j, 0)),
        out_shape=jax.ShapeDtypeStruct(xs.shape, BF16),
        compiler_params=_cparams(("parallel", "parallel")),
    )(xs, w.reshape(1, d), mods, mods)


def _final_norm_body(x_ref, w_ref, o_ref):
    x = x_ref[0]
    ms = jnp.mean(x * x, axis=-1, keepdims=True)
    o_ref[0] = x * lax.rsqrt(ms + NORM_EPS) * w_ref[...]


def final_norm(xs, w):
    b, _, d = xs.shape
    return pl.pallas_call(
        _final_norm_body,
        name='final_norm',
        grid=(b, N_LAT_TILES),
        in_specs=[pl.BlockSpec((1, ROW_TILE, d), lambda i, j: (i, j, 0)),
                  pl.BlockSpec((1, d), lambda i, j: (0, 0))],
        out_specs=pl.BlockSpec((1, ROW_TILE, d), lambda i, j: (i, j, 0)),
        out_shape=jax.ShapeDtypeStruct((b, SEQ, d), F32),
        compiler_params=_cparams(("parallel", "parallel")),
    )(xs, w.reshape(1, d))


def _route_body(x_ref, w_ref, sh_ref, sc_ref, rw_ref, rb_ref, h_ref, idx_ref, wsel_ref, rank_ref, cnt_ref,
                *, group_size):
    first = (pl.program_id(0) % group_size == 0) & (pl.program_id(1) == 0)

    @pl.when(first)
    def _():
        cnt_ref[...] = jnp.zeros_like(cnt_ref)

    h = _norm_mod(x_ref[0], w_ref[...], sh_ref[...], sc_ref[...])
    h_ref[0] = h.astype(h_ref.dtype)
    logits = jnp.dot(h, rw_ref[...], preferred_element_type=F32, precision=HIGHEST)
    scores = 1.0 / (1.0 + jnp.exp(-logits))
    tm, ne = scores.shape
    lane = lax.broadcasted_iota(jnp.int32, (tm, ne), 1).astype(F32)
    slot = lax.broadcasted_iota(jnp.int32, (tm, MOE_TOPK), 1)
    sel = scores + rb_ref[...]
    picked = jnp.zeros((tm, ne), F32)
    hits = []
    idx_out = jnp.zeros((tm, MOE_TOPK), F32)
    w_out = jnp.zeros((tm, MOE_TOPK), F32)
    for k in range(MOE_TOPK):
        m = jnp.max(sel, axis=-1, keepdims=True)
        ik = jnp.min(jnp.where(sel == m, lane, float(ne)), axis=-1, keepdims=True)
        hit = lane == ik
        wk = jnp.sum(jnp.where(hit, scores, 0.0), axis=-1, keepdims=True)
        sel = jnp.where(hit, -jnp.inf, sel)
        picked = picked + hit.astype(F32)
        hits.append(hit)
        idx_out = jnp.where(slot == k, ik, idx_out)
        w_out = jnp.where(slot == k, wk, w_out)
    wsum = jnp.sum(w_out, axis=-1, keepdims=True)
    wsel_ref[0] = w_out / wsum * MOE_SCALE
    idx_ref[0] = idx_out.astype(jnp.int32)
    r_i = lax.broadcasted_iota(jnp.int32, (tm, tm), 0)
    c_i = lax.broadcasted_iota(jnp.int32, (tm, tm), 1)
    strict_lower = (c_i < r_i).astype(BF16)
    before = jnp.dot(strict_lower, picked.astype(BF16), preferred_element_type=F32) + cnt_ref[...]
    rank_out = jnp.zeros((tm, MOE_TOPK), F32)
    for k in range(MOE_TOPK):
        rk = jnp.sum(jnp.where(hits[k], before, 0.0), axis=-1, keepdims=True)
        rank_out = jnp.where(slot == k, rk, rank_out)
    rank_ref[0] = rank_out.astype(jnp.int32)
    cnt_ref[...] = cnt_ref[...] + jnp.sum(picked, axis=0, keepdims=True)


def route(xs, w, mods, router_w, router_b, n_tiles, group_size):
    b, _, d = xs.shape
    rows = n_tiles * ROW_TILE
    small = lambda dt: jax.ShapeDtypeStruct((b, rows, MOE_TOPK), dt)
    small_spec = pl.BlockSpec((1, ROW_TILE, MOE_TOPK), lambda i, j: (i, j, 0))
    return pl.pallas_call(
        functools.partial(_route_body, group_size=group_size),
        name='route',
        grid=(b, n_tiles),
        in_specs=[pl.BlockSpec((1, ROW_TILE, d), lambda i, j: (i, j, 0)),
                  pl.BlockSpec((1, d), lambda i, j: (0, 0)),
                  _mod_spec(3, d), _mod_spec(4, d),
                  pl.BlockSpec((d, MOE_EXPERTS), lambda i, j: (0, 0)),
                  pl.BlockSpec((1, MOE_EXPERTS), lambda i, j: (0, 0))],
        out_specs=[pl.BlockSpec((1, ROW_TILE, d), lambda i, j: (i, j, 0)),
                   small_spec, small_spec, small_spec,
                   pl.BlockSpec((None, 1, MOE_EXPERTS), lambda i, j: (i // group_size, 0, 0))],
        out_shape=[jax.ShapeDtypeStruct((b, rows, d), BF16), small(jnp.int32), small(F32), small(jnp.int32),
                   jax.ShapeDtypeStruct((b // group_size, 1, MOE_EXPERTS), F32)],
        compiler_params=_cparams(("arbitrary", "arbitrary")),
    )(xs, w.reshape(1, d), mods, mods, router_w, router_b.reshape(1, MOE_EXPERTS))


def _route_t_body(x_ref, w_ref, sh_ref, sc_ref, rwh_ref, rwl_ref, rb_ref, h_ref, idx_ref, wsel_ref, rank_ref, cnt_ref,
                  *, group_size):
    first = (pl.program_id(0) % group_size == 0) & (pl.program_id(1) == 0)

    @pl.when(first)
    def _():
        cnt_ref[...] = jnp.zeros_like(cnt_ref)

    h = _norm_mod(x_ref[0], w_ref[...], sh_ref[...], sc_ref[...])
    h_hi = h.astype(BF16)
    h_ref[0] = h_hi
    h_lo = (h - h_hi.astype(F32)).astype(BF16)
    nt = (((1,), (1,)), ((), ()))
    logits = (lax.dot_general(rwh_ref[...], h_hi, nt, preferred_element_type=F32)
              + lax.dot_general(rwh_ref[...], h_lo, nt, preferred_element_type=F32)
              + lax.dot_general(rwl_ref[...], h_hi, nt, preferred_element_type=F32))
    scores = 1.0 / (1.0 + jnp.exp(-logits))
    ne, tm = scores.shape
    expert = lax.broadcasted_iota(jnp.int32, (ne, tm), 0).astype(F32)
    slot = lax.broadcasted_iota(jnp.int32, (MOE_TOPK, tm), 0)
    sel = scores + rb_ref[...]
    picked = jnp.zeros((ne, tm), F32)
    hits = []
    idx_out = jnp.zeros((MOE_TOPK, tm), F32)
    w_out = jnp.zeros((MOE_TOPK, tm), F32)
    for k in range(MOE_TOPK):
        m = jnp.max(sel, axis=0, keepdims=True)
        ik = jnp.min(jnp.where(sel == m, expert, float(ne)), axis=0, keepdims=True)
        hit = expert == ik
        wk = jnp.sum(jnp.where(hit, scores, 0.0), axis=0, keepdims=True)
        sel = jnp.where(hit, -jnp.inf, sel)
        picked = picked + hit.astype(F32)
        hits.append(hit)
        idx_out = jnp.where(slot == k, ik, idx_out)
        w_out = jnp.where(slot == k, wk, w_out)
    wsum = jnp.sum(w_out, axis=0, keepdims=True)
    wsel_ref[0] = w_out / wsum * MOE_SCALE
    idx_ref[0] = idx_out.astype(jnp.int32)
    r_i = lax.broadcasted_iota(jnp.int32, (tm, tm), 0)
    c_i = lax.broadcasted_iota(jnp.int32, (tm, tm), 1)
    earlier = (r_i < c_i).astype(BF16)
    before = jnp.dot(picked.astype(BF16), earlier, preferred_element_type=F32) + cnt_ref[...]
    rank_out = jnp.zeros((MOE_TOPK, tm), F32)
    for k in range(MOE_TOPK):
        rk = jnp.sum(jnp.where(hits[k], before, 0.0), axis=0, keepdims=True)
        rank_out = jnp.where(slot == k, rk, rank_out)
    rank_ref[0] = rank_out.astype(jnp.int32)
    cnt_ref[...] = cnt_ref[...] + jnp.sum(picked, axis=1, keepdims=True)


def route_t(xs, w, mods, router_w, router_b, n_tiles, group_size):
    b, _, d = xs.shape
    rows = n_tiles * ROW_TILE
    rwt = router_w.T.astype(F32)
    rwt_hi, rwt_lo = _split_bf16(rwt)
    small = lambda dt: jax.ShapeDtypeStruct((b, MOE_TOPK, rows), dt)
    small_spec = pl.BlockSpec((1, MOE_TOPK, ROW_TILE), lambda i, j: (i, 0, j))
    return pl.pallas_call(
        functools.partial(_route_t_body, group_size=group_size),
        name='route',
        grid=(b, n_tiles),
        in_specs=[pl.BlockSpec((1, ROW_TILE, d), lambda i, j: (i, j, 0)),
                  pl.BlockSpec((1, d), lambda i, j: (0, 0)),
                  _mod_spec(3, d), _mod_spec(4, d),
                  pl.BlockSpec((MOE_EXPERTS, d), lambda i, j: (0, 0)),
                  pl.BlockSpec((MOE_EXPERTS, d), lambda i, j: (0, 0)),
                  pl.BlockSpec((MOE_EXPERTS, 1), lambda i, j: (0, 0))],
        out_specs=[pl.BlockSpec((1, ROW_TILE, d), lambda i, j: (i, j, 0)),
                   small_spec, small_spec, small_spec,
                   pl.BlockSpec((None, MOE_EXPERTS, 1), lambda i, j: (i // group_size, 0, 0))],
        out_shape=[jax.ShapeDtypeStruct((b, rows, d), BF16), small(jnp.int32), small(F32), small(jnp.int32),
                   jax.ShapeDtypeStruct((b // group_size, MOE_EXPERTS, 1), F32)],
        compiler_params=_cparams(("arbitrary", "arbitrary")),
    )(xs, w.reshape(1, d), mods, mods, rwt_hi, rwt_lo, router_b.astype(F32).reshape(MOE_EXPERTS, 1))


def _swiglu(x, wg, wu, wd):
    g = jnp.dot(x, wg, preferred_element_type=F32)
    u = jnp.dot(x, wu, preferred_element_type=F32)
    h = (_silu(g) * u).astype(BF16)
    return jnp.dot(h, wd, preferred_element_type=F32)


def _expert_body(be_ref, nu_ref, xa_ref, xb_ref, wg_ref, wu_ref, wd_ref, o_ref, wg_s, wu_s, wd_s):
    i = pl.program_id(0)
    used = i < nu_ref[0]

    @pl.when(used & ((i == 0) | (be_ref[i] != be_ref[jnp.maximum(i - 1, 0)])))
    def _():
        wg_s[...] = wg_ref[0].astype(BF16)
        wu_s[...] = wu_ref[0].astype(BF16)
        wd_s[...] = wd_ref[0].astype(BF16)

    @pl.when(used)
    def _():
        x = jnp.concatenate([xa_ref[...], xb_ref[...]], axis=1)
        o_ref[...] = _swiglu(x, wg_s[...], wu_s[...], wd_s[...]).astype(o_ref.dtype)

    @pl.when(jnp.logical_not(used))
    def _():
        o_ref[...] = jnp.zeros_like(o_ref)


def moe_experts(x_rows, block_e, n_used, wg, wu, wd, layer):
    rows, d = x_rows.shape
    n_blocks = rows // MOE_BLOCK
    f = wg.shape[-1]
    grid_spec = pltpu.PrefetchScalarGridSpec(
        num_scalar_prefetch=2,
        grid=(n_blocks,),
        in_specs=[
            pl.BlockSpec((MOE_BLOCK, d // 2), lambda i, be, nu: (i, 0)),
            pl.BlockSpec((MOE_BLOCK, d // 2), lambda i, be, nu: (i, 1)),
            pl.BlockSpec((None, 1, d, f), lambda i, be, nu: (layer, be[i], 0, 0)),
            pl.BlockSpec((None, 1, d, f), lambda i, be, nu: (layer, be[i], 0, 0)),
            pl.BlockSpec((None, 1, f, d), lambda i, be, nu: (layer, be[i], 0, 0)),
        ],
        out_specs=pl.BlockSpec((MOE_BLOCK, d), lambda i, be, nu: (i, 0)),
        scratch_shapes=[pltpu.VMEM((d, f), BF16), pltpu.VMEM((d, f), BF16), pltpu.VMEM((f, d), BF16)],
    )
    return pl.pallas_call(
        _expert_body,
        name='experts',
        grid_spec=grid_spec,
        out_shape=jax.ShapeDtypeStruct((rows, d), BF16),
        compiler_params=_cparams(("arbitrary",)),
    )(block_e, n_used, x_rows, x_rows, wg, wu, wd)


COMBINE_SAMPLES = 2


def _shared_resid_body(h_ref, wg_ref, wu_ref, wd_ref, pk_ref, ws_ref, x_ref, g_ref, o_ref):
    for s in range(COMBINE_SAMPLES):
        y = _swiglu(h_ref[s], wg_ref[...], wu_ref[...], wd_ref[...])
        ws = ws_ref[s]
        for k in range(MOE_TOPK):
            y = y + ws[:, k:k + 1] * pk_ref[k, s].astype(F32)
        o_ref[s] = x_ref[s] + g_ref[s] * y


def shared_resid(h, picked, wsel, xs, mods, wg, wu, wd, n_tiles, b0):
    b, _, d = h.shape
    f = wg.shape[-1]
    ns = COMBINE_SAMPLES
    s0 = b0 // ns
    tile = pl.BlockSpec((ns, ROW_TILE, d), lambda i, j: (i, j, 0))
    xs_tile = pl.BlockSpec((ns, ROW_TILE, d), lambda i, j: (i + s0, j, 0))
    gate = pl.BlockSpec((ns, None, None, 1, d), lambda i, j: (i + s0, 1 - j // N_LAT_TILES, 5, 0, 0))
    return pl.pallas_call(
        _shared_resid_body,
        name='shared_resid',
        grid=(b // ns, n_tiles),
        in_specs=[tile,
                  pl.BlockSpec((d, f), lambda i, j: (0, 0)),
                  pl.BlockSpec((d, f), lambda i, j: (0, 0)),
                  pl.BlockSpec((f, d), lambda i, j: (0, 0)),
                  pl.BlockSpec((MOE_TOPK, ns, ROW_TILE, d), lambda i, j: (0, i, j, 0)),
                  pl.BlockSpec((ns, ROW_TILE, MOE_TOPK), lambda i, j: (i, j, 0)),
                  xs_tile, gate],
        out_specs=xs_tile,
        out_shape=jax.ShapeDtypeStruct(xs.shape, F32),
        input_output_aliases={6: 0},
        compiler_params=_cparams(("parallel", "parallel")),
    )(h, wg, wu, wd, picked, wsel, xs, mods)


MOE_GROUPS = 2


def moe_layer(xs, norm_w, mods, router_w, router_b, w_gate, w_up, w_down, sh_gate, sh_up, sh_down, n_tiles, layer):
    bsz, _, d = xs.shape
    shared_w = (sh_gate.astype(BF16), sh_up.astype(BF16), sh_down.astype(BF16))
    b = bsz // MOE_GROUPS
    h, idx, wsel, rank, counts = route_t(xs, norm_w, mods, router_w, router_b, n_tiles, b)
    h_flat = h.reshape(-1, d)
    wsel = jnp.swapaxes(wsel, 1, 2)
    for g in range(MOE_GROUPS):
        sl = slice(g * b, (g + 1) * b)
        xs = _moe_group(xs, mods, h_flat, h[sl], idx[sl], wsel[sl], rank[sl], counts[g, :, 0], w_gate, w_up, w_down,
                        shared_w, n_tiles, layer, g * b)
    return xs


def _moe_group(xs, mods, h_flat, h, idx, wsel, rank, counts, w_gate, w_up, w_down, shared_w, n_tiles, layer, b0):
    b, rows_per_sample, d = h.shape
    n = b * rows_per_sample
    counts = counts.astype(jnp.int32)
    padded = (counts + MOE_BLOCK - 1) // MOE_BLOCK * MOE_BLOCK
    ends = jnp.cumsum(padded)
    starts = ends - padded
    nk = n * MOE_TOPK
    n_blocks = -(-nk // MOE_BLOCK) + MOE_EXPERTS
    rows = n_blocks * MOE_BLOCK
    n_pad = rows - nk
    e_iota = jnp.arange(MOE_EXPERTS, dtype=jnp.int32)
    dest = jnp.sum(jnp.where(idx[..., None] == e_iota, starts, 0), axis=-1) + rank
    blk_start = jnp.arange(n_blocks, dtype=jnp.int32) * MOE_BLOCK
    block_e = jnp.minimum(jnp.sum(ends[None, :] <= blk_start[:, None], axis=1), MOE_EXPERTS - 1).astype(jnp.int32)
    n_used = (ends[-1:] // MOE_BLOCK).astype(jnp.int32)
    pad = padded - counts
    cum_pad = jnp.cumsum(pad)
    m = jnp.arange(n_pad, dtype=jnp.int32)
    e_m = jnp.sum(cum_pad[None, :] <= m[:, None], axis=1)
    base = jnp.sum(jnp.where(jnp.minimum(e_m, MOE_EXPERTS - 1)[:, None] == e_iota,
                             starts + counts - (cum_pad - pad), 0), axis=1)
    pad_row = jnp.where(e_m < MOE_EXPERTS, base + m, ends[-1] + m - cum_pad[-1])
    tok0 = b0 * rows_per_sample
    tok = (tok0 + jnp.arange(b, dtype=jnp.int32)[:, None, None] * rows_per_sample
           + jnp.arange(rows_per_sample, dtype=jnp.int32)[None, None, :])
    tok = jnp.broadcast_to(tok, dest.shape).reshape(-1)
    _, row_tok = lax.sort((jnp.concatenate([dest.reshape(-1), pad_row]).astype(jnp.int32),
                           jnp.concatenate([tok, tok0 + m % n])), num_keys=1)
    x_rows = h_flat[row_tok]
    y_rows = moe_experts(x_rows, block_e, n_used, w_gate, w_up, w_down, layer)
    picked = y_rows[jnp.swapaxes(dest, 0, 1)]
    return shared_resid(h, picked, wsel, xs, mods, *shared_w, n_tiles, b0)


def _dwconv_body(x_ref, w_ref, b_ref, o_ref, *, width, act):
    chunk = ROW_TILE
    n_chunks = NTOK // chunk
    first_of_seq = (0, N_LAT_TILES)
    last_of_seq = (N_LAT_TILES - 1, n_chunks - 1)
    tc = x_ref.shape[-1]
    halo = 16
    row = lax.broadcasted_iota(jnp.int32, (chunk, tc), 0)
    zero_row = jnp.zeros((1, tc), F32)
    for c in range(n_chunks):
        r0 = c * chunk
        cur = x_ref[0, r0:r0 + chunk, :].astype(F32)
        if c in first_of_seq:
            prev_last = zero_row
        else:
            prev_last = x_ref[0, r0 - halo:r0, :].astype(F32)[halo - 1:halo, :]
        if c in last_of_seq:
            next0 = next1 = zero_row
        else:
            nxt = x_ref[0, r0 + chunk:r0 + chunk + halo, :].astype(F32)
            next0, next1 = nxt[0:1, :], nxt[1:2, :]
        xm1 = jnp.where(row == 0, prev_last, pltpu.roll(cur, 1, 0))
        xp1 = jnp.where(row == chunk - 1, next0, pltpu.roll(cur, chunk - 1, 0))
        y = w_ref[0:1, :] * xm1 + w_ref[1:2, :] * cur + w_ref[2:3, :] * xp1 + b_ref[...]
        if width == 4:
            xp2 = jnp.where(row == chunk - 2, next0,
                            jnp.where(row == chunk - 1, next1, pltpu.roll(cur, chunk - 2, 0)))
            y = y + w_ref[3:4, :] * xp2
        if act:
            y = _silu(y)
        o_ref[0, c * chunk:(c + 1) * chunk, :] = y.astype(o_ref.dtype)


def dwconv_stream(x, w, b, act, tc=256):
    bsz, nt, c = x.shape
    width = w.shape[0]
    return pl.pallas_call(
        functools.partial(_dwconv_body, width=width, act=act),
        name='dwconv',
        grid=(bsz, c // tc),
        in_specs=[pl.BlockSpec((1, nt, tc), lambda i, j: (i, 0, j)),
                  pl.BlockSpec((width, tc), lambda i, j: (0, j)),
                  pl.BlockSpec((1, tc), lambda i, j: (0, j))],
        out_specs=pl.BlockSpec((1, nt, tc), lambda i, j: (i, 0, j)),
        out_shape=jax.ShapeDtypeStruct(x.shape, BF16),
        compiler_params=_cparams(("parallel", "parallel")),
    )(x, w, b.reshape(1, c))


HY_FB = 512
DFT_SPLIT = 64


def dft_matrices(n):
    t = jnp.arange(n, dtype=jnp.int32)[None, :]
    ka = jnp.arange(DFT_SPLIT, dtype=jnp.int32)[:, None]
    kb = jnp.arange(n // DFT_SPLIT, dtype=jnp.int32)[:, None] * DFT_SPLIT
    ang_a = (2.0 * math.pi / (2 * n)) * ((ka * t) % (2 * n)).astype(F32)
    ang_b = (2.0 * math.pi / (2 * n)) * ((kb * t) % (2 * n)).astype(F32)
    ca, sa = jnp.cos(ang_a)[None], jnp.sin(ang_a)[None]
    cb, sb = jnp.cos(ang_b)[:, None], jnp.sin(ang_b)[:, None]
    cos_kt = (ca * cb - sa * sb).reshape(n, n)
    sin_kt = (sa * cb + ca * sb).reshape(n, n)
    idx = jnp.arange(n, dtype=jnp.int32)
    nyq = jnp.where(idx % 2 == 0, 1.0, -1.0).astype(F32)
    fwd = jnp.concatenate([cos_kt, jnp.where(idx[:, None] == 0, nyq[None, :], -sin_kt)], axis=0)
    scale = jnp.where(idx == 0, 0.5, 1.0)[None, :] / n
    inv = jnp.concatenate([cos_kt * scale, jnp.where(idx[None, :] == 0, nyq[:, None], -sin_kt) * scale], axis=1)
    return fwd.astype(BF16), inv.astype(BF16)


def hyena_filter_taps(n, fw0, fb0, fw1, fb1, fw2, fb2, fw3, freq):
    pos = jnp.arange(n, dtype=F32)
    t = pos / max(n - 1, 1)
    bands = jnp.linspace(1e-4, HY_BANDS - 1, HY_BANDS, dtype=F32)
    ang = (2.0 * math.pi / n) * pos[:, None] * bands[None, :]
    feats = jnp.concatenate([t[:, None], jnp.cos(ang), -jnp.sin(ang)], axis=-1)
    h = jnp.sin(freq * (jnp.dot(feats, fw0, precision=HIGHEST) + fb0))
    h = jnp.sin(freq * (jnp.dot(h, fw1, precision=HIGHEST) + fb1))
    h = jnp.sin(freq * (jnp.dot(h, fw2, precision=HIGHEST) + fb2))
    h = pmatmul(h, fw3, exact=True).reshape(n, 2, HY_ORDER, D_HY)
    deltas = jnp.abs(jnp.linspace(math.log(HY_DECAY_PCT_LO) / HY_DECAY_TARGET,
                                  math.log(HY_DECAY_PCT_HI) / HY_DECAY_TARGET, D_HY, dtype=F32))
    h = h * jnp.exp(-t[:, None] * deltas)[:, None, None, :]
    h0 = h[:, 0]
    h1 = h[:, 1].at[0].set(0.0)
    norm = jnp.sum(jnp.abs(h0), axis=0, keepdims=True) + jnp.sum(jnp.abs(h1), axis=0, keepdims=True)
    h0 = (h0 / norm).reshape(n, HY_ORDER * D_HY)
    h1 = (h1 / norm).reshape(n, HY_ORDER * D_HY)
    return h0 + h1, h0 - h1


def _split_bf16(a):
    hi = a.astype(BF16)
    return hi, (a - hi.astype(F32)).astype(BF16)


def hyena_spectrum(fwd, hsum, hdiff, fb):
    n = hsum.shape[0]
    a = pmatmul(fwd, hsum.astype(BF16))
    bm = pmatmul(fwd, hdiff.astype(BF16))
    sr = a[:n]
    si = bm[n:]
    nyq = a[n]
    first = (jnp.arange(n) == 0)[:, None]
    p = sr
    q = jnp.where(first, 0.0, si)
    s = jnp.where(first, nyq[None, :], sr)
    spec = jnp.stack([p, q, s], axis=0).reshape(3, n // fb, fb, HY_ORDER, D_HY)
    return spec.transpose(3, 1, 0, 2, 4)


def _hyena_body(u_ref, fre_ref, fim_ref, gre_ref, gim_ref, sp_ref, bias_ref, prev_ref, o_ref,
                vin, acc, *, nf):
    del prev_ref
    o = pl.program_id(1)
    f = pl.program_id(2)
    c = D_HY

    @pl.when((o == 0) & (f == 0))
    def _():
        vin[...] = u_ref[0, :, 0:c]

    @pl.when(f == 0)
    def _():
        acc[...] = jnp.zeros_like(acc)

    v = vin[...]
    vr = jnp.dot(fre_ref[...], v, preferred_element_type=F32)
    vi = jnp.dot(fim_ref[...], v, preferred_element_type=F32)
    p, q, s = sp_ref[0], sp_ref[1], sp_ref[2]
    zr = (vr * p - vi * q).astype(BF16)
    zi = (vr * q + vi * s).astype(BF16)
    acc[...] += (jnp.dot(gre_ref[...], zr, preferred_element_type=F32)
                 + jnp.dot(gim_ref[...], zi, preferred_element_type=F32))

    @pl.when((o == 0) & (f == nf - 1))
    def _():
        z = u_ref[0, :, c:2 * c].astype(F32) * (acc[...] + bias_ref[0:1, :] * vin[...].astype(F32))
        vin[...] = z.astype(BF16)

    @pl.when((o == 1) & (f == nf - 1))
    def _():
        y = u_ref[0, :, 2 * c:3 * c].astype(F32) * (acc[...] + bias_ref[1:2, :] * vin[...].astype(F32))
        o_ref[0] = y.astype(o_ref.dtype)


def hyena_long_conv(u, fwd_bf16, inv_bf16, spec, bias, n, row_block, prev_out):
    bsz = u.shape[0]
    fb = spec.shape[3]
    nf = n // fb
    out_shape = jax.ShapeDtypeStruct((bsz, NTOK, D_HY), BF16)
    if prev_out is None:
        prev_out = jnp.zeros(out_shape.shape, BF16)
    args = [u, fwd_bf16, fwd_bf16, inv_bf16, inv_bf16, spec, bias, prev_out]
    aliases = {7: 0}
    return pl.pallas_call(
        functools.partial(_hyena_body, nf=nf),
        name='hyena',
        grid=(bsz, HY_ORDER, nf),
        in_specs=[pl.BlockSpec((1, n, 3 * D_HY), lambda b, o, f: (b, row_block, 0)),
                  pl.BlockSpec((fb, n), lambda b, o, f: (f, 0)),
                  pl.BlockSpec((fb, n), lambda b, o, f: (nf + f, 0)),
                  pl.BlockSpec((n, fb), lambda b, o, f: (0, f)),
                  pl.BlockSpec((n, fb), lambda b, o, f: (0, nf + f)),
                  pl.BlockSpec((None, None, 3, fb, D_HY), lambda b, o, f: (o, f, 0, 0, 0)),
                  pl.BlockSpec((HY_ORDER, D_HY), lambda b, o, f: (0, 0)),
                  pl.BlockSpec(memory_space=pl.ANY)],
        out_specs=pl.BlockSpec((1, n, D_HY), lambda b, o, f: (b, row_block, 0)),
        out_shape=out_shape,
        scratch_shapes=[pltpu.VMEM((n, D_HY), BF16), pltpu.VMEM((n, D_HY), F32)],
        input_output_aliases=aliases,
        compiler_params=_cparams(("parallel", "arbitrary", "arbitrary"), VMEM_LIMIT_BIG),
    )(*args)


def hyena_mixer_stream(p_hy, conv_w, conv_b, filt, bias):
    u = dwconv_stream(p_hy, conv_w, conv_b, act=False)
    out = None
    for n, row_block in ((SEQ, 0), (CTX_LEN, SEQ // CTX_LEN)):
        fb = min(HY_FB, n)
        fwd, inv = dft_matrices(n)
        hsum, hdiff = hyena_filter_taps(n, *filt)
        spec = hyena_spectrum(fwd, hsum, hdiff, fb)
        out = hyena_long_conv(u, fwd, inv, spec, bias, n, row_block, out)
    return out


SCAN_SAMPLES = 2


def _tri(n, kind):
    r = lax.broadcasted_iota(jnp.int32, (n, n), 0)
    c = lax.broadcasted_iota(jnp.int32, (n, n), 1)
    return (c <= r) if kind == 'lower' else (c >= r)


def _ssd_dir(xbc_ref, dt_ref, dtt_ref, bias_r, bias_c, a_r, a_c, st_ref, y_ref, *, s, d, reverse):
    q = SSD_CHUNK
    nh = SSD_H
    gw = SSD_HPG * SSD_P
    lower = _tri(q, 'lower')
    upper = _tri(q, 'upper')
    lower_f = lower.astype(F32)
    upper_f = upper.astype(F32)
    dt_col = _softplus(dt_ref[s] + bias_r)
    dt_row = _softplus(dtt_ref[s] + bias_c)
    da_col = dt_col * a_r
    da_row = dt_row * a_c
    if not reverse:
        acs_col = jnp.dot(lower_f, da_col, preferred_element_type=F32, precision=HIGHEST)
        acs_row = jnp.dot(da_row, upper_f, preferred_element_type=F32, precision=HIGHEST)
        mask = lower
        edge = q - 1
    else:
        acs_col = jnp.dot(upper_f, da_col, preferred_element_type=F32, precision=HIGHEST)
        acs_row = jnp.dot(da_row, lower_f, preferred_element_type=F32, precision=HIGHEST)
        mask = upper
        edge = 0
    h0 = d * nh
    hh = lax.broadcasted_iota(jnp.int32, (2 * nh, nh * SSD_P), 0)
    cc = lax.broadcasted_iota(jnp.int32, (2 * nh, nh * SSD_P), 1) // SSD_P
    expand = (hh == cc + h0).astype(F32)
    acs_c = jnp.dot(acs_col, expand, preferred_element_type=F32, precision=HIGHEST)
    dt_c = jnp.dot(dt_col, expand, preferred_element_type=F32, precision=HIGHEST)
    total_c = acs_c[edge:edge + 1, :]
    e_in_c = jnp.exp(acs_c)
    w_end_c = jnp.exp(total_c - acs_c) * dt_c
    dec_c = jnp.exp(total_c)
    xs = xbc_ref[s, :, 0:D_SSM]
    xs_f = xs.astype(F32)
    for g in range(SSD_G):
        bm = xbc_ref[s, :, D_SSM + g * SSD_N:D_SSM + (g + 1) * SSD_N]
        cm = xbc_ref[s, :, D_SSM + SSD_G * SSD_N + g * SSD_N:D_SSM + SSD_G * SSD_N + (g + 1) * SSD_N]
        cb = lax.dot_general(cm, bm, (((1,), (1,)), ((), ())), preferred_element_type=F32)
        lws = []
        for k in range(SSD_HPG):
            h = h0 + g * SSD_HPG + k
            seg = acs_col[:, h:h + 1] - acs_row[h:h + 1, :]
            decay = jnp.exp(jnp.where(mask, seg, -jnp.inf))
            lws.append((cb * decay * dt_row[h:h + 1, :]).astype(BF16))
        lw = jnp.concatenate(lws, axis=1)
        xg = xs[:, g * gw:(g + 1) * gw]
        rb = lax.broadcasted_iota(jnp.int32, (SSD_HPG * q, gw), 0) // q
        cbk = lax.broadcasted_iota(jnp.int32, (SSD_HPG * q, gw), 1) // SSD_P
        x_bd = jnp.where(rb == cbk, jnp.concatenate([xg] * SSD_HPG, axis=0), jnp.zeros((), BF16))
        y_in = jnp.dot(lw, x_bd, preferred_element_type=F32)
        st = st_ref[s, g]
        y_st = jnp.dot(cm, st.astype(BF16), preferred_element_type=F32) * e_in_c[:, g * gw:(g + 1) * gw]
        y_ref[s, :, g * gw:(g + 1) * gw] = y_in + y_st
        xw = (xs_f[:, g * gw:(g + 1) * gw] * w_end_c[:, g * gw:(g + 1) * gw]).astype(BF16)
        upd = lax.dot_general(bm, xw, (((0,), (0,)), ((), ())), preferred_element_type=F32)
        st_ref[s, g] = st * dec_c[:, g * gw:(g + 1) * gw] + upd


def _ssd_body(xf_ref, dtf_ref, dttf_ref, xb_ref, dtb_ref, dttb_ref, bias_r, bias_c, a_r, a_c,
              yf_ref, yb_ref, stf, stb):
    @pl.when(pl.program_id(1) == 0)
    def _():
        stf[...] = jnp.zeros_like(stf)
        stb[...] = jnp.zeros_like(stb)

    for s in range(SCAN_SAMPLES):
        _ssd_dir(xf_ref, dtf_ref, dttf_ref, bias_r[...], bias_c[...], a_r[...], a_c[...], stf, yf_ref,
                 s=s, d=0, reverse=False)
        _ssd_dir(xb_ref, dtb_ref, dttb_ref, bias_r[...], bias_c[...], a_r[...], a_c[...], stb, yb_ref,
                 s=s, d=1, reverse=True)


def ssd_scan(xbc, dt, dt_bias, a_log):
    bsz = xbc.shape[0]
    nc = NTOK // SSD_CHUNK
    nlat = SEQ // SSD_CHUNK
    dtt = jnp.swapaxes(dt, 1, 2)
    fwd_chunk = lambda s: (s + nlat) % nc
    bwd_chunk = lambda s: nc - 1 - s
    a = -jnp.exp(a_log.astype(F32)).reshape(1, 2 * SSD_H)
    bias = dt_bias.astype(F32).reshape(1, 2 * SSD_H)
    ns = SCAN_SAMPLES
    x_spec = lambda cm: pl.BlockSpec((ns, SSD_CHUNK, SSD_XBC), lambda b, s: (b, cm(s), 0))
    dt_spec = lambda cm: pl.BlockSpec((ns, SSD_CHUNK, 2 * SSD_H), lambda b, s: (b, cm(s), 0))
    dtt_spec = lambda cm: pl.BlockSpec((ns, 2 * SSD_H, SSD_CHUNK), lambda b, s: (b, 0, cm(s)))
    y_spec = lambda cm: pl.BlockSpec((ns, SSD_CHUNK, D_SSM), lambda b, s: (b, cm(s), 0))
    row = pl.BlockSpec((1, 2 * SSD_H), lambda b, s: (0, 0))
    col = pl.BlockSpec((2 * SSD_H, 1), lambda b, s: (0, 0))
    y_shape = jax.ShapeDtypeStruct((bsz, NTOK, D_SSM), F32)
    gw = SSD_HPG * SSD_P
    return pl.pallas_call(
        _ssd_body,
        name='ssd_scan',
        grid=(bsz // ns, nc),
        in_specs=[x_spec(fwd_chunk), dt_spec(fwd_chunk), dtt_spec(fwd_chunk),
                  x_spec(bwd_chunk), dt_spec(bwd_chunk), dtt_spec(bwd_chunk),
                  row, col, row, col],
        out_specs=[y_spec(fwd_chunk), y_spec(bwd_chunk)],
        out_shape=[y_shape, y_shape],
        scratch_shapes=[pltpu.VMEM((ns, SSD_G, SSD_N, gw), F32), pltpu.VMEM((ns, SSD_G, SSD_N, gw), F32)],
        compiler_params=_cparams(("parallel", "arbitrary")),
    )(xbc, dt, dtt, xbc, dt, dtt, bias, bias.reshape(-1, 1), a, a.reshape(-1, 1))


def _ssd_merge_body(yf_ref, yb_ref, xbc_ref, z_ref, d_ref, nw_ref, o_ref):
    xs = xbc_ref[0, :, 0:D_SSM].astype(F32)
    z = z_ref[0].astype(F32)
    g = (yf_ref[0] + yb_ref[0] + d_ref[...] * xs) * _silu(z)
    gw = D_SSM // SSD_G
    for k in range(SSD_G):
        gk = g[:, k * gw:(k + 1) * gw]
        ms = jnp.mean(gk * gk, axis=-1, keepdims=True)
        o_ref[0, :, k * gw:(k + 1) * gw] = (gk * lax.rsqrt(ms + NORM_EPS)
                                            * nw_ref[:, k * gw:(k + 1) * gw]).astype(o_ref.dtype)


def ssd_merge(yf, yb, xbc, z, d_skip, norm_w):
    bsz = yf.shape[0]
    tile = lambda w: pl.BlockSpec((1, ROW_TILE, w), lambda i, j: (i, j, 0))
    vec = pl.BlockSpec((1, D_SSM), lambda i, j: (0, 0))
    d_chan = jnp.repeat(d_skip.astype(F32), SSD_P).reshape(1, D_SSM)
    return pl.pallas_call(
        _ssd_merge_body,
        name='ssd_merge',
        grid=(bsz, N_ROW_TILES),
        in_specs=[tile(D_SSM), tile(D_SSM), tile(SSD_XBC), tile(D_SSM), vec, vec],
        out_specs=tile(D_SSM),
        out_shape=jax.ShapeDtypeStruct((bsz, NTOK, D_SSM), BF16),
        compiler_params=_cparams(("parallel", "parallel")),
    )(yf, yb, xbc, z, d_chan, norm_w.reshape(1, D_SSM))


def even_layer_mixer(xs, mods, norm_w, w_in, w_out, hy_conv_w, hy_conv_b, hy_filt, hy_bias,
                     ssd_conv_w, ssd_conv_b, ssd_dt_bias, ssd_a_log, ssd_d, ssd_norm_w):
    bsz = xs.shape[0]
    h = norm_mod(xs, norm_w, mods, 0, 1)
    splits = ((0, HY_IN), (HY_IN, D_SSM), (HY_IN + D_SSM, SSD_XBC), (HY_IN + D_SSM + SSD_XBC, 2 * SSD_H))
    p_hy, z, xbc_raw, dt = mm_split(h.reshape(bsz * NTOK, D_MODEL), w_in.astype(BF16), splits,
                                    (BF16, BF16, BF16, F32))
    to3 = lambda a: a.reshape(bsz, NTOK, a.shape[-1])
    y_hy = hyena_mixer_stream(to3(p_hy), hy_conv_w, hy_conv_b, hy_filt, hy_bias)
    xbc = dwconv_stream(to3(xbc_raw), ssd_conv_w, ssd_conv_b, act=True)
    yf, yb = ssd_scan(xbc, to3(dt), ssd_dt_bias, ssd_a_log)
    wo = w_out.astype(BF16)
    return even_out_resid(y_hy, yf, yb, xbc, to3(z), ssd_d, ssd_norm_w, wo[:D_HY], wo[D_HY:], xs, mods)


GLA_QK = GLA_H * GLA_DK
GLA_V = GLA_H * GLA_DV


def _gla_dir(qkv_ref, lr_ref, gw_ref, gb_ref, st_ref, o_ref, *, s, d, reverse):
    q = GLA_CHUNK
    tri = _tri(q, 'upper' if reverse else 'lower')
    edge = 0 if reverse else q - 1
    lr = lr_ref[s, :, d * GLA_RANK:(d + 1) * GLA_RANK]
    logit = jnp.dot(lr, gw_ref[d], preferred_element_type=F32, precision=HIGHEST) + gb_ref[d:d + 1, :]
    log_g = -_softplus(-logit) * (1.0 / GLA_GATE_NORM)
    gcum = jnp.dot(tri.astype(F32), log_g, preferred_element_type=F32, precision=HIGHEST)
    total = gcum[edge:edge + 1, :]
    qf = qkv_ref[s, :, 0:GLA_QK].astype(F32)
    kf = qkv_ref[s, :, GLA_QK:2 * GLA_QK].astype(F32)
    v = qkv_ref[s, :, 2 * GLA_QK:2 * GLA_QK + GLA_V]
    qg = (qf * (GLA_DK ** -0.5) * jnp.exp(gcum)).astype(BF16)
    kg = (kf * jnp.exp(-gcum)).astype(BF16)
    kw = (kf * jnp.exp(total - gcum)).astype(BF16)
    rb = lax.broadcasted_iota(jnp.int32, (GLA_H * q, GLA_QK), 0) // q
    cb = lax.broadcasted_iota(jnp.int32, (GLA_H * q, GLA_QK), 1) // GLA_DK
    k_bd = jnp.where(rb == cb, jnp.concatenate([kg] * GLA_H, axis=0), jnp.zeros((), BF16))
    att = lax.dot_general(qg, k_bd, (((1,), (1,)), ((), ())), preferred_element_type=F32)
    i_i = lax.broadcasted_iota(jnp.int32, (q, GLA_H * q), 0)
    j_i = lax.broadcasted_iota(jnp.int32, (q, GLA_H * q), 1) % q
    keep = (j_i >= i_i) if reverse else (j_i <= i_i)
    att = jnp.where(keep, att, 0.0).astype(BF16)
    rv = lax.broadcasted_iota(jnp.int32, (GLA_H * q, GLA_V), 0) // q
    cv = lax.broadcasted_iota(jnp.int32, (GLA_H * q, GLA_V), 1) // GLA_DV
    v_bd = jnp.where(rv == cv, jnp.concatenate([v] * GLA_H, axis=0), jnp.zeros((), BF16))
    st = st_ref[s]
    o_in = jnp.dot(att, v_bd, preferred_element_type=F32)
    o_st = lax.dot_general(qg, st.astype(BF16), (((1,), (1,)), ((), ())), preferred_element_type=F32)
    o_ref[s] = o_in + o_st
    upd = lax.dot_general(v, kw, (((0,), (0,)), ((), ())), preferred_element_type=F32)
    rs = lax.broadcasted_iota(jnp.int32, (GLA_V, GLA_QK), 0) // GLA_DV
    cs = lax.broadcasted_iota(jnp.int32, (GLA_V, GLA_QK), 1) // GLA_DK
    st_ref[s] = st * jnp.exp(total) + jnp.where(rs == cs, upd, 0.0)


def _gla_body(qf_ref, lf_ref, qb_ref, lb_ref, gw_ref, gb_ref, of_ref, ob_ref, stf, stb):
    @pl.when(pl.program_id(1) == 0)
    def _():
        stf[...] = jnp.zeros_like(stf)
        stb[...] = jnp.zeros_like(stb)

    for s in range(SCAN_SAMPLES):
        _gla_dir(qf_ref, lf_ref, gw_ref, gb_ref, stf, of_ref, s=s, d=0, reverse=False)
        _gla_dir(qb_ref, lb_ref, gw_ref, gb_ref, stb, ob_ref, s=s, d=1, reverse=True)


def gla_scan(qkv, lr, gate_w, gate_b):
    bsz = qkv.shape[0]
    nc = NTOK // GLA_CHUNK
    nlat = SEQ // GLA_CHUNK
    fwd_chunk = lambda s: (s + nlat) % nc
    bwd_chunk = lambda s: nc - 1 - s
    ns = SCAN_SAMPLES
    q_spec = lambda cm: pl.BlockSpec((ns, GLA_CHUNK, qkv.shape[-1]), lambda b, s: (b, cm(s), 0))
    l_spec = lambda cm: pl.BlockSpec((ns, GLA_CHUNK, 2 * GLA_RANK), lambda b, s: (b, cm(s), 0))
    o_spec = lambda cm: pl.BlockSpec((ns, GLA_CHUNK, GLA_V), lambda b, s: (b, cm(s), 0))
    o_shape = jax.ShapeDtypeStruct((bsz, NTOK, GLA_V), F32)
    return pl.pallas_call(
        _gla_body,
        name='gla_scan',
        grid=(bsz // ns, nc),
        in_specs=[q_spec(fwd_chunk), l_spec(fwd_chunk), q_spec(bwd_chunk), l_spec(bwd_chunk),
                  pl.BlockSpec((2, GLA_RANK, GLA_QK), lambda b, s: (0, 0, 0)),
                  pl.BlockSpec((2, GLA_QK), lambda b, s: (0, 0))],
        out_specs=[o_spec(fwd_chunk), o_spec(bwd_chunk)],
        out_shape=[o_shape, o_shape],
        scratch_shapes=[pltpu.VMEM((ns, GLA_V, GLA_QK), F32), pltpu.VMEM((ns, GLA_V, GLA_QK), F32)],
        compiler_params=_cparams(("parallel", "arbitrary")),
    )(qkv, lr, qkv, lr, gate_w.astype(F32), gate_b.astype(F32))


def _gla_merge_body(of_ref, ob_ref, r_ref, nw_ref, o_ref):
    o = of_ref[0] + ob_ref[0]
    r = r_ref[0].astype(F32)
    for h in range(GLA_H):
        sl = slice(h * GLA_DV, (h + 1) * GLA_DV)
        oh = o[:, sl]
        ms = jnp.mean(oh * oh, axis=-1, keepdims=True)
        o_ref[0, :, sl] = (oh * lax.rsqrt(ms + NORM_EPS) * nw_ref[:, sl] * _silu(r[:, sl])).astype(o_ref.dtype)


def gla_merge_stream(of, ob, r, norm_w):
    bsz = of.shape[0]
    tile = pl.BlockSpec((1, ROW_TILE, GLA_V), lambda i, j: (i, j, 0))
    return pl.pallas_call(
        _gla_merge_body,
        name='gla_merge',
        grid=(bsz, N_LAT_TILES),
        in_specs=[tile, tile, tile, pl.BlockSpec((1, GLA_V), lambda i, j: (0, 0))],
        out_specs=tile,
        out_shape=jax.ShapeDtypeStruct((bsz, SEQ, GLA_V), BF16),
        compiler_params=_cparams(("parallel", "parallel")),
    )(of, ob, r, norm_w.reshape(1, GLA_V))


RG_TILE = 8


def _gelu_tanh(x):
    return 0.5 * x * (1.0 + jnp.tanh(math.sqrt(2.0 / math.pi) * (x + 0.044715 * x * x * x)))


def _rg_scan_block(a_s, x_s, h_s, base, carry, reverse):
    n_tiles = ROW_TILE // RG_TILE
    row = lax.broadcasted_iota(jnp.int32, (RG_TILE, D_RG), 0)

    def tile_step(i, h_prev):
        t = (n_tiles - 1 - i) if reverse else i
        r0 = pl.multiple_of(t * RG_TILE, RG_TILE)
        a = a_s[pl.ds(r0, RG_TILE), :]
        x = x_s[pl.ds(r0, RG_TILE), :]
        for s in (1, 2, 4):
            if reverse:
                ok = row < RG_TILE - s
                shift = RG_TILE - s
            else:
                ok = row >= s
                shift = s
            a_sh = jnp.where(ok, pltpu.roll(a, shift, 0), 1.0)
            x_sh = jnp.where(ok, pltpu.roll(x, shift, 0), 0.0)
            x = a * x_sh + x
            a = a * a_sh
        h = x + a * h_prev
        h_s[pl.ds(base + r0, RG_TILE), :] = h
        edge = 0 if reverse else RG_TILE - 1
        return jnp.broadcast_to(h[edge:edge + 1, :], (RG_TILE, D_RG))

    return lax.fori_loop(0, n_tiles, tile_step, carry)


def _rglru_body(u_ref, g_ref, w_ref, b_ref, c_ref, o_ref, hf_s, a_s, x_s, hb_s):
    n_blocks = NTOK // ROW_TILE
    fwd_order = list(range(N_LAT_TILES, n_blocks)) + list(range(N_LAT_TILES))
    bwd_order = list(range(n_blocks - 1, N_LAT_TILES - 1, -1)) + list(range(N_LAT_TILES - 1, -1, -1))

    def gates(blk, d):
        ub = u_ref[0, blk * ROW_TILE:(blk + 1) * ROW_TILE, :]
        z = jnp.dot(ub, w_ref[:, 2 * d * D_RG:2 * (d + 1) * D_RG], preferred_element_type=F32)
        z = z + b_ref[:, 2 * d * D_RG:2 * (d + 1) * D_RG]
        r = 1.0 / (1.0 + jnp.exp(-z[:, :D_RG]))
        i = 1.0 / (1.0 + jnp.exp(-z[:, D_RG:]))
        a = jnp.exp(c_ref[d:d + 1, :] * r)
        a_s[...] = a
        x_s[...] = jnp.sqrt(1.0 - a * a) * i * ub.astype(F32)

    carry = jnp.zeros((RG_TILE, D_RG), F32)
    for blk in fwd_order:
        gates(blk, 0)
        carry = _rg_scan_block(a_s, x_s, hf_s, blk * ROW_TILE, carry, reverse=False)
    carry = jnp.zeros((RG_TILE, D_RG), F32)
    for blk in bwd_order:
        gates(blk, 1)
        carry = _rg_scan_block(a_s, x_s, hb_s, 0, carry, reverse=True)
        rows = slice(blk * ROW_TILE, (blk + 1) * ROW_TILE)
        gate = g_ref[0, rows, :].astype(F32)
        o_ref[0, rows, :] = ((hf_s[rows, :] + hb_s[...]) * _gelu_tanh(gate)).astype(o_ref.dtype)


def rglru_stream(u, gate, w_a, b_a, w_x, b_x, lam):
    bsz = u.shape[0]
    eye = jnp.eye(RG_BLOCKS, dtype=F32)
    dense = lambda w: jnp.einsum('nio,nm->nimo', w, eye).reshape(D_RG, D_RG)
    w_cat = jnp.concatenate([dense(w_a[0]), dense(w_x[0]), dense(w_a[1]), dense(w_x[1])], axis=1).astype(BF16)
    b_cat = jnp.concatenate([b_a[0], b_x[0], b_a[1], b_x[1]]).astype(F32).reshape(1, 4 * D_RG)
    c = -RG_C * jax.nn.softplus(-lam.astype(F32))
    seq = pl.BlockSpec((1, NTOK, D_RG), lambda i: (i, 0, 0))
    return pl.pallas_call(
        _rglru_body,
        name='rglru',
        grid=(bsz,),
        in_specs=[seq, seq,
                  pl.BlockSpec((D_RG, 4 * D_RG), lambda i: (0, 0)),
                  pl.BlockSpec((1, 4 * D_RG), lambda i: (0, 0)),
                  pl.BlockSpec((2, D_RG), lambda i: (0, 0))],
        out_specs=seq,
        out_shape=jax.ShapeDtypeStruct((bsz, NTOK, D_RG), BF16),
        scratch_shapes=[pltpu.VMEM((NTOK, D_RG), F32), pltpu.VMEM((ROW_TILE, D_RG), F32),
                        pltpu.VMEM((ROW_TILE, D_RG), F32), pltpu.VMEM((ROW_TILE, D_RG), F32)],
        compiler_params=_cparams(("parallel",)),
    )(u, gate, w_cat, b_cat, c)


def odd_layer_mixer_pallas(xs, mods, norm_w, w_in, w_out, gla_args, rg_args):
    bsz = xs.shape[0]
    gate_w, gate_b, gla_norm_w = gla_args
    rg_conv_w, rg_conv_b, w_a, b_a, w_x, b_x, lam = rg_args
    h = norm_mod(xs, norm_w, mods, 0, 1)
    h = jnp.concatenate([to_col_major(h[:, :SEQ]), h[:, SEQ:]], axis=1)
    nqk, nv = GLA_QK, GLA_V
    r0 = 2 * nqk + nv + 2 * GLA_RANK
    w = jnp.concatenate([w_in[:, :2 * nqk + nv], w_in[:, r0:r0 + nv], w_in[:, GLA_IN:],
                         w_in[:, 2 * nqk + nv:r0]], axis=1).astype(BF16)
    qkv_w = 2 * nqk + nv
    splits = ((0, qkv_w), (qkv_w, nv), (qkv_w + nv, D_RG), (qkv_w + nv + D_RG, D_RG),
              (qkv_w + nv + 2 * D_RG, 2 * GLA_RANK))
    qkv, r, u_raw, gate, lr = mm_split(h.reshape(bsz * NTOK, D_MODEL), w, splits, (BF16, BF16, BF16, BF16, F32))
    to3 = lambda a: a.reshape(bsz, NTOK, a.shape[-1])
    of, ob = gla_scan(to3(qkv), to3(lr), gate_w, gate_b.reshape(2, GLA_QK))
    a_l = gla_merge_stream(of, ob, to3(r), gla_norm_w)
    u = dwconv_stream(to3(u_raw), rg_conv_w, rg_conv_b, act=False)
    r_l = rglru_stream(u, to3(gate), w_a, b_a, w_x, b_x, lam)[:, :SEQ]
    wo = w_out.astype(BF16)
    return mm_resid([from_col_major(a_l), from_col_major(r_l)], [wo[:GLA_V], wo[GLA_V:]], xs, mods, 2, N_LAT_TILES)


def rms_norm(x, w):
    xf = x.astype(F32)
    y = xf * lax.rsqrt(jnp.mean(jnp.square(xf), axis=-1, keepdims=True) + NORM_EPS)
    return y.astype(x.dtype) * w


def dwconv(x, w, b):
    y = lax.conv_general_dilated(x, w[:, None, :].astype(x.dtype), window_strides=(1,), padding='SAME',
                                 dimension_numbers=('NWC', 'WIO', 'NWC'), feature_group_count=x.shape[-1])
    return y + b.astype(x.dtype)


def maybe_flip(a, rev):
    return jnp.flip(a, axis=1) if rev else a


def to_col_major(x):
    b, n, d = x.shape
    rows = n // GRID_W
    return x.reshape(b, rows, GRID_W, d).transpose(0, 2, 1, 3).reshape(b, n, d)


def from_col_major(x):
    b, n, d = x.shape
    rows = n // GRID_W
    return x.reshape(b, GRID_W, rows, d).transpose(0, 2, 1, 3).reshape(b, n, d)


def gla_inputs(p, gate_w, gate_b):
    b, n, _ = p.shape
    nqk, nv = GLA_H * GLA_DK, GLA_H * GLA_DV
    q = p[..., :nqk].reshape(b, n, GLA_H, GLA_DK) * GLA_DK ** -0.5
    k = p[..., nqk:2 * nqk].reshape(b, n, GLA_H, GLA_DK)
    v = p[..., 2 * nqk:2 * nqk + nv].reshape(b, n, GLA_H, GLA_DV)
    lr = p[..., 2 * nqk + nv:2 * nqk + nv + 2 * GLA_RANK].reshape(b, n, 2, GLA_RANK)
    r = p[..., 2 * nqk + nv + 2 * GLA_RANK:]
    logit = jnp.einsum('bler,erk->blek', lr, gate_w) + gate_b
    log_g = (jax.nn.log_sigmoid(logit.astype(F32)) / GLA_GATE_NORM).reshape(b, n, 2, GLA_H, GLA_DK)
    return q, k, v, log_g, r


def gla_states(k, v, log_g, s0):
    b, n = k.shape[:2]
    nc = n // GLA_CHUNK
    kc = k.reshape(b, nc, GLA_CHUNK, GLA_H, GLA_DK)
    vc = v.reshape(b, nc, GLA_CHUNK, GLA_H, GLA_DV)
    gcum = jnp.cumsum(log_g.reshape(b, nc, GLA_CHUNK, GLA_H, GLA_DK), axis=2)
    states = jnp.einsum('bcqhd,bcqhv->bchdv', kc * jnp.exp(gcum[:, :, -1:] - gcum), vc)
    chunk_decay = jnp.exp(gcum[:, :, -1])

    def step(s, inp):
        st, dcy = inp
        return dcy[..., None] * s + st, s

    s_fin, s_prev = lax.scan(step, s0, (jnp.moveaxis(states, 1, 0), jnp.moveaxis(chunk_decay, 1, 0)))
    return jnp.moveaxis(s_prev, 0, 1), s_fin


def gla_output(q, k, v, log_g, s_prev):
    b, n = q.shape[:2]
    nc = n // GLA_CHUNK
    qc = q.reshape(b, nc, GLA_CHUNK, GLA_H, GLA_DK)
    kc = k.reshape(b, nc, GLA_CHUNK, GLA_H, GLA_DK)
    vc = v.reshape(b, nc, GLA_CHUNK, GLA_H, GLA_DV)
    gcum = jnp.cumsum(log_g.reshape(b, nc, GLA_CHUNK, GLA_H, GLA_DK), axis=2)
    qg = qc * jnp.exp(gcum)
    kg = kc * jnp.exp(-gcum)
    mask = jnp.tril(jnp.ones((GLA_CHUNK, GLA_CHUNK), bool))
    att = jnp.where(mask, jnp.einsum('bcihd,bcjhd->bchij', qg, kg), 0.0)
    o = jnp.einsum('bchij,bcjhv->bcihv', att, vc) + jnp.einsum('bcihd,bchdv->bcihv', qg, s_prev)
    return o.reshape(b, n, GLA_H, GLA_DV)


def gla_merge(os_, r, norm_w):
    b, n = r.shape[:2]
    o = rms_norm(os_[0] + os_[1], norm_w.reshape(GLA_H, GLA_DV))
    return o.reshape(b, n, GLA_H * GLA_DV) * jax.nn.silu(r)


def gla_mixer(p_c, p_l, gate_w, gate_b, norm_w):
    q_c, k_c, v_c, g_c, r_c = gla_inputs(p_c, gate_w, gate_b)
    q_l, k_l, v_l, g_l, r_l = gla_inputs(p_l, gate_w, gate_b)
    s0 = jnp.zeros((p_c.shape[0], GLA_H, GLA_DK, GLA_DV), F32)
    os_l = []
    for d, rev in enumerate((False, True)):
        f = functools.partial(maybe_flip, rev=rev)
        _, sf_c = gla_states(f(k_c), f(v_c), f(g_c[:, :, d]), s0)
        sp_l, _ = gla_states(f(k_l), f(v_l), f(g_l[:, :, d]), sf_c)
        os_l.append(f(gla_output(f(q_l), f(k_l), f(v_l), f(g_l[:, :, d]), sp_l)))
    return gla_merge(os_l, r_l, norm_w)


def rglru_inputs(p, conv_w, conv_b, w_a, b_a, w_x, b_x, lam):
    b, n, _ = p.shape
    u = dwconv(p[..., :D_RG], conv_w, conv_b)
    ub = u.reshape(b, n, RG_BLOCKS, RG_BW)
    r = jax.nn.sigmoid((jnp.einsum('blni,enio->bleno', ub, w_a).reshape(b, n, 2, D_RG) + b_a).astype(F32))
    i = jax.nn.sigmoid((jnp.einsum('blni,enio->bleno', ub, w_x).reshape(b, n, 2, D_RG) + b_x).astype(F32))
    log_a = -RG_C * jax.nn.softplus(-lam.astype(F32)) * r
    x_in = jnp.sqrt(-jnp.expm1(2.0 * log_a)) * i * u[:, :, None, :].astype(F32)
    return p[..., D_RG:], jnp.exp(log_a), x_in


def lru_scan(a, u, h0):
    u = u.at[:, 0].add(a[:, 0] * h0)

    def combine(lhs, rhs):
        a1, b1 = lhs
        a2, b2 = rhs
        return a1 * a2, a2 * b1 + b2

    return lax.associative_scan(combine, (a, u), axis=1)[1]


def rglru_mixer(p_c, p_l, conv_w, conv_b, w_a, b_a, w_x, b_x, lam):
    gb_c, a_c, u_c = rglru_inputs(p_c, conv_w, conv_b, w_a, b_a, w_x, b_x, lam)
    gb_l, a_l, u_l = rglru_inputs(p_l, conv_w, conv_b, w_a, b_a, w_x, b_x, lam)
    h0 = jnp.zeros((p_c.shape[0], D_RG), F32)
    hs_l = []
    for d, rev in enumerate((False, True)):
        f = functools.partial(maybe_flip, rev=rev)
        h_c = f(lru_scan(f(a_c[:, :, d]), f(u_c[:, :, d]), h0))
        h_end = h_c[:, 0] if rev else h_c[:, -1]
        hs_l.append(f(lru_scan(f(a_l[:, :, d]), f(u_l[:, :, d]), h_end)))
    return (hs_l[0] + hs_l[1]) * jax.nn.gelu(gb_l.astype(F32))


def odd_layer_mixer(xs, mods, norm_w, w_in, w_out, gla_args, rg_args):
    bsz = xs.shape[0]
    h = norm_mod(xs, norm_w, mods, 0, 1)
    h = jnp.concatenate([to_col_major(h[:, :SEQ]), h[:, SEQ:]], axis=1)
    p = pmatmul(h.reshape(bsz * NTOK, D_MODEL), w_in.astype(BF16)).reshape(bsz, NTOK, OD_IN)
    p_l, p_c = p[:, :SEQ], p[:, SEQ:]
    a_l = gla_mixer(p_c[..., :GLA_IN], p_l[..., :GLA_IN], *gla_args)
    r_l = rglru_mixer(p_c[..., GLA_IN:], p_l[..., GLA_IN:], *rg_args)
    mix = from_col_major(jnp.concatenate([a_l, r_l], axis=-1)).astype(BF16)
    return mm_resid([mix], [w_out.astype(BF16)], xs, mods, 2, N_LAT_TILES)


def kernel(x, c, ctx, c_ctx, ada_w, ada_b, norm1_w, norm2_w, ev_w_in, ev_w_out, hy_conv_w, hy_conv_b, hy_fw0, hy_fb0, hy_fw1, hy_fb1, hy_fw2, hy_fb2, hy_fw3, hy_freq, hy_bias, ssd_conv_w, ssd_conv_b, ssd_dt_bias, ssd_a_log, ssd_d, ssd_norm_w, od_w_in, od_w_out, gla_gate_w, gla_gate_b, gla_norm_w, rg_conv_w, rg_conv_b, rg_w_a, rg_b_a, rg_w_x, rg_b_x, rg_lambda, router_w, router_b, moe_w_gate, moe_w_up, moe_w_down, sh_w_gate, sh_w_up, sh_w_down, final_norm_w):
    xs = jnp.concatenate([x, ctx], axis=1)
    for i in range(DEPTH):
        last = i == DEPTH - 1
        j = i // 2
        mods = adaln_table(c, c_ctx, ada_w[i], ada_b[i])
        if i % 2 == 0:
            hy_filt = (hy_fw0[j], hy_fb0[j], hy_fw1[j], hy_fb1[j], hy_fw2[j], hy_fb2[j], hy_fw3[j], hy_freq[j])
            xs = even_layer_mixer(xs, mods, norm1_w[i], ev_w_in[j], ev_w_out[j], hy_conv_w[j], hy_conv_b[j],
                                  hy_filt, hy_bias[j], ssd_conv_w[j], ssd_conv_b[j], ssd_dt_bias[j],
                                  ssd_a_log[j], ssd_d[j], ssd_norm_w[j])
        else:
            gla_args = (gla_gate_w[j], gla_gate_b[j], gla_norm_w[j])
            rg_args = (rg_conv_w[j], rg_conv_b[j], rg_w_a[j], rg_b_a[j], rg_w_x[j], rg_b_x[j], rg_lambda[j])
            xs = odd_layer_mixer_pallas(xs, mods, norm1_w[i], od_w_in[j], od_w_out[j], gla_args, rg_args)
        n_tiles = N_LAT_TILES if last else N_ROW_TILES
        xs = moe_layer(xs, norm2_w[i], mods, router_w[i], router_b[i], moe_w_gate, moe_w_up, moe_w_down,
                       sh_w_gate[i], sh_w_up[i], sh_w_down[i], n_tiles, i)
    return final_norm(xs, final_norm_w)
```

```python
import functools
import math

import jax
import jax.numpy as jnp
from jax import lax
from jax.experimental import pallas as pl
from jax.experimental.pallas import tpu as pltpu

D_MODEL = 1024
BATCH = 16
SEQ = 2048
DEPTH = 2

CTX_LEN = 256
GRID_W = 64
NORM_EPS = 1e-6

D_HY = D_MODEL // 2
HY_ORDER = 2
HY_SHORT = 3
HY_BANDS = 16
HY_EMB = 1 + 2 * HY_BANDS
HY_FF = 64
HY_DECAY_PCT_LO = 0.3
HY_DECAY_PCT_HI = 1.5
HY_DECAY_TARGET = 1e-2
HY_IN = 3 * D_HY

D_SSM = D_MODEL // 2
SSD_P = 64
SSD_H = D_SSM // SSD_P
SSD_G = 2
SSD_HPG = SSD_H // SSD_G
SSD_N = 128
SSD_CONV = 4
SSD_CHUNK = 128
SSD_XBC = D_SSM + 2 * SSD_G * SSD_N
SSD_IN = D_SSM + SSD_XBC + 2 * SSD_H
EV_IN = HY_IN + SSD_IN
EV_MIX = D_HY + D_SSM

GLA_H = 4
GLA_DV = (D_MODEL // 2) // GLA_H
GLA_DK = GLA_DV // 2
GLA_RANK = 16
GLA_GATE_NORM = 16.0
GLA_CHUNK = 64
GLA_IN = 2 * GLA_H * GLA_DK + 2 * GLA_H * GLA_DV + 2 * GLA_RANK

D_RG = D_MODEL // 2
RG_BLOCKS = 8
RG_BW = D_RG // RG_BLOCKS
RG_CONV = 4
RG_C = 8.0
RG_IN = 2 * D_RG
OD_IN = GLA_IN + RG_IN
OD_MIX = GLA_H * GLA_DV + D_RG

MOE_EXPERTS = 64
MOE_TOPK = 8
MOE_D_EXPERT = 256
MOE_D_SHARED = 256
MOE_SCALE = 2.5
MOE_BLOCK = 512

F32 = jnp.float32
BF16 = jnp.bfloat16
HIGHEST = lax.Precision.HIGHEST

NTOK = SEQ + CTX_LEN
ROW_TILE = 256
N_ROW_TILES = NTOK // ROW_TILE
N_LAT_TILES = SEQ // ROW_TILE

VMEM_LIMIT = 48 * 1024 * 1024
VMEM_LIMIT_BIG = 56 * 1024 * 1024


def _cparams(sem, limit=VMEM_LIMIT):
    return pltpu.CompilerParams(dimension_semantics=sem, vmem_limit_bytes=limit)


def _pick_tile(n, pref):
    t = min(n, pref)
    while n % t:
        t //= 2
    return t


def _silu(x):
    return x / (1.0 + jnp.exp(-x))


def _softplus(x):
    return jnp.maximum(x, 0.0) + jnp.log(1.0 + jnp.exp(-jnp.abs(x)))


def _mm_bf16_body(a_ref, w_ref, o_ref):
    o_ref[...] = jnp.dot(a_ref[...].astype(BF16), w_ref[...].astype(BF16),
                         preferred_element_type=F32).astype(o_ref.dtype)


def _mm_f32_body(a_ref, w_ref, o_ref):
    o_ref[...] = jnp.dot(a_ref[...], w_ref[...], preferred_element_type=F32,
                         precision=HIGHEST).astype(o_ref.dtype)


def pmatmul(a, w, *, exact=False, out_dtype=F32, tm=512, tn=None):
    m, k = a.shape
    n = w.shape[1]
    tm = _pick_tile(m, tm)
    tn = n if tn is None else _pick_tile(n, tn)
    body = _mm_f32_body if exact else _mm_bf16_body
    return pl.pallas_call(
        body,
        name='mm',
        grid=(m // tm, n // tn),
        in_specs=[pl.BlockSpec((tm, k), lambda i, j: (i, 0)),
                  pl.BlockSpec((k, tn), lambda i, j: (0, j))],
        out_specs=pl.BlockSpec((tm, tn), lambda i, j: (i, j)),
        out_shape=jax.ShapeDtypeStruct((m, n), out_dtype),
        compiler_params=_cparams(("parallel", "parallel")),
    )(a, w)


def _mm_split_body(a_ref, w_ref, *o_refs, splits):
    a = a_ref[...]
    for o_ref, (start, width) in zip(o_refs, splits):
        o_ref[...] = jnp.dot(a, w_ref[:, start:start + width],
                             preferred_element_type=F32).astype(o_ref.dtype)


def mm_split(a, w, splits, dtypes, tm=512):
    m, k = a.shape
    n = w.shape[1]
    tm = _pick_tile(m, tm)
    return pl.pallas_call(
        functools.partial(_mm_split_body, splits=tuple(splits)),
        name='mm_split',
        grid=(m // tm,),
        in_specs=[pl.BlockSpec((tm, k), lambda i: (i, 0)),
                  pl.BlockSpec((k, n), lambda i: (0, 0))],
        out_specs=[pl.BlockSpec((tm, wd), lambda i: (i, 0)) for _, wd in splits],
        out_shape=[jax.ShapeDtypeStruct((m, wd), dt) for (_, wd), dt in zip(splits, dtypes)],
        compiler_params=_cparams(("parallel",)),
    )(a, w)


MIX_SAMPLES = 2


def _stack_samples(ref):
    return jnp.concatenate([ref[s] for s in range(MIX_SAMPLES)], axis=0)


def _resid_store(o_ref, x_ref, g_ref, acc):
    for s in range(MIX_SAMPLES):
        o_ref[s] = x_ref[s] + g_ref[s] * acc[s * ROW_TILE:(s + 1) * ROW_TILE]


def _mm_resid_body(*refs, n_pairs):
    a_refs = refs[:n_pairs]
    w_refs = refs[n_pairs:2 * n_pairs]
    x_ref, g_ref, o_ref = refs[2 * n_pairs:]
    acc = jnp.dot(_stack_samples(a_refs[0]), w_refs[0][...], preferred_element_type=F32)
    for a_ref, w_ref in zip(a_refs[1:], w_refs[1:]):
        acc = acc + jnp.dot(_stack_samples(a_ref), w_ref[...], preferred_element_type=F32)
    _resid_store(o_ref, x_ref, g_ref, acc)


def _gate_spec(gate_idx, d):
    return pl.BlockSpec((MIX_SAMPLES, None, None, 1, d), lambda i, j: (i, 1 - j // N_LAT_TILES, gate_idx, 0, 0))


def _mix_tile(width):
    return pl.BlockSpec((MIX_SAMPLES, ROW_TILE, width), lambda i, j: (i, j, 0))


def mm_resid(a_list, w_list, xs, mods, gate_idx, n_tiles):
    b, nt, d = xs.shape
    n_pairs = len(a_list)
    in_specs = [_mix_tile(a.shape[-1]) for a in a_list]
    in_specs += [pl.BlockSpec(w.shape, lambda i, j: (0, 0)) for w in w_list]
    in_specs += [_mix_tile(d), _gate_spec(gate_idx, d)]
    return pl.pallas_call(
        functools.partial(_mm_resid_body, n_pairs=n_pairs),
        name='mm_resid',
        grid=(b // MIX_SAMPLES, n_tiles),
        in_specs=in_specs,
        out_specs=_mix_tile(d),
        out_shape=jax.ShapeDtypeStruct(xs.shape, F32),
        input_output_aliases={2 * n_pairs: 0},
        compiler_params=_cparams(("parallel", "parallel")),
    )(*a_list, *w_list, xs, mods)


def _even_out_body(hy_ref, yf_ref, yb_ref, xs_ref, z_ref, d_ref, nw_ref, w1_ref, w2_ref, x_ref, g_ref, o_ref):
    gw = D_SSM // SSD_G
    merged = []
    for s in range(MIX_SAMPLES):
        y = yf_ref[s].astype(F32) + yb_ref[s].astype(F32) + d_ref[...] * xs_ref[s].astype(F32)
        g = y * _silu(z_ref[s].astype(F32))
        groups = []
        for k in range(SSD_G):
            gk = g[:, k * gw:(k + 1) * gw]
            ms = jnp.mean(gk * gk, axis=-1, keepdims=True)
            groups.append(gk * lax.rsqrt(ms + NORM_EPS) * nw_ref[:, k * gw:(k + 1) * gw])
        merged.append(jnp.concatenate(groups, axis=1).astype(BF16))
    acc = (jnp.dot(_stack_samples(hy_ref), w1_ref[...], preferred_element_type=F32)
           + jnp.dot(jnp.concatenate(merged, axis=0), w2_ref[...], preferred_element_type=F32))
    _resid_store(o_ref, x_ref, g_ref, acc)


def even_out_resid(y_hy, yf, yb, xbc, z, d_skip, norm_w, w_hy, w_ssd, xs, mods):
    b, nt, d = xs.shape
    d_chan = jnp.repeat(d_skip.astype(F32), SSD_P).reshape(1, D_SSM)
    vec = pl.BlockSpec((1, D_SSM), lambda i, j: (0, 0))
    wspec = pl.BlockSpec((D_SSM, d), lambda i, j: (0, 0))
    half = _mix_tile(D_SSM)
    return pl.pallas_call(
        _even_out_body,
        name='even_out',
        grid=(b // MIX_SAMPLES, N_ROW_TILES),
        in_specs=[half, half, half, half, half, vec, vec, wspec, wspec, _mix_tile(d), _gate_spec(2, d)],
        out_specs=_mix_tile(d),
        out_shape=jax.ShapeDtypeStruct(xs.shape, F32),
        input_output_aliases={9: 0},
        compiler_params=_cparams(("parallel", "parallel")),
    )(y_hy, yf, yb, xbc, z, d_chan, norm_w.reshape(1, D_SSM), w_hy, w_ssd, xs, mods)


def _mod_spec(idx, d, b0=0):
    return pl.BlockSpec((None, None, None, 1, d), lambda i, j: (i + b0, 1 - j // N_LAT_TILES, idx, 0, 0))


def adaln_table(c, c_ctx, w, b):
    cv = jax.nn.silu(jnp.concatenate([c, c_ctx[None, :]], axis=0))
    cv = jnp.pad(cv, ((0, 24 - cv.shape[0]), (0, 0)))
    m = pmatmul(cv, w, exact=True, tn=1536)[:BATCH + 1] + b
    per_sample = m[:BATCH]
    ctx_row = jnp.broadcast_to(m[BATCH][None, :], per_sample.shape)
    return jnp.stack([ctx_row, per_sample], axis=1).reshape(BATCH, 2, 6, 1, D_MODEL)


def _norm_mod(x, w, shift, scale):
    ms = jnp.mean(x * x, axis=-1, keepdims=True)
    return (x * lax.rsqrt(ms + NORM_EPS) * w) * (1.0 + scale) + shift


def _norm_mod_body(x_ref, w_ref, sh_ref, sc_ref, o_ref):
    o_ref[0] = _norm_mod(x_ref[0], w_ref[...], sh_ref[...], sc_ref[...]).astype(o_ref.dtype)


def norm_mod(xs, w, mods, shift_idx, scale_idx):
    b, nt, d = xs.shape
    return pl.pallas_call(
        _norm_mod_body,
        name='norm_mod',
        grid=(b, nt // ROW_TILE),
        in_specs=[pl.BlockSpec((1, ROW_TILE, d), lambda i, j: (i, j, 0)),
                  pl.BlockSpec((1, d), lambda i, j: (0, 0)),
                  _mod_spec(shift_idx, d), _mod_spec(scale_idx, d)],
        out_specs=pl.BlockSpec((1, ROW_TILE, d), lambda i, j: (i, j, 0)),
        out_shape=jax.ShapeDtypeStruct(xs.shape, BF16),
        compiler_params=_cparams(("parallel", "parallel")),
    )(xs, w.reshape(1, d), mods, mods)


def _norm_mm_split_body(x_ref, nw_ref, sh_ref, sc_ref, w_ref, *o_refs, splits):
    h = jnp.concatenate([_norm_mod(x_ref[s], nw_ref[...], sh_ref[s], sc_ref[s]).astype(BF16)
                         for s in range(MIX_SAMPLES)], axis=0)
    for o_ref, (start, width) in zip(o_refs, splits):
        y = jnp.dot(h, w_ref[:, start:start + width], preferred_element_type=F32).astype(o_ref.dtype)
        for s in range(MIX_SAMPLES):
            o_ref[s] = y[s * ROW_TILE:(s + 1) * ROW_TILE]


def norm_mm_split(xs, norm_w, mods, w, splits, dtypes):
    b, nt, d = xs.shape
    n = w.shape[1]
    return pl.pallas_call(
        functools.partial(_norm_mm_split_body, splits=tuple(splits)),
        name='norm_mm_split',
        grid=(b // MIX_SAMPLES, nt // ROW_TILE),
        in_specs=[_mix_tile(d), pl.BlockSpec((1, d), lambda i, j: (0, 0)), _gate_spec(0, d), _gate_spec(1, d),
                  pl.BlockSpec((d, n), lambda i, j: (0, 0))],
        out_specs=[_mix_tile(wd) for _, wd in splits],
        out_shape=[jax.ShapeDtypeStruct((b, nt, wd), dt) for (_, wd), dt in zip(splits, dtypes)],
        compiler_params=_cparams(("parallel", "parallel")),
    )(xs, norm_w.reshape(1, d), mods, mods, w)


def _final_norm_body(x_ref, w_ref, o_ref):
    x = x_ref[0]
    ms = jnp.mean(x * x, axis=-1, keepdims=True)
    o_ref[0] = x * lax.rsqrt(ms + NORM_EPS) * w_ref[...]


def final_norm(xs, w):
    b, _, d = xs.shape
    return pl.pallas_call(
        _final_norm_body,
        name='final_norm',
        grid=(b, N_LAT_TILES),
        in_specs=[pl.BlockSpec((1, ROW_TILE, d), lambda i, j: (i, j, 0)),
                  pl.BlockSpec((1, d), lambda i, j: (0, 0))],
        out_specs=pl.BlockSpec((1, ROW_TILE, d), lambda i, j: (i, j, 0)),
        out_shape=jax.ShapeDtypeStruct((b, SEQ, d), F32),
        compiler_params=_cparams(("parallel", "parallel")),
    )(xs, w.reshape(1, d))


def _route_body(x_ref, w_ref, sh_ref, sc_ref, rw_ref, rb_ref, h_ref, idx_ref, wsel_ref, rank_ref, cnt_ref,
                *, group_size):
    first = (pl.program_id(0) % group_size == 0) & (pl.program_id(1) == 0)

    @pl.when(first)
    def _():
        cnt_ref[...] = jnp.zeros_like(cnt_ref)

    h = _norm_mod(x_ref[0], w_ref[...], sh_ref[...], sc_ref[...])
    h_ref[0] = h.astype(h_ref.dtype)
    logits = jnp.dot(h, rw_ref[...], preferred_element_type=F32, precision=HIGHEST)
    scores = 1.0 / (1.0 + jnp.exp(-logits))
    tm, ne = scores.shape
    lane = lax.broadcasted_iota(jnp.int32, (tm, ne), 1).astype(F32)
    slot = lax.broadcasted_iota(jnp.int32, (tm, MOE_TOPK), 1)
    sel = scores + rb_ref[...]
    picked = jnp.zeros((tm, ne), F32)
    hits = []
    idx_out = jnp.zeros((tm, MOE_TOPK), F32)
    w_out = jnp.zeros((tm, MOE_TOPK), F32)
    for k in range(MOE_TOPK):
        m = jnp.max(sel, axis=-1, keepdims=True)
        ik = jnp.min(jnp.where(sel == m, lane, float(ne)), axis=-1, keepdims=True)
        hit = lane == ik
        wk = jnp.sum(jnp.where(hit, scores, 0.0), axis=-1, keepdims=True)
        sel = jnp.where(hit, -jnp.inf, sel)
        picked = picked + hit.astype(F32)
        hits.append(hit)
        idx_out = jnp.where(slot == k, ik, idx_out)
        w_out = jnp.where(slot == k, wk, w_out)
    wsum = jnp.sum(w_out, axis=-1, keepdims=True)
    wsel_ref[0] = w_out / wsum * MOE_SCALE
    idx_ref[0] = idx_out.astype(jnp.int32)
    r_i = lax.broadcasted_iota(jnp.int32, (tm, tm), 0)
    c_i = lax.broadcasted_iota(jnp.int32, (tm, tm), 1)
    strict_lower = (c_i < r_i).astype(BF16)
    before = jnp.dot(strict_lower, picked.astype(BF16), preferred_element_type=F32) + cnt_ref[...]
    rank_out = jnp.zeros((tm, MOE_TOPK), F32)
    for k in range(MOE_TOPK):
        rk = jnp.sum(jnp.where(hits[k], before, 0.0), axis=-1, keepdims=True)
        rank_out = jnp.where(slot == k, rk, rank_out)
    rank_ref[0] = rank_out.astype(jnp.int32)
    cnt_ref[...] = cnt_ref[...] + jnp.sum(picked, axis=0, keepdims=True)


def route(xs, w, mods, router_w, router_b, n_tiles, group_size):
    b, _, d = xs.shape
    rows = n_tiles * ROW_TILE
    small = lambda dt: jax.ShapeDtypeStruct((b, rows, MOE_TOPK), dt)
    small_spec = pl.BlockSpec((1, ROW_TILE, MOE_TOPK), lambda i, j: (i, j, 0))
    return pl.pallas_call(
        functools.partial(_route_body, group_size=group_size),
        name='route',
        grid=(b, n_tiles),
        in_specs=[pl.BlockSpec((1, ROW_TILE, d), lambda i, j: (i, j, 0)),
                  pl.BlockSpec((1, d), lambda i, j: (0, 0)),
                  _mod_spec(3, d), _mod_spec(4, d),
                  pl.BlockSpec((d, MOE_EXPERTS), lambda i, j: (0, 0)),
                  pl.BlockSpec((1, MOE_EXPERTS), lambda i, j: (0, 0))],
        out_specs=[pl.BlockSpec((1, ROW_TILE, d), lambda i, j: (i, j, 0)),
                   small_spec, small_spec, small_spec,
                   pl.BlockSpec((None, 1, MOE_EXPERTS), lambda i, j: (i // group_size, 0, 0))],
        out_shape=[jax.ShapeDtypeStruct((b, rows, d), BF16), small(jnp.int32), small(F32), small(jnp.int32),
                   jax.ShapeDtypeStruct((b // group_size, 1, MOE_EXPERTS), F32)],
        compiler_params=_cparams(("arbitrary", "arbitrary")),
    )(xs, w.reshape(1, d), mods, mods, router_w, router_b.reshape(1, MOE_EXPERTS))


def _route_t_body(x_ref, w_ref, sh_ref, sc_ref, rwh_ref, rwl_ref, rb_ref, h_ref, idx_ref, wsel_ref, rank_ref, cnt_ref,
                  *, group_size):
    first = (pl.program_id(0) % group_size == 0) & (pl.program_id(1) == 0)

    @pl.when(first)
    def _():
        cnt_ref[...] = jnp.zeros_like(cnt_ref)

    h = _norm_mod(x_ref[0], w_ref[...], sh_ref[...], sc_ref[...])
    h_hi = h.astype(BF16)
    h_ref[0] = h_hi
    h_lo = (h - h_hi.astype(F32)).astype(BF16)
    nt = (((1,), (1,)), ((), ()))
    logits = (lax.dot_general(rwh_ref[...], h_hi, nt, preferred_element_type=F32)
              + lax.dot_general(rwh_ref[...], h_lo, nt, preferred_element_type=F32)
              + lax.dot_general(rwl_ref[...], h_hi, nt, preferred_element_type=F32))
    scores = 1.0 / (1.0 + jnp.exp(-logits))
    ne, tm = scores.shape
    expert = lax.broadcasted_iota(jnp.int32, (ne, tm), 0).astype(F32)
    slot = lax.broadcasted_iota(jnp.int32, (MOE_TOPK, tm), 0)
    sel = scores + rb_ref[...]
    picked = jnp.zeros((ne, tm), F32)
    hits = []
    idx_out = jnp.zeros((MOE_TOPK, tm), F32)
    w_out = jnp.zeros((MOE_TOPK, tm), F32)
    for k in range(MOE_TOPK):
        m = jnp.max(sel, axis=0, keepdims=True)
        ik = jnp.min(jnp.where(sel == m, expert, float(ne)), axis=0, keepdims=True)
        hit = expert == ik
        wk = jnp.sum(jnp.where(hit, scores, 0.0), axis=0, keepdims=True)
        sel = jnp.where(hit, -jnp.inf, sel)
        picked = picked + hit.astype(F32)
        hits.append(hit)
        idx_out = jnp.where(slot == k, ik, idx_out)
        w_out = jnp.where(slot == k, wk, w_out)
    wsum = jnp.sum(w_out, axis=0, keepdims=True)
    wsel_ref[0] = w_out / wsum * MOE_SCALE
    idx_ref[0] = idx_out.astype(jnp.int32)
    r_i = lax.broadcasted_iota(jnp.int32, (tm, tm), 0)
    c_i = lax.broadcasted_iota(jnp.int32, (tm, tm), 1)
    earlier = (r_i < c_i).astype(BF16)
    before = jnp.dot(picked.astype(BF16), earlier, preferred_element_type=F32) + cnt_ref[...]
    rank_out = jnp.zeros((MOE_TOPK, tm), F32)
    for k in range(MOE_TOPK):
        rk = jnp.sum(jnp.where(hits[k], before, 0.0), axis=0, keepdims=True)
        rank_out = jnp.where(slot == k, rk, rank_out)
    rank_ref[0] = rank_out.astype(jnp.int32)
    cnt_ref[...] = cnt_ref[...] + jnp.sum(picked, axis=1, keepdims=True)


def route_t(xs, w, mods, router_w, router_b, n_tiles, group_size):
    b, _, d = xs.shape
    rows = n_tiles * ROW_TILE
    rwt = router_w.T.astype(F32)
    rwt_hi, rwt_lo = _split_bf16(rwt)
    small = lambda dt: jax.ShapeDtypeStruct((b, MOE_TOPK, rows), dt)
    small_spec = pl.BlockSpec((1, MOE_TOPK, ROW_TILE), lambda i, j: (i, 0, j))
    return pl.pallas_call(
        functools.partial(_route_t_body, group_size=group_size),
        name='route',
        grid=(b, n_tiles),
        in_specs=[pl.BlockSpec((1, ROW_TILE, d), lambda i, j: (i, j, 0)),
                  pl.BlockSpec((1, d), lambda i, j: (0, 0)),
                  _mod_spec(3, d), _mod_spec(4, d),
                  pl.BlockSpec((MOE_EXPERTS, d), lambda i, j: (0, 0)),
                  pl.BlockSpec((MOE_EXPERTS, d), lambda i, j: (0, 0)),
                  pl.BlockSpec((MOE_EXPERTS, 1), lambda i, j: (0, 0))],
        out_specs=[pl.BlockSpec((1, ROW_TILE, d), lambda i, j: (i, j, 0)),
                   small_spec, small_spec, small_spec,
                   pl.BlockSpec((None, MOE_EXPERTS, 1), lambda i, j: (i // group_size, 0, 0))],
        out_shape=[jax.ShapeDtypeStruct((b, rows, d), BF16), small(jnp.int32), small(F32), small(jnp.int32),
                   jax.ShapeDtypeStruct((b // group_size, MOE_EXPERTS, 1), F32)],
        compiler_params=_cparams(("arbitrary", "arbitrary")),
    )(xs, w.reshape(1, d), mods, mods, rwt_hi, rwt_lo, router_b.astype(F32).reshape(MOE_EXPERTS, 1))


def _swiglu(x, wg, wu, wd):
    g = jnp.dot(x, wg, preferred_element_type=F32)
    u = jnp.dot(x, wu, preferred_element_type=F32)
    h = (_silu(g) * u).astype(BF16)
    return jnp.dot(h, wd, preferred_element_type=F32)


def _expert_body(be_ref, nu_ref, xa_ref, xb_ref, wg_ref, wu_ref, wd_ref, o_ref, wg_s, wu_s, wd_s):
    i = pl.program_id(0)
    used = i < nu_ref[0]

    @pl.when(used & ((i == 0) | (be_ref[i] != be_ref[jnp.maximum(i - 1, 0)])))
    def _():
        wg_s[...] = wg_ref[0].astype(BF16)
        wu_s[...] = wu_ref[0].astype(BF16)
        wd_s[...] = wd_ref[0].astype(BF16)

    @pl.when(used)
    def _():
        x = jnp.concatenate([xa_ref[...], xb_ref[...]], axis=1)
        o_ref[...] = _swiglu(x, wg_s[...], wu_s[...], wd_s[...]).astype(o_ref.dtype)

    @pl.when(jnp.logical_not(used))
    def _():
        o_ref[...] = jnp.zeros_like(o_ref)


def moe_experts(x_rows, block_e, n_used, wg, wu, wd, layer):
    rows, d = x_rows.shape
    n_blocks = rows // MOE_BLOCK
    f = wg.shape[-1]
    grid_spec = pltpu.PrefetchScalarGridSpec(
        num_scalar_prefetch=2,
        grid=(n_blocks,),
        in_specs=[
            pl.BlockSpec((MOE_BLOCK, d // 2), lambda i, be, nu: (i, 0)),
            pl.BlockSpec((MOE_BLOCK, d // 2), lambda i, be, nu: (i, 1)),
            pl.BlockSpec((None, 1, d, f), lambda i, be, nu: (layer, be[i], 0, 0)),
            pl.BlockSpec((None, 1, d, f), lambda i, be, nu: (layer, be[i], 0, 0)),
            pl.BlockSpec((None, 1, f, d), lambda i, be, nu: (layer, be[i], 0, 0)),
        ],
        out_specs=pl.BlockSpec((MOE_BLOCK, d), lambda i, be, nu: (i, 0)),
        scratch_shapes=[pltpu.VMEM((d, f), BF16), pltpu.VMEM((d, f), BF16), pltpu.VMEM((f, d), BF16)],
    )
    return pl.pallas_call(
        _expert_body,
        name='experts',
        grid_spec=grid_spec,
        out_shape=jax.ShapeDtypeStruct((rows, d), BF16),
        compiler_params=_cparams(("arbitrary",)),
    )(block_e, n_used, x_rows, x_rows, wg, wu, wd)


COMBINE_SAMPLES = 2


def _shared_resid_body(h_ref, wg_ref, wu_ref, wd_ref, pk_ref, ws_ref, x_ref, g_ref, o_ref):
    for s in range(COMBINE_SAMPLES):
        y = _swiglu(h_ref[s], wg_ref[...], wu_ref[...], wd_ref[...])
        ws = ws_ref[s]
        for k in range(MOE_TOPK):
            y = y + ws[:, k:k + 1] * pk_ref[k, s].astype(F32)
        o_ref[s] = x_ref[s] + g_ref[s] * y


def shared_resid(h, picked, wsel, xs, mods, wg, wu, wd, n_tiles, b0):
    b, _, d = h.shape
    f = wg.shape[-1]
    ns = COMBINE_SAMPLES
    s0 = b0 // ns
    tile = pl.BlockSpec((ns, ROW_TILE, d), lambda i, j: (i, j, 0))
    xs_tile = pl.BlockSpec((ns, ROW_TILE, d), lambda i, j: (i + s0, j, 0))
    gate = pl.BlockSpec((ns, None, None, 1, d), lambda i, j: (i + s0, 1 - j // N_LAT_TILES, 5, 0, 0))
    return pl.pallas_call(
        _shared_resid_body,
        name='shared_resid',
        grid=(b // ns, n_tiles),
        in_specs=[tile,
                  pl.BlockSpec((d, f), lambda i, j: (0, 0)),
                  pl.BlockSpec((d, f), lambda i, j: (0, 0)),
                  pl.BlockSpec((f, d), lambda i, j: (0, 0)),
                  pl.BlockSpec((MOE_TOPK, ns, ROW_TILE, d), lambda i, j: (0, i, j, 0)),
                  pl.BlockSpec((ns, ROW_TILE, MOE_TOPK), lambda i, j: (i, j, 0)),
                  xs_tile, gate],
        out_specs=xs_tile,
        out_shape=jax.ShapeDtypeStruct(xs.shape, F32),
        input_output_aliases={6: 0},
        compiler_params=_cparams(("parallel", "parallel")),
    )(h, wg, wu, wd, picked, wsel, xs, mods)


MOE_GROUPS = 2


def moe_layer(xs, norm_w, mods, router_w, router_b, w_gate, w_up, w_down, sh_gate, sh_up, sh_down, n_tiles, layer):
    bsz, _, d = xs.shape
    shared_w = (sh_gate.astype(BF16), sh_up.astype(BF16), sh_down.astype(BF16))
    b = bsz // MOE_GROUPS
    h, idx, wsel, rank, counts = route_t(xs, norm_w, mods, router_w, router_b, n_tiles, b)
    h_flat = h.reshape(-1, d)
    wsel = jnp.swapaxes(wsel, 1, 2)
    for g in range(MOE_GROUPS):
        sl = slice(g * b, (g + 1) * b)
        xs = _moe_group(xs, mods, h_flat, h[sl], idx[sl], wsel[sl], rank[sl], counts[g, :, 0], w_gate, w_up, w_down,
                        shared_w, n_tiles, layer, g * b)
    return xs


def _moe_group(xs, mods, h_flat, h, idx, wsel, rank, counts, w_gate, w_up, w_down, shared_w, n_tiles, layer, b0):
    b, rows_per_sample, d = h.shape
    n = b * rows_per_sample
    counts = counts.astype(jnp.int32)
    padded = (counts + MOE_BLOCK - 1) // MOE_BLOCK * MOE_BLOCK
    ends = jnp.cumsum(padded)
    starts = ends - padded
    nk = n * MOE_TOPK
    n_blocks = -(-nk // MOE_BLOCK) + MOE_EXPERTS
    rows = n_blocks * MOE_BLOCK
    n_pad = rows - nk
    e_iota = jnp.arange(MOE_EXPERTS, dtype=jnp.int32)
    dest = jnp.sum(jnp.where(idx[..., None] == e_iota, starts, 0), axis=-1) + rank
    blk_start = jnp.arange(n_blocks, dtype=jnp.int32) * MOE_BLOCK
    block_e = jnp.minimum(jnp.sum(ends[None, :] <= blk_start[:, None], axis=1), MOE_EXPERTS - 1).astype(jnp.int32)
    n_used = (ends[-1:] // MOE_BLOCK).astype(jnp.int32)
    pad = padded - counts
    cum_pad = jnp.cumsum(pad)
    m = jnp.arange(n_pad, dtype=jnp.int32)
    e_m = jnp.sum(cum_pad[None, :] <= m[:, None], axis=1)
    base = jnp.sum(jnp.where(jnp.minimum(e_m, MOE_EXPERTS - 1)[:, None] == e_iota,
                             starts + counts - (cum_pad - pad), 0), axis=1)
    pad_row = jnp.where(e_m < MOE_EXPERTS, base + m, ends[-1] + m - cum_pad[-1])
    tok0 = b0 * rows_per_sample
    tok = (tok0 + jnp.arange(b, dtype=jnp.int32)[:, None, None] * rows_per_sample
           + jnp.arange(rows_per_sample, dtype=jnp.int32)[None, None, :])
    tok = jnp.broadcast_to(tok, dest.shape).reshape(-1)
    _, row_tok = lax.sort((jnp.concatenate([dest.reshape(-1), pad_row]).astype(jnp.int32),
                           jnp.concatenate([tok, tok0 + m % n])), num_keys=1)
    x_rows = h_flat[row_tok]
    y_rows = moe_experts(x_rows, block_e, n_used, w_gate, w_up, w_down, layer)
    picked = y_rows[jnp.swapaxes(dest, 0, 1)]
    return shared_resid(h, picked, wsel, xs, mods, *shared_w, n_tiles, b0)


def _dwconv_body(x_ref, w_ref, b_ref, o_ref, *, width, act):
    chunk = ROW_TILE
    n_chunks = NTOK // chunk
    first_of_seq = (0, N_LAT_TILES)
    last_of_seq = (N_LAT_TILES - 1, n_chunks - 1)
    tc = x_ref.shape[-1]
    halo = 16
    row = lax.broadcasted_iota(jnp.int32, (chunk, tc), 0)
    zero_row = jnp.zeros((1, tc), F32)
    for c in range(n_chunks):
        r0 = c * chunk
        cur = x_ref[0, r0:r0 + chunk, :].astype(F32)
        if c in first_of_seq:
            prev_last = zero_row
        else:
            prev_last = x_ref[0, r0 - halo:r0, :].astype(F32)[halo - 1:halo, :]
        if c in last_of_seq:
            next0 = next1 = zero_row
        else:
            nxt = x_ref[0, r0 + chunk:r0 + chunk + halo, :].astype(F32)
            next0, next1 = nxt[0:1, :], nxt[1:2, :]
        xm1 = jnp.where(row == 0, prev_last, pltpu.roll(cur, 1, 0))
        xp1 = jnp.where(row == chunk - 1, next0, pltpu.roll(cur, chunk - 1, 0))
        y = w_ref[0:1, :] * xm1 + w_ref[1:2, :] * cur + w_ref[2:3, :] * xp1 + b_ref[...]
        if width == 4:
            xp2 = jnp.where(row == chunk - 2, next0,
                            jnp.where(row == chunk - 1, next1, pltpu.roll(cur, chunk - 2, 0)))
            y = y + w_ref[3:4, :] * xp2
        if act:
            y = _silu(y)
        o_ref[0, c * chunk:(c + 1) * chunk, :] = y.astype(o_ref.dtype)


def dwconv_stream(x, w, b, act, tc=256):
    bsz, nt, c = x.shape
    width = w.shape[0]
    return pl.pallas_call(
        functools.partial(_dwconv_body, width=width, act=act),
        name='dwconv',
        grid=(bsz, c // tc),
        in_specs=[pl.BlockSpec((1, nt, tc), lambda i, j: (i, 0, j)),
                  pl.BlockSpec((width, tc), lambda i, j: (0, j)),
                  pl.BlockSpec((1, tc), lambda i, j: (0, j))],
        out_specs=pl.BlockSpec((1, nt, tc), lambda i, j: (i, 0, j)),
        out_shape=jax.ShapeDtypeStruct(x.shape, BF16),
        compiler_params=_cparams(("parallel", "parallel")),
    )(x, w, b.reshape(1, c))


HY_FB = 512
DFT_SPLIT = 64


def dft_matrices(n):
    t = jnp.arange(n, dtype=jnp.int32)[None, :]
    ka = jnp.arange(DFT_SPLIT, dtype=jnp.int32)[:, None]
    kb = jnp.arange(n // DFT_SPLIT, dtype=jnp.int32)[:, None] * DFT_SPLIT
    ang_a = (2.0 * math.pi / (2 * n)) * ((ka * t) % (2 * n)).astype(F32)
    ang_b = (2.0 * math.pi / (2 * n)) * ((kb * t) % (2 * n)).astype(F32)
    ca, sa = jnp.cos(ang_a)[None], jnp.sin(ang_a)[None]
    cb, sb = jnp.cos(ang_b)[:, None], jnp.sin(ang_b)[:, None]
    cos_kt = (ca * cb - sa * sb).reshape(n, n)
    sin_kt = (sa * cb + ca * sb).reshape(n, n)
    idx = jnp.arange(n, dtype=jnp.int32)
    nyq = jnp.where(idx % 2 == 0, 1.0, -1.0).astype(F32)
    fwd = jnp.concatenate([cos_kt, jnp.where(idx[:, None] == 0, nyq[None, :], -sin_kt)], axis=0)
    scale = jnp.where(idx == 0, 0.5, 1.0)[None, :] / n
    inv = jnp.concatenate([cos_kt * scale, jnp.where(idx[None, :] == 0, nyq[:, None], -sin_kt) * scale], axis=1)
    return fwd.astype(BF16), inv.astype(BF16)


def hyena_filter_taps(n, fw0, fb0, fw1, fb1, fw2, fb2, fw3, freq):
    pos = jnp.arange(n, dtype=F32)
    t = pos / max(n - 1, 1)
    bands = jnp.linspace(1e-4, HY_BANDS - 1, HY_BANDS, dtype=F32)
    ang = (2.0 * math.pi / n) * pos[:, None] * bands[None, :]
    feats = jnp.concatenate([t[:, None], jnp.cos(ang), -jnp.sin(ang)], axis=-1)
    h = jnp.sin(freq * (jnp.dot(feats, fw0, precision=HIGHEST) + fb0))
    h = jnp.sin(freq * (jnp.dot(h, fw1, precision=HIGHEST) + fb1))
    h = jnp.sin(freq * (jnp.dot(h, fw2, precision=HIGHEST) + fb2))
    h = pmatmul(h, fw3, exact=True).reshape(n, 2, HY_ORDER, D_HY)
    deltas = jnp.abs(jnp.linspace(math.log(HY_DECAY_PCT_LO) / HY_DECAY_TARGET,
                                  math.log(HY_DECAY_PCT_HI) / HY_DECAY_TARGET, D_HY, dtype=F32))
    h = h * jnp.exp(-t[:, None] * deltas)[:, None, None, :]
    h0 = h[:, 0]
    h1 = h[:, 1].at[0].set(0.0)
    norm = jnp.sum(jnp.abs(h0), axis=0, keepdims=True) + jnp.sum(jnp.abs(h1), axis=0, keepdims=True)
    h0 = (h0 / norm).reshape(n, HY_ORDER * D_HY)
    h1 = (h1 / norm).reshape(n, HY_ORDER * D_HY)
    return h0 + h1, h0 - h1


def _split_bf16(a):
    hi = a.astype(BF16)
    return hi, (a - hi.astype(F32)).astype(BF16)


def hyena_spectrum(fwd, hsum, hdiff, fb):
    n = hsum.shape[0]
    a = pmatmul(fwd, hsum.astype(BF16))
    bm = pmatmul(fwd, hdiff.astype(BF16))
    sr = a[:n]
    si = bm[n:]
    nyq = a[n]
    first = (jnp.arange(n) == 0)[:, None]
    p = sr
    q = jnp.where(first, 0.0, si)
    s = jnp.where(first, nyq[None, :], sr)
    spec = jnp.stack([p, q, s], axis=0).reshape(3, n // fb, fb, HY_ORDER, D_HY)
    return spec.transpose(3, 1, 0, 2, 4)


def _hyena_body(u_ref, fre_ref, fim_ref, gre_ref, gim_ref, sp_ref, bias_ref, prev_ref, o_ref,
                vin, acc, *, nf):
    del prev_ref
    o = pl.program_id(1)
    f = pl.program_id(2)
    c = D_HY

    @pl.when((o == 0) & (f == 0))
    def _():
        vin[...] = u_ref[0, :, 0:c]

    @pl.when(f == 0)
    def _():
        acc[...] = jnp.zeros_like(acc)

    v = vin[...]
    vr = jnp.dot(fre_ref[...], v, preferred_element_type=F32)
    vi = jnp.dot(fim_ref[...], v, preferred_element_type=F32)
    p, q, s = sp_ref[0], sp_ref[1], sp_ref[2]
    zr = (vr * p - vi * q).astype(BF16)
    zi = (vr * q + vi * s).astype(BF16)
    acc[...] += (jnp.dot(gre_ref[...], zr, preferred_element_type=F32)
                 + jnp.dot(gim_ref[...], zi, preferred_element_type=F32))

    @pl.when((o == 0) & (f == nf - 1))
    def _():
        z = u_ref[0, :, c:2 * c].astype(F32) * (acc[...] + bias_ref[0:1, :] * vin[...].astype(F32))
        vin[...] = z.astype(BF16)

    @pl.when((o == 1) & (f == nf - 1))
    def _():
        y = u_ref[0, :, 2 * c:3 * c].astype(F32) * (acc[...] + bias_ref[1:2, :] * vin[...].astype(F32))
        o_ref[0] = y.astype(o_ref.dtype)


def hyena_long_conv(u, fwd_bf16, inv_bf16, spec, bias, n, row_block, prev_out):
    bsz = u.shape[0]
    fb = spec.shape[3]
    nf = n // fb
    out_shape = jax.ShapeDtypeStruct((bsz, NTOK, D_HY), BF16)
    if prev_out is None:
        prev_out = jnp.zeros(out_shape.shape, BF16)
    args = [u, fwd_bf16, fwd_bf16, inv_bf16, inv_bf16, spec, bias, prev_out]
    aliases = {7: 0}
    return pl.pallas_call(
        functools.partial(_hyena_body, nf=nf),
        name='hyena',
        grid=(bsz, HY_ORDER, nf),
        in_specs=[pl.BlockSpec((1, n, 3 * D_HY), lambda b, o, f: (b, row_block, 0)),
                  pl.BlockSpec((fb, n), lambda b, o, f: (f, 0)),
                  pl.BlockSpec((fb, n), lambda b, o, f: (nf + f, 0)),
                  pl.BlockSpec((n, fb), lambda b, o, f: (0, f)),
                  pl.BlockSpec((n, fb), lambda b, o, f: (0, nf + f)),
                  pl.BlockSpec((None, None, 3, fb, D_HY), lambda b, o, f: (o, f, 0, 0, 0)),
                  pl.BlockSpec((HY_ORDER, D_HY), lambda b, o, f: (0, 0)),
                  pl.BlockSpec(memory_space=pl.ANY)],
        out_specs=pl.BlockSpec((1, n, D_HY), lambda b, o, f: (b, row_block, 0)),
        out_shape=out_shape,
        scratch_shapes=[pltpu.VMEM((n, D_HY), BF16), pltpu.VMEM((n, D_HY), F32)],
        input_output_aliases=aliases,
        compiler_params=_cparams(("parallel", "arbitrary", "arbitrary"), VMEM_LIMIT_BIG),
    )(*args)


def hyena_mixer_stream(p_hy, conv_w, conv_b, filt, bias):
    u = dwconv_stream(p_hy, conv_w, conv_b, act=False)
    out = None
    for n, row_block in ((SEQ, 0), (CTX_LEN, SEQ // CTX_LEN)):
        fb = min(HY_FB, n)
        fwd, inv = dft_matrices(n)
        hsum, hdiff = hyena_filter_taps(n, *filt)
        spec = hyena_spectrum(fwd, hsum, hdiff, fb)
        out = hyena_long_conv(u, fwd, inv, spec, bias, n, row_block, out)
    return out


SCAN_SAMPLES = 2


def _tri(n, kind):
    r = lax.broadcasted_iota(jnp.int32, (n, n), 0)
    c = lax.broadcasted_iota(jnp.int32, (n, n), 1)
    return (c <= r) if kind == 'lower' else (c >= r)


def _ssd_dir(xbc_ref, dt_ref, dtt_ref, bias_r, bias_c, a_r, a_c, st_ref, y_ref, *, s, d, reverse):
    q = SSD_CHUNK
    nh = SSD_H
    gw = SSD_HPG * SSD_P
    lower = _tri(q, 'lower')
    upper = _tri(q, 'upper')
    lower_f = lower.astype(F32)
    upper_f = upper.astype(F32)
    dt_col = _softplus(dt_ref[s] + bias_r)
    dt_row = _softplus(dtt_ref[s] + bias_c)
    da_col = dt_col * a_r
    da_row = dt_row * a_c
    if not reverse:
        acs_col = jnp.dot(lower_f, da_col, preferred_element_type=F32, precision=HIGHEST)
        acs_row = jnp.dot(da_row, upper_f, preferred_element_type=F32, precision=HIGHEST)
        mask = lower
        edge = q - 1
    else:
        acs_col = jnp.dot(upper_f, da_col, preferred_element_type=F32, precision=HIGHEST)
        acs_row = jnp.dot(da_row, lower_f, preferred_element_type=F32, precision=HIGHEST)
        mask = upper
        edge = 0
    h0 = d * nh
    hh = lax.broadcasted_iota(jnp.int32, (2 * nh, nh * SSD_P), 0)
    cc = lax.broadcasted_iota(jnp.int32, (2 * nh, nh * SSD_P), 1) // SSD_P
    expand = (hh == cc + h0).astype(F32)
    acs_c = jnp.dot(acs_col, expand, preferred_element_type=F32, precision=HIGHEST)
    dt_c = jnp.dot(dt_col, expand, preferred_element_type=F32, precision=HIGHEST)
    total_c = acs_c[edge:edge + 1, :]
    e_in_c = jnp.exp(acs_c)
    w_end_c = jnp.exp(total_c - acs_c) * dt_c
    dec_c = jnp.exp(total_c)
    xs = xbc_ref[s, :, 0:D_SSM]
    xs_f = xs.astype(F32)
    for g in range(SSD_G):
        bm = xbc_ref[s, :, D_SSM + g * SSD_N:D_SSM + (g + 1) * SSD_N]
        cm = xbc_ref[s, :, D_SSM + SSD_G * SSD_N + g * SSD_N:D_SSM + SSD_G * SSD_N + (g + 1) * SSD_N]
        cb = lax.dot_general(cm, bm, (((1,), (1,)), ((), ())), preferred_element_type=F32)
        lws = []
        for k in range(SSD_HPG):
            h = h0 + g * SSD_HPG + k
            seg = acs_col[:, h:h + 1] - acs_row[h:h + 1, :]
            decay = jnp.exp(jnp.where(mask, seg, -jnp.inf))
            lws.append((cb * decay * dt_row[h:h + 1, :]).astype(BF16))
        lw = jnp.concatenate(lws, axis=1)
        xg = xs[:, g * gw:(g + 1) * gw]
        rb = lax.broadcasted_iota(jnp.int32, (SSD_HPG * q, gw), 0) // q
        cbk = lax.broadcasted_iota(jnp.int32, (SSD_HPG * q, gw), 1) // SSD_P
        x_bd = jnp.where(rb == cbk, jnp.concatenate([xg] * SSD_HPG, axis=0), jnp.zeros((), BF16))
        y_in = jnp.dot(lw, x_bd, preferred_element_type=F32)
        st = st_ref[s, g]
        y_st = jnp.dot(cm, st.astype(BF16), preferred_element_type=F32) * e_in_c[:, g * gw:(g + 1) * gw]
        y_ref[s, :, g * gw:(g + 1) * gw] = (y_in + y_st).astype(y_ref.dtype)
        xw = (xs_f[:, g * gw:(g + 1) * gw] * w_end_c[:, g * gw:(g + 1) * gw]).astype(BF16)
        upd = lax.dot_general(bm, xw, (((0,), (0,)), ((), ())), preferred_element_type=F32)
        st_ref[s, g] = st * dec_c[:, g * gw:(g + 1) * gw] + upd


def _ssd_body(xf_ref, dtf_ref, dttf_ref, xb_ref, dtb_ref, dttb_ref, bias_r, bias_c, a_r, a_c,
              yf_ref, yb_ref, stf, stb):
    @pl.when(pl.program_id(1) == 0)
    def _():
        stf[...] = jnp.zeros_like(stf)
        stb[...] = jnp.zeros_like(stb)

    for s in range(SCAN_SAMPLES):
        _ssd_dir(xf_ref, dtf_ref, dttf_ref, bias_r[...], bias_c[...], a_r[...], a_c[...], stf, yf_ref,
                 s=s, d=0, reverse=False)
        _ssd_dir(xb_ref, dtb_ref, dttb_ref, bias_r[...], bias_c[...], a_r[...], a_c[...], stb, yb_ref,
                 s=s, d=1, reverse=True)


def ssd_scan(xbc, dt, dt_bias, a_log):
    bsz = xbc.shape[0]
    nc = NTOK // SSD_CHUNK
    nlat = SEQ // SSD_CHUNK
    dtt = jnp.swapaxes(dt, 1, 2)
    fwd_chunk = lambda s: (s + nlat) % nc
    bwd_chunk = lambda s: nc - 1 - s
    a = -jnp.exp(a_log.astype(F32)).reshape(1, 2 * SSD_H)
    bias = dt_bias.astype(F32).reshape(1, 2 * SSD_H)
    ns = SCAN_SAMPLES
    x_spec = lambda cm: pl.BlockSpec((ns, SSD_CHUNK, SSD_XBC), lambda b, s: (b, cm(s), 0))
    dt_spec = lambda cm: pl.BlockSpec((ns, SSD_CHUNK, 2 * SSD_H), lambda b, s: (b, cm(s), 0))
    dtt_spec = lambda cm: pl.BlockSpec((ns, 2 * SSD_H, SSD_CHUNK), lambda b, s: (b, 0, cm(s)))
    y_spec = lambda cm: pl.BlockSpec((ns, SSD_CHUNK, D_SSM), lambda b, s: (b, cm(s), 0))
    row = pl.BlockSpec((1, 2 * SSD_H), lambda b, s: (0, 0))
    col = pl.BlockSpec((2 * SSD_H, 1), lambda b, s: (0, 0))
    y_shape = jax.ShapeDtypeStruct((bsz, NTOK, D_SSM), BF16)
    gw = SSD_HPG * SSD_P
    return pl.pallas_call(
        _ssd_body,
        name='ssd_scan',
        grid=(bsz // ns, nc),
        in_specs=[x_spec(fwd_chunk), dt_spec(fwd_chunk), dtt_spec(fwd_chunk),
                  x_spec(bwd_chunk), dt_spec(bwd_chunk), dtt_spec(bwd_chunk),
                  row, col, row, col],
        out_specs=[y_spec(fwd_chunk), y_spec(bwd_chunk)],
        out_shape=[y_shape, y_shape],
        scratch_shapes=[pltpu.VMEM((ns, SSD_G, SSD_N, gw), F32), pltpu.VMEM((ns, SSD_G, SSD_N, gw), F32)],
        compiler_params=_cparams(("parallel", "arbitrary")),
    )(xbc, dt, dtt, xbc, dt, dtt, bias, bias.reshape(-1, 1), a, a.reshape(-1, 1))


def _ssd_merge_body(yf_ref, yb_ref, xbc_ref, z_ref, d_ref, nw_ref, o_ref):
    xs = xbc_ref[0, :, 0:D_SSM].astype(F32)
    z = z_ref[0].astype(F32)
    g = (yf_ref[0] + yb_ref[0] + d_ref[...] * xs) * _silu(z)
    gw = D_SSM // SSD_G
    for k in range(SSD_G):
        gk = g[:, k * gw:(k + 1) * gw]
        ms = jnp.mean(gk * gk, axis=-1, keepdims=True)
        o_ref[0, :, k * gw:(k + 1) * gw] = (gk * lax.rsqrt(ms + NORM_EPS)
                                            * nw_ref[:, k * gw:(k + 1) * gw]).astype(o_ref.dtype)


def ssd_merge(yf, yb, xbc, z, d_skip, norm_w):
    bsz = yf.shape[0]
    tile = lambda w: pl.BlockSpec((1, ROW_TILE, w), lambda i, j: (i, j, 0))
    vec = pl.BlockSpec((1, D_SSM), lambda i, j: (0, 0))
    d_chan = jnp.repeat(d_skip.astype(F32), SSD_P).reshape(1, D_SSM)
    return pl.pallas_call(
        _ssd_merge_body,
        name='ssd_merge',
        grid=(bsz, N_ROW_TILES),
        in_specs=[tile(D_SSM), tile(D_SSM), tile(SSD_XBC), tile(D_SSM), vec, vec],
        out_specs=tile(D_SSM),
        out_shape=jax.ShapeDtypeStruct((bsz, NTOK, D_SSM), BF16),
        compiler_params=_cparams(("parallel", "parallel")),
    )(yf, yb, xbc, z, d_chan, norm_w.reshape(1, D_SSM))


def even_layer_mixer(xs, mods, norm_w, w_in, w_out, hy_conv_w, hy_conv_b, hy_filt, hy_bias,
                     ssd_conv_w, ssd_conv_b, ssd_dt_bias, ssd_a_log, ssd_d, ssd_norm_w):
    splits = ((0, HY_IN), (HY_IN, D_SSM), (HY_IN + D_SSM, SSD_XBC), (HY_IN + D_SSM + SSD_XBC, 2 * SSD_H))
    p_hy, z, xbc_raw, dt = norm_mm_split(xs, norm_w, mods, w_in.astype(BF16), splits, (BF16, BF16, BF16, F32))
    y_hy = hyena_mixer_stream(p_hy, hy_conv_w, hy_conv_b, hy_filt, hy_bias)
    xbc = dwconv_stream(xbc_raw, ssd_conv_w, ssd_conv_b, act=True)
    yf, yb = ssd_scan(xbc, dt, ssd_dt_bias, ssd_a_log)
    wo = w_out.astype(BF16)
    return even_out_resid(y_hy, yf, yb, xbc, z, ssd_d, ssd_norm_w, wo[:D_HY], wo[D_HY:], xs, mods)


GLA_QK = GLA_H * GLA_DK
GLA_V = GLA_H * GLA_DV


def _gla_dir(qkv_ref, lr_ref, gw_ref, gb_ref, st_ref, o_ref, *, s, d, reverse):
    q = GLA_CHUNK
    tri = _tri(q, 'upper' if reverse else 'lower')
    edge = 0 if reverse else q - 1
    lr = lr_ref[s, :, d * GLA_RANK:(d + 1) * GLA_RANK]
    logit = jnp.dot(lr, gw_ref[d], preferred_element_type=F32, precision=HIGHEST) + gb_ref[d:d + 1, :]
    log_g = -_softplus(-logit) * (1.0 / GLA_GATE_NORM)
    gcum = jnp.dot(tri.astype(F32), log_g, preferred_element_type=F32, precision=HIGHEST)
    total = gcum[edge:edge + 1, :]
    qf = qkv_ref[s, :, 0:GLA_QK].astype(F32)
    kf = qkv_ref[s, :, GLA_QK:2 * GLA_QK].astype(F32)
    v = qkv_ref[s, :, 2 * GLA_QK:2 * GLA_QK + GLA_V]
    qg = (qf * (GLA_DK ** -0.5) * jnp.exp(gcum)).astype(BF16)
    kg = (kf * jnp.exp(-gcum)).astype(BF16)
    kw = (kf * jnp.exp(total - gcum)).astype(BF16)
    rb = lax.broadcasted_iota(jnp.int32, (GLA_H * q, GLA_QK), 0) // q
    cb = lax.broadcasted_iota(jnp.int32, (GLA_H * q, GLA_QK), 1) // GLA_DK
    k_bd = jnp.where(rb == cb, jnp.concatenate([kg] * GLA_H, axis=0), jnp.zeros((), BF16))
    att = lax.dot_general(qg, k_bd, (((1,), (1,)), ((), ())), preferred_element_type=F32)
    i_i = lax.broadcasted_iota(jnp.int32, (q, GLA_H * q), 0)
    j_i = lax.broadcasted_iota(jnp.int32, (q, GLA_H * q), 1) % q
    keep = (j_i >= i_i) if reverse else (j_i <= i_i)
    att = jnp.where(keep, att, 0.0).astype(BF16)
    rv = lax.broadcasted_iota(jnp.int32, (GLA_H * q, GLA_V), 0) // q
    cv = lax.broadcasted_iota(jnp.int32, (GLA_H * q, GLA_V), 1) // GLA_DV
    v_bd = jnp.where(rv == cv, jnp.concatenate([v] * GLA_H, axis=0), jnp.zeros((), BF16))
    st = st_ref[s]
    o_in = jnp.dot(att, v_bd, preferred_element_type=F32)
    o_st = lax.dot_general(qg, st.astype(BF16), (((1,), (1,)), ((), ())), preferred_element_type=F32)
    o_ref[s] = (o_in + o_st).astype(o_ref.dtype)
    upd = lax.dot_general(v, kw, (((0,), (0,)), ((), ())), preferred_element_type=F32)
    rs = lax.broadcasted_iota(jnp.int32, (GLA_V, GLA_QK), 0) // GLA_DV
    cs = lax.broadcasted_iota(jnp.int32, (GLA_V, GLA_QK), 1) // GLA_DK
    st_ref[s] = st * jnp.exp(total) + jnp.where(rs == cs, upd, 0.0)


def _gla_body(qf_ref, lf_ref, qb_ref, lb_ref, gw_ref, gb_ref, of_ref, ob_ref, stf, stb):
    @pl.when(pl.program_id(1) == 0)
    def _():
        stf[...] = jnp.zeros_like(stf)
        stb[...] = jnp.zeros_like(stb)

    for s in range(SCAN_SAMPLES):
        _gla_dir(qf_ref, lf_ref, gw_ref, gb_ref, stf, of_ref, s=s, d=0, reverse=False)
        _gla_dir(qb_ref, lb_ref, gw_ref, gb_ref, stb, ob_ref, s=s, d=1, reverse=True)


def gla_scan(qkv, lr, gate_w, gate_b):
    bsz = qkv.shape[0]
    nc = NTOK // GLA_CHUNK
    nlat = SEQ // GLA_CHUNK
    fwd_chunk = lambda s: (s + nlat) % nc
    bwd_chunk = lambda s: nc - 1 - s
    ns = SCAN_SAMPLES
    q_spec = lambda cm: pl.BlockSpec((ns, GLA_CHUNK, qkv.shape[-1]), lambda b, s: (b, cm(s), 0))
    l_spec = lambda cm: pl.BlockSpec((ns, GLA_CHUNK, 2 * GLA_RANK), lambda b, s: (b, cm(s), 0))
    o_spec = lambda cm: pl.BlockSpec((ns, GLA_CHUNK, GLA_V), lambda b, s: (b, cm(s), 0))
    o_shape = jax.ShapeDtypeStruct((bsz, NTOK, GLA_V), BF16)
    return pl.pallas_call(
        _gla_body,
        name='gla_scan',
        grid=(bsz // ns, nc),
        in_specs=[q_spec(fwd_chunk), l_spec(fwd_chunk), q_spec(bwd_chunk), l_spec(bwd_chunk),
                  pl.BlockSpec((2, GLA_RANK, GLA_QK), lambda b, s: (0, 0, 0)),
                  pl.BlockSpec((2, GLA_QK), lambda b, s: (0, 0))],
        out_specs=[o_spec(fwd_chunk), o_spec(bwd_chunk)],
        out_shape=[o_shape, o_shape],
        scratch_shapes=[pltpu.VMEM((ns, GLA_V, GLA_QK), F32), pltpu.VMEM((ns, GLA_V, GLA_QK), F32)],
        compiler_params=_cparams(("parallel", "arbitrary")),
    )(qkv, lr, qkv, lr, gate_w.astype(F32), gate_b.astype(F32))


def _gla_merge_body(of_ref, ob_ref, r_ref, nw_ref, o_ref):
    o = of_ref[0].astype(F32) + ob_ref[0].astype(F32)
    r = r_ref[0].astype(F32)
    for h in range(GLA_H):
        sl = slice(h * GLA_DV, (h + 1) * GLA_DV)
        oh = o[:, sl]
        ms = jnp.mean(oh * oh, axis=-1, keepdims=True)
        o_ref[0, :, sl] = (oh * lax.rsqrt(ms + NORM_EPS) * nw_ref[:, sl] * _silu(r[:, sl])).astype(o_ref.dtype)


def gla_merge_stream(of, ob, r, norm_w):
    bsz = of.shape[0]
    tile = pl.BlockSpec((1, ROW_TILE, GLA_V), lambda i, j: (i, j, 0))
    return pl.pallas_call(
        _gla_merge_body,
        name='gla_merge',
        grid=(bsz, N_LAT_TILES),
        in_specs=[tile, tile, tile, pl.BlockSpec((1, GLA_V), lambda i, j: (0, 0))],
        out_specs=tile,
        out_shape=jax.ShapeDtypeStruct((bsz, SEQ, GLA_V), BF16),
        compiler_params=_cparams(("parallel", "parallel")),
    )(of, ob, r, norm_w.reshape(1, GLA_V))


RG_TILE = 8


def _gelu_tanh(x):
    return 0.5 * x * (1.0 + jnp.tanh(math.sqrt(2.0 / math.pi) * (x + 0.044715 * x * x * x)))


def _rg_scan_block(a_s, x_s, h_s, base, carry, reverse):
    n_tiles = ROW_TILE // RG_TILE
    row = lax.broadcasted_iota(jnp.int32, (RG_TILE, D_RG), 0)

    def tile_step(i, h_prev):
        t = (n_tiles - 1 - i) if reverse else i
        r0 = pl.multiple_of(t * RG_TILE, RG_TILE)
        a = a_s[pl.ds(r0, RG_TILE), :]
        x = x_s[pl.ds(r0, RG_TILE), :]
        for s in (1, 2, 4):
            if reverse:
                ok = row < RG_TILE - s
                shift = RG_TILE - s
            else:
                ok = row >= s
                shift = s
            a_sh = jnp.where(ok, pltpu.roll(a, shift, 0), 1.0)
            x_sh = jnp.where(ok, pltpu.roll(x, shift, 0), 0.0)
            x = a * x_sh + x
            a = a * a_sh
        h = x + a * h_prev
        h_s[pl.ds(base + r0, RG_TILE), :] = h
        edge = 0 if reverse else RG_TILE - 1
        return jnp.broadcast_to(h[edge:edge + 1, :], (RG_TILE, D_RG))

    return lax.fori_loop(0, n_tiles, tile_step, carry)


def _rglru_body(u_ref, g_ref, w_ref, b_ref, c_ref, o_ref, hf_s, a_s, x_s, hb_s):
    n_blocks = NTOK // ROW_TILE
    fwd_order = list(range(N_LAT_TILES, n_blocks)) + list(range(N_LAT_TILES))
    bwd_order = list(range(n_blocks - 1, N_LAT_TILES - 1, -1)) + list(range(N_LAT_TILES - 1, -1, -1))

    def gates(blk, d):
        ub = u_ref[0, blk * ROW_TILE:(blk + 1) * ROW_TILE, :]
        z = jnp.dot(ub, w_ref[:, 2 * d * D_RG:2 * (d + 1) * D_RG], preferred_element_type=F32)
        z = z + b_ref[:, 2 * d * D_RG:2 * (d + 1) * D_RG]
        r = 1.0 / (1.0 + jnp.exp(-z[:, :D_RG]))
        i = 1.0 / (1.0 + jnp.exp(-z[:, D_RG:]))
        a = jnp.exp(c_ref[d:d + 1, :] * r)
        a_s[...] = a
        x_s[...] = jnp.sqrt(1.0 - a * a) * i * ub.astype(F32)

    carry = jnp.zeros((RG_TILE, D_RG), F32)
    for blk in fwd_order:
        gates(blk, 0)
        carry = _rg_scan_block(a_s, x_s, hf_s, blk * ROW_TILE, carry, reverse=False)
    carry = jnp.zeros((RG_TILE, D_RG), F32)
    for blk in bwd_order:
        gates(blk, 1)
        carry = _rg_scan_block(a_s, x_s, hb_s, 0, carry, reverse=True)
        rows = slice(blk * ROW_TILE, (blk + 1) * ROW_TILE)
        gate = g_ref[0, rows, :].astype(F32)
        o_ref[0, rows, :] = ((hf_s[rows, :] + hb_s[...]) * _gelu_tanh(gate)).astype(o_ref.dtype)


def rglru_stream(u, gate, w_a, b_a, w_x, b_x, lam):
    bsz = u.shape[0]
    eye = jnp.eye(RG_BLOCKS, dtype=F32)
    dense = lambda w: jnp.einsum('nio,nm->nimo', w, eye).reshape(D_RG, D_RG)
    w_cat = jnp.concatenate([dense(w_a[0]), dense(w_x[0]), dense(w_a[1]), dense(w_x[1])], axis=1).astype(BF16)
    b_cat = jnp.concatenate([b_a[0], b_x[0], b_a[1], b_x[1]]).astype(F32).reshape(1, 4 * D_RG)
    c = -RG_C * jax.nn.softplus(-lam.astype(F32))
    seq = pl.BlockSpec((1, NTOK, D_RG), lambda i: (i, 0, 0))
    return pl.pallas_call(
        _rglru_body,
        name='rglru',
        grid=(bsz,),
        in_specs=[seq, seq,
                  pl.BlockSpec((D_RG, 4 * D_RG), lambda i: (0, 0)),
                  pl.BlockSpec((1, 4 * D_RG), lambda i: (0, 0)),
                  pl.BlockSpec((2, D_RG), lambda i: (0, 0))],
        out_specs=seq,
        out_shape=jax.ShapeDtypeStruct((bsz, NTOK, D_RG), BF16),
        scratch_shapes=[pltpu.VMEM((NTOK, D_RG), F32), pltpu.VMEM((ROW_TILE, D_RG), F32),
                        pltpu.VMEM((ROW_TILE, D_RG), F32), pltpu.VMEM((ROW_TILE, D_RG), F32)],
        compiler_params=_cparams(("parallel",)),
    )(u, gate, w_cat, b_cat, c)


def odd_layer_mixer_pallas(xs, mods, norm_w, w_in, w_out, gla_args, rg_args):
    bsz = xs.shape[0]
    gate_w, gate_b, gla_norm_w = gla_args
    rg_conv_w, rg_conv_b, w_a, b_a, w_x, b_x, lam = rg_args
    h = norm_mod(xs, norm_w, mods, 0, 1)
    h = jnp.concatenate([to_col_major(h[:, :SEQ]), h[:, SEQ:]], axis=1)
    nqk, nv = GLA_QK, GLA_V
    r0 = 2 * nqk + nv + 2 * GLA_RANK
    w = jnp.concatenate([w_in[:, :2 * nqk + nv], w_in[:, r0:r0 + nv], w_in[:, GLA_IN:],
                         w_in[:, 2 * nqk + nv:r0]], axis=1).astype(BF16)
    qkv_w = 2 * nqk + nv
    splits = ((0, qkv_w), (qkv_w, nv), (qkv_w + nv, D_RG), (qkv_w + nv + D_RG, D_RG),
              (qkv_w + nv + 2 * D_RG, 2 * GLA_RANK))
    qkv, r, u_raw, gate, lr = mm_split(h.reshape(bsz * NTOK, D_MODEL), w, splits, (BF16, BF16, BF16, BF16, F32))
    to3 = lambda a: a.reshape(bsz, NTOK, a.shape[-1])
    of, ob = gla_scan(to3(qkv), to3(lr), gate_w, gate_b.reshape(2, GLA_QK))
    a_l = gla_merge_stream(of, ob, to3(r), gla_norm_w)
    u = dwconv_stream(to3(u_raw), rg_conv_w, rg_conv_b, act=False)
    r_l = rglru_stream(u, to3(gate), w_a, b_a, w_x, b_x, lam)[:, :SEQ]
    wo = w_out.astype(BF16)
    return mm_resid([from_col_major(a_l), from_col_major(r_l)], [wo[:GLA_V], wo[GLA_V:]], xs, mods, 2, N_LAT_TILES)


def rms_norm(x, w):
    xf = x.astype(F32)
    y = xf * lax.rsqrt(jnp.mean(jnp.square(xf), axis=-1, keepdims=True) + NORM_EPS)
    return y.astype(x.dtype) * w


def dwconv(x, w, b):
    y = lax.conv_general_dilated(x, w[:, None, :].astype(x.dtype), window_strides=(1,), padding='SAME',
                                 dimension_numbers=('NWC', 'WIO', 'NWC'), feature_group_count=x.shape[-1])
    return y + b.astype(x.dtype)


def maybe_flip(a, rev):
    return jnp.flip(a, axis=1) if rev else a


def to_col_major(x):
    b, n, d = x.shape
    rows = n // GRID_W
    return x.reshape(b, rows, GRID_W, d).transpose(0, 2, 1, 3).reshape(b, n, d)


def from_col_major(x):
    b, n, d = x.shape
    rows = n // GRID_W
    return x.reshape(b, GRID_W, rows, d).transpose(0, 2, 1, 3).reshape(b, n, d)


def gla_inputs(p, gate_w, gate_b):
    b, n, _ = p.shape
    nqk, nv = GLA_H * GLA_DK, GLA_H * GLA_DV
    q = p[..., :nqk].reshape(b, n, GLA_H, GLA_DK) * GLA_DK ** -0.5
    k = p[..., nqk:2 * nqk].reshape(b, n, GLA_H, GLA_DK)
    v = p[..., 2 * nqk:2 * nqk + nv].reshape(b, n, GLA_H, GLA_DV)
    lr = p[..., 2 * nqk + nv:2 * nqk + nv + 2 * GLA_RANK].reshape(b, n, 2, GLA_RANK)
    r = p[..., 2 * nqk + nv + 2 * GLA_RANK:]
    logit = jnp.einsum('bler,erk->blek', lr, gate_w) + gate_b
    log_g = (jax.nn.log_sigmoid(logit.astype(F32)) / GLA_GATE_NORM).reshape(b, n, 2, GLA_H, GLA_DK)
    return q, k, v, log_g, r


def gla_states(k, v, log_g, s0):
    b, n = k.shape[:2]
    nc = n // GLA_CHUNK
    kc = k.reshape(b, nc, GLA_CHUNK, GLA_H, GLA_DK)
    vc = v.reshape(b, nc, GLA_CHUNK, GLA_H, GLA_DV)
    gcum = jnp.cumsum(log_g.reshape(b, nc, GLA_CHUNK, GLA_H, GLA_DK), axis=2)
    states = jnp.einsum('bcqhd,bcqhv->bchdv', kc * jnp.exp(gcum[:, :, -1:] - gcum), vc)
    chunk_decay = jnp.exp(gcum[:, :, -1])

    def step(s, inp):
        st, dcy = inp
        return dcy[..., None] * s + st, s

    s_fin, s_prev = lax.scan(step, s0, (jnp.moveaxis(states, 1, 0), jnp.moveaxis(chunk_decay, 1, 0)))
    return jnp.moveaxis(s_prev, 0, 1), s_fin


def gla_output(q, k, v, log_g, s_prev):
    b, n = q.shape[:2]
    nc = n // GLA_CHUNK
    qc = q.reshape(b, nc, GLA_CHUNK, GLA_H, GLA_DK)
    kc = k.reshape(b, nc, GLA_CHUNK, GLA_H, GLA_DK)
    vc = v.reshape(b, nc, GLA_CHUNK, GLA_H, GLA_DV)
    gcum = jnp.cumsum(log_g.reshape(b, nc, GLA_CHUNK, GLA_H, GLA_DK), axis=2)
    qg = qc * jnp.exp(gcum)
    kg = kc * jnp.exp(-gcum)
    mask = jnp.tril(jnp.ones((GLA_CHUNK, GLA_CHUNK), bool))
    att = jnp.where(mask, jnp.einsum('bcihd,bcjhd->bchij', qg, kg), 0.0)
    o = jnp.einsum('bchij,bcjhv->bcihv', att, vc) + jnp.einsum('bcihd,bchdv->bcihv', qg, s_prev)
    return o.reshape(b, n, GLA_H, GLA_DV)


def gla_merge(os_, r, norm_w):
    b, n = r.shape[:2]
    o = rms_norm(os_[0] + os_[1], norm_w.reshape(GLA_H, GLA_DV))
    return o.reshape(b, n, GLA_H * GLA_DV) * jax.nn.silu(r)


def gla_mixer(p_c, p_l, gate_w, gate_b, norm_w):
    q_c, k_c, v_c, g_c, r_c = gla_inputs(p_c, gate_w, gate_b)
    q_l, k_l, v_l, g_l, r_l = gla_inputs(p_l, gate_w, gate_b)
    s0 = jnp.zeros((p_c.shape[0], GLA_H, GLA_DK, GLA_DV), F32)
    os_l = []
    for d, rev in enumerate((False, True)):
        f = functools.partial(maybe_flip, rev=rev)
        _, sf_c = gla_states(f(k_c), f(v_c), f(g_c[:, :, d]), s0)
        sp_l, _ = gla_states(f(k_l), f(v_l), f(g_l[:, :, d]), sf_c)
        os_l.append(f(gla_output(f(q_l), f(k_l), f(v_l), f(g_l[:, :, d]), sp_l)))
    return gla_merge(os_l, r_l, norm_w)


def rglru_inputs(p, conv_w, conv_b, w_a, b_a, w_x, b_x, lam):
    b, n, _ = p.shape
    u = dwconv(p[..., :D_RG], conv_w, conv_b)
    ub = u.reshape(b, n, RG_BLOCKS, RG_BW)
    r = jax.nn.sigmoid((jnp.einsum('blni,enio->bleno', ub, w_a).reshape(b, n, 2, D_RG) + b_a).astype(F32))
    i = jax.nn.sigmoid((jnp.einsum('blni,enio->bleno', ub, w_x).reshape(b, n, 2, D_RG) + b_x).astype(F32))
    log_a = -RG_C * jax.nn.softplus(-lam.astype(F32)) * r
    x_in = jnp.sqrt(-jnp.expm1(2.0 * log_a)) * i * u[:, :, None, :].astype(F32)
    return p[..., D_RG:], jnp.exp(log_a), x_in


def lru_scan(a, u, h0):
    u = u.at[:, 0].add(a[:, 0] * h0)

    def combine(lhs, rhs):
        a1, b1 = lhs
        a2, b2 = rhs
        return a1 * a2, a2 * b1 + b2

    return lax.associative_scan(combine, (a, u), axis=1)[1]


def rglru_mixer(p_c, p_l, conv_w, conv_b, w_a, b_a, w_x, b_x, lam):
    gb_c, a_c, u_c = rglru_inputs(p_c, conv_w, conv_b, w_a, b_a, w_x, b_x, lam)
    gb_l, a_l, u_l = rglru_inputs(p_l, conv_w, conv_b, w_a, b_a, w_x, b_x, lam)
    h0 = jnp.zeros((p_c.shape[0], D_RG), F32)
    hs_l = []
    for d, rev in enumerate((False, True)):
        f = functools.partial(maybe_flip, rev=rev)
        h_c = f(lru_scan(f(a_c[:, :, d]), f(u_c[:, :, d]), h0))
        h_end = h_c[:, 0] if rev else h_c[:, -1]
        hs_l.append(f(lru_scan(f(a_l[:, :, d]), f(u_l[:, :, d]), h_end)))
    return (hs_l[0] + hs_l[1]) * jax.nn.gelu(gb_l.astype(F32))


def odd_layer_mixer(xs, mods, norm_w, w_in, w_out, gla_args, rg_args):
    bsz = xs.shape[0]
    h = norm_mod(xs, norm_w, mods, 0, 1)
    h = jnp.concatenate([to_col_major(h[:, :SEQ]), h[:, SEQ:]], axis=1)
    p = pmatmul(h.reshape(bsz * NTOK, D_MODEL), w_in.astype(BF16)).reshape(bsz, NTOK, OD_IN)
    p_l, p_c = p[:, :SEQ], p[:, SEQ:]
    a_l = gla_mixer(p_c[..., :GLA_IN], p_l[..., :GLA_IN], *gla_args)
    r_l = rglru_mixer(p_c[..., GLA_IN:], p_l[..., GLA_IN:], *rg_args)
    mix = from_col_major(jnp.concatenate([a_l, r_l], axis=-1)).astype(BF16)
    return mm_resid([mix], [w_out.astype(BF16)], xs, mods, 2, N_LAT_TILES)


def kernel(x, c, ctx, c_ctx, ada_w, ada_b, norm1_w, norm2_w, ev_w_in, ev_w_out, hy_conv_w, hy_conv_b, hy_fw0, hy_fb0, hy_fw1, hy_fb1, hy_fw2, hy_fb2, hy_fw3, hy_freq, hy_bias, ssd_conv_w, ssd_conv_b, ssd_dt_bias, ssd_a_log, ssd_d, ssd_norm_w, od_w_in, od_w_out, gla_gate_w, gla_gate_b, gla_norm_w, rg_conv_w, rg_conv_b, rg_w_a, rg_b_a, rg_w_x, rg_b_x, rg_lambda, router_w, router_b, moe_w_gate, moe_w_up, moe_w_down, sh_w_gate, sh_w_up, sh_w_down, final_norm_w):
    xs = jnp.concatenate([x, ctx], axis=1)
    for i in range(DEPTH):
        last = i == DEPTH - 1
        j = i // 2
        mods = adaln_table(c, c_ctx, ada_w[i], ada_b[i])
        if i % 2 == 0:
            hy_filt = (hy_fw0[j], hy_fb0[j], hy_fw1[j], hy_fb1[j], hy_fw2[j], hy_fb2[j], hy_fw3[j], hy_freq[j])
            xs = even_layer_mixer(xs, mods, norm1_w[i], ev_w_in[j], ev_w_out[j], hy_conv_w[j], hy_conv_b[j],
                                  hy_filt, hy_bias[j], ssd_conv_w[j], ssd_conv_b[j], ssd_dt_bias[j],
                                  ssd_a_log[j], ssd_d[j], ssd_norm_w[j])
        else:
            gla_args = (gla_gate_w[j], gla_gate_b[j], gla_norm_w[j])
            rg_args = (rg_conv_w[j], rg_conv_b[j], rg_w_a[j], rg_b_a[j], rg_w_x[j], rg_b_x[j], rg_lambda[j])
            xs = odd_layer_mixer_pallas(xs, mods, norm1_w[i], od_w_in[j], od_w_out[j], gla_args, rg_args)
        n_tiles = N_LAT_TILES if last else N_ROW_TILES
        xs = moe_layer(xs, norm2_w[i], mods, router_w[i], router_b[i], moe_w_gate, moe_w_up, moe_w_down,
                       sh_w_gate[i], sh_w_up[i], sh_w_down[i], n_tiles, i)
    return final_norm(xs, final_norm_w)
```

```python
import functools
import math

import jax
import jax.numpy as jnp
from jax import lax
from jax.experimental import pallas as pl
from jax.experimental.pallas import tpu as pltpu

D_MODEL = 1024
BATCH = 16
SEQ = 2048
DEPTH = 2

CTX_LEN = 256
GRID_W = 64
NORM_EPS = 1e-6

D_HY = D_MODEL // 2
HY_ORDER = 2
HY_SHORT = 3
HY_BANDS = 16
HY_EMB = 1 + 2 * HY_BANDS
HY_FF = 64
HY_DECAY_PCT_LO = 0.3
HY_DECAY_PCT_HI = 1.5
HY_DECAY_TARGET = 1e-2
HY_IN = 3 * D_HY

D_SSM = D_MODEL // 2
SSD_P = 64
SSD_H = D_SSM // SSD_P
SSD_G = 2
SSD_HPG = SSD_H // SSD_G
SSD_N = 128
SSD_CONV = 4
SSD_CHUNK = 128
SSD_XBC = D_SSM + 2 * SSD_G * SSD_N
SSD_IN = D_SSM + SSD_XBC + 2 * SSD_H
EV_IN = HY_IN + SSD_IN
EV_MIX = D_HY + D_SSM

GLA_H = 4
GLA_DV = (D_MODEL // 2) // GLA_H
GLA_DK = GLA_DV // 2
GLA_RANK = 16
GLA_GATE_NORM = 16.0
GLA_CHUNK = 64
GLA_IN = 2 * GLA_H * GLA_DK + 2 * GLA_H * GLA_DV + 2 * GLA_RANK

D_RG = D_MODEL // 2
RG_BLOCKS = 8
RG_BW = D_RG // RG_BLOCKS
RG_CONV = 4
RG_C = 8.0
RG_IN = 2 * D_RG
OD_IN = GLA_IN + RG_IN
OD_MIX = GLA_H * GLA_DV + D_RG

MOE_EXPERTS = 64
MOE_TOPK = 8
MOE_D_EXPERT = 256
MOE_D_SHARED = 256
MOE_SCALE = 2.5
MOE_BLOCK = 1024

F32 = jnp.float32
BF16 = jnp.bfloat16
HIGHEST = lax.Precision.HIGHEST

NTOK = SEQ + CTX_LEN
ROW_TILE = 256
N_ROW_TILES = NTOK // ROW_TILE
N_LAT_TILES = SEQ // ROW_TILE

VMEM_LIMIT = 48 * 1024 * 1024
VMEM_LIMIT_BIG = 56 * 1024 * 1024


def _cparams(sem, limit=VMEM_LIMIT):
    return pltpu.CompilerParams(dimension_semantics=sem, vmem_limit_bytes=limit)


def _pick_tile(n, pref):
    t = min(n, pref)
    while n % t:
        t //= 2
    return t


def _silu(x):
    return x / (1.0 + jnp.exp(-x))


def _softplus(x):
    return jnp.maximum(x, 0.0) + jnp.log(1.0 + jnp.exp(-jnp.abs(x)))


def _mm_bf16_body(a_ref, w_ref, o_ref):
    o_ref[...] = jnp.dot(a_ref[...].astype(BF16), w_ref[...].astype(BF16),
                         preferred_element_type=F32).astype(o_ref.dtype)


def _mm_f32_body(a_ref, w_ref, o_ref):
    o_ref[...] = jnp.dot(a_ref[...], w_ref[...], preferred_element_type=F32,
                         precision=HIGHEST).astype(o_ref.dtype)


def pmatmul(a, w, *, exact=False, out_dtype=F32, tm=512, tn=None):
    m, k = a.shape
    n = w.shape[1]
    tm = _pick_tile(m, tm)
    tn = n if tn is None else _pick_tile(n, tn)
    body = _mm_f32_body if exact else _mm_bf16_body
    return pl.pallas_call(
        body,
        name='mm',
        grid=(m // tm, n // tn),
        in_specs=[pl.BlockSpec((tm, k), lambda i, j: (i, 0)),
                  pl.BlockSpec((k, tn), lambda i, j: (0, j))],
        out_specs=pl.BlockSpec((tm, tn), lambda i, j: (i, j)),
        out_shape=jax.ShapeDtypeStruct((m, n), out_dtype),
        compiler_params=_cparams(("parallel", "parallel")),
    )(a, w)


def _mm_split_body(a_ref, w_ref, *o_refs, splits):
    a = a_ref[...]
    for o_ref, (start, width) in zip(o_refs, splits):
        o_ref[...] = jnp.dot(a, w_ref[:, start:start + width],
                             preferred_element_type=F32).astype(o_ref.dtype)


def mm_split(a, w, splits, dtypes, tm=512):
    m, k = a.shape
    n = w.shape[1]
    tm = _pick_tile(m, tm)
    return pl.pallas_call(
        functools.partial(_mm_split_body, splits=tuple(splits)),
        name='mm_split',
        grid=(m // tm,),
        in_specs=[pl.BlockSpec((tm, k), lambda i: (i, 0)),
                  pl.BlockSpec((k, n), lambda i: (0, 0))],
        out_specs=[pl.BlockSpec((tm, wd), lambda i: (i, 0)) for _, wd in splits],
        out_shape=[jax.ShapeDtypeStruct((m, wd), dt) for (_, wd), dt in zip(splits, dtypes)],
        compiler_params=_cparams(("parallel",)),
    )(a, w)


MIX_SAMPLES = 2


def _stack_samples(ref):
    return jnp.concatenate([ref[s] for s in range(MIX_SAMPLES)], axis=0)


def _resid_store(o_ref, x_ref, g_ref, acc):
    for s in range(MIX_SAMPLES):
        o_ref[s] = x_ref[s] + g_ref[s] * acc[s * ROW_TILE:(s + 1) * ROW_TILE]


def _mm_resid_body(*refs, n_pairs):
    a_refs = refs[:n_pairs]
    w_refs = refs[n_pairs:2 * n_pairs]
    x_ref, g_ref, o_ref = refs[2 * n_pairs:]
    acc = jnp.dot(_stack_samples(a_refs[0]), w_refs[0][...], preferred_element_type=F32)
    for a_ref, w_ref in zip(a_refs[1:], w_refs[1:]):
        acc = acc + jnp.dot(_stack_samples(a_ref), w_ref[...], preferred_element_type=F32)
    _resid_store(o_ref, x_ref, g_ref, acc)


def _gate_spec(gate_idx, d):
    return pl.BlockSpec((MIX_SAMPLES, None, None, 1, d), lambda i, j: (i, 1 - j // N_LAT_TILES, gate_idx, 0, 0))


def _mix_tile(width):
    return pl.BlockSpec((MIX_SAMPLES, ROW_TILE, width), lambda i, j: (i, j, 0))


def mm_resid(a_list, w_list, xs, mods, gate_idx, n_tiles):
    b, nt, d = xs.shape
    n_pairs = len(a_list)
    in_specs = [_mix_tile(a.shape[-1]) for a in a_list]
    in_specs += [pl.BlockSpec(w.shape, lambda i, j: (0, 0)) for w in w_list]
    in_specs += [_mix_tile(d), _gate_spec(gate_idx, d)]
    return pl.pallas_call(
        functools.partial(_mm_resid_body, n_pairs=n_pairs),
        name='mm_resid',
        grid=(b // MIX_SAMPLES, n_tiles),
        in_specs=in_specs,
        out_specs=_mix_tile(d),
        out_shape=jax.ShapeDtypeStruct(xs.shape, F32),
        input_output_aliases={2 * n_pairs: 0},
        compiler_params=_cparams(("parallel", "parallel")),
    )(*a_list, *w_list, xs, mods)


def _even_out_body(hy_ref, yf_ref, yb_ref, xs_ref, z_ref, d_ref, nw_ref, w1_ref, w2_ref, x_ref, g_ref, o_ref):
    gw = D_SSM // SSD_G
    merged = []
    for s in range(MIX_SAMPLES):
        y = yf_ref[s].astype(F32) + yb_ref[s].astype(F32) + d_ref[...] * xs_ref[s].astype(F32)
        g = y * _silu(z_ref[s].astype(F32))
        groups = []
        for k in range(SSD_G):
            gk = g[:, k * gw:(k + 1) * gw]
            ms = jnp.mean(gk * gk, axis=-1, keepdims=True)
            groups.append(gk * lax.rsqrt(ms + NORM_EPS) * nw_ref[:, k * gw:(k + 1) * gw])
        merged.append(jnp.concatenate(groups, axis=1).astype(BF16))
    acc = (jnp.dot(_stack_samples(hy_ref), w1_ref[...], preferred_element_type=F32)
           + jnp.dot(jnp.concatenate(merged, axis=0), w2_ref[...], preferred_element_type=F32))
    _resid_store(o_ref, x_ref, g_ref, acc)


def even_out_resid(y_hy, yf, yb, xbc, z, d_skip, norm_w, w_hy, w_ssd, xs, mods):
    b, nt, d = xs.shape
    d_chan = jnp.repeat(d_skip.astype(F32), SSD_P).reshape(1, D_SSM)
    vec = pl.BlockSpec((1, D_SSM), lambda i, j: (0, 0))
    wspec = pl.BlockSpec((D_SSM, d), lambda i, j: (0, 0))
    half = _mix_tile(D_SSM)
    return pl.pallas_call(
        _even_out_body,
        name='even_out',
        grid=(b // MIX_SAMPLES, N_ROW_TILES),
        in_specs=[half, half, half, half, half, vec, vec, wspec, wspec, _mix_tile(d), _gate_spec(2, d)],
        out_specs=_mix_tile(d),
        out_shape=jax.ShapeDtypeStruct(xs.shape, F32),
        input_output_aliases={9: 0},
        compiler_params=_cparams(("parallel", "parallel")),
    )(y_hy, yf, yb, xbc, z, d_chan, norm_w.reshape(1, D_SSM), w_hy, w_ssd, xs, mods)


def _mod_spec(idx, d, b0=0):
    return pl.BlockSpec((None, None, None, 1, d), lambda i, j: (i + b0, 1 - j // N_LAT_TILES, idx, 0, 0))


def adaln_table(c, c_ctx, w, b):
    cv = jax.nn.silu(jnp.concatenate([c, c_ctx[None, :]], axis=0))
    cv = jnp.pad(cv, ((0, 24 - cv.shape[0]), (0, 0)))
    m = pmatmul(cv, w, exact=True, tn=1536)[:BATCH + 1] + b
    per_sample = m[:BATCH]
    ctx_row = jnp.broadcast_to(m[BATCH][None, :], per_sample.shape)
    return jnp.stack([ctx_row, per_sample], axis=1).reshape(BATCH, 2, 6, 1, D_MODEL)


def _norm_mod(x, w, shift, scale):
    ms = jnp.mean(x * x, axis=-1, keepdims=True)
    return (x * lax.rsqrt(ms + NORM_EPS) * w) * (1.0 + scale) + shift


def _norm_mod_body(x_ref, w_ref, sh_ref, sc_ref, o_ref):
    o_ref[0] = _norm_mod(x_ref[0], w_ref[...], sh_ref[...], sc_ref[...]).astype(o_ref.dtype)


def norm_mod(xs, w, mods, shift_idx, scale_idx):
    b, nt, d = xs.shape
    return pl.pallas_call(
        _norm_mod_body,
        name='norm_mod',
        grid=(b, nt // ROW_TILE),
        in_specs=[pl.BlockSpec((1, ROW_TILE, d), lambda i, j: (i, j, 0)),
                  pl.BlockSpec((1, d), lambda i, j: (0, 0)),
                  _mod_spec(shift_idx, d), _mod_spec(scale_idx, d)],
        out_specs=pl.BlockSpec((1, ROW_TILE, d), lambda i, j: (i, j, 0)),
        out_shape=jax.ShapeDtypeStruct(xs.shape, BF16),
        compiler_params=_cparams(("parallel", "parallel")),
    )(xs, w.reshape(1, d), mods, mods)


def _norm_mm_split_body(x_ref, nw_ref, sh_ref, sc_ref, w_ref, *o_refs, splits):
    h = jnp.concatenate([_norm_mod(x_ref[s], nw_ref[...], sh_ref[s], sc_ref[s]).astype(BF16)
                         for s in range(MIX_SAMPLES)], axis=0)
    for o_ref, (start, width) in zip(o_refs, splits):
        y = jnp.dot(h, w_ref[:, start:start + width], preferred_element_type=F32).astype(o_ref.dtype)
        for s in range(MIX_SAMPLES):
            o_ref[s] = y[s * ROW_TILE:(s + 1) * ROW_TILE]


def norm_mm_split(xs, norm_w, mods, w, splits, dtypes):
    b, nt, d = xs.shape
    n = w.shape[1]
    return pl.pallas_call(
        functools.partial(_norm_mm_split_body, splits=tuple(splits)),
        name='norm_mm_split',
        grid=(b // MIX_SAMPLES, nt // ROW_TILE),
        in_specs=[_mix_tile(d), pl.BlockSpec((1, d), lambda i, j: (0, 0)), _gate_spec(0, d), _gate_spec(1, d),
                  pl.BlockSpec((d, n), lambda i, j: (0, 0))],
        out_specs=[_mix_tile(wd) for _, wd in splits],
        out_shape=[jax.ShapeDtypeStruct((b, nt, wd), dt) for (_, wd), dt in zip(splits, dtypes)],
        compiler_params=_cparams(("parallel", "parallel")),
    )(xs, norm_w.reshape(1, d), mods, mods, w)


def _final_norm_body(x_ref, w_ref, o_ref):
    x = x_ref[0]
    ms = jnp.mean(x * x, axis=-1, keepdims=True)
    o_ref[0] = x * lax.rsqrt(ms + NORM_EPS) * w_ref[...]


def final_norm(xs, w):
    b, _, d = xs.shape
    return pl.pallas_call(
        _final_norm_body,
        name='final_norm',
        grid=(b, N_LAT_TILES),
        in_specs=[pl.BlockSpec((1, ROW_TILE, d), lambda i, j: (i, j, 0)),
                  pl.BlockSpec((1, d), lambda i, j: (0, 0))],
        out_specs=pl.BlockSpec((1, ROW_TILE, d), lambda i, j: (i, j, 0)),
        out_shape=jax.ShapeDtypeStruct((b, SEQ, d), F32),
        compiler_params=_cparams(("parallel", "parallel")),
    )(xs, w.reshape(1, d))


def _route_body(x_ref, w_ref, sh_ref, sc_ref, rw_ref, rb_ref, h_ref, idx_ref, wsel_ref, rank_ref, cnt_ref,
                *, group_size):
    first = (pl.program_id(0) % group_size == 0) & (pl.program_id(1) == 0)

    @pl.when(first)
    def _():
        cnt_ref[...] = jnp.zeros_like(cnt_ref)

    h = _norm_mod(x_ref[0], w_ref[...], sh_ref[...], sc_ref[...])
    h_ref[0] = h.astype(h_ref.dtype)
    logits = jnp.dot(h, rw_ref[...], preferred_element_type=F32, precision=HIGHEST)
    scores = 1.0 / (1.0 + jnp.exp(-logits))
    tm, ne = scores.shape
    lane = lax.broadcasted_iota(jnp.int32, (tm, ne), 1).astype(F32)
    slot = lax.broadcasted_iota(jnp.int32, (tm, MOE_TOPK), 1)
    sel = scores + rb_ref[...]
    picked = jnp.zeros((tm, ne), F32)
    hits = []
    idx_out = jnp.zeros((tm, MOE_TOPK), F32)
    w_out = jnp.zeros((tm, MOE_TOPK), F32)
    for k in range(MOE_TOPK):
        m = jnp.max(sel, axis=-1, keepdims=True)
        ik = jnp.min(jnp.where(sel == m, lane, float(ne)), axis=-1, keepdims=True)
        hit = lane == ik
        wk = jnp.sum(jnp.where(hit, scores, 0.0), axis=-1, keepdims=True)
        sel = jnp.where(hit, -jnp.inf, sel)
        picked = picked + hit.astype(F32)
        hits.append(hit)
        idx_out = jnp.where(slot == k, ik, idx_out)
        w_out = jnp.where(slot == k, wk, w_out)
    wsum = jnp.sum(w_out, axis=-1, keepdims=True)
    wsel_ref[0] = w_out / wsum * MOE_SCALE
    idx_ref[0] = idx_out.astype(jnp.int32)
    r_i = lax.broadcasted_iota(jnp.int32, (tm, tm), 0)
    c_i = lax.broadcasted_iota(jnp.int32, (tm, tm), 1)
    strict_lower = (c_i < r_i).astype(BF16)
    before = jnp.dot(strict_lower, picked.astype(BF16), preferred_element_type=F32) + cnt_ref[...]
    rank_out = jnp.zeros((tm, MOE_TOPK), F32)
    for k in range(MOE_TOPK):
        rk = jnp.sum(jnp.where(hits[k], before, 0.0), axis=-1, keepdims=True)
        rank_out = jnp.where(slot == k, rk, rank_out)
    rank_ref[0] = rank_out.astype(jnp.int32)
    cnt_ref[...] = cnt_ref[...] + jnp.sum(picked, axis=0, keepdims=True)


def route(xs, w, mods, router_w, router_b, n_tiles, group_size):
    b, _, d = xs.shape
    rows = n_tiles * ROW_TILE
    small = lambda dt: jax.ShapeDtypeStruct((b, rows, MOE_TOPK), dt)
    small_spec = pl.BlockSpec((1, ROW_TILE, MOE_TOPK), lambda i, j: (i, j, 0))
    return pl.pallas_call(
        functools.partial(_route_body, group_size=group_size),
        name='route',
        grid=(b, n_tiles),
        in_specs=[pl.BlockSpec((1, ROW_TILE, d), lambda i, j: (i, j, 0)),
                  pl.BlockSpec((1, d), lambda i, j: (0, 0)),
                  _mod_spec(3, d), _mod_spec(4, d),
                  pl.BlockSpec((d, MOE_EXPERTS), lambda i, j: (0, 0)),
                  pl.BlockSpec((1, MOE_EXPERTS), lambda i, j: (0, 0))],
        out_specs=[pl.BlockSpec((1, ROW_TILE, d), lambda i, j: (i, j, 0)),
                   small_spec, small_spec, small_spec,
                   pl.BlockSpec((None, 1, MOE_EXPERTS), lambda i, j: (i // group_size, 0, 0))],
        out_shape=[jax.ShapeDtypeStruct((b, rows, d), BF16), small(jnp.int32), small(F32), small(jnp.int32),
                   jax.ShapeDtypeStruct((b // group_size, 1, MOE_EXPERTS), F32)],
        compiler_params=_cparams(("arbitrary", "arbitrary")),
    )(xs, w.reshape(1, d), mods, mods, router_w, router_b.reshape(1, MOE_EXPERTS))


def _route_t_body(x_ref, w_ref, sh_ref, sc_ref, rwh_ref, rwl_ref, rb_ref, h_ref, idx_ref, wsel_ref, rank_ref, cnt_ref,
                  *, group_size):
    first = (pl.program_id(0) % group_size == 0) & (pl.program_id(1) == 0)

    @pl.when(first)
    def _():
        cnt_ref[...] = jnp.zeros_like(cnt_ref)

    h = _norm_mod(x_ref[0], w_ref[...], sh_ref[...], sc_ref[...])
    h_hi = h.astype(BF16)
    h_ref[0] = h_hi
    h_lo = (h - h_hi.astype(F32)).astype(BF16)
    nt = (((1,), (1,)), ((), ()))
    logits = (lax.dot_general(rwh_ref[...], h_hi, nt, preferred_element_type=F32)
              + lax.dot_general(rwh_ref[...], h_lo, nt, preferred_element_type=F32)
              + lax.dot_general(rwl_ref[...], h_hi, nt, preferred_element_type=F32))
    scores = 1.0 / (1.0 + jnp.exp(-logits))
    ne, tm = scores.shape
    expert = lax.broadcasted_iota(jnp.int32, (ne, tm), 0).astype(F32)
    slot = lax.broadcasted_iota(jnp.int32, (MOE_TOPK, tm), 0)
    sel = scores + rb_ref[...]
    picked = jnp.zeros((ne, tm), F32)
    hits = []
    idx_out = jnp.zeros((MOE_TOPK, tm), F32)
    w_out = jnp.zeros((MOE_TOPK, tm), F32)
    for k in range(MOE_TOPK):
        m = jnp.max(sel, axis=0, keepdims=True)
        ik = jnp.min(jnp.where(sel == m, expert, float(ne)), axis=0, keepdims=True)
        hit = expert == ik
        wk = jnp.sum(jnp.where(hit, scores, 0.0), axis=0, keepdims=True)
        sel = jnp.where(hit, -jnp.inf, sel)
        picked = picked + hit.astype(F32)
        hits.append(hit)
        idx_out = jnp.where(slot == k, ik, idx_out)
        w_out = jnp.where(slot == k, wk, w_out)
    wsum = jnp.sum(w_out, axis=0, keepdims=True)
    wsel_ref[0] = w_out / wsum * MOE_SCALE
    idx_ref[0] = idx_out.astype(jnp.int32)
    r_i = lax.broadcasted_iota(jnp.int32, (tm, tm), 0)
    c_i = lax.broadcasted_iota(jnp.int32, (tm, tm), 1)
    earlier = (r_i < c_i).astype(BF16)
    before = jnp.dot(picked.astype(BF16), earlier, preferred_element_type=F32) + cnt_ref[...]
    rank_out = jnp.zeros((MOE_TOPK, tm), F32)
    for k in range(MOE_TOPK):
        rk = jnp.sum(jnp.where(hits[k], before, 0.0), axis=0, keepdims=True)
        rank_out = jnp.where(slot == k, rk, rank_out)
    rank_ref[0] = rank_out.astype(jnp.int32)
    cnt_ref[...] = cnt_ref[...] + jnp.sum(picked, axis=1, keepdims=True)


def route_t(xs, w, mods, router_w, router_b, n_tiles, group_size):
    b, _, d = xs.shape
    rows = n_tiles * ROW_TILE
    rwt = router_w.T.astype(F32)
    rwt_hi, rwt_lo = _split_bf16(rwt)
    small = lambda dt: jax.ShapeDtypeStruct((b, MOE_TOPK, rows), dt)
    small_spec = pl.BlockSpec((1, MOE_TOPK, ROW_TILE), lambda i, j: (i, 0, j))
    return pl.pallas_call(
        functools.partial(_route_t_body, group_size=group_size),
        name='route',
        grid=(b, n_tiles),
        in_specs=[pl.BlockSpec((1, ROW_TILE, d), lambda i, j: (i, j, 0)),
                  pl.BlockSpec((1, d), lambda i, j: (0, 0)),
                  _mod_spec(3, d), _mod_spec(4, d),
                  pl.BlockSpec((MOE_EXPERTS, d), lambda i, j: (0, 0)),
                  pl.BlockSpec((MOE_EXPERTS, d), lambda i, j: (0, 0)),
                  pl.BlockSpec((MOE_EXPERTS, 1), lambda i, j: (0, 0))],
        out_specs=[pl.BlockSpec((1, ROW_TILE, d), lambda i, j: (i, j, 0)),
                   small_spec, small_spec, small_spec,
                   pl.BlockSpec((None, MOE_EXPERTS, 1), lambda i, j: (i // group_size, 0, 0))],
        out_shape=[jax.ShapeDtypeStruct((b, rows, d), BF16), small(jnp.int32), small(F32), small(jnp.int32),
                   jax.ShapeDtypeStruct((b // group_size, MOE_EXPERTS, 1), F32)],
        compiler_params=_cparams(("arbitrary", "arbitrary")),
    )(xs, w.reshape(1, d), mods, mods, rwt_hi, rwt_lo, router_b.astype(F32).reshape(MOE_EXPERTS, 1))


def _swiglu(x, wg, wu, wd):
    g = jnp.dot(x, wg, preferred_element_type=F32)
    u = jnp.dot(x, wu, preferred_element_type=F32)
    h = (_silu(g) * u).astype(BF16)
    return jnp.dot(h, wd, preferred_element_type=F32)


def _expert_body(be_ref, nu_ref, xa_ref, xb_ref, wg_ref, wu_ref, wd_ref, o_ref, wg_s, wu_s, wd_s):
    i = pl.program_id(0)
    used = i < nu_ref[0]

    @pl.when(used & ((i == 0) | (be_ref[i] != be_ref[jnp.maximum(i - 1, 0)])))
    def _():
        wg_s[...] = wg_ref[0].astype(BF16)
        wu_s[...] = wu_ref[0].astype(BF16)
        wd_s[...] = wd_ref[0].astype(BF16)

    @pl.when(used)
    def _():
        x = jnp.concatenate([xa_ref[...], xb_ref[...]], axis=1)
        o_ref[...] = _swiglu(x, wg_s[...], wu_s[...], wd_s[...]).astype(o_ref.dtype)

    @pl.when(jnp.logical_not(used))
    def _():
        o_ref[...] = jnp.zeros_like(o_ref)


def moe_experts(x_rows, block_e, n_used, wg, wu, wd, layer):
    rows, d = x_rows.shape
    n_blocks = rows // MOE_BLOCK
    f = wg.shape[-1]
    grid_spec = pltpu.PrefetchScalarGridSpec(
        num_scalar_prefetch=2,
        grid=(n_blocks,),
        in_specs=[
            pl.BlockSpec((MOE_BLOCK, d // 2), lambda i, be, nu: (i, 0)),
            pl.BlockSpec((MOE_BLOCK, d // 2), lambda i, be, nu: (i, 1)),
            pl.BlockSpec((None, 1, d, f), lambda i, be, nu: (layer, be[i], 0, 0)),
            pl.BlockSpec((None, 1, d, f), lambda i, be, nu: (layer, be[i], 0, 0)),
            pl.BlockSpec((None, 1, f, d), lambda i, be, nu: (layer, be[i], 0, 0)),
        ],
        out_specs=pl.BlockSpec((MOE_BLOCK, d), lambda i, be, nu: (i, 0)),
        scratch_shapes=[pltpu.VMEM((d, f), BF16), pltpu.VMEM((d, f), BF16), pltpu.VMEM((f, d), BF16)],
    )
    return pl.pallas_call(
        _expert_body,
        name='experts',
        grid_spec=grid_spec,
        out_shape=jax.ShapeDtypeStruct((rows, d), BF16),
        compiler_params=_cparams(("arbitrary",)),
    )(block_e, n_used, x_rows, x_rows, wg, wu, wd)


COMBINE_SAMPLES = 2


def _shared_resid_body(h_ref, wg_ref, wu_ref, wd_ref, pk_ref, ws_ref, x_ref, g_ref, o_ref):
    for s in range(COMBINE_SAMPLES):
        y = _swiglu(h_ref[s], wg_ref[...], wu_ref[...], wd_ref[...])
        ws = ws_ref[s]
        for k in range(MOE_TOPK):
            y = y + ws[:, k:k + 1] * pk_ref[k, s].astype(F32)
        o_ref[s] = x_ref[s] + g_ref[s] * y


def shared_resid(h, picked, wsel, xs, mods, wg, wu, wd, n_tiles, b0):
    b, _, d = h.shape
    f = wg.shape[-1]
    ns = COMBINE_SAMPLES
    s0 = b0 // ns
    tile = pl.BlockSpec((ns, ROW_TILE, d), lambda i, j: (i, j, 0))
    xs_tile = pl.BlockSpec((ns, ROW_TILE, d), lambda i, j: (i + s0, j, 0))
    gate = pl.BlockSpec((ns, None, None, 1, d), lambda i, j: (i + s0, 1 - j // N_LAT_TILES, 5, 0, 0))
    return pl.pallas_call(
        _shared_resid_body,
        name='shared_resid',
        grid=(b // ns, n_tiles),
        in_specs=[tile,
                  pl.BlockSpec((d, f), lambda i, j: (0, 0)),
                  pl.BlockSpec((d, f), lambda i, j: (0, 0)),
                  pl.BlockSpec((f, d), lambda i, j: (0, 0)),
                  pl.BlockSpec((MOE_TOPK, ns, ROW_TILE, d), lambda i, j: (0, i, j, 0)),
                  pl.BlockSpec((ns, ROW_TILE, MOE_TOPK), lambda i, j: (i, j, 0)),
                  xs_tile, gate],
        out_specs=xs_tile,
        out_shape=jax.ShapeDtypeStruct(xs.shape, F32),
        input_output_aliases={6: 0},
        compiler_params=_cparams(("parallel", "parallel")),
    )(h, wg, wu, wd, picked, wsel, xs, mods)


MOE_GROUPS = 2


def moe_layer(xs, norm_w, mods, router_w, router_b, w_gate, w_up, w_down, sh_gate, sh_up, sh_down, n_tiles, layer):
    bsz, _, d = xs.shape
    shared_w = (sh_gate.astype(BF16), sh_up.astype(BF16), sh_down.astype(BF16))
    b = bsz // MOE_GROUPS
    h, idx, wsel, rank, counts = route_t(xs, norm_w, mods, router_w, router_b, n_tiles, b)
    h_flat = h.reshape(-1, d)
    wsel = jnp.swapaxes(wsel, 1, 2)
    for g in range(MOE_GROUPS):
        sl = slice(g * b, (g + 1) * b)
        xs = _moe_group(xs, mods, h_flat, h[sl], idx[sl], wsel[sl], rank[sl], counts[g, :, 0], w_gate, w_up, w_down,
                        shared_w, n_tiles, layer, g * b)
    return xs


def _moe_group(xs, mods, h_flat, h, idx, wsel, rank, counts, w_gate, w_up, w_down, shared_w, n_tiles, layer, b0):
    b, rows_per_sample, d = h.shape
    n = b * rows_per_sample
    counts = counts.astype(jnp.int32)
    padded = (counts + MOE_BLOCK - 1) // MOE_BLOCK * MOE_BLOCK
    ends = jnp.cumsum(padded)
    starts = ends - padded
    nk = n * MOE_TOPK
    n_blocks = -(-nk // MOE_BLOCK) + MOE_EXPERTS
    rows = n_blocks * MOE_BLOCK
    n_pad = rows - nk
    e_iota = jnp.arange(MOE_EXPERTS, dtype=jnp.int32)
    dest = jnp.sum(jnp.where(idx[..., None] == e_iota, starts, 0), axis=-1) + rank
    blk_start = jnp.arange(n_blocks, dtype=jnp.int32) * MOE_BLOCK
    block_e = jnp.minimum(jnp.sum(ends[None, :] <= blk_start[:, None], axis=1), MOE_EXPERTS - 1).astype(jnp.int32)
    n_used = (ends[-1:] // MOE_BLOCK).astype(jnp.int32)
    pad = padded - counts
    cum_pad = jnp.cumsum(pad)
    m = jnp.arange(n_pad, dtype=jnp.int32)
    e_m = jnp.sum(cum_pad[None, :] <= m[:, None], axis=1)
    base = jnp.sum(jnp.where(jnp.minimum(e_m, MOE_EXPERTS - 1)[:, None] == e_iota,
                             starts + counts - (cum_pad - pad), 0), axis=1)
    pad_row = jnp.where(e_m < MOE_EXPERTS, base + m, ends[-1] + m - cum_pad[-1])
    tok0 = b0 * rows_per_sample
    tok = (tok0 + jnp.arange(b, dtype=jnp.int32)[:, None, None] * rows_per_sample
           + jnp.arange(rows_per_sample, dtype=jnp.int32)[None, None, :])
    tok = jnp.broadcast_to(tok, dest.shape).reshape(-1)
    _, row_tok = lax.sort((jnp.concatenate([dest.reshape(-1), pad_row]).astype(jnp.int32),
                           jnp.concatenate([tok, tok0 + m % n])), num_keys=1)
    x_rows = h_flat[row_tok]
    y_rows = moe_experts(x_rows, block_e, n_used, w_gate, w_up, w_down, layer)
    picked = y_rows[jnp.swapaxes(dest, 0, 1)]
    return shared_resid(h, picked, wsel, xs, mods, *shared_w, n_tiles, b0)


def _dwconv_body(x_ref, w_ref, b_ref, o_ref, *, width, act):
    chunk = ROW_TILE
    n_chunks = NTOK // chunk
    first_of_seq = (0, N_LAT_TILES)
    last_of_seq = (N_LAT_TILES - 1, n_chunks - 1)
    tc = x_ref.shape[-1]
    halo = 16
    row = lax.broadcasted_iota(jnp.int32, (chunk, tc), 0)
    zero_row = jnp.zeros((1, tc), F32)
    for c in range(n_chunks):
        r0 = c * chunk
        cur = x_ref[0, r0:r0 + chunk, :].astype(F32)
        if c in first_of_seq:
            prev_last = zero_row
        else:
            prev_last = x_ref[0, r0 - halo:r0, :].astype(F32)[halo - 1:halo, :]
        if c in last_of_seq:
            next0 = next1 = zero_row
        else:
            nxt = x_ref[0, r0 + chunk:r0 + chunk + halo, :].astype(F32)
            next0, next1 = nxt[0:1, :], nxt[1:2, :]
        xm1 = jnp.where(row == 0, prev_last, pltpu.roll(cur, 1, 0))
        xp1 = jnp.where(row == chunk - 1, next0, pltpu.roll(cur, chunk - 1, 0))
        y = w_ref[0:1, :] * xm1 + w_ref[1:2, :] * cur + w_ref[2:3, :] * xp1 + b_ref[...]
        if width == 4:
            xp2 = jnp.where(row == chunk - 2, next0,
                            jnp.where(row == chunk - 1, next1, pltpu.roll(cur, chunk - 2, 0)))
            y = y + w_ref[3:4, :] * xp2
        if act:
            y = _silu(y)
        o_ref[0, c * chunk:(c + 1) * chunk, :] = y.astype(o_ref.dtype)


def dwconv_stream(x, w, b, act, tc=256):
    bsz, nt, c = x.shape
    width = w.shape[0]
    return pl.pallas_call(
        functools.partial(_dwconv_body, width=width, act=act),
        name='dwconv',
        grid=(bsz, c // tc),
        in_specs=[pl.BlockSpec((1, nt, tc), lambda i, j: (i, 0, j)),
                  pl.BlockSpec((width, tc), lambda i, j: (0, j)),
                  pl.BlockSpec((1, tc), lambda i, j: (0, j))],
        out_specs=pl.BlockSpec((1, nt, tc), lambda i, j: (i, 0, j)),
        out_shape=jax.ShapeDtypeStruct(x.shape, BF16),
        compiler_params=_cparams(("parallel", "parallel")),
    )(x, w, b.reshape(1, c))


HY_FB = 512
DFT_SPLIT = 64


def dft_matrices(n):
    t = jnp.arange(n, dtype=jnp.int32)[None, :]
    ka = jnp.arange(DFT_SPLIT, dtype=jnp.int32)[:, None]
    kb = jnp.arange(n // DFT_SPLIT, dtype=jnp.int32)[:, None] * DFT_SPLIT
    ang_a = (2.0 * math.pi / (2 * n)) * ((ka * t) % (2 * n)).astype(F32)
    ang_b = (2.0 * math.pi / (2 * n)) * ((kb * t) % (2 * n)).astype(F32)
    ca, sa = jnp.cos(ang_a)[None], jnp.sin(ang_a)[None]
    cb, sb = jnp.cos(ang_b)[:, None], jnp.sin(ang_b)[:, None]
    cos_kt = (ca * cb - sa * sb).reshape(n, n)
    sin_kt = (sa * cb + ca * sb).reshape(n, n)
    idx = jnp.arange(n, dtype=jnp.int32)
    nyq = jnp.where(idx % 2 == 0, 1.0, -1.0).astype(F32)
    fwd = jnp.concatenate([cos_kt, jnp.where(idx[:, None] == 0, nyq[None, :], -sin_kt)], axis=0)
    scale = jnp.where(idx == 0, 0.5, 1.0)[None, :] / n
    inv = jnp.concatenate([cos_kt * scale, jnp.where(idx[None, :] == 0, nyq[:, None], -sin_kt) * scale], axis=1)
    return fwd.astype(BF16), inv.astype(BF16)


def hyena_filter_taps(n, fw0, fb0, fw1, fb1, fw2, fb2, fw3, freq):
    pos = jnp.arange(n, dtype=F32)
    t = pos / max(n - 1, 1)
    bands = jnp.linspace(1e-4, HY_BANDS - 1, HY_BANDS, dtype=F32)
    ang = (2.0 * math.pi / n) * pos[:, None] * bands[None, :]
    feats = jnp.concatenate([t[:, None], jnp.cos(ang), -jnp.sin(ang)], axis=-1)
    h = jnp.sin(freq * (jnp.dot(feats, fw0, precision=HIGHEST) + fb0))
    h = jnp.sin(freq * (jnp.dot(h, fw1, precision=HIGHEST) + fb1))
    h = jnp.sin(freq * (jnp.dot(h, fw2, precision=HIGHEST) + fb2))
    h = pmatmul(h, fw3, exact=True).reshape(n, 2, HY_ORDER, D_HY)
    deltas = jnp.abs(jnp.linspace(math.log(HY_DECAY_PCT_LO) / HY_DECAY_TARGET,
                                  math.log(HY_DECAY_PCT_HI) / HY_DECAY_TARGET, D_HY, dtype=F32))
    h = h * jnp.exp(-t[:, None] * deltas)[:, None, None, :]
    h0 = h[:, 0]
    h1 = h[:, 1].at[0].set(0.0)
    norm = jnp.sum(jnp.abs(h0), axis=0, keepdims=True) + jnp.sum(jnp.abs(h1), axis=0, keepdims=True)
    h0 = (h0 / norm).reshape(n, HY_ORDER * D_HY)
    h1 = (h1 / norm).reshape(n, HY_ORDER * D_HY)
    return h0 + h1, h0 - h1


def _split_bf16(a):
    hi = a.astype(BF16)
    return hi, (a - hi.astype(F32)).astype(BF16)


def hyena_spectrum(fwd, hsum, hdiff, fb):
    n = hsum.shape[0]
    a = pmatmul(fwd, hsum.astype(BF16))
    bm = pmatmul(fwd, hdiff.astype(BF16))
    sr = a[:n]
    si = bm[n:]
    nyq = a[n]
    first = (jnp.arange(n) == 0)[:, None]
    p = sr
    q = jnp.where(first, 0.0, si)
    s = jnp.where(first, nyq[None, :], sr)
    spec = jnp.stack([p, q, s], axis=0).reshape(3, n // fb, fb, HY_ORDER, D_HY)
    return spec.transpose(3, 1, 0, 2, 4)


def _hyena_body(u_ref, fre_ref, fim_ref, gre_ref, gim_ref, sp_ref, bias_ref, prev_ref, o_ref,
                vin, acc, *, nf):
    del prev_ref
    o = pl.program_id(1)
    f = pl.program_id(2)
    c = D_HY

    @pl.when((o == 0) & (f == 0))
    def _():
        vin[...] = u_ref[0, :, 0:c]

    @pl.when(f == 0)
    def _():
        acc[...] = jnp.zeros_like(acc)

    v = vin[...]
    vr = jnp.dot(fre_ref[...], v, preferred_element_type=F32)
    vi = jnp.dot(fim_ref[...], v, preferred_element_type=F32)
    p, q, s = sp_ref[0], sp_ref[1], sp_ref[2]
    zr = (vr * p - vi * q).astype(BF16)
    zi = (vr * q + vi * s).astype(BF16)
    acc[...] += (jnp.dot(gre_ref[...], zr, preferred_element_type=F32)
                 + jnp.dot(gim_ref[...], zi, preferred_element_type=F32))

    @pl.when((o == 0) & (f == nf - 1))
    def _():
        z = u_ref[0, :, c:2 * c].astype(F32) * (acc[...] + bias_ref[0:1, :] * vin[...].astype(F32))
        vin[...] = z.astype(BF16)

    @pl.when((o == 1) & (f == nf - 1))
    def _():
        y = u_ref[0, :, 2 * c:3 * c].astype(F32) * (acc[...] + bias_ref[1:2, :] * vin[...].astype(F32))
        o_ref[0] = y.astype(o_ref.dtype)


def hyena_long_conv(u, fwd_bf16, inv_bf16, spec, bias, n, row_block, prev_out):
    bsz = u.shape[0]
    fb = spec.shape[3]
    nf = n // fb
    out_shape = jax.ShapeDtypeStruct((bsz, NTOK, D_HY), BF16)
    if prev_out is None:
        prev_out = jnp.zeros(out_shape.shape, BF16)
    args = [u, fwd_bf16, fwd_bf16, inv_bf16, inv_bf16, spec, bias, prev_out]
    aliases = {7: 0}
    return pl.pallas_call(
        functools.partial(_hyena_body, nf=nf),
        name='hyena',
        grid=(bsz, HY_ORDER, nf),
        in_specs=[pl.BlockSpec((1, n, 3 * D_HY), lambda b, o, f: (b, row_block, 0)),
                  pl.BlockSpec((fb, n), lambda b, o, f: (f, 0)),
                  pl.BlockSpec((fb, n), lambda b, o, f: (nf + f, 0)),
                  pl.BlockSpec((n, fb), lambda b, o, f: (0, f)),
                  pl.BlockSpec((n, fb), lambda b, o, f: (0, nf + f)),
                  pl.BlockSpec((None, None, 3, fb, D_HY), lambda b, o, f: (o, f, 0, 0, 0)),
                  pl.BlockSpec((HY_ORDER, D_HY), lambda b, o, f: (0, 0)),
                  pl.BlockSpec(memory_space=pl.ANY)],
        out_specs=pl.BlockSpec((1, n, D_HY), lambda b, o, f: (b, row_block, 0)),
        out_shape=out_shape,
        scratch_shapes=[pltpu.VMEM((n, D_HY), BF16), pltpu.VMEM((n, D_HY), F32)],
        input_output_aliases=aliases,
        compiler_params=_cparams(("parallel", "arbitrary", "arbitrary"), VMEM_LIMIT_BIG),
    )(*args)


def hyena_mixer_stream(p_hy, conv_w, conv_b, filt, bias):
    u = dwconv_stream(p_hy, conv_w, conv_b, act=False)
    out = None
    for n, row_block in ((SEQ, 0), (CTX_LEN, SEQ // CTX_LEN)):
        fb = min(HY_FB, n)
        fwd, inv = dft_matrices(n)
        hsum, hdiff = hyena_filter_taps(n, *filt)
        spec = hyena_spectrum(fwd, hsum, hdiff, fb)
        out = hyena_long_conv(u, fwd, inv, spec, bias, n, row_block, out)
    return out


SCAN_SAMPLES = 2


def _tri(n, kind):
    r = lax.broadcasted_iota(jnp.int32, (n, n), 0)
    c = lax.broadcasted_iota(jnp.int32, (n, n), 1)
    return (c <= r) if kind == 'lower' else (c >= r)


def _ssd_dir(xbc_ref, dt_ref, dtt_ref, bias_r, bias_c, a_r, a_c, st_ref, y_ref, *, s, d, reverse):
    q = SSD_CHUNK
    nh = SSD_H
    gw = SSD_HPG * SSD_P
    lower = _tri(q, 'lower')
    upper = _tri(q, 'upper')
    lower_f = lower.astype(F32)
    upper_f = upper.astype(F32)
    dt_col = _softplus(dt_ref[s] + bias_r)
    dt_row = _softplus(dtt_ref[s] + bias_c)
    da_col = dt_col * a_r
    da_row = dt_row * a_c
    if not reverse:
        acs_col = jnp.dot(lower_f, da_col, preferred_element_type=F32, precision=HIGHEST)
        acs_row = jnp.dot(da_row, upper_f, preferred_element_type=F32, precision=HIGHEST)
        mask = lower
        edge = q - 1
    else:
        acs_col = jnp.dot(upper_f, da_col, preferred_element_type=F32, precision=HIGHEST)
        acs_row = jnp.dot(da_row, lower_f, preferred_element_type=F32, precision=HIGHEST)
        mask = upper
        edge = 0
    h0 = d * nh
    hh = lax.broadcasted_iota(jnp.int32, (2 * nh, nh * SSD_P), 0)
    cc = lax.broadcasted_iota(jnp.int32, (2 * nh, nh * SSD_P), 1) // SSD_P
    expand = (hh == cc + h0).astype(F32)
    acs_c = jnp.dot(acs_col, expand, preferred_element_type=F32, precision=HIGHEST)
    dt_c = jnp.dot(dt_col, expand, preferred_element_type=F32, precision=HIGHEST)
    total_c = acs_c[edge:edge + 1, :]
    e_in_c = jnp.exp(acs_c)
    w_end_c = jnp.exp(total_c - acs_c) * dt_c
    dec_c = jnp.exp(total_c)
    xs = xbc_ref[s, :, 0:D_SSM]
    xs_f = xs.astype(F32)
    for g in range(SSD_G):
        bm = xbc_ref[s, :, D_SSM + g * SSD_N:D_SSM + (g + 1) * SSD_N]
        cm = xbc_ref[s, :, D_SSM + SSD_G * SSD_N + g * SSD_N:D_SSM + SSD_G * SSD_N + (g + 1) * SSD_N]
        cb = lax.dot_general(cm, bm, (((1,), (1,)), ((), ())), preferred_element_type=F32)
        lws = []
        for k in range(SSD_HPG):
            h = h0 + g * SSD_HPG + k
            seg = acs_col[:, h:h + 1] - acs_row[h:h + 1, :]
            decay = jnp.exp(jnp.where(mask, seg, -jnp.inf))
            lws.append((cb * decay * dt_row[h:h + 1, :]).astype(BF16))
        lw = jnp.concatenate(lws, axis=1)
        xg = xs[:, g * gw:(g + 1) * gw]
        rb = lax.broadcasted_iota(jnp.int32, (SSD_HPG * q, gw), 0) // q
        cbk = lax.broadcasted_iota(jnp.int32, (SSD_HPG * q, gw), 1) // SSD_P
        x_bd = jnp.where(rb == cbk, jnp.concatenate([xg] * SSD_HPG, axis=0), jnp.zeros((), BF16))
        y_in = jnp.dot(lw, x_bd, preferred_element_type=F32)
        st = st_ref[s, g]
        y_st = jnp.dot(cm, st.astype(BF16), preferred_element_type=F32) * e_in_c[:, g * gw:(g + 1) * gw]
        y_ref[s, :, g * gw:(g + 1) * gw] = (y_in + y_st).astype(y_ref.dtype)
        xw = (xs_f[:, g * gw:(g + 1) * gw] * w_end_c[:, g * gw:(g + 1) * gw]).astype(BF16)
        upd = lax.dot_general(bm, xw, (((0,), (0,)), ((), ())), preferred_element_type=F32)
        st_ref[s, g] = st * dec_c[:, g * gw:(g + 1) * gw] + upd


def _ssd_body(xf_ref, dtf_ref, dttf_ref, xb_ref, dtb_ref, dttb_ref, bias_r, bias_c, a_r, a_c,
              yf_ref, yb_ref, stf, stb):
    @pl.when(pl.program_id(1) == 0)
    def _():
        stf[...] = jnp.zeros_like(stf)
        stb[...] = jnp.zeros_like(stb)

    for s in range(SCAN_SAMPLES):
        _ssd_dir(xf_ref, dtf_ref, dttf_ref, bias_r[...], bias_c[...], a_r[...], a_c[...], stf, yf_ref,
                 s=s, d=0, reverse=False)
        _ssd_dir(xb_ref, dtb_ref, dttb_ref, bias_r[...], bias_c[...], a_r[...], a_c[...], stb, yb_ref,
                 s=s, d=1, reverse=True)


def ssd_scan(xbc, dt, dt_bias, a_log):
    bsz = xbc.shape[0]
    nc = NTOK // SSD_CHUNK
    nlat = SEQ // SSD_CHUNK
    dtt = jnp.swapaxes(dt, 1, 2)
    fwd_chunk = lambda s: (s + nlat) % nc
    bwd_chunk = lambda s: nc - 1 - s
    a = -jnp.exp(a_log.astype(F32)).reshape(1, 2 * SSD_H)
    bias = dt_bias.astype(F32).reshape(1, 2 * SSD_H)
    ns = SCAN_SAMPLES
    x_spec = lambda cm: pl.BlockSpec((ns, SSD_CHUNK, SSD_XBC), lambda b, s: (b, cm(s), 0))
    dt_spec = lambda cm: pl.BlockSpec((ns, SSD_CHUNK, 2 * SSD_H), lambda b, s: (b, cm(s), 0))
    dtt_spec = lambda cm: pl.BlockSpec((ns, 2 * SSD_H, SSD_CHUNK), lambda b, s: (b, 0, cm(s)))
    y_spec = lambda cm: pl.BlockSpec((ns, SSD_CHUNK, D_SSM), lambda b, s: (b, cm(s), 0))
    row = pl.BlockSpec((1, 2 * SSD_H), lambda b, s: (0, 0))
    col = pl.BlockSpec((2 * SSD_H, 1), lambda b, s: (0, 0))
    y_shape = jax.ShapeDtypeStruct((bsz, NTOK, D_SSM), BF16)
    gw = SSD_HPG * SSD_P
    return pl.pallas_call(
        _ssd_body,
        name='ssd_scan',
        grid=(bsz // ns, nc),
        in_specs=[x_spec(fwd_chunk), dt_spec(fwd_chunk), dtt_spec(fwd_chunk),
                  x_spec(bwd_chunk), dt_spec(bwd_chunk), dtt_spec(bwd_chunk),
                  row, col, row, col],
        out_specs=[y_spec(fwd_chunk), y_spec(bwd_chunk)],
        out_shape=[y_shape, y_shape],
        scratch_shapes=[pltpu.VMEM((ns, SSD_G, SSD_N, gw), F32), pltpu.VMEM((ns, SSD_G, SSD_N, gw), F32)],
        compiler_params=_cparams(("parallel", "arbitrary")),
    )(xbc, dt, dtt, xbc, dt, dtt, bias, bias.reshape(-1, 1), a, a.reshape(-1, 1))


def _ssd_merge_body(yf_ref, yb_ref, xbc_ref, z_ref, d_ref, nw_ref, o_ref):
    xs = xbc_ref[0, :, 0:D_SSM].astype(F32)
    z = z_ref[0].astype(F32)
    g = (yf_ref[0] + yb_ref[0] + d_ref[...] * xs) * _silu(z)
    gw = D_SSM // SSD_G
    for k in range(SSD_G):
        gk = g[:, k * gw:(k + 1) * gw]
        ms = jnp.mean(gk * gk, axis=-1, keepdims=True)
        o_ref[0, :, k * gw:(k + 1) * gw] = (gk * lax.rsqrt(ms + NORM_EPS)
                                            * nw_ref[:, k * gw:(k + 1) * gw]).astype(o_ref.dtype)


def ssd_merge(yf, yb, xbc, z, d_skip, norm_w):
    bsz = yf.shape[0]
    tile = lambda w: pl.BlockSpec((1, ROW_TILE, w), lambda i, j: (i, j, 0))
    vec = pl.BlockSpec((1, D_SSM), lambda i, j: (0, 0))
    d_chan = jnp.repeat(d_skip.astype(F32), SSD_P).reshape(1, D_SSM)
    return pl.pallas_call(
        _ssd_merge_body,
        name='ssd_merge',
        grid=(bsz, N_ROW_TILES),
        in_specs=[tile(D_SSM), tile(D_SSM), tile(SSD_XBC), tile(D_SSM), vec, vec],
        out_specs=tile(D_SSM),
        out_shape=jax.ShapeDtypeStruct((bsz, NTOK, D_SSM), BF16),
        compiler_params=_cparams(("parallel", "parallel")),
    )(yf, yb, xbc, z, d_chan, norm_w.reshape(1, D_SSM))


def even_layer_mixer(xs, mods, norm_w, w_in, w_out, hy_conv_w, hy_conv_b, hy_filt, hy_bias,
                     ssd_conv_w, ssd_conv_b, ssd_dt_bias, ssd_a_log, ssd_d, ssd_norm_w):
    splits = ((0, HY_IN), (HY_IN, D_SSM), (HY_IN + D_SSM, SSD_XBC), (HY_IN + D_SSM + SSD_XBC, 2 * SSD_H))
    p_hy, z, xbc_raw, dt = norm_mm_split(xs, norm_w, mods, w_in.astype(BF16), splits, (BF16, BF16, BF16, F32))
    y_hy = hyena_mixer_stream(p_hy, hy_conv_w, hy_conv_b, hy_filt, hy_bias)
    xbc = dwconv_stream(xbc_raw, ssd_conv_w, ssd_conv_b, act=True)
    yf, yb = ssd_scan(xbc, dt, ssd_dt_bias, ssd_a_log)
    wo = w_out.astype(BF16)
    return even_out_resid(y_hy, yf, yb, xbc, z, ssd_d, ssd_norm_w, wo[:D_HY], wo[D_HY:], xs, mods)


GLA_QK = GLA_H * GLA_DK
GLA_V = GLA_H * GLA_DV


def _gla_dir(qkv_ref, lr_ref, gw_ref, gb_ref, st_ref, o_ref, *, s, d, reverse):
    q = GLA_CHUNK
    tri = _tri(q, 'upper' if reverse else 'lower')
    edge = 0 if reverse else q - 1
    lr = lr_ref[s, :, d * GLA_RANK:(d + 1) * GLA_RANK]
    logit = jnp.dot(lr, gw_ref[d], preferred_element_type=F32, precision=HIGHEST) + gb_ref[d:d + 1, :]
    log_g = -_softplus(-logit) * (1.0 / GLA_GATE_NORM)
    gcum = jnp.dot(tri.astype(F32), log_g, preferred_element_type=F32, precision=HIGHEST)
    total = gcum[edge:edge + 1, :]
    qf = qkv_ref[s, :, 0:GLA_QK].astype(F32)
    kf = qkv_ref[s, :, GLA_QK:2 * GLA_QK].astype(F32)
    v = qkv_ref[s, :, 2 * GLA_QK:2 * GLA_QK + GLA_V]
    qg = (qf * (GLA_DK ** -0.5) * jnp.exp(gcum)).astype(BF16)
    kg = (kf * jnp.exp(-gcum)).astype(BF16)
    kw = (kf * jnp.exp(total - gcum)).astype(BF16)
    rb = lax.broadcasted_iota(jnp.int32, (GLA_H * q, GLA_QK), 0) // q
    cb = lax.broadcasted_iota(jnp.int32, (GLA_H * q, GLA_QK), 1) // GLA_DK
    k_bd = jnp.where(rb == cb, jnp.concatenate([kg] * GLA_H, axis=0), jnp.zeros((), BF16))
    att = lax.dot_general(qg, k_bd, (((1,), (1,)), ((), ())), preferred_element_type=F32)
    i_i = lax.broadcasted_iota(jnp.int32, (q, GLA_H * q), 0)
    j_i = lax.broadcasted_iota(jnp.int32, (q, GLA_H * q), 1) % q
    keep = (j_i >= i_i) if reverse else (j_i <= i_i)
    att = jnp.where(keep, att, 0.0).astype(BF16)
    rv = lax.broadcasted_iota(jnp.int32, (GLA_H * q, GLA_V), 0) // q
    cv = lax.broadcasted_iota(jnp.int32, (GLA_H * q, GLA_V), 1) // GLA_DV
    v_bd = jnp.where(rv == cv, jnp.concatenate([v] * GLA_H, axis=0), jnp.zeros((), BF16))
    st = st_ref[s]
    o_in = jnp.dot(att, v_bd, preferred_element_type=F32)
    o_st = lax.dot_general(qg, st.astype(BF16), (((1,), (1,)), ((), ())), preferred_element_type=F32)
    o_ref[s] = (o_in + o_st).astype(o_ref.dtype)
    upd = lax.dot_general(v, kw, (((0,), (0,)), ((), ())), preferred_element_type=F32)
    rs = lax.broadcasted_iota(jnp.int32, (GLA_V, GLA_QK), 0) // GLA_DV
    cs = lax.broadcasted_iota(jnp.int32, (GLA_V, GLA_QK), 1) // GLA_DK
    st_ref[s] = st * jnp.exp(total) + jnp.where(rs == cs, upd, 0.0)


def _gla_body(qf_ref, lf_ref, qb_ref, lb_ref, gw_ref, gb_ref, of_ref, ob_ref, stf, stb):
    @pl.when(pl.program_id(1) == 0)
    def _():
        stf[...] = jnp.zeros_like(stf)
        stb[...] = jnp.zeros_like(stb)

    for s in range(SCAN_SAMPLES):
        _gla_dir(qf_ref, lf_ref, gw_ref, gb_ref, stf, of_ref, s=s, d=0, reverse=False)
        _gla_dir(qb_ref, lb_ref, gw_ref, gb_ref, stb, ob_ref, s=s, d=1, reverse=True)


def gla_scan(qkv, lr, gate_w, gate_b):
    bsz = qkv.shape[0]
    nc = NTOK // GLA_CHUNK
    nlat = SEQ // GLA_CHUNK
    fwd_chunk = lambda s: (s + nlat) % nc
    bwd_chunk = lambda s: nc - 1 - s
    ns = SCAN_SAMPLES
    q_spec = lambda cm: pl.BlockSpec((ns, GLA_CHUNK, qkv.shape[-1]), lambda b, s: (b, cm(s), 0))
    l_spec = lambda cm: pl.BlockSpec((ns, GLA_CHUNK, 2 * GLA_RANK), lambda b, s: (b, cm(s), 0))
    o_spec = lambda cm: pl.BlockSpec((ns, GLA_CHUNK, GLA_V), lambda b, s: (b, cm(s), 0))
    o_shape = jax.ShapeDtypeStruct((bsz, NTOK, GLA_V), BF16)
    return pl.pallas_call(
        _gla_body,
        name='gla_scan',
        grid=(bsz // ns, nc),
        in_specs=[q_spec(fwd_chunk), l_spec(fwd_chunk), q_spec(bwd_chunk), l_spec(bwd_chunk),
                  pl.BlockSpec((2, GLA_RANK, GLA_QK), lambda b, s: (0, 0, 0)),
                  pl.BlockSpec((2, GLA_QK), lambda b, s: (0, 0))],
        out_specs=[o_spec(fwd_chunk), o_spec(bwd_chunk)],
        out_shape=[o_shape, o_shape],
        scratch_shapes=[pltpu.VMEM((ns, GLA_V, GLA_QK), F32), pltpu.VMEM((ns, GLA_V, GLA_QK), F32)],
        compiler_params=_cparams(("parallel", "arbitrary")),
    )(qkv, lr, qkv, lr, gate_w.astype(F32), gate_b.astype(F32))


def _gla_merge_body(of_ref, ob_ref, r_ref, nw_ref, o_ref):
    o = of_ref[0].astype(F32) + ob_ref[0].astype(F32)
    r = r_ref[0].astype(F32)
    for h in range(GLA_H):
        sl = slice(h * GLA_DV, (h + 1) * GLA_DV)
        oh = o[:, sl]
        ms = jnp.mean(oh * oh, axis=-1, keepdims=True)
        o_ref[0, :, sl] = (oh * lax.rsqrt(ms + NORM_EPS) * nw_ref[:, sl] * _silu(r[:, sl])).astype(o_ref.dtype)


def gla_merge_stream(of, ob, r, norm_w):
    bsz = of.shape[0]
    tile = pl.BlockSpec((1, ROW_TILE, GLA_V), lambda i, j: (i, j, 0))
    return pl.pallas_call(
        _gla_merge_body,
        name='gla_merge',
        grid=(bsz, N_LAT_TILES),
        in_specs=[tile, tile, tile, pl.BlockSpec((1, GLA_V), lambda i, j: (0, 0))],
        out_specs=tile,
        out_shape=jax.ShapeDtypeStruct((bsz, SEQ, GLA_V), BF16),
        compiler_params=_cparams(("parallel", "parallel")),
    )(of, ob, r, norm_w.reshape(1, GLA_V))


RG_TILE = 8


def _gelu_tanh(x):
    return 0.5 * x * (1.0 + jnp.tanh(math.sqrt(2.0 / math.pi) * (x + 0.044715 * x * x * x)))


def _rg_scan_block(a_s, x_s, h_s, base, carry, reverse):
    n_tiles = ROW_TILE // RG_TILE
    row = lax.broadcasted_iota(jnp.int32, (RG_TILE, D_RG), 0)

    def tile_step(i, h_prev):
        t = (n_tiles - 1 - i) if reverse else i
        r0 = pl.multiple_of(t * RG_TILE, RG_TILE)
        a = a_s[pl.ds(r0, RG_TILE), :]
        x = x_s[pl.ds(r0, RG_TILE), :]
        for s in (1, 2, 4):
            if reverse:
                ok = row < RG_TILE - s
                shift = RG_TILE - s
            else:
                ok = row >= s
                shift = s
            a_sh = jnp.where(ok, pltpu.roll(a, shift, 0), 1.0)
            x_sh = jnp.where(ok, pltpu.roll(x, shift, 0), 0.0)
            x = a * x_sh + x
            a = a * a_sh
        h = x + a * h_prev
        h_s[pl.ds(base + r0, RG_TILE), :] = h
        edge = 0 if reverse else RG_TILE - 1
        return jnp.broadcast_to(h[edge:edge + 1, :], (RG_TILE, D_RG))

    return lax.fori_loop(0, n_tiles, tile_step, carry)


def _rglru_body(u_ref, g_ref, w_ref, b_ref, c_ref, o_ref, hf_s, a_s, x_s, hb_s):
    n_blocks = NTOK // ROW_TILE
    fwd_order = list(range(N_LAT_TILES, n_blocks)) + list(range(N_LAT_TILES))
    bwd_order = list(range(n_blocks - 1, N_LAT_TILES - 1, -1)) + list(range(N_LAT_TILES - 1, -1, -1))

    def gates(blk, d):
        ub = u_ref[0, blk * ROW_TILE:(blk + 1) * ROW_TILE, :]
        z = jnp.dot(ub, w_ref[:, 2 * d * D_RG:2 * (d + 1) * D_RG], preferred_element_type=F32)
        z = z + b_ref[:, 2 * d * D_RG:2 * (d + 1) * D_RG]
        r = 1.0 / (1.0 + jnp.exp(-z[:, :D_RG]))
        i = 1.0 / (1.0 + jnp.exp(-z[:, D_RG:]))
        a = jnp.exp(c_ref[d:d + 1, :] * r)
        a_s[...] = a
        x_s[...] = jnp.sqrt(1.0 - a * a) * i * ub.astype(F32)

    carry = jnp.zeros((RG_TILE, D_RG), F32)
    for blk in fwd_order:
        gates(blk, 0)
        carry = _rg_scan_block(a_s, x_s, hf_s, blk * ROW_TILE, carry, reverse=False)
    carry = jnp.zeros((RG_TILE, D_RG), F32)
    for blk in bwd_order:
        gates(blk, 1)
        carry = _rg_scan_block(a_s, x_s, hb_s, 0, carry, reverse=True)
        rows = slice(blk * ROW_TILE, (blk + 1) * ROW_TILE)
        gate = g_ref[0, rows, :].astype(F32)
        o_ref[0, rows, :] = ((hf_s[rows, :] + hb_s[...]) * _gelu_tanh(gate)).astype(o_ref.dtype)


def rglru_stream(u, gate, w_a, b_a, w_x, b_x, lam):
    bsz = u.shape[0]
    eye = jnp.eye(RG_BLOCKS, dtype=F32)
    dense = lambda w: jnp.einsum('nio,nm->nimo', w, eye).reshape(D_RG, D_RG)
    w_cat = jnp.concatenate([dense(w_a[0]), dense(w_x[0]), dense(w_a[1]), dense(w_x[1])], axis=1).astype(BF16)
    b_cat = jnp.concatenate([b_a[0], b_x[0], b_a[1], b_x[1]]).astype(F32).reshape(1, 4 * D_RG)
    c = -RG_C * jax.nn.softplus(-lam.astype(F32))
    seq = pl.BlockSpec((1, NTOK, D_RG), lambda i: (i, 0, 0))
    return pl.pallas_call(
        _rglru_body,
        name='rglru',
        grid=(bsz,),
        in_specs=[seq, seq,
                  pl.BlockSpec((D_RG, 4 * D_RG), lambda i: (0, 0)),
                  pl.BlockSpec((1, 4 * D_RG), lambda i: (0, 0)),
                  pl.BlockSpec((2, D_RG), lambda i: (0, 0))],
        out_specs=seq,
        out_shape=jax.ShapeDtypeStruct((bsz, NTOK, D_RG), BF16),
        scratch_shapes=[pltpu.VMEM((NTOK, D_RG), F32), pltpu.VMEM((ROW_TILE, D_RG), F32),
                        pltpu.VMEM((ROW_TILE, D_RG), F32), pltpu.VMEM((ROW_TILE, D_RG), F32)],
        compiler_params=_cparams(("parallel",)),
    )(u, gate, w_cat, b_cat, c)


def odd_layer_mixer_pallas(xs, mods, norm_w, w_in, w_out, gla_args, rg_args):
    bsz = xs.shape[0]
    gate_w, gate_b, gla_norm_w = gla_args
    rg_conv_w, rg_conv_b, w_a, b_a, w_x, b_x, lam = rg_args
    h = norm_mod(xs, norm_w, mods, 0, 1)
    h = jnp.concatenate([to_col_major(h[:, :SEQ]), h[:, SEQ:]], axis=1)
    nqk, nv = GLA_QK, GLA_V
    r0 = 2 * nqk + nv + 2 * GLA_RANK
    w = jnp.concatenate([w_in[:, :2 * nqk + nv], w_in[:, r0:r0 + nv], w_in[:, GLA_IN:],
                         w_in[:, 2 * nqk + nv:r0]], axis=1).astype(BF16)
    qkv_w = 2 * nqk + nv
    splits = ((0, qkv_w), (qkv_w, nv), (qkv_w + nv, D_RG), (qkv_w + nv + D_RG, D_RG),
              (qkv_w + nv + 2 * D_RG, 2 * GLA_RANK))
    qkv, r, u_raw, gate, lr = mm_split(h.reshape(bsz * NTOK, D_MODEL), w, splits, (BF16, BF16, BF16, BF16, F32))
    to3 = lambda a: a.reshape(bsz, NTOK, a.shape[-1])
    of, ob = gla_scan(to3(qkv), to3(lr), gate_w, gate_b.reshape(2, GLA_QK))
    a_l = gla_merge_stream(of, ob, to3(r), gla_norm_w)
    u = dwconv_stream(to3(u_raw), rg_conv_w, rg_conv_b, act=False)
    r_l = rglru_stream(u, to3(gate), w_a, b_a, w_x, b_x, lam)[:, :SEQ]
    wo = w_out.astype(BF16)
    return mm_resid([from_col_major(a_l), from_col_major(r_l)], [wo[:GLA_V], wo[GLA_V:]], xs, mods, 2, N_LAT_TILES)


def rms_norm(x, w):
    xf = x.astype(F32)
    y = xf * lax.rsqrt(jnp.mean(jnp.square(xf), axis=-1, keepdims=True) + NORM_EPS)
    return y.astype(x.dtype) * w


def dwconv(x, w, b):
    y = lax.conv_general_dilated(x, w[:, None, :].astype(x.dtype), window_strides=(1,), padding='SAME',
                                 dimension_numbers=('NWC', 'WIO', 'NWC'), feature_group_count=x.shape[-1])
    return y + b.astype(x.dtype)


def maybe_flip(a, rev):
    return jnp.flip(a, axis=1) if rev else a


def to_col_major(x):
    b, n, d = x.shape
    rows = n // GRID_W
    return x.reshape(b, rows, GRID_W, d).transpose(0, 2, 1, 3).reshape(b, n, d)


def from_col_major(x):
    b, n, d = x.shape
    rows = n // GRID_W
    return x.reshape(b, GRID_W, rows, d).transpose(0, 2, 1, 3).reshape(b, n, d)


def gla_inputs(p, gate_w, gate_b):
    b, n, _ = p.shape
    nqk, nv = GLA_H * GLA_DK, GLA_H * GLA_DV
    q = p[..., :nqk].reshape(b, n, GLA_H, GLA_DK) * GLA_DK ** -0.5
    k = p[..., nqk:2 * nqk].reshape(b, n, GLA_H, GLA_DK)
    v = p[..., 2 * nqk:2 * nqk + nv].reshape(b, n, GLA_H, GLA_DV)
    lr = p[..., 2 * nqk + nv:2 * nqk + nv + 2 * GLA_RANK].reshape(b, n, 2, GLA_RANK)
    r = p[..., 2 * nqk + nv + 2 * GLA_RANK:]
    logit = jnp.einsum('bler,erk->blek', lr, gate_w) + gate_b
    log_g = (jax.nn.log_sigmoid(logit.astype(F32)) / GLA_GATE_NORM).reshape(b, n, 2, GLA_H, GLA_DK)
    return q, k, v, log_g, r


def gla_states(k, v, log_g, s0):
    b, n = k.shape[:2]
    nc = n // GLA_CHUNK
    kc = k.reshape(b, nc, GLA_CHUNK, GLA_H, GLA_DK)
    vc = v.reshape(b, nc, GLA_CHUNK, GLA_H, GLA_DV)
    gcum = jnp.cumsum(log_g.reshape(b, nc, GLA_CHUNK, GLA_H, GLA_DK), axis=2)
    states = jnp.einsum('bcqhd,bcqhv->bchdv', kc * jnp.exp(gcum[:, :, -1:] - gcum), vc)
    chunk_decay = jnp.exp(gcum[:, :, -1])

    def step(s, inp):
        st, dcy = inp
        return dcy[..., None] * s + st, s

    s_fin, s_prev = lax.scan(step, s0, (jnp.moveaxis(states, 1, 0), jnp.moveaxis(chunk_decay, 1, 0)))
    return jnp.moveaxis(s_prev, 0, 1), s_fin


def gla_output(q, k, v, log_g, s_prev):
    b, n = q.shape[:2]
    nc = n // GLA_CHUNK
    qc = q.reshape(b, nc, GLA_CHUNK, GLA_H, GLA_DK)
    kc = k.reshape(b, nc, GLA_CHUNK, GLA_H, GLA_DK)
    vc = v.reshape(b, nc, GLA_CHUNK, GLA_H, GLA_DV)
    gcum = jnp.cumsum(log_g.reshape(b, nc, GLA_CHUNK, GLA_H, GLA_DK), axis=2)
    qg = qc * jnp.exp(gcum)
    kg = kc * jnp.exp(-gcum)
    mask = jnp.tril(jnp.ones((GLA_CHUNK, GLA_CHUNK), bool))
    att = jnp.where(mask, jnp.einsum('bcihd,bcjhd->bchij', qg, kg), 0.0)
    o = jnp.einsum('bchij,bcjhv->bcihv', att, vc) + jnp.einsum('bcihd,bchdv->bcihv', qg, s_prev)
    return o.reshape(b, n, GLA_H, GLA_DV)


def gla_merge(os_, r, norm_w):
    b, n = r.shape[:2]
    o = rms_norm(os_[0] + os_[1], norm_w.reshape(GLA_H, GLA_DV))
    return o.reshape(b, n, GLA_H * GLA_DV) * jax.nn.silu(r)


def gla_mixer(p_c, p_l, gate_w, gate_b, norm_w):
    q_c, k_c, v_c, g_c, r_c = gla_inputs(p_c, gate_w, gate_b)
    q_l, k_l, v_l, g_l, r_l = gla_inputs(p_l, gate_w, gate_b)
    s0 = jnp.zeros((p_c.shape[0], GLA_H, GLA_DK, GLA_DV), F32)
    os_l = []
    for d, rev in enumerate((False, True)):
        f = functools.partial(maybe_flip, rev=rev)
        _, sf_c = gla_states(f(k_c), f(v_c), f(g_c[:, :, d]), s0)
        sp_l, _ = gla_states(f(k_l), f(v_l), f(g_l[:, :, d]), sf_c)
        os_l.append(f(gla_output(f(q_l), f(k_l), f(v_l), f(g_l[:, :, d]), sp_l)))
    return gla_merge(os_l, r_l, norm_w)


def rglru_inputs(p, conv_w, conv_b, w_a, b_a, w_x, b_x, lam):
    b, n, _ = p.shape
    u = dwconv(p[..., :D_RG], conv_w, conv_b)
    ub = u.reshape(b, n, RG_BLOCKS, RG_BW)
    r = jax.nn.sigmoid((jnp.einsum('blni,enio->bleno', ub, w_a).reshape(b, n, 2, D_RG) + b_a).astype(F32))
    i = jax.nn.sigmoid((jnp.einsum('blni,enio->bleno', ub, w_x).reshape(b, n, 2, D_RG) + b_x).astype(F32))
    log_a = -RG_C * jax.nn.softplus(-lam.astype(F32)) * r
    x_in = jnp.sqrt(-jnp.expm1(2.0 * log_a)) * i * u[:, :, None, :].astype(F32)
    return p[..., D_RG:], jnp.exp(log_a), x_in


def lru_scan(a, u, h0):
    u = u.at[:, 0].add(a[:, 0] * h0)

    def combine(lhs, rhs):
        a1, b1 = lhs
        a2, b2 = rhs
        return a1 * a2, a2 * b1 + b2

    return lax.associative_scan(combine, (a, u), axis=1)[1]


def rglru_mixer(p_c, p_l, conv_w, conv_b, w_a, b_a, w_x, b_x, lam):
    gb_c, a_c, u_c = rglru_inputs(p_c, conv_w, conv_b, w_a, b_a, w_x, b_x, lam)
    gb_l, a_l, u_l = rglru_inputs(p_l, conv_w, conv_b, w_a, b_a, w_x, b_x, lam)
    h0 = jnp.zeros((p_c.shape[0], D_RG), F32)
    hs_l = []
    for d, rev in enumerate((False, True)):
        f = functools.partial(maybe_flip, rev=rev)
        h_c = f(lru_scan(f(a_c[:, :, d]), f(u_c[:, :, d]), h0))
        h_end = h_c[:, 0] if rev else h_c[:, -1]
        hs_l.append(f(lru_scan(f(a_l[:, :, d]), f(u_l[:, :, d]), h_end)))
    return (hs_l[0] + hs_l[1]) * jax.nn.gelu(gb_l.astype(F32))


def odd_layer_mixer(xs, mods, norm_w, w_in, w_out, gla_args, rg_args):
    bsz = xs.shape[0]
    h = norm_mod(xs, norm_w, mods, 0, 1)
    h = jnp.concatenate([to_col_major(h[:, :SEQ]), h[:, SEQ:]], axis=1)
    p = pmatmul(h.reshape(bsz * NTOK, D_MODEL), w_in.astype(BF16)).reshape(bsz, NTOK, OD_IN)
    p_l, p_c = p[:, :SEQ], p[:, SEQ:]
    a_l = gla_mixer(p_c[..., :GLA_IN], p_l[..., :GLA_IN], *gla_args)
    r_l = rglru_mixer(p_c[..., GLA_IN:], p_l[..., GLA_IN:], *rg_args)
    mix = from_col_major(jnp.concatenate([a_l, r_l], axis=-1)).astype(BF16)
    return mm_resid([mix], [w_out.astype(BF16)], xs, mods, 2, N_LAT_TILES)


def kernel(x, c, ctx, c_ctx, ada_w, ada_b, norm1_w, norm2_w, ev_w_in, ev_w_out, hy_conv_w, hy_conv_b, hy_fw0, hy_fb0, hy_fw1, hy_fb1, hy_fw2, hy_fb2, hy_fw3, hy_freq, hy_bias, ssd_conv_w, ssd_conv_b, ssd_dt_bias, ssd_a_log, ssd_d, ssd_norm_w, od_w_in, od_w_out, gla_gate_w, gla_gate_b, gla_norm_w, rg_conv_w, rg_conv_b, rg_w_a, rg_b_a, rg_w_x, rg_b_x, rg_lambda, router_w, router_b, moe_w_gate, moe_w_up, moe_w_down, sh_w_gate, sh_w_up, sh_w_down, final_norm_w):
    xs = jnp.concatenate([x, ctx], axis=1)
    for i in range(DEPTH):
        last = i == DEPTH - 1
        j = i // 2
        mods = adaln_table(c, c_ctx, ada_w[i], ada_b[i])
        if i % 2 == 0:
            hy_filt = (hy_fw0[j], hy_fb0[j], hy_fw1[j], hy_fb1[j], hy_fw2[j], hy_fb2[j], hy_fw3[j], hy_freq[j])
            xs = even_layer_mixer(xs, mods, norm1_w[i], ev_w_in[j], ev_w_out[j], hy_conv_w[j], hy_conv_b[j],
                                  hy_filt, hy_bias[j], ssd_conv_w[j], ssd_conv_b[j], ssd_dt_bias[j],
                                  ssd_a_log[j], ssd_d[j], ssd_norm_w[j])
        else:
            gla_args = (gla_gate_w[j], gla_gate_b[j], gla_norm_w[j])
            rg_args = (rg_conv_w[j], rg_conv_b[j], rg_w_a[j], rg_b_a[j], rg_w_x[j], rg_b_x[j], rg_lambda[j])
            xs = odd_layer_mixer_pallas(xs, mods, norm1_w[i], od_w_in[j], od_w_out[j], gla_args, rg_args)
        n_tiles = N_LAT_TILES if last else N_ROW_TILES
        xs = moe_layer(xs, norm2_w[i], mods, router_w[i], router_b[i], moe_w_gate, moe_w_up, moe_w_down,
                       sh_w_gate[i], sh_w_up[i], sh_w_down[i], n_tiles, i)
    return final_norm(xs, final_norm_w)
```

```python
import functools
import math

import jax
import jax.numpy as jnp
from jax import lax
from jax.experimental import pallas as pl
from jax.experimental.pallas import tpu as pltpu

D_MODEL = 1024
BATCH = 16
SEQ = 2048
DEPTH = 2

CTX_LEN = 256
GRID_W = 64
NORM_EPS = 1e-6

D_HY = D_MODEL // 2
HY_ORDER = 2
HY_SHORT = 3
HY_BANDS = 16
HY_EMB = 1 + 2 * HY_BANDS
HY_FF = 64
HY_DECAY_PCT_LO = 0.3
HY_DECAY_PCT_HI = 1.5
HY_DECAY_TARGET = 1e-2
HY_IN = 3 * D_HY

D_SSM = D_MODEL // 2
SSD_P = 64
SSD_H = D_SSM // SSD_P
SSD_G = 2
SSD_HPG = SSD_H // SSD_G
SSD_N = 128
SSD_CONV = 4
SSD_CHUNK = 128
SSD_XBC = D_SSM + 2 * SSD_G * SSD_N
SSD_IN = D_SSM + SSD_XBC + 2 * SSD_H
EV_IN = HY_IN + SSD_IN
EV_MIX = D_HY + D_SSM

GLA_H = 4
GLA_DV = (D_MODEL // 2) // GLA_H
GLA_DK = GLA_DV // 2
GLA_RANK = 16
GLA_GATE_NORM = 16.0
GLA_CHUNK = 64
GLA_IN = 2 * GLA_H * GLA_DK + 2 * GLA_H * GLA_DV + 2 * GLA_RANK

D_RG = D_MODEL // 2
RG_BLOCKS = 8
RG_BW = D_RG // RG_BLOCKS
RG_CONV = 4
RG_C = 8.0
RG_IN = 2 * D_RG
OD_IN = GLA_IN + RG_IN
OD_MIX = GLA_H * GLA_DV + D_RG

MOE_EXPERTS = 64
MOE_TOPK = 8
MOE_D_EXPERT = 256
MOE_D_SHARED = 256
MOE_SCALE = 2.5
MOE_BLOCK = 512

F32 = jnp.float32
BF16 = jnp.bfloat16
HIGHEST = lax.Precision.HIGHEST

NTOK = SEQ + CTX_LEN
ROW_TILE = 256
N_ROW_TILES = NTOK // ROW_TILE
N_LAT_TILES = SEQ // ROW_TILE

VMEM_LIMIT = 48 * 1024 * 1024
VMEM_LIMIT_BIG = 56 * 1024 * 1024


def _cparams(sem, limit=VMEM_LIMIT):
    return pltpu.CompilerParams(dimension_semantics=sem, vmem_limit_bytes=limit)


def _pick_tile(n, pref):
    t = min(n, pref)
    while n % t:
        t //= 2
    return t


def _silu(x):
    return x / (1.0 + jnp.exp(-x))


def _softplus(x):
    return jnp.maximum(x, 0.0) + jnp.log(1.0 + jnp.exp(-jnp.abs(x)))


def _mm_bf16_body(a_ref, w_ref, o_ref):
    o_ref[...] = jnp.dot(a_ref[...].astype(BF16), w_ref[...].astype(BF16),
                         preferred_element_type=F32).astype(o_ref.dtype)


def _mm_f32_body(a_ref, w_ref, o_ref):
    o_ref[...] = jnp.dot(a_ref[...], w_ref[...], preferred_element_type=F32,
                         precision=HIGHEST).astype(o_ref.dtype)


def pmatmul(a, w, *, exact=False, out_dtype=F32, tm=512, tn=None):
    m, k = a.shape
    n = w.shape[1]
    tm = _pick_tile(m, tm)
    tn = n if tn is None else _pick_tile(n, tn)
    body = _mm_f32_body if exact else _mm_bf16_body
    return pl.pallas_call(
        body,
        name='mm',
        grid=(m // tm, n // tn),
        in_specs=[pl.BlockSpec((tm, k), lambda i, j: (i, 0)),
                  pl.BlockSpec((k, tn), lambda i, j: (0, j))],
        out_specs=pl.BlockSpec((tm, tn), lambda i, j: (i, j)),
        out_shape=jax.ShapeDtypeStruct((m, n), out_dtype),
        compiler_params=_cparams(("parallel", "parallel")),
    )(a, w)


def _mm_split_body(a_ref, w_ref, *o_refs, splits):
    a = a_ref[...]
    for o_ref, (start, width) in zip(o_refs, splits):
        o_ref[...] = jnp.dot(a, w_ref[:, start:start + width],
                             preferred_element_type=F32).astype(o_ref.dtype)


def mm_split(a, w, splits, dtypes, tm=512):
    m, k = a.shape
    n = w.shape[1]
    tm = _pick_tile(m, tm)
    return pl.pallas_call(
        functools.partial(_mm_split_body, splits=tuple(splits)),
        name='mm_split',
        grid=(m // tm,),
        in_specs=[pl.BlockSpec((tm, k), lambda i: (i, 0)),
                  pl.BlockSpec((k, n), lambda i: (0, 0))],
        out_specs=[pl.BlockSpec((tm, wd), lambda i: (i, 0)) for _, wd in splits],
        out_shape=[jax.ShapeDtypeStruct((m, wd), dt) for (_, wd), dt in zip(splits, dtypes)],
        compiler_params=_cparams(("parallel",)),
    )(a, w)


MIX_SAMPLES = 2


def _stack_samples(ref):
    return jnp.concatenate([ref[s] for s in range(MIX_SAMPLES)], axis=0)


def _resid_store(o_ref, x_ref, g_ref, acc):
    for s in range(MIX_SAMPLES):
        o_ref[s] = x_ref[s] + g_ref[s] * acc[s * ROW_TILE:(s + 1) * ROW_TILE]


def _mm_resid_body(*refs, n_pairs):
    a_refs = refs[:n_pairs]
    w_refs = refs[n_pairs:2 * n_pairs]
    x_ref, g_ref, o_ref = refs[2 * n_pairs:]
    acc = jnp.dot(_stack_samples(a_refs[0]), w_refs[0][...], preferred_element_type=F32)
    for a_ref, w_ref in zip(a_refs[1:], w_refs[1:]):
        acc = acc + jnp.dot(_stack_samples(a_ref), w_ref[...], preferred_element_type=F32)
    _resid_store(o_ref, x_ref, g_ref, acc)


def _gate_spec(gate_idx, d):
    return pl.BlockSpec((MIX_SAMPLES, None, None, 1, d), lambda i, j: (i, 1 - j // N_LAT_TILES, gate_idx, 0, 0))


def _mix_tile(width):
    return pl.BlockSpec((MIX_SAMPLES, ROW_TILE, width), lambda i, j: (i, j, 0))


def mm_resid(a_list, w_list, xs, mods, gate_idx, n_tiles):
    b, nt, d = xs.shape
    n_pairs = len(a_list)
    in_specs = [_mix_tile(a.shape[-1]) for a in a_list]
    in_specs += [pl.BlockSpec(w.shape, lambda i, j: (0, 0)) for w in w_list]
    in_specs += [_mix_tile(d), _gate_spec(gate_idx, d)]
    return pl.pallas_call(
        functools.partial(_mm_resid_body, n_pairs=n_pairs),
        name='mm_resid',
        grid=(b // MIX_SAMPLES, n_tiles),
        in_specs=in_specs,
        out_specs=_mix_tile(d),
        out_shape=jax.ShapeDtypeStruct(xs.shape, F32),
        input_output_aliases={2 * n_pairs: 0},
        compiler_params=_cparams(("parallel", "parallel")),
    )(*a_list, *w_list, xs, mods)


def _even_out_body(hy_ref, yf_ref, yb_ref, xs_ref, z_ref, d_ref, nw_ref, w1_ref, w2_ref, x_ref, g_ref, o_ref):
    gw = D_SSM // SSD_G
    merged = []
    for s in range(MIX_SAMPLES):
        y = yf_ref[s].astype(F32) + yb_ref[s].astype(F32) + d_ref[...] * xs_ref[s].astype(F32)
        g = y * _silu(z_ref[s].astype(F32))
        groups = []
        for k in range(SSD_G):
            gk = g[:, k * gw:(k + 1) * gw]
            ms = jnp.mean(gk * gk, axis=-1, keepdims=True)
            groups.append(gk * lax.rsqrt(ms + NORM_EPS) * nw_ref[:, k * gw:(k + 1) * gw])
        merged.append(jnp.concatenate(groups, axis=1).astype(BF16))
    acc = (jnp.dot(_stack_samples(hy_ref), w1_ref[...], preferred_element_type=F32)
           + jnp.dot(jnp.concatenate(merged, axis=0), w2_ref[...], preferred_element_type=F32))
    _resid_store(o_ref, x_ref, g_ref, acc)


def even_out_resid(y_hy, yf, yb, xbc, z, d_skip, norm_w, w_hy, w_ssd, xs, mods):
    b, nt, d = xs.shape
    d_chan = jnp.repeat(d_skip.astype(F32), SSD_P).reshape(1, D_SSM)
    vec = pl.BlockSpec((1, D_SSM), lambda i, j: (0, 0))
    wspec = pl.BlockSpec((D_SSM, d), lambda i, j: (0, 0))
    half = _mix_tile(D_SSM)
    return pl.pallas_call(
        _even_out_body,
        name='even_out',
        grid=(b // MIX_SAMPLES, N_ROW_TILES),
        in_specs=[half, half, half, half, half, vec, vec, wspec, wspec, _mix_tile(d), _gate_spec(2, d)],
        out_specs=_mix_tile(d),
        out_shape=jax.ShapeDtypeStruct(xs.shape, F32),
        input_output_aliases={9: 0},
        compiler_params=_cparams(("parallel", "parallel")),
    )(y_hy, yf, yb, xbc, z, d_chan, norm_w.reshape(1, D_SSM), w_hy, w_ssd, xs, mods)


def _mod_spec(idx, d, b0=0):
    return pl.BlockSpec((None, None, None, 1, d), lambda i, j: (i + b0, 1 - j // N_LAT_TILES, idx, 0, 0))


def adaln_table(c, c_ctx, w, b):
    cv = jax.nn.silu(jnp.concatenate([c, c_ctx[None, :]], axis=0))
    cv = jnp.pad(cv, ((0, 24 - cv.shape[0]), (0, 0)))
    m = pmatmul(cv, w, exact=True, tn=1536)[:BATCH + 1] + b
    per_sample = m[:BATCH]
    ctx_row = jnp.broadcast_to(m[BATCH][None, :], per_sample.shape)
    return jnp.stack([ctx_row, per_sample], axis=1).reshape(BATCH, 2, 6, 1, D_MODEL)


def _norm_mod(x, w, shift, scale):
    ms = jnp.mean(x * x, axis=-1, keepdims=True)
    return (x * lax.rsqrt(ms + NORM_EPS) * w) * (1.0 + scale) + shift


def _norm_mod_body(x_ref, w_ref, sh_ref, sc_ref, o_ref):
    o_ref[0] = _norm_mod(x_ref[0], w_ref[...], sh_ref[...], sc_ref[...]).astype(o_ref.dtype)


def norm_mod(xs, w, mods, shift_idx, scale_idx):
    b, nt, d = xs.shape
    return pl.pallas_call(
        _norm_mod_body,
        name='norm_mod',
        grid=(b, nt // ROW_TILE),
        in_specs=[pl.BlockSpec((1, ROW_TILE, d), lambda i, j: (i, j, 0)),
                  pl.BlockSpec((1, d), lambda i, j: (0, 0)),
                  _mod_spec(shift_idx, d), _mod_spec(scale_idx, d)],
        out_specs=pl.BlockSpec((1, ROW_TILE, d), lambda i, j: (i, j, 0)),
        out_shape=jax.ShapeDtypeStruct(xs.shape, BF16),
        compiler_params=_cparams(("parallel", "parallel")),
    )(xs, w.reshape(1, d), mods, mods)


def _norm_mm_split_body(x_ref, nw_ref, sh_ref, sc_ref, w_ref, *o_refs, splits):
    h = jnp.concatenate([_norm_mod(x_ref[s], nw_ref[...], sh_ref[s], sc_ref[s]).astype(BF16)
                         for s in range(MIX_SAMPLES)], axis=0)
    for o_ref, (start, width) in zip(o_refs, splits):
        y = jnp.dot(h, w_ref[:, start:start + width], preferred_element_type=F32).astype(o_ref.dtype)
        for s in range(MIX_SAMPLES):
            o_ref[s] = y[s * ROW_TILE:(s + 1) * ROW_TILE]


def norm_mm_split(xs, norm_w, mods, w, splits, dtypes):
    b, nt, d = xs.shape
    n = w.shape[1]
    return pl.pallas_call(
        functools.partial(_norm_mm_split_body, splits=tuple(splits)),
        name='norm_mm_split',
        grid=(b // MIX_SAMPLES, nt // ROW_TILE),
        in_specs=[_mix_tile(d), pl.BlockSpec((1, d), lambda i, j: (0, 0)), _gate_spec(0, d), _gate_spec(1, d),
                  pl.BlockSpec((d, n), lambda i, j: (0, 0))],
        out_specs=[_mix_tile(wd) for _, wd in splits],
        out_shape=[jax.ShapeDtypeStruct((b, nt, wd), dt) for (_, wd), dt in zip(splits, dtypes)],
        compiler_params=_cparams(("parallel", "parallel")),
    )(xs, norm_w.reshape(1, d), mods, mods, w)


def _final_norm_body(x_ref, w_ref, o_ref):
    x = x_ref[0]
    ms = jnp.mean(x * x, axis=-1, keepdims=True)
    o_ref[0] = x * lax.rsqrt(ms + NORM_EPS) * w_ref[...]


def final_norm(xs, w):
    b, _, d = xs.shape
    return pl.pallas_call(
        _final_norm_body,
        name='final_norm',
        grid=(b, N_LAT_TILES),
        in_specs=[pl.BlockSpec((1, ROW_TILE, d), lambda i, j: (i, j, 0)),
                  pl.BlockSpec((1, d), lambda i, j: (0, 0))],
        out_specs=pl.BlockSpec((1, ROW_TILE, d), lambda i, j: (i, j, 0)),
        out_shape=jax.ShapeDtypeStruct((b, SEQ, d), F32),
        compiler_params=_cparams(("parallel", "parallel")),
    )(xs, w.reshape(1, d))


def _route_t_body(x_ref, w_ref, sh_ref, sc_ref, rwh_ref, rwl_ref, rb_ref, h_ref, idx_ref, wsel_ref, rank_ref, cnt_ref,
                  *, group_size):
    first = (pl.program_id(0) % group_size == 0) & (pl.program_id(1) == 0)

    @pl.when(first)
    def _():
        cnt_ref[...] = jnp.zeros_like(cnt_ref)

    h = _norm_mod(x_ref[0], w_ref[...], sh_ref[...], sc_ref[...])
    h_hi = h.astype(BF16)
    h_ref[0] = h_hi
    h_lo = (h - h_hi.astype(F32)).astype(BF16)
    nt = (((1,), (1,)), ((), ()))
    logits = (lax.dot_general(rwh_ref[...], h_hi, nt, preferred_element_type=F32)
              + lax.dot_general(rwh_ref[...], h_lo, nt, preferred_element_type=F32)
              + lax.dot_general(rwl_ref[...], h_hi, nt, preferred_element_type=F32))
    scores = 1.0 / (1.0 + jnp.exp(-logits))
    ne, tm = scores.shape
    expert = lax.broadcasted_iota(jnp.int32, (ne, tm), 0).astype(F32)
    slot = lax.broadcasted_iota(jnp.int32, (MOE_TOPK, tm), 0)
    sel = scores + rb_ref[...]
    picked = jnp.zeros((ne, tm), F32)
    hits = []
    idx_out = jnp.zeros((MOE_TOPK, tm), F32)
    w_out = jnp.zeros((MOE_TOPK, tm), F32)
    for k in range(MOE_TOPK):
        m = jnp.max(sel, axis=0, keepdims=True)
        ik = jnp.min(jnp.where(sel == m, expert, float(ne)), axis=0, keepdims=True)
        hit = expert == ik
        wk = jnp.sum(jnp.where(hit, scores, 0.0), axis=0, keepdims=True)
        sel = jnp.where(hit, -jnp.inf, sel)
        picked = picked + hit.astype(F32)
        hits.append(hit)
        idx_out = jnp.where(slot == k, ik, idx_out)
        w_out = jnp.where(slot == k, wk, w_out)
    wsum = jnp.sum(w_out, axis=0, keepdims=True)
    wsel_ref[0] = w_out / wsum * MOE_SCALE
    idx_ref[0] = idx_out.astype(jnp.int32)
    r_i = lax.broadcasted_iota(jnp.int32, (tm, tm), 0)
    c_i = lax.broadcasted_iota(jnp.int32, (tm, tm), 1)
    earlier = (r_i < c_i).astype(BF16)
    before = jnp.dot(picked.astype(BF16), earlier, preferred_element_type=F32) + cnt_ref[...]
    rank_out = jnp.zeros((MOE_TOPK, tm), F32)
    for k in range(MOE_TOPK):
        rk = jnp.sum(jnp.where(hits[k], before, 0.0), axis=0, keepdims=True)
        rank_out = jnp.where(slot == k, rk, rank_out)
    rank_ref[0] = rank_out.astype(jnp.int32)
    cnt_ref[...] = cnt_ref[...] + jnp.sum(picked, axis=1, keepdims=True)


def route_t(xs, w, mods, router_w, router_b, n_tiles, group_size):
    b, _, d = xs.shape
    rows = n_tiles * ROW_TILE
    rwt = router_w.T.astype(F32)
    rwt_hi, rwt_lo = _split_bf16(rwt)
    small = lambda dt: jax.ShapeDtypeStruct((b, MOE_TOPK, rows), dt)
    small_spec = pl.BlockSpec((1, MOE_TOPK, ROW_TILE), lambda i, j: (i, 0, j))
    return pl.pallas_call(
        functools.partial(_route_t_body, group_size=group_size),
        name='route',
        grid=(b, n_tiles),
        in_specs=[pl.BlockSpec((1, ROW_TILE, d), lambda i, j: (i, j, 0)),
                  pl.BlockSpec((1, d), lambda i, j: (0, 0)),
                  _mod_spec(3, d), _mod_spec(4, d),
                  pl.BlockSpec((MOE_EXPERTS, d), lambda i, j: (0, 0)),
                  pl.BlockSpec((MOE_EXPERTS, d), lambda i, j: (0, 0)),
                  pl.BlockSpec((MOE_EXPERTS, 1), lambda i, j: (0, 0))],
        out_specs=[pl.BlockSpec((1, ROW_TILE, d), lambda i, j: (i, j, 0)),
                   small_spec, small_spec, small_spec,
                   pl.BlockSpec((None, MOE_EXPERTS, 1), lambda i, j: (i // group_size, 0, 0))],
        out_shape=[jax.ShapeDtypeStruct((b, rows, d), BF16), small(jnp.int32), small(F32), small(jnp.int32),
                   jax.ShapeDtypeStruct((b // group_size, MOE_EXPERTS, 1), F32)],
        compiler_params=_cparams(("arbitrary", "arbitrary")),
    )(xs, w.reshape(1, d), mods, mods, rwt_hi, rwt_lo, router_b.astype(F32).reshape(MOE_EXPERTS, 1))


def _swiglu(x, wg, wu, wd):
    g = jnp.dot(x, wg, preferred_element_type=F32)
    u = jnp.dot(x, wu, preferred_element_type=F32)
    h = (_silu(g) * u).astype(BF16)
    return jnp.dot(h, wd, preferred_element_type=F32)


def _expert_body(be_ref, nu_ref, xa_ref, xb_ref, wg_ref, wu_ref, wd_ref, o_ref, wg_s, wu_s, wd_s):
    i = pl.program_id(0)
    used = i < nu_ref[0]

    @pl.when(used & ((i == 0) | (be_ref[i] != be_ref[jnp.maximum(i - 1, 0)])))
    def _():
        wg_s[...] = wg_ref[0].astype(BF16)
        wu_s[...] = wu_ref[0].astype(BF16)
        wd_s[...] = wd_ref[0].astype(BF16)

    @pl.when(used)
    def _():
        x = jnp.concatenate([xa_ref[...], xb_ref[...]], axis=1)
        o_ref[...] = _swiglu(x, wg_s[...], wu_s[...], wd_s[...]).astype(o_ref.dtype)

    @pl.when(jnp.logical_not(used))
    def _():
        o_ref[...] = jnp.zeros_like(o_ref)


def moe_experts(x_rows, block_e, n_used, wg, wu, wd, layer):
    rows, d = x_rows.shape
    n_blocks = rows // MOE_BLOCK
    f = wg.shape[-1]
    grid_spec = pltpu.PrefetchScalarGridSpec(
        num_scalar_prefetch=2,
        grid=(n_blocks,),
        in_specs=[
            pl.BlockSpec((MOE_BLOCK, d // 2), lambda i, be, nu: (i, 0)),
            pl.BlockSpec((MOE_BLOCK, d // 2), lambda i, be, nu: (i, 1)),
            pl.BlockSpec((None, 1, d, f), lambda i, be, nu: (layer, be[i], 0, 0)),
            pl.BlockSpec((None, 1, d, f), lambda i, be, nu: (layer, be[i], 0, 0)),
            pl.BlockSpec((None, 1, f, d), lambda i, be, nu: (layer, be[i], 0, 0)),
        ],
        out_specs=pl.BlockSpec((MOE_BLOCK, d), lambda i, be, nu: (i, 0)),
        scratch_shapes=[pltpu.VMEM((d, f), BF16), pltpu.VMEM((d, f), BF16), pltpu.VMEM((f, d), BF16)],
    )
    return pl.pallas_call(
        _expert_body,
        name='experts',
        grid_spec=grid_spec,
        out_shape=jax.ShapeDtypeStruct((rows, d), BF16),
        compiler_params=_cparams(("arbitrary",)),
    )(block_e, n_used, x_rows, x_rows, wg, wu, wd)


COMBINE_SAMPLES = 2


def _shared_resid_body(h_ref, wg_ref, wu_ref, wd_ref, pk_ref, ws_ref, x_ref, g_ref, o_ref):
    for s in range(COMBINE_SAMPLES):
        y = _swiglu(h_ref[s], wg_ref[...], wu_ref[...], wd_ref[...])
        ws = ws_ref[s]
        for k in range(MOE_TOPK):
            y = y + ws[:, k:k + 1] * pk_ref[k, s].astype(F32)
        o_ref[s] = x_ref[s] + g_ref[s] * y


def shared_resid(h, picked, wsel, xs, mods, wg, wu, wd, n_tiles, b0):
    b, _, d = h.shape
    f = wg.shape[-1]
    ns = COMBINE_SAMPLES
    s0 = b0 // ns
    tile = pl.BlockSpec((ns, ROW_TILE, d), lambda i, j: (i, j, 0))
    xs_tile = pl.BlockSpec((ns, ROW_TILE, d), lambda i, j: (i + s0, j, 0))
    gate = pl.BlockSpec((ns, None, None, 1, d), lambda i, j: (i + s0, 1 - j // N_LAT_TILES, 5, 0, 0))
    return pl.pallas_call(
        _shared_resid_body,
        name='shared_resid',
        grid=(b // ns, n_tiles),
        in_specs=[tile,
                  pl.BlockSpec((d, f), lambda i, j: (0, 0)),
                  pl.BlockSpec((d, f), lambda i, j: (0, 0)),
                  pl.BlockSpec((f, d), lambda i, j: (0, 0)),
                  pl.BlockSpec((MOE_TOPK, ns, ROW_TILE, d), lambda i, j: (0, i, j, 0)),
                  pl.BlockSpec((ns, ROW_TILE, MOE_TOPK), lambda i, j: (i, j, 0)),
                  xs_tile, gate],
        out_specs=xs_tile,
        out_shape=jax.ShapeDtypeStruct(xs.shape, F32),
        input_output_aliases={6: 0},
        compiler_params=_cparams(("parallel", "parallel")),
    )(h, wg, wu, wd, picked, wsel, xs, mods)


MOE_GROUPS = 2


def moe_layer(xs, norm_w, mods, router_w, router_b, w_gate, w_up, w_down, sh_gate, sh_up, sh_down, n_tiles, layer):
    bsz, _, d = xs.shape
    shared_w = (sh_gate.astype(BF16), sh_up.astype(BF16), sh_down.astype(BF16))
    b = bsz // MOE_GROUPS
    h, idx, wsel, rank, counts = route_t(xs, norm_w, mods, router_w, router_b, n_tiles, b)
    h_flat = h.reshape(-1, d)
    wsel = jnp.swapaxes(wsel, 1, 2)
    for g in range(MOE_GROUPS):
        sl = slice(g * b, (g + 1) * b)
        xs = _moe_group(xs, mods, h_flat, h[sl], idx[sl], wsel[sl], rank[sl], counts[g, :, 0], w_gate, w_up, w_down,
                        shared_w, n_tiles, layer, g * b)
    return xs


def _moe_group(xs, mods, h_flat, h, idx, wsel, rank, counts, w_gate, w_up, w_down, shared_w, n_tiles, layer, b0):
    b, rows_per_sample, d = h.shape
    n = b * rows_per_sample
    counts = counts.astype(jnp.int32)
    padded = (counts + MOE_BLOCK - 1) // MOE_BLOCK * MOE_BLOCK
    ends = jnp.cumsum(padded)
    starts = ends - padded
    nk = n * MOE_TOPK
    n_blocks = -(-nk // MOE_BLOCK) + MOE_EXPERTS
    rows = n_blocks * MOE_BLOCK
    n_pad = rows - nk
    e_iota = jnp.arange(MOE_EXPERTS, dtype=jnp.int32)
    dest = jnp.sum(jnp.where(idx[..., None] == e_iota, starts, 0), axis=-1) + rank
    blk_start = jnp.arange(n_blocks, dtype=jnp.int32) * MOE_BLOCK
    block_e = jnp.minimum(jnp.sum(ends[None, :] <= blk_start[:, None], axis=1), MOE_EXPERTS - 1).astype(jnp.int32)
    n_used = (ends[-1:] // MOE_BLOCK).astype(jnp.int32)
    pad = padded - counts
    cum_pad = jnp.cumsum(pad)
    m = jnp.arange(n_pad, dtype=jnp.int32)
    e_m = jnp.sum(cum_pad[None, :] <= m[:, None], axis=1)
    base = jnp.sum(jnp.where(jnp.minimum(e_m, MOE_EXPERTS - 1)[:, None] == e_iota,
                             starts + counts - (cum_pad - pad), 0), axis=1)
    pad_row = jnp.where(e_m < MOE_EXPERTS, base + m, ends[-1] + m - cum_pad[-1])
    tok0 = b0 * rows_per_sample
    tok = (tok0 + jnp.arange(b, dtype=jnp.int32)[:, None, None] * rows_per_sample
           + jnp.arange(rows_per_sample, dtype=jnp.int32)[None, None, :])
    tok = jnp.broadcast_to(tok, dest.shape).reshape(-1)
    _, row_tok = lax.sort((jnp.concatenate([dest.reshape(-1), pad_row]).astype(jnp.int32),
                           jnp.concatenate([tok, tok0 + m % n])), num_keys=1)
    x_rows = h_flat[row_tok]
    y_rows = moe_experts(x_rows, block_e, n_used, w_gate, w_up, w_down, layer)
    picked = y_rows[jnp.swapaxes(dest, 0, 1)]
    return shared_resid(h, picked, wsel, xs, mods, *shared_w, n_tiles, b0)


def _dwconv_body(x_ref, w_ref, b_ref, o_ref, *, width, act):
    chunk = ROW_TILE
    n_chunks = NTOK // chunk
    first_of_seq = (0, N_LAT_TILES)
    last_of_seq = (N_LAT_TILES - 1, n_chunks - 1)
    tc = x_ref.shape[-1]
    halo = 16
    row = lax.broadcasted_iota(jnp.int32, (chunk, tc), 0)
    zero_row = jnp.zeros((1, tc), F32)
    for c in range(n_chunks):
        r0 = c * chunk
        cur = x_ref[0, r0:r0 + chunk, :].astype(F32)
        if c in first_of_seq:
            prev_last = zero_row
        else:
            prev_last = x_ref[0, r0 - halo:r0, :].astype(F32)[halo - 1:halo, :]
        if c in last_of_seq:
            next0 = next1 = zero_row
        else:
            nxt = x_ref[0, r0 + chunk:r0 + chunk + halo, :].astype(F32)
            next0, next1 = nxt[0:1, :], nxt[1:2, :]
        xm1 = jnp.where(row == 0, prev_last, pltpu.roll(cur, 1, 0))
        xp1 = jnp.where(row == chunk - 1, next0, pltpu.roll(cur, chunk - 1, 0))
        y = w_ref[0:1, :] * xm1 + w_ref[1:2, :] * cur + w_ref[2:3, :] * xp1 + b_ref[...]
        if width == 4:
            xp2 = jnp.where(row == chunk - 2, next0,
                            jnp.where(row == chunk - 1, next1, pltpu.roll(cur, chunk - 2, 0)))
            y = y + w_ref[3:4, :] * xp2
        if act:
            y = _silu(y)
        o_ref[0, c * chunk:(c + 1) * chunk, :] = y.astype(o_ref.dtype)


def dwconv_stream(x, w, b, act, tc=256):
    bsz, nt, c = x.shape
    width = w.shape[0]
    return pl.pallas_call(
        functools.partial(_dwconv_body, width=width, act=act),
        name='dwconv',
        grid=(bsz, c // tc),
        in_specs=[pl.BlockSpec((1, nt, tc), lambda i, j: (i, 0, j)),
                  pl.BlockSpec((width, tc), lambda i, j: (0, j)),
                  pl.BlockSpec((1, tc), lambda i, j: (0, j))],
        out_specs=pl.BlockSpec((1, nt, tc), lambda i, j: (i, 0, j)),
        out_shape=jax.ShapeDtypeStruct(x.shape, BF16),
        compiler_params=_cparams(("parallel", "parallel")),
    )(x, w, b.reshape(1, c))


HY_FB = 512
DFT_SPLIT = 64


def dft_matrices(n):
    t = jnp.arange(n, dtype=jnp.int32)[None, :]
    ka = jnp.arange(DFT_SPLIT, dtype=jnp.int32)[:, None]
    kb = jnp.arange(n // DFT_SPLIT, dtype=jnp.int32)[:, None] * DFT_SPLIT
    ang_a = (2.0 * math.pi / (2 * n)) * ((ka * t) % (2 * n)).astype(F32)
    ang_b = (2.0 * math.pi / (2 * n)) * ((kb * t) % (2 * n)).astype(F32)
    ca, sa = jnp.cos(ang_a)[None], jnp.sin(ang_a)[None]
    cb, sb = jnp.cos(ang_b)[:, None], jnp.sin(ang_b)[:, None]
    cos_kt = (ca * cb - sa * sb).reshape(n, n)
    sin_kt = (sa * cb + ca * sb).reshape(n, n)
    idx = jnp.arange(n, dtype=jnp.int32)
    nyq = jnp.where(idx % 2 == 0, 1.0, -1.0).astype(F32)
    fwd = jnp.concatenate([cos_kt, jnp.where(idx[:, None] == 0, nyq[None, :], -sin_kt)], axis=0)
    scale = jnp.where(idx == 0, 0.5, 1.0)[None, :] / n
    inv = jnp.concatenate([cos_kt * scale, jnp.where(idx[None, :] == 0, nyq[:, None], -sin_kt) * scale], axis=1)
    return fwd.astype(BF16), inv.astype(BF16)


def hyena_filter_taps(n, fw0, fb0, fw1, fb1, fw2, fb2, fw3, freq):
    pos = jnp.arange(n, dtype=F32)
    t = pos / max(n - 1, 1)
    bands = jnp.linspace(1e-4, HY_BANDS - 1, HY_BANDS, dtype=F32)
    ang = (2.0 * math.pi / n) * pos[:, None] * bands[None, :]
    feats = jnp.concatenate([t[:, None], jnp.cos(ang), -jnp.sin(ang)], axis=-1)
    h = jnp.sin(freq * (jnp.dot(feats, fw0, precision=HIGHEST) + fb0))
    h = jnp.sin(freq * (jnp.dot(h, fw1, precision=HIGHEST) + fb1))
    h = jnp.sin(freq * (jnp.dot(h, fw2, precision=HIGHEST) + fb2))
    h = pmatmul(h, fw3, exact=True).reshape(n, 2, HY_ORDER, D_HY)
    deltas = jnp.abs(jnp.linspace(math.log(HY_DECAY_PCT_LO) / HY_DECAY_TARGET,
                                  math.log(HY_DECAY_PCT_HI) / HY_DECAY_TARGET, D_HY, dtype=F32))
    h = h * jnp.exp(-t[:, None] * deltas)[:, None, None, :]
    h0 = h[:, 0]
    h1 = h[:, 1].at[0].set(0.0)
    norm = jnp.sum(jnp.abs(h0), axis=0, keepdims=True) + jnp.sum(jnp.abs(h1), axis=0, keepdims=True)
    h0 = (h0 / norm).reshape(n, HY_ORDER * D_HY)
    h1 = (h1 / norm).reshape(n, HY_ORDER * D_HY)
    return h0 + h1, h0 - h1


def _split_bf16(a):
    hi = a.astype(BF16)
    return hi, (a - hi.astype(F32)).astype(BF16)


def hyena_spectrum(fwd, hsum, hdiff, fb):
    n = hsum.shape[0]
    a = pmatmul(fwd, hsum.astype(BF16))
    bm = pmatmul(fwd, hdiff.astype(BF16))
    sr = a[:n]
    si = bm[n:]
    nyq = a[n]
    first = (jnp.arange(n) == 0)[:, None]
    p = sr
    q = jnp.where(first, 0.0, si)
    s = jnp.where(first, nyq[None, :], sr)
    spec = jnp.stack([p, q, s], axis=0).reshape(3, n // fb, fb, HY_ORDER, D_HY)
    return spec.transpose(3, 1, 0, 2, 4)


def _hyena_body(u_ref, fre_ref, fim_ref, gre_ref, gim_ref, sp_ref, bias_ref, prev_ref, o_ref,
                vin, acc, *, nf):
    del prev_ref
    o = pl.program_id(1)
    f = pl.program_id(2)
    c = D_HY

    @pl.when((o == 0) & (f == 0))
    def _():
        vin[...] = u_ref[0, :, 0:c]

    @pl.when(f == 0)
    def _():
        acc[...] = jnp.zeros_like(acc)

    v = vin[...]
    vr = jnp.dot(fre_ref[...], v, preferred_element_type=F32)
    vi = jnp.dot(fim_ref[...], v, preferred_element_type=F32)
    p, q, s = sp_ref[0], sp_ref[1], sp_ref[2]
    zr = (vr * p - vi * q).astype(BF16)
    zi = (vr * q + vi * s).astype(BF16)
    acc[...] += (jnp.dot(gre_ref[...], zr, preferred_element_type=F32)
                 + jnp.dot(gim_ref[...], zi, preferred_element_type=F32))

    @pl.when((o == 0) & (f == nf - 1))
    def _():
        z = u_ref[0, :, c:2 * c].astype(F32) * (acc[...] + bias_ref[0:1, :] * vin[...].astype(F32))
        vin[...] = z.astype(BF16)

    @pl.when((o == 1) & (f == nf - 1))
    def _():
        y = u_ref[0, :, 2 * c:3 * c].astype(F32) * (acc[...] + bias_ref[1:2, :] * vin[...].astype(F32))
        o_ref[0] = y.astype(o_ref.dtype)


def hyena_long_conv(u, fwd_bf16, inv_bf16, spec, bias, n, row_block, prev_out):
    bsz = u.shape[0]
    fb = spec.shape[3]
    nf = n // fb
    out_shape = jax.ShapeDtypeStruct((bsz, NTOK, D_HY), BF16)
    if prev_out is None:
        prev_out = jnp.zeros(out_shape.shape, BF16)
    args = [u, fwd_bf16, fwd_bf16, inv_bf16, inv_bf16, spec, bias, prev_out]
    aliases = {7: 0}
    return pl.pallas_call(
        functools.partial(_hyena_body, nf=nf),
        name='hyena',
        grid=(bsz, HY_ORDER, nf),
        in_specs=[pl.BlockSpec((1, n, 3 * D_HY), lambda b, o, f: (b, row_block, 0)),
                  pl.BlockSpec((fb, n), lambda b, o, f: (f, 0)),
                  pl.BlockSpec((fb, n), lambda b, o, f: (nf + f, 0)),
                  pl.BlockSpec((n, fb), lambda b, o, f: (0, f)),
                  pl.BlockSpec((n, fb), lambda b, o, f: (0, nf + f)),
                  pl.BlockSpec((None, None, 3, fb, D_HY), lambda b, o, f: (o, f, 0, 0, 0)),
                  pl.BlockSpec((HY_ORDER, D_HY), lambda b, o, f: (0, 0)),
                  pl.BlockSpec(memory_space=pl.ANY)],
        out_specs=pl.BlockSpec((1, n, D_HY), lambda b, o, f: (b, row_block, 0)),
        out_shape=out_shape,
        scratch_shapes=[pltpu.VMEM((n, D_HY), BF16), pltpu.VMEM((n, D_HY), F32)],
        input_output_aliases=aliases,
        compiler_params=_cparams(("parallel", "arbitrary", "arbitrary"), VMEM_LIMIT_BIG),
    )(*args)


def hyena_mixer_stream(p_hy, conv_w, conv_b, filt, bias):
    u = dwconv_stream(p_hy, conv_w, conv_b, act=False)
    out = None
    for n, row_block in ((SEQ, 0), (CTX_LEN, SEQ // CTX_LEN)):
        fb = min(HY_FB, n)
        fwd, inv = dft_matrices(n)
        hsum, hdiff = hyena_filter_taps(n, *filt)
        spec = hyena_spectrum(fwd, hsum, hdiff, fb)
        out = hyena_long_conv(u, fwd, inv, spec, bias, n, row_block, out)
    return out


SCAN_SAMPLES = 2


def _tri(n, kind):
    r = lax.broadcasted_iota(jnp.int32, (n, n), 0)
    c = lax.broadcasted_iota(jnp.int32, (n, n), 1)
    return (c <= r) if kind == 'lower' else (c >= r)


def _ssd_dir(xbc_ref, dt_ref, dtt_ref, bias_r, bias_c, a_r, a_c, st_ref, y_ref, *, s, d, reverse):
    q = SSD_CHUNK
    nh = SSD_H
    gw = SSD_HPG * SSD_P
    lower = _tri(q, 'lower')
    upper = _tri(q, 'upper')
    lower_f = lower.astype(F32)
    upper_f = upper.astype(F32)
    dt_col = _softplus(dt_ref[s] + bias_r)
    dt_row = _softplus(dtt_ref[s] + bias_c)
    da_col = dt_col * a_r
    da_row = dt_row * a_c
    if not reverse:
        acs_col = jnp.dot(lower_f, da_col, preferred_element_type=F32, precision=HIGHEST)
        acs_row = jnp.dot(da_row, upper_f, preferred_element_type=F32, precision=HIGHEST)
        mask = lower
        edge = q - 1
    else:
        acs_col = jnp.dot(upper_f, da_col, preferred_element_type=F32, precision=HIGHEST)
        acs_row = jnp.dot(da_row, lower_f, preferred_element_type=F32, precision=HIGHEST)
        mask = upper
        edge = 0
    h0 = d * nh
    hh = lax.broadcasted_iota(jnp.int32, (2 * nh, nh * SSD_P), 0)
    cc = lax.broadcasted_iota(jnp.int32, (2 * nh, nh * SSD_P), 1) // SSD_P
    expand = (hh == cc + h0).astype(F32)
    acs_c = jnp.dot(acs_col, expand, preferred_element_type=F32, precision=HIGHEST)
    dt_c = jnp.dot(dt_col, expand, preferred_element_type=F32, precision=HIGHEST)
    total_c = acs_c[edge:edge + 1, :]
    e_in_c = jnp.exp(acs_c)
    w_end_c = jnp.exp(total_c - acs_c) * dt_c
    dec_c = jnp.exp(total_c)
    xs = xbc_ref[s, :, 0:D_SSM]
    xs_f = xs.astype(F32)
    for g in range(SSD_G):
        bm = xbc_ref[s, :, D_SSM + g * SSD_N:D_SSM + (g + 1) * SSD_N]
        cm = xbc_ref[s, :, D_SSM + SSD_G * SSD_N + g * SSD_N:D_SSM + SSD_G * SSD_N + (g + 1) * SSD_N]
        cb = lax.dot_general(cm, bm, (((1,), (1,)), ((), ())), preferred_element_type=F32)
        lws = []
        for k in range(SSD_HPG):
            h = h0 + g * SSD_HPG + k
            seg = acs_col[:, h:h + 1] - acs_row[h:h + 1, :]
            decay = jnp.exp(jnp.where(mask, seg, -jnp.inf))
            lws.append((cb * decay * dt_row[h:h + 1, :]).astype(BF16))
        lw = jnp.concatenate(lws, axis=1)
        xg = xs[:, g * gw:(g + 1) * gw]
        rb = lax.broadcasted_iota(jnp.int32, (SSD_HPG * q, gw), 0) // q
        cbk = lax.broadcasted_iota(jnp.int32, (SSD_HPG * q, gw), 1) // SSD_P
        x_bd = jnp.where(rb == cbk, jnp.concatenate([xg] * SSD_HPG, axis=0), jnp.zeros((), BF16))
        y_in = jnp.dot(lw, x_bd, preferred_element_type=F32)
        st = st_ref[s, g]
        y_st = jnp.dot(cm, st.astype(BF16), preferred_element_type=F32) * e_in_c[:, g * gw:(g + 1) * gw]
        y_ref[s, :, g * gw:(g + 1) * gw] = (y_in + y_st).astype(y_ref.dtype)
        xw = (xs_f[:, g * gw:(g + 1) * gw] * w_end_c[:, g * gw:(g + 1) * gw]).astype(BF16)
        upd = lax.dot_general(bm, xw, (((0,), (0,)), ((), ())), preferred_element_type=F32)
        st_ref[s, g] = st * dec_c[:, g * gw:(g + 1) * gw] + upd


def _ssd_body(xf_ref, dtf_ref, dttf_ref, xb_ref, dtb_ref, dttb_ref, bias_r, bias_c, a_r, a_c,
              yf_ref, yb_ref, stf, stb):
    @pl.when(pl.program_id(1) == 0)
    def _():
        stf[...] = jnp.zeros_like(stf)
        stb[...] = jnp.zeros_like(stb)

    for s in range(SCAN_SAMPLES):
        _ssd_dir(xf_ref, dtf_ref, dttf_ref, bias_r[...], bias_c[...], a_r[...], a_c[...], stf, yf_ref,
                 s=s, d=0, reverse=False)
        _ssd_dir(xb_ref, dtb_ref, dttb_ref, bias_r[...], bias_c[...], a_r[...], a_c[...], stb, yb_ref,
                 s=s, d=1, reverse=True)


def ssd_scan(xbc, dt, dt_bias, a_log):
    bsz = xbc.shape[0]
    nc = NTOK // SSD_CHUNK
    nlat = SEQ // SSD_CHUNK
    dtt = jnp.swapaxes(dt, 1, 2)
    fwd_chunk = lambda s: (s + nlat) % nc
    bwd_chunk = lambda s: nc - 1 - s
    a = -jnp.exp(a_log.astype(F32)).reshape(1, 2 * SSD_H)
    bias = dt_bias.astype(F32).reshape(1, 2 * SSD_H)
    ns = SCAN_SAMPLES
    x_spec = lambda cm: pl.BlockSpec((ns, SSD_CHUNK, SSD_XBC), lambda b, s: (b, cm(s), 0))
    dt_spec = lambda cm: pl.BlockSpec((ns, SSD_CHUNK, 2 * SSD_H), lambda b, s: (b, cm(s), 0))
    dtt_spec = lambda cm: pl.BlockSpec((ns, 2 * SSD_H, SSD_CHUNK), lambda b, s: (b, 0, cm(s)))
    y_spec = lambda cm: pl.BlockSpec((ns, SSD_CHUNK, D_SSM), lambda b, s: (b, cm(s), 0))
    row = pl.BlockSpec((1, 2 * SSD_H), lambda b, s: (0, 0))
    col = pl.BlockSpec((2 * SSD_H, 1), lambda b, s: (0, 0))
    y_shape = jax.ShapeDtypeStruct((bsz, NTOK, D_SSM), BF16)
    gw = SSD_HPG * SSD_P
    return pl.pallas_call(
        _ssd_body,
        name='ssd_scan',
        grid=(bsz // ns, nc),
        in_specs=[x_spec(fwd_chunk), dt_spec(fwd_chunk), dtt_spec(fwd_chunk),
                  x_spec(bwd_chunk), dt_spec(bwd_chunk), dtt_spec(bwd_chunk),
                  row, col, row, col],
        out_specs=[y_spec(fwd_chunk), y_spec(bwd_chunk)],
        out_shape=[y_shape, y_shape],
        scratch_shapes=[pltpu.VMEM((ns, SSD_G, SSD_N, gw), F32), pltpu.VMEM((ns, SSD_G, SSD_N, gw), F32)],
        compiler_params=_cparams(("parallel", "arbitrary")),
    )(xbc, dt, dtt, xbc, dt, dtt, bias, bias.reshape(-1, 1), a, a.reshape(-1, 1))


def even_layer_mixer(xs, mods, norm_w, w_in, w_out, hy_conv_w, hy_conv_b, hy_filt, hy_bias,
                     ssd_conv_w, ssd_conv_b, ssd_dt_bias, ssd_a_log, ssd_d, ssd_norm_w):
    splits = ((0, HY_IN), (HY_IN, D_SSM), (HY_IN + D_SSM, SSD_XBC), (HY_IN + D_SSM + SSD_XBC, 2 * SSD_H))
    p_hy, z, xbc_raw, dt = norm_mm_split(xs, norm_w, mods, w_in.astype(BF16), splits, (BF16, BF16, BF16, F32))
    y_hy = hyena_mixer_stream(p_hy, hy_conv_w, hy_conv_b, hy_filt, hy_bias)
    xbc = dwconv_stream(xbc_raw, ssd_conv_w, ssd_conv_b, act=True)
    yf, yb = ssd_scan(xbc, dt, ssd_dt_bias, ssd_a_log)
    wo = w_out.astype(BF16)
    return even_out_resid(y_hy, yf, yb, xbc, z, ssd_d, ssd_norm_w, wo[:D_HY], wo[D_HY:], xs, mods)


GLA_QK = GLA_H * GLA_DK
GLA_V = GLA_H * GLA_DV


def _gla_dir(qkv_ref, lr_ref, gw_ref, gb_ref, st_ref, o_ref, *, s, d, reverse):
    q = GLA_CHUNK
    tri = _tri(q, 'upper' if reverse else 'lower')
    edge = 0 if reverse else q - 1
    lr = lr_ref[s, :, d * GLA_RANK:(d + 1) * GLA_RANK]
    logit = jnp.dot(lr, gw_ref[d], preferred_element_type=F32, precision=HIGHEST) + gb_ref[d:d + 1, :]
    log_g = -_softplus(-logit) * (1.0 / GLA_GATE_NORM)
    gcum = jnp.dot(tri.astype(F32), log_g, preferred_element_type=F32, precision=HIGHEST)
    total = gcum[edge:edge + 1, :]
    qf = qkv_ref[s, :, 0:GLA_QK].astype(F32)
    kf = qkv_ref[s, :, GLA_QK:2 * GLA_QK].astype(F32)
    v = qkv_ref[s, :, 2 * GLA_QK:2 * GLA_QK + GLA_V]
    qg = (qf * (GLA_DK ** -0.5) * jnp.exp(gcum)).astype(BF16)
    kg = (kf * jnp.exp(-gcum)).astype(BF16)
    kw = (kf * jnp.exp(total - gcum)).astype(BF16)
    rb = lax.broadcasted_iota(jnp.int32, (GLA_H * q, GLA_QK), 0) // q
    cb = lax.broadcasted_iota(jnp.int32, (GLA_H * q, GLA_QK), 1) // GLA_DK
    k_bd = jnp.where(rb == cb, jnp.concatenate([kg] * GLA_H, axis=0), jnp.zeros((), BF16))
    att = lax.dot_general(qg, k_bd, (((1,), (1,)), ((), ())), preferred_element_type=F32)
    i_i = lax.broadcasted_iota(jnp.int32, (q, GLA_H * q), 0)
    j_i = lax.broadcasted_iota(jnp.int32, (q, GLA_H * q), 1) % q
    keep = (j_i >= i_i) if reverse else (j_i <= i_i)
    att = jnp.where(keep, att, 0.0).astype(BF16)
    rv = lax.broadcasted_iota(jnp.int32, (GLA_H * q, GLA_V), 0) // q
    cv = lax.broadcasted_iota(jnp.int32, (GLA_H * q, GLA_V), 1) // GLA_DV
    v_bd = jnp.where(rv == cv, jnp.concatenate([v] * GLA_H, axis=0), jnp.zeros((), BF16))
    st = st_ref[s]
    o_in = jnp.dot(att, v_bd, preferred_element_type=F32)
    o_st = lax.dot_general(qg, st.astype(BF16), (((1,), (1,)), ((), ())), preferred_element_type=F32)
    o_ref[s] = (o_in + o_st).astype(o_ref.dtype)
    upd = lax.dot_general(v, kw, (((0,), (0,)), ((), ())), preferred_element_type=F32)
    rs = lax.broadcasted_iota(jnp.int32, (GLA_V, GLA_QK), 0) // GLA_DV
    cs = lax.broadcasted_iota(jnp.int32, (GLA_V, GLA_QK), 1) // GLA_DK
    st_ref[s] = st * jnp.exp(total) + jnp.where(rs == cs, upd, 0.0)


def _gla_body(qf_ref, lf_ref, qb_ref, lb_ref, gw_ref, gb_ref, of_ref, ob_ref, stf, stb):
    @pl.when(pl.program_id(1) == 0)
    def _():
        stf[...] = jnp.zeros_like(stf)
        stb[...] = jnp.zeros_like(stb)

    for s in range(SCAN_SAMPLES):
        _gla_dir(qf_ref, lf_ref, gw_ref, gb_ref, stf, of_ref, s=s, d=0, reverse=False)
        _gla_dir(qb_ref, lb_ref, gw_ref, gb_ref, stb, ob_ref, s=s, d=1, reverse=True)


def gla_scan(qkv, lr, gate_w, gate_b):
    bsz = qkv.shape[0]
    nc = NTOK // GLA_CHUNK
    nlat = SEQ // GLA_CHUNK
    fwd_chunk = lambda s: (s + nlat) % nc
    bwd_chunk = lambda s: nc - 1 - s
    ns = SCAN_SAMPLES
    q_spec = lambda cm: pl.BlockSpec((ns, GLA_CHUNK, qkv.shape[-1]), lambda b, s: (b, cm(s), 0))
    l_spec = lambda cm: pl.BlockSpec((ns, GLA_CHUNK, 2 * GLA_RANK), lambda b, s: (b, cm(s), 0))
    o_spec = lambda cm: pl.BlockSpec((ns, GLA_CHUNK, GLA_V), lambda b, s: (b, cm(s), 0))
    o_shape = jax.ShapeDtypeStruct((bsz, NTOK, GLA_V), BF16)
    return pl.pallas_call(
        _gla_body,
        name='gla_scan',
        grid=(bsz // ns, nc),
        in_specs=[q_spec(fwd_chunk), l_spec(fwd_chunk), q_spec(bwd_chunk), l_spec(bwd_chunk),
                  pl.BlockSpec((2, GLA_RANK, GLA_QK), lambda b, s: (0, 0, 0)),
                  pl.BlockSpec((2, GLA_QK), lambda b, s: (0, 0))],
        out_specs=[o_spec(fwd_chunk), o_spec(bwd_chunk)],
        out_shape=[o_shape, o_shape],
        scratch_shapes=[pltpu.VMEM((ns, GLA_V, GLA_QK), F32), pltpu.VMEM((ns, GLA_V, GLA_QK), F32)],
        compiler_params=_cparams(("parallel", "arbitrary")),
    )(qkv, lr, qkv, lr, gate_w.astype(F32), gate_b.astype(F32))


def _gla_merge_body(of_ref, ob_ref, r_ref, nw_ref, o_ref):
    o = of_ref[0].astype(F32) + ob_ref[0].astype(F32)
    r = r_ref[0].astype(F32)
    for h in range(GLA_H):
        sl = slice(h * GLA_DV, (h + 1) * GLA_DV)
        oh = o[:, sl]
        ms = jnp.mean(oh * oh, axis=-1, keepdims=True)
        o_ref[0, :, sl] = (oh * lax.rsqrt(ms + NORM_EPS) * nw_ref[:, sl] * _silu(r[:, sl])).astype(o_ref.dtype)


def gla_merge_stream(of, ob, r, norm_w):
    bsz = of.shape[0]
    tile = pl.BlockSpec((1, ROW_TILE, GLA_V), lambda i, j: (i, j, 0))
    return pl.pallas_call(
        _gla_merge_body,
        name='gla_merge',
        grid=(bsz, N_LAT_TILES),
        in_specs=[tile, tile, tile, pl.BlockSpec((1, GLA_V), lambda i, j: (0, 0))],
        out_specs=tile,
        out_shape=jax.ShapeDtypeStruct((bsz, SEQ, GLA_V), BF16),
        compiler_params=_cparams(("parallel", "parallel")),
    )(of, ob, r, norm_w.reshape(1, GLA_V))


RG_TILE = 8


def _gelu_tanh(x):
    return 0.5 * x * (1.0 + jnp.tanh(math.sqrt(2.0 / math.pi) * (x + 0.044715 * x * x * x)))


def _rg_scan_block(a_s, x_s, h_s, base, carry, reverse):
    n_tiles = ROW_TILE // RG_TILE
    row = lax.broadcasted_iota(jnp.int32, (RG_TILE, D_RG), 0)

    def tile_step(i, h_prev):
        t = (n_tiles - 1 - i) if reverse else i
        r0 = pl.multiple_of(t * RG_TILE, RG_TILE)
        a = a_s[pl.ds(r0, RG_TILE), :]
        x = x_s[pl.ds(r0, RG_TILE), :]
        for s in (1, 2, 4):
            if reverse:
                ok = row < RG_TILE - s
                shift = RG_TILE - s
            else:
                ok = row >= s
                shift = s
            a_sh = jnp.where(ok, pltpu.roll(a, shift, 0), 1.0)
            x_sh = jnp.where(ok, pltpu.roll(x, shift, 0), 0.0)
            x = a * x_sh + x
            a = a * a_sh
        h = x + a * h_prev
        h_s[pl.ds(base + r0, RG_TILE), :] = h
        edge = 0 if reverse else RG_TILE - 1
        return jnp.broadcast_to(h[edge:edge + 1, :], (RG_TILE, D_RG))

    return lax.fori_loop(0, n_tiles, tile_step, carry)


def _rglru_body(u_ref, g_ref, w_ref, b_ref, c_ref, o_ref, hf_s, a_s, x_s, hb_s):
    n_blocks = NTOK // ROW_TILE
    fwd_order = list(range(N_LAT_TILES, n_blocks)) + list(range(N_LAT_TILES))
    bwd_order = list(range(n_blocks - 1, N_LAT_TILES - 1, -1)) + list(range(N_LAT_TILES - 1, -1, -1))

    def gates(blk, d):
        ub = u_ref[0, blk * ROW_TILE:(blk + 1) * ROW_TILE, :]
        z = jnp.dot(ub, w_ref[:, 2 * d * D_RG:2 * (d + 1) * D_RG], preferred_element_type=F32)
        z = z + b_ref[:, 2 * d * D_RG:2 * (d + 1) * D_RG]
        r = 1.0 / (1.0 + jnp.exp(-z[:, :D_RG]))
        i = 1.0 / (1.0 + jnp.exp(-z[:, D_RG:]))
        a = jnp.exp(c_ref[d:d + 1, :] * r)
        a_s[...] = a
        x_s[...] = jnp.sqrt(1.0 - a * a) * i * ub.astype(F32)

    carry = jnp.zeros((RG_TILE, D_RG), F32)
    for blk in fwd_order:
        gates(blk, 0)
        carry = _rg_scan_block(a_s, x_s, hf_s, blk * ROW_TILE, carry, reverse=False)
    carry = jnp.zeros((RG_TILE, D_RG), F32)
    for blk in bwd_order:
        gates(blk, 1)
        carry = _rg_scan_block(a_s, x_s, hb_s, 0, carry, reverse=True)
        rows = slice(blk * ROW_TILE, (blk + 1) * ROW_TILE)
        gate = g_ref[0, rows, :].astype(F32)
        o_ref[0, rows, :] = ((hf_s[rows, :] + hb_s[...]) * _gelu_tanh(gate)).astype(o_ref.dtype)


def rglru_stream(u, gate, w_a, b_a, w_x, b_x, lam):
    bsz = u.shape[0]
    eye = jnp.eye(RG_BLOCKS, dtype=F32)
    dense = lambda w: jnp.einsum('nio,nm->nimo', w, eye).reshape(D_RG, D_RG)
    w_cat = jnp.concatenate([dense(w_a[0]), dense(w_x[0]), dense(w_a[1]), dense(w_x[1])], axis=1).astype(BF16)
    b_cat = jnp.concatenate([b_a[0], b_x[0], b_a[1], b_x[1]]).astype(F32).reshape(1, 4 * D_RG)
    c = -RG_C * jax.nn.softplus(-lam.astype(F32))
    seq = pl.BlockSpec((1, NTOK, D_RG), lambda i: (i, 0, 0))
    return pl.pallas_call(
        _rglru_body,
        name='rglru',
        grid=(bsz,),
        in_specs=[seq, seq,
                  pl.BlockSpec((D_RG, 4 * D_RG), lambda i: (0, 0)),
                  pl.BlockSpec((1, 4 * D_RG), lambda i: (0, 0)),
                  pl.BlockSpec((2, D_RG), lambda i: (0, 0))],
        out_specs=seq,
        out_shape=jax.ShapeDtypeStruct((bsz, NTOK, D_RG), BF16),
        scratch_shapes=[pltpu.VMEM((NTOK, D_RG), F32), pltpu.VMEM((ROW_TILE, D_RG), F32),
                        pltpu.VMEM((ROW_TILE, D_RG), F32), pltpu.VMEM((ROW_TILE, D_RG), F32)],
        compiler_params=_cparams(("parallel",)),
    )(u, gate, w_cat, b_cat, c)


def odd_layer_mixer_pallas(xs, mods, norm_w, w_in, w_out, gla_args, rg_args):
    bsz = xs.shape[0]
    gate_w, gate_b, gla_norm_w = gla_args
    rg_conv_w, rg_conv_b, w_a, b_a, w_x, b_x, lam = rg_args
    h = norm_mod(xs, norm_w, mods, 0, 1)
    h = jnp.concatenate([to_col_major(h[:, :SEQ]), h[:, SEQ:]], axis=1)
    nqk, nv = GLA_QK, GLA_V
    r0 = 2 * nqk + nv + 2 * GLA_RANK
    w = jnp.concatenate([w_in[:, :2 * nqk + nv], w_in[:, r0:r0 + nv], w_in[:, GLA_IN:],
                         w_in[:, 2 * nqk + nv:r0]], axis=1).astype(BF16)
    qkv_w = 2 * nqk + nv
    splits = ((0, qkv_w), (qkv_w, nv), (qkv_w + nv, D_RG), (qkv_w + nv + D_RG, D_RG),
              (qkv_w + nv + 2 * D_RG, 2 * GLA_RANK))
    qkv, r, u_raw, gate, lr = mm_split(h.reshape(bsz * NTOK, D_MODEL), w, splits, (BF16, BF16, BF16, BF16, F32))
    to3 = lambda a: a.reshape(bsz, NTOK, a.shape[-1])
    of, ob = gla_scan(to3(qkv), to3(lr), gate_w, gate_b.reshape(2, GLA_QK))
    a_l = gla_merge_stream(of, ob, to3(r), gla_norm_w)
    u = dwconv_stream(to3(u_raw), rg_conv_w, rg_conv_b, act=False)
    r_l = rglru_stream(u, to3(gate), w_a, b_a, w_x, b_x, lam)[:, :SEQ]
    wo = w_out.astype(BF16)
    return mm_resid([from_col_major(a_l), from_col_major(r_l)], [wo[:GLA_V], wo[GLA_V:]], xs, mods, 2, N_LAT_TILES)


def to_col_major(x):
    b, n, d = x.shape
    rows = n // GRID_W
    return x.reshape(b, rows, GRID_W, d).transpose(0, 2, 1, 3).reshape(b, n, d)


def from_col_major(x):
    b, n, d = x.shape
    rows = n // GRID_W
    return x.reshape(b, GRID_W, rows, d).transpose(0, 2, 1, 3).reshape(b, n, d)


def kernel(x, c, ctx, c_ctx, ada_w, ada_b, norm1_w, norm2_w, ev_w_in, ev_w_out, hy_conv_w, hy_conv_b, hy_fw0, hy_fb0, hy_fw1, hy_fb1, hy_fw2, hy_fb2, hy_fw3, hy_freq, hy_bias, ssd_conv_w, ssd_conv_b, ssd_dt_bias, ssd_a_log, ssd_d, ssd_norm_w, od_w_in, od_w_out, gla_gate_w, gla_gate_b, gla_norm_w, rg_conv_w, rg_conv_b, rg_w_a, rg_b_a, rg_w_x, rg_b_x, rg_lambda, router_w, router_b, moe_w_gate, moe_w_up, moe_w_down, sh_w_gate, sh_w_up, sh_w_down, final_norm_w):
    xs = jnp.concatenate([x, ctx], axis=1)
    for i in range(DEPTH):
        last = i == DEPTH - 1
        j = i // 2
        mods = adaln_table(c, c_ctx, ada_w[i], ada_b[i])
        if i % 2 == 0:
            hy_filt = (hy_fw0[j], hy_fb0[j], hy_fw1[j], hy_fb1[j], hy_fw2[j], hy_fb2[j], hy_fw3[j], hy_freq[j])
            xs = even_layer_mixer(xs, mods, norm1_w[i], ev_w_in[j], ev_w_out[j], hy_conv_w[j], hy_conv_b[j],
                                  hy_filt, hy_bias[j], ssd_conv_w[j], ssd_conv_b[j], ssd_dt_bias[j],
                                  ssd_a_log[j], ssd_d[j], ssd_norm_w[j])
        else:
            gla_args = (gla_gate_w[j], gla_gate_b[j], gla_norm_w[j])
            rg_args = (rg_conv_w[j], rg_conv_b[j], rg_w_a[j], rg_b_a[j], rg_w_x[j], rg_b_x[j], rg_lambda[j])
            xs = odd_layer_mixer_pallas(xs, mods, norm1_w[i], od_w_in[j], od_w_out[j], gla_args, rg_args)
        n_tiles = N_LAT_TILES if last else N_ROW_TILES
        xs = moe_layer(xs, norm2_w[i], mods, router_w[i], router_b[i], moe_w_gate, moe_w_up, moe_w_down,
                       sh_w_gate[i], sh_w_up[i], sh_w_down[i], n_tiles, i)
    return final_norm(xs, final_norm_w)
```

```python
import functools
import math

import jax
import jax.numpy as jnp
from jax import lax
from jax.experimental import pallas as pl
from jax.experimental.pallas import tpu as pltpu

D_MODEL = 1024
BATCH = 16
SEQ = 2048
DEPTH = 2

CTX_LEN = 256
GRID_W = 64
NORM_EPS = 1e-6

D_HY = D_MODEL // 2
HY_ORDER = 2
HY_SHORT = 3
HY_BANDS = 16
HY_EMB = 1 + 2 * HY_BANDS
HY_FF = 64
HY_DECAY_PCT_LO = 0.3
HY_DECAY_PCT_HI = 1.5
HY_DECAY_TARGET = 1e-2
HY_IN = 3 * D_HY

D_SSM = D_MODEL // 2
SSD_P = 64
SSD_H = D_SSM // SSD_P
SSD_G = 2
SSD_HPG = SSD_H // SSD_G
SSD_N = 128
SSD_CONV = 4
SSD_CHUNK = 128
SSD_XBC = D_SSM + 2 * SSD_G * SSD_N
SSD_IN = D_SSM + SSD_XBC + 2 * SSD_H
EV_IN = HY_IN + SSD_IN
EV_MIX = D_HY + D_SSM

GLA_H = 4
GLA_DV = (D_MODEL // 2) // GLA_H
GLA_DK = GLA_DV // 2
GLA_RANK = 16
GLA_GATE_NORM = 16.0
GLA_CHUNK = 64
GLA_IN = 2 * GLA_H * GLA_DK + 2 * GLA_H * GLA_DV + 2 * GLA_RANK

D_RG = D_MODEL // 2
RG_BLOCKS = 8
RG_BW = D_RG // RG_BLOCKS
RG_CONV = 4
RG_C = 8.0
RG_IN = 2 * D_RG
OD_IN = GLA_IN + RG_IN
OD_MIX = GLA_H * GLA_DV + D_RG

MOE_EXPERTS = 64
MOE_TOPK = 8
MOE_D_EXPERT = 256
MOE_D_SHARED = 256
MOE_SCALE = 2.5
MOE_BLOCK = 512

F32 = jnp.float32
BF16 = jnp.bfloat16
HIGHEST = lax.Precision.HIGHEST

NTOK = SEQ + CTX_LEN
ROW_TILE = 256
N_ROW_TILES = NTOK // ROW_TILE
N_LAT_TILES = SEQ // ROW_TILE

VMEM_LIMIT = 48 * 1024 * 1024
VMEM_LIMIT_BIG = 56 * 1024 * 1024


def _cparams(sem, limit=VMEM_LIMIT):
    return pltpu.CompilerParams(dimension_semantics=sem, vmem_limit_bytes=limit)


def _pick_tile(n, pref):
    t = min(n, pref)
    while n % t:
        t //= 2
    return t


def _silu(x):
    return x / (1.0 + jnp.exp(-x))


def _softplus(x):
    return jnp.maximum(x, 0.0) + jnp.log(1.0 + jnp.exp(-jnp.abs(x)))


def _mm_bf16_body(a_ref, w_ref, o_ref):
    o_ref[...] = jnp.dot(a_ref[...].astype(BF16), w_ref[...].astype(BF16),
                         preferred_element_type=F32).astype(o_ref.dtype)


def _mm_f32_body(a_ref, w_ref, o_ref):
    o_ref[...] = jnp.dot(a_ref[...], w_ref[...], preferred_element_type=F32,
                         precision=HIGHEST).astype(o_ref.dtype)


def pmatmul(a, w, *, exact=False, out_dtype=F32, tm=512, tn=None):
    m, k = a.shape
    n = w.shape[1]
    tm = _pick_tile(m, tm)
    tn = n if tn is None else _pick_tile(n, tn)
    body = _mm_f32_body if exact else _mm_bf16_body
    return pl.pallas_call(
        body,
        name='mm',
        grid=(m // tm, n // tn),
        in_specs=[pl.BlockSpec((tm, k), lambda i, j: (i, 0)),
                  pl.BlockSpec((k, tn), lambda i, j: (0, j))],
        out_specs=pl.BlockSpec((tm, tn), lambda i, j: (i, j)),
        out_shape=jax.ShapeDtypeStruct((m, n), out_dtype),
        compiler_params=_cparams(("parallel", "parallel")),
    )(a, w)


def _mm_split_body(a_ref, w_ref, *o_refs, splits):
    a = a_ref[...]
    for o_ref, (start, width) in zip(o_refs, splits):
        o_ref[...] = jnp.dot(a, w_ref[:, start:start + width],
                             preferred_element_type=F32).astype(o_ref.dtype)


def mm_split(a, w, splits, dtypes, tm=512):
    m, k = a.shape
    n = w.shape[1]
    tm = _pick_tile(m, tm)
    return pl.pallas_call(
        functools.partial(_mm_split_body, splits=tuple(splits)),
        name='mm_split',
        grid=(m // tm,),
        in_specs=[pl.BlockSpec((tm, k), lambda i: (i, 0)),
                  pl.BlockSpec((k, n), lambda i: (0, 0))],
        out_specs=[pl.BlockSpec((tm, wd), lambda i: (i, 0)) for _, wd in splits],
        out_shape=[jax.ShapeDtypeStruct((m, wd), dt) for (_, wd), dt in zip(splits, dtypes)],
        compiler_params=_cparams(("parallel",)),
    )(a, w)


MIX_SAMPLES = 4


def _stack_samples(ref):
    return jnp.concatenate([ref[s] for s in range(MIX_SAMPLES)], axis=0)


def _resid_store(o_ref, x_ref, g_ref, acc):
    for s in range(MIX_SAMPLES):
        o_ref[s] = x_ref[s] + g_ref[s] * acc[s * ROW_TILE:(s + 1) * ROW_TILE]


def _mm_resid_body(*refs, n_pairs):
    a_refs = refs[:n_pairs]
    w_refs = refs[n_pairs:2 * n_pairs]
    x_ref, g_ref, o_ref = refs[2 * n_pairs:]
    acc = jnp.dot(_stack_samples(a_refs[0]), w_refs[0][...], preferred_element_type=F32)
    for a_ref, w_ref in zip(a_refs[1:], w_refs[1:]):
        acc = acc + jnp.dot(_stack_samples(a_ref), w_ref[...], preferred_element_type=F32)
    _resid_store(o_ref, x_ref, g_ref, acc)


def _gate_spec(gate_idx, d):
    return pl.BlockSpec((MIX_SAMPLES, None, None, 1, d), lambda i, j: (i, 1 - j // N_LAT_TILES, gate_idx, 0, 0))


def _mix_tile(width):
    return pl.BlockSpec((MIX_SAMPLES, ROW_TILE, width), lambda i, j: (i, j, 0))


def mm_resid(a_list, w_list, xs, mods, gate_idx, n_tiles):
    b, nt, d = xs.shape
    n_pairs = len(a_list)
    in_specs = [_mix_tile(a.shape[-1]) for a in a_list]
    in_specs += [pl.BlockSpec(w.shape, lambda i, j: (0, 0)) for w in w_list]
    in_specs += [_mix_tile(d), _gate_spec(gate_idx, d)]
    return pl.pallas_call(
        functools.partial(_mm_resid_body, n_pairs=n_pairs),
        name='mm_resid',
        grid=(b // MIX_SAMPLES, n_tiles),
        in_specs=in_specs,
        out_specs=_mix_tile(d),
        out_shape=jax.ShapeDtypeStruct(xs.shape, F32),
        input_output_aliases={2 * n_pairs: 0},
        compiler_params=_cparams(("parallel", "parallel")),
    )(*a_list, *w_list, xs, mods)


def _even_out_body(hy_ref, yf_ref, yb_ref, xs_ref, z_ref, d_ref, nw_ref, w1_ref, w2_ref, x_ref, g_ref, o_ref):
    gw = D_SSM // SSD_G
    merged = []
    for s in range(MIX_SAMPLES):
        y = yf_ref[s].astype(F32) + yb_ref[s].astype(F32) + d_ref[...] * xs_ref[s].astype(F32)
        g = y * _silu(z_ref[s].astype(F32))
        groups = []
        for k in range(SSD_G):
            gk = g[:, k * gw:(k + 1) * gw]
            ms = jnp.mean(gk * gk, axis=-1, keepdims=True)
            groups.append(gk * lax.rsqrt(ms + NORM_EPS) * nw_ref[:, k * gw:(k + 1) * gw])
        merged.append(jnp.concatenate(groups, axis=1).astype(BF16))
    acc = (jnp.dot(_stack_samples(hy_ref), w1_ref[...], preferred_element_type=F32)
           + jnp.dot(jnp.concatenate(merged, axis=0), w2_ref[...], preferred_element_type=F32))
    _resid_store(o_ref, x_ref, g_ref, acc)


def even_out_resid(y_hy, yf, yb, xbc, z, d_skip, norm_w, w_hy, w_ssd, xs, mods):
    b, nt, d = xs.shape
    d_chan = jnp.repeat(d_skip.astype(F32), SSD_P).reshape(1, D_SSM)
    vec = pl.BlockSpec((1, D_SSM), lambda i, j: (0, 0))
    wspec = pl.BlockSpec((D_SSM, d), lambda i, j: (0, 0))
    half = _mix_tile(D_SSM)
    return pl.pallas_call(
        _even_out_body,
        name='even_out',
        grid=(b // MIX_SAMPLES, N_ROW_TILES),
        in_specs=[half, half, half, half, half, vec, vec, wspec, wspec, _mix_tile(d), _gate_spec(2, d)],
        out_specs=_mix_tile(d),
        out_shape=jax.ShapeDtypeStruct(xs.shape, F32),
        input_output_aliases={9: 0},
        compiler_params=_cparams(("parallel", "parallel")),
    )(y_hy, yf, yb, xbc, z, d_chan, norm_w.reshape(1, D_SSM), w_hy, w_ssd, xs, mods)


def _mod_spec(idx, d, b0=0):
    return pl.BlockSpec((None, None, None, 1, d), lambda i, j: (i + b0, 1 - j // N_LAT_TILES, idx, 0, 0))


def adaln_table(c, c_ctx, w, b):
    cv = jax.nn.silu(jnp.concatenate([c, c_ctx[None, :]], axis=0))
    cv = jnp.pad(cv, ((0, 24 - cv.shape[0]), (0, 0)))
    m = pmatmul(cv, w, exact=True, tn=1536)[:BATCH + 1] + b
    per_sample = m[:BATCH]
    ctx_row = jnp.broadcast_to(m[BATCH][None, :], per_sample.shape)
    return jnp.stack([ctx_row, per_sample], axis=1).reshape(BATCH, 2, 6, 1, D_MODEL)


def _norm_mod(x, w, shift, scale):
    ms = jnp.mean(x * x, axis=-1, keepdims=True)
    return (x * lax.rsqrt(ms + NORM_EPS) * w) * (1.0 + scale) + shift


def _norm_mod_body(x_ref, w_ref, sh_ref, sc_ref, o_ref):
    o_ref[0] = _norm_mod(x_ref[0], w_ref[...], sh_ref[...], sc_ref[...]).astype(o_ref.dtype)


def norm_mod(xs, w, mods, shift_idx, scale_idx):
    b, nt, d = xs.shape
    return pl.pallas_call(
        _norm_mod_body,
        name='norm_mod',
        grid=(b, nt // ROW_TILE),
        in_specs=[pl.BlockSpec((1, ROW_TILE, d), lambda i, j: (i, j, 0)),
                  pl.BlockSpec((1, d), lambda i, j: (0, 0)),
                  _mod_spec(shift_idx, d), _mod_spec(scale_idx, d)],
        out_specs=pl.BlockSpec((1, ROW_TILE, d), lambda i, j: (i, j, 0)),
        out_shape=jax.ShapeDtypeStruct(xs.shape, BF16),
        compiler_params=_cparams(("parallel", "parallel")),
    )(xs, w.reshape(1, d), mods, mods)


def _norm_mm_split_body(x_ref, nw_ref, sh_ref, sc_ref, w_ref, *o_refs, splits):
    h = jnp.concatenate([_norm_mod(x_ref[s], nw_ref[...], sh_ref[s], sc_ref[s]).astype(BF16)
                         for s in range(MIX_SAMPLES)], axis=0)
    for o_ref, (start, width) in zip(o_refs, splits):
        y = jnp.dot(h, w_ref[:, start:start + width], preferred_element_type=F32).astype(o_ref.dtype)
        for s in range(MIX_SAMPLES):
            o_ref[s] = y[s * ROW_TILE:(s + 1) * ROW_TILE]


def norm_mm_split(xs, norm_w, mods, w, splits, dtypes):
    b, nt, d = xs.shape
    n = w.shape[1]
    return pl.pallas_call(
        functools.partial(_norm_mm_split_body, splits=tuple(splits)),
        name='norm_mm_split',
        grid=(b // MIX_SAMPLES, nt // ROW_TILE),
        in_specs=[_mix_tile(d), pl.BlockSpec((1, d), lambda i, j: (0, 0)), _gate_spec(0, d), _gate_spec(1, d),
                  pl.BlockSpec((d, n), lambda i, j: (0, 0))],
        out_specs=[_mix_tile(wd) for _, wd in splits],
        out_shape=[jax.ShapeDtypeStruct((b, nt, wd), dt) for (_, wd), dt in zip(splits, dtypes)],
        compiler_params=_cparams(("parallel", "parallel")),
    )(xs, norm_w.reshape(1, d), mods, mods, w)


def _final_norm_body(x_ref, w_ref, o_ref):
    x = x_ref[0]
    ms = jnp.mean(x * x, axis=-1, keepdims=True)
    o_ref[0] = x * lax.rsqrt(ms + NORM_EPS) * w_ref[...]


def final_norm(xs, w):
    b, _, d = xs.shape
    return pl.pallas_call(
        _final_norm_body,
        name='final_norm',
        grid=(b, N_LAT_TILES),
        in_specs=[pl.BlockSpec((1, ROW_TILE, d), lambda i, j: (i, j, 0)),
                  pl.BlockSpec((1, d), lambda i, j: (0, 0))],
        out_specs=pl.BlockSpec((1, ROW_TILE, d), lambda i, j: (i, j, 0)),
        out_shape=jax.ShapeDtypeStruct((b, SEQ, d), F32),
        compiler_params=_cparams(("parallel", "parallel")),
    )(xs, w.reshape(1, d))


def _route_t_body(x_ref, w_ref, sh_ref, sc_ref, rwh_ref, rwl_ref, rb_ref, h_ref, idx_ref, wsel_ref, rank_ref, cnt_ref,
                  *, group_size):
    first = (pl.program_id(0) % group_size == 0) & (pl.program_id(1) == 0)

    @pl.when(first)
    def _():
        cnt_ref[...] = jnp.zeros_like(cnt_ref)

    h = _norm_mod(x_ref[0], w_ref[...], sh_ref[...], sc_ref[...])
    h_hi = h.astype(BF16)
    h_ref[0] = h_hi
    h_lo = (h - h_hi.astype(F32)).astype(BF16)
    nt = (((1,), (1,)), ((), ()))
    logits = (lax.dot_general(rwh_ref[...], h_hi, nt, preferred_element_type=F32)
              + lax.dot_general(rwh_ref[...], h_lo, nt, preferred_element_type=F32)
              + lax.dot_general(rwl_ref[...], h_hi, nt, preferred_element_type=F32))
    scores = 1.0 / (1.0 + jnp.exp(-logits))
    ne, tm = scores.shape
    expert = lax.broadcasted_iota(jnp.int32, (ne, tm), 0).astype(F32)
    slot = lax.broadcasted_iota(jnp.int32, (MOE_TOPK, tm), 0)
    sel = scores + rb_ref[...]
    picked = jnp.zeros((ne, tm), F32)
    hits = []
    idx_out = jnp.zeros((MOE_TOPK, tm), F32)
    w_out = jnp.zeros((MOE_TOPK, tm), F32)
    for k in range(MOE_TOPK):
        m = jnp.max(sel, axis=0, keepdims=True)
        ik = jnp.min(jnp.where(sel == m, expert, float(ne)), axis=0, keepdims=True)
        hit = expert == ik
        wk = jnp.sum(jnp.where(hit, scores, 0.0), axis=0, keepdims=True)
        sel = jnp.where(hit, -jnp.inf, sel)
        picked = picked + hit.astype(F32)
        hits.append(hit)
        idx_out = jnp.where(slot == k, ik, idx_out)
        w_out = jnp.where(slot == k, wk, w_out)
    wsum = jnp.sum(w_out, axis=0, keepdims=True)
    wsel_ref[0] = w_out / wsum * MOE_SCALE
    idx_ref[0] = idx_out.astype(jnp.int32)
    r_i = lax.broadcasted_iota(jnp.int32, (tm, tm), 0)
    c_i = lax.broadcasted_iota(jnp.int32, (tm, tm), 1)
    earlier = (r_i < c_i).astype(BF16)
    before = jnp.dot(picked.astype(BF16), earlier, preferred_element_type=F32) + cnt_ref[...]
    rank_out = jnp.zeros((MOE_TOPK, tm), F32)
    for k in range(MOE_TOPK):
        rk = jnp.sum(jnp.where(hits[k], before, 0.0), axis=0, keepdims=True)
        rank_out = jnp.where(slot == k, rk, rank_out)
    rank_ref[0] = rank_out.astype(jnp.int32)
    cnt_ref[...] = cnt_ref[...] + jnp.sum(picked, axis=1, keepdims=True)


def route_t(xs, w, mods, router_w, router_b, n_tiles, group_size):
    b, _, d = xs.shape
    rows = n_tiles * ROW_TILE
    rwt = router_w.T.astype(F32)
    rwt_hi, rwt_lo = _split_bf16(rwt)
    small = lambda dt: jax.ShapeDtypeStruct((b, MOE_TOPK, rows), dt)
    small_spec = pl.BlockSpec((1, MOE_TOPK, ROW_TILE), lambda i, j: (i, 0, j))
    return pl.pallas_call(
        functools.partial(_route_t_body, group_size=group_size),
        name='route',
        grid=(b, n_tiles),
        in_specs=[pl.BlockSpec((1, ROW_TILE, d), lambda i, j: (i, j, 0)),
                  pl.BlockSpec((1, d), lambda i, j: (0, 0)),
                  _mod_spec(3, d), _mod_spec(4, d),
                  pl.BlockSpec((MOE_EXPERTS, d), lambda i, j: (0, 0)),
                  pl.BlockSpec((MOE_EXPERTS, d), lambda i, j: (0, 0)),
                  pl.BlockSpec((MOE_EXPERTS, 1), lambda i, j: (0, 0))],
        out_specs=[pl.BlockSpec((1, ROW_TILE, d), lambda i, j: (i, j, 0)),
                   small_spec, small_spec, small_spec,
                   pl.BlockSpec((None, MOE_EXPERTS, 1), lambda i, j: (i // group_size, 0, 0))],
        out_shape=[jax.ShapeDtypeStruct((b, rows, d), BF16), small(jnp.int32), small(F32), small(jnp.int32),
                   jax.ShapeDtypeStruct((b // group_size, MOE_EXPERTS, 1), F32)],
        compiler_params=_cparams(("arbitrary", "arbitrary")),
    )(xs, w.reshape(1, d), mods, mods, rwt_hi, rwt_lo, router_b.astype(F32).reshape(MOE_EXPERTS, 1))


def _swiglu(x, wg, wu, wd):
    g = jnp.dot(x, wg, preferred_element_type=F32)
    u = jnp.dot(x, wu, preferred_element_type=F32)
    h = (_silu(g) * u).astype(BF16)
    return jnp.dot(h, wd, preferred_element_type=F32)


def _expert_body(be_ref, nu_ref, xa_ref, xb_ref, wg_ref, wu_ref, wd_ref, o_ref, wg_s, wu_s, wd_s):
    i = pl.program_id(0)
    used = i < nu_ref[0]

    @pl.when(used & ((i == 0) | (be_ref[i] != be_ref[jnp.maximum(i - 1, 0)])))
    def _():
        wg_s[...] = wg_ref[0].astype(BF16)
        wu_s[...] = wu_ref[0].astype(BF16)
        wd_s[...] = wd_ref[0].astype(BF16)

    @pl.when(used)
    def _():
        x = jnp.concatenate([xa_ref[...], xb_ref[...]], axis=1)
        o_ref[...] = _swiglu(x, wg_s[...], wu_s[...], wd_s[...]).astype(o_ref.dtype)

    @pl.when(jnp.logical_not(used))
    def _():
        o_ref[...] = jnp.zeros_like(o_ref)


def moe_experts(x_rows, block_e, n_used, wg, wu, wd, layer):
    rows, d = x_rows.shape
    n_blocks = rows // MOE_BLOCK
    f = wg.shape[-1]
    grid_spec = pltpu.PrefetchScalarGridSpec(
        num_scalar_prefetch=2,
        grid=(n_blocks,),
        in_specs=[
            pl.BlockSpec((MOE_BLOCK, d // 2), lambda i, be, nu: (i, 0)),
            pl.BlockSpec((MOE_BLOCK, d // 2), lambda i, be, nu: (i, 1)),
            pl.BlockSpec((None, 1, d, f), lambda i, be, nu: (layer, be[i], 0, 0)),
            pl.BlockSpec((None, 1, d, f), lambda i, be, nu: (layer, be[i], 0, 0)),
            pl.BlockSpec((None, 1, f, d), lambda i, be, nu: (layer, be[i], 0, 0)),
        ],
        out_specs=pl.BlockSpec((MOE_BLOCK, d), lambda i, be, nu: (i, 0)),
        scratch_shapes=[pltpu.VMEM((d, f), BF16), pltpu.VMEM((d, f), BF16), pltpu.VMEM((f, d), BF16)],
    )
    return pl.pallas_call(
        _expert_body,
        name='experts',
        grid_spec=grid_spec,
        out_shape=jax.ShapeDtypeStruct((rows, d), BF16),
        compiler_params=_cparams(("arbitrary",)),
    )(block_e, n_used, x_rows, x_rows, wg, wu, wd)


COMBINE_SAMPLES = 2


def _shared_resid_body(h_ref, wg_ref, wu_ref, wd_ref, pk_ref, ws_ref, x_ref, g_ref, o_ref):
    for s in range(COMBINE_SAMPLES):
        y = _swiglu(h_ref[s], wg_ref[...], wu_ref[...], wd_ref[...])
        ws = ws_ref[s]
        for k in range(MOE_TOPK):
            y = y + ws[:, k:k + 1] * pk_ref[k, s].astype(F32)
        o_ref[s] = x_ref[s] + g_ref[s] * y


def shared_resid(h, picked, wsel, xs, mods, wg, wu, wd, n_tiles, b0):
    b, _, d = h.shape
    f = wg.shape[-1]
    ns = COMBINE_SAMPLES
    s0 = b0 // ns
    tile = pl.BlockSpec((ns, ROW_TILE, d), lambda i, j: (i, j, 0))
    xs_tile = pl.BlockSpec((ns, ROW_TILE, d), lambda i, j: (i + s0, j, 0))
    gate = pl.BlockSpec((ns, None, None, 1, d), lambda i, j: (i + s0, 1 - j // N_LAT_TILES, 5, 0, 0))
    return pl.pallas_call(
        _shared_resid_body,
        name='shared_resid',
        grid=(b // ns, n_tiles),
        in_specs=[tile,
                  pl.BlockSpec((d, f), lambda i, j: (0, 0)),
                  pl.BlockSpec((d, f), lambda i, j: (0, 0)),
                  pl.BlockSpec((f, d), lambda i, j: (0, 0)),
                  pl.BlockSpec((MOE_TOPK, ns, ROW_TILE, d), lambda i, j: (0, i, j, 0)),
                  pl.BlockSpec((ns, ROW_TILE, MOE_TOPK), lambda i, j: (i, j, 0)),
                  xs_tile, gate],
        out_specs=xs_tile,
        out_shape=jax.ShapeDtypeStruct(xs.shape, F32),
        input_output_aliases={6: 0},
        compiler_params=_cparams(("parallel", "parallel")),
    )(h, wg, wu, wd, picked, wsel, xs, mods)


MOE_GROUPS = 2


def moe_layer(xs, norm_w, mods, router_w, router_b, w_gate, w_up, w_down, sh_gate, sh_up, sh_down, n_tiles, layer):
    bsz, _, d = xs.shape
    shared_w = (sh_gate.astype(BF16), sh_up.astype(BF16), sh_down.astype(BF16))
    b = bsz // MOE_GROUPS
    h, idx, wsel, rank, counts = route_t(xs, norm_w, mods, router_w, router_b, n_tiles, b)
    h_flat = h.reshape(-1, d)
    wsel = jnp.swapaxes(wsel, 1, 2)
    for g in range(MOE_GROUPS):
        sl = slice(g * b, (g + 1) * b)
        xs = _moe_group(xs, mods, h_flat, h[sl], idx[sl], wsel[sl], rank[sl], counts[g, :, 0], w_gate, w_up, w_down,
                        shared_w, n_tiles, layer, g * b)
    return xs


def _moe_group(xs, mods, h_flat, h, idx, wsel, rank, counts, w_gate, w_up, w_down, shared_w, n_tiles, layer, b0):
    b, rows_per_sample, d = h.shape
    n = b * rows_per_sample
    counts = counts.astype(jnp.int32)
    padded = (counts + MOE_BLOCK - 1) // MOE_BLOCK * MOE_BLOCK
    ends = jnp.cumsum(padded)
    starts = ends - padded
    nk = n * MOE_TOPK
    n_blocks = -(-nk // MOE_BLOCK) + MOE_EXPERTS
    rows = n_blocks * MOE_BLOCK
    n_pad = rows - nk
    e_iota = jnp.arange(MOE_EXPERTS, dtype=jnp.int32)
    dest = jnp.sum(jnp.where(idx[..., None] == e_iota, starts, 0), axis=-1) + rank
    blk_start = jnp.arange(n_blocks, dtype=jnp.int32) * MOE_BLOCK
    block_e = jnp.minimum(jnp.sum(ends[None, :] <= blk_start[:, None], axis=1), MOE_EXPERTS - 1).astype(jnp.int32)
    n_used = (ends[-1:] // MOE_BLOCK).astype(jnp.int32)
    pad = padded - counts
    cum_pad = jnp.cumsum(pad)
    m = jnp.arange(n_pad, dtype=jnp.int32)
    e_m = jnp.sum(cum_pad[None, :] <= m[:, None], axis=1)
    base = jnp.sum(jnp.where(jnp.minimum(e_m, MOE_EXPERTS - 1)[:, None] == e_iota,
                             starts + counts - (cum_pad - pad), 0), axis=1)
    pad_row = jnp.where(e_m < MOE_EXPERTS, base + m, ends[-1] + m - cum_pad[-1])
    tok0 = b0 * rows_per_sample
    tok = (tok0 + jnp.arange(b, dtype=jnp.int32)[:, None, None] * rows_per_sample
           + jnp.arange(rows_per_sample, dtype=jnp.int32)[None, None, :])
    tok = jnp.broadcast_to(tok, dest.shape).reshape(-1)
    _, row_tok = lax.sort((jnp.concatenate([dest.reshape(-1), pad_row]).astype(jnp.int32),
                           jnp.concatenate([tok, tok0 + m % n])), num_keys=1)
    x_rows = h_flat[row_tok]
    y_rows = moe_experts(x_rows, block_e, n_used, w_gate, w_up, w_down, layer)
    picked = y_rows[jnp.swapaxes(dest, 0, 1)]
    return shared_resid(h, picked, wsel, xs, mods, *shared_w, n_tiles, b0)


def _dwconv_body(x_ref, w_ref, b_ref, o_ref, *, width, act):
    chunk = ROW_TILE
    n_chunks = NTOK // chunk
    first_of_seq = (0, N_LAT_TILES)
    last_of_seq = (N_LAT_TILES - 1, n_chunks - 1)
    tc = x_ref.shape[-1]
    halo = 16
    row = lax.broadcasted_iota(jnp.int32, (chunk, tc), 0)
    zero_row = jnp.zeros((1, tc), F32)
    for c in range(n_chunks):
        r0 = c * chunk
        cur = x_ref[0, r0:r0 + chunk, :].astype(F32)
        if c in first_of_seq:
            prev_last = zero_row
        else:
            prev_last = x_ref[0, r0 - halo:r0, :].astype(F32)[halo - 1:halo, :]
        if c in last_of_seq:
            next0 = next1 = zero_row
        else:
            nxt = x_ref[0, r0 + chunk:r0 + chunk + halo, :].astype(F32)
            next0, next1 = nxt[0:1, :], nxt[1:2, :]
        xm1 = jnp.where(row == 0, prev_last, pltpu.roll(cur, 1, 0))
        xp1 = jnp.where(row == chunk - 1, next0, pltpu.roll(cur, chunk - 1, 0))
        y = w_ref[0:1, :] * xm1 + w_ref[1:2, :] * cur + w_ref[2:3, :] * xp1 + b_ref[...]
        if width == 4:
            xp2 = jnp.where(row == chunk - 2, next0,
                            jnp.where(row == chunk - 1, next1, pltpu.roll(cur, chunk - 2, 0)))
            y = y + w_ref[3:4, :] * xp2
        if act:
            y = _silu(y)
        o_ref[0, c * chunk:(c + 1) * chunk, :] = y.astype(o_ref.dtype)


def dwconv_stream(x, w, b, act, tc=256):
    bsz, nt, c = x.shape
    width = w.shape[0]
    return pl.pallas_call(
        functools.partial(_dwconv_body, width=width, act=act),
        name='dwconv',
        grid=(bsz, c // tc),
        in_specs=[pl.BlockSpec((1, nt, tc), lambda i, j: (i, 0, j)),
                  pl.BlockSpec((width, tc), lambda i, j: (0, j)),
                  pl.BlockSpec((1, tc), lambda i, j: (0, j))],
        out_specs=pl.BlockSpec((1, nt, tc), lambda i, j: (i, 0, j)),
        out_shape=jax.ShapeDtypeStruct(x.shape, BF16),
        compiler_params=_cparams(("parallel", "parallel")),
    )(x, w, b.reshape(1, c))


HY_FB = 512
DFT_SPLIT = 64


def dft_matrices(n):
    t = jnp.arange(n, dtype=jnp.int32)[None, :]
    ka = jnp.arange(DFT_SPLIT, dtype=jnp.int32)[:, None]
    kb = jnp.arange(n // DFT_SPLIT, dtype=jnp.int32)[:, None] * DFT_SPLIT
    ang_a = (2.0 * math.pi / (2 * n)) * ((ka * t) % (2 * n)).astype(F32)
    ang_b = (2.0 * math.pi / (2 * n)) * ((kb * t) % (2 * n)).astype(F32)
    ca, sa = jnp.cos(ang_a)[None], jnp.sin(ang_a)[None]
    cb, sb = jnp.cos(ang_b)[:, None], jnp.sin(ang_b)[:, None]
    cos_kt = (ca * cb - sa * sb).reshape(n, n)
    sin_kt = (sa * cb + ca * sb).reshape(n, n)
    idx = jnp.arange(n, dtype=jnp.int32)
    nyq = jnp.where(idx % 2 == 0, 1.0, -1.0).astype(F32)
    fwd = jnp.concatenate([cos_kt, jnp.where(idx[:, None] == 0, nyq[None, :], -sin_kt)], axis=0)
    scale = jnp.where(idx == 0, 0.5, 1.0)[None, :] / n
    inv = jnp.concatenate([cos_kt * scale, jnp.where(idx[None, :] == 0, nyq[:, None], -sin_kt) * scale], axis=1)
    return fwd.astype(BF16), inv.astype(BF16)


def hyena_filter_taps(n, fw0, fb0, fw1, fb1, fw2, fb2, fw3, freq):
    pos = jnp.arange(n, dtype=F32)
    t = pos / max(n - 1, 1)
    bands = jnp.linspace(1e-4, HY_BANDS - 1, HY_BANDS, dtype=F32)
    ang = (2.0 * math.pi / n) * pos[:, None] * bands[None, :]
    feats = jnp.concatenate([t[:, None], jnp.cos(ang), -jnp.sin(ang)], axis=-1)
    h = jnp.sin(freq * (jnp.dot(feats, fw0, precision=HIGHEST) + fb0))
    h = jnp.sin(freq * (jnp.dot(h, fw1, precision=HIGHEST) + fb1))
    h = jnp.sin(freq * (jnp.dot(h, fw2, precision=HIGHEST) + fb2))
    h = pmatmul(h, fw3, exact=True).reshape(n, 2, HY_ORDER, D_HY)
    deltas = jnp.abs(jnp.linspace(math.log(HY_DECAY_PCT_LO) / HY_DECAY_TARGET,
                                  math.log(HY_DECAY_PCT_HI) / HY_DECAY_TARGET, D_HY, dtype=F32))
    h = h * jnp.exp(-t[:, None] * deltas)[:, None, None, :]
    h0 = h[:, 0]
    h1 = h[:, 1].at[0].set(0.0)
    norm = jnp.sum(jnp.abs(h0), axis=0, keepdims=True) + jnp.sum(jnp.abs(h1), axis=0, keepdims=True)
    h0 = (h0 / norm).reshape(n, HY_ORDER * D_HY)
    h1 = (h1 / norm).reshape(n, HY_ORDER * D_HY)
    return h0 + h1, h0 - h1


def _split_bf16(a):
    hi = a.astype(BF16)
    return hi, (a - hi.astype(F32)).astype(BF16)


def hyena_spectrum(fwd, hsum, hdiff, fb):
    n = hsum.shape[0]
    a = pmatmul(fwd, hsum.astype(BF16))
    bm = pmatmul(fwd, hdiff.astype(BF16))
    sr = a[:n]
    si = bm[n:]
    nyq = a[n]
    first = (jnp.arange(n) == 0)[:, None]
    p = sr
    q = jnp.where(first, 0.0, si)
    s = jnp.where(first, nyq[None, :], sr)
    spec = jnp.stack([p, q, s], axis=0).reshape(3, n // fb, fb, HY_ORDER, D_HY)
    return spec.transpose(3, 1, 0, 2, 4)


def _hyena_body(u_ref, fre_ref, fim_ref, gre_ref, gim_ref, sp_ref, bias_ref, prev_ref, o_ref,
                vin, acc, *, nf):
    del prev_ref
    o = pl.program_id(1)
    f = pl.program_id(2)
    c = D_HY

    @pl.when((o == 0) & (f == 0))
    def _():
        vin[...] = u_ref[0, :, 0:c]

    @pl.when(f == 0)
    def _():
        acc[...] = jnp.zeros_like(acc)

    v = vin[...]
    vr = jnp.dot(fre_ref[...], v, preferred_element_type=F32)
    vi = jnp.dot(fim_ref[...], v, preferred_element_type=F32)
    p, q, s = sp_ref[0], sp_ref[1], sp_ref[2]
    zr = (vr * p - vi * q).astype(BF16)
    zi = (vr * q + vi * s).astype(BF16)
    acc[...] += (jnp.dot(gre_ref[...], zr, preferred_element_type=F32)
                 + jnp.dot(gim_ref[...], zi, preferred_element_type=F32))

    @pl.when((o == 0) & (f == nf - 1))
    def _():
        z = u_ref[0, :, c:2 * c].astype(F32) * (acc[...] + bias_ref[0:1, :] * vin[...].astype(F32))
        vin[...] = z.astype(BF16)

    @pl.when((o == 1) & (f == nf - 1))
    def _():
        y = u_ref[0, :, 2 * c:3 * c].astype(F32) * (acc[...] + bias_ref[1:2, :] * vin[...].astype(F32))
        o_ref[0] = y.astype(o_ref.dtype)


def hyena_long_conv(u, fwd_bf16, inv_bf16, spec, bias, n, row_block, prev_out):
    bsz = u.shape[0]
    fb = spec.shape[3]
    nf = n // fb
    out_shape = jax.ShapeDtypeStruct((bsz, NTOK, D_HY), BF16)
    if prev_out is None:
        prev_out = jnp.zeros(out_shape.shape, BF16)
    args = [u, fwd_bf16, fwd_bf16, inv_bf16, inv_bf16, spec, bias, prev_out]
    aliases = {7: 0}
    return pl.pallas_call(
        functools.partial(_hyena_body, nf=nf),
        name='hyena',
        grid=(bsz, HY_ORDER, nf),
        in_specs=[pl.BlockSpec((1, n, 3 * D_HY), lambda b, o, f: (b, row_block, 0)),
                  pl.BlockSpec((fb, n), lambda b, o, f: (f, 0)),
                  pl.BlockSpec((fb, n), lambda b, o, f: (nf + f, 0)),
                  pl.BlockSpec((n, fb), lambda b, o, f: (0, f)),
                  pl.BlockSpec((n, fb), lambda b, o, f: (0, nf + f)),
                  pl.BlockSpec((None, None, 3, fb, D_HY), lambda b, o, f: (o, f, 0, 0, 0)),
                  pl.BlockSpec((HY_ORDER, D_HY), lambda b, o, f: (0, 0)),
                  pl.BlockSpec(memory_space=pl.ANY)],
        out_specs=pl.BlockSpec((1, n, D_HY), lambda b, o, f: (b, row_block, 0)),
        out_shape=out_shape,
        scratch_shapes=[pltpu.VMEM((n, D_HY), BF16), pltpu.VMEM((n, D_HY), F32)],
        input_output_aliases=aliases,
        compiler_params=_cparams(("parallel", "arbitrary", "arbitrary"), VMEM_LIMIT_BIG),
    )(*args)


def hyena_mixer_stream(p_hy, conv_w, conv_b, filt, bias):
    u = dwconv_stream(p_hy, conv_w, conv_b, act=False)
    out = None
    for n, row_block in ((SEQ, 0), (CTX_LEN, SEQ // CTX_LEN)):
        fb = min(HY_FB, n)
        fwd, inv = dft_matrices(n)
        hsum, hdiff = hyena_filter_taps(n, *filt)
        spec = hyena_spectrum(fwd, hsum, hdiff, fb)
        out = hyena_long_conv(u, fwd, inv, spec, bias, n, row_block, out)
    return out


SCAN_SAMPLES = 4


def _tri(n, kind):
    r = lax.broadcasted_iota(jnp.int32, (n, n), 0)
    c = lax.broadcasted_iota(jnp.int32, (n, n), 1)
    return (c <= r) if kind == 'lower' else (c >= r)


def _ssd_dir(xbc_ref, dt_ref, dtt_ref, bias_r, bias_c, a_r, a_c, st_ref, y_ref, *, s, d, reverse):
    q = SSD_CHUNK
    nh = SSD_H
    gw = SSD_HPG * SSD_P
    lower = _tri(q, 'lower')
    upper = _tri(q, 'upper')
    lower_f = lower.astype(F32)
    upper_f = upper.astype(F32)
    dt_col = _softplus(dt_ref[s] + bias_r)
    dt_row = _softplus(dtt_ref[s] + bias_c)
    da_col = dt_col * a_r
    da_row = dt_row * a_c
    if not reverse:
        acs_col = jnp.dot(lower_f, da_col, preferred_element_type=F32, precision=HIGHEST)
        acs_row = jnp.dot(da_row, upper_f, preferred_element_type=F32, precision=HIGHEST)
        mask = lower
        edge = q - 1
    else:
        acs_col = jnp.dot(upper_f, da_col, preferred_element_type=F32, precision=HIGHEST)
        acs_row = jnp.dot(da_row, lower_f, preferred_element_type=F32, precision=HIGHEST)
        mask = upper
        edge = 0
    h0 = d * nh
    hh = lax.broadcasted_iota(jnp.int32, (2 * nh, nh * SSD_P), 0)
    cc = lax.broadcasted_iota(jnp.int32, (2 * nh, nh * SSD_P), 1) // SSD_P
    expand = (hh == cc + h0).astype(F32)
    acs_c = jnp.dot(acs_col, expand, preferred_element_type=F32, precision=HIGHEST)
    dt_c = jnp.dot(dt_col, expand, preferred_element_type=F32, precision=HIGHEST)
    total_c = acs_c[edge:edge + 1, :]
    e_in_c = jnp.exp(acs_c)
    w_end_c = jnp.exp(total_c - acs_c) * dt_c
    dec_c = jnp.exp(total_c)
    xs = xbc_ref[s, :, 0:D_SSM]
    xs_f = xs.astype(F32)
    for g in range(SSD_G):
        bm = xbc_ref[s, :, D_SSM + g * SSD_N:D_SSM + (g + 1) * SSD_N]
        cm = xbc_ref[s, :, D_SSM + SSD_G * SSD_N + g * SSD_N:D_SSM + SSD_G * SSD_N + (g + 1) * SSD_N]
        cb = lax.dot_general(cm, bm, (((1,), (1,)), ((), ())), preferred_element_type=F32)
        lws = []
        for k in range(SSD_HPG):
            h = h0 + g * SSD_HPG + k
            seg = acs_col[:, h:h + 1] - acs_row[h:h + 1, :]
            decay = jnp.exp(jnp.where(mask, seg, -jnp.inf))
            lws.append((cb * decay * dt_row[h:h + 1, :]).astype(BF16))
        lw = jnp.concatenate(lws, axis=1)
        xg = xs[:, g * gw:(g + 1) * gw]
        rb = lax.broadcasted_iota(jnp.int32, (SSD_HPG * q, gw), 0) // q
        cbk = lax.broadcasted_iota(jnp.int32, (SSD_HPG * q, gw), 1) // SSD_P
        x_bd = jnp.where(rb == cbk, jnp.concatenate([xg] * SSD_HPG, axis=0), jnp.zeros((), BF16))
        y_in = jnp.dot(lw, x_bd, preferred_element_type=F32)
        st = st_ref[s, g]
        y_st = jnp.dot(cm, st.astype(BF16), preferred_element_type=F32) * e_in_c[:, g * gw:(g + 1) * gw]
        y_ref[s, :, g * gw:(g + 1) * gw] = (y_in + y_st).astype(y_ref.dtype)
        xw = (xs_f[:, g * gw:(g + 1) * gw] * w_end_c[:, g * gw:(g + 1) * gw]).astype(BF16)
        upd = lax.dot_general(bm, xw, (((0,), (0,)), ((), ())), preferred_element_type=F32)
        st_ref[s, g] = st * dec_c[:, g * gw:(g + 1) * gw] + upd


def _ssd_body(xf_ref, dtf_ref, dttf_ref, xb_ref, dtb_ref, dttb_ref, bias_r, bias_c, a_r, a_c,
              yf_ref, yb_ref, stf, stb):
    @pl.when(pl.program_id(1) == 0)
    def _():
        stf[...] = jnp.zeros_like(stf)
        stb[...] = jnp.zeros_like(stb)

    for s in range(SCAN_SAMPLES):
        _ssd_dir(xf_ref, dtf_ref, dttf_ref, bias_r[...], bias_c[...], a_r[...], a_c[...], stf, yf_ref,
                 s=s, d=0, reverse=False)
        _ssd_dir(xb_ref, dtb_ref, dttb_ref, bias_r[...], bias_c[...], a_r[...], a_c[...], stb, yb_ref,
                 s=s, d=1, reverse=True)


def ssd_scan(xbc, dt, dt_bias, a_log):
    bsz = xbc.shape[0]
    nc = NTOK // SSD_CHUNK
    nlat = SEQ // SSD_CHUNK
    dtt = jnp.swapaxes(dt, 1, 2)
    fwd_chunk = lambda s: (s + nlat) % nc
    bwd_chunk = lambda s: nc - 1 - s
    a = -jnp.exp(a_log.astype(F32)).reshape(1, 2 * SSD_H)
    bias = dt_bias.astype(F32).reshape(1, 2 * SSD_H)
    ns = SCAN_SAMPLES
    x_spec = lambda cm: pl.BlockSpec((ns, SSD_CHUNK, SSD_XBC), lambda b, s: (b, cm(s), 0))
    dt_spec = lambda cm: pl.BlockSpec((ns, SSD_CHUNK, 2 * SSD_H), lambda b, s: (b, cm(s), 0))
    dtt_spec = lambda cm: pl.BlockSpec((ns, 2 * SSD_H, SSD_CHUNK), lambda b, s: (b, 0, cm(s)))
    y_spec = lambda cm: pl.BlockSpec((ns, SSD_CHUNK, D_SSM), lambda b, s: (b, cm(s), 0))
    row = pl.BlockSpec((1, 2 * SSD_H), lambda b, s: (0, 0))
    col = pl.BlockSpec((2 * SSD_H, 1), lambda b, s: (0, 0))
    y_shape = jax.ShapeDtypeStruct((bsz, NTOK, D_SSM), BF16)
    gw = SSD_HPG * SSD_P
    return pl.pallas_call(
        _ssd_body,
        name='ssd_scan',
        grid=(bsz // ns, nc),
        in_specs=[x_spec(fwd_chunk), dt_spec(fwd_chunk), dtt_spec(fwd_chunk),
                  x_spec(bwd_chunk), dt_spec(bwd_chunk), dtt_spec(bwd_chunk),
                  row, col, row, col],
        out_specs=[y_spec(fwd_chunk), y_spec(bwd_chunk)],
        out_shape=[y_shape, y_shape],
        scratch_shapes=[pltpu.VMEM((ns, SSD_G, SSD_N, gw), F32), pltpu.VMEM((ns, SSD_G, SSD_N, gw), F32)],
        compiler_params=_cparams(("parallel", "arbitrary")),
    )(xbc, dt, dtt, xbc, dt, dtt, bias, bias.reshape(-1, 1), a, a.reshape(-1, 1))


def even_layer_mixer(xs, mods, norm_w, w_in, w_out, hy_conv_w, hy_conv_b, hy_filt, hy_bias,
                     ssd_conv_w, ssd_conv_b, ssd_dt_bias, ssd_a_log, ssd_d, ssd_norm_w):
    splits = ((0, HY_IN), (HY_IN, D_SSM), (HY_IN + D_SSM, SSD_XBC), (HY_IN + D_SSM + SSD_XBC, 2 * SSD_H))
    p_hy, z, xbc_raw, dt = norm_mm_split(xs, norm_w, mods, w_in.astype(BF16), splits, (BF16, BF16, BF16, F32))
    y_hy = hyena_mixer_stream(p_hy, hy_conv_w, hy_conv_b, hy_filt, hy_bias)
    xbc = dwconv_stream(xbc_raw, ssd_conv_w, ssd_conv_b, act=True)
    yf, yb = ssd_scan(xbc, dt, ssd_dt_bias, ssd_a_log)
    wo = w_out.astype(BF16)
    return even_out_resid(y_hy, yf, yb, xbc, z, ssd_d, ssd_norm_w, wo[:D_HY], wo[D_HY:], xs, mods)


GLA_QK = GLA_H * GLA_DK
GLA_V = GLA_H * GLA_DV


def _gla_dir(qkv_ref, lr_ref, gw_ref, gb_ref, st_ref, o_ref, *, s, d, reverse):
    q = GLA_CHUNK
    tri = _tri(q, 'upper' if reverse else 'lower')
    edge = 0 if reverse else q - 1
    lr = lr_ref[s, :, d * GLA_RANK:(d + 1) * GLA_RANK]
    logit = jnp.dot(lr, gw_ref[d], preferred_element_type=F32, precision=HIGHEST) + gb_ref[d:d + 1, :]
    log_g = -_softplus(-logit) * (1.0 / GLA_GATE_NORM)
    gcum = jnp.dot(tri.astype(F32), log_g, preferred_element_type=F32, precision=HIGHEST)
    total = gcum[edge:edge + 1, :]
    qf = qkv_ref[s, :, 0:GLA_QK].astype(F32)
    kf = qkv_ref[s, :, GLA_QK:2 * GLA_QK].astype(F32)
    v = qkv_ref[s, :, 2 * GLA_QK:2 * GLA_QK + GLA_V]
    qg = (qf * (GLA_DK ** -0.5) * jnp.exp(gcum)).astype(BF16)
    kg = (kf * jnp.exp(-gcum)).astype(BF16)
    kw = (kf * jnp.exp(total - gcum)).astype(BF16)
    rb = lax.broadcasted_iota(jnp.int32, (GLA_H * q, GLA_QK), 0) // q
    cb = lax.broadcasted_iota(jnp.int32, (GLA_H * q, GLA_QK), 1) // GLA_DK
    k_bd = jnp.where(rb == cb, jnp.concatenate([kg] * GLA_H, axis=0), jnp.zeros((), BF16))
    att = lax.dot_general(qg, k_bd, (((1,), (1,)), ((), ())), preferred_element_type=F32)
    i_i = lax.broadcasted_iota(jnp.int32, (q, GLA_H * q), 0)
    j_i = lax.broadcasted_iota(jnp.int32, (q, GLA_H * q), 1) % q
    keep = (j_i >= i_i) if reverse else (j_i <= i_i)
    att = jnp.where(keep, att, 0.0).astype(BF16)
    rv = lax.broadcasted_iota(jnp.int32, (GLA_H * q, GLA_V), 0) // q
    cv = lax.broadcasted_iota(jnp.int32, (GLA_H * q, GLA_V), 1) // GLA_DV
    v_bd = jnp.where(rv == cv, jnp.concatenate([v] * GLA_H, axis=0), jnp.zeros((), BF16))
    st = st_ref[s]
    o_in = jnp.dot(att, v_bd, preferred_element_type=F32)
    o_st = lax.dot_general(qg, st.astype(BF16), (((1,), (1,)), ((), ())), preferred_element_type=F32)
    o_ref[s] = (o_in + o_st).astype(o_ref.dtype)
    upd = lax.dot_general(v, kw, (((0,), (0,)), ((), ())), preferred_element_type=F32)
    rs = lax.broadcasted_iota(jnp.int32, (GLA_V, GLA_QK), 0) // GLA_DV
    cs = lax.broadcasted_iota(jnp.int32, (GLA_V, GLA_QK), 1) // GLA_DK
    st_ref[s] = st * jnp.exp(total) + jnp.where(rs == cs, upd, 0.0)


def _gla_body(qf_ref, lf_ref, qb_ref, lb_ref, gw_ref, gb_ref, of_ref, ob_ref, stf, stb):
    @pl.when(pl.program_id(1) == 0)
    def _():
        stf[...] = jnp.zeros_like(stf)
        stb[...] = jnp.zeros_like(stb)

    for s in range(SCAN_SAMPLES):
        _gla_dir(qf_ref, lf_ref, gw_ref, gb_ref, stf, of_ref, s=s, d=0, reverse=False)
        _gla_dir(qb_ref, lb_ref, gw_ref, gb_ref, stb, ob_ref, s=s, d=1, reverse=True)


def gla_scan(qkv, lr, gate_w, gate_b):
    bsz = qkv.shape[0]
    nc = NTOK // GLA_CHUNK
    nlat = SEQ // GLA_CHUNK
    fwd_chunk = lambda s: (s + nlat) % nc
    bwd_chunk = lambda s: nc - 1 - s
    ns = SCAN_SAMPLES
    q_spec = lambda cm: pl.BlockSpec((ns, GLA_CHUNK, qkv.shape[-1]), lambda b, s: (b, cm(s), 0))
    l_spec = lambda cm: pl.BlockSpec((ns, GLA_CHUNK, 2 * GLA_RANK), lambda b, s: (b, cm(s), 0))
    o_spec = lambda cm: pl.BlockSpec((ns, GLA_CHUNK, GLA_V), lambda b, s: (b, cm(s), 0))
    o_shape = jax.ShapeDtypeStruct((bsz, NTOK, GLA_V), BF16)
    return pl.pallas_call(
        _gla_body,
        name='gla_scan',
        grid=(bsz // ns, nc),
        in_specs=[q_spec(fwd_chunk), l_spec(fwd_chunk), q_spec(bwd_chunk), l_spec(bwd_chunk),
                  pl.BlockSpec((2, GLA_RANK, GLA_QK), lambda b, s: (0, 0, 0)),
                  pl.BlockSpec((2, GLA_QK), lambda b, s: (0, 0))],
        out_specs=[o_spec(fwd_chunk), o_spec(bwd_chunk)],
        out_shape=[o_shape, o_shape],
        scratch_shapes=[pltpu.VMEM((ns, GLA_V, GLA_QK), F32), pltpu.VMEM((ns, GLA_V, GLA_QK), F32)],
        compiler_params=_cparams(("parallel", "arbitrary")),
    )(qkv, lr, qkv, lr, gate_w.astype(F32), gate_b.astype(F32))


def _gla_merge_body(of_ref, ob_ref, r_ref, nw_ref, o_ref):
    o = of_ref[0].astype(F32) + ob_ref[0].astype(F32)
    r = r_ref[0].astype(F32)
    for h in range(GLA_H):
        sl = slice(h * GLA_DV, (h + 1) * GLA_DV)
        oh = o[:, sl]
        ms = jnp.mean(oh * oh, axis=-1, keepdims=True)
        o_ref[0, :, sl] = (oh * lax.rsqrt(ms + NORM_EPS) * nw_ref[:, sl] * _silu(r[:, sl])).astype(o_ref.dtype)


def gla_merge_stream(of, ob, r, norm_w):
    bsz = of.shape[0]
    tile = pl.BlockSpec((1, ROW_TILE, GLA_V), lambda i, j: (i, j, 0))
    return pl.pallas_call(
        _gla_merge_body,
        name='gla_merge',
        grid=(bsz, N_LAT_TILES),
        in_specs=[tile, tile, tile, pl.BlockSpec((1, GLA_V), lambda i, j: (0, 0))],
        out_specs=tile,
        out_shape=jax.ShapeDtypeStruct((bsz, SEQ, GLA_V), BF16),
        compiler_params=_cparams(("parallel", "parallel")),
    )(of, ob, r, norm_w.reshape(1, GLA_V))


RG_TILE = 8


def _gelu_tanh(x):
    return 0.5 * x * (1.0 + jnp.tanh(math.sqrt(2.0 / math.pi) * (x + 0.044715 * x * x * x)))


def _rg_scan_block(a_s, x_s, h_s, base, carry, reverse):
    n_tiles = ROW_TILE // RG_TILE
    row = lax.broadcasted_iota(jnp.int32, (RG_TILE, D_RG), 0)

    def tile_step(i, h_prev):
        t = (n_tiles - 1 - i) if reverse else i
        r0 = pl.multiple_of(t * RG_TILE, RG_TILE)
        a = a_s[pl.ds(r0, RG_TILE), :]
        x = x_s[pl.ds(r0, RG_TILE), :]
        for s in (1, 2, 4):
            if reverse:
                ok = row < RG_TILE - s
                shift = RG_TILE - s
            else:
                ok = row >= s
                shift = s
            a_sh = jnp.where(ok, pltpu.roll(a, shift, 0), 1.0)
            x_sh = jnp.where(ok, pltpu.roll(x, shift, 0), 0.0)
            x = a * x_sh + x
            a = a * a_sh
        h = x + a * h_prev
        h_s[pl.ds(base + r0, RG_TILE), :] = h
        edge = 0 if reverse else RG_TILE - 1
        return jnp.broadcast_to(h[edge:edge + 1, :], (RG_TILE, D_RG))

    return lax.fori_loop(0, n_tiles, tile_step, carry)


def _rglru_body(u_ref, g_ref, w_ref, b_ref, c_ref, o_ref, hf_s, a_s, x_s, hb_s):
    n_blocks = NTOK // ROW_TILE
    fwd_order = list(range(N_LAT_TILES, n_blocks)) + list(range(N_LAT_TILES))
    bwd_order = list(range(n_blocks - 1, N_LAT_TILES - 1, -1)) + list(range(N_LAT_TILES - 1, -1, -1))

    def gates(blk, d):
        ub = u_ref[0, blk * ROW_TILE:(blk + 1) * ROW_TILE, :]
        z = jnp.dot(ub, w_ref[:, 2 * d * D_RG:2 * (d + 1) * D_RG], preferred_element_type=F32)
        z = z + b_ref[:, 2 * d * D_RG:2 * (d + 1) * D_RG]
        r = 1.0 / (1.0 + jnp.exp(-z[:, :D_RG]))
        i = 1.0 / (1.0 + jnp.exp(-z[:, D_RG:]))
        a = jnp.exp(c_ref[d:d + 1, :] * r)
        a_s[...] = a
        x_s[...] = jnp.sqrt(1.0 - a * a) * i * ub.astype(F32)

    carry = jnp.zeros((RG_TILE, D_RG), F32)
    for blk in fwd_order:
        gates(blk, 0)
        carry = _rg_scan_block(a_s, x_s, hf_s, blk * ROW_TILE, carry, reverse=False)
    carry = jnp.zeros((RG_TILE, D_RG), F32)
    for blk in bwd_order:
        gates(blk, 1)
        carry = _rg_scan_block(a_s, x_s, hb_s, 0, carry, reverse=True)
        rows = slice(blk * ROW_TILE, (blk + 1) * ROW_TILE)
        gate = g_ref[0, rows, :].astype(F32)
        o_ref[0, rows, :] = ((hf_s[rows, :] + hb_s[...]) * _gelu_tanh(gate)).astype(o_ref.dtype)


def rglru_stream(u, gate, w_a, b_a, w_x, b_x, lam):
    bsz = u.shape[0]
    eye = jnp.eye(RG_BLOCKS, dtype=F32)
    dense = lambda w: jnp.einsum('nio,nm->nimo', w, eye).reshape(D_RG, D_RG)
    w_cat = jnp.concatenate([dense(w_a[0]), dense(w_x[0]), dense(w_a[1]), dense(w_x[1])], axis=1).astype(BF16)
    b_cat = jnp.concatenate([b_a[0], b_x[0], b_a[1], b_x[1]]).astype(F32).reshape(1, 4 * D_RG)
    c = -RG_C * jax.nn.softplus(-lam.astype(F32))
    seq = pl.BlockSpec((1, NTOK, D_RG), lambda i: (i, 0, 0))
    return pl.pallas_call(
        _rglru_body,
        name='rglru',
        grid=(bsz,),
        in_specs=[seq, seq,
                  pl.BlockSpec((D_RG, 4 * D_RG), lambda i: (0, 0)),
                  pl.BlockSpec((1, 4 * D_RG), lambda i: (0, 0)),
                  pl.BlockSpec((2, D_RG), lambda i: (0, 0))],
        out_specs=seq,
        out_shape=jax.ShapeDtypeStruct((bsz, NTOK, D_RG), BF16),
        scratch_shapes=[pltpu.VMEM((NTOK, D_RG), F32), pltpu.VMEM((ROW_TILE, D_RG), F32),
                        pltpu.VMEM((ROW_TILE, D_RG), F32), pltpu.VMEM((ROW_TILE, D_RG), F32)],
        compiler_params=_cparams(("parallel",)),
    )(u, gate, w_cat, b_cat, c)


def odd_layer_mixer_pallas(xs, mods, norm_w, w_in, w_out, gla_args, rg_args):
    bsz = xs.shape[0]
    gate_w, gate_b, gla_norm_w = gla_args
    rg_conv_w, rg_conv_b, w_a, b_a, w_x, b_x, lam = rg_args
    h = norm_mod(xs, norm_w, mods, 0, 1)
    h = jnp.concatenate([to_col_major(h[:, :SEQ]), h[:, SEQ:]], axis=1)
    nqk, nv = GLA_QK, GLA_V
    r0 = 2 * nqk + nv + 2 * GLA_RANK
    w = jnp.concatenate([w_in[:, :2 * nqk + nv], w_in[:, r0:r0 + nv], w_in[:, GLA_IN:],
                         w_in[:, 2 * nqk + nv:r0]], axis=1).astype(BF16)
    qkv_w = 2 * nqk + nv
    splits = ((0, qkv_w), (qkv_w, nv), (qkv_w + nv, D_RG), (qkv_w + nv + D_RG, D_RG),
              (qkv_w + nv + 2 * D_RG, 2 * GLA_RANK))
    qkv, r, u_raw, gate, lr = mm_split(h.reshape(bsz * NTOK, D_MODEL), w, splits, (BF16, BF16, BF16, BF16, F32))
    to3 = lambda a: a.reshape(bsz, NTOK, a.shape[-1])
    of, ob = gla_scan(to3(qkv), to3(lr), gate_w, gate_b.reshape(2, GLA_QK))
    a_l = gla_merge_stream(of, ob, to3(r), gla_norm_w)
    u = dwconv_stream(to3(u_raw), rg_conv_w, rg_conv_b, act=False)
    r_l = rglru_stream(u, to3(gate), w_a, b_a, w_x, b_x, lam)[:, :SEQ]
    wo = w_out.astype(BF16)
    return mm_resid([from_col_major(a_l), from_col_major(r_l)], [wo[:GLA_V], wo[GLA_V:]], xs, mods, 2, N_LAT_TILES)


def to_col_major(x):
    b, n, d = x.shape
    rows = n // GRID_W
    return x.reshape(b, rows, GRID_W, d).transpose(0, 2, 1, 3).reshape(b, n, d)


def from_col_major(x):
    b, n, d = x.shape
    rows = n // GRID_W
    return x.reshape(b, GRID_W, rows, d).transpose(0, 2, 1, 3).reshape(b, n, d)


def kernel(x, c, ctx, c_ctx, ada_w, ada_b, norm1_w, norm2_w, ev_w_in, ev_w_out, hy_conv_w, hy_conv_b, hy_fw0, hy_fb0, hy_fw1, hy_fb1, hy_fw2, hy_fb2, hy_fw3, hy_freq, hy_bias, ssd_conv_w, ssd_conv_b, ssd_dt_bias, ssd_a_log, ssd_d, ssd_norm_w, od_w_in, od_w_out, gla_gate_w, gla_gate_b, gla_norm_w, rg_conv_w, rg_conv_b, rg_w_a, rg_b_a, rg_w_x, rg_b_x, rg_lambda, router_w, router_b, moe_w_gate, moe_w_up, moe_w_down, sh_w_gate, sh_w_up, sh_w_down, final_norm_w):
    xs = jnp.concatenate([x, ctx], axis=1)
    for i in range(DEPTH):
        last = i == DEPTH - 1
        j = i // 2
        mods = adaln_table(c, c_ctx, ada_w[i], ada_b[i])
        if i % 2 == 0:
            hy_filt = (hy_fw0[j], hy_fb0[j], hy_fw1[j], hy_fb1[j], hy_fw2[j], hy_fb2[j], hy_fw3[j], hy_freq[j])
            xs = even_layer_mixer(xs, mods, norm1_w[i], ev_w_in[j], ev_w_out[j], hy_conv_w[j], hy_conv_b[j],
                                  hy_filt, hy_bias[j], ssd_conv_w[j], ssd_conv_b[j], ssd_dt_bias[j],
                                  ssd_a_log[j], ssd_d[j], ssd_norm_w[j])
        else:
            gla_args = (gla_gate_w[j], gla_gate_b[j], gla_norm_w[j])
            rg_args = (rg_conv_w[j], rg_conv_b[j], rg_w_a[j], rg_b_a[j], rg_w_x[j], rg_b_x[j], rg_lambda[j])
            xs = odd_layer_mixer_pallas(xs, mods, norm1_w[i], od_w_in[j], od_w_out[j], gla_args, rg_args)
        n_tiles = N_LAT_TILES if last else N_ROW_TILES
        xs = moe_layer(xs, norm2_w[i], mods, router_w[i], router_b[i], moe_w_gate, moe_w_up, moe_w_down,
                       sh_w_gate[i], sh_w_up[i], sh_w_down[i], n_tiles, i)
    return final_norm(xs, final_norm_w)
```

```python
import functools
import math

import jax
import jax.numpy as jnp
from jax import lax
from jax.experimental import pallas as pl
from jax.experimental.pallas import tpu as pltpu

D_MODEL = 1024
BATCH = 16
SEQ = 2048
DEPTH = 2

CTX_LEN = 256
GRID_W = 64
NORM_EPS = 1e-6

D_HY = D_MODEL // 2
HY_ORDER = 2
HY_SHORT = 3
HY_BANDS = 16
HY_EMB = 1 + 2 * HY_BANDS
HY_FF = 64
HY_DECAY_PCT_LO = 0.3
HY_DECAY_PCT_HI = 1.5
HY_DECAY_TARGET = 1e-2
HY_IN = 3 * D_HY

D_SSM = D_MODEL // 2
SSD_P = 64
SSD_H = D_SSM // SSD_P
SSD_G = 2
SSD_HPG = SSD_H // SSD_G
SSD_N = 128
SSD_CONV = 4
SSD_CHUNK = 128
SSD_XBC = D_SSM + 2 * SSD_G * SSD_N
SSD_IN = D_SSM + SSD_XBC + 2 * SSD_H
EV_IN = HY_IN + SSD_IN
EV_MIX = D_HY + D_SSM

GLA_H = 4
GLA_DV = (D_MODEL // 2) // GLA_H
GLA_DK = GLA_DV // 2
GLA_RANK = 16
GLA_GATE_NORM = 16.0
GLA_CHUNK = 64
GLA_IN = 2 * GLA_H * GLA_DK + 2 * GLA_H * GLA_DV + 2 * GLA_RANK

D_RG = D_MODEL // 2
RG_BLOCKS = 8
RG_BW = D_RG // RG_BLOCKS
RG_CONV = 4
RG_C = 8.0
RG_IN = 2 * D_RG
OD_IN = GLA_IN + RG_IN
OD_MIX = GLA_H * GLA_DV + D_RG

MOE_EXPERTS = 64
MOE_TOPK = 8
MOE_D_EXPERT = 256
MOE_D_SHARED = 256
MOE_SCALE = 2.5
MOE_BLOCK = 512

F32 = jnp.float32
BF16 = jnp.bfloat16
HIGHEST = lax.Precision.HIGHEST

NTOK = SEQ + CTX_LEN
ROW_TILE = 256
N_ROW_TILES = NTOK // ROW_TILE
N_LAT_TILES = SEQ // ROW_TILE

VMEM_LIMIT = 48 * 1024 * 1024
VMEM_LIMIT_BIG = 56 * 1024 * 1024


def _cparams(sem, limit=VMEM_LIMIT):
    return pltpu.CompilerParams(dimension_semantics=sem, vmem_limit_bytes=limit)


def _pick_tile(n, pref):
    t = min(n, pref)
    while n % t:
        t //= 2
    return t


def _silu(x):
    return x / (1.0 + jnp.exp(-x))


def _softplus(x):
    return jnp.maximum(x, 0.0) + jnp.log(1.0 + jnp.exp(-jnp.abs(x)))


def _mm_bf16_body(a_ref, w_ref, o_ref):
    o_ref[...] = jnp.dot(a_ref[...].astype(BF16), w_ref[...].astype(BF16),
                         preferred_element_type=F32).astype(o_ref.dtype)


def _mm_f32_body(a_ref, w_ref, o_ref):
    o_ref[...] = jnp.dot(a_ref[...], w_ref[...], preferred_element_type=F32,
                         precision=HIGHEST).astype(o_ref.dtype)


def pmatmul(a, w, *, exact=False, out_dtype=F32, tm=512, tn=None):
    m, k = a.shape
    n = w.shape[1]
    tm = _pick_tile(m, tm)
    tn = n if tn is None else _pick_tile(n, tn)
    body = _mm_f32_body if exact else _mm_bf16_body
    return pl.pallas_call(
        body,
        name='mm',
        grid=(m // tm, n // tn),
        in_specs=[pl.BlockSpec((tm, k), lambda i, j: (i, 0)),
                  pl.BlockSpec((k, tn), lambda i, j: (0, j))],
        out_specs=pl.BlockSpec((tm, tn), lambda i, j: (i, j)),
        out_shape=jax.ShapeDtypeStruct((m, n), out_dtype),
        compiler_params=_cparams(("parallel", "parallel")),
    )(a, w)


def _mm_split_body(a_ref, w_ref, *o_refs, splits):
    a = a_ref[...]
    for o_ref, (start, width) in zip(o_refs, splits):
        o_ref[...] = jnp.dot(a, w_ref[:, start:start + width],
                             preferred_element_type=F32).astype(o_ref.dtype)


def mm_split(a, w, splits, dtypes, tm=512):
    m, k = a.shape
    n = w.shape[1]
    tm = _pick_tile(m, tm)
    return pl.pallas_call(
        functools.partial(_mm_split_body, splits=tuple(splits)),
        name='mm_split',
        grid=(m // tm,),
        in_specs=[pl.BlockSpec((tm, k), lambda i: (i, 0)),
                  pl.BlockSpec((k, n), lambda i: (0, 0))],
        out_specs=[pl.BlockSpec((tm, wd), lambda i: (i, 0)) for _, wd in splits],
        out_shape=[jax.ShapeDtypeStruct((m, wd), dt) for (_, wd), dt in zip(splits, dtypes)],
        compiler_params=_cparams(("parallel",)),
    )(a, w)


MIX_SAMPLES = 4


def _stack_samples(ref):
    return jnp.concatenate([ref[s] for s in range(MIX_SAMPLES)], axis=0)


def _resid_store(o_ref, x_ref, g_ref, acc):
    for s in range(MIX_SAMPLES):
        o_ref[s] = x_ref[s] + g_ref[s] * acc[s * ROW_TILE:(s + 1) * ROW_TILE]


def _mm_resid_body(*refs, n_pairs):
    a_refs = refs[:n_pairs]
    w_refs = refs[n_pairs:2 * n_pairs]
    x_ref, g_ref, o_ref = refs[2 * n_pairs:]
    acc = jnp.dot(_stack_samples(a_refs[0]), w_refs[0][...], preferred_element_type=F32)
    for a_ref, w_ref in zip(a_refs[1:], w_refs[1:]):
        acc = acc + jnp.dot(_stack_samples(a_ref), w_ref[...], preferred_element_type=F32)
    _resid_store(o_ref, x_ref, g_ref, acc)


def _gate_spec(gate_idx, d):
    return pl.BlockSpec((MIX_SAMPLES, None, None, 1, d), lambda i, j: (i, 1 - j // N_LAT_TILES, gate_idx, 0, 0))


def _mix_tile(width):
    return pl.BlockSpec((MIX_SAMPLES, ROW_TILE, width), lambda i, j: (i, j, 0))


def mm_resid(a_list, w_list, xs, mods, gate_idx, n_tiles):
    b, nt, d = xs.shape
    n_pairs = len(a_list)
    in_specs = [_mix_tile(a.shape[-1]) for a in a_list]
    in_specs += [pl.BlockSpec(w.shape, lambda i, j: (0, 0)) for w in w_list]
    in_specs += [_mix_tile(d), _gate_spec(gate_idx, d)]
    return pl.pallas_call(
        functools.partial(_mm_resid_body, n_pairs=n_pairs),
        name='mm_resid',
        grid=(b // MIX_SAMPLES, n_tiles),
        in_specs=in_specs,
        out_specs=_mix_tile(d),
        out_shape=jax.ShapeDtypeStruct(xs.shape, F32),
        input_output_aliases={2 * n_pairs: 0},
        compiler_params=_cparams(("parallel", "parallel")),
    )(*a_list, *w_list, xs, mods)


def _even_out_body(hy_ref, yf_ref, yb_ref, xs_ref, z_ref, d_ref, nw_ref, w1_ref, w2_ref, x_ref, g_ref, o_ref):
    gw = D_SSM // SSD_G
    merged = []
    for s in range(MIX_SAMPLES):
        y = yf_ref[s].astype(F32) + yb_ref[s].astype(F32) + d_ref[...] * xs_ref[s].astype(F32)
        g = y * _silu(z_ref[s].astype(F32))
        groups = []
        for k in range(SSD_G):
            gk = g[:, k * gw:(k + 1) * gw]
            ms = jnp.mean(gk * gk, axis=-1, keepdims=True)
            groups.append(gk * lax.rsqrt(ms + NORM_EPS) * nw_ref[:, k * gw:(k + 1) * gw])
        merged.append(jnp.concatenate(groups, axis=1).astype(BF16))
    acc = (jnp.dot(_stack_samples(hy_ref), w1_ref[...], preferred_element_type=F32)
           + jnp.dot(jnp.concatenate(merged, axis=0), w2_ref[...], preferred_element_type=F32))
    _resid_store(o_ref, x_ref, g_ref, acc)


def even_out_resid(y_hy, yf, yb, xbc, z, d_skip, norm_w, w_hy, w_ssd, xs, mods):
    b, nt, d = xs.shape
    d_chan = jnp.repeat(d_skip.astype(F32), SSD_P).reshape(1, D_SSM)
    vec = pl.BlockSpec((1, D_SSM), lambda i, j: (0, 0))
    wspec = pl.BlockSpec((D_SSM, d), lambda i, j: (0, 0))
    half = _mix_tile(D_SSM)
    return pl.pallas_call(
        _even_out_body,
        name='even_out',
        grid=(b // MIX_SAMPLES, N_ROW_TILES),
        in_specs=[half, half, half, half, half, vec, vec, wspec, wspec, _mix_tile(d), _gate_spec(2, d)],
        out_specs=_mix_tile(d),
        out_shape=jax.ShapeDtypeStruct(xs.shape, F32),
        input_output_aliases={9: 0},
        compiler_params=_cparams(("parallel", "parallel")),
    )(y_hy, yf, yb, xbc, z, d_chan, norm_w.reshape(1, D_SSM), w_hy, w_ssd, xs, mods)


def _mod_spec(idx, d, b0=0):
    return pl.BlockSpec((None, None, None, 1, d), lambda i, j: (i + b0, 1 - j // N_LAT_TILES, idx, 0, 0))


def adaln_table(c, c_ctx, w, b):
    cv = jax.nn.silu(jnp.concatenate([c, c_ctx[None, :]], axis=0))
    cv = jnp.pad(cv, ((0, -cv.shape[0] % 8), (0, 0)))
    m = pmatmul(cv, w, exact=True, tn=1536)[:BATCH + 1] + b
    per_sample = m[:BATCH]
    ctx_row = jnp.broadcast_to(m[BATCH][None, :], per_sample.shape)
    return jnp.stack([ctx_row, per_sample], axis=1).reshape(BATCH, 2, 6, 1, D_MODEL)


def _norm_mod(x, w, shift, scale):
    ms = jnp.mean(x * x, axis=-1, keepdims=True)
    return (x * lax.rsqrt(ms + NORM_EPS) * w) * (1.0 + scale) + shift


def _norm_mod_body(x_ref, w_ref, sh_ref, sc_ref, o_ref):
    o_ref[0] = _norm_mod(x_ref[0], w_ref[...], sh_ref[...], sc_ref[...]).astype(o_ref.dtype)


def norm_mod(xs, w, mods, shift_idx, scale_idx):
    b, nt, d = xs.shape
    return pl.pallas_call(
        _norm_mod_body,
        name='norm_mod',
        grid=(b, nt // ROW_TILE),
        in_specs=[pl.BlockSpec((1, ROW_TILE, d), lambda i, j: (i, j, 0)),
                  pl.BlockSpec((1, d), lambda i, j: (0, 0)),
                  _mod_spec(shift_idx, d), _mod_spec(scale_idx, d)],
        out_specs=pl.BlockSpec((1, ROW_TILE, d), lambda i, j: (i, j, 0)),
        out_shape=jax.ShapeDtypeStruct(xs.shape, BF16),
        compiler_params=_cparams(("parallel", "parallel")),
    )(xs, w.reshape(1, d), mods, mods)


def _norm_mm_split_body(x_ref, nw_ref, sh_ref, sc_ref, w_ref, *o_refs, splits):
    h = jnp.concatenate([_norm_mod(x_ref[s], nw_ref[...], sh_ref[s], sc_ref[s]).astype(BF16)
                         for s in range(MIX_SAMPLES)], axis=0)
    for o_ref, (start, width) in zip(o_refs, splits):
        y = jnp.dot(h, w_ref[:, start:start + width], preferred_element_type=F32).astype(o_ref.dtype)
        for s in range(MIX_SAMPLES):
            o_ref[s] = y[s * ROW_TILE:(s + 1) * ROW_TILE]


def norm_mm_split(xs, norm_w, mods, w, splits, dtypes):
    b, nt, d = xs.shape
    n = w.shape[1]
    return pl.pallas_call(
        functools.partial(_norm_mm_split_body, splits=tuple(splits)),
        name='norm_mm_split',
        grid=(b // MIX_SAMPLES, nt // ROW_TILE),
        in_specs=[_mix_tile(d), pl.BlockSpec((1, d), lambda i, j: (0, 0)), _gate_spec(0, d), _gate_spec(1, d),
                  pl.BlockSpec((d, n), lambda i, j: (0, 0))],
        out_specs=[_mix_tile(wd) for _, wd in splits],
        out_shape=[jax.ShapeDtypeStruct((b, nt, wd), dt) for (_, wd), dt in zip(splits, dtypes)],
        compiler_params=_cparams(("parallel", "parallel")),
    )(xs, norm_w.reshape(1, d), mods, mods, w)


def _final_norm_body(x_ref, w_ref, o_ref):
    x = x_ref[0]
    ms = jnp.mean(x * x, axis=-1, keepdims=True)
    o_ref[0] = x * lax.rsqrt(ms + NORM_EPS) * w_ref[...]


def final_norm(xs, w):
    b, _, d = xs.shape
    return pl.pallas_call(
        _final_norm_body,
        name='final_norm',
        grid=(b, N_LAT_TILES),
        in_specs=[pl.BlockSpec((1, ROW_TILE, d), lambda i, j: (i, j, 0)),
                  pl.BlockSpec((1, d), lambda i, j: (0, 0))],
        out_specs=pl.BlockSpec((1, ROW_TILE, d), lambda i, j: (i, j, 0)),
        out_shape=jax.ShapeDtypeStruct((b, SEQ, d), F32),
        compiler_params=_cparams(("parallel", "parallel")),
    )(xs, w.reshape(1, d))


def _route_t_body(x_ref, w_ref, sh_ref, sc_ref, rwh_ref, rwl_ref, rb_ref, h_ref, idx_ref, wsel_ref, rank_ref, cnt_ref,
                  *, group_size):
    first = (pl.program_id(0) % group_size == 0) & (pl.program_id(1) == 0)

    @pl.when(first)
    def _():
        cnt_ref[...] = jnp.zeros_like(cnt_ref)

    h = _norm_mod(x_ref[0], w_ref[...], sh_ref[...], sc_ref[...])
    h_hi = h.astype(BF16)
    h_ref[0] = h_hi
    h_lo = (h - h_hi.astype(F32)).astype(BF16)
    nt = (((1,), (1,)), ((), ()))
    logits = (lax.dot_general(rwh_ref[...], h_hi, nt, preferred_element_type=F32)
              + lax.dot_general(rwh_ref[...], h_lo, nt, preferred_element_type=F32)
              + lax.dot_general(rwl_ref[...], h_hi, nt, preferred_element_type=F32))
    scores = 1.0 / (1.0 + jnp.exp(-logits))
    ne, tm = scores.shape
    expert = lax.broadcasted_iota(jnp.int32, (ne, tm), 0).astype(F32)
    slot = lax.broadcasted_iota(jnp.int32, (MOE_TOPK, tm), 0)
    sel = scores + rb_ref[...]
    picked = jnp.zeros((ne, tm), F32)
    hits = []
    idx_out = jnp.zeros((MOE_TOPK, tm), F32)
    w_out = jnp.zeros((MOE_TOPK, tm), F32)
    for k in range(MOE_TOPK):
        m = jnp.max(sel, axis=0, keepdims=True)
        ik = jnp.min(jnp.where(sel == m, expert, float(ne)), axis=0, keepdims=True)
        hit = expert == ik
        wk = jnp.sum(jnp.where(hit, scores, 0.0), axis=0, keepdims=True)
        sel = jnp.where(hit, -jnp.inf, sel)
        picked = picked + hit.astype(F32)
        hits.append(hit)
        idx_out = jnp.where(slot == k, ik, idx_out)
        w_out = jnp.where(slot == k, wk, w_out)
    wsum = jnp.sum(w_out, axis=0, keepdims=True)
    wsel_ref[0] = w_out / wsum * MOE_SCALE
    idx_ref[0] = idx_out.astype(jnp.int32)
    r_i = lax.broadcasted_iota(jnp.int32, (tm, tm), 0)
    c_i = lax.broadcasted_iota(jnp.int32, (tm, tm), 1)
    earlier = (r_i < c_i).astype(BF16)
    before = jnp.dot(picked.astype(BF16), earlier, preferred_element_type=F32) + cnt_ref[...]
    rank_out = jnp.zeros((MOE_TOPK, tm), F32)
    for k in range(MOE_TOPK):
        rk = jnp.sum(jnp.where(hits[k], before, 0.0), axis=0, keepdims=True)
        rank_out = jnp.where(slot == k, rk, rank_out)
    rank_ref[0] = rank_out.astype(jnp.int32)
    cnt_ref[...] = cnt_ref[...] + jnp.sum(picked, axis=1, keepdims=True)


def route_t(xs, w, mods, router_w, router_b, n_tiles, group_size):
    b, _, d = xs.shape
    rows = n_tiles * ROW_TILE
    rwt = router_w.T.astype(F32)
    rwt_hi, rwt_lo = _split_bf16(rwt)
    small = lambda dt: jax.ShapeDtypeStruct((b, MOE_TOPK, rows), dt)
    small_spec = pl.BlockSpec((1, MOE_TOPK, ROW_TILE), lambda i, j: (i, 0, j))
    return pl.pallas_call(
        functools.partial(_route_t_body, group_size=group_size),
        name='route',
        grid=(b, n_tiles),
        in_specs=[pl.BlockSpec((1, ROW_TILE, d), lambda i, j: (i, j, 0)),
                  pl.BlockSpec((1, d), lambda i, j: (0, 0)),
                  _mod_spec(3, d), _mod_spec(4, d),
                  pl.BlockSpec((MOE_EXPERTS, d), lambda i, j: (0, 0)),
                  pl.BlockSpec((MOE_EXPERTS, d), lambda i, j: (0, 0)),
                  pl.BlockSpec((MOE_EXPERTS, 1), lambda i, j: (0, 0))],
        out_specs=[pl.BlockSpec((1, ROW_TILE, d), lambda i, j: (i, j, 0)),
                   small_spec, small_spec, small_spec,
                   pl.BlockSpec((None, MOE_EXPERTS, 1), lambda i, j: (i // group_size, 0, 0))],
        out_shape=[jax.ShapeDtypeStruct((b, rows, d), BF16), small(jnp.int32), small(F32), small(jnp.int32),
                   jax.ShapeDtypeStruct((b // group_size, MOE_EXPERTS, 1), F32)],
        compiler_params=_cparams(("arbitrary", "arbitrary")),
    )(xs, w.reshape(1, d), mods, mods, rwt_hi, rwt_lo, router_b.astype(F32).reshape(MOE_EXPERTS, 1))


def _swiglu(x, wg, wu, wd):
    g = jnp.dot(x, wg, preferred_element_type=F32)
    u = jnp.dot(x, wu, preferred_element_type=F32)
    h = (_silu(g) * u).astype(BF16)
    return jnp.dot(h, wd, preferred_element_type=F32)


def _expert_body(be_ref, nu_ref, xa_ref, xb_ref, wg_ref, wu_ref, wd_ref, o_ref, wg_s, wu_s, wd_s):
    i = pl.program_id(0)
    used = i < nu_ref[0]

    @pl.when(used & ((i == 0) | (be_ref[i] != be_ref[jnp.maximum(i - 1, 0)])))
    def _():
        wg_s[...] = wg_ref[0].astype(BF16)
        wu_s[...] = wu_ref[0].astype(BF16)
        wd_s[...] = wd_ref[0].astype(BF16)

    @pl.when(used)
    def _():
        x = jnp.concatenate([xa_ref[...], xb_ref[...]], axis=1)
        o_ref[...] = _swiglu(x, wg_s[...], wu_s[...], wd_s[...]).astype(o_ref.dtype)

    @pl.when(jnp.logical_not(used))
    def _():
        o_ref[...] = jnp.zeros_like(o_ref)


def moe_experts(x_rows, block_e, n_used, wg, wu, wd, layer):
    rows, d = x_rows.shape
    n_blocks = rows // MOE_BLOCK
    f = wg.shape[-1]
    grid_spec = pltpu.PrefetchScalarGridSpec(
        num_scalar_prefetch=2,
        grid=(n_blocks,),
        in_specs=[
            pl.BlockSpec((MOE_BLOCK, d // 2), lambda i, be, nu: (i, 0)),
            pl.BlockSpec((MOE_BLOCK, d // 2), lambda i, be, nu: (i, 1)),
            pl.BlockSpec((None, 1, d, f), lambda i, be, nu: (layer, be[i], 0, 0)),
            pl.BlockSpec((None, 1, d, f), lambda i, be, nu: (layer, be[i], 0, 0)),
            pl.BlockSpec((None, 1, f, d), lambda i, be, nu: (layer, be[i], 0, 0)),
        ],
        out_specs=pl.BlockSpec((MOE_BLOCK, d), lambda i, be, nu: (i, 0)),
        scratch_shapes=[pltpu.VMEM((d, f), BF16), pltpu.VMEM((d, f), BF16), pltpu.VMEM((f, d), BF16)],
    )
    return pl.pallas_call(
        _expert_body,
        name='experts',
        grid_spec=grid_spec,
        out_shape=jax.ShapeDtypeStruct((rows, d), BF16),
        compiler_params=_cparams(("arbitrary",)),
    )(block_e, n_used, x_rows, x_rows, wg, wu, wd)


COMBINE_SAMPLES = 2


def _shared_resid_body(h_ref, wg_ref, wu_ref, wd_ref, pk_ref, ws_ref, x_ref, g_ref, o_ref):
    for s in range(COMBINE_SAMPLES):
        y = _swiglu(h_ref[s], wg_ref[...], wu_ref[...], wd_ref[...])
        ws = ws_ref[s]
        for k in range(MOE_TOPK):
            y = y + ws[:, k:k + 1] * pk_ref[k, s].astype(F32)
        o_ref[s] = x_ref[s] + g_ref[s] * y


def shared_resid(h, picked, wsel, xs, mods, wg, wu, wd, n_tiles, b0):
    b, _, d = h.shape
    f = wg.shape[-1]
    ns = COMBINE_SAMPLES
    s0 = b0 // ns
    tile = pl.BlockSpec((ns, ROW_TILE, d), lambda i, j: (i, j, 0))
    xs_tile = pl.BlockSpec((ns, ROW_TILE, d), lambda i, j: (i + s0, j, 0))
    gate = pl.BlockSpec((ns, None, None, 1, d), lambda i, j: (i + s0, 1 - j // N_LAT_TILES, 5, 0, 0))
    return pl.pallas_call(
        _shared_resid_body,
        name='shared_resid',
        grid=(b // ns, n_tiles),
        in_specs=[tile,
                  pl.BlockSpec((d, f), lambda i, j: (0, 0)),
                  pl.BlockSpec((d, f), lambda i, j: (0, 0)),
                  pl.BlockSpec((f, d), lambda i, j: (0, 0)),
                  pl.BlockSpec((MOE_TOPK, ns, ROW_TILE, d), lambda i, j: (0, i, j, 0)),
                  pl.BlockSpec((ns, ROW_TILE, MOE_TOPK), lambda i, j: (i, j, 0)),
                  xs_tile, gate],
        out_specs=xs_tile,
        out_shape=jax.ShapeDtypeStruct(xs.shape, F32),
        input_output_aliases={6: 0},
        compiler_params=_cparams(("parallel", "parallel")),
    )(h, wg, wu, wd, picked, wsel, xs, mods)


MOE_GROUPS = 2


def moe_layer(xs, norm_w, mods, router_w, router_b, w_gate, w_up, w_down, sh_gate, sh_up, sh_down, n_tiles, layer):
    bsz, _, d = xs.shape
    shared_w = (sh_gate.astype(BF16), sh_up.astype(BF16), sh_down.astype(BF16))
    b = bsz // MOE_GROUPS
    h, idx, wsel, rank, counts = route_t(xs, norm_w, mods, router_w, router_b, n_tiles, b)
    h_flat = h.reshape(-1, d)
    wsel = jnp.swapaxes(wsel, 1, 2)
    for g in range(MOE_GROUPS):
        sl = slice(g * b, (g + 1) * b)
        xs = _moe_group(xs, mods, h_flat, h[sl], idx[sl], wsel[sl], rank[sl], counts[g, :, 0], w_gate, w_up, w_down,
                        shared_w, n_tiles, layer, g * b)
    return xs


def _moe_group(xs, mods, h_flat, h, idx, wsel, rank, counts, w_gate, w_up, w_down, shared_w, n_tiles, layer, b0):
    b, rows_per_sample, d = h.shape
    n = b * rows_per_sample
    counts = counts.astype(jnp.int32)
    padded = (counts + MOE_BLOCK - 1) // MOE_BLOCK * MOE_BLOCK
    ends = jnp.cumsum(padded)
    starts = ends - padded
    nk = n * MOE_TOPK
    n_blocks = -(-nk // MOE_BLOCK) + MOE_EXPERTS
    rows = n_blocks * MOE_BLOCK
    n_pad = rows - nk
    e_iota = jnp.arange(MOE_EXPERTS, dtype=jnp.int32)
    dest = jnp.sum(jnp.where(idx[..., None] == e_iota, starts, 0), axis=-1) + rank
    blk_start = jnp.arange(n_blocks, dtype=jnp.int32) * MOE_BLOCK
    block_e = jnp.minimum(jnp.sum(ends[None, :] <= blk_start[:, None], axis=1), MOE_EXPERTS - 1).astype(jnp.int32)
    n_used = (ends[-1:] // MOE_BLOCK).astype(jnp.int32)
    pad = padded - counts
    cum_pad = jnp.cumsum(pad)
    m = jnp.arange(n_pad, dtype=jnp.int32)
    e_m = jnp.sum(cum_pad[None, :] <= m[:, None], axis=1)
    base = jnp.sum(jnp.where(jnp.minimum(e_m, MOE_EXPERTS - 1)[:, None] == e_iota,
                             starts + counts - (cum_pad - pad), 0), axis=1)
    pad_row = jnp.where(e_m < MOE_EXPERTS, base + m, ends[-1] + m - cum_pad[-1])
    tok0 = b0 * rows_per_sample
    tok = (tok0 + jnp.arange(b, dtype=jnp.int32)[:, None, None] * rows_per_sample
           + jnp.arange(rows_per_sample, dtype=jnp.int32)[None, None, :])
    tok = jnp.broadcast_to(tok, dest.shape).reshape(-1)
    _, row_tok = lax.sort((jnp.concatenate([dest.reshape(-1), pad_row]).astype(jnp.int32),
                           jnp.concatenate([tok, tok0 + m % n])), num_keys=1)
    x_rows = h_flat[row_tok]
    y_rows = moe_experts(x_rows, block_e, n_used, w_gate, w_up, w_down, layer)
    picked = y_rows[jnp.swapaxes(dest, 0, 1)]
    return shared_resid(h, picked, wsel, xs, mods, *shared_w, n_tiles, b0)


def _dwconv_body(x_ref, w_ref, b_ref, o_ref, *, width, act):
    chunk = ROW_TILE
    n_chunks = NTOK // chunk
    first_of_seq = (0, N_LAT_TILES)
    last_of_seq = (N_LAT_TILES - 1, n_chunks - 1)
    tc = x_ref.shape[-1]
    halo = 16
    row = lax.broadcasted_iota(jnp.int32, (chunk, tc), 0)
    zero_row = jnp.zeros((1, tc), F32)
    for c in range(n_chunks):
        r0 = c * chunk
        cur = x_ref[0, r0:r0 + chunk, :].astype(F32)
        if c in first_of_seq:
            prev_last = zero_row
        else:
            prev_last = x_ref[0, r0 - halo:r0, :].astype(F32)[halo - 1:halo, :]
        if c in last_of_seq:
            next0 = next1 = zero_row
        else:
            nxt = x_ref[0, r0 + chunk:r0 + chunk + halo, :].astype(F32)
            next0, next1 = nxt[0:1, :], nxt[1:2, :]
        xm1 = jnp.where(row == 0, prev_last, pltpu.roll(cur, 1, 0))
        xp1 = jnp.where(row == chunk - 1, next0, pltpu.roll(cur, chunk - 1, 0))
        y = w_ref[0:1, :] * xm1 + w_ref[1:2, :] * cur + w_ref[2:3, :] * xp1 + b_ref[...]
        if width == 4:
            xp2 = jnp.where(row == chunk - 2, next0,
                            jnp.where(row == chunk - 1, next1, pltpu.roll(cur, chunk - 2, 0)))
            y = y + w_ref[3:4, :] * xp2
        if act:
            y = _silu(y)
        o_ref[0, c * chunk:(c + 1) * chunk, :] = y.astype(o_ref.dtype)


def dwconv_stream(x, w, b, act, tc=256):
    bsz, nt, c = x.shape
    width = w.shape[0]
    return pl.pallas_call(
        functools.partial(_dwconv_body, width=width, act=act),
        name='dwconv',
        grid=(bsz, c // tc),
        in_specs=[pl.BlockSpec((1, nt, tc), lambda i, j: (i, 0, j)),
                  pl.BlockSpec((width, tc), lambda i, j: (0, j)),
                  pl.BlockSpec((1, tc), lambda i, j: (0, j))],
        out_specs=pl.BlockSpec((1, nt, tc), lambda i, j: (i, 0, j)),
        out_shape=jax.ShapeDtypeStruct(x.shape, BF16),
        compiler_params=_cparams(("parallel", "parallel")),
    )(x, w, b.reshape(1, c))


HY_FB = 512
DFT_SPLIT = 64


def dft_matrices(n):
    t = jnp.arange(n, dtype=jnp.int32)[None, :]
    ka = jnp.arange(DFT_SPLIT, dtype=jnp.int32)[:, None]
    kb = jnp.arange(n // DFT_SPLIT, dtype=jnp.int32)[:, None] * DFT_SPLIT
    ang_a = (2.0 * math.pi / (2 * n)) * ((ka * t) % (2 * n)).astype(F32)
    ang_b = (2.0 * math.pi / (2 * n)) * ((kb * t) % (2 * n)).astype(F32)
    ca, sa = jnp.cos(ang_a)[None], jnp.sin(ang_a)[None]
    cb, sb = jnp.cos(ang_b)[:, None], jnp.sin(ang_b)[:, None]
    cos_kt = (ca * cb - sa * sb).reshape(n, n)
    sin_kt = (sa * cb + ca * sb).reshape(n, n)
    idx = jnp.arange(n, dtype=jnp.int32)
    nyq = jnp.where(idx % 2 == 0, 1.0, -1.0).astype(F32)
    fwd = jnp.concatenate([cos_kt, jnp.where(idx[:, None] == 0, nyq[None, :], -sin_kt)], axis=0)
    scale = jnp.where(idx == 0, 0.5, 1.0)[None, :] / n
    inv = jnp.concatenate([cos_kt * scale, jnp.where(idx[None, :] == 0, nyq[:, None], -sin_kt) * scale], axis=1)
    return fwd.astype(BF16), inv.astype(BF16)


def hyena_filter_taps(n, fw0, fb0, fw1, fb1, fw2, fb2, fw3, freq):
    pos = jnp.arange(n, dtype=F32)
    t = pos / max(n - 1, 1)
    bands = jnp.linspace(1e-4, HY_BANDS - 1, HY_BANDS, dtype=F32)
    ang = (2.0 * math.pi / n) * pos[:, None] * bands[None, :]
    feats = jnp.concatenate([t[:, None], jnp.cos(ang), -jnp.sin(ang)], axis=-1)
    h = jnp.sin(freq * (jnp.dot(feats, fw0, precision=HIGHEST) + fb0))
    h = jnp.sin(freq * (jnp.dot(h, fw1, precision=HIGHEST) + fb1))
    h = jnp.sin(freq * (jnp.dot(h, fw2, precision=HIGHEST) + fb2))
    h = pmatmul(h, fw3, exact=True).reshape(n, 2, HY_ORDER, D_HY)
    deltas = jnp.abs(jnp.linspace(math.log(HY_DECAY_PCT_LO) / HY_DECAY_TARGET,
                                  math.log(HY_DECAY_PCT_HI) / HY_DECAY_TARGET, D_HY, dtype=F32))
    h = h * jnp.exp(-t[:, None] * deltas)[:, None, None, :]
    h0 = h[:, 0]
    h1 = h[:, 1].at[0].set(0.0)
    norm = jnp.sum(jnp.abs(h0), axis=0, keepdims=True) + jnp.sum(jnp.abs(h1), axis=0, keepdims=True)
    h0 = (h0 / norm).reshape(n, HY_ORDER * D_HY)
    h1 = (h1 / norm).reshape(n, HY_ORDER * D_HY)
    return h0 + h1, h0 - h1


def _split_bf16(a):
    hi = a.astype(BF16)
    return hi, (a - hi.astype(F32)).astype(BF16)


def hyena_spectrum(fwd, hsum, hdiff, fb):
    n = hsum.shape[0]
    a = pmatmul(fwd, hsum.astype(BF16))
    bm = pmatmul(fwd, hdiff.astype(BF16))
    sr = a[:n]
    si = bm[n:]
    nyq = a[n]
    first = (jnp.arange(n) == 0)[:, None]
    p = sr
    q = jnp.where(first, 0.0, si)
    s = jnp.where(first, nyq[None, :], sr)
    spec = jnp.stack([p, q, s], axis=0).reshape(3, n // fb, fb, HY_ORDER, D_HY)
    return spec.transpose(3, 1, 0, 2, 4)


def _hyena_body(u_ref, fre_ref, fim_ref, gre_ref, gim_ref, sp_ref, bias_ref, prev_ref, o_ref,
                vin, acc, *, nf):
    del prev_ref
    o = pl.program_id(1)
    f = pl.program_id(2)
    c = D_HY

    @pl.when((o == 0) & (f == 0))
    def _():
        vin[...] = u_ref[0, :, 0:c]

    @pl.when(f == 0)
    def _():
        acc[...] = jnp.zeros_like(acc)

    v = vin[...]
    vr = jnp.dot(fre_ref[...], v, preferred_element_type=F32)
    vi = jnp.dot(fim_ref[...], v, preferred_element_type=F32)
    p, q, s = sp_ref[0], sp_ref[1], sp_ref[2]
    zr = (vr * p - vi * q).astype(BF16)
    zi = (vr * q + vi * s).astype(BF16)
    acc[...] += (jnp.dot(gre_ref[...], zr, preferred_element_type=F32)
                 + jnp.dot(gim_ref[...], zi, preferred_element_type=F32))

    @pl.when((o == 0) & (f == nf - 1))
    def _():
        z = u_ref[0, :, c:2 * c].astype(F32) * (acc[...] + bias_ref[0:1, :] * vin[...].astype(F32))
        vin[...] = z.astype(BF16)

    @pl.when((o == 1) & (f == nf - 1))
    def _():
        y = u_ref[0, :, 2 * c:3 * c].astype(F32) * (acc[...] + bias_ref[1:2, :] * vin[...].astype(F32))
        o_ref[0] = y.astype(o_ref.dtype)


def hyena_long_conv(u, fwd_bf16, inv_bf16, spec, bias, n, row_block, prev_out):
    bsz = u.shape[0]
    fb = spec.shape[3]
    nf = n // fb
    out_shape = jax.ShapeDtypeStruct((bsz, NTOK, D_HY), BF16)
    if prev_out is None:
        prev_out = jnp.zeros(out_shape.shape, BF16)
    args = [u, fwd_bf16, fwd_bf16, inv_bf16, inv_bf16, spec, bias, prev_out]
    aliases = {7: 0}
    return pl.pallas_call(
        functools.partial(_hyena_body, nf=nf),
        name='hyena',
        grid=(bsz, HY_ORDER, nf),
        in_specs=[pl.BlockSpec((1, n, 3 * D_HY), lambda b, o, f: (b, row_block, 0)),
                  pl.BlockSpec((fb, n), lambda b, o, f: (f, 0)),
                  pl.BlockSpec((fb, n), lambda b, o, f: (nf + f, 0)),
                  pl.BlockSpec((n, fb), lambda b, o, f: (0, f)),
                  pl.BlockSpec((n, fb), lambda b, o, f: (0, nf + f)),
                  pl.BlockSpec((None, None, 3, fb, D_HY), lambda b, o, f: (o, f, 0, 0, 0)),
                  pl.BlockSpec((HY_ORDER, D_HY), lambda b, o, f: (0, 0)),
                  pl.BlockSpec(memory_space=pl.ANY)],
        out_specs=pl.BlockSpec((1, n, D_HY), lambda b, o, f: (b, row_block, 0)),
        out_shape=out_shape,
        scratch_shapes=[pltpu.VMEM((n, D_HY), BF16), pltpu.VMEM((n, D_HY), F32)],
        input_output_aliases=aliases,
        compiler_params=_cparams(("parallel", "arbitrary", "arbitrary"), VMEM_LIMIT_BIG),
    )(*args)


def hyena_mixer_stream(p_hy, conv_w, conv_b, filt, bias):
    u = dwconv_stream(p_hy, conv_w, conv_b, act=False)
    out = None
    for n, row_block in ((SEQ, 0), (CTX_LEN, SEQ // CTX_LEN)):
        fb = min(HY_FB, n)
        fwd, inv = dft_matrices(n)
        hsum, hdiff = hyena_filter_taps(n, *filt)
        spec = hyena_spectrum(fwd, hsum, hdiff, fb)
        out = hyena_long_conv(u, fwd, inv, spec, bias, n, row_block, out)
    return out


SCAN_SAMPLES = 8


def _tri(n, kind):
    r = lax.broadcasted_iota(jnp.int32, (n, n), 0)
    c = lax.broadcasted_iota(jnp.int32, (n, n), 1)
    return (c <= r) if kind == 'lower' else (c >= r)


def _ssd_dir(xbc_ref, dt_ref, dtt_ref, bias_r, bias_c, a_r, a_c, st_ref, y_ref, *, s, d, reverse):
    q = SSD_CHUNK
    nh = SSD_H
    gw = SSD_HPG * SSD_P
    lower = _tri(q, 'lower')
    upper = _tri(q, 'upper')
    lower_f = lower.astype(F32)
    upper_f = upper.astype(F32)
    dt_col = _softplus(dt_ref[s] + bias_r)
    dt_row = _softplus(dtt_ref[s] + bias_c)
    da_col = dt_col * a_r
    da_row = dt_row * a_c
    if not reverse:
        acs_col = jnp.dot(lower_f, da_col, preferred_element_type=F32, precision=HIGHEST)
        acs_row = jnp.dot(da_row, upper_f, preferred_element_type=F32, precision=HIGHEST)
        mask = lower
        edge = q - 1
    else:
        acs_col = jnp.dot(upper_f, da_col, preferred_element_type=F32, precision=HIGHEST)
        acs_row = jnp.dot(da_row, lower_f, preferred_element_type=F32, precision=HIGHEST)
        mask = upper
        edge = 0
    h0 = d * nh
    hh = lax.broadcasted_iota(jnp.int32, (2 * nh, nh * SSD_P), 0)
    cc = lax.broadcasted_iota(jnp.int32, (2 * nh, nh * SSD_P), 1) // SSD_P
    expand = (hh == cc + h0).astype(F32)
    acs_c = jnp.dot(acs_col, expand, preferred_element_type=F32, precision=HIGHEST)
    dt_c = jnp.dot(dt_col, expand, preferred_element_type=F32, precision=HIGHEST)
    total_c = acs_c[edge:edge + 1, :]
    e_in_c = jnp.exp(acs_c)
    w_end_c = jnp.exp(total_c - acs_c) * dt_c
    dec_c = jnp.exp(total_c)
    xs = xbc_ref[s, :, 0:D_SSM]
    xs_f = xs.astype(F32)
    for g in range(SSD_G):
        bm = xbc_ref[s, :, D_SSM + g * SSD_N:D_SSM + (g + 1) * SSD_N]
        cm = xbc_ref[s, :, D_SSM + SSD_G * SSD_N + g * SSD_N:D_SSM + SSD_G * SSD_N + (g + 1) * SSD_N]
        cb = lax.dot_general(cm, bm, (((1,), (1,)), ((), ())), preferred_element_type=F32)
        lws = []
        for k in range(SSD_HPG):
            h = h0 + g * SSD_HPG + k
            seg = acs_col[:, h:h + 1] - acs_row[h:h + 1, :]
            decay = jnp.exp(jnp.where(mask, seg, -jnp.inf))
            lws.append((cb * decay * dt_row[h:h + 1, :]).astype(BF16))
        lw = jnp.concatenate(lws, axis=1)
        xg = xs[:, g * gw:(g + 1) * gw]
        rb = lax.broadcasted_iota(jnp.int32, (SSD_HPG * q, gw), 0) // q
        cbk = lax.broadcasted_iota(jnp.int32, (SSD_HPG * q, gw), 1) // SSD_P
        x_bd = jnp.where(rb == cbk, jnp.concatenate([xg] * SSD_HPG, axis=0), jnp.zeros((), BF16))
        y_in = jnp.dot(lw, x_bd, preferred_element_type=F32)
        st = st_ref[s, g]
        y_st = jnp.dot(cm, st.astype(BF16), preferred_element_type=F32) * e_in_c[:, g * gw:(g + 1) * gw]
        y_ref[s, :, g * gw:(g + 1) * gw] = (y_in + y_st).astype(y_ref.dtype)
        xw = (xs_f[:, g * gw:(g + 1) * gw] * w_end_c[:, g * gw:(g + 1) * gw]).astype(BF16)
        upd = lax.dot_general(bm, xw, (((0,), (0,)), ((), ())), preferred_element_type=F32)
        st_ref[s, g] = st * dec_c[:, g * gw:(g + 1) * gw] + upd


def _ssd_body(xf_ref, dtf_ref, dttf_ref, xb_ref, dtb_ref, dttb_ref, bias_r, bias_c, a_r, a_c,
              yf_ref, yb_ref, stf, stb):
    @pl.when(pl.program_id(1) == 0)
    def _():
        stf[...] = jnp.zeros_like(stf)
        stb[...] = jnp.zeros_like(stb)

    for s in range(SCAN_SAMPLES):
        _ssd_dir(xf_ref, dtf_ref, dttf_ref, bias_r[...], bias_c[...], a_r[...], a_c[...], stf, yf_ref,
                 s=s, d=0, reverse=False)
        _ssd_dir(xb_ref, dtb_ref, dttb_ref, bias_r[...], bias_c[...], a_r[...], a_c[...], stb, yb_ref,
                 s=s, d=1, reverse=True)


def ssd_scan(xbc, dt, dt_bias, a_log):
    bsz = xbc.shape[0]
    nc = NTOK // SSD_CHUNK
    nlat = SEQ // SSD_CHUNK
    dtt = jnp.swapaxes(dt, 1, 2)
    fwd_chunk = lambda s: (s + nlat) % nc
    bwd_chunk = lambda s: nc - 1 - s
    a = -jnp.exp(a_log.astype(F32)).reshape(1, 2 * SSD_H)
    bias = dt_bias.astype(F32).reshape(1, 2 * SSD_H)
    ns = SCAN_SAMPLES
    x_spec = lambda cm: pl.BlockSpec((ns, SSD_CHUNK, SSD_XBC), lambda b, s: (b, cm(s), 0))
    dt_spec = lambda cm: pl.BlockSpec((ns, SSD_CHUNK, 2 * SSD_H), lambda b, s: (b, cm(s), 0))
    dtt_spec = lambda cm: pl.BlockSpec((ns, 2 * SSD_H, SSD_CHUNK), lambda b, s: (b, 0, cm(s)))
    y_spec = lambda cm: pl.BlockSpec((ns, SSD_CHUNK, D_SSM), lambda b, s: (b, cm(s), 0))
    row = pl.BlockSpec((1, 2 * SSD_H), lambda b, s: (0, 0))
    col = pl.BlockSpec((2 * SSD_H, 1), lambda b, s: (0, 0))
    y_shape = jax.ShapeDtypeStruct((bsz, NTOK, D_SSM), BF16)
    gw = SSD_HPG * SSD_P
    return pl.pallas_call(
        _ssd_body,
        name='ssd_scan',
        grid=(bsz // ns, nc),
        in_specs=[x_spec(fwd_chunk), dt_spec(fwd_chunk), dtt_spec(fwd_chunk),
                  x_spec(bwd_chunk), dt_spec(bwd_chunk), dtt_spec(bwd_chunk),
                  row, col, row, col],
        out_specs=[y_spec(fwd_chunk), y_spec(bwd_chunk)],
        out_shape=[y_shape, y_shape],
        scratch_shapes=[pltpu.VMEM((ns, SSD_G, SSD_N, gw), F32), pltpu.VMEM((ns, SSD_G, SSD_N, gw), F32)],
        compiler_params=_cparams(("parallel", "arbitrary")),
    )(xbc, dt, dtt, xbc, dt, dtt, bias, bias.reshape(-1, 1), a, a.reshape(-1, 1))


def even_layer_mixer(xs, mods, norm_w, w_in, w_out, hy_conv_w, hy_conv_b, hy_filt, hy_bias,
                     ssd_conv_w, ssd_conv_b, ssd_dt_bias, ssd_a_log, ssd_d, ssd_norm_w):
    splits = ((0, HY_IN), (HY_IN, D_SSM), (HY_IN + D_SSM, SSD_XBC), (HY_IN + D_SSM + SSD_XBC, 2 * SSD_H))
    p_hy, z, xbc_raw, dt = norm_mm_split(xs, norm_w, mods, w_in.astype(BF16), splits, (BF16, BF16, BF16, F32))
    y_hy = hyena_mixer_stream(p_hy, hy_conv_w, hy_conv_b, hy_filt, hy_bias)
    xbc = dwconv_stream(xbc_raw, ssd_conv_w, ssd_conv_b, act=True)
    yf, yb = ssd_scan(xbc, dt, ssd_dt_bias, ssd_a_log)
    wo = w_out.astype(BF16)
    return even_out_resid(y_hy, yf, yb, xbc, z, ssd_d, ssd_norm_w, wo[:D_HY], wo[D_HY:], xs, mods)


GLA_QK = GLA_H * GLA_DK
GLA_V = GLA_H * GLA_DV


def _gla_dir(qkv_ref, lr_ref, gw_ref, gb_ref, st_ref, o_ref, *, s, d, reverse):
    q = GLA_CHUNK
    tri = _tri(q, 'upper' if reverse else 'lower')
    edge = 0 if reverse else q - 1
    lr = lr_ref[s, :, d * GLA_RANK:(d + 1) * GLA_RANK]
    logit = jnp.dot(lr, gw_ref[d], preferred_element_type=F32, precision=HIGHEST) + gb_ref[d:d + 1, :]
    log_g = -_softplus(-logit) * (1.0 / GLA_GATE_NORM)
    gcum = jnp.dot(tri.astype(F32), log_g, preferred_element_type=F32, precision=HIGHEST)
    total = gcum[edge:edge + 1, :]
    qf = qkv_ref[s, :, 0:GLA_QK].astype(F32)
    kf = qkv_ref[s, :, GLA_QK:2 * GLA_QK].astype(F32)
    v = qkv_ref[s, :, 2 * GLA_QK:2 * GLA_QK + GLA_V]
    qg = (qf * (GLA_DK ** -0.5) * jnp.exp(gcum)).astype(BF16)
    kg = (kf * jnp.exp(-gcum)).astype(BF16)
    kw = (kf * jnp.exp(total - gcum)).astype(BF16)
    rb = lax.broadcasted_iota(jnp.int32, (GLA_H * q, GLA_QK), 0) // q
    cb = lax.broadcasted_iota(jnp.int32, (GLA_H * q, GLA_QK), 1) // GLA_DK
    k_bd = jnp.where(rb == cb, jnp.concatenate([kg] * GLA_H, axis=0), jnp.zeros((), BF16))
    att = lax.dot_general(qg, k_bd, (((1,), (1,)), ((), ())), preferred_element_type=F32)
    i_i = lax.broadcasted_iota(jnp.int32, (q, GLA_H * q), 0)
    j_i = lax.broadcasted_iota(jnp.int32, (q, GLA_H * q), 1) % q
    keep = (j_i >= i_i) if reverse else (j_i <= i_i)
    att = jnp.where(keep, att, 0.0).astype(BF16)
    rv = lax.broadcasted_iota(jnp.int32, (GLA_H * q, GLA_V), 0) // q
    cv = lax.broadcasted_iota(jnp.int32, (GLA_H * q, GLA_V), 1) // GLA_DV
    v_bd = jnp.where(rv == cv, jnp.concatenate([v] * GLA_H, axis=0), jnp.zeros((), BF16))
    st = st_ref[s]
    o_in = jnp.dot(att, v_bd, preferred_element_type=F32)
    o_st = lax.dot_general(qg, st.astype(BF16), (((1,), (1,)), ((), ())), preferred_element_type=F32)
    o_ref[s] = (o_in + o_st).astype(o_ref.dtype)
    upd = lax.dot_general(v, kw, (((0,), (0,)), ((), ())), preferred_element_type=F32)
    rs = lax.broadcasted_iota(jnp.int32, (GLA_V, GLA_QK), 0) // GLA_DV
    cs = lax.broadcasted_iota(jnp.int32, (GLA_V, GLA_QK), 1) // GLA_DK
    st_ref[s] = st * jnp.exp(total) + jnp.where(rs == cs, upd, 0.0)


def _gla_body(qf_ref, lf_ref, qb_ref, lb_ref, gw_ref, gb_ref, of_ref, ob_ref, stf, stb):
    @pl.when(pl.program_id(1) == 0)
    def _():
        stf[...] = jnp.zeros_like(stf)
        stb[...] = jnp.zeros_like(stb)

    for s in range(SCAN_SAMPLES):
        _gla_dir(qf_ref, lf_ref, gw_ref, gb_ref, stf, of_ref, s=s, d=0, reverse=False)
        _gla_dir(qb_ref, lb_ref, gw_ref, gb_ref, stb, ob_ref, s=s, d=1, reverse=True)


def gla_scan(qkv, lr, gate_w, gate_b):
    bsz = qkv.shape[0]
    nc = NTOK // GLA_CHUNK
    nlat = SEQ // GLA_CHUNK
    fwd_chunk = lambda s: (s + nlat) % nc
    bwd_chunk = lambda s: nc - 1 - s
    ns = SCAN_SAMPLES
    q_spec = lambda cm: pl.BlockSpec((ns, GLA_CHUNK, qkv.shape[-1]), lambda b, s: (b, cm(s), 0))
    l_spec = lambda cm: pl.BlockSpec((ns, GLA_CHUNK, 2 * GLA_RANK), lambda b, s: (b, cm(s), 0))
    o_spec = lambda cm: pl.BlockSpec((ns, GLA_CHUNK, GLA_V), lambda b, s: (b, cm(s), 0))
    o_shape = jax.ShapeDtypeStruct((bsz, NTOK, GLA_V), BF16)
    return pl.pallas_call(
        _gla_body,
        name='gla_scan',
        grid=(bsz // ns, nc),
        in_specs=[q_spec(fwd_chunk), l_spec(fwd_chunk), q_spec(bwd_chunk), l_spec(bwd_chunk),
                  pl.BlockSpec((2, GLA_RANK, GLA_QK), lambda b, s: (0, 0, 0)),
                  pl.BlockSpec((2, GLA_QK), lambda b, s: (0, 0))],
        out_specs=[o_spec(fwd_chunk), o_spec(bwd_chunk)],
        out_shape=[o_shape, o_shape],
        scratch_shapes=[pltpu.VMEM((ns, GLA_V, GLA_QK), F32), pltpu.VMEM((ns, GLA_V, GLA_QK), F32)],
        compiler_params=_cparams(("parallel", "arbitrary")),
    )(qkv, lr, qkv, lr, gate_w.astype(F32), gate_b.astype(F32))


def _gla_merge_body(of_ref, ob_ref, r_ref, nw_ref, o_ref):
    o = of_ref[0].astype(F32) + ob_ref[0].astype(F32)
    r = r_ref[0].astype(F32)
    for h in range(GLA_H):
        sl = slice(h * GLA_DV, (h + 1) * GLA_DV)
        oh = o[:, sl]
        ms = jnp.mean(oh * oh, axis=-1, keepdims=True)
        o_ref[0, :, sl] = (oh * lax.rsqrt(ms + NORM_EPS) * nw_ref[:, sl] * _silu(r[:, sl])).astype(o_ref.dtype)


def gla_merge_stream(of, ob, r, norm_w):
    bsz = of.shape[0]
    tile = pl.BlockSpec((1, ROW_TILE, GLA_V), lambda i, j: (i, j, 0))
    return pl.pallas_call(
        _gla_merge_body,
        name='gla_merge',
        grid=(bsz, N_LAT_TILES),
        in_specs=[tile, tile, tile, pl.BlockSpec((1, GLA_V), lambda i, j: (0, 0))],
        out_specs=tile,
        out_shape=jax.ShapeDtypeStruct((bsz, SEQ, GLA_V), BF16),
        compiler_params=_cparams(("parallel", "parallel")),
    )(of, ob, r, norm_w.reshape(1, GLA_V))


RG_TILE = 8


def _gelu_tanh(x):
    return 0.5 * x * (1.0 + jnp.tanh(math.sqrt(2.0 / math.pi) * (x + 0.044715 * x * x * x)))


def _rg_scan_block(a_s, x_s, h_s, base, carry, reverse):
    n_tiles = ROW_TILE // RG_TILE
    row = lax.broadcasted_iota(jnp.int32, (RG_TILE, D_RG), 0)

    def tile_step(i, h_prev):
        t = (n_tiles - 1 - i) if reverse else i
        r0 = pl.multiple_of(t * RG_TILE, RG_TILE)
        a = a_s[pl.ds(r0, RG_TILE), :]
        x = x_s[pl.ds(r0, RG_TILE), :]
        for s in (1, 2, 4):
            if reverse:
                ok = row < RG_TILE - s
                shift = RG_TILE - s
            else:
                ok = row >= s
                shift = s
            a_sh = jnp.where(ok, pltpu.roll(a, shift, 0), 1.0)
            x_sh = jnp.where(ok, pltpu.roll(x, shift, 0), 0.0)
            x = a * x_sh + x
            a = a * a_sh
        h = x + a * h_prev
        h_s[pl.ds(base + r0, RG_TILE), :] = h
        edge = 0 if reverse else RG_TILE - 1
        return jnp.broadcast_to(h[edge:edge + 1, :], (RG_TILE, D_RG))

    return lax.fori_loop(0, n_tiles, tile_step, carry)


def _rglru_body(u_ref, g_ref, w_ref, b_ref, c_ref, o_ref, hf_s, a_s, x_s, hb_s):
    n_blocks = NTOK // ROW_TILE
    fwd_order = list(range(N_LAT_TILES, n_blocks)) + list(range(N_LAT_TILES))
    bwd_order = list(range(n_blocks - 1, N_LAT_TILES - 1, -1)) + list(range(N_LAT_TILES - 1, -1, -1))

    def gates(blk, d):
        ub = u_ref[0, blk * ROW_TILE:(blk + 1) * ROW_TILE, :]
        z = jnp.dot(ub, w_ref[:, 2 * d * D_RG:2 * (d + 1) * D_RG], preferred_element_type=F32)
        z = z + b_ref[:, 2 * d * D_RG:2 * (d + 1) * D_RG]
        r = 1.0 / (1.0 + jnp.exp(-z[:, :D_RG]))
        i = 1.0 / (1.0 + jnp.exp(-z[:, D_RG:]))
        a = jnp.exp(c_ref[d:d + 1, :] * r)
        a_s[...] = a
        x_s[...] = jnp.sqrt(1.0 - a * a) * i * ub.astype(F32)

    carry = jnp.zeros((RG_TILE, D_RG), F32)
    for blk in fwd_order:
        gates(blk, 0)
        carry = _rg_scan_block(a_s, x_s, hf_s, blk * ROW_TILE, carry, reverse=False)
    carry = jnp.zeros((RG_TILE, D_RG), F32)
    for blk in bwd_order:
        gates(blk, 1)
        carry = _rg_scan_block(a_s, x_s, hb_s, 0, carry, reverse=True)
        rows = slice(blk * ROW_TILE, (blk + 1) * ROW_TILE)
        gate = g_ref[0, rows, :].astype(F32)
        o_ref[0, rows, :] = ((hf_s[rows, :] + hb_s[...]) * _gelu_tanh(gate)).astype(o_ref.dtype)


def rglru_stream(u, gate, w_a, b_a, w_x, b_x, lam):
    bsz = u.shape[0]
    eye = jnp.eye(RG_BLOCKS, dtype=F32)
    dense = lambda w: jnp.einsum('nio,nm->nimo', w, eye).reshape(D_RG, D_RG)
    w_cat = jnp.concatenate([dense(w_a[0]), dense(w_x[0]), dense(w_a[1]), dense(w_x[1])], axis=1).astype(BF16)
    b_cat = jnp.concatenate([b_a[0], b_x[0], b_a[1], b_x[1]]).astype(F32).reshape(1, 4 * D_RG)
    c = -RG_C * jax.nn.softplus(-lam.astype(F32))
    seq = pl.BlockSpec((1, NTOK, D_RG), lambda i: (i, 0, 0))
    return pl.pallas_call(
        _rglru_body,
        name='rglru',
        grid=(bsz,),
        in_specs=[seq, seq,
                  pl.BlockSpec((D_RG, 4 * D_RG), lambda i: (0, 0)),
                  pl.BlockSpec((1, 4 * D_RG), lambda i: (0, 0)),
                  pl.BlockSpec((2, D_RG), lambda i: (0, 0))],
        out_specs=seq,
        out_shape=jax.ShapeDtypeStruct((bsz, NTOK, D_RG), BF16),
        scratch_shapes=[pltpu.VMEM((NTOK, D_RG), F32), pltpu.VMEM((ROW_TILE, D_RG), F32),
                        pltpu.VMEM((ROW_TILE, D_RG), F32), pltpu.VMEM((ROW_TILE, D_RG), F32)],
        compiler_params=_cparams(("parallel",)),
    )(u, gate, w_cat, b_cat, c)


def odd_layer_mixer_pallas(xs, mods, norm_w, w_in, w_out, gla_args, rg_args):
    bsz = xs.shape[0]
    gate_w, gate_b, gla_norm_w = gla_args
    rg_conv_w, rg_conv_b, w_a, b_a, w_x, b_x, lam = rg_args
    h = norm_mod(xs, norm_w, mods, 0, 1)
    h = jnp.concatenate([to_col_major(h[:, :SEQ]), h[:, SEQ:]], axis=1)
    nqk, nv = GLA_QK, GLA_V
    r0 = 2 * nqk + nv + 2 * GLA_RANK
    w = jnp.concatenate([w_in[:, :2 * nqk + nv], w_in[:, r0:r0 + nv], w_in[:, GLA_IN:],
                         w_in[:, 2 * nqk + nv:r0]], axis=1).astype(BF16)
    qkv_w = 2 * nqk + nv
    splits = ((0, qkv_w), (qkv_w, nv), (qkv_w + nv, D_RG), (qkv_w + nv + D_RG, D_RG),
              (qkv_w + nv + 2 * D_RG, 2 * GLA_RANK))
    qkv, r, u_raw, gate, lr = mm_split(h.reshape(bsz * NTOK, D_MODEL), w, splits, (BF16, BF16, BF16, BF16, F32))
    to3 = lambda a: a.reshape(bsz, NTOK, a.shape[-1])
    of, ob = gla_scan(to3(qkv), to3(lr), gate_w, gate_b.reshape(2, GLA_QK))
    a_l = gla_merge_stream(of, ob, to3(r), gla_norm_w)
    u = dwconv_stream(to3(u_raw), rg_conv_w, rg_conv_b, act=False)
    r_l = rglru_stream(u, to3(gate), w_a, b_a, w_x, b_x, lam)[:, :SEQ]
    wo = w_out.astype(BF16)
    return mm_resid([from_col_major(a_l), from_col_major(r_l)], [wo[:GLA_V], wo[GLA_V:]], xs, mods, 2, N_LAT_TILES)


def to_col_major(x):
    b, n, d = x.shape
    rows = n // GRID_W
    return x.reshape(b, rows, GRID_W, d).transpose(0, 2, 1, 3).reshape(b, n, d)


def from_col_major(x):
    b, n, d = x.shape
    rows = n // GRID_W
    return x.reshape(b, GRID_W, rows, d).transpose(0, 2, 1, 3).reshape(b, n, d)


def kernel(x, c, ctx, c_ctx, ada_w, ada_b, norm1_w, norm2_w, ev_w_in, ev_w_out, hy_conv_w, hy_conv_b, hy_fw0, hy_fb0, hy_fw1, hy_fb1, hy_fw2, hy_fb2, hy_fw3, hy_freq, hy_bias, ssd_conv_w, ssd_conv_b, ssd_dt_bias, ssd_a_log, ssd_d, ssd_norm_w, od_w_in, od_w_out, gla_gate_w, gla_gate_b, gla_norm_w, rg_conv_w, rg_conv_b, rg_w_a, rg_b_a, rg_w_x, rg_b_x, rg_lambda, router_w, router_b, moe_w_gate, moe_w_up, moe_w_down, sh_w_gate, sh_w_up, sh_w_down, final_norm_w):
    xs = jnp.concatenate([x, ctx], axis=1)
    for i in range(DEPTH):
        last = i == DEPTH - 1
        j = i // 2
        mods = adaln_table(c, c_ctx, ada_w[i], ada_b[i])
        if i % 2 == 0:
            hy_filt = (hy_fw0[j], hy_fb0[j], hy_fw1[j], hy_fb1[j], hy_fw2[j], hy_fb2[j], hy_fw3[j], hy_freq[j])
            xs = even_layer_mixer(xs, mods, norm1_w[i], ev_w_in[j], ev_w_out[j], hy_conv_w[j], hy_conv_b[j],
                                  hy_filt, hy_bias[j], ssd_conv_w[j], ssd_conv_b[j], ssd_dt_bias[j],
                                  ssd_a_log[j], ssd_d[j], ssd_norm_w[j])
        else:
            gla_args = (gla_gate_w[j], gla_gate_b[j], gla_norm_w[j])
            rg_args = (rg_conv_w[j], rg_conv_b[j], rg_w_a[j], rg_b_a[j], rg_w_x[j], rg_b_x[j], rg_lambda[j])
            xs = odd_layer_mixer_pallas(xs, mods, norm1_w[i], od_w_in[j], od_w_out[j], gla_args, rg_args)
        n_tiles = N_LAT_TILES if last else N_ROW_TILES
        xs = moe_layer(xs, norm2_w[i], mods, router_w[i], router_b[i], moe_w_gate, moe_w_up, moe_w_down,
                       sh_w_gate[i], sh_w_up[i], sh_w_down[i], n_tiles, i)
    return final_norm(xs, final_norm_w)
```

```python
import functools
import math

import jax
import jax.numpy as jnp
from jax import lax
from jax.experimental import pallas as pl
from jax.experimental.pallas import tpu as pltpu

D_MODEL = 1024
BATCH = 16
SEQ = 2048
DEPTH = 2

CTX_LEN = 256
GRID_W = 64
NORM_EPS = 1e-6

D_HY = D_MODEL // 2
HY_ORDER = 2
HY_SHORT = 3
HY_BANDS = 16
HY_EMB = 1 + 2 * HY_BANDS
HY_FF = 64
HY_DECAY_PCT_LO = 0.3
HY_DECAY_PCT_HI = 1.5
HY_DECAY_TARGET = 1e-2
HY_IN = 3 * D_HY

D_SSM = D_MODEL // 2
SSD_P = 64
SSD_H = D_SSM // SSD_P
SSD_G = 2
SSD_HPG = SSD_H // SSD_G
SSD_N = 128
SSD_CONV = 4
SSD_CHUNK = 128
SSD_XBC = D_SSM + 2 * SSD_G * SSD_N
SSD_IN = D_SSM + SSD_XBC + 2 * SSD_H
EV_IN = HY_IN + SSD_IN
EV_MIX = D_HY + D_SSM

GLA_H = 4
GLA_DV = (D_MODEL // 2) // GLA_H
GLA_DK = GLA_DV // 2
GLA_RANK = 16
GLA_GATE_NORM = 16.0
GLA_CHUNK = 64
GLA_IN = 2 * GLA_H * GLA_DK + 2 * GLA_H * GLA_DV + 2 * GLA_RANK

D_RG = D_MODEL // 2
RG_BLOCKS = 8
RG_BW = D_RG // RG_BLOCKS
RG_CONV = 4
RG_C = 8.0
RG_IN = 2 * D_RG
OD_IN = GLA_IN + RG_IN
OD_MIX = GLA_H * GLA_DV + D_RG

MOE_EXPERTS = 64
MOE_TOPK = 8
MOE_D_EXPERT = 256
MOE_D_SHARED = 256
MOE_SCALE = 2.5
MOE_BLOCK = 512

F32 = jnp.float32
BF16 = jnp.bfloat16
HIGHEST = lax.Precision.HIGHEST

NTOK = SEQ + CTX_LEN
ROW_TILE = 256
N_ROW_TILES = NTOK // ROW_TILE
N_LAT_TILES = SEQ // ROW_TILE

VMEM_LIMIT = 48 * 1024 * 1024
VMEM_LIMIT_BIG = 56 * 1024 * 1024


def _cparams(sem, limit=VMEM_LIMIT):
    return pltpu.CompilerParams(dimension_semantics=sem, vmem_limit_bytes=limit)


def _pick_tile(n, pref):
    t = min(n, pref)
    while n % t:
        t //= 2
    return t


def _silu(x):
    return x / (1.0 + jnp.exp(-x))


def _softplus(x):
    return jnp.maximum(x, 0.0) + jnp.log(1.0 + jnp.exp(-jnp.abs(x)))


def _mm_bf16_body(a_ref, w_ref, o_ref):
    o_ref[...] = jnp.dot(a_ref[...].astype(BF16), w_ref[...].astype(BF16),
                         preferred_element_type=F32).astype(o_ref.dtype)


def _mm_f32_body(a_ref, w_ref, o_ref):
    o_ref[...] = jnp.dot(a_ref[...], w_ref[...], preferred_element_type=F32,
                         precision=HIGHEST).astype(o_ref.dtype)


def pmatmul(a, w, *, exact=False, out_dtype=F32, tm=512, tn=None):
    m, k = a.shape
    n = w.shape[1]
    tm = _pick_tile(m, tm)
    tn = n if tn is None else _pick_tile(n, tn)
    body = _mm_f32_body if exact else _mm_bf16_body
    return pl.pallas_call(
        body,
        name='mm',
        grid=(m // tm, n // tn),
        in_specs=[pl.BlockSpec((tm, k), lambda i, j: (i, 0)),
                  pl.BlockSpec((k, tn), lambda i, j: (0, j))],
        out_specs=pl.BlockSpec((tm, tn), lambda i, j: (i, j)),
        out_shape=jax.ShapeDtypeStruct((m, n), out_dtype),
        compiler_params=_cparams(("parallel", "parallel")),
    )(a, w)


def _mm_split_body(a_ref, w_ref, *o_refs, splits):
    a = a_ref[...]
    for o_ref, (start, width) in zip(o_refs, splits):
        o_ref[...] = jnp.dot(a, w_ref[:, start:start + width],
                             preferred_element_type=F32).astype(o_ref.dtype)


def mm_split(a, w, splits, dtypes, tm=512):
    m, k = a.shape
    n = w.shape[1]
    tm = _pick_tile(m, tm)
    return pl.pallas_call(
        functools.partial(_mm_split_body, splits=tuple(splits)),
        name='mm_split',
        grid=(m // tm,),
        in_specs=[pl.BlockSpec((tm, k), lambda i: (i, 0)),
                  pl.BlockSpec((k, n), lambda i: (0, 0))],
        out_specs=[pl.BlockSpec((tm, wd), lambda i: (i, 0)) for _, wd in splits],
        out_shape=[jax.ShapeDtypeStruct((m, wd), dt) for (_, wd), dt in zip(splits, dtypes)],
        compiler_params=_cparams(("parallel",)),
    )(a, w)


MIX_SAMPLES = 4


def _stack_samples(ref):
    return jnp.concatenate([ref[s] for s in range(MIX_SAMPLES)], axis=0)


def _resid_store(o_ref, x_ref, g_ref, acc):
    for s in range(MIX_SAMPLES):
        o_ref[s] = x_ref[s] + g_ref[s] * acc[s * ROW_TILE:(s + 1) * ROW_TILE]


def _mm_resid_body(*refs, n_pairs):
    a_refs = refs[:n_pairs]
    w_refs = refs[n_pairs:2 * n_pairs]
    x_ref, g_ref, o_ref = refs[2 * n_pairs:]
    acc = jnp.dot(_stack_samples(a_refs[0]), w_refs[0][...], preferred_element_type=F32)
    for a_ref, w_ref in zip(a_refs[1:], w_refs[1:]):
        acc = acc + jnp.dot(_stack_samples(a_ref), w_ref[...], preferred_element_type=F32)
    _resid_store(o_ref, x_ref, g_ref, acc)


def _gate_spec(gate_idx, d):
    return pl.BlockSpec((MIX_SAMPLES, None, None, 1, d), lambda i, j: (i, 1 - j // N_LAT_TILES, gate_idx, 0, 0))


def _mix_tile(width):
    return pl.BlockSpec((MIX_SAMPLES, ROW_TILE, width), lambda i, j: (i, j, 0))


def mm_resid(a_list, w_list, xs, mods, gate_idx, n_tiles):
    b, nt, d = xs.shape
    n_pairs = len(a_list)
    in_specs = [_mix_tile(a.shape[-1]) for a in a_list]
    in_specs += [pl.BlockSpec(w.shape, lambda i, j: (0, 0)) for w in w_list]
    in_specs += [_mix_tile(d), _gate_spec(gate_idx, d)]
    return pl.pallas_call(
        functools.partial(_mm_resid_body, n_pairs=n_pairs),
        name='mm_resid',
        grid=(b // MIX_SAMPLES, n_tiles),
        in_specs=in_specs,
        out_specs=_mix_tile(d),
        out_shape=jax.ShapeDtypeStruct(xs.shape, F32),
        input_output_aliases={2 * n_pairs: 0},
        compiler_params=_cparams(("parallel", "parallel")),
    )(*a_list, *w_list, xs, mods)


def _even_out_body(hy_ref, yf_ref, yb_ref, xs_ref, z_ref, d_ref, nw_ref, w1_ref, w2_ref, x_ref, g_ref, o_ref):
    gw = D_SSM // SSD_G
    merged = []
    for s in range(MIX_SAMPLES):
        y = yf_ref[s].astype(F32) + yb_ref[s].astype(F32) + d_ref[...] * xs_ref[s].astype(F32)
        g = y * _silu(z_ref[s].astype(F32))
        groups = []
        for k in range(SSD_G):
            gk = g[:, k * gw:(k + 1) * gw]
            ms = jnp.mean(gk * gk, axis=-1, keepdims=True)
            groups.append(gk * lax.rsqrt(ms + NORM_EPS) * nw_ref[:, k * gw:(k + 1) * gw])
        merged.append(jnp.concatenate(groups, axis=1).astype(BF16))
    acc = (jnp.dot(_stack_samples(hy_ref), w1_ref[...], preferred_element_type=F32)
           + jnp.dot(jnp.concatenate(merged, axis=0), w2_ref[...], preferred_element_type=F32))
    _resid_store(o_ref, x_ref, g_ref, acc)


def even_out_resid(y_hy, yf, yb, xbc, z, d_skip, norm_w, w_hy, w_ssd, xs, mods):
    b, nt, d = xs.shape
    d_chan = jnp.repeat(d_skip.astype(F32), SSD_P).reshape(1, D_SSM)
    vec = pl.BlockSpec((1, D_SSM), lambda i, j: (0, 0))
    wspec = pl.BlockSpec((D_SSM, d), lambda i, j: (0, 0))
    half = _mix_tile(D_SSM)
    return pl.pallas_call(
        _even_out_body,
        name='even_out',
        grid=(b // MIX_SAMPLES, N_ROW_TILES),
        in_specs=[half, half, half, half, half, vec, vec, wspec, wspec, _mix_tile(d), _gate_spec(2, d)],
        out_specs=_mix_tile(d),
        out_shape=jax.ShapeDtypeStruct(xs.shape, F32),
        input_output_aliases={9: 0},
        compiler_params=_cparams(("parallel", "parallel")),
    )(y_hy, yf, yb, xbc, z, d_chan, norm_w.reshape(1, D_SSM), w_hy, w_ssd, xs, mods)


def _mod_spec(idx, d, b0=0):
    return pl.BlockSpec((None, None, None, 1, d), lambda i, j: (i + b0, 1 - j // N_LAT_TILES, idx, 0, 0))


def adaln_table(c, c_ctx, w, b):
    cv = jax.nn.silu(jnp.concatenate([c, c_ctx[None, :]], axis=0))
    cv = jnp.pad(cv, ((0, -cv.shape[0] % 8), (0, 0)))
    m = pmatmul(cv, w, exact=True, tn=1536)[:BATCH + 1] + b
    per_sample = m[:BATCH]
    ctx_row = jnp.broadcast_to(m[BATCH][None, :], per_sample.shape)
    return jnp.stack([ctx_row, per_sample], axis=1).reshape(BATCH, 2, 6, 1, D_MODEL)


def _norm_mod(x, w, shift, scale):
    ms = jnp.mean(x * x, axis=-1, keepdims=True)
    return (x * lax.rsqrt(ms + NORM_EPS) * w) * (1.0 + scale) + shift


def _norm_mod_body(x_ref, w_ref, sh_ref, sc_ref, o_ref):
    o_ref[0] = _norm_mod(x_ref[0], w_ref[...], sh_ref[...], sc_ref[...]).astype(o_ref.dtype)


def norm_mod(xs, w, mods, shift_idx, scale_idx):
    b, nt, d = xs.shape
    return pl.pallas_call(
        _norm_mod_body,
        name='norm_mod',
        grid=(b, nt // ROW_TILE),
        in_specs=[pl.BlockSpec((1, ROW_TILE, d), lambda i, j: (i, j, 0)),
                  pl.BlockSpec((1, d), lambda i, j: (0, 0)),
                  _mod_spec(shift_idx, d), _mod_spec(scale_idx, d)],
        out_specs=pl.BlockSpec((1, ROW_TILE, d), lambda i, j: (i, j, 0)),
        out_shape=jax.ShapeDtypeStruct(xs.shape, BF16),
        compiler_params=_cparams(("parallel", "parallel")),
    )(xs, w.reshape(1, d), mods, mods)


def _norm_mm_split_body(x_ref, nw_ref, sh_ref, sc_ref, w_ref, *o_refs, splits):
    h = jnp.concatenate([_norm_mod(x_ref[s], nw_ref[...], sh_ref[s], sc_ref[s]).astype(BF16)
                         for s in range(MIX_SAMPLES)], axis=0)
    for o_ref, (start, width) in zip(o_refs, splits):
        y = jnp.dot(h, w_ref[:, start:start + width], preferred_element_type=F32).astype(o_ref.dtype)
        for s in range(MIX_SAMPLES):
            o_ref[s] = y[s * ROW_TILE:(s + 1) * ROW_TILE]


def norm_mm_split(xs, norm_w, mods, w, splits, dtypes):
    b, nt, d = xs.shape
    n = w.shape[1]
    return pl.pallas_call(
        functools.partial(_norm_mm_split_body, splits=tuple(splits)),
        name='norm_mm_split',
        grid=(b // MIX_SAMPLES, nt // ROW_TILE),
        in_specs=[_mix_tile(d), pl.BlockSpec((1, d), lambda i, j: (0, 0)), _gate_spec(0, d), _gate_spec(1, d),
                  pl.BlockSpec((d, n), lambda i, j: (0, 0))],
        out_specs=[_mix_tile(wd) for _, wd in splits],
        out_shape=[jax.ShapeDtypeStruct((b, nt, wd), dt) for (_, wd), dt in zip(splits, dtypes)],
        compiler_params=_cparams(("parallel", "parallel")),
    )(xs, norm_w.reshape(1, d), mods, mods, w)


def _final_norm_body(x_ref, w_ref, o_ref):
    x = x_ref[0]
    ms = jnp.mean(x * x, axis=-1, keepdims=True)
    o_ref[0] = x * lax.rsqrt(ms + NORM_EPS) * w_ref[...]


def final_norm(xs, w):
    b, _, d = xs.shape
    return pl.pallas_call(
        _final_norm_body,
        name='final_norm',
        grid=(b, N_LAT_TILES),
        in_specs=[pl.BlockSpec((1, ROW_TILE, d), lambda i, j: (i, j, 0)),
                  pl.BlockSpec((1, d), lambda i, j: (0, 0))],
        out_specs=pl.BlockSpec((1, ROW_TILE, d), lambda i, j: (i, j, 0)),
        out_shape=jax.ShapeDtypeStruct((b, SEQ, d), F32),
        compiler_params=_cparams(("parallel", "parallel")),
    )(xs, w.reshape(1, d))


def _route_t_body(x_ref, w_ref, sh_ref, sc_ref, rwh_ref, rwl_ref, rb_ref, h_ref, idx_ref, wsel_ref, rank_ref, cnt_ref,
                  *, group_size):
    first = (pl.program_id(0) % group_size == 0) & (pl.program_id(1) == 0)

    @pl.when(first)
    def _():
        cnt_ref[...] = jnp.zeros_like(cnt_ref)

    h = _norm_mod(x_ref[0], w_ref[...], sh_ref[...], sc_ref[...])
    h_hi = h.astype(BF16)
    h_ref[0] = h_hi
    h_lo = (h - h_hi.astype(F32)).astype(BF16)
    nt = (((1,), (1,)), ((), ()))
    logits = (lax.dot_general(rwh_ref[...], h_hi, nt, preferred_element_type=F32)
              + lax.dot_general(rwh_ref[...], h_lo, nt, preferred_element_type=F32)
              + lax.dot_general(rwl_ref[...], h_hi, nt, preferred_element_type=F32))
    scores = 1.0 / (1.0 + jnp.exp(-logits))
    ne, tm = scores.shape
    expert = lax.broadcasted_iota(jnp.int32, (ne, tm), 0).astype(F32)
    slot = lax.broadcasted_iota(jnp.int32, (MOE_TOPK, tm), 0)
    sel = scores + rb_ref[...]
    picked = jnp.zeros((ne, tm), F32)
    hits = []
    idx_out = jnp.zeros((MOE_TOPK, tm), F32)
    w_out = jnp.zeros((MOE_TOPK, tm), F32)
    for k in range(MOE_TOPK):
        m = jnp.max(sel, axis=0, keepdims=True)
        ik = jnp.min(jnp.where(sel == m, expert, float(ne)), axis=0, keepdims=True)
        hit = expert == ik
        wk = jnp.sum(jnp.where(hit, scores, 0.0), axis=0, keepdims=True)
        sel = jnp.where(hit, -jnp.inf, sel)
        picked = picked + hit.astype(F32)
        hits.append(hit)
        idx_out = jnp.where(slot == k, ik, idx_out)
        w_out = jnp.where(slot == k, wk, w_out)
    wsum = jnp.sum(w_out, axis=0, keepdims=True)
    wsel_ref[0] = w_out / wsum * MOE_SCALE
    idx_ref[0] = idx_out.astype(jnp.int32)
    r_i = lax.broadcasted_iota(jnp.int32, (tm, tm), 0)
    c_i = lax.broadcasted_iota(jnp.int32, (tm, tm), 1)
    earlier = (r_i < c_i).astype(BF16)
    before = jnp.dot(picked.astype(BF16), earlier, preferred_element_type=F32) + cnt_ref[...]
    rank_out = jnp.zeros((MOE_TOPK, tm), F32)
    for k in range(MOE_TOPK):
        rk = jnp.sum(jnp.where(hits[k], before, 0.0), axis=0, keepdims=True)
        rank_out = jnp.where(slot == k, rk, rank_out)
    rank_ref[0] = rank_out.astype(jnp.int32)
    cnt_ref[...] = cnt_ref[...] + jnp.sum(picked, axis=1, keepdims=True)


def route_t(xs, w, mods, router_w, router_b, n_tiles, group_size):
    b, _, d = xs.shape
    rows = n_tiles * ROW_TILE
    rwt = router_w.T.astype(F32)
    rwt_hi, rwt_lo = _split_bf16(rwt)
    small = lambda dt: jax.ShapeDtypeStruct((b, MOE_TOPK, rows), dt)
    small_spec = pl.BlockSpec((1, MOE_TOPK, ROW_TILE), lambda i, j: (i, 0, j))
    return pl.pallas_call(
        functools.partial(_route_t_body, group_size=group_size),
        name='route',
        grid=(b, n_tiles),
        in_specs=[pl.BlockSpec((1, ROW_TILE, d), lambda i, j: (i, j, 0)),
                  pl.BlockSpec((1, d), lambda i, j: (0, 0)),
                  _mod_spec(3, d), _mod_spec(4, d),
                  pl.BlockSpec((MOE_EXPERTS, d), lambda i, j: (0, 0)),
                  pl.BlockSpec((MOE_EXPERTS, d), lambda i, j: (0, 0)),
                  pl.BlockSpec((MOE_EXPERTS, 1), lambda i, j: (0, 0))],
        out_specs=[pl.BlockSpec((1, ROW_TILE, d), lambda i, j: (i, j, 0)),
                   small_spec, small_spec, small_spec,
                   pl.BlockSpec((None, MOE_EXPERTS, 1), lambda i, j: (i // group_size, 0, 0))],
        out_shape=[jax.ShapeDtypeStruct((b, rows, d), BF16), small(jnp.int32), small(F32), small(jnp.int32),
                   jax.ShapeDtypeStruct((b // group_size, MOE_EXPERTS, 1), F32)],
        compiler_params=_cparams(("arbitrary", "arbitrary")),
    )(xs, w.reshape(1, d), mods, mods, rwt_hi, rwt_lo, router_b.astype(F32).reshape(MOE_EXPERTS, 1))


def _swiglu(x, wg, wu, wd):
    g = jnp.dot(x, wg, preferred_element_type=F32)
    u = jnp.dot(x, wu, preferred_element_type=F32)
    h = (_silu(g) * u).astype(BF16)
    return jnp.dot(h, wd, preferred_element_type=F32)


def _expert_body(be_ref, nu_ref, xa_ref, xb_ref, wg_ref, wu_ref, wd_ref, o_ref, wg_s, wu_s, wd_s):
    i = pl.program_id(0)
    used = i < nu_ref[0]

    @pl.when(used & ((i == 0) | (be_ref[i] != be_ref[jnp.maximum(i - 1, 0)])))
    def _():
        wg_s[...] = wg_ref[0].astype(BF16)
        wu_s[...] = wu_ref[0].astype(BF16)
        wd_s[...] = wd_ref[0].astype(BF16)

    @pl.when(used)
    def _():
        x = jnp.concatenate([xa_ref[...], xb_ref[...]], axis=1)
        o_ref[...] = _swiglu(x, wg_s[...], wu_s[...], wd_s[...]).astype(o_ref.dtype)

    @pl.when(jnp.logical_not(used))
    def _():
        o_ref[...] = jnp.zeros_like(o_ref)


def moe_experts(x_rows, block_e, n_used, wg, wu, wd, layer):
    rows, d = x_rows.shape
    n_blocks = rows // MOE_BLOCK
    f = wg.shape[-1]
    grid_spec = pltpu.PrefetchScalarGridSpec(
        num_scalar_prefetch=2,
        grid=(n_blocks,),
        in_specs=[
            pl.BlockSpec((MOE_BLOCK, d // 2), lambda i, be, nu: (i, 0)),
            pl.BlockSpec((MOE_BLOCK, d // 2), lambda i, be, nu: (i, 1)),
            pl.BlockSpec((None, 1, d, f), lambda i, be, nu: (layer, be[i], 0, 0)),
            pl.BlockSpec((None, 1, d, f), lambda i, be, nu: (layer, be[i], 0, 0)),
            pl.BlockSpec((None, 1, f, d), lambda i, be, nu: (layer, be[i], 0, 0)),
        ],
        out_specs=pl.BlockSpec((MOE_BLOCK, d), lambda i, be, nu: (i, 0)),
        scratch_shapes=[pltpu.VMEM((d, f), BF16), pltpu.VMEM((d, f), BF16), pltpu.VMEM((f, d), BF16)],
    )
    return pl.pallas_call(
        _expert_body,
        name='experts',
        grid_spec=grid_spec,
        out_shape=jax.ShapeDtypeStruct((rows, d), BF16),
        compiler_params=_cparams(("arbitrary",)),
    )(block_e, n_used, x_rows, x_rows, wg, wu, wd)


COMBINE_SAMPLES = 2


def _shared_expert_body(h_ref, wg_ref, wu_ref, wd_ref, o_ref):
    h = jnp.concatenate([h_ref[s] for s in range(MIX_SAMPLES)], axis=0)
    y = _swiglu(h, wg_ref[...], wu_ref[...], wd_ref[...]).astype(o_ref.dtype)
    for s in range(MIX_SAMPLES):
        o_ref[s] = y[s * ROW_TILE:(s + 1) * ROW_TILE]


def shared_expert(h, wg, wu, wd):
    b, rows, d = h.shape
    f = wg.shape[-1]
    return pl.pallas_call(
        _shared_expert_body,
        name='shared_expert',
        grid=(b // MIX_SAMPLES, rows // ROW_TILE),
        in_specs=[_mix_tile(d),
                  pl.BlockSpec((d, f), lambda i, j: (0, 0)),
                  pl.BlockSpec((d, f), lambda i, j: (0, 0)),
                  pl.BlockSpec((f, d), lambda i, j: (0, 0))],
        out_specs=_mix_tile(d),
        out_shape=jax.ShapeDtypeStruct(h.shape, BF16),
        compiler_params=_cparams(("parallel", "parallel")),
    )(h, wg, wu, wd)


def _shared_resid_body(sh_ref, pk_ref, ws_ref, x_ref, g_ref, o_ref):
    for s in range(COMBINE_SAMPLES):
        y = sh_ref[s].astype(F32)
        ws = ws_ref[s]
        for k in range(MOE_TOPK):
            y = y + ws[:, k:k + 1] * pk_ref[k, s].astype(F32)
        o_ref[s] = x_ref[s] + g_ref[s] * y


def shared_resid(shared, picked, wsel, xs, mods, n_tiles, b0):
    b, _, d = shared.shape
    ns = COMBINE_SAMPLES
    s0 = b0 // ns
    tile = pl.BlockSpec((ns, ROW_TILE, d), lambda i, j: (i, j, 0))
    xs_tile = pl.BlockSpec((ns, ROW_TILE, d), lambda i, j: (i + s0, j, 0))
    gate = pl.BlockSpec((ns, None, None, 1, d), lambda i, j: (i + s0, 1 - j // N_LAT_TILES, 5, 0, 0))
    return pl.pallas_call(
        _shared_resid_body,
        name='shared_resid',
        grid=(b // ns, n_tiles),
        in_specs=[tile,
                  pl.BlockSpec((MOE_TOPK, ns, ROW_TILE, d), lambda i, j: (0, i, j, 0)),
                  pl.BlockSpec((ns, ROW_TILE, MOE_TOPK), lambda i, j: (i, j, 0)),
                  xs_tile, gate],
        out_specs=xs_tile,
        out_shape=jax.ShapeDtypeStruct(xs.shape, F32),
        input_output_aliases={3: 0},
        compiler_params=_cparams(("parallel", "parallel")),
    )(shared, picked, wsel, xs, mods)


MOE_GROUPS = 2


def moe_layer(xs, norm_w, mods, router_w, router_b, w_gate, w_up, w_down, sh_gate, sh_up, sh_down, n_tiles, layer):
    bsz, _, d = xs.shape
    shared_w = (sh_gate.astype(BF16), sh_up.astype(BF16), sh_down.astype(BF16))
    b = bsz // MOE_GROUPS
    h, idx, wsel, rank, counts = route_t(xs, norm_w, mods, router_w, router_b, n_tiles, b)
    h_flat = h.reshape(-1, d)
    shared = shared_expert(h, *shared_w)
    wsel = jnp.swapaxes(wsel, 1, 2)
    for g in range(MOE_GROUPS):
        sl = slice(g * b, (g + 1) * b)
        xs = _moe_group(xs, mods, h_flat, shared[sl], idx[sl], wsel[sl], rank[sl], counts[g, :, 0], w_gate, w_up,
                        w_down, n_tiles, layer, g * b)
    return xs


def _moe_group(xs, mods, h_flat, shared, idx, wsel, rank, counts, w_gate, w_up, w_down, n_tiles, layer, b0):
    b, rows_per_sample, d = shared.shape
    n = b * rows_per_sample
    counts = counts.astype(jnp.int32)
    padded = (counts + MOE_BLOCK - 1) // MOE_BLOCK * MOE_BLOCK
    ends = jnp.cumsum(padded)
    starts = ends - padded
    nk = n * MOE_TOPK
    n_blocks = -(-nk // MOE_BLOCK) + MOE_EXPERTS
    rows = n_blocks * MOE_BLOCK
    n_pad = rows - nk
    e_iota = jnp.arange(MOE_EXPERTS, dtype=jnp.int32)
    dest = jnp.sum(jnp.where(idx[..., None] == e_iota, starts, 0), axis=-1) + rank
    blk_start = jnp.arange(n_blocks, dtype=jnp.int32) * MOE_BLOCK
    block_e = jnp.minimum(jnp.sum(ends[None, :] <= blk_start[:, None], axis=1), MOE_EXPERTS - 1).astype(jnp.int32)
    n_used = (ends[-1:] // MOE_BLOCK).astype(jnp.int32)
    pad = padded - counts
    cum_pad = jnp.cumsum(pad)
    m = jnp.arange(n_pad, dtype=jnp.int32)
    e_m = jnp.sum(cum_pad[None, :] <= m[:, None], axis=1)
    base = jnp.sum(jnp.where(jnp.minimum(e_m, MOE_EXPERTS - 1)[:, None] == e_iota,
                             starts + counts - (cum_pad - pad), 0), axis=1)
    pad_row = jnp.where(e_m < MOE_EXPERTS, base + m, ends[-1] + m - cum_pad[-1])
    tok0 = b0 * rows_per_sample
    tok = (tok0 + jnp.arange(b, dtype=jnp.int32)[:, None, None] * rows_per_sample
           + jnp.arange(rows_per_sample, dtype=jnp.int32)[None, None, :])
    tok = jnp.broadcast_to(tok, dest.shape).reshape(-1)
    _, row_tok = lax.sort((jnp.concatenate([dest.reshape(-1), pad_row]).astype(jnp.int32),
                           jnp.concatenate([tok, tok0 + m % n])), num_keys=1)
    x_rows = h_flat[row_tok]
    y_rows = moe_experts(x_rows, block_e, n_used, w_gate, w_up, w_down, layer)
    picked = y_rows[jnp.swapaxes(dest, 0, 1)]
    return shared_resid(shared, picked, wsel, xs, mods, n_tiles, b0)


def _dwconv_body(x_ref, w_ref, b_ref, o_ref, *, width, act):
    chunk = ROW_TILE
    n_chunks = NTOK // chunk
    first_of_seq = (0, N_LAT_TILES)
    last_of_seq = (N_LAT_TILES - 1, n_chunks - 1)
    tc = x_ref.shape[-1]
    halo = 16
    row = lax.broadcasted_iota(jnp.int32, (chunk, tc), 0)
    zero_row = jnp.zeros((1, tc), F32)
    for c in range(n_chunks):
        r0 = c * chunk
        cur = x_ref[0, r0:r0 + chunk, :].astype(F32)
        if c in first_of_seq:
            prev_last = zero_row
        else:
            prev_last = x_ref[0, r0 - halo:r0, :].astype(F32)[halo - 1:halo, :]
        if c in last_of_seq:
            next0 = next1 = zero_row
        else:
            nxt = x_ref[0, r0 + chunk:r0 + chunk + halo, :].astype(F32)
            next0, next1 = nxt[0:1, :], nxt[1:2, :]
        xm1 = jnp.where(row == 0, prev_last, pltpu.roll(cur, 1, 0))
        xp1 = jnp.where(row == chunk - 1, next0, pltpu.roll(cur, chunk - 1, 0))
        y = w_ref[0:1, :] * xm1 + w_ref[1:2, :] * cur + w_ref[2:3, :] * xp1 + b_ref[...]
        if width == 4:
            xp2 = jnp.where(row == chunk - 2, next0,
                            jnp.where(row == chunk - 1, next1, pltpu.roll(cur, chunk - 2, 0)))
            y = y + w_ref[3:4, :] * xp2
        if act:
            y = _silu(y)
        o_ref[0, c * chunk:(c + 1) * chunk, :] = y.astype(o_ref.dtype)


def dwconv_stream(x, w, b, act, tc=256):
    bsz, nt, c = x.shape
    width = w.shape[0]
    return pl.pallas_call(
        functools.partial(_dwconv_body, width=width, act=act),
        name='dwconv',
        grid=(bsz, c // tc),
        in_specs=[pl.BlockSpec((1, nt, tc), lambda i, j: (i, 0, j)),
                  pl.BlockSpec((width, tc), lambda i, j: (0, j)),
                  pl.BlockSpec((1, tc), lambda i, j: (0, j))],
        out_specs=pl.BlockSpec((1, nt, tc), lambda i, j: (i, 0, j)),
        out_shape=jax.ShapeDtypeStruct(x.shape, BF16),
        compiler_params=_cparams(("parallel", "parallel")),
    )(x, w, b.reshape(1, c))


HY_FB = 512
DFT_SPLIT = 64


def dft_matrices(n):
    t = jnp.arange(n, dtype=jnp.int32)[None, :]
    ka = jnp.arange(DFT_SPLIT, dtype=jnp.int32)[:, None]
    kb = jnp.arange(n // DFT_SPLIT, dtype=jnp.int32)[:, None] * DFT_SPLIT
    ang_a = (2.0 * math.pi / (2 * n)) * ((ka * t) % (2 * n)).astype(F32)
    ang_b = (2.0 * math.pi / (2 * n)) * ((kb * t) % (2 * n)).astype(F32)
    ca, sa = jnp.cos(ang_a)[None], jnp.sin(ang_a)[None]
    cb, sb = jnp.cos(ang_b)[:, None], jnp.sin(ang_b)[:, None]
    cos_kt = (ca * cb - sa * sb).reshape(n, n)
    sin_kt = (sa * cb + ca * sb).reshape(n, n)
    idx = jnp.arange(n, dtype=jnp.int32)
    nyq = jnp.where(idx % 2 == 0, 1.0, -1.0).astype(F32)
    fwd = jnp.concatenate([cos_kt, jnp.where(idx[:, None] == 0, nyq[None, :], -sin_kt)], axis=0)
    scale = jnp.where(idx == 0, 0.5, 1.0)[None, :] / n
    inv = jnp.concatenate([cos_kt * scale, jnp.where(idx[None, :] == 0, nyq[:, None], -sin_kt) * scale], axis=1)
    return fwd.astype(BF16), inv.astype(BF16)


def hyena_filter_taps(n, fw0, fb0, fw1, fb1, fw2, fb2, fw3, freq):
    pos = jnp.arange(n, dtype=F32)
    t = pos / max(n - 1, 1)
    bands = jnp.linspace(1e-4, HY_BANDS - 1, HY_BANDS, dtype=F32)
    ang = (2.0 * math.pi / n) * pos[:, None] * bands[None, :]
    feats = jnp.concatenate([t[:, None], jnp.cos(ang), -jnp.sin(ang)], axis=-1)
    h = jnp.sin(freq * (jnp.dot(feats, fw0, precision=HIGHEST) + fb0))
    h = jnp.sin(freq * (jnp.dot(h, fw1, precision=HIGHEST) + fb1))
    h = jnp.sin(freq * (jnp.dot(h, fw2, precision=HIGHEST) + fb2))
    h = pmatmul(h, fw3, exact=True).reshape(n, 2, HY_ORDER, D_HY)
    deltas = jnp.abs(jnp.linspace(math.log(HY_DECAY_PCT_LO) / HY_DECAY_TARGET,
                                  math.log(HY_DECAY_PCT_HI) / HY_DECAY_TARGET, D_HY, dtype=F32))
    h = h * jnp.exp(-t[:, None] * deltas)[:, None, None, :]
    h0 = h[:, 0]
    h1 = h[:, 1].at[0].set(0.0)
    norm = jnp.sum(jnp.abs(h0), axis=0, keepdims=True) + jnp.sum(jnp.abs(h1), axis=0, keepdims=True)
    h0 = (h0 / norm).reshape(n, HY_ORDER * D_HY)
    h1 = (h1 / norm).reshape(n, HY_ORDER * D_HY)
    return h0 + h1, h0 - h1


def _split_bf16(a):
    hi = a.astype(BF16)
    return hi, (a - hi.astype(F32)).astype(BF16)


def hyena_spectrum(fwd, hsum, hdiff, fb):
    n = hsum.shape[0]
    a = pmatmul(fwd, hsum.astype(BF16))
    bm = pmatmul(fwd, hdiff.astype(BF16))
    sr = a[:n]
    si = bm[n:]
    nyq = a[n]
    first = (jnp.arange(n) == 0)[:, None]
    p = sr
    q = jnp.where(first, 0.0, si)
    s = jnp.where(first, nyq[None, :], sr)
    spec = jnp.stack([p, q, s], axis=0).reshape(3, n // fb, fb, HY_ORDER, D_HY)
    return spec.transpose(3, 1, 0, 2, 4)


def _hyena_body(u_ref, fre_ref, fim_ref, gre_ref, gim_ref, sp_ref, bias_ref, prev_ref, o_ref,
                vin, acc, *, nf):
    del prev_ref
    o = pl.program_id(1)
    f = pl.program_id(2)
    c = D_HY

    @pl.when((o == 0) & (f == 0))
    def _():
        vin[...] = u_ref[0, :, 0:c]

    @pl.when(f == 0)
    def _():
        acc[...] = jnp.zeros_like(acc)

    v = vin[...]
    vr = jnp.dot(fre_ref[...], v, preferred_element_type=F32)
    vi = jnp.dot(fim_ref[...], v, preferred_element_type=F32)
    p, q, s = sp_ref[0], sp_ref[1], sp_ref[2]
    zr = (vr * p - vi * q).astype(BF16)
    zi = (vr * q + vi * s).astype(BF16)
    acc[...] += (jnp.dot(gre_ref[...], zr, preferred_element_type=F32)
                 + jnp.dot(gim_ref[...], zi, preferred_element_type=F32))

    @pl.when((o == 0) & (f == nf - 1))
    def _():
        z = u_ref[0, :, c:2 * c].astype(F32) * (acc[...] + bias_ref[0:1, :] * vin[...].astype(F32))
        vin[...] = z.astype(BF16)

    @pl.when((o == 1) & (f == nf - 1))
    def _():
        y = u_ref[0, :, 2 * c:3 * c].astype(F32) * (acc[...] + bias_ref[1:2, :] * vin[...].astype(F32))
        o_ref[0] = y.astype(o_ref.dtype)


def hyena_long_conv(u, fwd_bf16, inv_bf16, spec, bias, n, row_block, prev_out):
    bsz = u.shape[0]
    fb = spec.shape[3]
    nf = n // fb
    out_shape = jax.ShapeDtypeStruct((bsz, NTOK, D_HY), BF16)
    if prev_out is None:
        prev_out = jnp.zeros(out_shape.shape, BF16)
    args = [u, fwd_bf16, fwd_bf16, inv_bf16, inv_bf16, spec, bias, prev_out]
    aliases = {7: 0}
    return pl.pallas_call(
        functools.partial(_hyena_body, nf=nf),
        name='hyena',
        grid=(bsz, HY_ORDER, nf),
        in_specs=[pl.BlockSpec((1, n, 3 * D_HY), lambda b, o, f: (b, row_block, 0)),
                  pl.BlockSpec((fb, n), lambda b, o, f: (f, 0)),
                  pl.BlockSpec((fb, n), lambda b, o, f: (nf + f, 0)),
                  pl.BlockSpec((n, fb), lambda b, o, f: (0, f)),
                  pl.BlockSpec((n, fb), lambda b, o, f: (0, nf + f)),
                  pl.BlockSpec((None, None, 3, fb, D_HY), lambda b, o, f: (o, f, 0, 0, 0)),
                  pl.BlockSpec((HY_ORDER, D_HY), lambda b, o, f: (0, 0)),
                  pl.BlockSpec(memory_space=pl.ANY)],
        out_specs=pl.BlockSpec((1, n, D_HY), lambda b, o, f: (b, row_block, 0)),
        out_shape=out_shape,
        scratch_shapes=[pltpu.VMEM((n, D_HY), BF16), pltpu.VMEM((n, D_HY), F32)],
        input_output_aliases=aliases,
        compiler_params=_cparams(("parallel", "arbitrary", "arbitrary"), VMEM_LIMIT_BIG),
    )(*args)


def hyena_mixer_stream(p_hy, conv_w, conv_b, filt, bias):
    u = dwconv_stream(p_hy, conv_w, conv_b, act=False)
    out = None
    for n, row_block in ((SEQ, 0), (CTX_LEN, SEQ // CTX_LEN)):
        fb = min(HY_FB, n)
        fwd, inv = dft_matrices(n)
        hsum, hdiff = hyena_filter_taps(n, *filt)
        spec = hyena_spectrum(fwd, hsum, hdiff, fb)
        out = hyena_long_conv(u, fwd, inv, spec, bias, n, row_block, out)
    return out


SCAN_SAMPLES = 8


def _tri(n, kind):
    r = lax.broadcasted_iota(jnp.int32, (n, n), 0)
    c = lax.broadcasted_iota(jnp.int32, (n, n), 1)
    return (c <= r) if kind == 'lower' else (c >= r)


def _ssd_dir(xbc_ref, dt_ref, dtt_ref, bias_r, bias_c, a_r, a_c, st_ref, y_ref, *, s, d, reverse):
    q = SSD_CHUNK
    nh = SSD_H
    gw = SSD_HPG * SSD_P
    lower = _tri(q, 'lower')
    upper = _tri(q, 'upper')
    lower_f = lower.astype(F32)
    upper_f = upper.astype(F32)
    dt_col = _softplus(dt_ref[s] + bias_r)
    dt_row = _softplus(dtt_ref[s] + bias_c)
    da_col = dt_col * a_r
    da_row = dt_row * a_c
    if not reverse:
        acs_col = jnp.dot(lower_f, da_col, preferred_element_type=F32, precision=HIGHEST)
        acs_row = jnp.dot(da_row, upper_f, preferred_element_type=F32, precision=HIGHEST)
        mask = lower
        edge = q - 1
    else:
        acs_col = jnp.dot(upper_f, da_col, preferred_element_type=F32, precision=HIGHEST)
        acs_row = jnp.dot(da_row, lower_f, preferred_element_type=F32, precision=HIGHEST)
        mask = upper
        edge = 0
    h0 = d * nh
    hh = lax.broadcasted_iota(jnp.int32, (2 * nh, nh * SSD_P), 0)
    cc = lax.broadcasted_iota(jnp.int32, (2 * nh, nh * SSD_P), 1) // SSD_P
    expand = (hh == cc + h0).astype(F32)
    acs_c = jnp.dot(acs_col, expand, preferred_element_type=F32, precision=HIGHEST)
    dt_c = jnp.dot(dt_col, expand, preferred_element_type=F32, precision=HIGHEST)
    total_c = acs_c[edge:edge + 1, :]
    e_in_c = jnp.exp(acs_c)
    w_end_c = jnp.exp(total_c - acs_c) * dt_c
    dec_c = jnp.exp(total_c)
    xs = xbc_ref[s, :, 0:D_SSM]
    xs_f = xs.astype(F32)
    for g in range(SSD_G):
        bm = xbc_ref[s, :, D_SSM + g * SSD_N:D_SSM + (g + 1) * SSD_N]
        cm = xbc_ref[s, :, D_SSM + SSD_G * SSD_N + g * SSD_N:D_SSM + SSD_G * SSD_N + (g + 1) * SSD_N]
        cb = lax.dot_general(cm, bm, (((1,), (1,)), ((), ())), preferred_element_type=F32)
        lws = []
        for k in range(SSD_HPG):
            h = h0 + g * SSD_HPG + k
            seg = acs_col[:, h:h + 1] - acs_row[h:h + 1, :]
            decay = jnp.exp(jnp.where(mask, seg, -jnp.inf))
            lws.append((cb * decay * dt_row[h:h + 1, :]).astype(BF16))
        lw = jnp.concatenate(lws, axis=1)
        xg = xs[:, g * gw:(g + 1) * gw]
        rb = lax.broadcasted_iota(jnp.int32, (SSD_HPG * q, gw), 0) // q
        cbk = lax.broadcasted_iota(jnp.int32, (SSD_HPG * q, gw), 1) // SSD_P
        x_bd = jnp.where(rb == cbk, jnp.concatenate([xg] * SSD_HPG, axis=0), jnp.zeros((), BF16))
        y_in = jnp.dot(lw, x_bd, preferred_element_type=F32)
        st = st_ref[s, g]
        y_st = jnp.dot(cm, st.astype(BF16), preferred_element_type=F32) * e_in_c[:, g * gw:(g + 1) * gw]
        y_ref[s, :, g * gw:(g + 1) * gw] = (y_in + y_st).astype(y_ref.dtype)
        xw = (xs_f[:, g * gw:(g + 1) * gw] * w_end_c[:, g * gw:(g + 1) * gw]).astype(BF16)
        upd = lax.dot_general(bm, xw, (((0,), (0,)), ((), ())), preferred_element_type=F32)
        st_ref[s, g] = st * dec_c[:, g * gw:(g + 1) * gw] + upd


def _ssd_body(xf_ref, dtf_ref, dttf_ref, xb_ref, dtb_ref, dttb_ref, bias_r, bias_c, a_r, a_c,
              yf_ref, yb_ref, stf, stb):
    @pl.when(pl.program_id(1) == 0)
    def _():
        stf[...] = jnp.zeros_like(stf)
        stb[...] = jnp.zeros_like(stb)

    for s in range(SCAN_SAMPLES):
        _ssd_dir(xf_ref, dtf_ref, dttf_ref, bias_r[...], bias_c[...], a_r[...], a_c[...], stf, yf_ref,
                 s=s, d=0, reverse=False)
        _ssd_dir(xb_ref, dtb_ref, dttb_ref, bias_r[...], bias_c[...], a_r[...], a_c[...], stb, yb_ref,
                 s=s, d=1, reverse=True)


def ssd_scan(xbc, dt, dt_bias, a_log):
    bsz = xbc.shape[0]
    nc = NTOK // SSD_CHUNK
    nlat = SEQ // SSD_CHUNK
    dtt = jnp.swapaxes(dt, 1, 2)
    fwd_chunk = lambda s: (s + nlat) % nc
    bwd_chunk = lambda s: nc - 1 - s
    a = -jnp.exp(a_log.astype(F32)).reshape(1, 2 * SSD_H)
    bias = dt_bias.astype(F32).reshape(1, 2 * SSD_H)
    ns = SCAN_SAMPLES
    x_spec = lambda cm: pl.BlockSpec((ns, SSD_CHUNK, SSD_XBC), lambda b, s: (b, cm(s), 0))
    dt_spec = lambda cm: pl.BlockSpec((ns, SSD_CHUNK, 2 * SSD_H), lambda b, s: (b, cm(s), 0))
    dtt_spec = lambda cm: pl.BlockSpec((ns, 2 * SSD_H, SSD_CHUNK), lambda b, s: (b, 0, cm(s)))
    y_spec = lambda cm: pl.BlockSpec((ns, SSD_CHUNK, D_SSM), lambda b, s: (b, cm(s), 0))
    row = pl.BlockSpec((1, 2 * SSD_H), lambda b, s: (0, 0))
    col = pl.BlockSpec((2 * SSD_H, 1), lambda b, s: (0, 0))
    y_shape = jax.ShapeDtypeStruct((bsz, NTOK, D_SSM), BF16)
    gw = SSD_HPG * SSD_P
    return pl.pallas_call(
        _ssd_body,
        name='ssd_scan',
        grid=(bsz // ns, nc),
        in_specs=[x_spec(fwd_chunk), dt_spec(fwd_chunk), dtt_spec(fwd_chunk),
                  x_spec(bwd_chunk), dt_spec(bwd_chunk), dtt_spec(bwd_chunk),
                  row, col, row, col],
        out_specs=[y_spec(fwd_chunk), y_spec(bwd_chunk)],
        out_shape=[y_shape, y_shape],
        scratch_shapes=[pltpu.VMEM((ns, SSD_G, SSD_N, gw), F32), pltpu.VMEM((ns, SSD_G, SSD_N, gw), F32)],
        compiler_params=_cparams(("parallel", "arbitrary")),
    )(xbc, dt, dtt, xbc, dt, dtt, bias, bias.reshape(-1, 1), a, a.reshape(-1, 1))


def even_layer_mixer(xs, mods, norm_w, w_in, w_out, hy_conv_w, hy_conv_b, hy_filt, hy_bias,
                     ssd_conv_w, ssd_conv_b, ssd_dt_bias, ssd_a_log, ssd_d, ssd_norm_w):
    splits = ((0, HY_IN), (HY_IN, D_SSM), (HY_IN + D_SSM, SSD_XBC), (HY_IN + D_SSM + SSD_XBC, 2 * SSD_H))
    p_hy, z, xbc_raw, dt = norm_mm_split(xs, norm_w, mods, w_in.astype(BF16), splits, (BF16, BF16, BF16, F32))
    y_hy = hyena_mixer_stream(p_hy, hy_conv_w, hy_conv_b, hy_filt, hy_bias)
    xbc = dwconv_stream(xbc_raw, ssd_conv_w, ssd_conv_b, act=True)
    yf, yb = ssd_scan(xbc, dt, ssd_dt_bias, ssd_a_log)
    wo = w_out.astype(BF16)
    return even_out_resid(y_hy, yf, yb, xbc, z, ssd_d, ssd_norm_w, wo[:D_HY], wo[D_HY:], xs, mods)


GLA_QK = GLA_H * GLA_DK
GLA_V = GLA_H * GLA_DV


def _gla_dir(qkv_ref, lr_ref, gw_ref, gb_ref, st_ref, o_ref, *, s, d, reverse):
    q = GLA_CHUNK
    tri = _tri(q, 'upper' if reverse else 'lower')
    edge = 0 if reverse else q - 1
    lr = lr_ref[s, :, d * GLA_RANK:(d + 1) * GLA_RANK]
    logit = jnp.dot(lr, gw_ref[d], preferred_element_type=F32, precision=HIGHEST) + gb_ref[d:d + 1, :]
    log_g = -_softplus(-logit) * (1.0 / GLA_GATE_NORM)
    gcum = jnp.dot(tri.astype(F32), log_g, preferred_element_type=F32, precision=HIGHEST)
    total = gcum[edge:edge + 1, :]
    qf = qkv_ref[s, :, 0:GLA_QK].astype(F32)
    kf = qkv_ref[s, :, GLA_QK:2 * GLA_QK].astype(F32)
    v = qkv_ref[s, :, 2 * GLA_QK:2 * GLA_QK + GLA_V]
    qg = (qf * (GLA_DK ** -0.5) * jnp.exp(gcum)).astype(BF16)
    kg = (kf * jnp.exp(-gcum)).astype(BF16)
    kw = (kf * jnp.exp(total - gcum)).astype(BF16)
    rb = lax.broadcasted_iota(jnp.int32, (GLA_H * q, GLA_QK), 0) // q
    cb = lax.broadcasted_iota(jnp.int32, (GLA_H * q, GLA_QK), 1) // GLA_DK
    k_bd = jnp.where(rb == cb, jnp.concatenate([kg] * GLA_H, axis=0), jnp.zeros((), BF16))
    att = lax.dot_general(qg, k_bd, (((1,), (1,)), ((), ())), preferred_element_type=F32)
    i_i = lax.broadcasted_iota(jnp.int32, (q, GLA_H * q), 0)
    j_i = lax.broadcasted_iota(jnp.int32, (q, GLA_H * q), 1) % q
    keep = (j_i >= i_i) if reverse else (j_i <= i_i)
    att = jnp.where(keep, att, 0.0).astype(BF16)
    rv = lax.broadcasted_iota(jnp.int32, (GLA_H * q, GLA_V), 0) // q
    cv = lax.broadcasted_iota(jnp.int32, (GLA_H * q, GLA_V), 1) // GLA_DV
    v_bd = jnp.where(rv == cv, jnp.concatenate([v] * GLA_H, axis=0), jnp.zeros((), BF16))
    st = st_ref[s]
    o_in = jnp.dot(att, v_bd, preferred_element_type=F32)
    o_st = lax.dot_general(qg, st.astype(BF16), (((1,), (1,)), ((), ())), preferred_element_type=F32)
    o_ref[s] = (o_in + o_st).astype(o_ref.dtype)
    upd = lax.dot_general(v, kw, (((0,), (0,)), ((), ())), preferred_element_type=F32)
    rs = lax.broadcasted_iota(jnp.int32, (GLA_V, GLA_QK), 0) // GLA_DV
    cs = lax.broadcasted_iota(jnp.int32, (GLA_V, GLA_QK), 1) // GLA_DK
    st_ref[s] = st * jnp.exp(total) + jnp.where(rs == cs, upd, 0.0)


def _gla_body(qf_ref, lf_ref, qb_ref, lb_ref, gw_ref, gb_ref, of_ref, ob_ref, stf, stb):
    @pl.when(pl.program_id(1) == 0)
    def _():
        stf[...] = jnp.zeros_like(stf)
        stb[...] = jnp.zeros_like(stb)

    for s in range(SCAN_SAMPLES):
        _gla_dir(qf_ref, lf_ref, gw_ref, gb_ref, stf, of_ref, s=s, d=0, reverse=False)
        _gla_dir(qb_ref, lb_ref, gw_ref, gb_ref, stb, ob_ref, s=s, d=1, reverse=True)


def gla_scan(qkv, lr, gate_w, gate_b):
    bsz = qkv.shape[0]
    nc = NTOK // GLA_CHUNK
    nlat = SEQ // GLA_CHUNK
    fwd_chunk = lambda s: (s + nlat) % nc
    bwd_chunk = lambda s: nc - 1 - s
    ns = SCAN_SAMPLES
    q_spec = lambda cm: pl.BlockSpec((ns, GLA_CHUNK, qkv.shape[-1]), lambda b, s: (b, cm(s), 0))
    l_spec = lambda cm: pl.BlockSpec((ns, GLA_CHUNK, 2 * GLA_RANK), lambda b, s: (b, cm(s), 0))
    o_spec = lambda cm: pl.BlockSpec((ns, GLA_CHUNK, GLA_V), lambda b, s: (b, cm(s), 0))
    o_shape = jax.ShapeDtypeStruct((bsz, NTOK, GLA_V), BF16)
    return pl.pallas_call(
        _gla_body,
        name='gla_scan',
        grid=(bsz // ns, nc),
        in_specs=[q_spec(fwd_chunk), l_spec(fwd_chunk), q_spec(bwd_chunk), l_spec(bwd_chunk),
                  pl.BlockSpec((2, GLA_RANK, GLA_QK), lambda b, s: (0, 0, 0)),
                  pl.BlockSpec((2, GLA_QK), lambda b, s: (0, 0))],
        out_specs=[o_spec(fwd_chunk), o_spec(bwd_chunk)],
        out_shape=[o_shape, o_shape],
        scratch_shapes=[pltpu.VMEM((ns, GLA_V, GLA_QK), F32), pltpu.VMEM((ns, GLA_V, GLA_QK), F32)],
        compiler_params=_cparams(("parallel", "arbitrary")),
    )(qkv, lr, qkv, lr, gate_w.astype(F32), gate_b.astype(F32))


def _gla_merge_body(of_ref, ob_ref, r_ref, nw_ref, o_ref):
    o = of_ref[0].astype(F32) + ob_ref[0].astype(F32)
    r = r_ref[0].astype(F32)
    for h in range(GLA_H):
        sl = slice(h * GLA_DV, (h + 1) * GLA_DV)
        oh = o[:, sl]
        ms = jnp.mean(oh * oh, axis=-1, keepdims=True)
        o_ref[0, :, sl] = (oh * lax.rsqrt(ms + NORM_EPS) * nw_ref[:, sl] * _silu(r[:, sl])).astype(o_ref.dtype)


def gla_merge_stream(of, ob, r, norm_w):
    bsz = of.shape[0]
    tile = pl.BlockSpec((1, ROW_TILE, GLA_V), lambda i, j: (i, j, 0))
    return pl.pallas_call(
        _gla_merge_body,
        name='gla_merge',
        grid=(bsz, N_LAT_TILES),
        in_specs=[tile, tile, tile, pl.BlockSpec((1, GLA_V), lambda i, j: (0, 0))],
        out_specs=tile,
        out_shape=jax.ShapeDtypeStruct((bsz, SEQ, GLA_V), BF16),
        compiler_params=_cparams(("parallel", "parallel")),
    )(of, ob, r, norm_w.reshape(1, GLA_V))


RG_TILE = 8


def _gelu_tanh(x):
    return 0.5 * x * (1.0 + jnp.tanh(math.sqrt(2.0 / math.pi) * (x + 0.044715 * x * x * x)))


def _rg_scan_block(a_s, x_s, h_s, base, carry, reverse):
    n_tiles = ROW_TILE // RG_TILE
    row = lax.broadcasted_iota(jnp.int32, (RG_TILE, D_RG), 0)

    def tile_step(i, h_prev):
        t = (n_tiles - 1 - i) if reverse else i
        r0 = pl.multiple_of(t * RG_TILE, RG_TILE)
        a = a_s[pl.ds(r0, RG_TILE), :]
        x = x_s[pl.ds(r0, RG_TILE), :]
        for s in (1, 2, 4):
            if reverse:
                ok = row < RG_TILE - s
                shift = RG_TILE - s
            else:
                ok = row >= s
                shift = s
            a_sh = jnp.where(ok, pltpu.roll(a, shift, 0), 1.0)
            x_sh = jnp.where(ok, pltpu.roll(x, shift, 0), 0.0)
            x = a * x_sh + x
            a = a * a_sh
        h = x + a * h_prev
        h_s[pl.ds(base + r0, RG_TILE), :] = h
        edge = 0 if reverse else RG_TILE - 1
        return jnp.broadcast_to(h[edge:edge + 1, :], (RG_TILE, D_RG))

    return lax.fori_loop(0, n_tiles, tile_step, carry)


def _rglru_body(u_ref, g_ref, w_ref, b_ref, c_ref, o_ref, hf_s, a_s, x_s, hb_s):
    n_blocks = NTOK // ROW_TILE
    fwd_order = list(range(N_LAT_TILES, n_blocks)) + list(range(N_LAT_TILES))
    bwd_order = list(range(n_blocks - 1, N_LAT_TILES - 1, -1)) + list(range(N_LAT_TILES - 1, -1, -1))

    def gates(blk, d):
        ub = u_ref[0, blk * ROW_TILE:(blk + 1) * ROW_TILE, :]
        z = jnp.dot(ub, w_ref[:, 2 * d * D_RG:2 * (d + 1) * D_RG], preferred_element_type=F32)
        z = z + b_ref[:, 2 * d * D_RG:2 * (d + 1) * D_RG]
        r = 1.0 / (1.0 + jnp.exp(-z[:, :D_RG]))
        i = 1.0 / (1.0 + jnp.exp(-z[:, D_RG:]))
        a = jnp.exp(c_ref[d:d + 1, :] * r)
        a_s[...] = a
        x_s[...] = jnp.sqrt(1.0 - a * a) * i * ub.astype(F32)

    carry = jnp.zeros((RG_TILE, D_RG), F32)
    for blk in fwd_order:
        gates(blk, 0)
        carry = _rg_scan_block(a_s, x_s, hf_s, blk * ROW_TILE, carry, reverse=False)
    carry = jnp.zeros((RG_TILE, D_RG), F32)
    for blk in bwd_order:
        gates(blk, 1)
        carry = _rg_scan_block(a_s, x_s, hb_s, 0, carry, reverse=True)
        rows = slice(blk * ROW_TILE, (blk + 1) * ROW_TILE)
        gate = g_ref[0, rows, :].astype(F32)
        o_ref[0, rows, :] = ((hf_s[rows, :] + hb_s[...]) * _gelu_tanh(gate)).astype(o_ref.dtype)


def rglru_stream(u, gate, w_a, b_a, w_x, b_x, lam):
    bsz = u.shape[0]
    eye = jnp.eye(RG_BLOCKS, dtype=F32)
    dense = lambda w: jnp.einsum('nio,nm->nimo', w, eye).reshape(D_RG, D_RG)
    w_cat = jnp.concatenate([dense(w_a[0]), dense(w_x[0]), dense(w_a[1]), dense(w_x[1])], axis=1).astype(BF16)
    b_cat = jnp.concatenate([b_a[0], b_x[0], b_a[1], b_x[1]]).astype(F32).reshape(1, 4 * D_RG)
    c = -RG_C * jax.nn.softplus(-lam.astype(F32))
    seq = pl.BlockSpec((1, NTOK, D_RG), lambda i: (i, 0, 0))
    return pl.pallas_call(
        _rglru_body,
        name='rglru',
        grid=(bsz,),
        in_specs=[seq, seq,
                  pl.BlockSpec((D_RG, 4 * D_RG), lambda i: (0, 0)),
                  pl.BlockSpec((1, 4 * D_RG), lambda i: (0, 0)),
                  pl.BlockSpec((2, D_RG), lambda i: (0, 0))],
        out_specs=seq,
        out_shape=jax.ShapeDtypeStruct((bsz, NTOK, D_RG), BF16),
        scratch_shapes=[pltpu.VMEM((NTOK, D_RG), F32), pltpu.VMEM((ROW_TILE, D_RG), F32),
                        pltpu.VMEM((ROW_TILE, D_RG), F32), pltpu.VMEM((ROW_TILE, D_RG), F32)],
        compiler_params=_cparams(("parallel",)),
    )(u, gate, w_cat, b_cat, c)


def odd_layer_mixer_pallas(xs, mods, norm_w, w_in, w_out, gla_args, rg_args):
    bsz = xs.shape[0]
    gate_w, gate_b, gla_norm_w = gla_args
    rg_conv_w, rg_conv_b, w_a, b_a, w_x, b_x, lam = rg_args
    h = norm_mod(xs, norm_w, mods, 0, 1)
    h = jnp.concatenate([to_col_major(h[:, :SEQ]), h[:, SEQ:]], axis=1)
    nqk, nv = GLA_QK, GLA_V
    r0 = 2 * nqk + nv + 2 * GLA_RANK
    w = jnp.concatenate([w_in[:, :2 * nqk + nv], w_in[:, r0:r0 + nv], w_in[:, GLA_IN:],
                         w_in[:, 2 * nqk + nv:r0]], axis=1).astype(BF16)
    qkv_w = 2 * nqk + nv
    splits = ((0, qkv_w), (qkv_w, nv), (qkv_w + nv, D_RG), (qkv_w + nv + D_RG, D_RG),
              (qkv_w + nv + 2 * D_RG, 2 * GLA_RANK))
    qkv, r, u_raw, gate, lr = mm_split(h.reshape(bsz * NTOK, D_MODEL), w, splits, (BF16, BF16, BF16, BF16, F32))
    to3 = lambda a: a.reshape(bsz, NTOK, a.shape[-1])
    of, ob = gla_scan(to3(qkv), to3(lr), gate_w, gate_b.reshape(2, GLA_QK))
    a_l = gla_merge_stream(of, ob, to3(r), gla_norm_w)
    u = dwconv_stream(to3(u_raw), rg_conv_w, rg_conv_b, act=False)
    r_l = rglru_stream(u, to3(gate), w_a, b_a, w_x, b_x, lam)[:, :SEQ]
    wo = w_out.astype(BF16)
    return mm_resid([from_col_major(a_l), from_col_major(r_l)], [wo[:GLA_V], wo[GLA_V:]], xs, mods, 2, N_LAT_TILES)


def to_col_major(x):
    b, n, d = x.shape
    rows = n // GRID_W
    return x.reshape(b, rows, GRID_W, d).transpose(0, 2, 1, 3).reshape(b, n, d)


def from_col_major(x):
    b, n, d = x.shape
    rows = n // GRID_W
    return x.reshape(b, GRID_W, rows, d).transpose(0, 2, 1, 3).reshape(b, n, d)


def kernel(x, c, ctx, c_ctx, ada_w, ada_b, norm1_w, norm2_w, ev_w_in, ev_w_out, hy_conv_w, hy_conv_b, hy_fw0, hy_fb0, hy_fw1, hy_fb1, hy_fw2, hy_fb2, hy_fw3, hy_freq, hy_bias, ssd_conv_w, ssd_conv_b, ssd_dt_bias, ssd_a_log, ssd_d, ssd_norm_w, od_w_in, od_w_out, gla_gate_w, gla_gate_b, gla_norm_w, rg_conv_w, rg_conv_b, rg_w_a, rg_b_a, rg_w_x, rg_b_x, rg_lambda, router_w, router_b, moe_w_gate, moe_w_up, moe_w_down, sh_w_gate, sh_w_up, sh_w_down, final_norm_w):
    xs = jnp.concatenate([x, ctx], axis=1)
    for i in range(DEPTH):
        last = i == DEPTH - 1
        j = i // 2
        mods = adaln_table(c, c_ctx, ada_w[i], ada_b[i])
        if i % 2 == 0:
            hy_filt = (hy_fw0[j], hy_fb0[j], hy_fw1[j], hy_fb1[j], hy_fw2[j], hy_fb2[j], hy_fw3[j], hy_freq[j])
            xs = even_layer_mixer(xs, mods, norm1_w[i], ev_w_in[j], ev_w_out[j], hy_conv_w[j], hy_conv_b[j],
                                  hy_filt, hy_bias[j], ssd_conv_w[j], ssd_conv_b[j], ssd_dt_bias[j],
                                  ssd_a_log[j], ssd_d[j], ssd_norm_w[j])
        else:
            gla_args = (gla_gate_w[j], gla_gate_b[j], gla_norm_w[j])
            rg_args = (rg_conv_w[j], rg_conv_b[j], rg_w_a[j], rg_b_a[j], rg_w_x[j], rg_b_x[j], rg_lambda[j])
            xs = odd_layer_mixer_pallas(xs, mods, norm1_w[i], od_w_in[j], od_w_out[j], gla_args, rg_args)
        n_tiles = N_LAT_TILES if last else N_ROW_TILES
        xs = moe_layer(xs, norm2_w[i], mods, router_w[i], router_b[i], moe_w_gate, moe_w_up, moe_w_down,
                       sh_w_gate[i], sh_w_up[i], sh_w_down[i], n_tiles, i)
    return final_norm(xs, final_norm_w)
```
